```python
import jax, jax.numpy as jnp
from jax import lax
import numpy as np

D_MODEL = 2048
BATCH = 8
SEQ = 8192
DEPTH = 1

MLA_HEADS = 8
QK_NOPE_DIM = 128
QK_ROPE_DIM = 64
V_HEAD_DIM = 128
Q_LORA_RANK = 512
KV_LORA_RANK = 512
MLA_WIDTH = MLA_HEADS * V_HEAD_DIM
QK_HEAD_DIM = QK_NOPE_DIM + QK_ROPE_DIM
CONV_CHANNELS = D_MODEL - MLA_WIDTH
CONV_WIDTH = 31
CONV_PAD = CONV_WIDTH // 2
D_FF = 4 * D_MODEL
ROPE_BASE = 10000.0
Q_BLOCK = 128
LN_EPS = 1e-5
RMS_EPS = 1e-6
DEEPNORM_ALPHA = (2.0 * DEPTH) ** 0.25
DEEPNORM_BETA = (8.0 * DEPTH) ** -0.25
IN_COLS = Q_LORA_RANK + KV_LORA_RANK + QK_ROPE_DIM + 2 * CONV_CHANNELS

kernel_name = "hybrid_mla_conformer_deepnorm_encoder"


def layer_norm(x, g, b):
    xf = x.astype(jnp.float32)
    mu = jnp.mean(xf, axis=-1, keepdims=True)
    xc = xf - mu
    var = jnp.mean(jnp.square(xc), axis=-1, keepdims=True)
    y = xc * lax.rsqrt(var + LN_EPS)
    return (y * g.astype(jnp.float32) + b.astype(jnp.float32)).astype(x.dtype)


def rms_norm(x, g):
    xf = x.astype(jnp.float32)
    y = xf * lax.rsqrt(jnp.mean(jnp.square(xf), axis=-1, keepdims=True) + RMS_EPS)
    return (y * g.astype(jnp.float32)).astype(x.dtype)


def rope_tables(positions, dtype):
    half = QK_ROPE_DIM // 2
    inv_freq = ROPE_BASE ** (-jnp.arange(half, dtype=jnp.float32) * (2.0 / QK_ROPE_DIM))
    ang = positions.astype(jnp.float32)[..., None] * inv_freq
    return jnp.cos(ang).astype(dtype), jnp.sin(ang).astype(dtype)


def apply_rope(x, cos, sin):
    x1, x2 = jnp.split(x, 2, axis=-1)
    return jnp.concatenate([x1 * cos - x2 * sin, x2 * cos + x1 * sin], axis=-1)


def mla_attention(q_nope, q_rope, k_nope, k_rope, v):
    b, s, h, _ = q_nope.shape
    nb = s // Q_BLOCK
    scale = QK_HEAD_DIM ** -0.5
    qn = q_nope.reshape(b, nb, Q_BLOCK, h, QK_NOPE_DIM).transpose(1, 0, 2, 3, 4)
    qr = q_rope.reshape(b, nb, Q_BLOCK, h, QK_ROPE_DIM).transpose(1, 0, 2, 3, 4)

    def block(args):
        qn_b, qr_b = args
        scores = (jnp.einsum('bqhd,bkhd->bhqk', qn_b, k_nope)
                  + jnp.einsum('bqhr,bkr->bhqk', qr_b, k_rope))
        p = jax.nn.softmax(scores.astype(jnp.float32) * scale, axis=-1).astype(v.dtype)
        return jnp.einsum('bhqk,bkhd->bqhd', p, v)

    out = lax.map(block, (qn, qr))
    return out.transpose(1, 0, 2, 3, 4).reshape(b, s, h * V_HEAD_DIM)


def conformer_conv(u_in, conv_w, conv_b, g_ln, b_ln):
    a, gate = jnp.split(u_in, 2, axis=-1)
    u = a * jax.nn.sigmoid(gate)
    kern = conv_w.reshape(CONV_WIDTH, 1, CONV_CHANNELS).astype(u.dtype)
    u = lax.conv_general_dilated(
        u, kern, window_strides=(1,), padding=[(CONV_PAD, CONV_PAD)],
        dimension_numbers=('NWC', 'WIO', 'NWC'),
        feature_group_count=CONV_CHANNELS) + conv_b
    return jax.nn.silu(layer_norm(u, g_ln, b_ln))


def hybrid_layer(x, cos, sin, w_in, g_cq, w_uq, g_ckv, w_uk, w_uv, conv_w, conv_b,
                 g_conv_ln, b_conv_ln, w_out, g_ln1, b_ln1, w_ff1, w_ff2, g_ln2, b_ln2):
    b, s, _ = x.shape
    h = x @ w_in
    c_q, c_kv, k_rope, conv_in = jnp.split(
        h, [Q_LORA_RANK, Q_LORA_RANK + KV_LORA_RANK,
            Q_LORA_RANK + KV_LORA_RANK + QK_ROPE_DIM], axis=-1)
    q = (rms_norm(c_q, g_cq) @ w_uq).reshape(b, s, MLA_HEADS, QK_HEAD_DIM)
    q_nope, q_rope = jnp.split(q, [QK_NOPE_DIM], axis=-1)
    q_rope = apply_rope(q_rope, cos[:, :, None, :], sin[:, :, None, :])
    k_rope = apply_rope(k_rope, cos, sin)
    ckv = rms_norm(c_kv, g_ckv)
    k_nope = (ckv @ w_uk).reshape(b, s, MLA_HEADS, QK_NOPE_DIM)
    v = (ckv @ w_uv).reshape(b, s, MLA_HEADS, V_HEAD_DIM)
    attn_out = mla_attention(q_nope, q_rope, k_nope, k_rope, v)
    conv_out = conformer_conv(conv_in, conv_w, conv_b, g_conv_ln, b_conv_ln)
    mix = jnp.concatenate([attn_out, conv_out], axis=-1) @ w_out
    x = layer_norm(DEEPNORM_ALPHA * x + mix, g_ln1, b_ln1)
    ff = jnp.square(jax.nn.relu(x @ w_ff1)) @ w_ff2
    return layer_norm(DEEPNORM_ALPHA * x + ff, g_ln2, b_ln2)


def _fwd_setup_inputs(seed: int = 0) -> dict:
    key = jax.random.key(seed)
    ks = jax.random.split(key, 24)
    f32 = jnp.float32

    def nrm(k, shape, scale):
        return jax.random.normal(k, shape, f32) * scale

    def gain(k, shape):
        return 1.0 + 0.02 * jax.random.normal(k, shape, f32)

    L = DEPTH
    beta = DEEPNORM_BETA
    return {
        "x": jax.random.normal(ks[0], (BATCH, SEQ, D_MODEL), f32),
        "positions": jnp.broadcast_to(jnp.arange(SEQ, dtype=jnp.int32), (BATCH, SEQ)),
        "ln_in_g": gain(ks[1], (D_MODEL,)),
        "ln_in_b": nrm(ks[2], (D_MODEL,), 0.02),
        "w_in": nrm(ks[3], (L, D_MODEL, IN_COLS), D_MODEL ** -0.5),
        "g_cq": gain(ks[4], (L, Q_LORA_RANK)),
        "w_uq": nrm(ks[5], (L, Q_LORA_RANK, MLA_HEADS * QK_HEAD_DIM), Q_LORA_RANK ** -0.5),
        "g_ckv": gain(ks[6], (L, KV_LORA_RANK)),
        "w_uk": nrm(ks[7], (L, KV_LORA_RANK, MLA_HEADS * QK_NOPE_DIM), KV_LORA_RANK ** -0.5),
        "w_uv": nrm(ks[8], (L, KV_LORA_RANK, MLA_HEADS * V_HEAD_DIM), beta * KV_LORA_RANK ** -0.5),
        "conv_w": nrm(ks[9], (L, CONV_WIDTH, CONV_CHANNELS), CONV_WIDTH ** -0.5),
        "conv_b": nrm(ks[10], (L, CONV_CHANNELS), 0.02),
        "g_conv_ln": gain(ks[11], (L, CONV_CHANNELS)),
        "b_conv_ln": nrm(ks[12], (L, CONV_CHANNELS), 0.02),
        "w_out": nrm(ks[13], (L, D_MODEL, D_MODEL), beta * D_MODEL ** -0.5),
        "g_ln1": gain(ks[14], (L, D_MODEL)),
        "b_ln1": nrm(ks[15], (L, D_MODEL), 0.02),
        "w_ff1": nrm(ks[16], (L, D_MODEL, D_FF), beta * D_MODEL ** -0.5),
        "w_ff2": nrm(ks[17], (L, D_FF, D_MODEL), beta * D_FF ** -0.5),
        "g_ln2": gain(ks[18], (L, D_MODEL)),
        "b_ln2": nrm(ks[19], (L, D_MODEL), 0.02),
    }


def _fwd_reference(x, positions, ln_in_g, ln_in_b, w_in, g_cq, w_uq, g_ckv, w_uk, w_uv,
              conv_w, conv_b, g_conv_ln, b_conv_ln, w_out, g_ln1, b_ln1,
              w_ff1, w_ff2, g_ln2, b_ln2):
    cos, sin = rope_tables(positions, x.dtype)
    x = layer_norm(x, ln_in_g, ln_in_b)
    for l in range(DEPTH):
        x = hybrid_layer(x, cos, sin, w_in[l], g_cq[l], w_uq[l], g_ckv[l], w_uk[l], w_uv[l],
                         conv_w[l], conv_b[l], g_conv_ln[l], b_conv_ln[l], w_out[l],
                         g_ln1[l], b_ln1[l], w_ff1[l], w_ff2[l], g_ln2[l], b_ln2[l])
    return x


import jax as _jax
import jax.numpy as _jnp

TWIN_FORMAT = 'train_step'
FWD_PARAMS = ['x', 'positions', 'ln_in_g', 'ln_in_b', 'w_in', 'g_cq', 'w_uq', 'g_ckv', 'w_uk', 'w_uv', 'conv_w', 'conv_b', 'g_conv_ln', 'b_conv_ln', 'w_out', 'g_ln1', 'b_ln1', 'w_ff1', 'w_ff2', 'g_ln2', 'b_ln2']
TWIN_WEIGHTS = ['ln_in_g', 'ln_in_b', 'w_in', 'g_cq', 'w_uq', 'g_ckv', 'w_uk', 'w_uv', 'conv_w', 'conv_b', 'g_conv_ln', 'b_conv_ln', 'w_out', 'g_ln1', 'b_ln1', 'w_ff1', 'w_ff2', 'g_ln2', 'b_ln2']
TWIN_DIFF_INPUT = 'x'
TWIN_INPUTS = ['x', 'positions', 'ln_in_g', 'ln_in_b', 'w_in', 'g_cq', 'w_uq', 'g_ckv', 'w_uk', 'w_uv', 'conv_w', 'conv_b', 'g_conv_ln', 'b_conv_ln', 'w_out', 'g_ln1', 'b_ln1', 'w_ff1', 'w_ff2', 'g_ln2', 'b_ln2', 'loss_target', 'm_ln_in_g', 'm_ln_in_b', 'm_w_in', 'm_g_cq', 'm_w_uq', 'm_g_ckv', 'm_w_uk', 'm_w_uv', 'm_conv_w', 'm_conv_b', 'm_g_conv_ln', 'm_b_conv_ln', 'm_w_out', 'm_g_ln1', 'm_b_ln1', 'm_w_ff1', 'm_w_ff2', 'm_g_ln2', 'm_b_ln2', 'v_ln_in_g', 'v_ln_in_b', 'v_w_in', 'v_g_cq', 'v_w_uq', 'v_g_ckv', 'v_w_uk', 'v_w_uv', 'v_conv_w', 'v_conv_b', 'v_g_conv_ln', 'v_b_conv_ln', 'v_w_out', 'v_g_ln1', 'v_b_ln1', 'v_w_ff1', 'v_w_ff2', 'v_g_ln2', 'v_b_ln2']
TWIN_OUTPUTS = ['loss', 'grad_x', 'grad_ln_in_g', 'grad_ln_in_b', 'grad_w_in', 'grad_g_cq', 'grad_w_uq', 'grad_g_ckv', 'grad_w_uk', 'grad_w_uv', 'grad_conv_w', 'grad_conv_b', 'grad_g_conv_ln', 'grad_b_conv_ln', 'grad_w_out', 'grad_g_ln1', 'grad_b_ln1', 'grad_w_ff1', 'grad_w_ff2', 'grad_g_ln2', 'grad_b_ln2', 'delta_ln_in_g', 'delta_ln_in_b', 'delta_w_in', 'delta_g_cq', 'delta_w_uq', 'delta_g_ckv', 'delta_w_uk', 'delta_w_uv', 'delta_conv_w', 'delta_conv_b', 'delta_g_conv_ln', 'delta_b_conv_ln', 'delta_w_out', 'delta_g_ln1', 'delta_b_ln1', 'delta_w_ff1', 'delta_w_ff2', 'delta_g_ln2', 'delta_b_ln2', 'new_m_ln_in_g', 'new_m_ln_in_b', 'new_m_w_in', 'new_m_g_cq', 'new_m_w_uq', 'new_m_g_ckv', 'new_m_w_uk', 'new_m_w_uv', 'new_m_conv_w', 'new_m_conv_b', 'new_m_g_conv_ln', 'new_m_b_conv_ln', 'new_m_w_out', 'new_m_g_ln1', 'new_m_b_ln1', 'new_m_w_ff1', 'new_m_w_ff2', 'new_m_g_ln2', 'new_m_b_ln2', 'new_v_ln_in_g', 'new_v_ln_in_b', 'new_v_w_in', 'new_v_g_cq', 'new_v_w_uq', 'new_v_g_ckv', 'new_v_w_uk', 'new_v_w_uv', 'new_v_conv_w', 'new_v_conv_b', 'new_v_g_conv_ln', 'new_v_b_conv_ln', 'new_v_w_out', 'new_v_g_ln1', 'new_v_b_ln1', 'new_v_w_ff1', 'new_v_w_ff2', 'new_v_g_ln2', 'new_v_b_ln2']
TWIN_LEAF_KINDS = {'loss': 'loss', 'grad_x': 'grad_x', 'grad_ln_in_g': 'grad_w', 'grad_ln_in_b': 'grad_w', 'grad_w_in': 'grad_w', 'grad_g_cq': 'grad_w', 'grad_w_uq': 'grad_w', 'grad_g_ckv': 'grad_w', 'grad_w_uk': 'grad_w', 'grad_w_uv': 'grad_w', 'grad_conv_w': 'grad_w', 'grad_conv_b': 'grad_w', 'grad_g_conv_ln': 'grad_w', 'grad_b_conv_ln': 'grad_w', 'grad_w_out': 'grad_w', 'grad_g_ln1': 'grad_w', 'grad_b_ln1': 'grad_w', 'grad_w_ff1': 'grad_w', 'grad_w_ff2': 'grad_w', 'grad_g_ln2': 'grad_w', 'grad_b_ln2': 'grad_w', 'delta_ln_in_g': 'delta_w', 'delta_ln_in_b': 'delta_w', 'delta_w_in': 'delta_w', 'delta_g_cq': 'delta_w', 'delta_w_uq': 'delta_w', 'delta_g_ckv': 'delta_w', 'delta_w_uk': 'delta_w', 'delta_w_uv': 'delta_w', 'delta_conv_w': 'delta_w', 'delta_conv_b': 'delta_w', 'delta_g_conv_ln': 'delta_w', 'delta_b_conv_ln': 'delta_w', 'delta_w_out': 'delta_w', 'delta_g_ln1': 'delta_w', 'delta_b_ln1': 'delta_w', 'delta_w_ff1': 'delta_w', 'delta_w_ff2': 'delta_w', 'delta_g_ln2': 'delta_w', 'delta_b_ln2': 'delta_w', 'new_m_ln_in_g': 'new_m', 'new_m_ln_in_b': 'new_m', 'new_m_w_in': 'new_m', 'new_m_g_cq': 'new_m', 'new_m_w_uq': 'new_m', 'new_m_g_ckv': 'new_m', 'new_m_w_uk': 'new_m', 'new_m_w_uv': 'new_m', 'new_m_conv_w': 'new_m', 'new_m_conv_b': 'new_m', 'new_m_g_conv_ln': 'new_m', 'new_m_b_conv_ln': 'new_m', 'new_m_w_out': 'new_m', 'new_m_g_ln1': 'new_m', 'new_m_b_ln1': 'new_m', 'new_m_w_ff1': 'new_m', 'new_m_w_ff2': 'new_m', 'new_m_g_ln2': 'new_m', 'new_m_b_ln2': 'new_m', 'new_v_ln_in_g': 'new_v', 'new_v_ln_in_b': 'new_v', 'new_v_w_in': 'new_v', 'new_v_g_cq': 'new_v', 'new_v_w_uq': 'new_v', 'new_v_g_ckv': 'new_v', 'new_v_w_uk': 'new_v', 'new_v_w_uv': 'new_v', 'new_v_conv_w': 'new_v', 'new_v_conv_b': 'new_v', 'new_v_g_conv_ln': 'new_v', 'new_v_b_conv_ln': 'new_v', 'new_v_w_out': 'new_v', 'new_v_g_ln1': 'new_v', 'new_v_b_ln1': 'new_v', 'new_v_w_ff1': 'new_v', 'new_v_w_ff2': 'new_v', 'new_v_g_ln2': 'new_v', 'new_v_b_ln2': 'new_v'}


def _forward(args):
    return _fwd_reference(*[args[k] for k in FWD_PARAMS])


def _output_shape():
    def fwd():
        inp = _fwd_setup_inputs(0)
        return _fwd_reference(*[inp[k] for k in FWD_PARAMS])
    out = _jax.eval_shape(fwd)
    return out.shape, out.dtype

N_MICROBATCH = 1
ADAM_LR = 0.001
ADAM_B1 = 0.9
ADAM_B2 = 0.999
ADAM_EPS = 1e-08
ADAM_WD = 0.01
ADAM_STEP = 10
PER_EXAMPLE_BATCH_AXIS = {'x': 0, 'positions': 0, 'loss_target': 0}
SHARED_INPUTS = []
_WEIGHT_DTYPES = {'ln_in_g': _jnp.float32, 'ln_in_b': _jnp.float32, 'w_in': _jnp.float32, 'g_cq': _jnp.float32, 'w_uq': _jnp.float32, 'g_ckv': _jnp.float32, 'w_uk': _jnp.float32, 'w_uv': _jnp.float32, 'conv_w': _jnp.float32, 'conv_b': _jnp.float32, 'g_conv_ln': _jnp.float32, 'b_conv_ln': _jnp.float32, 'w_out': _jnp.float32, 'g_ln1': _jnp.float32, 'b_ln1': _jnp.float32, 'w_ff1': _jnp.float32, 'w_ff2': _jnp.float32, 'g_ln2': _jnp.float32, 'b_ln2': _jnp.float32}
MOMENT_SCALE = {'ln_in_g': 1.002565e+00, 'ln_in_b': 4.987917e-01, 'w_in': 2.376172e-02, 'g_cq': 5.860683e-03, 'w_uq': 3.192426e-03, 'g_ckv': 8.273679e-03, 'w_uk': 3.292805e-03, 'w_uv': 7.077197e-03, 'conv_w': 3.949840e-02, 'conv_b': 2.161620e-01, 'g_conv_ln': 8.268991e-02, 'b_conv_ln': 1.198767e-01, 'w_out': 6.230332e-02, 'g_ln1': 1.052485e+00, 'b_ln1': 4.981128e-01, 'w_ff1': 2.599356e-02, 'w_ff2': 7.087472e-02, 'g_ln2': 3.201522e+01, 'b_ln2': 3.683905e+00}


def _to_microbatches(a, axis):
    t = _jnp.moveaxis(a, axis, 0)
    t = t.reshape((N_MICROBATCH, t.shape[0] // N_MICROBATCH) + t.shape[1:])
    return _jnp.moveaxis(t, 1, axis + 1)


def setup_inputs(seed: int = 0) -> dict:
    inp = _fwd_setup_inputs(seed)
    key = _jax.random.fold_in(_jax.random.key(seed), 7919)
    shape, _ = _output_shape()
    out = dict(inp)
    out["loss_target"] = _jax.random.normal(_jax.random.fold_in(key, 0), shape, _jnp.float32)
    for i, name in enumerate(TWIN_WEIGHTS):
        w = inp[name].astype(_jnp.float32)
        if MOMENT_SCALE is None:
            s = _jnp.sqrt(_jnp.mean(_jnp.square(w)) + 1e-30)
        else:
            s = MOMENT_SCALE[name]
        km, kv = _jax.random.split(_jax.random.fold_in(key, i + 1))
        out[name] = w
        out["m_" + name] = s * _jax.random.normal(km, w.shape, _jnp.float32)
        out["v_" + name] = (s * s) * _jax.random.uniform(kv, w.shape, _jnp.float32, 0.5, 1.5)
    if N_MICROBATCH > 1:
        for name, axis in PER_EXAMPLE_BATCH_AXIS.items():
            out[name] = _to_microbatches(out[name], axis)
    return {'x': out['x'], 'positions': out['positions'], 'ln_in_g': out['ln_in_g'], 'ln_in_b': out['ln_in_b'], 'w_in': out['w_in'], 'g_cq': out['g_cq'], 'w_uq': out['w_uq'], 'g_ckv': out['g_ckv'], 'w_uk': out['w_uk'], 'w_uv': out['w_uv'], 'conv_w': out['conv_w'], 'conv_b': out['conv_b'], 'g_conv_ln': out['g_conv_ln'], 'b_conv_ln': out['b_conv_ln'], 'w_out': out['w_out'], 'g_ln1': out['g_ln1'], 'b_ln1': out['b_ln1'], 'w_ff1': out['w_ff1'], 'w_ff2': out['w_ff2'], 'g_ln2': out['g_ln2'], 'b_ln2': out['b_ln2'], 'loss_target': out['loss_target'], 'm_ln_in_g': out['m_ln_in_g'], 'm_ln_in_b': out['m_ln_in_b'], 'm_w_in': out['m_w_in'], 'm_g_cq': out['m_g_cq'], 'm_w_uq': out['m_w_uq'], 'm_g_ckv': out['m_g_ckv'], 'm_w_uk': out['m_w_uk'], 'm_w_uv': out['m_w_uv'], 'm_conv_w': out['m_conv_w'], 'm_conv_b': out['m_conv_b'], 'm_g_conv_ln': out['m_g_conv_ln'], 'm_b_conv_ln': out['m_b_conv_ln'], 'm_w_out': out['m_w_out'], 'm_g_ln1': out['m_g_ln1'], 'm_b_ln1': out['m_b_ln1'], 'm_w_ff1': out['m_w_ff1'], 'm_w_ff2': out['m_w_ff2'], 'm_g_ln2': out['m_g_ln2'], 'm_b_ln2': out['m_b_ln2'], 'v_ln_in_g': out['v_ln_in_g'], 'v_ln_in_b': out['v_ln_in_b'], 'v_w_in': out['v_w_in'], 'v_g_cq': out['v_g_cq'], 'v_w_uq': out['v_w_uq'], 'v_g_ckv': out['v_g_ckv'], 'v_w_uk': out['v_w_uk'], 'v_w_uv': out['v_w_uv'], 'v_conv_w': out['v_conv_w'], 'v_conv_b': out['v_conv_b'], 'v_g_conv_ln': out['v_g_conv_ln'], 'v_b_conv_ln': out['v_b_conv_ln'], 'v_w_out': out['v_w_out'], 'v_g_ln1': out['v_g_ln1'], 'v_b_ln1': out['v_b_ln1'], 'v_w_ff1': out['v_w_ff1'], 'v_w_ff2': out['v_w_ff2'], 'v_g_ln2': out['v_g_ln2'], 'v_b_ln2': out['v_b_ln2']}


def _loss(weights, diff, rest, loss_target):
    with _jax.named_scope("forward"):
        args = {**rest, TWIN_DIFF_INPUT: diff, **{k: w.astype(_WEIGHT_DTYPES[k]) for k, w in weights.items()}}
        y = _forward(args)
    with _jax.named_scope("loss_head"):
        err = _jnp.square(y.astype(_jnp.float32) - loss_target)
        return 0.5 * _jnp.sum(_jnp.mean(err, axis=-1)) if err.ndim else 0.5 * err


def _adamw(w, g, m, v):
    m = ADAM_B1 * m + (1.0 - ADAM_B1) * g
    v = ADAM_B2 * v + (1.0 - ADAM_B2) * _jnp.square(g)
    m_hat = m / (1.0 - ADAM_B1 ** ADAM_STEP)
    v_hat = v / (1.0 - ADAM_B2 ** ADAM_STEP)
    delta = -ADAM_LR * (m_hat / (_jnp.sqrt(v_hat) + ADAM_EPS) + ADAM_WD * w)
    return delta, m, v


def reference(x, positions, ln_in_g, ln_in_b, w_in, g_cq, w_uq, g_ckv, w_uk, w_uv, conv_w, conv_b, g_conv_ln, b_conv_ln, w_out, g_ln1, b_ln1, w_ff1, w_ff2, g_ln2, b_ln2, loss_target, m_ln_in_g, m_ln_in_b, m_w_in, m_g_cq, m_w_uq, m_g_ckv, m_w_uk, m_w_uv, m_conv_w, m_conv_b, m_g_conv_ln, m_b_conv_ln, m_w_out, m_g_ln1, m_b_ln1, m_w_ff1, m_w_ff2, m_g_ln2, m_b_ln2, v_ln_in_g, v_ln_in_b, v_w_in, v_g_cq, v_w_uq, v_g_ckv, v_w_uk, v_w_uv, v_conv_w, v_conv_b, v_g_conv_ln, v_b_conv_ln, v_w_out, v_g_ln1, v_b_ln1, v_w_ff1, v_w_ff2, v_g_ln2, v_b_ln2):
    given = dict(x=x, positions=positions, ln_in_g=ln_in_g, ln_in_b=ln_in_b, w_in=w_in, g_cq=g_cq, w_uq=w_uq, g_ckv=g_ckv, w_uk=w_uk, w_uv=w_uv, conv_w=conv_w, conv_b=conv_b, g_conv_ln=g_conv_ln, b_conv_ln=b_conv_ln, w_out=w_out, g_ln1=g_ln1, b_ln1=b_ln1, w_ff1=w_ff1, w_ff2=w_ff2, g_ln2=g_ln2, b_ln2=b_ln2, loss_target=loss_target, m_ln_in_g=m_ln_in_g, m_ln_in_b=m_ln_in_b, m_w_in=m_w_in, m_g_cq=m_g_cq, m_w_uq=m_w_uq, m_g_ckv=m_g_ckv, m_w_uk=m_w_uk, m_w_uv=m_w_uv, m_conv_w=m_conv_w, m_conv_b=m_conv_b, m_g_conv_ln=m_g_conv_ln, m_b_conv_ln=m_b_conv_ln, m_w_out=m_w_out, m_g_ln1=m_g_ln1, m_b_ln1=m_b_ln1, m_w_ff1=m_w_ff1, m_w_ff2=m_w_ff2, m_g_ln2=m_g_ln2, m_b_ln2=m_b_ln2, v_ln_in_g=v_ln_in_g, v_ln_in_b=v_ln_in_b, v_w_in=v_w_in, v_g_cq=v_g_cq, v_w_uq=v_w_uq, v_g_ckv=v_g_ckv, v_w_uk=v_w_uk, v_w_uv=v_w_uv, v_conv_w=v_conv_w, v_conv_b=v_conv_b, v_g_conv_ln=v_g_conv_ln, v_b_conv_ln=v_b_conv_ln, v_w_out=v_w_out, v_g_ln1=v_g_ln1, v_b_ln1=v_b_ln1, v_w_ff1=v_w_ff1, v_w_ff2=v_w_ff2, v_g_ln2=v_g_ln2, v_b_ln2=v_b_ln2)
    weights = {n: given[n] for n in TWIN_WEIGHTS}
    shared = {n: given[n] for n in SHARED_INPUTS}
    per_example = {n: given[n] for n in ['x', 'positions']}
    grad_fn = _jax.value_and_grad(_loss, argnums=(0, 1))

    def one_microbatch(ex, loss_target):
        ex = dict(ex)
        diff = ex.pop(TWIN_DIFF_INPUT)
        return grad_fn(weights, diff, {**shared, **ex}, loss_target)

    if N_MICROBATCH == 1:
        loss, (grad_w, grad_x) = one_microbatch(per_example, given["loss_target"])
    else:
        def body(carry, xs):
            loss_sum, grad_sum = carry
            l_k, (gw_k, gx_k) = one_microbatch(xs[0], xs[1])
            with _jax.named_scope("update"):
                return (loss_sum + l_k, _jax.tree.map(_jnp.add, grad_sum, gw_k)), gx_k

        init = (_jnp.zeros((), _jnp.float32), _jax.tree.map(_jnp.zeros_like, weights))
        (loss, grad_w), grad_x = _jax.lax.scan(body, init, (per_example, given["loss_target"]))
    with _jax.named_scope("update"):
        delta_w, new_m, new_v = {}, {}, {}
        for n in TWIN_WEIGHTS:
            delta_w[n], new_m[n], new_v[n] = _adamw(weights[n], grad_w[n], given["m_" + n], given["v_" + n])
    return (loss, grad_x, *[grad_w[n] for n in TWIN_WEIGHTS], *[delta_w[n] for n in TWIN_WEIGHTS],
            *[new_m[n] for n in TWIN_WEIGHTS], *[new_v[n] for n in TWIN_WEIGHTS])
```

```python
import functools

import jax
import jax.numpy as jnp
from jax import lax
from jax.experimental import pallas as pl
from jax.experimental.pallas import tpu as pltpu

F32 = jnp.float32
BF = jnp.bfloat16

HEADS = 8
D_NOPE = 128
D_ROPE = 64
D_V = 128
D_QK = D_NOPE + D_ROPE
R_Q = 512
R_KV = 512
MLA_W = HEADS * D_V
CONV_K = 31
CONV_PAD = CONV_K // 2
ROPE_BASE = 10000.0
LN_EPS = 1e-5
RMS_EPS = 1e-6
ALPHA = (2.0 * 1) ** 0.25
ADAM_LR = 0.001
ADAM_B1 = 0.9
ADAM_B2 = 0.999
ADAM_EPS = 1e-08
ADAM_WD = 0.01
ADAM_STEP = 10

LANE = 128
SUB = 8
HALO = 16
N_CHIP = 4
MESH = pl.DeviceIdType.MESH
VMEM_MB = 1024 * 1024


def _call(body, **kw):
    return pl.pallas_call(body, **kw)


def _cp(sem, mb=48):
    return pltpu.CompilerParams(dimension_semantics=sem, vmem_limit_bytes=mb * VMEM_MB)


def _sds(shape, dt):
    return jax.ShapeDtypeStruct(shape, dt)


def _dot(a, b):
    return jnp.dot(a, b, preferred_element_type=F32)


def _dot_nt(a, b):
    return lax.dot_general(a, b, (((1,), (1,)), ((), ())), preferred_element_type=F32)


def _dot_tn(a, b):
    return lax.dot_general(a, b, (((0,), (0,)), ((), ())), preferred_element_type=F32)


def _rows8(v):
    t, n = v.shape
    return v.reshape(t // SUB, SUB, n).sum(axis=0)


def _ln_stats(r):
    mu = jnp.mean(r, axis=-1, keepdims=True)
    xc = r - mu
    var = jnp.mean(xc * xc, axis=-1, keepdims=True)
    rstd = lax.rsqrt(var + LN_EPS)
    return xc * rstd, rstd


def _ln_bwd(dy, xhat, rstd, g):
    dyh = dy * g
    m1 = jnp.mean(dyh, axis=-1, keepdims=True)
    m2 = jnp.mean(dyh * xhat, axis=-1, keepdims=True)
    return rstd * (dyh - m1 - xhat * m2)


def _rms_fwd(x, g):
    rr = lax.rsqrt(jnp.mean(x * x, axis=-1, keepdims=True) + RMS_EPS)
    xh = x * rr
    return xh * g, xh, rr


def _rms_bwd(dy, xh, rr, g):
    dyg = dy * g
    return rr * (dyg - xh * jnp.mean(dyg * xh, axis=-1, keepdims=True))


def _rope128(x, cos, sin_signed):
    lane = lax.broadcasted_iota(jnp.int32, x.shape, 1)
    rot = jnp.where(lane < D_ROPE // 2, pltpu.roll(x, LANE - D_ROPE // 2, 1), pltpu.roll(x, D_ROPE // 2, 1))
    return x * cos + rot * sin_signed


def _unrope128(dy, cos, sin_signed):
    t = dy * sin_signed
    lane = lax.broadcasted_iota(jnp.int32, dy.shape, 1)
    rot = jnp.where(lane < D_ROPE // 2, pltpu.roll(t, LANE - D_ROPE // 2, 1), pltpu.roll(t, D_ROPE // 2, 1))
    return dy * cos + rot


def _sigmoid(x):
    return 1.0 / (1.0 + jnp.exp(-x))


def _row_chunks(tm, fn, rc=128):
    rc = min(rc, tm)

    def step(ci, carry):
        fn(pl.ds(pl.multiple_of(ci * rc, rc), rc))
        return carry

    lax.fori_loop(0, tm // rc, step, 0)


def _tile(s, want):
    t = min(s, want)
    assert s % t == 0
    return t


def rope_tables(pos_f, invf):
    s = pos_f.shape[0]
    tm = _tile(s, 1024)

    def body(p_ref, f_ref, c_ref, s_ref):
        ang = p_ref[...] * f_ref[...]
        lane = lax.broadcasted_iota(jnp.int32, ang.shape, 1)
        c = jnp.cos(ang)
        sn = jnp.sin(ang)
        c_ref[...] = jnp.where(lane < D_ROPE, c, 0.0)
        s_ref[...] = jnp.where(lane < D_ROPE // 2, -sn, jnp.where(lane < D_ROPE, sn, 0.0))

    return _call(
        body, name="rope_tables", grid=(s // tm,),
        in_specs=[pl.BlockSpec((tm, 1), lambda i: (i, 0)), pl.BlockSpec((1, LANE), lambda i: (0, 0))],
        out_specs=[pl.BlockSpec((tm, LANE), lambda i: (i, 0))] * 2,
        out_shape=[_sds((s, LANE), F32)] * 2,
        compiler_params=_cp(("arbitrary",)),
    )(pos_f, invf)


def ln_in_fwd(x, g, b):
    s, d = x.shape
    tm = _tile(s, 512)

    def body(x_ref, g_ref, b_ref, o_ref, ob_ref):
        xhat, _ = _ln_stats(x_ref[...])
        y = xhat * g_ref[...] + b_ref[...]
        o_ref[...] = y
        ob_ref[...] = y.astype(BF)

    row = pl.BlockSpec((1, d), lambda i: (0, 0))
    tok = pl.BlockSpec((tm, d), lambda i: (i, 0))
    return _call(
        body, name="ln_in_fwd", grid=(s // tm,), in_specs=[tok, row, row], out_specs=[tok, tok],
        out_shape=[_sds((s, d), F32), _sds((s, d), BF)], compiler_params=_cp(("arbitrary",)),
    )(x, g, b)


def matmul(name, a, w, tm, tn, out_dtype=F32):
    s, k = a.shape
    n = w.shape[1]
    tm = _tile(s, tm)
    tn = _tile(n, tn)

    def body(a_ref, w_ref, o_ref):
        o_ref[...] = _dot(a_ref[...], w_ref[...]).astype(o_ref.dtype)

    return _call(
        body, name=name, grid=(s // tm, n // tn),
        in_specs=[pl.BlockSpec((tm, k), lambda i, j: (i, 0)), pl.BlockSpec((k, tn), lambda i, j: (0, j))],
        out_specs=pl.BlockSpec((tm, tn), lambda i, j: (i, j)),
        out_shape=_sds((s, n), out_dtype), compiler_params=_cp(("arbitrary", "arbitrary")),
    )(a, w)


def q_proj(h, g_cq, wuq, cos, sin):
    s = h.shape[0]
    tm = _tile(s, 512)

    def body(h_ref, g_ref, w_ref, c_ref, s_ref, q_ref, n_ref):
        y, _, _ = _rms_fwd(h_ref[...], g_ref[...])
        yb = y.astype(BF)
        n_ref[...] = yb
        q = _dot(yb, w_ref[...])
        c = c_ref[...]
        sn = s_ref[...]
        for hd in range(HEADS):
            q_ref[hd, :, 0:LANE] = q[:, LANE * hd:LANE * (hd + 1)].astype(BF)
            qr = q[:, MLA_W + LANE * hd:MLA_W + LANE * (hd + 1)]
            q_ref[hd, :, LANE:2 * LANE] = _rope128(qr, c, sn).astype(BF)

    return _call(
        body, name="q_proj", grid=(s // tm,),
        in_specs=[pl.BlockSpec((tm, R_Q), lambda i: (i, 0)), pl.BlockSpec((1, R_Q), lambda i: (0, 0)),
                  pl.BlockSpec((R_Q, 2 * MLA_W), lambda i: (0, 0)),
                  pl.BlockSpec((tm, LANE), lambda i: (i, 0)), pl.BlockSpec((tm, LANE), lambda i: (i, 0))],
        out_specs=[pl.BlockSpec((HEADS, tm, 2 * LANE), lambda i: (0, i, 0)), pl.BlockSpec((tm, R_Q), lambda i: (i, 0))],
        out_shape=[_sds((HEADS, s, 2 * LANE), BF), _sds((s, R_Q), BF)], compiler_params=_cp(("arbitrary",)),
    )(h, g_cq, wuq, cos, sin)


def kv_proj(h, g_ckv, wuk, wuv, cos, sin, kr_blk):
    s = h.shape[0]
    tm = _tile(s, 512)

    def body(h_ref, kr_ref, g_ref, wk_ref, wv_ref, c_ref, s_ref, k_ref, v_ref, n_ref):
        y, _, _ = _rms_fwd(h_ref[...], g_ref[...])
        yb = y.astype(BF)
        n_ref[...] = yb
        kn = _dot(yb, wk_ref[...])
        v = _dot(yb, wv_ref[...])
        kr = _rope128(kr_ref[...], c_ref[...], s_ref[...]).astype(BF)
        for hd in range(HEADS):
            k_ref[hd, :, 0:LANE] = kn[:, LANE * hd:LANE * (hd + 1)].astype(BF)
            k_ref[hd, :, LANE:2 * LANE] = kr
            v_ref[hd] = v[:, LANE * hd:LANE * (hd + 1)].astype(BF)

    tab = pl.BlockSpec((tm, LANE), lambda i: (i, 0))
    wsp = pl.BlockSpec((R_KV, MLA_W), lambda i: (0, 0))
    return _call(
        body, name="kv_proj", grid=(s // tm,),
        in_specs=[pl.BlockSpec((tm, R_KV), lambda i: (i, 1)), pl.BlockSpec((tm, LANE), lambda i: (i, kr_blk)),
                  pl.BlockSpec((1, R_KV), lambda i: (0, 0)), wsp, wsp, tab, tab],
        out_specs=[pl.BlockSpec((HEADS, tm, 2 * LANE), lambda i: (0, i, 0)),
                   pl.BlockSpec((HEADS, tm, LANE), lambda i: (0, i, 0)), pl.BlockSpec((tm, R_KV), lambda i: (i, 0))],
        out_shape=[_sds((HEADS, s, 2 * LANE), BF), _sds((HEADS, s, LANE), BF), _sds((s, R_KV), BF)],
        compiler_params=_cp(("arbitrary",)),
    )(h, h, g_ckv, wuk, wuv, cos, sin)


def attn_fwd(qc, kc, v):
    _, s, _ = qc.shape
    tq = _tile(s, 256)
    tk = _tile(s, 512)
    scale = D_QK ** -0.5

    def body(q_ref, k_ref, v_ref, o_ref, ob_ref, l_ref):
        q = q_ref[...]

        def step(j, carry):
            m, l, acc = carry
            off = pl.multiple_of(j * tk, tk)
            sc = _dot_nt(q, k_ref[pl.ds(off, tk), :]) * scale
            m_new = jnp.maximum(m, jnp.max(sc, axis=-1, keepdims=True))
            p = jnp.exp(sc - m_new)
            a = jnp.exp(m - m_new)
            l = a * l + jnp.sum(p, axis=-1, keepdims=True)
            acc = a * acc + _dot(p.astype(BF), v_ref[pl.ds(off, tk), :])
            return m_new, l, acc

        m0 = jnp.full((tq, 1), -jnp.inf, F32)
        m, l, acc = lax.fori_loop(0, s // tk, step, (m0, jnp.zeros((tq, 1), F32), jnp.zeros((tq, D_V), F32)))
        o = acc / l
        o_ref[...] = o
        ob_ref[...] = o.astype(BF)
        l_ref[...] = m + jnp.log(l)

    return _call(
        body, name="attn_fwd", grid=(HEADS, s // tq),
        in_specs=[pl.BlockSpec((None, tq, 2 * LANE), lambda h, i: (h, i, 0)),
                  pl.BlockSpec((None, s, 2 * LANE), lambda h, i: (h, 0, 0)),
                  pl.BlockSpec((None, s, LANE), lambda h, i: (h, 0, 0))],
        out_specs=[pl.BlockSpec((tq, LANE), lambda h, i: (i, h)), pl.BlockSpec((tq, LANE), lambda h, i: (i, h)),
                   pl.BlockSpec((None, tq, 1), lambda h, i: (h, i, 0))],
        out_shape=[_sds((s, MLA_W), F32), _sds((s, MLA_W), BF), _sds((HEADS, s, 1), F32)],
        compiler_params=_cp(("arbitrary", "arbitrary")),
    )(qc, kc, v)


def _halo_specs(tm, s, width, col):
    r = tm // HALO
    nb = s // HALO
    cur = pl.BlockSpec((tm, width), lambda i: (i, col))
    prev = pl.BlockSpec((HALO, width), lambda i: (jnp.maximum(i * r - 1, 0), col))
    nxt = pl.BlockSpec((HALO, width), lambda i: (jnp.minimum((i + 1) * r, nb - 1), col))
    return cur, prev, nxt


def _fill_slab(slab, tm, prev, cur, nxt):
    i = pl.program_id(0)
    last = pl.num_programs(0) - 1
    slab[0:HALO, :] = jnp.where(i > 0, prev, 0.0)
    slab[HALO:HALO + tm, :] = cur
    slab[HALO + tm:2 * HALO + tm, :] = jnp.where(i < last, nxt, 0.0)


def conv_fwd(h, conv_w, conv_b, g_ln, b_ln):
    s = h.shape[0]
    c = conv_w.shape[1]
    tm = _tile(s, 256)
    rc = _tile(tm, 128)

    def body(a_ref, ap_ref, an_ref, g_ref, gp_ref, gn_ref, w_ref, cb_ref, lg_ref, lb_ref, co_ref, uc_ref, slab):
        _fill_slab(slab, tm, ap_ref[...] * _sigmoid(gp_ref[...]), a_ref[...] * _sigmoid(g_ref[...]),
                   an_ref[...] * _sigmoid(gn_ref[...]))
        for cb in range(c // LANE):
            cs = slice(LANE * cb, LANE * (cb + 1))
            for r0 in range(0, tm, rc):
                acc = jnp.zeros((rc, LANE), F32)
                for k in range(CONV_K):
                    acc = acc + w_ref[k:k + 1, cs] * slab[pl.ds(r0 + HALO - CONV_PAD + k, rc), cs]
                uc_ref[r0:r0 + rc, cs] = acc + cb_ref[:, cs]
        xhat, _ = _ln_stats(uc_ref[...])
        cl = xhat * lg_ref[...] + lb_ref[...]
        co_ref[...] = (cl * _sigmoid(cl)).astype(BF)

    a_specs = _halo_specs(tm, s, c, 1)
    g_specs = _halo_specs(tm, s, c, 2)
    row = pl.BlockSpec((1, c), lambda i: (0, 0))
    tok = pl.BlockSpec((tm, c), lambda i: (i, 0))
    return _call(
        body, name="conv_fwd", grid=(s // tm,),
        in_specs=[*a_specs, *g_specs, pl.BlockSpec(conv_w.shape, lambda i: (0, 0)), row, row, row],
        out_specs=[tok, tok], out_shape=[_sds((s, c), BF), _sds((s, c), F32)],
        scratch_shapes=[pltpu.VMEM((tm + 2 * HALO, c), F32)], compiler_params=_cp(("arbitrary",)),
    )(h, h, h, h, h, h, conv_w, conv_b, g_ln, b_ln)


def out_proj_ln1(ob, co, wout, x0, g1, b1):
    s, d = x0.shape
    kh = ob.shape[1]
    tm = _tile(s, 256)

    def body(o_ref, c_ref, w_ref, x_ref, g_ref, b_ref, r_ref, x1_ref, x1b_ref, acc):
        k = pl.program_id(1)

        @pl.when(k == 0)
        def _():
            acc[...] = _dot(o_ref[...], w_ref[...])

        @pl.when(k == 1)
        def _():
            r = ALPHA * x_ref[...] + (acc[...] + _dot(c_ref[...], w_ref[...]))
            r_ref[...] = r
            xhat, _ = _ln_stats(r)
            y = xhat * g_ref[...] + b_ref[...]
            x1_ref[...] = y
            x1b_ref[...] = y.astype(BF)

    half = pl.BlockSpec((tm, kh), lambda i, k: (i, 0))
    tok = pl.BlockSpec((tm, d), lambda i, k: (i, 0))
    row = pl.BlockSpec((1, d), lambda i, k: (0, 0))
    return _call(
        body, name="out_proj_ln1", grid=(s // tm, 2),
        in_specs=[half, half, pl.BlockSpec((kh, d), lambda i, k: (k, 0)), tok, row, row],
        out_specs=[tok, tok, tok], out_shape=[_sds((s, d), F32), _sds((s, d), F32), _sds((s, d), BF)],
        scratch_shapes=[pltpu.VMEM((tm, d), F32)], compiler_params=_cp(("arbitrary", "arbitrary")),
    )(ob, co, wout, x0, g1, b1)


def ff1_fwd(x1b, wff1_g):
    s, d = x1b.shape
    nsh, _, fs = wff1_g.shape
    tm = _tile(s, 512)
    tn = _tile(fs, 1024)
    per = fs // tn

    def body(a_ref, w_ref, r_ref, a1_ref):
        r = jnp.maximum(_dot(a_ref[...], w_ref[...]), 0.0)
        r_ref[...] = r.astype(BF)
        a1_ref[...] = (r * r).astype(BF)

    out = pl.BlockSpec((tm, tn), lambda i, j: (i, j))
    return _call(
        body, name="ff1_fwd", grid=(s // tm, nsh * per),
        in_specs=[pl.BlockSpec((tm, d), lambda i, j: (i, 0)),
                  pl.BlockSpec((None, d, tn), lambda i, j: (j // per, 0, j % per))],
        out_specs=[out, out], out_shape=[_sds((s, nsh * fs), BF)] * 2,
        compiler_params=_cp(("arbitrary", "arbitrary")),
    )(x1b, wff1_g)


def ff2_ln2_loss(a1b, wff2, x1, target, g2, b2):
    s, f = a1b.shape
    d = x1.shape[1]
    tm = _tile(s, 512)
    tk = _tile(f, 1024)
    nk = f // tk

    def body(a_ref, w_ref, x_ref, t_ref, g_ref, b_ref, dr_ref, drb_ref, loss_ref, dg_ref, db_ref, acc):
        i = pl.program_id(0)
        k = pl.program_id(1)

        @pl.when(k == 0)
        def _():
            acc[...] = _dot(a_ref[...], w_ref[...])

        @pl.when(k > 0)
        def _():
            acc[...] += _dot(a_ref[...], w_ref[...])

        @pl.when(jnp.logical_and(i == 0, k == 0))
        def _():
            loss_ref[...] = jnp.zeros_like(loss_ref)
            dg_ref[...] = jnp.zeros_like(dg_ref)
            db_ref[...] = jnp.zeros_like(db_ref)

        @pl.when(k == nk - 1)
        def _():
            g = g_ref[...]

            def chunk(rows):
                r = ALPHA * x_ref[rows, :] + acc[rows, :]
                xhat, rstd = _ln_stats(r)
                e = xhat * g + b_ref[...] - t_ref[rows, :]
                e2 = _rows8(e * e)
                part = e2[:, 0:LANE]
                for c in range(1, d // LANE):
                    part = part + e2[:, LANE * c:LANE * (c + 1)]
                loss_ref[...] += part * (0.5 / d)
                dy = e * (1.0 / d)
                dg_ref[...] += _rows8(dy * xhat)
                db_ref[...] += _rows8(dy)
                dr = _ln_bwd(dy, xhat, rstd, g)
                dr_ref[rows, :] = dr
                drb_ref[rows, :] = dr.astype(BF)

            _row_chunks(tm, chunk)

    tok = pl.BlockSpec((tm, d), lambda i, k: (i, 0))
    row = pl.BlockSpec((1, d), lambda i, k: (0, 0))
    accs = pl.BlockSpec((SUB, d), lambda i, k: (0, 0))
    return _call(
        body, name="ff2_ln2_loss", grid=(s // tm, nk),
        in_specs=[pl.BlockSpec((tm, tk), lambda i, k: (i, k)), pl.BlockSpec((tk, d), lambda i, k: (k, 0)),
                  tok, tok, row, row],
        out_specs=[tok, tok, pl.BlockSpec((SUB, LANE), lambda i, k: (0, 0)), accs, accs],
        out_shape=[_sds((s, d), F32), _sds((s, d), BF), _sds((SUB, LANE), F32), _sds((SUB, d), F32), _sds((SUB, d), F32)],
        scratch_shapes=[pltpu.VMEM((tm, d), F32)], compiler_params=_cp(("arbitrary", "arbitrary"), 56),
    )(a1b, wff2, x1, target, g2, b2)


def ff2_bwd_act(dr2b, wff2t, rb):
    s, d = dr2b.shape
    f = wff2t.shape[1]
    tm = _tile(s, 512)
    tn = _tile(f, 1024)

    def body(a_ref, w_ref, r_ref, o_ref):
        o_ref[...] = (_dot(a_ref[...], w_ref[...]) * (2.0 * r_ref[...].astype(F32))).astype(BF)

    return _call(
        body, name="ff2_bwd_act", grid=(s // tm, f // tn),
        in_specs=[pl.BlockSpec((tm, d), lambda i, j: (i, 0)), pl.BlockSpec((d, tn), lambda i, j: (0, j)),
                  pl.BlockSpec((tm, tn), lambda i, j: (i, j))],
        out_specs=pl.BlockSpec((tm, tn), lambda i, j: (i, j)), out_shape=_sds((s, f), BF),
        compiler_params=_cp(("arbitrary", "arbitrary")),
    )(dr2b, wff2t, rb)


def wgrad(name, a, b, tm, tn, tk=1024, shards=1):
    s, m = a.shape
    n = b.shape[1]
    tm = _tile(m, tm)
    ns = n // shards
    tn = _tile(ns, tn)
    tk = _tile(s, tk)
    per = ns // tn

    def body(a_ref, b_ref, o_ref):
        k = pl.program_id(2)

        @pl.when(k == 0)
        def _():
            o_ref[...] = _dot_tn(a_ref[...], b_ref[...])

        @pl.when(k > 0)
        def _():
            o_ref[...] += _dot_tn(a_ref[...], b_ref[...])

    return _call(
        body, name=name, grid=(m // tm, n // tn, s // tk),
        in_specs=[pl.BlockSpec((tk, tm), lambda i, j, k: (k, i)), pl.BlockSpec((tk, tn), lambda i, j, k: (k, j))],
        out_specs=pl.BlockSpec((None, tm, tn), lambda i, j, k: (j // per, i, j % per)),
        out_shape=_sds((shards, m, ns), F32), compiler_params=_cp(("arbitrary", "arbitrary", "arbitrary")),
    )(a, b)


def ff1_bwd_ln1(df1b, wff1t, dr2, r1, g1):
    s, f = df1b.shape
    d = dr2.shape[1]
    tm = _tile(s, 512)
    tk = _tile(f, 1024)
    nk = f // tk

    def body(a_ref, w_ref, d2_ref, r_ref, g_ref, dr_ref, drb_ref, dg_ref, db_ref, acc):
        i = pl.program_id(0)
        k = pl.program_id(1)

        @pl.when(k == 0)
        def _():
            acc[...] = _dot(a_ref[...], w_ref[...])

        @pl.when(k > 0)
        def _():
            acc[...] += _dot(a_ref[...], w_ref[...])

        @pl.when(jnp.logical_and(i == 0, k == 0))
        def _():
            dg_ref[...] = jnp.zeros_like(dg_ref)
            db_ref[...] = jnp.zeros_like(db_ref)

        @pl.when(k == nk - 1)
        def _():
            g = g_ref[...]

            def chunk(rows):
                dy = ALPHA * d2_ref[rows, :] + acc[rows, :]
                xhat, rstd = _ln_stats(r_ref[rows, :])
                dg_ref[...] += _rows8(dy * xhat)
                db_ref[...] += _rows8(dy)
                dr = _ln_bwd(dy, xhat, rstd, g)
                dr_ref[rows, :] = dr
                drb_ref[rows, :] = dr.astype(BF)

            _row_chunks(tm, chunk)

    tok = pl.BlockSpec((tm, d), lambda i, k: (i, 0))
    accs = pl.BlockSpec((SUB, d), lambda i, k: (0, 0))
    return _call(
        body, name="ff1_bwd_ln1", grid=(s // tm, nk),
        in_specs=[pl.BlockSpec((tm, tk), lambda i, k: (i, k)), pl.BlockSpec((tk, d), lambda i, k: (k, 0)),
                  tok, tok, pl.BlockSpec((1, d), lambda i, k: (0, 0))],
        out_specs=[tok, tok, accs, accs],
        out_shape=[_sds((s, d), F32), _sds((s, d), BF), _sds((SUB, d), F32), _sds((SUB, d), F32)],
        scratch_shapes=[pltpu.VMEM((tm, d), F32)], compiler_params=_cp(("arbitrary", "arbitrary"), 56),
    )(df1b, wff1t, dr2, r1, g1)


def out_proj_bwd(dr1b, woutt, o):
    s, d = dr1b.shape
    tm = _tile(s, 256)

    def body(a_ref, w_ref, o_ref, do_ref, dc_ref, dl_ref):
        dcat = _dot(a_ref[...], w_ref[...])
        do = dcat[:, 0:MLA_W]
        do_ref[...] = do.astype(BF)
        dc_ref[...] = dcat[:, MLA_W:]
        prod = do * o_ref[...]
        for hd in range(HEADS):
            dl_ref[hd] = jnp.sum(prod[:, LANE * hd:LANE * (hd + 1)], axis=-1, keepdims=True)

    half = pl.BlockSpec((tm, MLA_W), lambda i: (i, 0))
    return _call(
        body, name="out_proj_bwd", grid=(s // tm,),
        in_specs=[pl.BlockSpec((tm, d), lambda i: (i, 0)), pl.BlockSpec((d, d), lambda i: (0, 0)), half],
        out_specs=[half, pl.BlockSpec((tm, d - MLA_W), lambda i: (i, 0)), pl.BlockSpec((HEADS, tm, 1), lambda i: (0, i, 0))],
        out_shape=[_sds((s, MLA_W), BF), _sds((s, d - MLA_W), F32), _sds((HEADS, s, 1), F32)],
        compiler_params=_cp(("arbitrary",)),
    )(dr1b, woutt, o)


def conv_bwd_ln(uc, dco, g_ln, b_ln):
    s, c = uc.shape
    tm = _tile(s, 512)

    def body(u_ref, d_ref, g_ref, b_ref, du_ref, dg_ref, db_ref, dcb_ref):
        @pl.when(pl.program_id(0) == 0)
        def _():
            dg_ref[...] = jnp.zeros_like(dg_ref)
            db_ref[...] = jnp.zeros_like(db_ref)
            dcb_ref[...] = jnp.zeros_like(dcb_ref)

        xhat, rstd = _ln_stats(u_ref[...])
        g = g_ref[...]
        cl = xhat * g + b_ref[...]
        sg = _sigmoid(cl)
        dcl = d_ref[...] * (sg * (1.0 + cl * (1.0 - sg)))
        dg_ref[...] += _rows8(dcl * xhat)
        db_ref[...] += _rows8(dcl)
        du = _ln_bwd(dcl, xhat, rstd, g)
        du_ref[...] = du
        dcb_ref[...] += _rows8(du)

    tok = pl.BlockSpec((tm, c), lambda i: (i, 0))
    row = pl.BlockSpec((1, c), lambda i: (0, 0))
    accs = pl.BlockSpec((SUB, c), lambda i: (0, 0))
    return _call(
        body, name="conv_bwd_ln", grid=(s // tm,), in_specs=[tok, tok, row, row], out_specs=[tok, accs, accs, accs],
        out_shape=[_sds((s, c), F32)] + [_sds((SUB, c), F32)] * 3, compiler_params=_cp(("arbitrary",)),
    )(uc, dco, g_ln, b_ln)


def conv_bwd_taps(h, duc, conv_w):
    s, c = duc.shape
    tm = _tile(s, 256)
    rc = _tile(tm, 128)

    def body(a_ref, ap_ref, an_ref, g_ref, gp_ref, gn_ref, d_ref, dp_ref, dn_ref, w_ref, o_ref, dw_ref, uslab, dslab, du_s):
        @pl.when(pl.program_id(0) == 0)
        def _():
            dw_ref[...] = jnp.zeros_like(dw_ref)

        sg = _sigmoid(g_ref[...])
        a = a_ref[...]
        _fill_slab(uslab, tm, ap_ref[...] * _sigmoid(gp_ref[...]), a * sg, an_ref[...] * _sigmoid(gn_ref[...]))
        _fill_slab(dslab, tm, dp_ref[...], d_ref[...], dn_ref[...])
        for cb in range(c // LANE):
            cs = slice(LANE * cb, LANE * (cb + 1))
            for r0 in range(0, tm, rc):
                acc = jnp.zeros((rc, LANE), F32)
                dcur = dslab[pl.ds(r0 + HALO, rc), cs]
                for k in range(CONV_K):
                    acc = acc + w_ref[k:k + 1, cs] * dslab[pl.ds(r0 + HALO + CONV_PAD - k, rc), cs]
                    dw_ref[k:k + 1, cs] += jnp.sum(dcur * uslab[pl.ds(r0 + HALO - CONV_PAD + k, rc), cs], axis=0, keepdims=True)
                du_s[r0:r0 + rc, cs] = acc
        du = du_s[...]
        o_ref[:, 0:c] = (du * sg).astype(BF)
        o_ref[:, c:2 * c] = (du * a * sg * (1.0 - sg)).astype(BF)

    a_specs = _halo_specs(tm, s, c, 1)
    g_specs = _halo_specs(tm, s, c, 2)
    d_specs = _halo_specs(tm, s, c, 0)
    wsp = pl.BlockSpec(conv_w.shape, lambda i: (0, 0))
    return _call(
        body, name="conv_bwd_taps", grid=(s // tm,), in_specs=[*a_specs, *g_specs, *d_specs, wsp],
        out_specs=[pl.BlockSpec((tm, 2 * c), lambda i: (i, 0)), wsp],
        out_shape=[_sds((s, 2 * c), BF), _sds(conv_w.shape, F32)],
        scratch_shapes=[pltpu.VMEM((tm + 2 * HALO, c), F32), pltpu.VMEM((tm + 2 * HALO, c), F32), pltpu.VMEM((tm, c), F32)],
        compiler_params=_cp(("arbitrary",)),
    )(h, h, h, h, h, h, duc, duc, duc, conv_w)


def attn_bwd(qc, kc, kct, v, dob, lse_r, delta_r):
    _, s, _ = qc.shape
    tk = _tile(s, 512)
    tq = _tile(s, 256)
    scale = D_QK ** -0.5

    def body(k_ref, kt_ref, v_ref, q_ref, do_ref, l_ref, dl_ref, dqt_ref, dk_ref, dv_ref):
        @pl.when(pl.program_id(1) == 0)
        def _():
            dqt_ref[...] = jnp.zeros_like(dqt_ref)

        k = k_ref[...]
        kt = kt_ref[...]
        vv = v_ref[...]

        def step(i, carry):
            dk, dv = carry
            off = pl.multiple_of(i * tq, tq)
            q = q_ref[pl.ds(off, tq), :]
            do = do_ref[pl.ds(off, tq), :]
            st = _dot_nt(k, q) * scale
            pt = jnp.exp(st - l_ref[:, pl.ds(off, tq)])
            dv = dv + _dot(pt.astype(BF), do)
            dpt = _dot_nt(vv, do)
            dsb = (pt * (dpt - dl_ref[:, pl.ds(off, tq)]) * scale).astype(BF)
            dk = dk + _dot(dsb, q)
            dqt_ref[:, pl.ds(off, tq)] += _dot(kt, dsb)
            return dk, dv

        dk, dv = lax.fori_loop(0, s // tq, step, (jnp.zeros((tk, 2 * LANE), F32), jnp.zeros((tk, LANE), F32)))
        dk_ref[...] = dk
        dv_ref[...] = dv

    rowv = pl.BlockSpec((None, 1, s), lambda h, j: (h, 0, 0))
    return _call(
        body, name="attn_bwd", grid=(HEADS, s // tk),
        in_specs=[pl.BlockSpec((None, tk, 2 * LANE), lambda h, j: (h, j, 0)),
                  pl.BlockSpec((None, 2 * LANE, tk), lambda h, j: (h, 0, j)),
                  pl.BlockSpec((None, tk, LANE), lambda h, j: (h, j, 0)),
                  pl.BlockSpec((None, s, 2 * LANE), lambda h, j: (h, 0, 0)),
                  pl.BlockSpec((s, LANE), lambda h, j: (0, h)), rowv, rowv],
        out_specs=[pl.BlockSpec((None, 2 * LANE, s), lambda h, j: (h, 0, 0)),
                   pl.BlockSpec((None, tk, 2 * LANE), lambda h, j: (h, j, 0)),
                   pl.BlockSpec((None, tk, LANE), lambda h, j: (h, j, 0))],
        out_shape=[_sds((HEADS, 2 * LANE, s), F32), _sds((HEADS, s, 2 * LANE), F32), _sds((HEADS, s, LANE), F32)],
        compiler_params=_cp(("arbitrary", "arbitrary"), 56),
    )(kc, kct, v, qc, dob, lse_r, delta_r)


def q_bwd(dqt, h, g_cq, wuqt, cos, sin):
    s = h.shape[0]
    tm = _tile(s, 256)

    def body(d_ref, h_ref, g_ref, w_ref, c_ref, s_ref, dq_ref, dc_ref, dg_ref):
        @pl.when(pl.program_id(0) == 0)
        def _():
            dg_ref[...] = jnp.zeros_like(dg_ref)

        c = c_ref[...]
        sn = s_ref[...]
        for hd in range(HEADS):
            t = d_ref[hd].T
            dq_ref[:, LANE * hd:LANE * (hd + 1)] = t[:, 0:LANE].astype(BF)
            dq_ref[:, MLA_W + LANE * hd:MLA_W + LANE * (hd + 1)] = _unrope128(t[:, LANE:2 * LANE], c, sn).astype(BF)
        dy = _dot(dq_ref[...], w_ref[...])
        g = g_ref[...]
        _, xh, rr = _rms_fwd(h_ref[...], g)
        dg_ref[...] += _rows8(dy * xh)
        dc_ref[...] = _rms_bwd(dy, xh, rr, g).astype(BF)

    tab = pl.BlockSpec((tm, LANE), lambda i: (i, 0))
    return _call(
        body, name="q_bwd", grid=(s // tm,),
        in_specs=[pl.BlockSpec((HEADS, 2 * LANE, tm), lambda i: (0, 0, i)), pl.BlockSpec((tm, R_Q), lambda i: (i, 0)),
                  pl.BlockSpec((1, R_Q), lambda i: (0, 0)), pl.BlockSpec((2 * MLA_W, R_Q), lambda i: (0, 0)), tab, tab],
        out_specs=[pl.BlockSpec((tm, 2 * MLA_W), lambda i: (i, 0)), pl.BlockSpec((tm, R_Q), lambda i: (i, 0)),
                   pl.BlockSpec((SUB, R_Q), lambda i: (0, 0))],
        out_shape=[_sds((s, 2 * MLA_W), BF), _sds((s, R_Q), BF), _sds((SUB, R_Q), F32)],
        compiler_params=_cp(("arbitrary",)),
    )(dqt, h, g_cq, wuqt, cos, sin)


def kv_bwd(dk, dv, h, g_ckv, wukt, wuvt, cos, sin):
    s = h.shape[0]
    tm = _tile(s, 256)

    def body(dk_ref, dv_ref, h_ref, g_ref, wk_ref, wv_ref, c_ref, s_ref, dkn_ref, dvb_ref, dc_ref, dkr_ref, dg_ref):
        @pl.when(pl.program_id(0) == 0)
        def _():
            dg_ref[...] = jnp.zeros_like(dg_ref)

        dkr = dk_ref[0, :, LANE:2 * LANE]
        for hd in range(HEADS):
            dkn_ref[:, LANE * hd:LANE * (hd + 1)] = dk_ref[hd, :, 0:LANE].astype(BF)
            dvb_ref[:, LANE * hd:LANE * (hd + 1)] = dv_ref[hd].astype(BF)
            if hd > 0:
                dkr = dkr + dk_ref[hd, :, LANE:2 * LANE]
        dkr_ref[...] = _unrope128(dkr, c_ref[...], s_ref[...]).astype(BF)
        dy = _dot(dkn_ref[...], wk_ref[...]) + _dot(dvb_ref[...], wv_ref[...])
        g = g_ref[...]
        _, xh, rr = _rms_fwd(h_ref[...], g)
        dg_ref[...] += _rows8(dy * xh)
        dc_ref[...] = _rms_bwd(dy, xh, rr, g).astype(BF)

    tab = pl.BlockSpec((tm, LANE), lambda i: (i, 0))
    wsp = pl.BlockSpec((MLA_W, R_KV), lambda i: (0, 0))
    wide = pl.BlockSpec((tm, MLA_W), lambda i: (i, 0))
    return _call(
        body, name="kv_bwd", grid=(s // tm,),
        in_specs=[pl.BlockSpec((HEADS, tm, 2 * LANE), lambda i: (0, i, 0)), pl.BlockSpec((HEADS, tm, LANE), lambda i: (0, i, 0)),
                  pl.BlockSpec((tm, R_KV), lambda i: (i, 1)), pl.BlockSpec((1, R_KV), lambda i: (0, 0)), wsp, wsp, tab, tab],
        out_specs=[wide, wide, pl.BlockSpec((tm, R_KV), lambda i: (i, 0)), tab, pl.BlockSpec((SUB, R_KV), lambda i: (0, 0))],
        out_shape=[_sds((s, MLA_W), BF), _sds((s, MLA_W), BF), _sds((s, R_KV), BF), _sds((s, LANE), BF), _sds((SUB, R_KV), F32)],
        compiler_params=_cp(("arbitrary",)),
    )(dk, dv, h, g_ckv, wukt, wuvt, cos, sin)


def in_proj_bwd_ln(dh, wint, dr1, x, g_in):
    s, hc = dh.shape
    d = x.shape[1]
    tm = _tile(s, 512)
    tk = _tile(hc, 640)
    nk = hc // tk

    def body(a_ref, w_ref, d1_ref, x_ref, g_ref, gx_ref, dg_ref, db_ref, acc):
        i = pl.program_id(0)
        k = pl.program_id(1)

        @pl.when(k == 0)
        def _():
            acc[...] = _dot(a_ref[...], w_ref[...])

        @pl.when(k > 0)
        def _():
            acc[...] += _dot(a_ref[...], w_ref[...])

        @pl.when(jnp.logical_and(i == 0, k == 0))
        def _():
            dg_ref[...] = jnp.zeros_like(dg_ref)
            db_ref[...] = jnp.zeros_like(db_ref)

        @pl.when(k == nk - 1)
        def _():
            g = g_ref[...]

            def chunk(rows):
                dy = ALPHA * d1_ref[rows, :] + acc[rows, :]
                xhat, rstd = _ln_stats(x_ref[rows, :])
                dg_ref[...] += _rows8(dy * xhat)
                db_ref[...] += _rows8(dy)
                gx_ref[rows, :] = _ln_bwd(dy, xhat, rstd, g)

            _row_chunks(tm, chunk)

    tok = pl.BlockSpec((tm, d), lambda i, k: (i, 0))
    accs = pl.BlockSpec((SUB, d), lambda i, k: (0, 0))
    return _call(
        body, name="in_proj_bwd_ln", grid=(s // tm, nk),
        in_specs=[pl.BlockSpec((tm, tk), lambda i, k: (i, k)), pl.BlockSpec((tk, d), lambda i, k: (k, 0)),
                  tok, tok, pl.BlockSpec((1, d), lambda i, k: (0, 0))],
        out_specs=[tok, accs, accs], out_shape=[_sds((s, d), F32), _sds((SUB, d), F32), _sds((SUB, d), F32)],
        scratch_shapes=[pltpu.VMEM((tm, d), F32)], compiler_params=_cp(("arbitrary", "arbitrary")),
    )(dh, wint, dr1, x, g_in)


def _adamw_math(w, g, m, v):
    m = ADAM_B1 * m + (1.0 - ADAM_B1) * g
    v = ADAM_B2 * v + (1.0 - ADAM_B2) * (g * g)
    m_hat = m / (1.0 - ADAM_B1 ** ADAM_STEP)
    v_hat = v / (1.0 - ADAM_B2 ** ADAM_STEP)
    delta = -ADAM_LR * (m_hat / (jnp.sqrt(v_hat) + ADAM_EPS) + ADAM_WD * w)
    return delta, m, v


def adamw(name, w, g, m, v):
    r, c = w.shape
    tr = _row_tile(r, c)

    def body(w_ref, g_ref, m_ref, v_ref, d_ref, mo_ref, vo_ref):
        d_ref[...], mo_ref[...], vo_ref[...] = _adamw_math(w_ref[...], g_ref[...], m_ref[...], v_ref[...])

    blk = pl.BlockSpec((tr, c), lambda i: (i, 0))
    return _call(
        body, name=name, grid=(r // tr,), in_specs=[blk] * 4, out_specs=[blk] * 3,
        out_shape=[_sds((r, c), F32)] * 3, compiler_params=_cp(("arbitrary",)),
    )(w, g, m, v)


def _coords():
    return lax.axis_index("x"), lax.axis_index("y"), lax.axis_index("c")


def _other_chips(x, y):
    return [(1 - x, y, 2 * (1 - x) + y), (x, 1 - y, 2 * x + 1 - y), (1 - x, 1 - y, 2 * (1 - x) + 1 - y)]


ANY = pl.BlockSpec(memory_space=pl.ANY)


def all_gather_shards(shards):
    n = len(shards)

    def body(*refs):
        ins, outs = refs[:n], refs[n:2 * n]
        ici_s, ici_r, d2d_s, d2d_r, loc = refs[2 * n:]
        x, y, c = _coords()
        me = 2 * x + y
        peers = _other_chips(x, y)
        locals_, sends, fwds = [], [], []
        for a in range(n):
            rh = ins[a].shape[0] // 2
            mine = pl.ds(c * rh, rh)
            lc = pltpu.make_async_copy(ins[a], outs[a].at[me], loc.at[a])
            lc.start()
            locals_.append(lc)
            for j, (px, py, pk) in enumerate(peers):
                cp = pltpu.make_async_remote_copy(
                    src_ref=ins[a].at[mine], dst_ref=outs[a].at[me, mine], send_sem=ici_s.at[a, j], recv_sem=ici_r.at[a, j],
                    device_id=(px, py, c), device_id_type=MESH)
                cp.start()
                sends.append(cp)
        for a in range(n):
            rh = ins[a].shape[0] // 2
            mine = pl.ds(c * rh, rh)
            for j, (px, py, pk) in enumerate(peers):
                got = outs[a].at[pk, mine]
                pltpu.make_async_remote_copy(
                    src_ref=got, dst_ref=got, send_sem=ici_s.at[a, j], recv_sem=ici_r.at[a, j],
                    device_id=(px, py, c), device_id_type=MESH).wait_recv()
                fw = pltpu.make_async_remote_copy(
                    src_ref=got, dst_ref=got, send_sem=d2d_s.at[a, j], recv_sem=d2d_r.at[a, j],
                    device_id=(x, y, 1 - c), device_id_type=MESH)
                fw.start()
                fwds.append(fw)
        for a in range(n):
            rh = ins[a].shape[0] // 2
            theirs = pl.ds((1 - c) * rh, rh)
            for j, (px, py, pk) in enumerate(peers):
                got = outs[a].at[pk, theirs]
                pltpu.make_async_remote_copy(
                    src_ref=got, dst_ref=got, send_sem=d2d_s.at[a, j], recv_sem=d2d_r.at[a, j],
                    device_id=(x, y, 1 - c), device_id_type=MESH).wait_recv()
        for cp in sends + fwds:
            cp.wait_send()
        for lc in locals_:
            lc.wait()

    return _call(
        body, name="all_gather_shards", in_specs=[ANY] * n, out_specs=[ANY] * n,
        out_shape=[_sds((N_CHIP,) + w.shape, w.dtype) for w in shards],
        scratch_shapes=[pltpu.SemaphoreType.DMA((n, 3))] * 4 + [pltpu.SemaphoreType.DMA((n,))],
    )(*shards)


def pair_exchange(grads):
    n = len(grads)

    def body(*refs):
        ins, outs = refs[:n], refs[n:2 * n]
        ss, rs = refs[2 * n:]
        x, y, c = _coords()
        cps = []
        for a in range(n):
            rh = ins[a].shape[1] // 2
            cp = pltpu.make_async_remote_copy(
                src_ref=ins[a].at[:, pl.ds((1 - c) * rh, rh)], dst_ref=outs[a], send_sem=ss.at[a], recv_sem=rs.at[a],
                device_id=(x, y, 1 - c), device_id_type=MESH)
            cp.start()
            cps.append(cp)
        for cp in cps:
            cp.wait()

    return _call(
        body, name="pair_exchange", in_specs=[ANY] * n, out_specs=[ANY] * n,
        out_shape=[_sds((N_CHIP, g.shape[1] // 2, g.shape[2]), F32) for g in grads],
        scratch_shapes=[pltpu.SemaphoreType.DMA((n,))] * 2,
    )(*grads)


def _row_tile(rows, cols, itemsize=4, budget=2 * VMEM_MB):
    t = rows
    while t * cols * itemsize > budget and t % (2 * SUB) == 0:
        t //= 2
    return t


def pair_add(g, r, cidx):
    _, rows, cols = g.shape
    rh = rows // 2
    tr = _row_tile(rh, cols)
    per = rh // tr

    def body(c_ref, g_ref, r_ref, o_ref):
        o_ref[...] = g_ref[...] + r_ref[...]

    return _call(
        body, name="pair_add",
        grid_spec=pltpu.PrefetchScalarGridSpec(
            num_scalar_prefetch=1, grid=(N_CHIP, per),
            in_specs=[pl.BlockSpec((None, tr, cols), lambda k, i, c: (k, c[0] * per + i, 0)),
                      pl.BlockSpec((None, tr, cols), lambda k, i, c: (k, i, 0))],
            out_specs=pl.BlockSpec((None, tr, cols), lambda k, i, c: (k, i, 0))),
        out_shape=_sds((N_CHIP, rh, cols), F32), compiler_params=_cp(("arbitrary", "arbitrary")),
    )(cidx, g, r)


def chip_exchange(parts):
    n = len(parts)

    def body(*refs):
        ins, outs = refs[:n], refs[n:2 * n]
        ss, rs = refs[2 * n:]
        x, y, c = _coords()
        cps = []
        for a in range(n):
            for j, (px, py, pk) in enumerate(_other_chips(x, y)):
                cp = pltpu.make_async_remote_copy(
                    src_ref=ins[a].at[pk], dst_ref=outs[a].at[j], send_sem=ss.at[a, j], recv_sem=rs.at[a, j],
                    device_id=(px, py, c), device_id_type=MESH)
                cp.start()
                cps.append(cp)
        for cp in cps:
            cp.wait()

    return _call(
        body, name="chip_exchange", in_specs=[ANY] * n, out_specs=[ANY] * n,
        out_shape=[_sds((N_CHIP - 1,) + p.shape[1:], F32) for p in parts],
        scratch_shapes=[pltpu.SemaphoreType.DMA((n, 3))] * 2,
    )(*parts)


def chip_add(p, r, kidx):
    _, rh, cols = p.shape
    tr = _row_tile(rh, cols)

    def body(k_ref, p_ref, r_ref, o_ref):
        o_ref[...] = ((p_ref[...] + r_ref[0]) + r_ref[1]) + r_ref[2]

    return _call(
        body, name="chip_add",
        grid_spec=pltpu.PrefetchScalarGridSpec(
            num_scalar_prefetch=1, grid=(rh // tr,),
            in_specs=[pl.BlockSpec((None, tr, cols), lambda i, k: (k[0], i, 0)),
                      pl.BlockSpec((N_CHIP - 1, tr, cols), lambda i, k: (0, i, 0))],
            out_specs=pl.BlockSpec((tr, cols), lambda i, k: (i, 0))),
        out_shape=_sds((rh, cols), F32), compiler_params=_cp(("arbitrary",)),
    )(kidx, p, r)


def pair_share(halves):
    n = len(halves)

    def body(*refs):
        ins, outs = refs[:n], refs[n:2 * n]
        ss, rs, loc = refs[2 * n:]
        x, y, c = _coords()
        cps = []
        for a in range(n):
            rh = ins[a].shape[0]
            mine = outs[a].at[pl.ds(c * rh, rh)]
            lc = pltpu.make_async_copy(ins[a], mine, loc.at[a])
            lc.start()
            cp = pltpu.make_async_remote_copy(
                src_ref=ins[a], dst_ref=mine, send_sem=ss.at[a], recv_sem=rs.at[a],
                device_id=(x, y, 1 - c), device_id_type=MESH)
            cp.start()
            cps.append((lc, cp))
        for a, (lc, cp) in enumerate(cps):
            rh = ins[a].shape[0]
            theirs = outs[a].at[pl.ds((1 - c) * rh, rh)]
            cp.wait_send()
            pltpu.make_async_remote_copy(
                src_ref=ins[a], dst_ref=theirs, send_sem=ss.at[a], recv_sem=rs.at[a],
                device_id=(x, y, 1 - c), device_id_type=MESH).wait_recv()
            lc.wait()

    return _call(
        body, name="pair_share", in_specs=[ANY] * n, out_specs=[ANY] * n,
        out_shape=[_sds((2 * h.shape[0], h.shape[1]), F32) for h in halves],
        scratch_shapes=[pltpu.SemaphoreType.DMA((n,))] * 3,
    )(*halves)


def small_allreduce_adamw(part, w, m, v):
    n = part.shape[1]

    def body(p_ref, w_ref, m_ref, v_ref, g_ref, d_ref, mo_ref, vo_ref, mine, gath, ss, rs):
        x, y, c = _coords()
        me = 4 * x + 2 * y + c
        mine[...] = jnp.sum(p_ref[...], axis=0, keepdims=True)
        gath[me] = mine[...]
        cps = []
        for k in range(1, 8):
            px, py, pc = x ^ (k >> 2), y ^ ((k >> 1) & 1), c ^ (k & 1)
            cp = pltpu.make_async_remote_copy(
                src_ref=mine, dst_ref=gath.at[me], send_sem=ss.at[k - 1], recv_sem=rs.at[k - 1],
                device_id=(px, py, pc), device_id_type=MESH)
            cp.start()
            cps.append(cp)
        for k in range(1, 8):
            src = 4 * (x ^ (k >> 2)) + 2 * (y ^ ((k >> 1) & 1)) + (c ^ (k & 1))
            pltpu.make_async_remote_copy(
                src_ref=mine, dst_ref=gath.at[src], send_sem=ss.at[k - 1], recv_sem=rs.at[k - 1],
                device_id=(x, y, c), device_id_type=MESH).wait_recv()
        for cp in cps:
            cp.wait_send()
        g = gath[0]
        for dv in range(1, 8):
            g = g + gath[dv]
        g_ref[...] = g
        d_ref[...], mo_ref[...], vo_ref[...] = _adamw_math(w_ref[...], g, m_ref[...], v_ref[...])

    vm = pl.BlockSpec(memory_space=pltpu.VMEM)
    return _call(
        body, name="small_allreduce_adamw", in_specs=[vm] * 4, out_specs=[vm] * 4, out_shape=[_sds((1, n), F32)] * 4,
        scratch_shapes=[pltpu.VMEM((1, n), F32), pltpu.VMEM((8, 1, n), F32),
                        pltpu.SemaphoreType.DMA((7,)), pltpu.SemaphoreType.DMA((7,))],
    )(part, w, m, v)


def _unshard_cols(g):
    k, r, cs = g.shape
    return g.transpose(1, 0, 2).reshape(r, k * cs)


def _shard_cols(w):
    r, c = w.shape
    return w.reshape(r, N_CHIP, c // N_CHIP).transpose(1, 0, 2)


def local_step(x, positions, ln_in_g, ln_in_b, win_g, g_cq, wuq_g, g_ckv, wuk_g, wuv_g, convw_g, conv_b, g_conv_ln,
               b_conv_ln, wout_g, g_ln1, b_ln1, wff1_g, wff2_g, g_ln2, b_ln2, target):
    s, d = x.shape
    c = d - MLA_W
    row = lambda a: a.reshape(1, -1)

    win = _unshard_cols(win_g)
    o_kr = R_Q + R_KV
    o_cv = o_kr + D_ROPE
    win_r = jnp.concatenate([win[:, :o_kr], win[:, o_cv:], win[:, o_kr:o_cv], jnp.zeros((d, LANE - D_ROPE), BF)], axis=1)
    hc = win_r.shape[1]
    kr_blk = (o_kr + 2 * c) // LANE
    wuq = _unshard_cols(wuq_g).reshape(R_Q, HEADS, D_QK)
    wuq_r = jnp.concatenate([wuq[:, :, :D_NOPE].reshape(R_Q, MLA_W),
                             jnp.pad(wuq[:, :, D_NOPE:], ((0, 0), (0, 0), (0, LANE - D_ROPE))).reshape(R_Q, MLA_W)], axis=1)
    wuk = _unshard_cols(wuk_g)
    wuv = _unshard_cols(wuv_g)
    conv_w = jnp.pad(_unshard_cols(convw_g), ((0, 1), (0, 0)))
    wout = wout_g.reshape(d, d)
    wff2 = wff2_g.reshape(-1, d)
    wff1t = wff1_g.transpose(0, 2, 1).reshape(-1, d)
    wff2t = wff2.T

    half = D_ROPE // 2
    inv_freq = ROPE_BASE ** (-jnp.arange(half, dtype=F32) * (2.0 / D_ROPE))
    invf = jnp.concatenate([inv_freq, inv_freq, jnp.zeros((LANE - D_ROPE,), F32)]).reshape(1, LANE)
    cos, sin = rope_tables(positions.astype(F32).reshape(s, 1), invf)
    x0, x0b = ln_in_fwd(x, row(ln_in_g), row(ln_in_b))
    h = matmul("in_proj", x0b, win_r, 512, 640)
    qc, cqn = q_proj(h, g_cq, wuq_r, cos, sin)
    kc, v, ckvn = kv_proj(h, g_ckv, wuk, wuv, cos, sin, kr_blk)
    o, ob, lse = attn_fwd(qc, kc, v)
    co, uc = conv_fwd(h, conv_w, conv_b, g_conv_ln, b_conv_ln)
    r1, x1, x1b = out_proj_ln1(ob, co, wout, x0, g_ln1, b_ln1)
    rb, a1b = ff1_fwd(x1b, wff1_g)
    dr2, dr2b, loss8, dg2, db2 = ff2_ln2_loss(a1b, wff2, x1, target, g_ln2, b_ln2)

    df1b = ff2_bwd_act(dr2b, wff2t, rb)
    gw_ff2 = wgrad("wgrad_ff2", a1b, dr2b, 1024, 1024).reshape(N_CHIP, -1, d)
    gw_ff1 = wgrad("wgrad_ff1", x1b, df1b, 1024, 1024, shards=N_CHIP)
    dr1, dr1b, dg1, db1 = ff1_bwd_ln1(df1b, wff1t, dr2, r1, g_ln1)
    gw_out = jnp.concatenate([wgrad("wgrad_out_attn", ob, dr1b, 1024, 1024)[0],
                              wgrad("wgrad_out_conv", co, dr1b, 1024, 1024)[0]], axis=0).reshape(N_CHIP, -1, d)
    dob, dco, delta = out_proj_bwd(dr1b, wout.T, o)
    duc, dgc, dbc, dcb = conv_bwd_ln(uc, dco, g_conv_ln, b_conv_ln)
    dconv, gconvw = conv_bwd_taps(h, duc, conv_w)
    dqt, dk, dv = attn_bwd(qc, kc, kc.transpose(0, 2, 1), v, dob, lse.reshape(HEADS, 1, s), delta.reshape(HEADS, 1, s))
    dqb, dcq, dgq = q_bwd(dqt, h, g_cq, wuq_r.T, cos, sin)
    dknb, dvb, dckv, dkr, dgkv = kv_bwd(dk, dv, h, g_ckv, wuk.T, wuv.T, cos, sin)
    gwuq_r = wgrad("wgrad_uq", cqn, dqb, 512, 1024)[0]
    gw_uk = wgrad("wgrad_uk", ckvn, dknb, 512, 1024, shards=N_CHIP)
    gw_uv = wgrad("wgrad_uv", ckvn, dvb, 512, 1024, shards=N_CHIP)
    dh = jnp.concatenate([dcq, dckv, dconv, dkr], axis=1)
    gx, dgin, dbin = in_proj_bwd_ln(dh, win_r.T, dr1, x, row(ln_in_g))
    gwin_r = wgrad("wgrad_in", x0b, dh, 1024, 640)[0]

    gwin = jnp.concatenate([gwin_r[:, :o_kr], gwin_r[:, o_kr + 2 * c:o_kr + 2 * c + D_ROPE], gwin_r[:, o_kr:o_kr + 2 * c]], axis=1)
    gwuq = jnp.concatenate([gwuq_r[:, :MLA_W].reshape(R_Q, HEADS, D_NOPE),
                            gwuq_r[:, MLA_W:].reshape(R_Q, HEADS, LANE)[:, :, :D_ROPE]], axis=2).reshape(R_Q, HEADS * D_QK)
    big = dict(w_in=_shard_cols(gwin), w_uq=_shard_cols(gwuq), w_uk=gw_uk, w_uv=gw_uv,
               conv_w=_shard_cols(jnp.pad(gconvw, ((0, 0), (0, 0)))), w_out=gw_out, w_ff1=gw_ff1, w_ff2=gw_ff2)
    small = jnp.concatenate([dgin, dbin, dgq, dgkv, dcb, dgc, dbc, dg1, db1, dg2, db2, loss8], axis=1)
    return gx, big, small


BIG = ["w_in", "w_uq", "w_uk", "w_uv", "conv_w", "w_out", "w_ff1", "w_ff2"]
SMALL = ["ln_in_g", "ln_in_b", "g_cq", "g_ckv", "conv_b", "g_conv_ln", "b_conv_ln", "g_ln1", "b_ln1", "g_ln2", "b_ln2"]
WEIGHTS = ["ln_in_g", "ln_in_b", "w_in", "g_cq", "w_uq", "g_ckv", "w_uk", "w_uv", "conv_w", "conv_b", "g_conv_ln",
           "b_conv_ln", "w_out", "g_ln1", "b_ln1", "w_ff1", "w_ff2", "g_ln2", "b_ln2"]


def _pad_rows(a, rows):
    return jnp.pad(a, ((0, rows - a.shape[0]), (0, 0)))


def kernel(x, positions, ln_in_g, ln_in_b, w_in, g_cq, w_uq, g_ckv, w_uk, w_uv, conv_w, conv_b, g_conv_ln, b_conv_ln, w_out, g_ln1, b_ln1, w_ff1, w_ff2, g_ln2, b_ln2, loss_target, m_ln_in_g, m_ln_in_b, m_w_in, m_g_cq, m_w_uq, m_g_ckv, m_w_uk, m_w_uv, m_conv_w, m_conv_b, m_g_conv_ln, m_b_conv_ln, m_w_out, m_g_ln1, m_b_ln1, m_w_ff1, m_w_ff2, m_g_ln2, m_b_ln2, v_ln_in_g, v_ln_in_b, v_w_in, v_g_cq, v_w_uq, v_g_ckv, v_w_uk, v_w_uv, v_conv_w, v_conv_b, v_g_conv_ln, v_b_conv_ln, v_w_out, v_g_ln1, v_b_ln1, v_w_ff1, v_w_ff2, v_g_ln2, v_b_ln2):
    w = dict(ln_in_g=ln_in_g, ln_in_b=ln_in_b, w_in=w_in, g_cq=g_cq, w_uq=w_uq, g_ckv=g_ckv, w_uk=w_uk, w_uv=w_uv,
             conv_w=conv_w, conv_b=conv_b, g_conv_ln=g_conv_ln, b_conv_ln=b_conv_ln, w_out=w_out, g_ln1=g_ln1,
             b_ln1=b_ln1, w_ff1=w_ff1, w_ff2=w_ff2, g_ln2=g_ln2, b_ln2=b_ln2)
    m = dict(ln_in_g=m_ln_in_g, ln_in_b=m_ln_in_b, w_in=m_w_in, g_cq=m_g_cq, w_uq=m_w_uq, g_ckv=m_g_ckv, w_uk=m_w_uk,
             w_uv=m_w_uv, conv_w=m_conv_w, conv_b=m_conv_b, g_conv_ln=m_g_conv_ln, b_conv_ln=m_b_conv_ln, w_out=m_w_out,
             g_ln1=m_g_ln1, b_ln1=m_b_ln1, w_ff1=m_w_ff1, w_ff2=m_w_ff2, g_ln2=m_g_ln2, b_ln2=m_b_ln2)
    v = dict(ln_in_g=v_ln_in_g, ln_in_b=v_ln_in_b, w_in=v_w_in, g_cq=v_g_cq, w_uq=v_w_uq, g_ckv=v_g_ckv, w_uk=v_w_uk,
             w_uv=v_w_uv, conv_w=v_conv_w, conv_b=v_conv_b, g_conv_ln=v_g_conv_ln, b_conv_ln=v_b_conv_ln, w_out=v_w_out,
             g_ln1=v_g_ln1, b_ln1=v_b_ln1, w_ff1=v_w_ff1, w_ff2=v_w_ff2, g_ln2=v_g_ln2, b_ln2=v_b_ln2)

    sh2 = {n: w[n][0] for n in BIG}
    send = [sh2[n].astype(BF) if n != "conv_w" else _pad_rows(sh2[n], CONV_K + 1) for n in BIG]
    gathered = all_gather_shards(send)
    gw = dict(zip(BIG, gathered))
    gw["conv_w"] = gw["conv_w"][:, :CONV_K]

    gx, big, small = local_step(
        x[0], positions[0], ln_in_g, ln_in_b, gw["w_in"], g_cq, gw["w_uq"], g_ckv, gw["w_uk"], gw["w_uv"], gw["conv_w"],
        conv_b, g_conv_ln, b_conv_ln, gw["w_out"], g_ln1, b_ln1, gw["w_ff1"], gw["w_ff2"], g_ln2, b_ln2, loss_target[0])

    cidx = lax.axis_index("c").astype(jnp.int32).reshape(1)
    kidx = (2 * lax.axis_index("x") + lax.axis_index("y")).astype(jnp.int32).reshape(1)
    full = [big[n] for n in BIG]
    recv = pair_exchange(full)
    psum = [pair_add(g, r, cidx) for g, r in zip(full, recv)]
    got = chip_exchange(psum)
    halves = [chip_add(p, r, kidx) for p, r in zip(psum, got)]
    gsh = dict(zip(BIG, pair_share(halves)))
    gsh["conv_w"] = gsh["conv_w"][:CONV_K]

    grad, delta, new_m, new_v = {}, {}, {}, {}
    for n in BIG:
        grad[n] = gsh[n][None]
        d_, m_, v_ = adamw("adamw_" + n, sh2[n], gsh[n], m[n][0], v[n][0])
        delta[n], new_m[n], new_v[n] = d_[None], m_[None], v_[None]

    flat = lambda t: jnp.concatenate([t[n].reshape(1, -1) for n in SMALL] + [jnp.zeros((1, LANE), F32)], axis=1)
    g_s, d_s, m_s, v_s = small_allreduce_adamw(small, flat(w), flat(m), flat(v))
    off = 0
    for n in SMALL:
        sz = w[n].size
        for dst, src in ((grad, g_s), (delta, d_s), (new_m, m_s), (new_v, v_s)):
            dst[n] = src[0, off:off + sz].reshape(w[n].shape)
        off += sz
    loss = jnp.sum(g_s[0, off:off + LANE])

    return (loss, gx[None], *[grad[n] for n in WEIGHTS], *[delta[n] for n in WEIGHTS],
            *[new_m[n] for n in WEIGHTS], *[new_v[n] for n in WEIGHTS])
```

```python
import functools

import jax
import jax.numpy as jnp
from jax import lax
from jax.experimental import pallas as pl
from jax.experimental.pallas import tpu as pltpu

F32 = jnp.float32
BF = jnp.bfloat16

HEADS = 8
D_NOPE = 128
D_ROPE = 64
D_V = 128
D_QK = D_NOPE + D_ROPE
R_Q = 512
R_KV = 512
MLA_W = HEADS * D_V
CONV_K = 31
CONV_PAD = CONV_K // 2
ROPE_BASE = 10000.0
LOG2E = 1.4426950408889634
LN2 = 0.6931471805599453
LN_EPS = 1e-5
RMS_EPS = 1e-6
ALPHA = (2.0 * 1) ** 0.25
ADAM_LR = 0.001
ADAM_B1 = 0.9
ADAM_B2 = 0.999
ADAM_EPS = 1e-08
ADAM_WD = 0.01
ADAM_STEP = 10

LANE = 128
SUB = 8
HALO = 16
N_CHIP = 4
MESH = pl.DeviceIdType.MESH
VMEM_MB = 1024 * 1024


def _call(body, **kw):
    return pl.pallas_call(body, **kw)


def _cp(sem, mb=48):
    return pltpu.CompilerParams(dimension_semantics=sem, vmem_limit_bytes=mb * VMEM_MB)


def _sds(shape, dt):
    return jax.ShapeDtypeStruct(shape, dt)


def _dot(a, b):
    return jnp.dot(a, b, preferred_element_type=F32)


def _dot_nt(a, b):
    return lax.dot_general(a, b, (((1,), (1,)), ((), ())), preferred_element_type=F32)


def _dot_tn(a, b):
    return lax.dot_general(a, b, (((0,), (0,)), ((), ())), preferred_element_type=F32)


def _rows8(v):
    t, n = v.shape
    return v.reshape(t // SUB, SUB, n).sum(axis=0)


def _ln_stats(r):
    mu = jnp.mean(r, axis=-1, keepdims=True)
    xc = r - mu
    var = jnp.mean(xc * xc, axis=-1, keepdims=True)
    rstd = lax.rsqrt(var + LN_EPS)
    return xc * rstd, rstd


def _ln_bwd(dy, xhat, rstd, g):
    dyh = dy * g
    m1 = jnp.mean(dyh, axis=-1, keepdims=True)
    m2 = jnp.mean(dyh * xhat, axis=-1, keepdims=True)
    return rstd * (dyh - m1 - xhat * m2)


def _rms_fwd(x, g):
    rr = lax.rsqrt(jnp.mean(x * x, axis=-1, keepdims=True) + RMS_EPS)
    xh = x * rr
    return xh * g, xh, rr


def _rms_bwd(dy, xh, rr, g):
    dyg = dy * g
    return rr * (dyg - xh * jnp.mean(dyg * xh, axis=-1, keepdims=True))


def _rope128(x, cos, sin_signed):
    lane = lax.broadcasted_iota(jnp.int32, x.shape, 1)
    rot = jnp.where(lane < D_ROPE // 2, pltpu.roll(x, LANE - D_ROPE // 2, 1), pltpu.roll(x, D_ROPE // 2, 1))
    return x * cos + rot * sin_signed


def _unrope128(dy, cos, sin_signed):
    t = dy * sin_signed
    lane = lax.broadcasted_iota(jnp.int32, dy.shape, 1)
    rot = jnp.where(lane < D_ROPE // 2, pltpu.roll(t, LANE - D_ROPE // 2, 1), pltpu.roll(t, D_ROPE // 2, 1))
    return dy * cos + rot


def _sigmoid(x):
    return 1.0 / (1.0 + jnp.exp(-x))


def _row_chunks(tm, fn, rc=128):
    rc = min(rc, tm)

    def step(ci, carry):
        fn(pl.ds(pl.multiple_of(ci * rc, rc), rc))
        return carry

    lax.fori_loop(0, tm // rc, step, 0)


def _unrolled_loop(n, unroll, fn, init):
    unroll = min(n, unroll)
    assert n % unroll == 0

    def body(t, carry):
        for u in range(unroll):
            carry = fn(t * unroll + u, carry)
        return carry

    return lax.fori_loop(0, n // unroll, body, init)


def _tile(s, want):
    t = min(s, want)
    assert s % t == 0
    return t


def rope_tables(pos_f, invf):
    s = pos_f.shape[0]
    tm = _tile(s, 1024)

    def body(p_ref, f_ref, c_ref, s_ref):
        ang = p_ref[...] * f_ref[...]
        lane = lax.broadcasted_iota(jnp.int32, ang.shape, 1)
        c = jnp.cos(ang)
        sn = jnp.sin(ang)
        c_ref[...] = jnp.where(lane < D_ROPE, c, 0.0)
        s_ref[...] = jnp.where(lane < D_ROPE // 2, -sn, jnp.where(lane < D_ROPE, sn, 0.0))

    return _call(
        body, name="rope_tables", grid=(s // tm,),
        in_specs=[pl.BlockSpec((tm, 1), lambda i: (i, 0)), pl.BlockSpec((1, LANE), lambda i: (0, 0))],
        out_specs=[pl.BlockSpec((tm, LANE), lambda i: (i, 0))] * 2,
        out_shape=[_sds((s, LANE), F32)] * 2,
        compiler_params=_cp(("arbitrary",)),
    )(pos_f, invf)


def ln_in_fwd(x, g, b):
    s, d = x.shape
    tm = _tile(s, 512)

    def body(x_ref, g_ref, b_ref, o_ref, ob_ref):
        xhat, _ = _ln_stats(x_ref[...])
        y = xhat * g_ref[...] + b_ref[...]
        o_ref[...] = y
        ob_ref[...] = y.astype(BF)

    row = pl.BlockSpec((1, d), lambda i: (0, 0))
    tok = pl.BlockSpec((tm, d), lambda i: (i, 0))
    return _call(
        body, name="ln_in_fwd", grid=(s // tm,), in_specs=[tok, row, row], out_specs=[tok, tok],
        out_shape=[_sds((s, d), F32), _sds((s, d), BF)], compiler_params=_cp(("arbitrary",)),
    )(x, g, b)


def matmul(name, a, w, tm, tn, out_dtype=F32):
    s, k = a.shape
    n = w.shape[1]
    tm = _tile(s, tm)
    tn = _tile(n, tn)

    def body(a_ref, w_ref, o_ref):
        o_ref[...] = _dot(a_ref[...], w_ref[...]).astype(o_ref.dtype)

    return _call(
        body, name=name, grid=(s // tm, n // tn),
        in_specs=[pl.BlockSpec((tm, k), lambda i, j: (i, 0)), pl.BlockSpec((k, tn), lambda i, j: (0, j))],
        out_specs=pl.BlockSpec((tm, tn), lambda i, j: (i, j)),
        out_shape=_sds((s, n), out_dtype), compiler_params=_cp(("arbitrary", "arbitrary")),
    )(a, w)


def q_proj(h, g_cq, wuq, cos, sin):
    s = h.shape[0]
    tm = _tile(s, 512)

    def body(h_ref, g_ref, w_ref, c_ref, s_ref, q_ref, n_ref):
        y, _, _ = _rms_fwd(h_ref[...], g_ref[...])
        yb = y.astype(BF)
        n_ref[...] = yb
        q = _dot(yb, w_ref[...])
        c = c_ref[...]
        sn = s_ref[...]
        for hd in range(HEADS):
            q_ref[hd, :, 0:LANE] = q[:, LANE * hd:LANE * (hd + 1)].astype(BF)
            qr = q[:, MLA_W + LANE * hd:MLA_W + LANE * (hd + 1)]
            q_ref[hd, :, LANE:2 * LANE] = _rope128(qr, c, sn).astype(BF)

    return _call(
        body, name="q_proj", grid=(s // tm,),
        in_specs=[pl.BlockSpec((tm, R_Q), lambda i: (i, 0)), pl.BlockSpec((1, R_Q), lambda i: (0, 0)),
                  pl.BlockSpec((R_Q, 2 * MLA_W), lambda i: (0, 0)),
                  pl.BlockSpec((tm, LANE), lambda i: (i, 0)), pl.BlockSpec((tm, LANE), lambda i: (i, 0))],
        out_specs=[pl.BlockSpec((HEADS, tm, 2 * LANE), lambda i: (0, i, 0)), pl.BlockSpec((tm, R_Q), lambda i: (i, 0))],
        out_shape=[_sds((HEADS, s, 2 * LANE), BF), _sds((s, R_Q), BF)], compiler_params=_cp(("arbitrary",)),
    )(h, g_cq, wuq, cos, sin)


def kv_proj(h, g_ckv, wuk, wuv, cos, sin, kr_blk):
    s = h.shape[0]
    tm = _tile(s, 512)

    def body(h_ref, kr_ref, g_ref, wk_ref, wv_ref, c_ref, s_ref, k_ref, v_ref, n_ref):
        y, _, _ = _rms_fwd(h_ref[...], g_ref[...])
        yb = y.astype(BF)
        n_ref[...] = yb
        kn = _dot(yb, wk_ref[...])
        v = _dot(yb, wv_ref[...])
        kr = _rope128(kr_ref[...], c_ref[...], s_ref[...]).astype(BF)
        for hd in range(HEADS):
            k_ref[hd, :, 0:LANE] = kn[:, LANE * hd:LANE * (hd + 1)].astype(BF)
            k_ref[hd, :, LANE:2 * LANE] = kr
            v_ref[hd] = v[:, LANE * hd:LANE * (hd + 1)].astype(BF)

    tab = pl.BlockSpec((tm, LANE), lambda i: (i, 0))
    wsp = pl.BlockSpec((R_KV, MLA_W), lambda i: (0, 0))
    return _call(
        body, name="kv_proj", grid=(s // tm,),
        in_specs=[pl.BlockSpec((tm, R_KV), lambda i: (i, 1)), pl.BlockSpec((tm, LANE), lambda i: (i, kr_blk)),
                  pl.BlockSpec((1, R_KV), lambda i: (0, 0)), wsp, wsp, tab, tab],
        out_specs=[pl.BlockSpec((HEADS, tm, 2 * LANE), lambda i: (0, i, 0)),
                   pl.BlockSpec((HEADS, tm, LANE), lambda i: (0, i, 0)), pl.BlockSpec((tm, R_KV), lambda i: (i, 0))],
        out_shape=[_sds((HEADS, s, 2 * LANE), BF), _sds((HEADS, s, LANE), BF), _sds((s, R_KV), BF)],
        compiler_params=_cp(("arbitrary",)),
    )(h, h, g_ckv, wuk, wuv, cos, sin)


def attn_fwd(qc, kc, v):
    _, s, _ = qc.shape
    tq = _tile(s, 256)
    tk = _tile(s, 512)
    scale = D_QK ** -0.5
    c2 = scale * LOG2E
    nk = s // tk
    nb = tk // LANE
    un = 4

    def body(q_ref, k_ref, v_ref, o_ref, ob_ref, l_ref, s_scr):
        q = q_ref[...]

        def scores(j, mpart):
            off = pl.multiple_of(j * tk, tk)
            sc = _dot_nt(q, k_ref[pl.ds(off, tk), :])
            s_scr[:, pl.ds(off, tk)] = sc
            for b in range(nb):
                mpart = jnp.maximum(mpart, sc[:, LANE * b:LANE * (b + 1)])
            return mpart

        mpart = _unrolled_loop(nk, un, scores, jnp.full((tq, LANE), -jnp.inf, F32))
        m = jnp.max(mpart, axis=-1, keepdims=True)
        mb = jnp.broadcast_to(m * c2, (tq, LANE))

        def weigh(j, carry):
            lpart, acc = carry
            off = pl.multiple_of(j * tk, tk)
            ps = []
            for b in range(nb):
                p = jnp.exp2(s_scr[:, pl.ds(off + LANE * b, LANE)] * c2 - mb)
                lpart = lpart + p
                ps.append(p.astype(BF))
            acc = acc + _dot(jnp.concatenate(ps, axis=1), v_ref[pl.ds(off, tk), :])
            return lpart, acc

        lpart, acc = _unrolled_loop(nk, un, weigh, (jnp.zeros((tq, LANE), F32), jnp.zeros((tq, D_V), F32)))
        l = jnp.sum(lpart, axis=-1, keepdims=True)
        o = acc / l
        o_ref[...] = o
        ob_ref[...] = o.astype(BF)
        l_ref[...] = m * c2 + jnp.log(l) * LOG2E

    return _call(
        body, name="attn_fwd", grid=(HEADS, s // tq),
        in_specs=[pl.BlockSpec((None, tq, 2 * LANE), lambda h, i: (h, i, 0)),
                  pl.BlockSpec((None, s, 2 * LANE), lambda h, i: (h, 0, 0)),
                  pl.BlockSpec((None, s, LANE), lambda h, i: (h, 0, 0))],
        out_specs=[pl.BlockSpec((tq, LANE), lambda h, i: (i, h)), pl.BlockSpec((tq, LANE), lambda h, i: (i, h)),
                   pl.BlockSpec((None, tq, 1), lambda h, i: (h, i, 0))],
        out_shape=[_sds((s, MLA_W), F32), _sds((s, MLA_W), BF), _sds((HEADS, s, 1), F32)],
        scratch_shapes=[pltpu.VMEM((tq, s), F32)], compiler_params=_cp(("arbitrary", "arbitrary")),
    )(qc, kc, v)


def _halo_specs(tm, s, width, col):
    r = tm // HALO
    nb = s // HALO
    cur = pl.BlockSpec((tm, width), lambda i: (i, col))
    prev = pl.BlockSpec((HALO, width), lambda i: (jnp.maximum(i * r - 1, 0), col))
    nxt = pl.BlockSpec((HALO, width), lambda i: (jnp.minimum((i + 1) * r, nb - 1), col))
    return cur, prev, nxt


def _fill_slab(slab, tm, prev, cur, nxt):
    i = pl.program_id(0)
    last = pl.num_programs(0) - 1
    slab[0:HALO, :] = jnp.where(i > 0, prev, 0.0)
    slab[HALO:HALO + tm, :] = cur
    slab[HALO + tm:2 * HALO + tm, :] = jnp.where(i < last, nxt, 0.0)


def conv_fwd(h, conv_w, conv_b, g_ln, b_ln):
    s = h.shape[0]
    c = conv_w.shape[1]
    tm = _tile(s, 256)
    rc = _tile(tm, 128)

    def body(a_ref, ap_ref, an_ref, g_ref, gp_ref, gn_ref, w_ref, cb_ref, lg_ref, lb_ref, co_ref, uc_ref, slab):
        _fill_slab(slab, tm, ap_ref[...] * _sigmoid(gp_ref[...]), a_ref[...] * _sigmoid(g_ref[...]),
                   an_ref[...] * _sigmoid(gn_ref[...]))
        for cb in range(c // LANE):
            cs = slice(LANE * cb, LANE * (cb + 1))
            for r0 in range(0, tm, rc):
                acc = jnp.zeros((rc, LANE), F32)
                for k in range(CONV_K):
                    acc = acc + w_ref[k:k + 1, cs] * slab[pl.ds(r0 + HALO - CONV_PAD + k, rc), cs]
                uc_ref[r0:r0 + rc, cs] = acc + cb_ref[:, cs]
        xhat, _ = _ln_stats(uc_ref[...])
        cl = xhat * lg_ref[...] + lb_ref[...]
        co_ref[...] = (cl * _sigmoid(cl)).astype(BF)

    a_specs = _halo_specs(tm, s, c, 1)
    g_specs = _halo_specs(tm, s, c, 2)
    row = pl.BlockSpec((1, c), lambda i: (0, 0))
    tok = pl.BlockSpec((tm, c), lambda i: (i, 0))
    return _call(
        body, name="conv_fwd", grid=(s // tm,),
        in_specs=[*a_specs, *g_specs, pl.BlockSpec(conv_w.shape, lambda i: (0, 0)), row, row, row],
        out_specs=[tok, tok], out_shape=[_sds((s, c), BF), _sds((s, c), F32)],
        scratch_shapes=[pltpu.VMEM((tm + 2 * HALO, c), F32)], compiler_params=_cp(("arbitrary",)),
    )(h, h, h, h, h, h, conv_w, conv_b, g_ln, b_ln)


def out_proj_ln1(ob, co, wout, x0, g1, b1):
    s, d = x0.shape
    kh = ob.shape[1]
    tm = _tile(s, 256)

    def body(o_ref, c_ref, w_ref, x_ref, g_ref, b_ref, r_ref, x1_ref, x1b_ref, acc):
        k = pl.program_id(1)

        @pl.when(k == 0)
        def _():
            acc[...] = _dot(o_ref[...], w_ref[...])

        @pl.when(k == 1)
        def _():
            r = ALPHA * x_ref[...] + (acc[...] + _dot(c_ref[...], w_ref[...]))
            r_ref[...] = r
            xhat, _ = _ln_stats(r)
            y = xhat * g_ref[...] + b_ref[...]
            x1_ref[...] = y
            x1b_ref[...] = y.astype(BF)

    half = pl.BlockSpec((tm, kh), lambda i, k: (i, 0))
    tok = pl.BlockSpec((tm, d), lambda i, k: (i, 0))
    row = pl.BlockSpec((1, d), lambda i, k: (0, 0))
    return _call(
        body, name="out_proj_ln1", grid=(s // tm, 2),
        in_specs=[half, half, pl.BlockSpec((kh, d), lambda i, k: (k, 0)), tok, row, row],
        out_specs=[tok, tok, tok], out_shape=[_sds((s, d), F32), _sds((s, d), F32), _sds((s, d), BF)],
        scratch_shapes=[pltpu.VMEM((tm, d), F32)], compiler_params=_cp(("arbitrary", "arbitrary")),
    )(ob, co, wout, x0, g1, b1)


def ff1_fwd(x1b, wff1_g):
    s, d = x1b.shape
    nsh, _, fs = wff1_g.shape
    tm = _tile(s, 512)
    tn = _tile(fs, 1024)
    per = fs // tn

    def body(a_ref, w_ref, r_ref, a1_ref):
        r = jnp.maximum(_dot(a_ref[...], w_ref[...]), 0.0)
        r_ref[...] = r.astype(BF)
        a1_ref[...] = (r * r).astype(BF)

    out = pl.BlockSpec((tm, tn), lambda i, j: (i, j))
    return _call(
        body, name="ff1_fwd", grid=(s // tm, nsh * per),
        in_specs=[pl.BlockSpec((tm, d), lambda i, j: (i, 0)),
                  pl.BlockSpec((None, d, tn), lambda i, j: (j // per, 0, j % per))],
        out_specs=[out, out], out_shape=[_sds((s, nsh * fs), BF)] * 2,
        compiler_params=_cp(("arbitrary", "arbitrary")),
    )(x1b, wff1_g)


def ff2_ln2_loss(a1b, wff2, x1, target, g2, b2):
    s, f = a1b.shape
    d = x1.shape[1]
    tm = _tile(s, 512)
    tk = _tile(f, 1024)
    nk = f // tk

    def body(a_ref, w_ref, x_ref, t_ref, g_ref, b_ref, dr_ref, drb_ref, loss_ref, dg_ref, db_ref, acc):
        i = pl.program_id(0)
        k = pl.program_id(1)

        @pl.when(k == 0)
        def _():
            acc[...] = _dot(a_ref[...], w_ref[...])

        @pl.when(k > 0)
        def _():
            acc[...] += _dot(a_ref[...], w_ref[...])

        @pl.when(jnp.logical_and(i == 0, k == 0))
        def _():
            loss_ref[...] = jnp.zeros_like(loss_ref)
            dg_ref[...] = jnp.zeros_like(dg_ref)
            db_ref[...] = jnp.zeros_like(db_ref)

        @pl.when(k == nk - 1)
        def _():
            g = g_ref[...]

            def chunk(rows):
                r = ALPHA * x_ref[rows, :] + acc[rows, :]
                xhat, rstd = _ln_stats(r)
                e = xhat * g + b_ref[...] - t_ref[rows, :]
                e2 = _rows8(e * e)
                part = e2[:, 0:LANE]
                for c in range(1, d // LANE):
                    part = part + e2[:, LANE * c:LANE * (c + 1)]
                loss_ref[...] += part * (0.5 / d)
                dy = e * (1.0 / d)
                dg_ref[...] += _rows8(dy * xhat)
                db_ref[...] += _rows8(dy)
                dr = _ln_bwd(dy, xhat, rstd, g)
                dr_ref[rows, :] = dr
                drb_ref[rows, :] = dr.astype(BF)

            _row_chunks(tm, chunk)

    tok = pl.BlockSpec((tm, d), lambda i, k: (i, 0))
    row = pl.BlockSpec((1, d), lambda i, k: (0, 0))
    accs = pl.BlockSpec((SUB, d), lambda i, k: (0, 0))
    return _call(
        body, name="ff2_ln2_loss", grid=(s // tm, nk),
        in_specs=[pl.BlockSpec((tm, tk), lambda i, k: (i, k)), pl.BlockSpec((tk, d), lambda i, k: (k, 0)),
                  tok, tok, row, row],
        out_specs=[tok, tok, pl.BlockSpec((SUB, LANE), lambda i, k: (0, 0)), accs, accs],
        out_shape=[_sds((s, d), F32), _sds((s, d), BF), _sds((SUB, LANE), F32), _sds((SUB, d), F32), _sds((SUB, d), F32)],
        scratch_shapes=[pltpu.VMEM((tm, d), F32)], compiler_params=_cp(("arbitrary", "arbitrary"), 56),
    )(a1b, wff2, x1, target, g2, b2)


def ff2_bwd_act(dr2b, wff2t, rb):
    s, d = dr2b.shape
    f = wff2t.shape[1]
    tm = _tile(s, 512)
    tn = _tile(f, 1024)

    def body(a_ref, w_ref, r_ref, o_ref):
        o_ref[...] = (_dot(a_ref[...], w_ref[...]) * (2.0 * r_ref[...].astype(F32))).astype(BF)

    return _call(
        body, name="ff2_bwd_act", grid=(s // tm, f // tn),
        in_specs=[pl.BlockSpec((tm, d), lambda i, j: (i, 0)), pl.BlockSpec((d, tn), lambda i, j: (0, j)),
                  pl.BlockSpec((tm, tn), lambda i, j: (i, j))],
        out_specs=pl.BlockSpec((tm, tn), lambda i, j: (i, j)), out_shape=_sds((s, f), BF),
        compiler_params=_cp(("arbitrary", "arbitrary")),
    )(dr2b, wff2t, rb)


def wgrad(name, a, b, tm, tn, tk=1024, shards=1):
    s, m = a.shape
    n = b.shape[1]
    tm = _tile(m, tm)
    ns = n // shards
    tn = _tile(ns, tn)
    tk = _tile(s, tk)
    per = ns // tn

    def body(a_ref, b_ref, o_ref):
        k = pl.program_id(2)

        @pl.when(k == 0)
        def _():
            o_ref[...] = _dot_tn(a_ref[...], b_ref[...])

        @pl.when(k > 0)
        def _():
            o_ref[...] += _dot_tn(a_ref[...], b_ref[...])

    return _call(
        body, name=name, grid=(m // tm, n // tn, s // tk),
        in_specs=[pl.BlockSpec((tk, tm), lambda i, j, k: (k, i)), pl.BlockSpec((tk, tn), lambda i, j, k: (k, j))],
        out_specs=pl.BlockSpec((None, tm, tn), lambda i, j, k: (j // per, i, j % per)),
        out_shape=_sds((shards, m, ns), F32), compiler_params=_cp(("arbitrary", "arbitrary", "arbitrary")),
    )(a, b)


def ff1_bwd_ln1(df1b, wff1t, dr2, r1, g1):
    s, f = df1b.shape
    d = dr2.shape[1]
    tm = _tile(s, 512)
    tk = _tile(f, 1024)
    nk = f // tk

    def body(a_ref, w_ref, d2_ref, r_ref, g_ref, dr_ref, drb_ref, dg_ref, db_ref, acc):
        i = pl.program_id(0)
        k = pl.program_id(1)

        @pl.when(k == 0)
        def _():
            acc[...] = _dot(a_ref[...], w_ref[...])

        @pl.when(k > 0)
        def _():
            acc[...] += _dot(a_ref[...], w_ref[...])

        @pl.when(jnp.logical_and(i == 0, k == 0))
        def _():
            dg_ref[...] = jnp.zeros_like(dg_ref)
            db_ref[...] = jnp.zeros_like(db_ref)

        @pl.when(k == nk - 1)
        def _():
            g = g_ref[...]

            def chunk(rows):
                dy = ALPHA * d2_ref[rows, :] + acc[rows, :]
                xhat, rstd = _ln_stats(r_ref[rows, :])
                dg_ref[...] += _rows8(dy * xhat)
                db_ref[...] += _rows8(dy)
                dr = _ln_bwd(dy, xhat, rstd, g)
                dr_ref[rows, :] = dr
                drb_ref[rows, :] = dr.astype(BF)

            _row_chunks(tm, chunk)

    tok = pl.BlockSpec((tm, d), lambda i, k: (i, 0))
    accs = pl.BlockSpec((SUB, d), lambda i, k: (0, 0))
    return _call(
        body, name="ff1_bwd_ln1", grid=(s // tm, nk),
        in_specs=[pl.BlockSpec((tm, tk), lambda i, k: (i, k)), pl.BlockSpec((tk, d), lambda i, k: (k, 0)),
                  tok, tok, pl.BlockSpec((1, d), lambda i, k: (0, 0))],
        out_specs=[tok, tok, accs, accs],
        out_shape=[_sds((s, d), F32), _sds((s, d), BF), _sds((SUB, d), F32), _sds((SUB, d), F32)],
        scratch_shapes=[pltpu.VMEM((tm, d), F32)], compiler_params=_cp(("arbitrary", "arbitrary"), 56),
    )(df1b, wff1t, dr2, r1, g1)


def out_proj_bwd(dr1b, woutt, o):
    s, d = dr1b.shape
    tm = _tile(s, 256)

    def body(a_ref, w_ref, o_ref, do_ref, dc_ref, dl_ref):
        dcat = _dot(a_ref[...], w_ref[...])
        do = dcat[:, 0:MLA_W]
        do_ref[...] = do.astype(BF)
        dc_ref[...] = dcat[:, MLA_W:]
        prod = do * o_ref[...]
        for hd in range(HEADS):
            dl_ref[hd] = jnp.sum(prod[:, LANE * hd:LANE * (hd + 1)], axis=-1, keepdims=True)

    half = pl.BlockSpec((tm, MLA_W), lambda i: (i, 0))
    return _call(
        body, name="out_proj_bwd", grid=(s // tm,),
        in_specs=[pl.BlockSpec((tm, d), lambda i: (i, 0)), pl.BlockSpec((d, d), lambda i: (0, 0)), half],
        out_specs=[half, pl.BlockSpec((tm, d - MLA_W), lambda i: (i, 0)), pl.BlockSpec((HEADS, tm, 1), lambda i: (0, i, 0))],
        out_shape=[_sds((s, MLA_W), BF), _sds((s, d - MLA_W), F32), _sds((HEADS, s, 1), F32)],
        compiler_params=_cp(("arbitrary",)),
    )(dr1b, woutt, o)


def conv_bwd_ln(uc, dco, g_ln, b_ln):
    s, c = uc.shape
    tm = _tile(s, 512)

    def body(u_ref, d_ref, g_ref, b_ref, du_ref, dg_ref, db_ref, dcb_ref):
        @pl.when(pl.program_id(0) == 0)
        def _():
            dg_ref[...] = jnp.zeros_like(dg_ref)
            db_ref[...] = jnp.zeros_like(db_ref)
            dcb_ref[...] = jnp.zeros_like(dcb_ref)

        xhat, rstd = _ln_stats(u_ref[...])
        g = g_ref[...]
        cl = xhat * g + b_ref[...]
        sg = _sigmoid(cl)
        dcl = d_ref[...] * (sg * (1.0 + cl * (1.0 - sg)))
        dg_ref[...] += _rows8(dcl * xhat)
        db_ref[...] += _rows8(dcl)
        du = _ln_bwd(dcl, xhat, rstd, g)
        du_ref[...] = du
        dcb_ref[...] += _rows8(du)

    tok = pl.BlockSpec((tm, c), lambda i: (i, 0))
    row = pl.BlockSpec((1, c), lambda i: (0, 0))
    accs = pl.BlockSpec((SUB, c), lambda i: (0, 0))
    return _call(
        body, name="conv_bwd_ln", grid=(s // tm,), in_specs=[tok, tok, row, row], out_specs=[tok, accs, accs, accs],
        out_shape=[_sds((s, c), F32)] + [_sds((SUB, c), F32)] * 3, compiler_params=_cp(("arbitrary",)),
    )(uc, dco, g_ln, b_ln)


def conv_bwd_taps(h, duc, conv_w):
    s, c = duc.shape
    tm = _tile(s, 256)
    rc = _tile(tm, 128)

    def body(a_ref, ap_ref, an_ref, g_ref, gp_ref, gn_ref, d_ref, dp_ref, dn_ref, w_ref, o_ref, dw_ref, uslab, dslab, du_s):
        @pl.when(pl.program_id(0) == 0)
        def _():
            dw_ref[...] = jnp.zeros_like(dw_ref)

        sg = _sigmoid(g_ref[...])
        a = a_ref[...]
        _fill_slab(uslab, tm, ap_ref[...] * _sigmoid(gp_ref[...]), a * sg, an_ref[...] * _sigmoid(gn_ref[...]))
        _fill_slab(dslab, tm, dp_ref[...], d_ref[...], dn_ref[...])
        for cb in range(c // LANE):
            cs = slice(LANE * cb, LANE * (cb + 1))
            for r0 in range(0, tm, rc):
                acc = jnp.zeros((rc, LANE), F32)
                dcur = dslab[pl.ds(r0 + HALO, rc), cs]
                for k in range(CONV_K):
                    acc = acc + w_ref[k:k + 1, cs] * dslab[pl.ds(r0 + HALO + CONV_PAD - k, rc), cs]
                    dw_ref[k:k + 1, cs] += jnp.sum(dcur * uslab[pl.ds(r0 + HALO - CONV_PAD + k, rc), cs], axis=0, keepdims=True)
                du_s[r0:r0 + rc, cs] = acc
        du = du_s[...]
        o_ref[:, 0:c] = (du * sg).astype(BF)
        o_ref[:, c:2 * c] = (du * a * sg * (1.0 - sg)).astype(BF)

    a_specs = _halo_specs(tm, s, c, 1)
    g_specs = _halo_specs(tm, s, c, 2)
    d_specs = _halo_specs(tm, s, c, 0)
    wsp = pl.BlockSpec(conv_w.shape, lambda i: (0, 0))
    return _call(
        body, name="conv_bwd_taps", grid=(s // tm,), in_specs=[*a_specs, *g_specs, *d_specs, wsp],
        out_specs=[pl.BlockSpec((tm, 2 * c), lambda i: (i, 0)), wsp],
        out_shape=[_sds((s, 2 * c), BF), _sds(conv_w.shape, F32)],
        scratch_shapes=[pltpu.VMEM((tm + 2 * HALO, c), F32), pltpu.VMEM((tm + 2 * HALO, c), F32), pltpu.VMEM((tm, c), F32)],
        compiler_params=_cp(("arbitrary",)),
    )(h, h, h, h, h, h, duc, duc, duc, conv_w)


def attn_bwd(qc, kc, kct, v, dob, lse_r, delta_r):
    _, s, _ = qc.shape
    tk = _tile(s, 512)
    tq = _tile(s, 512)
    scale = D_QK ** -0.5
    c2 = scale * LOG2E

    def body(k_ref, kt_ref, v_ref, q_ref, do_ref, l_ref, dl_ref, dqt_ref, dk_ref, dv_ref):
        @pl.when(pl.program_id(1) == 0)
        def _():
            dqt_ref[...] = jnp.zeros_like(dqt_ref)

        k = k_ref[...]
        kt = kt_ref[...]
        vv = v_ref[...]

        def step(i, carry):
            dk, dv = carry
            off = pl.multiple_of(i * tq, tq)
            q = q_ref[pl.ds(off, tq), :]
            do = do_ref[pl.ds(off, tq), :]
            pt = jnp.exp2(_dot_nt(k, q) * c2 - l_ref[:, pl.ds(off, tq)])
            dv = dv + _dot(pt.astype(BF), do)
            dpt = _dot_nt(vv, do)
            dsb = (pt * (dpt - dl_ref[:, pl.ds(off, tq)]) * scale).astype(BF)
            dk = dk + _dot(dsb, q)
            dqt_ref[:, pl.ds(off, tq)] += _dot(kt, dsb)
            return dk, dv

        dk, dv = _unrolled_loop(s // tq, 2, step, (jnp.zeros((tk, 2 * LANE), F32), jnp.zeros((tk, LANE), F32)))
        dk_ref[...] = dk
        dv_ref[...] = dv

    rowv = pl.BlockSpec((None, 1, s), lambda h, j: (h, 0, 0))
    return _call(
        body, name="attn_bwd", grid=(HEADS, s // tk),
        in_specs=[pl.BlockSpec((None, tk, 2 * LANE), lambda h, j: (h, j, 0)),
                  pl.BlockSpec((None, 2 * LANE, tk), lambda h, j: (h, 0, j)),
                  pl.BlockSpec((None, tk, LANE), lambda h, j: (h, j, 0)),
                  pl.BlockSpec((None, s, 2 * LANE), lambda h, j: (h, 0, 0)),
                  pl.BlockSpec((s, LANE), lambda h, j: (0, h)), rowv, rowv],
        out_specs=[pl.BlockSpec((None, 2 * LANE, s), lambda h, j: (h, 0, 0)),
                   pl.BlockSpec((None, tk, 2 * LANE), lambda h, j: (h, j, 0)),
                   pl.BlockSpec((None, tk, LANE), lambda h, j: (h, j, 0))],
        out_shape=[_sds((HEADS, 2 * LANE, s), F32), _sds((HEADS, s, 2 * LANE), F32), _sds((HEADS, s, LANE), F32)],
        compiler_params=_cp(("arbitrary", "arbitrary"), 56),
    )(kc, kct, v, qc, dob, lse_r, delta_r)


def q_bwd(dqt, h, g_cq, wuqt, cos, sin):
    s = h.shape[0]
    tm = _tile(s, 256)

    def body(d_ref, h_ref, g_ref, w_ref, c_ref, s_ref, dq_ref, dc_ref, dg_ref):
        @pl.when(pl.program_id(0) == 0)
        def _():
            dg_ref[...] = jnp.zeros_like(dg_ref)

        c = c_ref[...]
        sn = s_ref[...]
        for hd in range(HEADS):
            t = d_ref[hd].T
            dq_ref[:, LANE * hd:LANE * (hd + 1)] = t[:, 0:LANE].astype(BF)
            dq_ref[:, MLA_W + LANE * hd:MLA_W + LANE * (hd + 1)] = _unrope128(t[:, LANE:2 * LANE], c, sn).astype(BF)
        dy = _dot(dq_ref[...], w_ref[...])
        g = g_ref[...]
        _, xh, rr = _rms_fwd(h_ref[...], g)
        dg_ref[...] += _rows8(dy * xh)
        dc_ref[...] = _rms_bwd(dy, xh, rr, g).astype(BF)

    tab = pl.BlockSpec((tm, LANE), lambda i: (i, 0))
    return _call(
        body, name="q_bwd", grid=(s // tm,),
        in_specs=[pl.BlockSpec((HEADS, 2 * LANE, tm), lambda i: (0, 0, i)), pl.BlockSpec((tm, R_Q), lambda i: (i, 0)),
                  pl.BlockSpec((1, R_Q), lambda i: (0, 0)), pl.BlockSpec((2 * MLA_W, R_Q), lambda i: (0, 0)), tab, tab],
        out_specs=[pl.BlockSpec((tm, 2 * MLA_W), lambda i: (i, 0)), pl.BlockSpec((tm, R_Q), lambda i: (i, 0)),
                   pl.BlockSpec((SUB, R_Q), lambda i: (0, 0))],
        out_shape=[_sds((s, 2 * MLA_W), BF), _sds((s, R_Q), BF), _sds((SUB, R_Q), F32)],
        compiler_params=_cp(("arbitrary",)),
    )(dqt, h, g_cq, wuqt, cos, sin)


def kv_bwd(dk, dv, h, g_ckv, wukt, wuvt, cos, sin):
    s = h.shape[0]
    tm = _tile(s, 256)

    def body(dk_ref, dv_ref, h_ref, g_ref, wk_ref, wv_ref, c_ref, s_ref, dkn_ref, dvb_ref, dc_ref, dkr_ref, dg_ref):
        @pl.when(pl.program_id(0) == 0)
        def _():
            dg_ref[...] = jnp.zeros_like(dg_ref)

        dkr = dk_ref[0, :, LANE:2 * LANE]
        for hd in range(HEADS):
            dkn_ref[:, LANE * hd:LANE * (hd + 1)] = dk_ref[hd, :, 0:LANE].astype(BF)
            dvb_ref[:, LANE * hd:LANE * (hd + 1)] = dv_ref[hd].astype(BF)
            if hd > 0:
                dkr = dkr + dk_ref[hd, :, LANE:2 * LANE]
        dkr_ref[...] = _unrope128(dkr, c_ref[...], s_ref[...]).astype(BF)
        dy = _dot(dkn_ref[...], wk_ref[...]) + _dot(dvb_ref[...], wv_ref[...])
        g = g_ref[...]
        _, xh, rr = _rms_fwd(h_ref[...], g)
        dg_ref[...] += _rows8(dy * xh)
        dc_ref[...] = _rms_bwd(dy, xh, rr, g).astype(BF)

    tab = pl.BlockSpec((tm, LANE), lambda i: (i, 0))
    wsp = pl.BlockSpec((MLA_W, R_KV), lambda i: (0, 0))
    wide = pl.BlockSpec((tm, MLA_W), lambda i: (i, 0))
    return _call(
        body, name="kv_bwd", grid=(s // tm,),
        in_specs=[pl.BlockSpec((HEADS, tm, 2 * LANE), lambda i: (0, i, 0)), pl.BlockSpec((HEADS, tm, LANE), lambda i: (0, i, 0)),
                  pl.BlockSpec((tm, R_KV), lambda i: (i, 1)), pl.BlockSpec((1, R_KV), lambda i: (0, 0)), wsp, wsp, tab, tab],
        out_specs=[wide, wide, pl.BlockSpec((tm, R_KV), lambda i: (i, 0)), tab, pl.BlockSpec((SUB, R_KV), lambda i: (0, 0))],
        out_shape=[_sds((s, MLA_W), BF), _sds((s, MLA_W), BF), _sds((s, R_KV), BF), _sds((s, LANE), BF), _sds((SUB, R_KV), F32)],
        compiler_params=_cp(("arbitrary",)),
    )(dk, dv, h, g_ckv, wukt, wuvt, cos, sin)


def in_proj_bwd_ln(dh, wint, dr1, x, g_in):
    s, hc = dh.shape
    d = x.shape[1]
    tm = _tile(s, 512)
    tk = _tile(hc, 640)
    nk = hc // tk

    def body(a_ref, w_ref, d1_ref, x_ref, g_ref, gx_ref, dg_ref, db_ref, acc):
        i = pl.program_id(0)
        k = pl.program_id(1)

        @pl.when(k == 0)
        def _():
            acc[...] = _dot(a_ref[...], w_ref[...])

        @pl.when(k > 0)
        def _():
            acc[...] += _dot(a_ref[...], w_ref[...])

        @pl.when(jnp.logical_and(i == 0, k == 0))
        def _():
            dg_ref[...] = jnp.zeros_like(dg_ref)
            db_ref[...] = jnp.zeros_like(db_ref)

        @pl.when(k == nk - 1)
        def _():
            g = g_ref[...]

            def chunk(rows):
                dy = ALPHA * d1_ref[rows, :] + acc[rows, :]
                xhat, rstd = _ln_stats(x_ref[rows, :])
                dg_ref[...] += _rows8(dy * xhat)
                db_ref[...] += _rows8(dy)
                gx_ref[rows, :] = _ln_bwd(dy, xhat, rstd, g)

            _row_chunks(tm, chunk)

    tok = pl.BlockSpec((tm, d), lambda i, k: (i, 0))
    accs = pl.BlockSpec((SUB, d), lambda i, k: (0, 0))
    return _call(
        body, name="in_proj_bwd_ln", grid=(s // tm, nk),
        in_specs=[pl.BlockSpec((tm, tk), lambda i, k: (i, k)), pl.BlockSpec((tk, d), lambda i, k: (k, 0)),
                  tok, tok, pl.BlockSpec((1, d), lambda i, k: (0, 0))],
        out_specs=[tok, accs, accs], out_shape=[_sds((s, d), F32), _sds((SUB, d), F32), _sds((SUB, d), F32)],
        scratch_shapes=[pltpu.VMEM((tm, d), F32)], compiler_params=_cp(("arbitrary", "arbitrary")),
    )(dh, wint, dr1, x, g_in)


def _adamw_math(w, g, m, v):
    m = ADAM_B1 * m + (1.0 - ADAM_B1) * g
    v = ADAM_B2 * v + (1.0 - ADAM_B2) * (g * g)
    m_hat = m / (1.0 - ADAM_B1 ** ADAM_STEP)
    v_hat = v / (1.0 - ADAM_B2 ** ADAM_STEP)
    delta = -ADAM_LR * (m_hat / (jnp.sqrt(v_hat) + ADAM_EPS) + ADAM_WD * w)
    return delta, m, v


def adamw(name, w, g, m, v):
    r, c = w.shape
    tr = _row_tile(r, c)

    def body(w_ref, g_ref, m_ref, v_ref, d_ref, mo_ref, vo_ref):
        d_ref[...], mo_ref[...], vo_ref[...] = _adamw_math(w_ref[...], g_ref[...], m_ref[...], v_ref[...])

    blk = pl.BlockSpec((tr, c), lambda i: (i, 0))
    return _call(
        body, name=name, grid=(r // tr,), in_specs=[blk] * 4, out_specs=[blk] * 3,
        out_shape=[_sds((r, c), F32)] * 3, compiler_params=_cp(("arbitrary",)),
    )(w, g, m, v)


def _coords():
    return lax.axis_index("x"), lax.axis_index("y"), lax.axis_index("c")


def _other_chips(x, y):
    return [(1 - x, y, 2 * (1 - x) + y), (x, 1 - y, 2 * x + 1 - y), (1 - x, 1 - y, 2 * (1 - x) + 1 - y)]


ANY = pl.BlockSpec(memory_space=pl.ANY)


def all_gather_shards(shards):
    n = len(shards)

    def body(*refs):
        ins, outs = refs[:n], refs[n:2 * n]
        ici_s, ici_r, d2d_s, d2d_r = refs[2 * n:]
        x, y, c = _coords()
        me = 2 * x + y
        peers = _other_chips(x, y)
        sends, fwds = [], []
        for a in range(n):
            rh = ins[a].shape[0] // 2
            mine = pl.ds(c * rh, rh)
            for j, (px, py, pk) in enumerate(peers):
                cp = pltpu.make_async_remote_copy(
                    src_ref=ins[a].at[mine], dst_ref=outs[a].at[me, mine], send_sem=ici_s.at[a, j], recv_sem=ici_r.at[a, j],
                    device_id=(px, py, c), device_id_type=MESH)
                cp.start()
                sends.append(cp)
        for a in range(n):
            rh = ins[a].shape[0] // 2
            mine = pl.ds(c * rh, rh)
            for j, (px, py, pk) in enumerate(peers):
                got = outs[a].at[pk, mine]
                pltpu.make_async_remote_copy(
                    src_ref=got, dst_ref=got, send_sem=ici_s.at[a, j], recv_sem=ici_r.at[a, j],
                    device_id=(px, py, c), device_id_type=MESH).wait_recv()
                fw = pltpu.make_async_remote_copy(
                    src_ref=got, dst_ref=got, send_sem=d2d_s.at[a, j], recv_sem=d2d_r.at[a, j],
                    device_id=(x, y, 1 - c), device_id_type=MESH)
                fw.start()
                fwds.append(fw)
        for a in range(n):
            rh = ins[a].shape[0] // 2
            theirs = pl.ds((1 - c) * rh, rh)
            for j, (px, py, pk) in enumerate(peers):
                got = outs[a].at[pk, theirs]
                pltpu.make_async_remote_copy(
                    src_ref=got, dst_ref=got, send_sem=d2d_s.at[a, j], recv_sem=d2d_r.at[a, j],
                    device_id=(x, y, 1 - c), device_id_type=MESH).wait_recv()
        for cp in sends + fwds:
            cp.wait_send()

    got = _call(
        body, name="all_gather_shards", in_specs=[ANY] * n, out_specs=[ANY] * n,
        out_shape=[_sds((N_CHIP,) + w.shape, w.dtype) for w in shards],
        scratch_shapes=[pltpu.SemaphoreType.DMA((n, 3))] * 4,
    )(*shards)
    me = 2 * lax.axis_index("x") + lax.axis_index("y")
    return [lax.dynamic_update_slice(g, w[None], (me, 0, 0)) for g, w in zip(got, shards)]


def pair_exchange(grads):
    n = len(grads)

    def body(*refs):
        ins, outs = refs[:n], refs[n:2 * n]
        ss, rs = refs[2 * n:]
        x, y, c = _coords()
        cps = []
        for a in range(n):
            rh = ins[a].shape[1] // 2
            cp = pltpu.make_async_remote_copy(
                src_ref=ins[a].at[:, pl.ds((1 - c) * rh, rh)], dst_ref=outs[a], send_sem=ss.at[a], recv_sem=rs.at[a],
                device_id=(x, y, 1 - c), device_id_type=MESH)
            cp.start()
            cps.append(cp)
        for cp in cps:
            cp.wait()

    return _call(
        body, name="pair_exchange", in_specs=[ANY] * n, out_specs=[ANY] * n,
        out_shape=[_sds((N_CHIP, g.shape[1] // 2, g.shape[2]), F32) for g in grads],
        scratch_shapes=[pltpu.SemaphoreType.DMA((n,))] * 2,
    )(*grads)


def _row_tile(rows, cols, itemsize=4, budget=2 * VMEM_MB):
    t = rows
    while t * cols * itemsize > budget and t % (2 * SUB) == 0:
        t //= 2
    return t


def pair_add(g, r, cidx):
    _, rows, cols = g.shape
    rh = rows // 2
    tr = _row_tile(rh, cols)
    per = rh // tr

    def body(c_ref, g_ref, r_ref, o_ref):
        o_ref[...] = g_ref[...] + r_ref[...]

    return _call(
        body, name="pair_add",
        grid_spec=pltpu.PrefetchScalarGridSpec(
            num_scalar_prefetch=1, grid=(N_CHIP, per),
            in_specs=[pl.BlockSpec((None, tr, cols), lambda k, i, c: (k, c[0] * per + i, 0)),
                      pl.BlockSpec((None, tr, cols), lambda k, i, c: (k, i, 0))],
            out_specs=pl.BlockSpec((None, tr, cols), lambda k, i, c: (k, i, 0))),
        out_shape=_sds((N_CHIP, rh, cols), F32), compiler_params=_cp(("arbitrary", "arbitrary")),
    )(cidx, g, r)


def chip_exchange(parts):
    n = len(parts)

    def body(*refs):
        ins, outs = refs[:n], refs[n:2 * n]
        ss, rs = refs[2 * n:]
        x, y, c = _coords()
        cps = []
        for a in range(n):
            for j, (px, py, pk) in enumerate(_other_chips(x, y)):
                cp = pltpu.make_async_remote_copy(
                    src_ref=ins[a].at[pk], dst_ref=outs[a].at[j], send_sem=ss.at[a, j], recv_sem=rs.at[a, j],
                    device_id=(px, py, c), device_id_type=MESH)
                cp.start()
                cps.append(cp)
        for cp in cps:
            cp.wait()

    return _call(
        body, name="chip_exchange", in_specs=[ANY] * n, out_specs=[ANY] * n,
        out_shape=[_sds((N_CHIP - 1,) + p.shape[1:], F32) for p in parts],
        scratch_shapes=[pltpu.SemaphoreType.DMA((n, 3))] * 2,
    )(*parts)


def chip_add(p, r, kc):
    _, rh, cols = p.shape
    tr = _row_tile(rh, cols)
    per = rh // tr

    def body(k_ref, p_ref, r_ref, o_ref):
        o_ref[...] = ((p_ref[...] + r_ref[0]) + r_ref[1]) + r_ref[2]

    return _call(
        body, name="chip_add",
        grid_spec=pltpu.PrefetchScalarGridSpec(
            num_scalar_prefetch=1, grid=(per,),
            in_specs=[pl.BlockSpec((None, tr, cols), lambda i, k: (k[0], i, 0)),
                      pl.BlockSpec((N_CHIP - 1, tr, cols), lambda i, k: (0, i, 0))],
            out_specs=pl.BlockSpec((tr, cols), lambda i, k: (k[1] * per + i, 0))),
        out_shape=_sds((2 * rh, cols), F32), compiler_params=_cp(("arbitrary",)),
    )(kc, p, r)


def pair_share(fulls):
    n = len(fulls)

    def body(*refs):
        outs = refs[n:2 * n]
        ss, rs = refs[2 * n:]
        x, y, c = _coords()
        cps = []
        for a in range(n):
            rh = outs[a].shape[0] // 2
            mine = outs[a].at[pl.ds(c * rh, rh)]
            cp = pltpu.make_async_remote_copy(
                src_ref=mine, dst_ref=mine, send_sem=ss.at[a], recv_sem=rs.at[a],
                device_id=(x, y, 1 - c), device_id_type=MESH)
            cp.start()
            cps.append(cp)
        for a, cp in enumerate(cps):
            rh = outs[a].shape[0] // 2
            theirs = outs[a].at[pl.ds((1 - c) * rh, rh)]
            cp.wait_send()
            pltpu.make_async_remote_copy(
                src_ref=theirs, dst_ref=theirs, send_sem=ss.at[a], recv_sem=rs.at[a],
                device_id=(x, y, 1 - c), device_id_type=MESH).wait_recv()

    return _call(
        body, name="pair_share", in_specs=[ANY] * n, out_specs=[ANY] * n,
        out_shape=[_sds(f.shape, F32) for f in fulls], input_output_aliases={a: a for a in range(n)},
        scratch_shapes=[pltpu.SemaphoreType.DMA((n,))] * 2,
    )(*fulls)


def small_allreduce_adamw(part, w, m, v):
    n = part.shape[1]

    def body(p_ref, w_ref, m_ref, v_ref, g_ref, d_ref, mo_ref, vo_ref, mine, gath, ss, rs):
        x, y, c = _coords()
        me = 4 * x + 2 * y + c
        mine[...] = jnp.sum(p_ref[...], axis=0, keepdims=True)
        gath[me] = mine[...]
        cps = []
        for k in range(1, 8):
            px, py, pc = x ^ (k >> 2), y ^ ((k >> 1) & 1), c ^ (k & 1)
            cp = pltpu.make_async_remote_copy(
                src_ref=mine, dst_ref=gath.at[me], send_sem=ss.at[k - 1], recv_sem=rs.at[k - 1],
                device_id=(px, py, pc), device_id_type=MESH)
            cp.start()
            cps.append(cp)
        for k in range(1, 8):
            src = 4 * (x ^ (k >> 2)) + 2 * (y ^ ((k >> 1) & 1)) + (c ^ (k & 1))
            pltpu.make_async_remote_copy(
                src_ref=mine, dst_ref=gath.at[src], send_sem=ss.at[k - 1], recv_sem=rs.at[k - 1],
                device_id=(x, y, c), device_id_type=MESH).wait_recv()
        for cp in cps:
            cp.wait_send()
        g = gath[0]
        for dv in range(1, 8):
            g = g + gath[dv]
        g_ref[...] = g
        d_ref[...], mo_ref[...], vo_ref[...] = _adamw_math(w_ref[...], g, m_ref[...], v_ref[...])

    vm = pl.BlockSpec(memory_space=pltpu.VMEM)
    return _call(
        body, name="small_allreduce_adamw", in_specs=[vm] * 4, out_specs=[vm] * 4, out_shape=[_sds((1, n), F32)] * 4,
        scratch_shapes=[pltpu.VMEM((1, n), F32), pltpu.VMEM((8, 1, n), F32),
                        pltpu.SemaphoreType.DMA((7,)), pltpu.SemaphoreType.DMA((7,))],
    )(part, w, m, v)


def _unshard_cols(g):
    k, r, cs = g.shape
    return g.transpose(1, 0, 2).reshape(r, k * cs)


def _shard_cols(w):
    r, c = w.shape
    return w.reshape(r, N_CHIP, c // N_CHIP).transpose(1, 0, 2)


def local_step(x, positions, ln_in_g, ln_in_b, win_g, g_cq, wuq_g, g_ckv, wuk_g, wuv_g, convw_g, conv_b, g_conv_ln,
               b_conv_ln, wout_g, g_ln1, b_ln1, wff1_g, wff2_g, g_ln2, b_ln2, target):
    s, d = x.shape
    c = d - MLA_W
    row = lambda a: a.reshape(1, -1)

    win = _unshard_cols(win_g)
    o_kr = R_Q + R_KV
    o_cv = o_kr + D_ROPE
    win_r = jnp.concatenate([win[:, :o_kr], win[:, o_cv:], win[:, o_kr:o_cv], jnp.zeros((d, LANE - D_ROPE), BF)], axis=1)
    hc = win_r.shape[1]
    kr_blk = (o_kr + 2 * c) // LANE
    wuq = _unshard_cols(wuq_g).reshape(R_Q, HEADS, D_QK)
    wuq_r = jnp.concatenate([wuq[:, :, :D_NOPE].reshape(R_Q, MLA_W),
                             jnp.pad(wuq[:, :, D_NOPE:], ((0, 0), (0, 0), (0, LANE - D_ROPE))).reshape(R_Q, MLA_W)], axis=1)
    wuk = _unshard_cols(wuk_g)
    wuv = _unshard_cols(wuv_g)
    conv_w = jnp.pad(_unshard_cols(convw_g), ((0, 1), (0, 0)))
    wout = wout_g.reshape(d, d)
    wff2 = wff2_g.reshape(-1, d)
    wff1t = wff1_g.transpose(0, 2, 1).reshape(-1, d)
    wff2t = wff2.T

    half = D_ROPE // 2
    inv_freq = ROPE_BASE ** (-jnp.arange(half, dtype=F32) * (2.0 / D_ROPE))
    invf = jnp.concatenate([inv_freq, inv_freq, jnp.zeros((LANE - D_ROPE,), F32)]).reshape(1, LANE)
    cos, sin = rope_tables(positions.astype(F32).reshape(s, 1), invf)
    x0, x0b = ln_in_fwd(x, row(ln_in_g), row(ln_in_b))
    h = matmul("in_proj", x0b, win_r, 512, 640)
    qc, cqn = q_proj(h, g_cq, wuq_r, cos, sin)
    kc, v, ckvn = kv_proj(h, g_ckv, wuk, wuv, cos, sin, kr_blk)
    o, ob, lse = attn_fwd(qc, kc, v)
    co, uc = conv_fwd(h, conv_w, conv_b, g_conv_ln, b_conv_ln)
    r1, x1, x1b = out_proj_ln1(ob, co, wout, x0, g_ln1, b_ln1)
    rb, a1b = ff1_fwd(x1b, wff1_g)
    dr2, dr2b, loss8, dg2, db2 = ff2_ln2_loss(a1b, wff2, x1, target, g_ln2, b_ln2)

    df1b = ff2_bwd_act(dr2b, wff2t, rb)
    gw_ff2 = wgrad("wgrad_ff2", a1b, dr2b, 1024, 1024).reshape(N_CHIP, -1, d)
    gw_ff1 = wgrad("wgrad_ff1", x1b, df1b, 1024, 1024, shards=N_CHIP)
    dr1, dr1b, dg1, db1 = ff1_bwd_ln1(df1b, wff1t, dr2, r1, g_ln1)
    gw_out = jnp.concatenate([wgrad("wgrad_out_attn", ob, dr1b, 1024, 1024)[0],
                              wgrad("wgrad_out_conv", co, dr1b, 1024, 1024)[0]], axis=0).reshape(N_CHIP, -1, d)
    dob, dco, delta = out_proj_bwd(dr1b, wout.T, o)
    duc, dgc, dbc, dcb = conv_bwd_ln(uc, dco, g_conv_ln, b_conv_ln)
    dconv, gconvw = conv_bwd_taps(h, duc, conv_w)
    dqt, dk, dv = attn_bwd(qc, kc, kc.transpose(0, 2, 1), v, dob, lse.reshape(HEADS, 1, s), delta.reshape(HEADS, 1, s))
    dqb, dcq, dgq = q_bwd(dqt, h, g_cq, wuq_r.T, cos, sin)
    dknb, dvb, dckv, dkr, dgkv = kv_bwd(dk, dv, h, g_ckv, wuk.T, wuv.T, cos, sin)
    gwuq_r = wgrad("wgrad_uq", cqn, dqb, 512, 1024)[0]
    gw_uk = wgrad("wgrad_uk", ckvn, dknb, 512, 1024, shards=N_CHIP)
    gw_uv = wgrad("wgrad_uv", ckvn, dvb, 512, 1024, shards=N_CHIP)
    dh = jnp.concatenate([dcq, dckv, dconv, dkr], axis=1)
    gx, dgin, dbin = in_proj_bwd_ln(dh, win_r.T, dr1, x, row(ln_in_g))
    gwin_r = wgrad("wgrad_in", x0b, dh, 1024, 640)[0]

    gwin = jnp.concatenate([gwin_r[:, :o_kr], gwin_r[:, o_kr + 2 * c:o_kr + 2 * c + D_ROPE], gwin_r[:, o_kr:o_kr + 2 * c]], axis=1)
    gwuq = jnp.concatenate([gwuq_r[:, :MLA_W].reshape(R_Q, HEADS, D_NOPE),
                            gwuq_r[:, MLA_W:].reshape(R_Q, HEADS, LANE)[:, :, :D_ROPE]], axis=2).reshape(R_Q, HEADS * D_QK)
    big = dict(w_in=_shard_cols(gwin), w_uq=_shard_cols(gwuq), w_uk=gw_uk, w_uv=gw_uv,
               conv_w=_shard_cols(jnp.pad(gconvw, ((0, 0), (0, 0)))), w_out=gw_out, w_ff1=gw_ff1, w_ff2=gw_ff2)
    small = jnp.concatenate([dgin, dbin, dgq, dgkv, dcb, dgc, dbc, dg1, db1, dg2, db2, loss8], axis=1)
    return gx, big, small


BIG = ["w_in", "w_uq", "w_uk", "w_uv", "conv_w", "w_out", "w_ff1", "w_ff2"]
SMALL = ["ln_in_g", "ln_in_b", "g_cq", "g_ckv", "conv_b", "g_conv_ln", "b_conv_ln", "g_ln1", "b_ln1", "g_ln2", "b_ln2"]
WEIGHTS = ["ln_in_g", "ln_in_b", "w_in", "g_cq", "w_uq", "g_ckv", "w_uk", "w_uv", "conv_w", "conv_b", "g_conv_ln",
           "b_conv_ln", "w_out", "g_ln1", "b_ln1", "w_ff1", "w_ff2", "g_ln2", "b_ln2"]


def _pad_rows(a, rows):
    return jnp.pad(a, ((0, rows - a.shape[0]), (0, 0)))


def kernel(x, positions, ln_in_g, ln_in_b, w_in, g_cq, w_uq, g_ckv, w_uk, w_uv, conv_w, conv_b, g_conv_ln, b_conv_ln, w_out, g_ln1, b_ln1, w_ff1, w_ff2, g_ln2, b_ln2, loss_target, m_ln_in_g, m_ln_in_b, m_w_in, m_g_cq, m_w_uq, m_g_ckv, m_w_uk, m_w_uv, m_conv_w, m_conv_b, m_g_conv_ln, m_b_conv_ln, m_w_out, m_g_ln1, m_b_ln1, m_w_ff1, m_w_ff2, m_g_ln2, m_b_ln2, v_ln_in_g, v_ln_in_b, v_w_in, v_g_cq, v_w_uq, v_g_ckv, v_w_uk, v_w_uv, v_conv_w, v_conv_b, v_g_conv_ln, v_b_conv_ln, v_w_out, v_g_ln1, v_b_ln1, v_w_ff1, v_w_ff2, v_g_ln2, v_b_ln2):
    w = dict(ln_in_g=ln_in_g, ln_in_b=ln_in_b, w_in=w_in, g_cq=g_cq, w_uq=w_uq, g_ckv=g_ckv, w_uk=w_uk, w_uv=w_uv,
             conv_w=conv_w, conv_b=conv_b, g_conv_ln=g_conv_ln, b_conv_ln=b_conv_ln, w_out=w_out, g_ln1=g_ln1,
             b_ln1=b_ln1, w_ff1=w_ff1, w_ff2=w_ff2, g_ln2=g_ln2, b_ln2=b_ln2)
    m = dict(ln_in_g=m_ln_in_g, ln_in_b=m_ln_in_b, w_in=m_w_in, g_cq=m_g_cq, w_uq=m_w_uq, g_ckv=m_g_ckv, w_uk=m_w_uk,
             w_uv=m_w_uv, conv_w=m_conv_w, conv_b=m_conv_b, g_conv_ln=m_g_conv_ln, b_conv_ln=m_b_conv_ln, w_out=m_w_out,
             g_ln1=m_g_ln1, b_ln1=m_b_ln1, w_ff1=m_w_ff1, w_ff2=m_w_ff2, g_ln2=m_g_ln2, b_ln2=m_b_ln2)
    v = dict(ln_in_g=v_ln_in_g, ln_in_b=v_ln_in_b, w_in=v_w_in, g_cq=v_g_cq, w_uq=v_w_uq, g_ckv=v_g_ckv, w_uk=v_w_uk,
             w_uv=v_w_uv, conv_w=v_conv_w, conv_b=v_conv_b, g_conv_ln=v_g_conv_ln, b_conv_ln=v_b_conv_ln, w_out=v_w_out,
             g_ln1=v_g_ln1, b_ln1=v_b_ln1, w_ff1=v_w_ff1, w_ff2=v_w_ff2, g_ln2=v_g_ln2, b_ln2=v_b_ln2)

    sh2 = {n: w[n][0] for n in BIG}
    send = [sh2[n].astype(BF) if n != "conv_w" else _pad_rows(sh2[n], CONV_K + 1) for n in BIG]
    gathered = all_gather_shards(send)
    gw = dict(zip(BIG, gathered))
    gw["conv_w"] = gw["conv_w"][:, :CONV_K]

    gx, big, small = local_step(
        x[0], positions[0], ln_in_g, ln_in_b, gw["w_in"], g_cq, gw["w_uq"], g_ckv, gw["w_uk"], gw["w_uv"], gw["conv_w"],
        conv_b, g_conv_ln, b_conv_ln, gw["w_out"], g_ln1, b_ln1, gw["w_ff1"], gw["w_ff2"], g_ln2, b_ln2, loss_target[0])

    cidx = lax.axis_index("c").astype(jnp.int32).reshape(1)
    kc = jnp.stack([2 * lax.axis_index("x") + lax.axis_index("y"), lax.axis_index("c")]).astype(jnp.int32)
    full = [big[n] for n in BIG]
    recv = pair_exchange(full)
    psum = [pair_add(g, r, cidx) for g, r in zip(full, recv)]
    got = chip_exchange(psum)
    gsh = dict(zip(BIG, pair_share([chip_add(p, r, kc) for p, r in zip(psum, got)])))
    gsh["conv_w"] = gsh["conv_w"][:CONV_K]

    grad, delta, new_m, new_v = {}, {}, {}, {}
    for n in BIG:
        grad[n] = gsh[n][None]
        d_, m_, v_ = adamw("adamw_" + n, sh2[n], gsh[n], m[n][0], v[n][0])
        delta[n], new_m[n], new_v[n] = d_[None], m_[None], v_[None]

    flat = lambda t: jnp.concatenate([t[n].reshape(1, -1) for n in SMALL] + [jnp.zeros((1, LANE), F32)], axis=1)
    g_s, d_s, m_s, v_s = small_allreduce_adamw(small, flat(w), flat(m), flat(v))
    off = 0
    for n in SMALL:
        sz = w[n].size
        for dst, src in ((grad, g_s), (delta, d_s), (new_m, m_s), (new_v, v_s)):
            dst[n] = src[0, off:off + sz].reshape(w[n].shape)
        off += sz
    loss = jnp.sum(g_s[0, off:off + LANE])

    return (loss, gx[None], *[grad[n] for n in WEIGHTS], *[delta[n] for n in WEIGHTS],
            *[new_m[n] for n in WEIGHTS], *[new_v[n] for n in WEIGHTS])
```

```python
import functools

import jax
import jax.numpy as jnp
from jax import lax
from jax.experimental import pallas as pl
from jax.experimental.pallas import tpu as pltpu

F32 = jnp.float32
BF = jnp.bfloat16

HEADS = 8
D_NOPE = 128
D_ROPE = 64
D_V = 128
D_QK = D_NOPE + D_ROPE
R_Q = 512
R_KV = 512
MLA_W = HEADS * D_V
CONV_K = 31
CONV_PAD = CONV_K // 2
ROPE_BASE = 10000.0
LOG2E = 1.4426950408889634
LN2 = 0.6931471805599453
LN_EPS = 1e-5
RMS_EPS = 1e-6
ALPHA = (2.0 * 1) ** 0.25
ADAM_LR = 0.001
ADAM_B1 = 0.9
ADAM_B2 = 0.999
ADAM_EPS = 1e-08
ADAM_WD = 0.01
ADAM_STEP = 10

LANE = 128
SUB = 8
HALO = 16
N_CHIP = 4
MESH = pl.DeviceIdType.MESH
VMEM_MB = 1024 * 1024


def _call(body, **kw):
    return pl.pallas_call(body, **kw)


def _cp(sem, mb=48):
    return pltpu.CompilerParams(dimension_semantics=sem, vmem_limit_bytes=mb * VMEM_MB)


def _sds(shape, dt):
    return jax.ShapeDtypeStruct(shape, dt)


def _dot(a, b):
    return jnp.dot(a, b, preferred_element_type=F32)


def _dot_nt(a, b):
    return lax.dot_general(a, b, (((1,), (1,)), ((), ())), preferred_element_type=F32)


def _dot_tn(a, b):
    return lax.dot_general(a, b, (((0,), (0,)), ((), ())), preferred_element_type=F32)


def _rows8(v):
    t, n = v.shape
    return v.reshape(t // SUB, SUB, n).sum(axis=0)


def _ln_stats(r):
    mu = jnp.mean(r, axis=-1, keepdims=True)
    xc = r - mu
    var = jnp.mean(xc * xc, axis=-1, keepdims=True)
    rstd = lax.rsqrt(var + LN_EPS)
    return xc * rstd, rstd


def _ln_bwd(dy, xhat, rstd, g):
    dyh = dy * g
    m1 = jnp.mean(dyh, axis=-1, keepdims=True)
    m2 = jnp.mean(dyh * xhat, axis=-1, keepdims=True)
    return rstd * (dyh - m1 - xhat * m2)


def _rms_fwd(x, g):
    rr = lax.rsqrt(jnp.mean(x * x, axis=-1, keepdims=True) + RMS_EPS)
    xh = x * rr
    return xh * g, xh, rr


def _rms_bwd(dy, xh, rr, g):
    dyg = dy * g
    return rr * (dyg - xh * jnp.mean(dyg * xh, axis=-1, keepdims=True))


def _rope128(x, cos, sin_signed):
    lane = lax.broadcasted_iota(jnp.int32, x.shape, 1)
    rot = jnp.where(lane < D_ROPE // 2, pltpu.roll(x, LANE - D_ROPE // 2, 1), pltpu.roll(x, D_ROPE // 2, 1))
    return x * cos + rot * sin_signed


def _unrope128(dy, cos, sin_signed):
    t = dy * sin_signed
    lane = lax.broadcasted_iota(jnp.int32, dy.shape, 1)
    rot = jnp.where(lane < D_ROPE // 2, pltpu.roll(t, LANE - D_ROPE // 2, 1), pltpu.roll(t, D_ROPE // 2, 1))
    return dy * cos + rot


def _sigmoid(x):
    return 1.0 / (1.0 + jnp.exp(-x))


def _row_chunks(tm, fn, rc=128):
    rc = min(rc, tm)

    def step(ci, carry):
        fn(pl.ds(pl.multiple_of(ci * rc, rc), rc))
        return carry

    lax.fori_loop(0, tm // rc, step, 0)


def _unrolled_loop(n, unroll, fn, init):
    unroll = min(n, unroll)
    assert n % unroll == 0

    def body(t, carry):
        for u in range(unroll):
            carry = fn(t * unroll + u, carry)
        return carry

    return lax.fori_loop(0, n // unroll, body, init)


def _tile(s, want):
    t = min(s, want)
    assert s % t == 0
    return t


def rope_tables(pos_f, invf):
    s = pos_f.shape[0]
    tm = _tile(s, 1024)

    def body(p_ref, f_ref, c_ref, s_ref):
        ang = p_ref[...] * f_ref[...]
        lane = lax.broadcasted_iota(jnp.int32, ang.shape, 1)
        c = jnp.cos(ang)
        sn = jnp.sin(ang)
        c_ref[...] = jnp.where(lane < D_ROPE, c, 0.0)
        s_ref[...] = jnp.where(lane < D_ROPE // 2, -sn, jnp.where(lane < D_ROPE, sn, 0.0))

    return _call(
        body, name="rope_tables", grid=(s // tm,),
        in_specs=[pl.BlockSpec((tm, 1), lambda i: (i, 0)), pl.BlockSpec((1, LANE), lambda i: (0, 0))],
        out_specs=[pl.BlockSpec((tm, LANE), lambda i: (i, 0))] * 2,
        out_shape=[_sds((s, LANE), F32)] * 2,
        compiler_params=_cp(("arbitrary",)),
    )(pos_f, invf)


def ln_in_fwd(x, g, b):
    s, d = x.shape
    tm = _tile(s, 512)

    def body(x_ref, g_ref, b_ref, o_ref, ob_ref):
        xhat, _ = _ln_stats(x_ref[...])
        y = xhat * g_ref[...] + b_ref[...]
        o_ref[...] = y
        ob_ref[...] = y.astype(BF)

    row = pl.BlockSpec((1, d), lambda i: (0, 0))
    tok = pl.BlockSpec((tm, d), lambda i: (i, 0))
    return _call(
        body, name="ln_in_fwd", grid=(s // tm,), in_specs=[tok, row, row], out_specs=[tok, tok],
        out_shape=[_sds((s, d), F32), _sds((s, d), BF)], compiler_params=_cp(("arbitrary",)),
    )(x, g, b)


def matmul(name, a, w, tm, tn, out_dtype=F32):
    s, k = a.shape
    n = w.shape[1]
    tm = _tile(s, tm)
    tn = _tile(n, tn)

    def body(a_ref, w_ref, o_ref):
        o_ref[...] = _dot(a_ref[...], w_ref[...]).astype(o_ref.dtype)

    return _call(
        body, name=name, grid=(s // tm, n // tn),
        in_specs=[pl.BlockSpec((tm, k), lambda i, j: (i, 0)), pl.BlockSpec((k, tn), lambda i, j: (0, j))],
        out_specs=pl.BlockSpec((tm, tn), lambda i, j: (i, j)),
        out_shape=_sds((s, n), out_dtype), compiler_params=_cp(("arbitrary", "arbitrary")),
    )(a, w)


def q_proj(h, g_cq, wuq, cos, sin):
    s = h.shape[0]
    tm = _tile(s, 512)

    def body(h_ref, g_ref, w_ref, c_ref, s_ref, q_ref, n_ref):
        y, _, _ = _rms_fwd(h_ref[...], g_ref[...])
        yb = y.astype(BF)
        n_ref[...] = yb
        q = _dot(yb, w_ref[...])
        c = c_ref[...]
        sn = s_ref[...]
        for hd in range(HEADS):
            q_ref[hd, :, 0:LANE] = q[:, LANE * hd:LANE * (hd + 1)].astype(BF)
            qr = q[:, MLA_W + LANE * hd:MLA_W + LANE * (hd + 1)]
            q_ref[hd, :, LANE:2 * LANE] = _rope128(qr, c, sn).astype(BF)

    return _call(
        body, name="q_proj", grid=(s // tm,),
        in_specs=[pl.BlockSpec((tm, R_Q), lambda i: (i, 0)), pl.BlockSpec((1, R_Q), lambda i: (0, 0)),
                  pl.BlockSpec((R_Q, 2 * MLA_W), lambda i: (0, 0)),
                  pl.BlockSpec((tm, LANE), lambda i: (i, 0)), pl.BlockSpec((tm, LANE), lambda i: (i, 0))],
        out_specs=[pl.BlockSpec((HEADS, tm, 2 * LANE), lambda i: (0, i, 0)), pl.BlockSpec((tm, R_Q), lambda i: (i, 0))],
        out_shape=[_sds((HEADS, s, 2 * LANE), BF), _sds((s, R_Q), BF)], compiler_params=_cp(("arbitrary",)),
    )(h, g_cq, wuq, cos, sin)


def kv_proj(h, g_ckv, wuk, wuv, cos, sin, kr_blk):
    s = h.shape[0]
    tm = _tile(s, 512)

    def body(h_ref, kr_ref, g_ref, wk_ref, wv_ref, c_ref, s_ref, k_ref, v_ref, n_ref):
        y, _, _ = _rms_fwd(h_ref[...], g_ref[...])
        yb = y.astype(BF)
        n_ref[...] = yb
        kn = _dot(yb, wk_ref[...])
        v = _dot(yb, wv_ref[...])
        kr = _rope128(kr_ref[...], c_ref[...], s_ref[...]).astype(BF)
        for hd in range(HEADS):
            k_ref[hd, :, 0:LANE] = kn[:, LANE * hd:LANE * (hd + 1)].astype(BF)
            k_ref[hd, :, LANE:2 * LANE] = kr
            v_ref[hd] = v[:, LANE * hd:LANE * (hd + 1)].astype(BF)

    tab = pl.BlockSpec((tm, LANE), lambda i: (i, 0))
    wsp = pl.BlockSpec((R_KV, MLA_W), lambda i: (0, 0))
    return _call(
        body, name="kv_proj", grid=(s // tm,),
        in_specs=[pl.BlockSpec((tm, R_KV), lambda i: (i, 1)), pl.BlockSpec((tm, LANE), lambda i: (i, kr_blk)),
                  pl.BlockSpec((1, R_KV), lambda i: (0, 0)), wsp, wsp, tab, tab],
        out_specs=[pl.BlockSpec((HEADS, tm, 2 * LANE), lambda i: (0, i, 0)),
                   pl.BlockSpec((HEADS, tm, LANE), lambda i: (0, i, 0)), pl.BlockSpec((tm, R_KV), lambda i: (i, 0))],
        out_shape=[_sds((HEADS, s, 2 * LANE), BF), _sds((HEADS, s, LANE), BF), _sds((s, R_KV), BF)],
        compiler_params=_cp(("arbitrary",)),
    )(h, h, g_ckv, wuk, wuv, cos, sin)


def attn_fwd(qc, kc, v):
    _, s, _ = qc.shape
    tq = _tile(s, 256)
    tk = _tile(s, 512)
    scale = D_QK ** -0.5
    c2 = scale * LOG2E
    nk = s // tk
    nb = tk // LANE
    un = 4

    def body(q_ref, k_ref, v_ref, o_ref, ob_ref, l_ref, s_scr):
        q = q_ref[...]

        def scores(j, mpart):
            off = pl.multiple_of(j * tk, tk)
            sc = _dot_nt(q, k_ref[pl.ds(off, tk), :])
            s_scr[:, pl.ds(off, tk)] = sc
            for b in range(nb):
                mpart = jnp.maximum(mpart, sc[:, LANE * b:LANE * (b + 1)])
            return mpart

        mpart = _unrolled_loop(nk, un, scores, jnp.full((tq, LANE), -jnp.inf, F32))
        m = jnp.max(mpart, axis=-1, keepdims=True)
        mb = jnp.broadcast_to(m * c2, (tq, LANE))

        def weigh(j, carry):
            lpart, acc = carry
            off = pl.multiple_of(j * tk, tk)
            ps = []
            for b in range(nb):
                p = jnp.exp2(s_scr[:, pl.ds(off + LANE * b, LANE)] * c2 - mb)
                lpart = lpart + p
                ps.append(p.astype(BF))
            acc = acc + _dot(jnp.concatenate(ps, axis=1), v_ref[pl.ds(off, tk), :])
            return lpart, acc

        lpart, acc = _unrolled_loop(nk, un, weigh, (jnp.zeros((tq, LANE), F32), jnp.zeros((tq, D_V), F32)))
        l = jnp.sum(lpart, axis=-1, keepdims=True)
        o = acc / l
        o_ref[...] = o
        ob_ref[...] = o.astype(BF)
        l_ref[...] = m * c2 + jnp.log(l) * LOG2E

    return _call(
        body, name="attn_fwd", grid=(HEADS, s // tq),
        in_specs=[pl.BlockSpec((None, tq, 2 * LANE), lambda h, i: (h, i, 0)),
                  pl.BlockSpec((None, s, 2 * LANE), lambda h, i: (h, 0, 0)),
                  pl.BlockSpec((None, s, LANE), lambda h, i: (h, 0, 0))],
        out_specs=[pl.BlockSpec((tq, LANE), lambda h, i: (i, h)), pl.BlockSpec((tq, LANE), lambda h, i: (i, h)),
                   pl.BlockSpec((None, tq, 1), lambda h, i: (h, i, 0))],
        out_shape=[_sds((s, MLA_W), F32), _sds((s, MLA_W), BF), _sds((HEADS, s, 1), F32)],
        scratch_shapes=[pltpu.VMEM((tq, s), F32)], compiler_params=_cp(("arbitrary", "arbitrary")),
    )(qc, kc, v)


def _halo_specs(tm, s, width, col):
    r = tm // HALO
    nb = s // HALO
    cur = pl.BlockSpec((tm, width), lambda i: (i, col))
    prev = pl.BlockSpec((HALO, width), lambda i: (jnp.maximum(i * r - 1, 0), col))
    nxt = pl.BlockSpec((HALO, width), lambda i: (jnp.minimum((i + 1) * r, nb - 1), col))
    return cur, prev, nxt


def _fill_slab(slab, tm, prev, cur, nxt):
    i = pl.program_id(0)
    last = pl.num_programs(0) - 1
    slab[0:HALO, :] = jnp.where(i > 0, prev, 0.0)
    slab[HALO:HALO + tm, :] = cur
    slab[HALO + tm:2 * HALO + tm, :] = jnp.where(i < last, nxt, 0.0)


def conv_fwd(h, conv_w, conv_b, g_ln, b_ln):
    s = h.shape[0]
    c = conv_w.shape[1]
    tm = _tile(s, 256)
    rc = _tile(tm, 128)

    def body(a_ref, ap_ref, an_ref, g_ref, gp_ref, gn_ref, w_ref, cb_ref, lg_ref, lb_ref, co_ref, uc_ref, slab):
        _fill_slab(slab, tm, ap_ref[...] * _sigmoid(gp_ref[...]), a_ref[...] * _sigmoid(g_ref[...]),
                   an_ref[...] * _sigmoid(gn_ref[...]))
        for cb in range(c // LANE):
            cs = slice(LANE * cb, LANE * (cb + 1))
            for r0 in range(0, tm, rc):
                acc = jnp.zeros((rc, LANE), F32)
                for k in range(CONV_K):
                    acc = acc + w_ref[k:k + 1, cs] * slab[pl.ds(r0 + HALO - CONV_PAD + k, rc), cs]
                uc_ref[r0:r0 + rc, cs] = acc + cb_ref[:, cs]
        xhat, _ = _ln_stats(uc_ref[...])
        cl = xhat * lg_ref[...] + lb_ref[...]
        co_ref[...] = (cl * _sigmoid(cl)).astype(BF)

    a_specs = _halo_specs(tm, s, c, 1)
    g_specs = _halo_specs(tm, s, c, 2)
    row = pl.BlockSpec((1, c), lambda i: (0, 0))
    tok = pl.BlockSpec((tm, c), lambda i: (i, 0))
    return _call(
        body, name="conv_fwd", grid=(s // tm,),
        in_specs=[*a_specs, *g_specs, pl.BlockSpec(conv_w.shape, lambda i: (0, 0)), row, row, row],
        out_specs=[tok, tok], out_shape=[_sds((s, c), BF), _sds((s, c), F32)],
        scratch_shapes=[pltpu.VMEM((tm + 2 * HALO, c), F32)], compiler_params=_cp(("arbitrary",)),
    )(h, h, h, h, h, h, conv_w, conv_b, g_ln, b_ln)


def out_proj_ln1(ob, co, wout, x0, g1, b1):
    s, d = x0.shape
    kh = ob.shape[1]
    tm = _tile(s, 256)

    def body(o_ref, c_ref, w_ref, x_ref, g_ref, b_ref, r_ref, x1_ref, x1b_ref, acc):
        k = pl.program_id(1)

        @pl.when(k == 0)
        def _():
            acc[...] = _dot(o_ref[...], w_ref[...])

        @pl.when(k == 1)
        def _():
            r = ALPHA * x_ref[...] + (acc[...] + _dot(c_ref[...], w_ref[...]))
            r_ref[...] = r
            xhat, _ = _ln_stats(r)
            y = xhat * g_ref[...] + b_ref[...]
            x1_ref[...] = y
            x1b_ref[...] = y.astype(BF)

    half = pl.BlockSpec((tm, kh), lambda i, k: (i, 0))
    tok = pl.BlockSpec((tm, d), lambda i, k: (i, 0))
    row = pl.BlockSpec((1, d), lambda i, k: (0, 0))
    return _call(
        body, name="out_proj_ln1", grid=(s // tm, 2),
        in_specs=[half, half, pl.BlockSpec((kh, d), lambda i, k: (k, 0)), tok, row, row],
        out_specs=[tok, tok, tok], out_shape=[_sds((s, d), F32), _sds((s, d), F32), _sds((s, d), BF)],
        scratch_shapes=[pltpu.VMEM((tm, d), F32)], compiler_params=_cp(("arbitrary", "arbitrary")),
    )(ob, co, wout, x0, g1, b1)


def ff1_fwd(x1b, wff1_g):
    s, d = x1b.shape
    nsh, _, fs = wff1_g.shape
    tm = _tile(s, 512)
    tn = _tile(fs, 1024)
    per = fs // tn

    def body(a_ref, w_ref, r_ref, a1_ref):
        r = jnp.maximum(_dot(a_ref[...], w_ref[...]), 0.0)
        r_ref[...] = r.astype(BF)
        a1_ref[...] = (r * r).astype(BF)

    out = pl.BlockSpec((tm, tn), lambda i, j: (i, j))
    return _call(
        body, name="ff1_fwd", grid=(s // tm, nsh * per),
        in_specs=[pl.BlockSpec((tm, d), lambda i, j: (i, 0)),
                  pl.BlockSpec((None, d, tn), lambda i, j: (j // per, 0, j % per))],
        out_specs=[out, out], out_shape=[_sds((s, nsh * fs), BF)] * 2,
        compiler_params=_cp(("arbitrary", "arbitrary")),
    )(x1b, wff1_g)


def ff2_ln2_loss(a1b, wff2, x1, target, g2, b2):
    s, f = a1b.shape
    d = x1.shape[1]
    tm = _tile(s, 512)
    tk = _tile(f, 1024)
    nk = f // tk

    def body(a_ref, w_ref, x_ref, t_ref, g_ref, b_ref, dr_ref, drb_ref, loss_ref, dg_ref, db_ref, acc):
        i = pl.program_id(0)
        k = pl.program_id(1)

        @pl.when(k == 0)
        def _():
            acc[...] = _dot(a_ref[...], w_ref[...])

        @pl.when(k > 0)
        def _():
            acc[...] += _dot(a_ref[...], w_ref[...])

        @pl.when(jnp.logical_and(i == 0, k == 0))
        def _():
            loss_ref[...] = jnp.zeros_like(loss_ref)
            dg_ref[...] = jnp.zeros_like(dg_ref)
            db_ref[...] = jnp.zeros_like(db_ref)

        @pl.when(k == nk - 1)
        def _():
            g = g_ref[...]

            def chunk(rows):
                r = ALPHA * x_ref[rows, :] + acc[rows, :]
                xhat, rstd = _ln_stats(r)
                e = xhat * g + b_ref[...] - t_ref[rows, :]
                e2 = _rows8(e * e)
                part = e2[:, 0:LANE]
                for c in range(1, d // LANE):
                    part = part + e2[:, LANE * c:LANE * (c + 1)]
                loss_ref[...] += part * (0.5 / d)
                dy = e * (1.0 / d)
                dg_ref[...] += _rows8(dy * xhat)
                db_ref[...] += _rows8(dy)
                dr = _ln_bwd(dy, xhat, rstd, g)
                dr_ref[rows, :] = dr
                drb_ref[rows, :] = dr.astype(BF)

            _row_chunks(tm, chunk)

    tok = pl.BlockSpec((tm, d), lambda i, k: (i, 0))
    row = pl.BlockSpec((1, d), lambda i, k: (0, 0))
    accs = pl.BlockSpec((SUB, d), lambda i, k: (0, 0))
    return _call(
        body, name="ff2_ln2_loss", grid=(s // tm, nk),
        in_specs=[pl.BlockSpec((tm, tk), lambda i, k: (i, k)), pl.BlockSpec((tk, d), lambda i, k: (k, 0)),
                  tok, tok, row, row],
        out_specs=[tok, tok, pl.BlockSpec((SUB, LANE), lambda i, k: (0, 0)), accs, accs],
        out_shape=[_sds((s, d), F32), _sds((s, d), BF), _sds((SUB, LANE), F32), _sds((SUB, d), F32), _sds((SUB, d), F32)],
        scratch_shapes=[pltpu.VMEM((tm, d), F32)], compiler_params=_cp(("arbitrary", "arbitrary"), 56),
    )(a1b, wff2, x1, target, g2, b2)


def ff2_bwd_act(dr2b, wff2t, rb):
    s, d = dr2b.shape
    f = wff2t.shape[1]
    tm = _tile(s, 512)
    tn = _tile(f, 1024)

    def body(a_ref, w_ref, r_ref, o_ref):
        o_ref[...] = (_dot(a_ref[...], w_ref[...]) * (2.0 * r_ref[...].astype(F32))).astype(BF)

    return _call(
        body, name="ff2_bwd_act", grid=(s // tm, f // tn),
        in_specs=[pl.BlockSpec((tm, d), lambda i, j: (i, 0)), pl.BlockSpec((d, tn), lambda i, j: (0, j)),
                  pl.BlockSpec((tm, tn), lambda i, j: (i, j))],
        out_specs=pl.BlockSpec((tm, tn), lambda i, j: (i, j)), out_shape=_sds((s, f), BF),
        compiler_params=_cp(("arbitrary", "arbitrary")),
    )(dr2b, wff2t, rb)


def wgrad(name, a, b, tm, tn, tk=1024, shards=1):
    s, m = a.shape
    n = b.shape[1]
    tm = _tile(m, tm)
    ns = n // shards
    tn = _tile(ns, tn)
    tk = _tile(s, tk)
    per = ns // tn

    def body(a_ref, b_ref, o_ref):
        k = pl.program_id(2)

        @pl.when(k == 0)
        def _():
            o_ref[...] = _dot_tn(a_ref[...], b_ref[...])

        @pl.when(k > 0)
        def _():
            o_ref[...] += _dot_tn(a_ref[...], b_ref[...])

    return _call(
        body, name=name, grid=(m // tm, n // tn, s // tk),
        in_specs=[pl.BlockSpec((tk, tm), lambda i, j, k: (k, i)), pl.BlockSpec((tk, tn), lambda i, j, k: (k, j))],
        out_specs=pl.BlockSpec((None, tm, tn), lambda i, j, k: (j // per, i, j % per)),
        out_shape=_sds((shards, m, ns), F32), compiler_params=_cp(("arbitrary", "arbitrary", "arbitrary")),
    )(a, b)


def ff1_bwd_ln1(df1b, wff1t, dr2, r1, g1):
    s, f = df1b.shape
    d = dr2.shape[1]
    tm = _tile(s, 512)
    tk = _tile(f, 1024)
    nk = f // tk

    def body(a_ref, w_ref, d2_ref, r_ref, g_ref, dr_ref, drb_ref, dg_ref, db_ref, acc):
        i = pl.program_id(0)
        k = pl.program_id(1)

        @pl.when(k == 0)
        def _():
            acc[...] = _dot(a_ref[...], w_ref[...])

        @pl.when(k > 0)
        def _():
            acc[...] += _dot(a_ref[...], w_ref[...])

        @pl.when(jnp.logical_and(i == 0, k == 0))
        def _():
            dg_ref[...] = jnp.zeros_like(dg_ref)
            db_ref[...] = jnp.zeros_like(db_ref)

        @pl.when(k == nk - 1)
        def _():
            g = g_ref[...]

            def chunk(rows):
                dy = ALPHA * d2_ref[rows, :] + acc[rows, :]
                xhat, rstd = _ln_stats(r_ref[rows, :])
                dg_ref[...] += _rows8(dy * xhat)
                db_ref[...] += _rows8(dy)
                dr = _ln_bwd(dy, xhat, rstd, g)
                dr_ref[rows, :] = dr
                drb_ref[rows, :] = dr.astype(BF)

            _row_chunks(tm, chunk)

    tok = pl.BlockSpec((tm, d), lambda i, k: (i, 0))
    accs = pl.BlockSpec((SUB, d), lambda i, k: (0, 0))
    return _call(
        body, name="ff1_bwd_ln1", grid=(s // tm, nk),
        in_specs=[pl.BlockSpec((tm, tk), lambda i, k: (i, k)), pl.BlockSpec((tk, d), lambda i, k: (k, 0)),
                  tok, tok, pl.BlockSpec((1, d), lambda i, k: (0, 0))],
        out_specs=[tok, tok, accs, accs],
        out_shape=[_sds((s, d), F32), _sds((s, d), BF), _sds((SUB, d), F32), _sds((SUB, d), F32)],
        scratch_shapes=[pltpu.VMEM((tm, d), F32)], compiler_params=_cp(("arbitrary", "arbitrary"), 56),
    )(df1b, wff1t, dr2, r1, g1)


def out_proj_bwd(dr1b, woutt, o):
    s, d = dr1b.shape
    tm = _tile(s, 256)

    def body(a_ref, w_ref, o_ref, do_ref, dc_ref, dl_ref):
        dcat = _dot(a_ref[...], w_ref[...])
        do = dcat[:, 0:MLA_W]
        do_ref[...] = do.astype(BF)
        dc_ref[...] = dcat[:, MLA_W:]
        prod = do * o_ref[...]
        for hd in range(HEADS):
            dl_ref[hd] = jnp.sum(prod[:, LANE * hd:LANE * (hd + 1)], axis=-1, keepdims=True)

    half = pl.BlockSpec((tm, MLA_W), lambda i: (i, 0))
    return _call(
        body, name="out_proj_bwd", grid=(s // tm,),
        in_specs=[pl.BlockSpec((tm, d), lambda i: (i, 0)), pl.BlockSpec((d, d), lambda i: (0, 0)), half],
        out_specs=[half, pl.BlockSpec((tm, d - MLA_W), lambda i: (i, 0)), pl.BlockSpec((HEADS, tm, 1), lambda i: (0, i, 0))],
        out_shape=[_sds((s, MLA_W), BF), _sds((s, d - MLA_W), F32), _sds((HEADS, s, 1), F32)],
        compiler_params=_cp(("arbitrary",)),
    )(dr1b, woutt, o)


def conv_bwd_ln(uc, dco, g_ln, b_ln):
    s, c = uc.shape
    tm = _tile(s, 512)

    def body(u_ref, d_ref, g_ref, b_ref, du_ref, dg_ref, db_ref, dcb_ref):
        @pl.when(pl.program_id(0) == 0)
        def _():
            dg_ref[...] = jnp.zeros_like(dg_ref)
            db_ref[...] = jnp.zeros_like(db_ref)
            dcb_ref[...] = jnp.zeros_like(dcb_ref)

        xhat, rstd = _ln_stats(u_ref[...])
        g = g_ref[...]
        cl = xhat * g + b_ref[...]
        sg = _sigmoid(cl)
        dcl = d_ref[...] * (sg * (1.0 + cl * (1.0 - sg)))
        dg_ref[...] += _rows8(dcl * xhat)
        db_ref[...] += _rows8(dcl)
        du = _ln_bwd(dcl, xhat, rstd, g)
        du_ref[...] = du
        dcb_ref[...] += _rows8(du)

    tok = pl.BlockSpec((tm, c), lambda i: (i, 0))
    row = pl.BlockSpec((1, c), lambda i: (0, 0))
    accs = pl.BlockSpec((SUB, c), lambda i: (0, 0))
    return _call(
        body, name="conv_bwd_ln", grid=(s // tm,), in_specs=[tok, tok, row, row], out_specs=[tok, accs, accs, accs],
        out_shape=[_sds((s, c), F32)] + [_sds((SUB, c), F32)] * 3, compiler_params=_cp(("arbitrary",)),
    )(uc, dco, g_ln, b_ln)


def conv_bwd_taps(h, duc, conv_w):
    s, c = duc.shape
    tm = _tile(s, 256)
    rc = _tile(tm, 128)

    def body(a_ref, ap_ref, an_ref, g_ref, gp_ref, gn_ref, d_ref, dp_ref, dn_ref, w_ref, o_ref, dw_ref, uslab, dslab, du_s):
        @pl.when(pl.program_id(0) == 0)
        def _():
            dw_ref[...] = jnp.zeros_like(dw_ref)

        sg = _sigmoid(g_ref[...])
        a = a_ref[...]
        _fill_slab(uslab, tm, ap_ref[...] * _sigmoid(gp_ref[...]), a * sg, an_ref[...] * _sigmoid(gn_ref[...]))
        _fill_slab(dslab, tm, dp_ref[...], d_ref[...], dn_ref[...])
        for cb in range(c // LANE):
            cs = slice(LANE * cb, LANE * (cb + 1))
            for r0 in range(0, tm, rc):
                acc = jnp.zeros((rc, LANE), F32)
                dcur = dslab[pl.ds(r0 + HALO, rc), cs]
                for k in range(CONV_K):
                    acc = acc + w_ref[k:k + 1, cs] * dslab[pl.ds(r0 + HALO + CONV_PAD - k, rc), cs]
                    dw_ref[k:k + 1, cs] += jnp.sum(dcur * uslab[pl.ds(r0 + HALO - CONV_PAD + k, rc), cs], axis=0, keepdims=True)
                du_s[r0:r0 + rc, cs] = acc
        du = du_s[...]
        o_ref[:, 0:c] = (du * sg).astype(BF)
        o_ref[:, c:2 * c] = (du * a * sg * (1.0 - sg)).astype(BF)

    a_specs = _halo_specs(tm, s, c, 1)
    g_specs = _halo_specs(tm, s, c, 2)
    d_specs = _halo_specs(tm, s, c, 0)
    wsp = pl.BlockSpec(conv_w.shape, lambda i: (0, 0))
    return _call(
        body, name="conv_bwd_taps", grid=(s // tm,), in_specs=[*a_specs, *g_specs, *d_specs, wsp],
        out_specs=[pl.BlockSpec((tm, 2 * c), lambda i: (i, 0)), wsp],
        out_shape=[_sds((s, 2 * c), BF), _sds(conv_w.shape, F32)],
        scratch_shapes=[pltpu.VMEM((tm + 2 * HALO, c), F32), pltpu.VMEM((tm + 2 * HALO, c), F32), pltpu.VMEM((tm, c), F32)],
        compiler_params=_cp(("arbitrary",)),
    )(h, h, h, h, h, h, duc, duc, duc, conv_w)


def attn_bwd(qc, kc, kct, v, dob, lse_r, delta_r):
    _, s, _ = qc.shape
    tk = _tile(s, 512)
    tq = _tile(s, 512)
    scale = D_QK ** -0.5
    c2 = scale * LOG2E

    def body(k_ref, kt_ref, v_ref, q_ref, do_ref, l_ref, dl_ref, dqt_ref, dk_ref, dv_ref):
        @pl.when(pl.program_id(1) == 0)
        def _():
            dqt_ref[...] = jnp.zeros_like(dqt_ref)

        k = k_ref[...]
        kt = kt_ref[...]
        vv = v_ref[...]

        def step(i, carry):
            dk, dv = carry
            off = pl.multiple_of(i * tq, tq)
            q = q_ref[pl.ds(off, tq), :]
            do = do_ref[pl.ds(off, tq), :]
            pt = jnp.exp2(_dot_nt(k, q) * c2 - l_ref[:, pl.ds(off, tq)])
            dv = dv + _dot(pt.astype(BF), do)
            dpt = _dot_nt(vv, do)
            dsb = (pt * (dpt - dl_ref[:, pl.ds(off, tq)]) * scale).astype(BF)
            dk = dk + _dot(dsb, q)
            dqt_ref[:, pl.ds(off, tq)] += _dot(kt, dsb)
            return dk, dv

        dk, dv = _unrolled_loop(s // tq, 2, step, (jnp.zeros((tk, 2 * LANE), F32), jnp.zeros((tk, LANE), F32)))
        dk_ref[...] = dk
        dv_ref[...] = dv

    rowv = pl.BlockSpec((None, 1, s), lambda h, j: (h, 0, 0))
    return _call(
        body, name="attn_bwd", grid=(HEADS, s // tk),
        in_specs=[pl.BlockSpec((None, tk, 2 * LANE), lambda h, j: (h, j, 0)),
                  pl.BlockSpec((None, 2 * LANE, tk), lambda h, j: (h, 0, j)),
                  pl.BlockSpec((None, tk, LANE), lambda h, j: (h, j, 0)),
                  pl.BlockSpec((None, s, 2 * LANE), lambda h, j: (h, 0, 0)),
                  pl.BlockSpec((s, LANE), lambda h, j: (0, h)), rowv, rowv],
        out_specs=[pl.BlockSpec((None, 2 * LANE, s), lambda h, j: (h, 0, 0)),
                   pl.BlockSpec((None, tk, 2 * LANE), lambda h, j: (h, j, 0)),
                   pl.BlockSpec((None, tk, LANE), lambda h, j: (h, j, 0))],
        out_shape=[_sds((HEADS, 2 * LANE, s), F32), _sds((HEADS, s, 2 * LANE), F32), _sds((HEADS, s, LANE), F32)],
        compiler_params=_cp(("arbitrary", "arbitrary"), 56),
    )(kc, kct, v, qc, dob, lse_r, delta_r)


def q_bwd(dqt, h, g_cq, wuqt, cos, sin):
    s = h.shape[0]
    tm = _tile(s, 256)

    def body(d_ref, h_ref, g_ref, w_ref, c_ref, s_ref, dq_ref, dc_ref, dg_ref):
        @pl.when(pl.program_id(0) == 0)
        def _():
            dg_ref[...] = jnp.zeros_like(dg_ref)

        c = c_ref[...]
        sn = s_ref[...]
        for hd in range(HEADS):
            t = d_ref[hd].T
            dq_ref[:, LANE * hd:LANE * (hd + 1)] = t[:, 0:LANE].astype(BF)
            dq_ref[:, MLA_W + LANE * hd:MLA_W + LANE * (hd + 1)] = _unrope128(t[:, LANE:2 * LANE], c, sn).astype(BF)
        dy = _dot(dq_ref[...], w_ref[...])
        g = g_ref[...]
        _, xh, rr = _rms_fwd(h_ref[...], g)
        dg_ref[...] += _rows8(dy * xh)
        dc_ref[...] = _rms_bwd(dy, xh, rr, g).astype(BF)

    tab = pl.BlockSpec((tm, LANE), lambda i: (i, 0))
    return _call(
        body, name="q_bwd", grid=(s // tm,),
        in_specs=[pl.BlockSpec((HEADS, 2 * LANE, tm), lambda i: (0, 0, i)), pl.BlockSpec((tm, R_Q), lambda i: (i, 0)),
                  pl.BlockSpec((1, R_Q), lambda i: (0, 0)), pl.BlockSpec((2 * MLA_W, R_Q), lambda i: (0, 0)), tab, tab],
        out_specs=[pl.BlockSpec((tm, 2 * MLA_W), lambda i: (i, 0)), pl.BlockSpec((tm, R_Q), lambda i: (i, 0)),
                   pl.BlockSpec((SUB, R_Q), lambda i: (0, 0))],
        out_shape=[_sds((s, 2 * MLA_W), BF), _sds((s, R_Q), BF), _sds((SUB, R_Q), F32)],
        compiler_params=_cp(("arbitrary",)),
    )(dqt, h, g_cq, wuqt, cos, sin)


def kv_bwd(dk, dv, h, g_ckv, wukt, wuvt, cos, sin):
    s = h.shape[0]
    tm = _tile(s, 256)

    def body(dk_ref, dv_ref, h_ref, g_ref, wk_ref, wv_ref, c_ref, s_ref, dkn_ref, dvb_ref, dc_ref, dkr_ref, dg_ref):
        @pl.when(pl.program_id(0) == 0)
        def _():
            dg_ref[...] = jnp.zeros_like(dg_ref)

        dkr = dk_ref[0, :, LANE:2 * LANE]
        for hd in range(HEADS):
            dkn_ref[:, LANE * hd:LANE * (hd + 1)] = dk_ref[hd, :, 0:LANE].astype(BF)
            dvb_ref[:, LANE * hd:LANE * (hd + 1)] = dv_ref[hd].astype(BF)
            if hd > 0:
                dkr = dkr + dk_ref[hd, :, LANE:2 * LANE]
        dkr_ref[...] = _unrope128(dkr, c_ref[...], s_ref[...]).astype(BF)
        dy = _dot(dkn_ref[...], wk_ref[...]) + _dot(dvb_ref[...], wv_ref[...])
        g = g_ref[...]
        _, xh, rr = _rms_fwd(h_ref[...], g)
        dg_ref[...] += _rows8(dy * xh)
        dc_ref[...] = _rms_bwd(dy, xh, rr, g).astype(BF)

    tab = pl.BlockSpec((tm, LANE), lambda i: (i, 0))
    wsp = pl.BlockSpec((MLA_W, R_KV), lambda i: (0, 0))
    wide = pl.BlockSpec((tm, MLA_W), lambda i: (i, 0))
    return _call(
        body, name="kv_bwd", grid=(s // tm,),
        in_specs=[pl.BlockSpec((HEADS, tm, 2 * LANE), lambda i: (0, i, 0)), pl.BlockSpec((HEADS, tm, LANE), lambda i: (0, i, 0)),
                  pl.BlockSpec((tm, R_KV), lambda i: (i, 1)), pl.BlockSpec((1, R_KV), lambda i: (0, 0)), wsp, wsp, tab, tab],
        out_specs=[wide, wide, pl.BlockSpec((tm, R_KV), lambda i: (i, 0)), tab, pl.BlockSpec((SUB, R_KV), lambda i: (0, 0))],
        out_shape=[_sds((s, MLA_W), BF), _sds((s, MLA_W), BF), _sds((s, R_KV), BF), _sds((s, LANE), BF), _sds((SUB, R_KV), F32)],
        compiler_params=_cp(("arbitrary",)),
    )(dk, dv, h, g_ckv, wukt, wuvt, cos, sin)


def in_proj_bwd_ln(dh, wint, dr1, x, g_in):
    s, hc = dh.shape
    d = x.shape[1]
    tm = _tile(s, 512)
    tk = _tile(hc, 640)
    nk = hc // tk

    def body(a_ref, w_ref, d1_ref, x_ref, g_ref, gx_ref, dg_ref, db_ref, acc):
        i = pl.program_id(0)
        k = pl.program_id(1)

        @pl.when(k == 0)
        def _():
            acc[...] = _dot(a_ref[...], w_ref[...])

        @pl.when(k > 0)
        def _():
            acc[...] += _dot(a_ref[...], w_ref[...])

        @pl.when(jnp.logical_and(i == 0, k == 0))
        def _():
            dg_ref[...] = jnp.zeros_like(dg_ref)
            db_ref[...] = jnp.zeros_like(db_ref)

        @pl.when(k == nk - 1)
        def _():
            g = g_ref[...]

            def chunk(rows):
                dy = ALPHA * d1_ref[rows, :] + acc[rows, :]
                xhat, rstd = _ln_stats(x_ref[rows, :])
                dg_ref[...] += _rows8(dy * xhat)
                db_ref[...] += _rows8(dy)
                gx_ref[rows, :] = _ln_bwd(dy, xhat, rstd, g)

            _row_chunks(tm, chunk)

    tok = pl.BlockSpec((tm, d), lambda i, k: (i, 0))
    accs = pl.BlockSpec((SUB, d), lambda i, k: (0, 0))
    return _call(
        body, name="in_proj_bwd_ln", grid=(s // tm, nk),
        in_specs=[pl.BlockSpec((tm, tk), lambda i, k: (i, k)), pl.BlockSpec((tk, d), lambda i, k: (k, 0)),
                  tok, tok, pl.BlockSpec((1, d), lambda i, k: (0, 0))],
        out_specs=[tok, accs, accs], out_shape=[_sds((s, d), F32), _sds((SUB, d), F32), _sds((SUB, d), F32)],
        scratch_shapes=[pltpu.VMEM((tm, d), F32)], compiler_params=_cp(("arbitrary", "arbitrary")),
    )(dh, wint, dr1, x, g_in)


def _adamw_math(w, g, m, v):
    m = ADAM_B1 * m + (1.0 - ADAM_B1) * g
    v = ADAM_B2 * v + (1.0 - ADAM_B2) * (g * g)
    m_hat = m / (1.0 - ADAM_B1 ** ADAM_STEP)
    v_hat = v / (1.0 - ADAM_B2 ** ADAM_STEP)
    delta = -ADAM_LR * (m_hat / (jnp.sqrt(v_hat) + ADAM_EPS) + ADAM_WD * w)
    return delta, m, v


def adamw(name, w, g, m, v):
    r, c = w.shape
    tr = _row_tile(r, c)

    def body(w_ref, g_ref, m_ref, v_ref, d_ref, mo_ref, vo_ref):
        d_ref[...], mo_ref[...], vo_ref[...] = _adamw_math(w_ref[...], g_ref[...], m_ref[...], v_ref[...])

    blk = pl.BlockSpec((tr, c), lambda i: (i, 0))
    return _call(
        body, name=name, grid=(r // tr,), in_specs=[blk] * 4, out_specs=[blk] * 3,
        out_shape=[_sds((r, c), F32)] * 3, compiler_params=_cp(("arbitrary",)),
    )(w, g, m, v)


def _coords():
    return lax.axis_index("x"), lax.axis_index("y"), lax.axis_index("c")


def _other_chips(x, y):
    return [(1 - x, y, 2 * (1 - x) + y), (x, 1 - y, 2 * x + 1 - y), (1 - x, 1 - y, 2 * (1 - x) + 1 - y)]


ANY = pl.BlockSpec(memory_space=pl.ANY)
HBM = pl.BlockSpec(memory_space=pltpu.HBM)
SEM = pl.BlockSpec(memory_space=pltpu.SEMAPHORE)
EFFECT = pltpu.SideEffectType.DATAFLOW_SIDE_EFFECTING


def _in_hbm(a):
    return pltpu.with_memory_space_constraint(a, pltpu.HBM)


def _split_refs(own, src, land, me, j, pk):
    if own:
        return src, land.at[me], land.at[pk]
    return src.at[pk], land.at[j], land.at[j]


def chips_send_start(name, own, srcs, land_shapes, order_after):
    n = len(srcs)

    def body(*refs):
        ins, lands = refs[:n], refs[n:2 * n]
        ss, rs = refs[2 * n + 1], refs[2 * n + 2]
        token = refs[-1]
        x, y, c = _coords()
        me = 2 * x + y
        for a in range(n):
            for j, (px, py, pk) in enumerate(_other_chips(x, y)):
                src, dst, _ = _split_refs(own, ins[a], lands[a], me, j, pk)
                pltpu.make_async_remote_copy(src_ref=src, dst_ref=dst, send_sem=ss.at[3 * a + j], recv_sem=rs.at[3 * a + j],
                                             device_id=(px, py, c), device_id_type=MESH).start()
        token[...] = jnp.zeros_like(token)

    lands = [lax.empty(shp, s.dtype) for shp, s in zip(land_shapes, srcs)]
    outs = _call(
        body, name=name,
        out_shape=(pltpu.SemaphoreType.DMA((3 * n,)), pltpu.SemaphoreType.DMA((3 * n,)),
                   *[pltpu.HBM(s.shape, s.dtype) for s in srcs], *[pltpu.HBM(l.shape, l.dtype) for l in lands],
                   _sds((SUB, LANE), F32)),
        in_specs=[HBM] * (2 * n) + [ANY], out_specs=(SEM, SEM, *[HBM] * (2 * n), pl.BlockSpec(memory_space=pltpu.VMEM)),
        input_output_aliases={a: 2 + a for a in range(2 * n)},
        compiler_params=pltpu.CompilerParams(has_side_effects=EFFECT),
    )(*[_in_hbm(s) for s in srcs], *[_in_hbm(l) for l in lands], order_after)
    return outs[0], outs[1], list(outs[2:2 + n]), list(outs[2 + n:2 + 2 * n]), outs[-1]


def chips_send_wait(name, own, ss, rs, srcs, lands, order_after):
    n = len(srcs)

    def body(*refs):
        ins, lnd = refs[:n], refs[n:2 * n]
        s_ref, r_ref = refs[2 * n], refs[2 * n + 1]
        x, y, c = _coords()
        me = 2 * x + y
        for a in range(n):
            for j, (px, py, pk) in enumerate(_other_chips(x, y)):
                src, _, got = _split_refs(own, ins[a], lnd[a], me, j, pk)
                cp = pltpu.make_async_remote_copy(src_ref=src, dst_ref=got, send_sem=s_ref.at[3 * a + j], recv_sem=r_ref.at[3 * a + j],
                                                  device_id=(px, py, c), device_id_type=MESH)
                cp.wait_send()
                cp.wait_recv()

    outs = _call(
        body, name=name, out_shape=tuple(pltpu.HBM(t.shape, t.dtype) for t in (*srcs, *lands)),
        in_specs=[HBM] * (2 * n) + [SEM, SEM, ANY], out_specs=tuple([HBM] * (2 * n)),
        input_output_aliases={a: a for a in range(2 * n)},
        compiler_params=pltpu.CompilerParams(has_side_effects=EFFECT),
    )(*srcs, *lands, ss, rs, order_after)
    return list(outs[:n]), list(outs[n:])


def all_gather_shards(shards):
    n = len(shards)

    def body(*refs):
        ins, outs = refs[:n], refs[n:2 * n]
        ici_s, ici_r, d2d_s, d2d_r = refs[2 * n:]
        x, y, c = _coords()
        me = 2 * x + y
        peers = _other_chips(x, y)
        sends, fwds = [], []
        for a in range(n):
            rh = ins[a].shape[0] // 2
            mine = pl.ds(c * rh, rh)
            for j, (px, py, pk) in enumerate(peers):
                cp = pltpu.make_async_remote_copy(
                    src_ref=ins[a].at[mine], dst_ref=outs[a].at[me, mine], send_sem=ici_s.at[a, j], recv_sem=ici_r.at[a, j],
                    device_id=(px, py, c), device_id_type=MESH)
                cp.start()
                sends.append(cp)
        for a in range(n):
            rh = ins[a].shape[0] // 2
            mine = pl.ds(c * rh, rh)
            for j, (px, py, pk) in enumerate(peers):
                got = outs[a].at[pk, mine]
                pltpu.make_async_remote_copy(
                    src_ref=got, dst_ref=got, send_sem=ici_s.at[a, j], recv_sem=ici_r.at[a, j],
                    device_id=(px, py, c), device_id_type=MESH).wait_recv()
                fw = pltpu.make_async_remote_copy(
                    src_ref=got, dst_ref=got, send_sem=d2d_s.at[a, j], recv_sem=d2d_r.at[a, j],
                    device_id=(x, y, 1 - c), device_id_type=MESH)
                fw.start()
                fwds.append(fw)
        for a in range(n):
            rh = ins[a].shape[0] // 2
            theirs = pl.ds((1 - c) * rh, rh)
            for j, (px, py, pk) in enumerate(peers):
                got = outs[a].at[pk, theirs]
                pltpu.make_async_remote_copy(
                    src_ref=got, dst_ref=got, send_sem=d2d_s.at[a, j], recv_sem=d2d_r.at[a, j],
                    device_id=(x, y, 1 - c), device_id_type=MESH).wait_recv()
        for cp in sends + fwds:
            cp.wait_send()

    got = _call(
        body, name="all_gather_shards", in_specs=[ANY] * n, out_specs=[ANY] * n,
        out_shape=[_sds((N_CHIP,) + w.shape, w.dtype) for w in shards],
        scratch_shapes=[pltpu.SemaphoreType.DMA((n, 3))] * 4,
    )(*shards)
    me = 2 * lax.axis_index("x") + lax.axis_index("y")
    return [lax.dynamic_update_slice(g, w[None], (me, 0, 0)) for g, w in zip(got, shards)]


def pair_exchange(grads, tag):
    n = len(grads)

    def body(*refs):
        ins, outs = refs[:n], refs[n:2 * n]
        ss, rs = refs[2 * n:]
        x, y, c = _coords()
        cps = []
        for a in range(n):
            rh = ins[a].shape[1] // 2
            cp = pltpu.make_async_remote_copy(
                src_ref=ins[a].at[:, pl.ds((1 - c) * rh, rh)], dst_ref=outs[a], send_sem=ss.at[a], recv_sem=rs.at[a],
                device_id=(x, y, 1 - c), device_id_type=MESH)
            cp.start()
            cps.append(cp)
        for cp in cps:
            cp.wait()

    return _call(
        body, name="pair_exchange_" + tag, in_specs=[ANY] * n, out_specs=[ANY] * n,
        out_shape=[_sds((N_CHIP, g.shape[1] // 2, g.shape[2]), F32) for g in grads],
        scratch_shapes=[pltpu.SemaphoreType.DMA((n,))] * 2,
    )(*grads)


def _row_tile(rows, cols, itemsize=4, budget=2 * VMEM_MB):
    t = rows
    while t * cols * itemsize > budget and t % (2 * SUB) == 0:
        t //= 2
    return t


def pair_add(g, r, cidx):
    _, rows, cols = g.shape
    rh = rows // 2
    tr = _row_tile(rh, cols)
    per = rh // tr

    def body(c_ref, g_ref, r_ref, o_ref):
        o_ref[...] = g_ref[...] + r_ref[...]

    return _call(
        body, name="pair_add",
        grid_spec=pltpu.PrefetchScalarGridSpec(
            num_scalar_prefetch=1, grid=(N_CHIP, per),
            in_specs=[pl.BlockSpec((None, tr, cols), lambda k, i, c: (k, c[0] * per + i, 0)),
                      pl.BlockSpec((None, tr, cols), lambda k, i, c: (k, i, 0))],
            out_specs=pl.BlockSpec((None, tr, cols), lambda k, i, c: (k, i, 0))),
        out_shape=_sds((N_CHIP, rh, cols), F32), compiler_params=_cp(("arbitrary", "arbitrary")),
    )(cidx, g, r)


def chip_exchange(parts):
    n = len(parts)

    def body(*refs):
        ins, outs = refs[:n], refs[n:2 * n]
        ss, rs = refs[2 * n:]
        x, y, c = _coords()
        cps = []
        for a in range(n):
            for j, (px, py, pk) in enumerate(_other_chips(x, y)):
                cp = pltpu.make_async_remote_copy(
                    src_ref=ins[a].at[pk], dst_ref=outs[a].at[j], send_sem=ss.at[a, j], recv_sem=rs.at[a, j],
                    device_id=(px, py, c), device_id_type=MESH)
                cp.start()
                cps.append(cp)
        for cp in cps:
            cp.wait()

    return _call(
        body, name="chip_exchange", in_specs=[ANY] * n, out_specs=[ANY] * n,
        out_shape=[_sds((N_CHIP - 1,) + p.shape[1:], F32) for p in parts],
        scratch_shapes=[pltpu.SemaphoreType.DMA((n, 3))] * 2,
    )(*parts)


def chip_add(p, r, kc):
    _, rh, cols = p.shape
    tr = _row_tile(rh, cols)
    per = rh // tr

    def body(k_ref, p_ref, r_ref, o_ref):
        o_ref[...] = ((p_ref[...] + r_ref[0]) + r_ref[1]) + r_ref[2]

    return _call(
        body, name="chip_add",
        grid_spec=pltpu.PrefetchScalarGridSpec(
            num_scalar_prefetch=1, grid=(per,),
            in_specs=[pl.BlockSpec((None, tr, cols), lambda i, k: (k[0], i, 0)),
                      pl.BlockSpec((N_CHIP - 1, tr, cols), lambda i, k: (0, i, 0))],
            out_specs=pl.BlockSpec((tr, cols), lambda i, k: (k[1] * per + i, 0))),
        out_shape=_sds((2 * rh, cols), F32), compiler_params=_cp(("arbitrary",)),
    )(kc, p, r)


def pair_share(fulls, tag):
    n = len(fulls)

    def body(*refs):
        outs = refs[n:2 * n]
        ss, rs = refs[2 * n:]
        x, y, c = _coords()
        cps = []
        for a in range(n):
            rh = outs[a].shape[0] // 2
            mine = outs[a].at[pl.ds(c * rh, rh)]
            cp = pltpu.make_async_remote_copy(
                src_ref=mine, dst_ref=mine, send_sem=ss.at[a], recv_sem=rs.at[a],
                device_id=(x, y, 1 - c), device_id_type=MESH)
            cp.start()
            cps.append(cp)
        for a, cp in enumerate(cps):
            rh = outs[a].shape[0] // 2
            theirs = outs[a].at[pl.ds((1 - c) * rh, rh)]
            cp.wait_send()
            pltpu.make_async_remote_copy(
                src_ref=theirs, dst_ref=theirs, send_sem=ss.at[a], recv_sem=rs.at[a],
                device_id=(x, y, 1 - c), device_id_type=MESH).wait_recv()

    return _call(
        body, name="pair_share_" + tag, in_specs=[ANY] * n, out_specs=[ANY] * n,
        out_shape=[_sds(f.shape, F32) for f in fulls], input_output_aliases={a: a for a in range(n)},
        scratch_shapes=[pltpu.SemaphoreType.DMA((n,))] * 2,
    )(*fulls)


def small_allreduce_adamw(part, w, m, v):
    n = part.shape[1]

    def body(p_ref, w_ref, m_ref, v_ref, g_ref, d_ref, mo_ref, vo_ref, mine, gath, ss, rs):
        x, y, c = _coords()
        me = 4 * x + 2 * y + c
        mine[...] = jnp.sum(p_ref[...], axis=0, keepdims=True)
        gath[me] = mine[...]
        cps = []
        for k in range(1, 8):
            px, py, pc = x ^ (k >> 2), y ^ ((k >> 1) & 1), c ^ (k & 1)
            cp = pltpu.make_async_remote_copy(
                src_ref=mine, dst_ref=gath.at[me], send_sem=ss.at[k - 1], recv_sem=rs.at[k - 1],
                device_id=(px, py, pc), device_id_type=MESH)
            cp.start()
            cps.append(cp)
        for k in range(1, 8):
            src = 4 * (x ^ (k >> 2)) + 2 * (y ^ ((k >> 1) & 1)) + (c ^ (k & 1))
            pltpu.make_async_remote_copy(
                src_ref=mine, dst_ref=gath.at[src], send_sem=ss.at[k - 1], recv_sem=rs.at[k - 1],
                device_id=(x, y, c), device_id_type=MESH).wait_recv()
        for cp in cps:
            cp.wait_send()
        g = gath[0]
        for dv in range(1, 8):
            g = g + gath[dv]
        g_ref[...] = g
        d_ref[...], mo_ref[...], vo_ref[...] = _adamw_math(w_ref[...], g, m_ref[...], v_ref[...])

    vm = pl.BlockSpec(memory_space=pltpu.VMEM)
    return _call(
        body, name="small_allreduce_adamw", in_specs=[vm] * 4, out_specs=[vm] * 4, out_shape=[_sds((1, n), F32)] * 4,
        scratch_shapes=[pltpu.VMEM((1, n), F32), pltpu.VMEM((8, 1, n), F32),
                        pltpu.SemaphoreType.DMA((7,)), pltpu.SemaphoreType.DMA((7,))],
    )(part, w, m, v)


def _unshard_cols(g):
    k, r, cs = g.shape
    return g.transpose(1, 0, 2).reshape(r, k * cs)


def _shard_cols(w):
    r, c = w.shape
    return w.reshape(r, N_CHIP, c // N_CHIP).transpose(1, 0, 2)


def local_step(x, positions, ln_in_g, ln_in_b, win_g, g_cq, wuq_g, g_ckv, wuk_g, wuv_g, convw_g, conv_b, g_conv_ln,
               b_conv_ln, late_weights, g_ln1, b_ln1, g_ln2, b_ln2, target, start_token, early_grads):
    s, d = x.shape
    c = d - MLA_W
    row = lambda a: a.reshape(1, -1)
    ln_in_g = row(ln_in_g) + start_token[0:1, 0:1]

    win = _unshard_cols(win_g)
    o_kr = R_Q + R_KV
    o_cv = o_kr + D_ROPE
    win_r = jnp.concatenate([win[:, :o_kr], win[:, o_cv:], win[:, o_kr:o_cv], jnp.zeros((d, LANE - D_ROPE), BF)], axis=1)
    hc = win_r.shape[1]
    kr_blk = (o_kr + 2 * c) // LANE
    wuq = _unshard_cols(wuq_g).reshape(R_Q, HEADS, D_QK)
    wuq_r = jnp.concatenate([wuq[:, :, :D_NOPE].reshape(R_Q, MLA_W),
                             jnp.pad(wuq[:, :, D_NOPE:], ((0, 0), (0, 0), (0, LANE - D_ROPE))).reshape(R_Q, MLA_W)], axis=1)
    wuk = _unshard_cols(wuk_g)
    wuv = _unshard_cols(wuv_g)
    conv_w = jnp.pad(_unshard_cols(convw_g), ((0, 1), (0, 0)))

    half = D_ROPE // 2
    inv_freq = ROPE_BASE ** (-jnp.arange(half, dtype=F32) * (2.0 / D_ROPE))
    invf = jnp.concatenate([inv_freq, inv_freq, jnp.zeros((LANE - D_ROPE,), F32)]).reshape(1, LANE)
    cos, sin = rope_tables(positions.astype(F32).reshape(s, 1), invf)
    x0, x0b = ln_in_fwd(x, ln_in_g, row(ln_in_b))
    h = matmul("in_proj", x0b, win_r, 512, 640)
    qc, cqn = q_proj(h, g_cq, wuq_r, cos, sin)
    kc, v, ckvn = kv_proj(h, g_ckv, wuk, wuv, cos, sin, kr_blk)
    o, ob, lse = attn_fwd(qc, kc, v)
    co, uc = conv_fwd(h, conv_w, conv_b, g_conv_ln, b_conv_ln)
    wout_g, wff1_g, wff2_g = late_weights(ob)
    wout = wout_g.reshape(d, d)
    wff2 = wff2_g.reshape(-1, d)
    wff1t = wff1_g.transpose(0, 2, 1).reshape(-1, d)
    wff2t = wff2.T
    r1, x1, x1b = out_proj_ln1(ob, co, wout, x0, g_ln1, b_ln1)
    rb, a1b = ff1_fwd(x1b, wff1_g)
    dr2, dr2b, loss8, dg2, db2 = ff2_ln2_loss(a1b, wff2, x1, target, g_ln2, b_ln2)

    df1b = ff2_bwd_act(dr2b, wff2t, rb)
    gw_ff2 = wgrad("wgrad_ff2", a1b, dr2b, 1024, 1024).reshape(N_CHIP, -1, d)
    gw_ff1 = wgrad("wgrad_ff1", x1b, df1b, 1024, 1024, shards=N_CHIP)
    sent = early_grads(gw_ff2, gw_ff1)
    dr1, dr1b, dg1, db1 = ff1_bwd_ln1(df1b, wff1t, dr2, r1, g_ln1 + sent[0:1, 0:1])
    gw_out = jnp.concatenate([wgrad("wgrad_out_attn", ob, dr1b, 1024, 1024)[0],
                              wgrad("wgrad_out_conv", co, dr1b, 1024, 1024)[0]], axis=0).reshape(N_CHIP, -1, d)
    dob, dco, delta = out_proj_bwd(dr1b, wout.T, o)
    duc, dgc, dbc, dcb = conv_bwd_ln(uc, dco, g_conv_ln, b_conv_ln)
    dconv, gconvw = conv_bwd_taps(h, duc, conv_w)
    dqt, dk, dv = attn_bwd(qc, kc, kc.transpose(0, 2, 1), v, dob, lse.reshape(HEADS, 1, s), delta.reshape(HEADS, 1, s))
    dqb, dcq, dgq = q_bwd(dqt, h, g_cq, wuq_r.T, cos, sin)
    dknb, dvb, dckv, dkr, dgkv = kv_bwd(dk, dv, h, g_ckv, wuk.T, wuv.T, cos, sin)
    gwuq_r = wgrad("wgrad_uq", cqn, dqb, 512, 1024)[0]
    gw_uk = wgrad("wgrad_uk", ckvn, dknb, 512, 1024, shards=N_CHIP)
    gw_uv = wgrad("wgrad_uv", ckvn, dvb, 512, 1024, shards=N_CHIP)
    dh = jnp.concatenate([dcq, dckv, dconv, dkr], axis=1)
    gx, dgin, dbin = in_proj_bwd_ln(dh, win_r.T, dr1, x, ln_in_g)
    gwin_r = wgrad("wgrad_in", x0b, dh, 1024, 640)[0]

    gwin = jnp.concatenate([gwin_r[:, :o_kr], gwin_r[:, o_kr + 2 * c:o_kr + 2 * c + D_ROPE], gwin_r[:, o_kr:o_kr + 2 * c]], axis=1)
    gwuq = jnp.concatenate([gwuq_r[:, :MLA_W].reshape(R_Q, HEADS, D_NOPE),
                            gwuq_r[:, MLA_W:].reshape(R_Q, HEADS, LANE)[:, :, :D_ROPE]], axis=2).reshape(R_Q, HEADS * D_QK)
    big = dict(w_in=_shard_cols(gwin), w_uq=_shard_cols(gwuq), w_uk=gw_uk, w_uv=gw_uv, conv_w=_shard_cols(gconvw), w_out=gw_out)
    small = jnp.concatenate([dgin, dbin, dgq, dgkv, dcb, dgc, dbc, dg1, db1, dg2, db2, loss8], axis=1)
    return gx, big, small


BIG = ["w_in", "w_uq", "w_uk", "w_uv", "conv_w", "w_out", "w_ff1", "w_ff2"]
EARLY = ["w_in", "w_uq", "w_uk", "w_uv", "conv_w"]
LATE = ["w_out", "w_ff1", "w_ff2"]
SMALL = ["ln_in_g", "ln_in_b", "g_cq", "g_ckv", "conv_b", "g_conv_ln", "b_conv_ln", "g_ln1", "b_ln1", "g_ln2", "b_ln2"]
WEIGHTS = ["ln_in_g", "ln_in_b", "w_in", "g_cq", "w_uq", "g_ckv", "w_uk", "w_uv", "conv_w", "conv_b", "g_conv_ln",
           "b_conv_ln", "w_out", "g_ln1", "b_ln1", "w_ff1", "w_ff2", "g_ln2", "b_ln2"]


def _pad_rows(a, rows):
    return jnp.pad(a, ((0, rows - a.shape[0]), (0, 0)))


def kernel(x, positions, ln_in_g, ln_in_b, w_in, g_cq, w_uq, g_ckv, w_uk, w_uv, conv_w, conv_b, g_conv_ln, b_conv_ln, w_out, g_ln1, b_ln1, w_ff1, w_ff2, g_ln2, b_ln2, loss_target, m_ln_in_g, m_ln_in_b, m_w_in, m_g_cq, m_w_uq, m_g_ckv, m_w_uk, m_w_uv, m_conv_w, m_conv_b, m_g_conv_ln, m_b_conv_ln, m_w_out, m_g_ln1, m_b_ln1, m_w_ff1, m_w_ff2, m_g_ln2, m_b_ln2, v_ln_in_g, v_ln_in_b, v_w_in, v_g_cq, v_w_uq, v_g_ckv, v_w_uk, v_w_uv, v_conv_w, v_conv_b, v_g_conv_ln, v_b_conv_ln, v_w_out, v_g_ln1, v_b_ln1, v_w_ff1, v_w_ff2, v_g_ln2, v_b_ln2):
    w = dict(ln_in_g=ln_in_g, ln_in_b=ln_in_b, w_in=w_in, g_cq=g_cq, w_uq=w_uq, g_ckv=g_ckv, w_uk=w_uk, w_uv=w_uv,
             conv_w=conv_w, conv_b=conv_b, g_conv_ln=g_conv_ln, b_conv_ln=b_conv_ln, w_out=w_out, g_ln1=g_ln1,
             b_ln1=b_ln1, w_ff1=w_ff1, w_ff2=w_ff2, g_ln2=g_ln2, b_ln2=b_ln2)
    m = dict(ln_in_g=m_ln_in_g, ln_in_b=m_ln_in_b, w_in=m_w_in, g_cq=m_g_cq, w_uq=m_w_uq, g_ckv=m_g_ckv, w_uk=m_w_uk,
             w_uv=m_w_uv, conv_w=m_conv_w, conv_b=m_conv_b, g_conv_ln=m_g_conv_ln, b_conv_ln=m_b_conv_ln, w_out=m_w_out,
             g_ln1=m_g_ln1, b_ln1=m_b_ln1, w_ff1=m_w_ff1, w_ff2=m_w_ff2, g_ln2=m_g_ln2, b_ln2=m_b_ln2)
    v = dict(ln_in_g=v_ln_in_g, ln_in_b=v_ln_in_b, w_in=v_w_in, g_cq=v_g_cq, w_uq=v_w_uq, g_ckv=v_g_ckv, w_uk=v_w_uk,
             w_uv=v_w_uv, conv_w=v_conv_w, conv_b=v_conv_b, g_conv_ln=v_g_conv_ln, b_conv_ln=v_b_conv_ln, w_out=v_w_out,
             g_ln1=v_g_ln1, b_ln1=v_b_ln1, w_ff1=v_w_ff1, w_ff2=v_w_ff2, g_ln2=v_g_ln2, b_ln2=v_b_ln2)

    sh2 = {n: w[n][0] for n in BIG}
    cidx = lax.axis_index("c").astype(jnp.int32).reshape(1)
    me = 2 * lax.axis_index("x") + lax.axis_index("y")
    kc = jnp.stack([me, lax.axis_index("c")]).astype(jnp.int32)

    early = [sh2[n].astype(BF) if n != "conv_w" else _pad_rows(sh2[n], CONV_K + 1) for n in EARLY]
    gw = dict(zip(EARLY, all_gather_shards(early)))
    gw["conv_w"] = gw["conv_w"][:, :CONV_K]
    late = [sh2[n].astype(BF) for n in LATE]
    ag_ss, ag_rs, ag_src, ag_land, ag_token = chips_send_start(
        "late_weights_start", True, late, [(N_CHIP,) + a.shape for a in late], gw["w_uq"])

    def late_weights(after):
        mine, lands = chips_send_wait("late_weights_wait", True, ag_ss, ag_rs, ag_src, ag_land, after)
        return [lax.dynamic_update_slice(g, a[None], (me, 0, 0)) for g, a in zip(lands, mine)]

    sent = {}

    def early_grads(gw_ff2, gw_ff1):
        full = [gw_ff2, gw_ff1]
        psum = [pair_add(g, r, cidx) for g, r in zip(full, pair_exchange(full, "ff"))]
        ss, rs, src, land, token = chips_send_start(
            "ff_grads_start", False, psum, [(N_CHIP - 1,) + p.shape[1:] for p in psum], psum[0])
        sent.update(ss=ss, rs=rs, src=src, land=land)
        return token

    gx, big, small = local_step(
        x[0], positions[0], ln_in_g, ln_in_b, gw["w_in"], g_cq, gw["w_uq"], g_ckv, gw["w_uk"], gw["w_uv"], gw["conv_w"],
        conv_b, g_conv_ln, b_conv_ln, late_weights, g_ln1, b_ln1, g_ln2, b_ln2, loss_target[0], ag_token, early_grads)

    rest = [n for n in BIG if n not in ("w_ff2", "w_ff1")]
    full = [big[n] for n in rest]
    psum = [pair_add(g, r, cidx) for g, r in zip(full, pair_exchange(full, "rest"))]
    got = chip_exchange(psum)
    ff_psum, ff_got = chips_send_wait("ff_grads_wait", False, sent["ss"], sent["rs"], sent["src"], sent["land"], got[0])
    summed = [chip_add(p, r, kc) for p, r in zip(psum + ff_psum, list(got) + ff_got)]
    gsh = dict(zip(rest + ["w_ff2", "w_ff1"], pair_share(summed, "all")))
    gsh["conv_w"] = gsh["conv_w"][:CONV_K]

    grad, delta, new_m, new_v = {}, {}, {}, {}
    for n in BIG:
        grad[n] = gsh[n][None]
        d_, m_, v_ = adamw("adamw_" + n, sh2[n], gsh[n], m[n][0], v[n][0])
        delta[n], new_m[n], new_v[n] = d_[None], m_[None], v_[None]

    flat = lambda t: jnp.concatenate([t[n].reshape(1, -1) for n in SMALL] + [jnp.zeros((1, LANE), F32)], axis=1)
    g_s, d_s, m_s, v_s = small_allreduce_adamw(small, flat(w), flat(m), flat(v))
    off = 0
    for n in SMALL:
        sz = w[n].size
        for dst, src in ((grad, g_s), (delta, d_s), (new_m, m_s), (new_v, v_s)):
            dst[n] = src[0, off:off + sz].reshape(w[n].shape)
        off += sz
    loss = jnp.sum(g_s[0, off:off + LANE])

    return (loss, gx[None], *[grad[n] for n in WEIGHTS], *[delta[n] for n in WEIGHTS],
            *[new_m[n] for n in WEIGHTS], *[new_v[n] for n in WEIGHTS])
```

```python
import functools

import jax
import jax.numpy as jnp
from jax import lax
from jax.experimental import pallas as pl
from jax.experimental.pallas import tpu as pltpu

F32 = jnp.float32
BF = jnp.bfloat16

HEADS = 8
D_NOPE = 128
D_ROPE = 64
D_V = 128
D_QK = D_NOPE + D_ROPE
R_Q = 512
R_KV = 512
MLA_W = HEADS * D_V
CONV_K = 31
CONV_PAD = CONV_K // 2
ROPE_BASE = 10000.0
LOG2E = 1.4426950408889634
LN2 = 0.6931471805599453
LN_EPS = 1e-5
RMS_EPS = 1e-6
ALPHA = (2.0 * 1) ** 0.25
ADAM_LR = 0.001
ADAM_B1 = 0.9
ADAM_B2 = 0.999
ADAM_EPS = 1e-08
ADAM_WD = 0.01
ADAM_STEP = 10

LANE = 128
SUB = 8
HALO = 16
N_CHIP = 4
MESH = pl.DeviceIdType.MESH
VMEM_MB = 1024 * 1024


def _call(body, **kw):
    return pl.pallas_call(body, **kw)


def _cp(sem, mb=48):
    return pltpu.CompilerParams(dimension_semantics=sem, vmem_limit_bytes=mb * VMEM_MB)


def _sds(shape, dt):
    return jax.ShapeDtypeStruct(shape, dt)


def _dot(a, b):
    return jnp.dot(a, b, preferred_element_type=F32)


def _dot_nt(a, b):
    return lax.dot_general(a, b, (((1,), (1,)), ((), ())), preferred_element_type=F32)


def _dot_tn(a, b):
    return lax.dot_general(a, b, (((0,), (0,)), ((), ())), preferred_element_type=F32)


def _rows8(v):
    t, n = v.shape
    return v.reshape(t // SUB, SUB, n).sum(axis=0)


def _ln_stats(r):
    mu = jnp.mean(r, axis=-1, keepdims=True)
    xc = r - mu
    var = jnp.mean(xc * xc, axis=-1, keepdims=True)
    rstd = lax.rsqrt(var + LN_EPS)
    return xc * rstd, rstd


def _ln_bwd(dy, xhat, rstd, g):
    dyh = dy * g
    m1 = jnp.mean(dyh, axis=-1, keepdims=True)
    m2 = jnp.mean(dyh * xhat, axis=-1, keepdims=True)
    return rstd * (dyh - m1 - xhat * m2)


def _rms_fwd(x, g):
    rr = lax.rsqrt(jnp.mean(x * x, axis=-1, keepdims=True) + RMS_EPS)
    xh = x * rr
    return xh * g, xh, rr


def _rms_bwd(dy, xh, rr, g):
    dyg = dy * g
    return rr * (dyg - xh * jnp.mean(dyg * xh, axis=-1, keepdims=True))


def _rope128(x, cos, sin_signed):
    lane = lax.broadcasted_iota(jnp.int32, x.shape, 1)
    rot = jnp.where(lane < D_ROPE // 2, pltpu.roll(x, LANE - D_ROPE // 2, 1), pltpu.roll(x, D_ROPE // 2, 1))
    return x * cos + rot * sin_signed


def _unrope128(dy, cos, sin_signed):
    t = dy * sin_signed
    lane = lax.broadcasted_iota(jnp.int32, dy.shape, 1)
    rot = jnp.where(lane < D_ROPE // 2, pltpu.roll(t, LANE - D_ROPE // 2, 1), pltpu.roll(t, D_ROPE // 2, 1))
    return dy * cos + rot


def _sigmoid(x):
    return 1.0 / (1.0 + jnp.exp(-x))


def _row_chunks(tm, fn, rc=128):
    rc = min(rc, tm)

    def step(ci, carry):
        fn(pl.ds(pl.multiple_of(ci * rc, rc), rc))
        return carry

    lax.fori_loop(0, tm // rc, step, 0)


def _unrolled_loop(n, unroll, fn, init):
    unroll = min(n, unroll)
    assert n % unroll == 0

    def body(t, carry):
        for u in range(unroll):
            carry = fn(t * unroll + u, carry)
        return carry

    return lax.fori_loop(0, n // unroll, body, init)


def _tile(s, want):
    t = min(s, want)
    assert s % t == 0
    return t


def rope_tables(pos_f, invf):
    s = pos_f.shape[0]
    tm = _tile(s, 1024)

    def body(p_ref, f_ref, c_ref, s_ref):
        ang = p_ref[...] * f_ref[...]
        lane = lax.broadcasted_iota(jnp.int32, ang.shape, 1)
        c = jnp.cos(ang)
        sn = jnp.sin(ang)
        c_ref[...] = jnp.where(lane < D_ROPE, c, 0.0)
        s_ref[...] = jnp.where(lane < D_ROPE // 2, -sn, jnp.where(lane < D_ROPE, sn, 0.0))

    return _call(
        body, name="rope_tables", grid=(s // tm,),
        in_specs=[pl.BlockSpec((tm, 1), lambda i: (i, 0)), pl.BlockSpec((1, LANE), lambda i: (0, 0))],
        out_specs=[pl.BlockSpec((tm, LANE), lambda i: (i, 0))] * 2,
        out_shape=[_sds((s, LANE), F32)] * 2,
        compiler_params=_cp(("arbitrary",)),
    )(pos_f, invf)


def ln_in_fwd(x, g, b):
    s, d = x.shape
    tm = _tile(s, 512)

    def body(x_ref, g_ref, b_ref, o_ref, ob_ref):
        xhat, _ = _ln_stats(x_ref[...])
        y = xhat * g_ref[...] + b_ref[...]
        o_ref[...] = y
        ob_ref[...] = y.astype(BF)

    row = pl.BlockSpec((1, d), lambda i: (0, 0))
    tok = pl.BlockSpec((tm, d), lambda i: (i, 0))
    return _call(
        body, name="ln_in_fwd", grid=(s // tm,), in_specs=[tok, row, row], out_specs=[tok, tok],
        out_shape=[_sds((s, d), F32), _sds((s, d), BF)], compiler_params=_cp(("arbitrary",)),
    )(x, g, b)


def matmul(name, a, w, tm, tn, out_dtype=F32):
    s, k = a.shape
    n = w.shape[1]
    tm = _tile(s, tm)
    tn = _tile(n, tn)

    def body(a_ref, w_ref, o_ref):
        o_ref[...] = _dot(a_ref[...], w_ref[...]).astype(o_ref.dtype)

    return _call(
        body, name=name, grid=(s // tm, n // tn),
        in_specs=[pl.BlockSpec((tm, k), lambda i, j: (i, 0)), pl.BlockSpec((k, tn), lambda i, j: (0, j))],
        out_specs=pl.BlockSpec((tm, tn), lambda i, j: (i, j)),
        out_shape=_sds((s, n), out_dtype), compiler_params=_cp(("arbitrary", "arbitrary")),
    )(a, w)


def q_proj(h, g_cq, wuq, cos, sin):
    s = h.shape[0]
    tm = _tile(s, 512)

    def body(h_ref, g_ref, w_ref, c_ref, s_ref, q_ref, n_ref):
        y, _, _ = _rms_fwd(h_ref[...], g_ref[...])
        yb = y.astype(BF)
        n_ref[...] = yb
        q = _dot(yb, w_ref[...])
        c = c_ref[...]
        sn = s_ref[...]
        for hd in range(HEADS):
            q_ref[hd, :, 0:LANE] = q[:, LANE * hd:LANE * (hd + 1)].astype(BF)
            qr = q[:, MLA_W + LANE * hd:MLA_W + LANE * (hd + 1)]
            q_ref[hd, :, LANE:2 * LANE] = _rope128(qr, c, sn).astype(BF)

    return _call(
        body, name="q_proj", grid=(s // tm,),
        in_specs=[pl.BlockSpec((tm, R_Q), lambda i: (i, 0)), pl.BlockSpec((1, R_Q), lambda i: (0, 0)),
                  pl.BlockSpec((R_Q, 2 * MLA_W), lambda i: (0, 0)),
                  pl.BlockSpec((tm, LANE), lambda i: (i, 0)), pl.BlockSpec((tm, LANE), lambda i: (i, 0))],
        out_specs=[pl.BlockSpec((HEADS, tm, 2 * LANE), lambda i: (0, i, 0)), pl.BlockSpec((tm, R_Q), lambda i: (i, 0))],
        out_shape=[_sds((HEADS, s, 2 * LANE), BF), _sds((s, R_Q), BF)], compiler_params=_cp(("arbitrary",)),
    )(h, g_cq, wuq, cos, sin)


def kv_proj(h, g_ckv, wuk, wuv, cos, sin, kr_blk):
    s = h.shape[0]
    tm = _tile(s, 512)

    def body(h_ref, kr_ref, g_ref, wk_ref, wv_ref, c_ref, s_ref, k_ref, v_ref, n_ref):
        y, _, _ = _rms_fwd(h_ref[...], g_ref[...])
        yb = y.astype(BF)
        n_ref[...] = yb
        kn = _dot(yb, wk_ref[...])
        v = _dot(yb, wv_ref[...])
        kr = _rope128(kr_ref[...], c_ref[...], s_ref[...]).astype(BF)
        for hd in range(HEADS):
            k_ref[hd, :, 0:LANE] = kn[:, LANE * hd:LANE * (hd + 1)].astype(BF)
            k_ref[hd, :, LANE:2 * LANE] = kr
            v_ref[hd] = v[:, LANE * hd:LANE * (hd + 1)].astype(BF)

    tab = pl.BlockSpec((tm, LANE), lambda i: (i, 0))
    wsp = pl.BlockSpec((R_KV, MLA_W), lambda i: (0, 0))
    return _call(
        body, name="kv_proj", grid=(s // tm,),
        in_specs=[pl.BlockSpec((tm, R_KV), lambda i: (i, 1)), pl.BlockSpec((tm, LANE), lambda i: (i, kr_blk)),
                  pl.BlockSpec((1, R_KV), lambda i: (0, 0)), wsp, wsp, tab, tab],
        out_specs=[pl.BlockSpec((HEADS, tm, 2 * LANE), lambda i: (0, i, 0)),
                   pl.BlockSpec((HEADS, tm, LANE), lambda i: (0, i, 0)), pl.BlockSpec((tm, R_KV), lambda i: (i, 0))],
        out_shape=[_sds((HEADS, s, 2 * LANE), BF), _sds((HEADS, s, LANE), BF), _sds((s, R_KV), BF)],
        compiler_params=_cp(("arbitrary",)),
    )(h, h, g_ckv, wuk, wuv, cos, sin)


def attn_fwd(qc, kc, v):
    _, s, _ = qc.shape
    tq = _tile(s, 256)
    tk = _tile(s, 512)
    scale = D_QK ** -0.5
    c2 = scale * LOG2E
    nk = s // tk
    nb = tk // LANE
    un = 8

    def body(q_ref, k_ref, v_ref, o_ref, ob_ref, l_ref, s_scr, m_scr):
        q = q_ref[...]

        def scores(j, mpart):
            off = pl.multiple_of(j * tk, tk)
            sc = _dot_nt(q, k_ref[pl.ds(off, tk), :]) * c2
            s_scr[:, pl.ds(off, tk)] = sc
            for b in range(nb):
                mpart = jnp.maximum(mpart, sc[:, LANE * b:LANE * (b + 1)])
            return mpart

        mpart = _unrolled_loop(nk, un, scores, jnp.full((tq, LANE), -jnp.inf, F32))
        m = jnp.max(mpart, axis=-1, keepdims=True)
        m_scr[...] = jnp.broadcast_to(m, (tq, LANE))

        def weigh(j, carry):
            lpart, acc = carry
            off = pl.multiple_of(j * tk, tk)
            ps = []
            for b in range(nb):
                p = jnp.exp2(s_scr[:, pl.ds(off + LANE * b, LANE)] - m_scr[...])
                lpart = lpart + p
                ps.append(p.astype(BF))
            acc = acc + _dot(jnp.concatenate(ps, axis=1), v_ref[pl.ds(off, tk), :])
            return lpart, acc

        lpart, acc = _unrolled_loop(nk, un, weigh, (jnp.zeros((tq, LANE), F32), jnp.zeros((tq, D_V), F32)))
        l = jnp.sum(lpart, axis=-1, keepdims=True)
        o = acc / l
        o_ref[...] = o
        ob_ref[...] = o.astype(BF)
        l_ref[...] = m + jnp.log(l) * LOG2E

    return _call(
        body, name="attn_fwd", grid=(HEADS, s // tq),
        in_specs=[pl.BlockSpec((None, tq, 2 * LANE), lambda h, i: (h, i, 0)),
                  pl.BlockSpec((None, s, 2 * LANE), lambda h, i: (h, 0, 0)),
                  pl.BlockSpec((None, s, LANE), lambda h, i: (h, 0, 0))],
        out_specs=[pl.BlockSpec((tq, LANE), lambda h, i: (i, h)), pl.BlockSpec((tq, LANE), lambda h, i: (i, h)),
                   pl.BlockSpec((None, tq, 1), lambda h, i: (h, i, 0))],
        out_shape=[_sds((s, MLA_W), F32), _sds((s, MLA_W), BF), _sds((HEADS, s, 1), F32)],
        scratch_shapes=[pltpu.VMEM((tq, s + LANE), F32), pltpu.VMEM((tq, LANE), F32)],
        compiler_params=_cp(("arbitrary", "arbitrary")),
    )(qc, kc, v)


def _halo_specs(tm, s, width, col):
    r = tm // HALO
    nb = s // HALO
    cur = pl.BlockSpec((tm, width), lambda i: (i, col))
    prev = pl.BlockSpec((HALO, width), lambda i: (jnp.maximum(i * r - 1, 0), col))
    nxt = pl.BlockSpec((HALO, width), lambda i: (jnp.minimum((i + 1) * r, nb - 1), col))
    return cur, prev, nxt


def _slab_shapes(tm, c):
    return (tm + 2 * HALO, c + LANE), (SUB - 1, tm + 2 * HALO - SUB, c + LANE)


def _fill_slab(slab, tm, prev, cur, nxt):
    i = pl.program_id(0)
    last = pl.num_programs(0) - 1
    c = cur.shape[1]
    slab[0:HALO, 0:c] = jnp.where(i > 0, prev, 0.0)
    slab[HALO:HALO + tm, 0:c] = cur
    slab[HALO + tm:2 * HALO + tm, 0:c] = jnp.where(i < last, nxt, 0.0)


def _rotate_slab(slab, rot, tm):
    rows = tm + 2 * HALO - SUB
    c = slab.shape[1] - LANE
    for b in range(1, SUB):
        rot[b - 1, :, 0:c] = slab[pl.ds(b, rows), 0:c]


def _shifted(slab, rot, start, rc, cs):
    b = start % SUB
    if b == 0:
        return slab[pl.ds(start, rc), cs]
    return rot[b - 1, pl.ds(start - b, rc), cs]


def conv_fwd(h, conv_w, conv_b, g_ln, b_ln):
    s = h.shape[0]
    c = conv_w.shape[1]
    tm = _tile(s, 256)
    rc = _tile(tm, 64)

    def body(a_ref, ap_ref, an_ref, g_ref, gp_ref, gn_ref, w_ref, cb_ref, lg_ref, lb_ref, co_ref, uc_ref, slab, rot):
        _fill_slab(slab, tm, ap_ref[...] * _sigmoid(gp_ref[...]), a_ref[...] * _sigmoid(g_ref[...]),
                   an_ref[...] * _sigmoid(gn_ref[...]))
        _rotate_slab(slab, rot, tm)

        def lane_block(cb, carry):
            cs = pl.ds(pl.multiple_of(cb * LANE, LANE), LANE)
            for r0 in range(0, tm, rc):
                acc = jnp.zeros((rc, LANE), F32)
                for k in range(CONV_K):
                    acc = acc + w_ref[k:k + 1, cs] * _shifted(slab, rot, r0 + HALO - CONV_PAD + k, rc, cs)
                uc_ref[r0:r0 + rc, cs] = acc + cb_ref[:, cs]
            return carry

        lax.fori_loop(0, c // LANE, lane_block, 0)
        xhat, _ = _ln_stats(uc_ref[...])
        cl = xhat * lg_ref[...] + lb_ref[...]
        co_ref[...] = (cl * _sigmoid(cl)).astype(BF)

    a_specs = _halo_specs(tm, s, c, 1)
    g_specs = _halo_specs(tm, s, c, 2)
    row = pl.BlockSpec((1, c), lambda i: (0, 0))
    tok = pl.BlockSpec((tm, c), lambda i: (i, 0))
    return _call(
        body, name="conv_fwd", grid=(s // tm,),
        in_specs=[*a_specs, *g_specs, pl.BlockSpec(conv_w.shape, lambda i: (0, 0)), row, row, row],
        out_specs=[tok, tok], out_shape=[_sds((s, c), BF), _sds((s, c), F32)],
        scratch_shapes=[pltpu.VMEM(shp, F32) for shp in _slab_shapes(tm, c)],
        compiler_params=_cp(("arbitrary",)),
    )(h, h, h, h, h, h, conv_w, conv_b, g_ln, b_ln)


def out_proj_ln1(ob, co, wout, x0, g1, b1):
    s, d = x0.shape
    kh = ob.shape[1]
    tm = _tile(s, 256)

    def body(o_ref, c_ref, w_ref, x_ref, g_ref, b_ref, r_ref, x1_ref, x1b_ref, acc):
        k = pl.program_id(1)

        @pl.when(k == 0)
        def _():
            acc[...] = _dot(o_ref[...], w_ref[...])

        @pl.when(k == 1)
        def _():
            r = ALPHA * x_ref[...] + (acc[...] + _dot(c_ref[...], w_ref[...]))
            r_ref[...] = r
            xhat, _ = _ln_stats(r)
            y = xhat * g_ref[...] + b_ref[...]
            x1_ref[...] = y
            x1b_ref[...] = y.astype(BF)

    half = pl.BlockSpec((tm, kh), lambda i, k: (i, 0))
    tok = pl.BlockSpec((tm, d), lambda i, k: (i, 0))
    row = pl.BlockSpec((1, d), lambda i, k: (0, 0))
    return _call(
        body, name="out_proj_ln1", grid=(s // tm, 2),
        in_specs=[half, half, pl.BlockSpec((kh, d), lambda i, k: (k, 0)), tok, row, row],
        out_specs=[tok, tok, tok], out_shape=[_sds((s, d), F32), _sds((s, d), F32), _sds((s, d), BF)],
        scratch_shapes=[pltpu.VMEM((tm, d), F32)], compiler_params=_cp(("arbitrary", "arbitrary")),
    )(ob, co, wout, x0, g1, b1)


def ff1_fwd(x1b, wff1_g):
    s, d = x1b.shape
    nsh, _, fs = wff1_g.shape
    tm = _tile(s, 1024)
    tn = _tile(fs, 1024)
    per = fs // tn

    def body(a_ref, w_ref, r_ref, a1_ref):
        r = jnp.maximum(_dot(a_ref[...], w_ref[...]), 0.0)
        r_ref[...] = r.astype(BF)
        a1_ref[...] = (r * r).astype(BF)

    out = pl.BlockSpec((tm, tn), lambda i, j: (i, j))
    return _call(
        body, name="ff1_fwd", grid=(s // tm, nsh * per),
        in_specs=[pl.BlockSpec((tm, d), lambda i, j: (i, 0)),
                  pl.BlockSpec((None, d, tn), lambda i, j: (j // per, 0, j % per))],
        out_specs=[out, out], out_shape=[_sds((s, nsh * fs), BF)] * 2,
        compiler_params=_cp(("arbitrary", "arbitrary")),
    )(x1b, wff1_g)


def ff2_ln2_loss(a1b, wff2, x1, target, g2, b2):
    s, f = a1b.shape
    d = x1.shape[1]
    tm = _tile(s, 512)
    tk = _tile(f, 1024)
    nk = f // tk

    def body(a_ref, w_ref, x_ref, t_ref, g_ref, b_ref, dr_ref, drb_ref, loss_ref, dg_ref, db_ref, acc):
        i = pl.program_id(0)
        k = pl.program_id(1)

        @pl.when(k == 0)
        def _():
            acc[...] = _dot(a_ref[...], w_ref[...])

        @pl.when(k > 0)
        def _():
            acc[...] += _dot(a_ref[...], w_ref[...])

        @pl.when(jnp.logical_and(i == 0, k == 0))
        def _():
            loss_ref[...] = jnp.zeros_like(loss_ref)
            dg_ref[...] = jnp.zeros_like(dg_ref)
            db_ref[...] = jnp.zeros_like(db_ref)

        @pl.when(k == nk - 1)
        def _():
            g = g_ref[...]

            def chunk(rows):
                r = ALPHA * x_ref[rows, :] + acc[rows, :]
                xhat, rstd = _ln_stats(r)
                e = xhat * g + b_ref[...] - t_ref[rows, :]
                e2 = _rows8(e * e)
                part = e2[:, 0:LANE]
                for c in range(1, d // LANE):
                    part = part + e2[:, LANE * c:LANE * (c + 1)]
                loss_ref[...] += part * (0.5 / d)
                dy = e * (1.0 / d)
                dg_ref[...] += _rows8(dy * xhat)
                db_ref[...] += _rows8(dy)
                dr = _ln_bwd(dy, xhat, rstd, g)
                dr_ref[rows, :] = dr
                drb_ref[rows, :] = dr.astype(BF)

            _row_chunks(tm, chunk)

    tok = pl.BlockSpec((tm, d), lambda i, k: (i, 0))
    row = pl.BlockSpec((1, d), lambda i, k: (0, 0))
    accs = pl.BlockSpec((SUB, d), lambda i, k: (0, 0))
    return _call(
        body, name="ff2_ln2_loss", grid=(s // tm, nk),
        in_specs=[pl.BlockSpec((tm, tk), lambda i, k: (i, k)), pl.BlockSpec((tk, d), lambda i, k: (k, 0)),
                  tok, tok, row, row],
        out_specs=[tok, tok, pl.BlockSpec((SUB, LANE), lambda i, k: (0, 0)), accs, accs],
        out_shape=[_sds((s, d), F32), _sds((s, d), BF), _sds((SUB, LANE), F32), _sds((SUB, d), F32), _sds((SUB, d), F32)],
        scratch_shapes=[pltpu.VMEM((tm, d), F32)], compiler_params=_cp(("arbitrary", "arbitrary"), 56),
    )(a1b, wff2, x1, target, g2, b2)


def ff2_bwd_act(dr2b, wff2t, rb):
    s, d = dr2b.shape
    f = wff2t.shape[1]
    tm = _tile(s, 1024)
    tn = _tile(f, 1024)

    def body(a_ref, w_ref, r_ref, o_ref):
        o_ref[...] = (_dot(a_ref[...], w_ref[...]) * (2.0 * r_ref[...].astype(F32))).astype(BF)

    return _call(
        body, name="ff2_bwd_act", grid=(s // tm, f // tn),
        in_specs=[pl.BlockSpec((tm, d), lambda i, j: (i, 0)), pl.BlockSpec((d, tn), lambda i, j: (0, j)),
                  pl.BlockSpec((tm, tn), lambda i, j: (i, j))],
        out_specs=pl.BlockSpec((tm, tn), lambda i, j: (i, j)), out_shape=_sds((s, f), BF),
        compiler_params=_cp(("arbitrary", "arbitrary")),
    )(dr2b, wff2t, rb)


def wgrad(name, a, b, tm, tn, tk=2048, shards=1):
    s, m = a.shape
    n = b.shape[1]
    tm = _tile(m, tm)
    ns = n // shards
    tn = _tile(ns, tn)
    tk = _tile(s, tk)
    per = ns // tn

    def body(a_ref, b_ref, o_ref):
        k = pl.program_id(2)

        @pl.when(k == 0)
        def _():
            o_ref[...] = _dot_tn(a_ref[...], b_ref[...])

        @pl.when(k > 0)
        def _():
            o_ref[...] += _dot_tn(a_ref[...], b_ref[...])

    return _call(
        body, name=name, grid=(m // tm, n // tn, s // tk),
        in_specs=[pl.BlockSpec((tk, tm), lambda i, j, k: (k, i)), pl.BlockSpec((tk, tn), lambda i, j, k: (k, j))],
        out_specs=pl.BlockSpec((None, tm, tn), lambda i, j, k: (j // per, i, j % per)),
        out_shape=_sds((shards, m, ns), F32), compiler_params=_cp(("arbitrary", "arbitrary", "arbitrary")),
    )(a, b)


def ff1_bwd_ln1(df1b, wff1t, dr2, r1, g1):
    s, f = df1b.shape
    d = dr2.shape[1]
    tm = _tile(s, 512)
    tk = _tile(f, 1024)
    nk = f // tk

    def body(a_ref, w_ref, d2_ref, r_ref, g_ref, dr_ref, drb_ref, dg_ref, db_ref, acc):
        i = pl.program_id(0)
        k = pl.program_id(1)

        @pl.when(k == 0)
        def _():
            acc[...] = _dot(a_ref[...], w_ref[...])

        @pl.when(k > 0)
        def _():
            acc[...] += _dot(a_ref[...], w_ref[...])

        @pl.when(jnp.logical_and(i == 0, k == 0))
        def _():
            dg_ref[...] = jnp.zeros_like(dg_ref)
            db_ref[...] = jnp.zeros_like(db_ref)

        @pl.when(k == nk - 1)
        def _():
            g = g_ref[...]

            def chunk(rows):
                dy = ALPHA * d2_ref[rows, :] + acc[rows, :]
                xhat, rstd = _ln_stats(r_ref[rows, :])
                dg_ref[...] += _rows8(dy * xhat)
                db_ref[...] += _rows8(dy)
                dr = _ln_bwd(dy, xhat, rstd, g)
                dr_ref[rows, :] = dr
                drb_ref[rows, :] = dr.astype(BF)

            _row_chunks(tm, chunk)

    tok = pl.BlockSpec((tm, d), lambda i, k: (i, 0))
    accs = pl.BlockSpec((SUB, d), lambda i, k: (0, 0))
    return _call(
        body, name="ff1_bwd_ln1", grid=(s // tm, nk),
        in_specs=[pl.BlockSpec((tm, tk), lambda i, k: (i, k)), pl.BlockSpec((tk, d), lambda i, k: (k, 0)),
                  tok, tok, pl.BlockSpec((1, d), lambda i, k: (0, 0))],
        out_specs=[tok, tok, accs, accs],
        out_shape=[_sds((s, d), F32), _sds((s, d), BF), _sds((SUB, d), F32), _sds((SUB, d), F32)],
        scratch_shapes=[pltpu.VMEM((tm, d), F32)], compiler_params=_cp(("arbitrary", "arbitrary"), 56),
    )(df1b, wff1t, dr2, r1, g1)


def out_proj_bwd(dr1b, woutt, o):
    s, d = dr1b.shape
    tm = _tile(s, 256)

    def body(a_ref, w_ref, o_ref, do_ref, dot_ref, dc_ref, dl_ref):
        dcat = _dot(a_ref[...], w_ref[...])
        do = dcat[:, 0:MLA_W]
        do_ref[...] = do.astype(BF)
        dc_ref[...] = dcat[:, MLA_W:]
        prod = do * o_ref[...]
        for hd in range(HEADS):
            hs = slice(LANE * hd, LANE * (hd + 1))
            dl_ref[hd] = jnp.sum(prod[:, hs], axis=-1, keepdims=True)
            dot_ref[hd] = do[:, hs].T.astype(BF)

    half = pl.BlockSpec((tm, MLA_W), lambda i: (i, 0))
    return _call(
        body, name="out_proj_bwd", grid=(s // tm,),
        in_specs=[pl.BlockSpec((tm, d), lambda i: (i, 0)), pl.BlockSpec((d, d), lambda i: (0, 0)), half],
        out_specs=[half, pl.BlockSpec((HEADS, LANE, tm), lambda i: (0, 0, i)),
                   pl.BlockSpec((tm, d - MLA_W), lambda i: (i, 0)), pl.BlockSpec((HEADS, tm, 1), lambda i: (0, i, 0))],
        out_shape=[_sds((s, MLA_W), BF), _sds((HEADS, LANE, s), BF), _sds((s, d - MLA_W), F32), _sds((HEADS, s, 1), F32)],
        compiler_params=_cp(("arbitrary",)),
    )(dr1b, woutt, o)


def conv_bwd_ln(uc, dco, g_ln, b_ln):
    s, c = uc.shape
    tm = _tile(s, 512)

    def body(u_ref, d_ref, g_ref, b_ref, du_ref, dg_ref, db_ref, dcb_ref):
        @pl.when(pl.program_id(0) == 0)
        def _():
            dg_ref[...] = jnp.zeros_like(dg_ref)
            db_ref[...] = jnp.zeros_like(db_ref)
            dcb_ref[...] = jnp.zeros_like(dcb_ref)

        xhat, rstd = _ln_stats(u_ref[...])
        g = g_ref[...]
        cl = xhat * g + b_ref[...]
        sg = _sigmoid(cl)
        dcl = d_ref[...] * (sg * (1.0 + cl * (1.0 - sg)))
        dg_ref[...] += _rows8(dcl * xhat)
        db_ref[...] += _rows8(dcl)
        du = _ln_bwd(dcl, xhat, rstd, g)
        du_ref[...] = du
        dcb_ref[...] += _rows8(du)

    tok = pl.BlockSpec((tm, c), lambda i: (i, 0))
    row = pl.BlockSpec((1, c), lambda i: (0, 0))
    accs = pl.BlockSpec((SUB, c), lambda i: (0, 0))
    return _call(
        body, name="conv_bwd_ln", grid=(s // tm,), in_specs=[tok, tok, row, row], out_specs=[tok, accs, accs, accs],
        out_shape=[_sds((s, c), F32)] + [_sds((SUB, c), F32)] * 3, compiler_params=_cp(("arbitrary",)),
    )(uc, dco, g_ln, b_ln)


def conv_bwd_taps(h, duc, conv_w):
    s, c = duc.shape
    tm = _tile(s, 256)
    rc = _tile(tm, 64)

    def body(a_ref, ap_ref, an_ref, g_ref, gp_ref, gn_ref, d_ref, dp_ref, dn_ref, w_ref, o_ref, dw_ref,
             uslab, dslab, du_s, urot, drot, dw8):
        @pl.when(pl.program_id(0) == 0)
        def _():
            dw8[...] = jnp.zeros_like(dw8)

        sg = _sigmoid(g_ref[...])
        a = a_ref[...]
        _fill_slab(uslab, tm, ap_ref[...] * _sigmoid(gp_ref[...]), a * sg, an_ref[...] * _sigmoid(gn_ref[...]))
        _fill_slab(dslab, tm, dp_ref[...], d_ref[...], dn_ref[...])
        _rotate_slab(uslab, urot, tm)
        _rotate_slab(dslab, drot, tm)

        def lane_block(cb, carry):
            cs = pl.ds(pl.multiple_of(cb * LANE, LANE), LANE)
            for r0 in range(0, tm, rc):
                acc = jnp.zeros((rc, LANE), F32)
                for k in range(CONV_K):
                    acc = acc + w_ref[k:k + 1, cs] * _shifted(dslab, drot, r0 + HALO + CONV_PAD - k, rc, cs)
                du_s[r0:r0 + rc, cs] = acc
            return carry

        def lane_block_taps(cb, carry):
            cs = pl.ds(pl.multiple_of(cb * LANE, LANE), LANE)
            parts = []
            for k in range(CONV_K):
                prod = None
                for r0 in range(0, tm, rc):
                    t = dslab[pl.ds(r0 + HALO, rc), cs] * _shifted(uslab, urot, r0 + HALO - CONV_PAD + k, rc, cs)
                    prod = t if prod is None else prod + t
                parts.append(_rows8(prod))
            rows = SUB * CONV_K
            dw8[0:rows, cs] = dw8[0:rows, cs] + jnp.concatenate(parts, axis=0)
            return carry

        lax.fori_loop(0, c // LANE, lane_block, 0)
        lax.fori_loop(0, c // LANE, lane_block_taps, 0)

        @pl.when(pl.program_id(0) == pl.num_programs(0) - 1)
        def _():
            dw_ref[...] = jnp.zeros_like(dw_ref)
            for k in range(CONV_K):
                dw_ref[k:k + 1, :] = jnp.sum(dw8[SUB * k:SUB * (k + 1), :], axis=0, keepdims=True)

        du = du_s[...]
        o_ref[:, 0:c] = (du * sg).astype(BF)
        o_ref[:, c:2 * c] = (du * a * sg * (1.0 - sg)).astype(BF)

    a_specs = _halo_specs(tm, s, c, 1)
    g_specs = _halo_specs(tm, s, c, 2)
    d_specs = _halo_specs(tm, s, c, 0)
    wsp = pl.BlockSpec(conv_w.shape, lambda i: (0, 0))
    return _call(
        body, name="conv_bwd_taps", grid=(s // tm,), in_specs=[*a_specs, *g_specs, *d_specs, wsp],
        out_specs=[pl.BlockSpec((tm, 2 * c), lambda i: (i, 0)), wsp],
        out_shape=[_sds((s, 2 * c), BF), _sds(conv_w.shape, F32)],
        scratch_shapes=[pltpu.VMEM(_slab_shapes(tm, c)[0], F32), pltpu.VMEM(_slab_shapes(tm, c)[0], F32), pltpu.VMEM((tm, c), F32),
                        pltpu.VMEM(_slab_shapes(tm, c)[1], F32), pltpu.VMEM(_slab_shapes(tm, c)[1], F32),
                        pltpu.VMEM((SUB * conv_w.shape[0], c), F32)],
        compiler_params=_cp(("arbitrary",)),
    )(h, h, h, h, h, h, duc, duc, duc, conv_w)


def attn_bwd(qc, kc, kct, v, dob, dot, lse_r, delta_r):
    _, s, _ = qc.shape
    tk = _tile(s, 512)
    tq = _tile(s, 512)
    scale = D_QK ** -0.5
    c2 = scale * LOG2E

    def body(k_ref, kt_ref, v_ref, q_ref, do_ref, dot_ref, l_ref, dl_ref, dqt_ref, dk_ref, dvt_ref):
        @pl.when(pl.program_id(1) == 0)
        def _():
            dqt_ref[...] = jnp.zeros_like(dqt_ref)

        k = k_ref[...]
        kt = kt_ref[...]
        vv = v_ref[...]

        def step(i, carry):
            dk, dvt = carry
            off = pl.multiple_of(i * tq, tq)
            q = q_ref[pl.ds(off, tq), :]
            do = do_ref[pl.ds(off, tq), :]
            pt = jnp.exp2(_dot_nt(k, q) * c2 - l_ref[:, pl.ds(off, tq)])
            dvt = dvt + _dot_nt(dot_ref[:, pl.ds(off, tq)], pt.astype(BF))
            dpt = _dot_nt(vv, do)
            dsb = (pt * (dpt - dl_ref[:, pl.ds(off, tq)]) * scale).astype(BF)
            dk = dk + _dot(dsb, q)
            dqt_ref[:, pl.ds(off, tq)] += _dot(kt, dsb)
            return dk, dvt

        dk, dvt = _unrolled_loop(s // tq, 4, step, (jnp.zeros((tk, 2 * LANE), F32), jnp.zeros((LANE, tk), F32)))
        dk_ref[...] = dk
        dvt_ref[...] = dvt

    rowv = pl.BlockSpec((None, 1, s), lambda h, j: (h, 0, 0))
    return _call(
        body, name="attn_bwd", grid=(HEADS, s // tk),
        in_specs=[pl.BlockSpec((None, tk, 2 * LANE), lambda h, j: (h, j, 0)),
                  pl.BlockSpec((None, 2 * LANE, tk), lambda h, j: (h, 0, j)),
                  pl.BlockSpec((None, tk, LANE), lambda h, j: (h, j, 0)),
                  pl.BlockSpec((None, s, 2 * LANE), lambda h, j: (h, 0, 0)),
                  pl.BlockSpec((s, LANE), lambda h, j: (0, h)),
                  pl.BlockSpec((None, LANE, s), lambda h, j: (h, 0, 0)), rowv, rowv],
        out_specs=[pl.BlockSpec((None, 2 * LANE, s), lambda h, j: (h, 0, 0)),
                   pl.BlockSpec((None, tk, 2 * LANE), lambda h, j: (h, j, 0)),
                   pl.BlockSpec((None, LANE, tk), lambda h, j: (h, 0, j))],
        out_shape=[_sds((HEADS, 2 * LANE, s), F32), _sds((HEADS, s, 2 * LANE), F32), _sds((HEADS, LANE, s), F32)],
        compiler_params=_cp(("arbitrary", "arbitrary"), 56),
    )(kc, kct, v, qc, dob, dot, lse_r, delta_r)


def q_bwd(dqt, h, g_cq, wuqt, cos, sin):
    s = h.shape[0]
    tm = _tile(s, 256)

    def body(d_ref, h_ref, g_ref, w_ref, c_ref, s_ref, dq_ref, dc_ref, dg_ref):
        @pl.when(pl.program_id(0) == 0)
        def _():
            dg_ref[...] = jnp.zeros_like(dg_ref)

        c = c_ref[...]
        sn = s_ref[...]
        for hd in range(HEADS):
            t = d_ref[hd].T
            dq_ref[:, LANE * hd:LANE * (hd + 1)] = t[:, 0:LANE].astype(BF)
            dq_ref[:, MLA_W + LANE * hd:MLA_W + LANE * (hd + 1)] = _unrope128(t[:, LANE:2 * LANE], c, sn).astype(BF)
        dy = _dot(dq_ref[...], w_ref[...])
        g = g_ref[...]
        _, xh, rr = _rms_fwd(h_ref[...], g)
        dg_ref[...] += _rows8(dy * xh)
        dc_ref[...] = _rms_bwd(dy, xh, rr, g).astype(BF)

    tab = pl.BlockSpec((tm, LANE), lambda i: (i, 0))
    return _call(
        body, name="q_bwd", grid=(s // tm,),
        in_specs=[pl.BlockSpec((HEADS, 2 * LANE, tm), lambda i: (0, 0, i)), pl.BlockSpec((tm, R_Q), lambda i: (i, 0)),
                  pl.BlockSpec((1, R_Q), lambda i: (0, 0)), pl.BlockSpec((2 * MLA_W, R_Q), lambda i: (0, 0)), tab, tab],
        out_specs=[pl.BlockSpec((tm, 2 * MLA_W), lambda i: (i, 0)), pl.BlockSpec((tm, R_Q), lambda i: (i, 0)),
                   pl.BlockSpec((SUB, R_Q), lambda i: (0, 0))],
        out_shape=[_sds((s, 2 * MLA_W), BF), _sds((s, R_Q), BF), _sds((SUB, R_Q), F32)],
        compiler_params=_cp(("arbitrary",)),
    )(dqt, h, g_cq, wuqt, cos, sin)


def kv_bwd(dk, dv, h, g_ckv, wukt, wuvt, cos, sin):
    s = h.shape[0]
    tm = _tile(s, 256)

    def body(dk_ref, dv_ref, h_ref, g_ref, wk_ref, wv_ref, c_ref, s_ref, dkn_ref, dvb_ref, dc_ref, dkr_ref, dg_ref):
        @pl.when(pl.program_id(0) == 0)
        def _():
            dg_ref[...] = jnp.zeros_like(dg_ref)

        dkr = dk_ref[0, :, LANE:2 * LANE]
        for hd in range(HEADS):
            dkn_ref[:, LANE * hd:LANE * (hd + 1)] = dk_ref[hd, :, 0:LANE].astype(BF)
            dvb_ref[:, LANE * hd:LANE * (hd + 1)] = dv_ref[hd].T.astype(BF)
            if hd > 0:
                dkr = dkr + dk_ref[hd, :, LANE:2 * LANE]
        dkr_ref[...] = _unrope128(dkr, c_ref[...], s_ref[...]).astype(BF)
        dy = _dot(dkn_ref[...], wk_ref[...]) + _dot(dvb_ref[...], wv_ref[...])
        g = g_ref[...]
        _, xh, rr = _rms_fwd(h_ref[...], g)
        dg_ref[...] += _rows8(dy * xh)
        dc_ref[...] = _rms_bwd(dy, xh, rr, g).astype(BF)

    tab = pl.BlockSpec((tm, LANE), lambda i: (i, 0))
    wsp = pl.BlockSpec((MLA_W, R_KV), lambda i: (0, 0))
    wide = pl.BlockSpec((tm, MLA_W), lambda i: (i, 0))
    return _call(
        body, name="kv_bwd", grid=(s // tm,),
        in_specs=[pl.BlockSpec((HEADS, tm, 2 * LANE), lambda i: (0, i, 0)), pl.BlockSpec((HEADS, LANE, tm), lambda i: (0, 0, i)),
                  pl.BlockSpec((tm, R_KV), lambda i: (i, 1)), pl.BlockSpec((1, R_KV), lambda i: (0, 0)), wsp, wsp, tab, tab],
        out_specs=[wide, wide, pl.BlockSpec((tm, R_KV), lambda i: (i, 0)), tab, pl.BlockSpec((SUB, R_KV), lambda i: (0, 0))],
        out_shape=[_sds((s, MLA_W), BF), _sds((s, MLA_W), BF), _sds((s, R_KV), BF), _sds((s, LANE), BF), _sds((SUB, R_KV), F32)],
        compiler_params=_cp(("arbitrary",)),
    )(dk, dv, h, g_ckv, wukt, wuvt, cos, sin)


def in_proj_bwd_ln(dh, wint, dr1, x, g_in):
    s, hc = dh.shape
    d = x.shape[1]
    tm = _tile(s, 512)
    tk = _tile(hc, 640)
    nk = hc // tk

    def body(a_ref, w_ref, d1_ref, x_ref, g_ref, gx_ref, dg_ref, db_ref, acc):
        i = pl.program_id(0)
        k = pl.program_id(1)

        @pl.when(k == 0)
        def _():
            acc[...] = _dot(a_ref[...], w_ref[...])

        @pl.when(k > 0)
        def _():
            acc[...] += _dot(a_ref[...], w_ref[...])

        @pl.when(jnp.logical_and(i == 0, k == 0))
        def _():
            dg_ref[...] = jnp.zeros_like(dg_ref)
            db_ref[...] = jnp.zeros_like(db_ref)

        @pl.when(k == nk - 1)
        def _():
            g = g_ref[...]

            def chunk(rows):
                dy = ALPHA * d1_ref[rows, :] + acc[rows, :]
                xhat, rstd = _ln_stats(x_ref[rows, :])
                dg_ref[...] += _rows8(dy * xhat)
                db_ref[...] += _rows8(dy)
                gx_ref[rows, :] = _ln_bwd(dy, xhat, rstd, g)

            _row_chunks(tm, chunk)

    tok = pl.BlockSpec((tm, d), lambda i, k: (i, 0))
    accs = pl.BlockSpec((SUB, d), lambda i, k: (0, 0))
    return _call(
        body, name="in_proj_bwd_ln", grid=(s // tm, nk),
        in_specs=[pl.BlockSpec((tm, tk), lambda i, k: (i, k)), pl.BlockSpec((tk, d), lambda i, k: (k, 0)),
                  tok, tok, pl.BlockSpec((1, d), lambda i, k: (0, 0))],
        out_specs=[tok, accs, accs], out_shape=[_sds((s, d), F32), _sds((SUB, d), F32), _sds((SUB, d), F32)],
        scratch_shapes=[pltpu.VMEM((tm, d), F32)], compiler_params=_cp(("arbitrary", "arbitrary")),
    )(dh, wint, dr1, x, g_in)


def _adamw_math(w, g, m, v):
    m = ADAM_B1 * m + (1.0 - ADAM_B1) * g
    v = ADAM_B2 * v + (1.0 - ADAM_B2) * (g * g)
    m_hat = m / (1.0 - ADAM_B1 ** ADAM_STEP)
    v_hat = v / (1.0 - ADAM_B2 ** ADAM_STEP)
    delta = -ADAM_LR * (m_hat / (jnp.sqrt(v_hat) + ADAM_EPS) + ADAM_WD * w)
    return delta, m, v


def adamw(name, w, g, m, v):
    r, c = w.shape
    tr = _row_tile(r, c)

    def body(w_ref, g_ref, m_ref, v_ref, d_ref, mo_ref, vo_ref):
        d_ref[...], mo_ref[...], vo_ref[...] = _adamw_math(w_ref[...], g_ref[...], m_ref[...], v_ref[...])

    blk = pl.BlockSpec((tr, c), lambda i: (i, 0))
    return _call(
        body, name=name, grid=(r // tr,), in_specs=[blk] * 4, out_specs=[blk] * 3,
        out_shape=[_sds((r, c), F32)] * 3, compiler_params=_cp(("arbitrary",)),
    )(w, g, m, v)


def _coords():
    return lax.axis_index("x"), lax.axis_index("y"), lax.axis_index("c")


def _other_chips(x, y):
    return [(1 - x, y, 2 * (1 - x) + y), (x, 1 - y, 2 * x + 1 - y), (1 - x, 1 - y, 2 * (1 - x) + 1 - y)]


ANY = pl.BlockSpec(memory_space=pl.ANY)
HBM = pl.BlockSpec(memory_space=pltpu.HBM)
SEM = pl.BlockSpec(memory_space=pltpu.SEMAPHORE)
EFFECT = pltpu.SideEffectType.DATAFLOW_SIDE_EFFECTING


def _in_hbm(a):
    return pltpu.with_memory_space_constraint(a, pltpu.HBM)


def _split_refs(own, src, land, me, j, pk):
    if own:
        return src, land.at[me], land.at[pk]
    return src.at[pk], land.at[j], land.at[j]


def chips_send_start(name, own, srcs, land_shapes, order_after):
    n = len(srcs)

    def body(*refs):
        ins, lands = refs[:n], refs[n:2 * n]
        ss, rs = refs[2 * n + 1], refs[2 * n + 2]
        token = refs[-1]
        x, y, c = _coords()
        me = 2 * x + y
        for a in range(n):
            for j, (px, py, pk) in enumerate(_other_chips(x, y)):
                src, dst, _ = _split_refs(own, ins[a], lands[a], me, j, pk)
                pltpu.make_async_remote_copy(src_ref=src, dst_ref=dst, send_sem=ss.at[3 * a + j], recv_sem=rs.at[3 * a + j],
                                             device_id=(px, py, c), device_id_type=MESH).start()
        token[...] = jnp.zeros_like(token)

    lands = [lax.empty(shp, s.dtype) for shp, s in zip(land_shapes, srcs)]
    outs = _call(
        body, name=name,
        out_shape=(pltpu.SemaphoreType.DMA((3 * n,)), pltpu.SemaphoreType.DMA((3 * n,)),
                   *[pltpu.HBM(s.shape, s.dtype) for s in srcs], *[pltpu.HBM(l.shape, l.dtype) for l in lands],
                   _sds((SUB, LANE), F32)),
        in_specs=[HBM] * (2 * n) + [ANY], out_specs=(SEM, SEM, *[HBM] * (2 * n), pl.BlockSpec(memory_space=pltpu.VMEM)),
        input_output_aliases={a: 2 + a for a in range(2 * n)},
        compiler_params=pltpu.CompilerParams(has_side_effects=EFFECT),
    )(*[_in_hbm(s) for s in srcs], *[_in_hbm(l) for l in lands], order_after)
    return outs[0], outs[1], list(outs[2:2 + n]), list(outs[2 + n:2 + 2 * n]), outs[-1]


def chips_send_wait(name, own, ss, rs, srcs, lands, order_after):
    n = len(srcs)

    def body(*refs):
        ins, lnd = refs[:n], refs[n:2 * n]
        s_ref, r_ref = refs[2 * n], refs[2 * n + 1]
        x, y, c = _coords()
        me = 2 * x + y
        for a in range(n):
            for j, (px, py, pk) in enumerate(_other_chips(x, y)):
                src, _, got = _split_refs(own, ins[a], lnd[a], me, j, pk)
                cp = pltpu.make_async_remote_copy(src_ref=src, dst_ref=got, send_sem=s_ref.at[3 * a + j], recv_sem=r_ref.at[3 * a + j],
                                                  device_id=(px, py, c), device_id_type=MESH)
                cp.wait_send()
                cp.wait_recv()

    outs = _call(
        body, name=name, out_shape=tuple(pltpu.HBM(t.shape, t.dtype) for t in (*srcs, *lands)),
        in_specs=[HBM] * (2 * n) + [SEM, SEM, ANY], out_specs=tuple([HBM] * (2 * n)),
        input_output_aliases={a: a for a in range(2 * n)},
        compiler_params=pltpu.CompilerParams(has_side_effects=EFFECT),
    )(*srcs, *lands, ss, rs, order_after)
    return list(outs[:n]), list(outs[n:])


def all_gather_shards(shards):
    n = len(shards)

    def body(*refs):
        ins, outs = refs[:n], refs[n:2 * n]
        ici_s, ici_r, d2d_s, d2d_r = refs[2 * n:]
        x, y, c = _coords()
        me = 2 * x + y
        peers = _other_chips(x, y)
        sends, fwds = [], []
        for a in range(n):
            rh = ins[a].shape[0] // 2
            mine = pl.ds(c * rh, rh)
            for j, (px, py, pk) in enumerate(peers):
                cp = pltpu.make_async_remote_copy(
                    src_ref=ins[a].at[mine], dst_ref=outs[a].at[me, mine], send_sem=ici_s.at[a, j], recv_sem=ici_r.at[a, j],
                    device_id=(px, py, c), device_id_type=MESH)
                cp.start()
                sends.append(cp)
        for a in range(n):
            rh = ins[a].shape[0] // 2
            mine = pl.ds(c * rh, rh)
            for j, (px, py, pk) in enumerate(peers):
                got = outs[a].at[pk, mine]
                pltpu.make_async_remote_copy(
                    src_ref=got, dst_ref=got, send_sem=ici_s.at[a, j], recv_sem=ici_r.at[a, j],
                    device_id=(px, py, c), device_id_type=MESH).wait_recv()
                fw = pltpu.make_async_remote_copy(
                    src_ref=got, dst_ref=got, send_sem=d2d_s.at[a, j], recv_sem=d2d_r.at[a, j],
                    device_id=(x, y, 1 - c), device_id_type=MESH)
                fw.start()
                fwds.append(fw)
        for a in range(n):
            rh = ins[a].shape[0] // 2
            theirs = pl.ds((1 - c) * rh, rh)
            for j, (px, py, pk) in enumerate(peers):
                got = outs[a].at[pk, theirs]
                pltpu.make_async_remote_copy(
                    src_ref=got, dst_ref=got, send_sem=d2d_s.at[a, j], recv_sem=d2d_r.at[a, j],
                    device_id=(x, y, 1 - c), device_id_type=MESH).wait_recv()
        for cp in sends + fwds:
            cp.wait_send()

    got = _call(
        body, name="all_gather_shards", in_specs=[ANY] * n, out_specs=[ANY] * n,
        out_shape=[_sds((N_CHIP,) + w.shape, w.dtype) for w in shards],
        scratch_shapes=[pltpu.SemaphoreType.DMA((n, 3))] * 4,
    )(*shards)
    me = 2 * lax.axis_index("x") + lax.axis_index("y")
    return [lax.dynamic_update_slice(g, w[None], (me, 0, 0)) for g, w in zip(got, shards)]


def pair_exchange(grads, tag):
    n = len(grads)

    def body(*refs):
        ins, outs = refs[:n], refs[n:2 * n]
        ss, rs = refs[2 * n:]
        x, y, c = _coords()
        cps = []
        for a in range(n):
            rh = ins[a].shape[1] // 2
            cp = pltpu.make_async_remote_copy(
                src_ref=ins[a].at[:, pl.ds((1 - c) * rh, rh)], dst_ref=outs[a], send_sem=ss.at[a], recv_sem=rs.at[a],
                device_id=(x, y, 1 - c), device_id_type=MESH)
            cp.start()
            cps.append(cp)
        for cp in cps:
            cp.wait()

    return _call(
        body, name="pair_exchange_" + tag, in_specs=[ANY] * n, out_specs=[ANY] * n,
        out_shape=[_sds((N_CHIP, g.shape[1] // 2, g.shape[2]), F32) for g in grads],
        scratch_shapes=[pltpu.SemaphoreType.DMA((n,))] * 2,
    )(*grads)


def _row_tile(rows, cols, itemsize=4, budget=2 * VMEM_MB):
    t = rows
    while t * cols * itemsize > budget and t % (2 * SUB) == 0:
        t //= 2
    return t


def pair_add(g, r, cidx):
    _, rows, cols = g.shape
    rh = rows // 2
    tr = _row_tile(rh, cols)
    per = rh // tr

    def body(c_ref, g_ref, r_ref, o_ref):
        o_ref[...] = g_ref[...] + r_ref[...]

    return _call(
        body, name="pair_add",
        grid_spec=pltpu.PrefetchScalarGridSpec(
            num_scalar_prefetch=1, grid=(N_CHIP, per),
            in_specs=[pl.BlockSpec((None, tr, cols), lambda k, i, c: (k, c[0] * per + i, 0)),
                      pl.BlockSpec((None, tr, cols), lambda k, i, c: (k, i, 0))],
            out_specs=pl.BlockSpec((None, tr, cols), lambda k, i, c: (k, i, 0))),
        out_shape=_sds((N_CHIP, rh, cols), F32), compiler_params=_cp(("arbitrary", "arbitrary")),
    )(cidx, g, r)


def chip_exchange(parts):
    n = len(parts)

    def body(*refs):
        ins, outs = refs[:n], refs[n:2 * n]
        ss, rs = refs[2 * n:]
        x, y, c = _coords()
        cps = []
        for a in range(n):
            for j, (px, py, pk) in enumerate(_other_chips(x, y)):
                cp = pltpu.make_async_remote_copy(
                    src_ref=ins[a].at[pk], dst_ref=outs[a].at[j], send_sem=ss.at[a, j], recv_sem=rs.at[a, j],
                    device_id=(px, py, c), device_id_type=MESH)
                cp.start()
                cps.append(cp)
        for cp in cps:
            cp.wait()

    return _call(
        body, name="chip_exchange", in_specs=[ANY] * n, out_specs=[ANY] * n,
        out_shape=[_sds((N_CHIP - 1,) + p.shape[1:], F32) for p in parts],
        scratch_shapes=[pltpu.SemaphoreType.DMA((n, 3))] * 2,
    )(*parts)


def chip_add(p, r, kc):
    _, rh, cols = p.shape
    tr = _row_tile(rh, cols)
    per = rh // tr

    def body(k_ref, p_ref, r_ref, o_ref):
        o_ref[...] = ((p_ref[...] + r_ref[0]) + r_ref[1]) + r_ref[2]

    return _call(
        body, name="chip_add",
        grid_spec=pltpu.PrefetchScalarGridSpec(
            num_scalar_prefetch=1, grid=(per,),
            in_specs=[pl.BlockSpec((None, tr, cols), lambda i, k: (k[0], i, 0)),
                      pl.BlockSpec((N_CHIP - 1, tr, cols), lambda i, k: (0, i, 0))],
            out_specs=pl.BlockSpec((tr, cols), lambda i, k: (k[1] * per + i, 0))),
        out_shape=_sds((2 * rh, cols), F32), compiler_params=_cp(("arbitrary",)),
    )(kc, p, r)


def pair_share(fulls, tag):
    n = len(fulls)

    def body(*refs):
        outs = refs[n:2 * n]
        ss, rs = refs[2 * n:]
        x, y, c = _coords()
        cps = []
        for a in range(n):
            rh = outs[a].shape[0] // 2
            mine = outs[a].at[pl.ds(c * rh, rh)]
            cp = pltpu.make_async_remote_copy(
                src_ref=mine, dst_ref=mine, send_sem=ss.at[a], recv_sem=rs.at[a],
                device_id=(x, y, 1 - c), device_id_type=MESH)
            cp.start()
            cps.append(cp)
        for a, cp in enumerate(cps):
            rh = outs[a].shape[0] // 2
            theirs = outs[a].at[pl.ds((1 - c) * rh, rh)]
            cp.wait_send()
            pltpu.make_async_remote_copy(
                src_ref=theirs, dst_ref=theirs, send_sem=ss.at[a], recv_sem=rs.at[a],
                device_id=(x, y, 1 - c), device_id_type=MESH).wait_recv()

    return _call(
        body, name="pair_share_" + tag, in_specs=[ANY] * n, out_specs=[ANY] * n,
        out_shape=[_sds(f.shape, F32) for f in fulls], input_output_aliases={a: a for a in range(n)},
        scratch_shapes=[pltpu.SemaphoreType.DMA((n,))] * 2,
    )(*fulls)


def small_allreduce_adamw(part, w, m, v):
    n = part.shape[1]

    def body(p_ref, w_ref, m_ref, v_ref, g_ref, d_ref, mo_ref, vo_ref, mine, gath, ss, rs):
        x, y, c = _coords()
        me = 4 * x + 2 * y + c
        mine[...] = jnp.sum(p_ref[...], axis=0, keepdims=True)
        gath[me] = mine[...]
        cps = []
        for k in range(1, 8):
            px, py, pc = x ^ (k >> 2), y ^ ((k >> 1) & 1), c ^ (k & 1)
            cp = pltpu.make_async_remote_copy(
                src_ref=mine, dst_ref=gath.at[me], send_sem=ss.at[k - 1], recv_sem=rs.at[k - 1],
                device_id=(px, py, pc), device_id_type=MESH)
            cp.start()
            cps.append(cp)
        for k in range(1, 8):
            src = 4 * (x ^ (k >> 2)) + 2 * (y ^ ((k >> 1) & 1)) + (c ^ (k & 1))
            pltpu.make_async_remote_copy(
                src_ref=mine, dst_ref=gath.at[src], send_sem=ss.at[k - 1], recv_sem=rs.at[k - 1],
                device_id=(x, y, c), device_id_type=MESH).wait_recv()
        for cp in cps:
            cp.wait_send()
        g = gath[0]
        for dv in range(1, 8):
            g = g + gath[dv]
        g_ref[...] = g
        d_ref[...], mo_ref[...], vo_ref[...] = _adamw_math(w_ref[...], g, m_ref[...], v_ref[...])

    vm = pl.BlockSpec(memory_space=pltpu.VMEM)
    return _call(
        body, name="small_allreduce_adamw", in_specs=[vm] * 4, out_specs=[vm] * 4, out_shape=[_sds((1, n), F32)] * 4,
        scratch_shapes=[pltpu.VMEM((1, n), F32), pltpu.VMEM((8, 1, n), F32),
                        pltpu.SemaphoreType.DMA((7,)), pltpu.SemaphoreType.DMA((7,))],
    )(part, w, m, v)


def _unshard_cols(g):
    k, r, cs = g.shape
    return g.transpose(1, 0, 2).reshape(r, k * cs)


def _shard_cols(w):
    r, c = w.shape
    return w.reshape(r, N_CHIP, c // N_CHIP).transpose(1, 0, 2)


def local_step(x, positions, ln_in_g, ln_in_b, win_g, g_cq, wuq_g, g_ckv, wuk_g, wuv_g, convw_g, conv_b, g_conv_ln,
               b_conv_ln, late_weights, g_ln1, b_ln1, g_ln2, b_ln2, target, start_token, early_grads):
    s, d = x.shape
    c = d - MLA_W
    row = lambda a: a.reshape(1, -1)
    ln_in_g = row(ln_in_g) + start_token[0:1, 0:1]

    win = _unshard_cols(win_g)
    o_kr = R_Q + R_KV
    o_cv = o_kr + D_ROPE
    win_r = jnp.concatenate([win[:, :o_kr], win[:, o_cv:], win[:, o_kr:o_cv], jnp.zeros((d, LANE - D_ROPE), BF)], axis=1)
    hc = win_r.shape[1]
    kr_blk = (o_kr + 2 * c) // LANE
    wuq = _unshard_cols(wuq_g).reshape(R_Q, HEADS, D_QK)
    wuq_r = jnp.concatenate([wuq[:, :, :D_NOPE].reshape(R_Q, MLA_W),
                             jnp.pad(wuq[:, :, D_NOPE:], ((0, 0), (0, 0), (0, LANE - D_ROPE))).reshape(R_Q, MLA_W)], axis=1)
    wuk = _unshard_cols(wuk_g)
    wuv = _unshard_cols(wuv_g)
    conv_w = jnp.pad(_unshard_cols(convw_g), ((0, 1), (0, 0)))

    half = D_ROPE // 2
    inv_freq = ROPE_BASE ** (-jnp.arange(half, dtype=F32) * (2.0 / D_ROPE))
    invf = jnp.concatenate([inv_freq, inv_freq, jnp.zeros((LANE - D_ROPE,), F32)]).reshape(1, LANE)
    cos, sin = rope_tables(positions.astype(F32).reshape(s, 1), invf)
    x0, x0b = ln_in_fwd(x, ln_in_g, row(ln_in_b))
    h = matmul("in_proj", x0b, win_r, 1024, 640)
    qc, cqn = q_proj(h, g_cq, wuq_r, cos, sin)
    kc, v, ckvn = kv_proj(h, g_ckv, wuk, wuv, cos, sin, kr_blk)
    o, ob, lse = attn_fwd(qc, kc, v)
    co, uc = conv_fwd(h, conv_w, conv_b, g_conv_ln, b_conv_ln)
    wout_g, wff1_g, wff2_g = late_weights(ob)
    wout = wout_g.reshape(d, d)
    wff2 = wff2_g.reshape(-1, d)
    wff1t = wff1_g.transpose(0, 2, 1).reshape(-1, d)
    wff2t = wff2.T
    r1, x1, x1b = out_proj_ln1(ob, co, wout, x0, g_ln1, b_ln1)
    rb, a1b = ff1_fwd(x1b, wff1_g)
    dr2, dr2b, loss8, dg2, db2 = ff2_ln2_loss(a1b, wff2, x1, target, g_ln2, b_ln2)

    df1b = ff2_bwd_act(dr2b, wff2t, rb)
    gw_ff2 = wgrad("wgrad_ff2", a1b, dr2b, 1024, 1024).reshape(N_CHIP, -1, d)
    gw_ff1 = wgrad("wgrad_ff1", x1b, df1b, 1024, 1024, shards=N_CHIP)
    sent = early_grads(gw_ff2, gw_ff1)
    dr1, dr1b, dg1, db1 = ff1_bwd_ln1(df1b, wff1t, dr2, r1, g_ln1 + sent[0:1, 0:1])
    gw_out = jnp.concatenate([wgrad("wgrad_out_attn", ob, dr1b, 1024, 1024)[0],
                              wgrad("wgrad_out_conv", co, dr1b, 1024, 1024)[0]], axis=0).reshape(N_CHIP, -1, d)
    dob, dot, dco, delta = out_proj_bwd(dr1b, wout.T, o)
    duc, dgc, dbc, dcb = conv_bwd_ln(uc, dco, g_conv_ln, b_conv_ln)
    dconv, gconvw = conv_bwd_taps(h, duc, conv_w)
    dqt, dk, dv = attn_bwd(qc, kc, kc.transpose(0, 2, 1), v, dob, dot, lse.reshape(HEADS, 1, s), delta.reshape(HEADS, 1, s))
    dqb, dcq, dgq = q_bwd(dqt, h, g_cq, wuq_r.T, cos, sin)
    dknb, dvb, dckv, dkr, dgkv = kv_bwd(dk, dv, h, g_ckv, wuk.T, wuv.T, cos, sin)
    gwuq_r = wgrad("wgrad_uq", cqn, dqb, 512, 1024)[0]
    gw_uk = wgrad("wgrad_uk", ckvn, dknb, 512, 1024, shards=N_CHIP)
    gw_uv = wgrad("wgrad_uv", ckvn, dvb, 512, 1024, shards=N_CHIP)
    dh = jnp.concatenate([dcq, dckv, dconv, dkr], axis=1)
    gx, dgin, dbin = in_proj_bwd_ln(dh, win_r.T, dr1, x, ln_in_g)
    gwin_r = wgrad("wgrad_in", x0b, dh, 1024, 640)[0]

    gwin = jnp.concatenate([gwin_r[:, :o_kr], gwin_r[:, o_kr + 2 * c:o_kr + 2 * c + D_ROPE], gwin_r[:, o_kr:o_kr + 2 * c]], axis=1)
    gwuq = jnp.concatenate([gwuq_r[:, :MLA_W].reshape(R_Q, HEADS, D_NOPE),
                            gwuq_r[:, MLA_W:].reshape(R_Q, HEADS, LANE)[:, :, :D_ROPE]], axis=2).reshape(R_Q, HEADS * D_QK)
    big = dict(w_in=_shard_cols(gwin), w_uq=_shard_cols(gwuq), w_uk=gw_uk, w_uv=gw_uv, conv_w=_shard_cols(gconvw), w_out=gw_out)
    small = jnp.concatenate([dgin, dbin, dgq, dgkv, dcb, dgc, dbc, dg1, db1, dg2, db2, loss8], axis=1)
    return gx, big, small


BIG = ["w_in", "w_uq", "w_uk", "w_uv", "conv_w", "w_out", "w_ff1", "w_ff2"]
EARLY = ["w_in", "w_uq", "w_uk", "w_uv", "conv_w"]
LATE = ["w_out", "w_ff1", "w_ff2"]
SMALL = ["ln_in_g", "ln_in_b", "g_cq", "g_ckv", "conv_b", "g_conv_ln", "b_conv_ln", "g_ln1", "b_ln1", "g_ln2", "b_ln2"]
WEIGHTS = ["ln_in_g", "ln_in_b", "w_in", "g_cq", "w_uq", "g_ckv", "w_uk", "w_uv", "conv_w", "conv_b", "g_conv_ln",
           "b_conv_ln", "w_out", "g_ln1", "b_ln1", "w_ff1", "w_ff2", "g_ln2", "b_ln2"]


def _pad_rows(a, rows):
    return jnp.pad(a, ((0, rows - a.shape[0]), (0, 0)))


def kernel(x, positions, ln_in_g, ln_in_b, w_in, g_cq, w_uq, g_ckv, w_uk, w_uv, conv_w, conv_b, g_conv_ln, b_conv_ln, w_out, g_ln1, b_ln1, w_ff1, w_ff2, g_ln2, b_ln2, loss_target, m_ln_in_g, m_ln_in_b, m_w_in, m_g_cq, m_w_uq, m_g_ckv, m_w_uk, m_w_uv, m_conv_w, m_conv_b, m_g_conv_ln, m_b_conv_ln, m_w_out, m_g_ln1, m_b_ln1, m_w_ff1, m_w_ff2, m_g_ln2, m_b_ln2, v_ln_in_g, v_ln_in_b, v_w_in, v_g_cq, v_w_uq, v_g_ckv, v_w_uk, v_w_uv, v_conv_w, v_conv_b, v_g_conv_ln, v_b_conv_ln, v_w_out, v_g_ln1, v_b_ln1, v_w_ff1, v_w_ff2, v_g_ln2, v_b_ln2):
    w = dict(ln_in_g=ln_in_g, ln_in_b=ln_in_b, w_in=w_in, g_cq=g_cq, w_uq=w_uq, g_ckv=g_ckv, w_uk=w_uk, w_uv=w_uv,
             conv_w=conv_w, conv_b=conv_b, g_conv_ln=g_conv_ln, b_conv_ln=b_conv_ln, w_out=w_out, g_ln1=g_ln1,
             b_ln1=b_ln1, w_ff1=w_ff1, w_ff2=w_ff2, g_ln2=g_ln2, b_ln2=b_ln2)
    m = dict(ln_in_g=m_ln_in_g, ln_in_b=m_ln_in_b, w_in=m_w_in, g_cq=m_g_cq, w_uq=m_w_uq, g_ckv=m_g_ckv, w_uk=m_w_uk,
             w_uv=m_w_uv, conv_w=m_conv_w, conv_b=m_conv_b, g_conv_ln=m_g_conv_ln, b_conv_ln=m_b_conv_ln, w_out=m_w_out,
             g_ln1=m_g_ln1, b_ln1=m_b_ln1, w_ff1=m_w_ff1, w_ff2=m_w_ff2, g_ln2=m_g_ln2, b_ln2=m_b_ln2)
    v = dict(ln_in_g=v_ln_in_g, ln_in_b=v_ln_in_b, w_in=v_w_in, g_cq=v_g_cq, w_uq=v_w_uq, g_ckv=v_g_ckv, w_uk=v_w_uk,
             w_uv=v_w_uv, conv_w=v_conv_w, conv_b=v_conv_b, g_conv_ln=v_g_conv_ln, b_conv_ln=v_b_conv_ln, w_out=v_w_out,
             g_ln1=v_g_ln1, b_ln1=v_b_ln1, w_ff1=v_w_ff1, w_ff2=v_w_ff2, g_ln2=v_g_ln2, b_ln2=v_b_ln2)

    sh2 = {n: w[n][0] for n in BIG}
    cidx = lax.axis_index("c").astype(jnp.int32).reshape(1)
    me = 2 * lax.axis_index("x") + lax.axis_index("y")
    kc = jnp.stack([me, lax.axis_index("c")]).astype(jnp.int32)

    early = [sh2[n].astype(BF) if n != "conv_w" else _pad_rows(sh2[n], CONV_K + 1) for n in EARLY]
    gw = dict(zip(EARLY, all_gather_shards(early)))
    gw["conv_w"] = gw["conv_w"][:, :CONV_K]
    late = [sh2[n].astype(BF) for n in LATE]
    ag_ss, ag_rs, ag_src, ag_land, ag_token = chips_send_start(
        "late_weights_start", True, late, [(N_CHIP,) + a.shape for a in late], gw["w_uq"])

    def late_weights(after):
        mine, lands = chips_send_wait("late_weights_wait", True, ag_ss, ag_rs, ag_src, ag_land, after)
        return [lax.dynamic_update_slice(g, a[None], (me, 0, 0)) for g, a in zip(lands, mine)]

    sent = {}

    def early_grads(gw_ff2, gw_ff1):
        full = [gw_ff2, gw_ff1]
        psum = [pair_add(g, r, cidx) for g, r in zip(full, pair_exchange(full, "ff"))]
        ss, rs, src, land, token = chips_send_start(
            "ff_grads_start", False, psum, [(N_CHIP - 1,) + p.shape[1:] for p in psum], psum[0])
        sent.update(ss=ss, rs=rs, src=src, land=land)
        return token

    gx, big, small = local_step(
        x[0], positions[0], ln_in_g, ln_in_b, gw["w_in"], g_cq, gw["w_uq"], g_ckv, gw["w_uk"], gw["w_uv"], gw["conv_w"],
        conv_b, g_conv_ln, b_conv_ln, late_weights, g_ln1, b_ln1, g_ln2, b_ln2, loss_target[0], ag_token, early_grads)

    rest = [n for n in BIG if n not in ("w_ff2", "w_ff1")]
    full = [big[n] for n in rest]
    psum = [pair_add(g, r, cidx) for g, r in zip(full, pair_exchange(full, "rest"))]
    got = chip_exchange(psum)
    ff_psum, ff_got = chips_send_wait("ff_grads_wait", False, sent["ss"], sent["rs"], sent["src"], sent["land"], got[0])
    summed = [chip_add(p, r, kc) for p, r in zip(psum + ff_psum, list(got) + ff_got)]
    gsh = dict(zip(rest + ["w_ff2", "w_ff1"], pair_share(summed, "all")))
    gsh["conv_w"] = gsh["conv_w"][:CONV_K]

    grad, delta, new_m, new_v = {}, {}, {}, {}
    for n in BIG:
        grad[n] = gsh[n][None]
        d_, m_, v_ = adamw("adamw_" + n, sh2[n], gsh[n], m[n][0], v[n][0])
        delta[n], new_m[n], new_v[n] = d_[None], m_[None], v_[None]

    flat = lambda t: jnp.concatenate([t[n].reshape(1, -1) for n in SMALL] + [jnp.zeros((1, LANE), F32)], axis=1)
    g_s, d_s, m_s, v_s = small_allreduce_adamw(small, flat(w), flat(m), flat(v))
    off = 0
    for n in SMALL:
        sz = w[n].size
        for dst, src in ((grad, g_s), (delta, d_s), (new_m, m_s), (new_v, v_s)):
            dst[n] = src[0, off:off + sz].reshape(w[n].shape)
        off += sz
    loss = jnp.sum(g_s[0, off:off + LANE])

    return (loss, gx[None], *[grad[n] for n in WEIGHTS], *[delta[n] for n in WEIGHTS],
            *[new_m[n] for n in WEIGHTS], *[new_v[n] for n in WEIGHTS])
```

```python
import functools

import jax
import jax.numpy as jnp
from jax import lax
from jax.experimental import pallas as pl
from jax.experimental.pallas import tpu as pltpu

F32 = jnp.float32
BF = jnp.bfloat16

HEADS = 8
D_NOPE = 128
D_ROPE = 64
D_V = 128
D_QK = D_NOPE + D_ROPE
R_Q = 512
R_KV = 512
MLA_W = HEADS * D_V
CONV_K = 31
CONV_PAD = CONV_K // 2
ROPE_BASE = 10000.0
LOG2E = 1.4426950408889634
LN2 = 0.6931471805599453
LN_EPS = 1e-5
RMS_EPS = 1e-6
ALPHA = (2.0 * 1) ** 0.25
ADAM_LR = 0.001
ADAM_B1 = 0.9
ADAM_B2 = 0.999
ADAM_EPS = 1e-08
ADAM_WD = 0.01
ADAM_STEP = 10

LANE = 128
SUB = 8
HALO = 16
N_CHIP = 4
MESH = pl.DeviceIdType.MESH
VMEM_MB = 1024 * 1024


def _call(body, **kw):
    return pl.pallas_call(body, **kw)


def _cp(sem, mb=48):
    return pltpu.CompilerParams(dimension_semantics=sem, vmem_limit_bytes=mb * VMEM_MB)


def _sds(shape, dt):
    return jax.ShapeDtypeStruct(shape, dt)


def _dot(a, b):
    return jnp.dot(a, b, preferred_element_type=F32)


def _dot_nt(a, b):
    return lax.dot_general(a, b, (((1,), (1,)), ((), ())), preferred_element_type=F32)


def _dot_tn(a, b):
    return lax.dot_general(a, b, (((0,), (0,)), ((), ())), preferred_element_type=F32)


def _rows8(v):
    t, n = v.shape
    return v.reshape(t // SUB, SUB, n).sum(axis=0)


def _ln_stats(r):
    mu = jnp.mean(r, axis=-1, keepdims=True)
    xc = r - mu
    var = jnp.mean(xc * xc, axis=-1, keepdims=True)
    rstd = lax.rsqrt(var + LN_EPS)
    return xc * rstd, rstd


def _ln_bwd(dy, xhat, rstd, g):
    dyh = dy * g
    m1 = jnp.mean(dyh, axis=-1, keepdims=True)
    m2 = jnp.mean(dyh * xhat, axis=-1, keepdims=True)
    return rstd * (dyh - m1 - xhat * m2)


def _rms_fwd(x, g):
    rr = lax.rsqrt(jnp.mean(x * x, axis=-1, keepdims=True) + RMS_EPS)
    xh = x * rr
    return xh * g, xh, rr


def _rms_bwd(dy, xh, rr, g):
    dyg = dy * g
    return rr * (dyg - xh * jnp.mean(dyg * xh, axis=-1, keepdims=True))


def _rope128(x, cos, sin_signed):
    lane = lax.broadcasted_iota(jnp.int32, x.shape, 1)
    rot = jnp.where(lane < D_ROPE // 2, pltpu.roll(x, LANE - D_ROPE // 2, 1), pltpu.roll(x, D_ROPE // 2, 1))
    return x * cos + rot * sin_signed


def _unrope128(dy, cos, sin_signed):
    t = dy * sin_signed
    lane = lax.broadcasted_iota(jnp.int32, dy.shape, 1)
    rot = jnp.where(lane < D_ROPE // 2, pltpu.roll(t, LANE - D_ROPE // 2, 1), pltpu.roll(t, D_ROPE // 2, 1))
    return dy * cos + rot


def _as_row(col):
    return jnp.transpose(jnp.broadcast_to(col, (col.shape[0], LANE)))[0:1, :]


def _sigmoid(x):
    return 1.0 / (1.0 + jnp.exp(-x))


def _row_chunks(tm, fn, rc=128):
    rc = min(rc, tm)

    def step(ci, carry):
        fn(pl.ds(pl.multiple_of(ci * rc, rc), rc))
        return carry

    lax.fori_loop(0, tm // rc, step, 0)


def _unrolled_loop(n, unroll, fn, init):
    unroll = min(n, unroll)
    assert n % unroll == 0

    def body(t, carry):
        for u in range(unroll):
            carry = fn(t * unroll + u, carry)
        return carry

    return lax.fori_loop(0, n // unroll, body, init)


def _tile(s, want):
    t = min(s, want)
    assert s % t == 0
    return t


def rope_tables(pos_f, invf):
    s = pos_f.shape[0]
    tm = _tile(s, 1024)

    def body(p_ref, f_ref, c_ref, s_ref):
        ang = p_ref[...] * f_ref[...]
        lane = lax.broadcasted_iota(jnp.int32, ang.shape, 1)
        c = jnp.cos(ang)
        sn = jnp.sin(ang)
        c_ref[...] = jnp.where(lane < D_ROPE, c, 0.0)
        s_ref[...] = jnp.where(lane < D_ROPE // 2, -sn, jnp.where(lane < D_ROPE, sn, 0.0))

    return _call(
        body, name="rope_tables", grid=(s // tm,),
        in_specs=[pl.BlockSpec((tm, 1), lambda i: (i, 0)), pl.BlockSpec((1, LANE), lambda i: (0, 0))],
        out_specs=[pl.BlockSpec((tm, LANE), lambda i: (i, 0))] * 2,
        out_shape=[_sds((s, LANE), F32)] * 2,
        compiler_params=_cp(("arbitrary",)),
    )(pos_f, invf)


def ln_in_fwd(x, g, b):
    s, d = x.shape
    tm = _tile(s, 512)

    def body(x_ref, g_ref, b_ref, o_ref, ob_ref):
        xhat, _ = _ln_stats(x_ref[...])
        y = xhat * g_ref[...] + b_ref[...]
        o_ref[...] = y
        ob_ref[...] = y.astype(BF)

    row = pl.BlockSpec((1, d), lambda i: (0, 0))
    tok = pl.BlockSpec((tm, d), lambda i: (i, 0))
    return _call(
        body, name="ln_in_fwd", grid=(s // tm,), in_specs=[tok, row, row], out_specs=[tok, tok],
        out_shape=[_sds((s, d), F32), _sds((s, d), BF)], compiler_params=_cp(("arbitrary",)),
    )(x, g, b)


def matmul(name, a, w, tm, tn, out_dtype=F32):
    s, k = a.shape
    n = w.shape[1]
    tm = _tile(s, tm)
    tn = _tile(n, tn)

    def body(a_ref, w_ref, o_ref):
        o_ref[...] = _dot(a_ref[...], w_ref[...]).astype(o_ref.dtype)

    return _call(
        body, name=name, grid=(s // tm, n // tn),
        in_specs=[pl.BlockSpec((tm, k), lambda i, j: (i, 0)), pl.BlockSpec((k, tn), lambda i, j: (0, j))],
        out_specs=pl.BlockSpec((tm, tn), lambda i, j: (i, j)),
        out_shape=_sds((s, n), out_dtype), compiler_params=_cp(("arbitrary", "arbitrary")),
    )(a, w)


def q_proj(h, g_cq, wuq, cos, sin):
    s = h.shape[0]
    tm = _tile(s, 512)

    def body(h_ref, g_ref, w_ref, c_ref, s_ref, q_ref, n_ref):
        y, _, _ = _rms_fwd(h_ref[...], g_ref[...])
        yb = y.astype(BF)
        n_ref[...] = yb
        q = _dot(yb, w_ref[...])
        c = c_ref[...]
        sn = s_ref[...]
        for hd in range(HEADS):
            q_ref[hd, :, 0:LANE] = q[:, LANE * hd:LANE * (hd + 1)].astype(BF)
            qr = q[:, MLA_W + LANE * hd:MLA_W + LANE * (hd + 1)]
            q_ref[hd, :, LANE:2 * LANE] = _rope128(qr, c, sn).astype(BF)

    return _call(
        body, name="q_proj", grid=(s // tm,),
        in_specs=[pl.BlockSpec((tm, R_Q), lambda i: (i, 0)), pl.BlockSpec((1, R_Q), lambda i: (0, 0)),
                  pl.BlockSpec((R_Q, 2 * MLA_W), lambda i: (0, 0)),
                  pl.BlockSpec((tm, LANE), lambda i: (i, 0)), pl.BlockSpec((tm, LANE), lambda i: (i, 0))],
        out_specs=[pl.BlockSpec((HEADS, tm, 2 * LANE), lambda i: (0, i, 0)), pl.BlockSpec((tm, R_Q), lambda i: (i, 0))],
        out_shape=[_sds((HEADS, s, 2 * LANE), BF), _sds((s, R_Q), BF)], compiler_params=_cp(("arbitrary",)),
    )(h, g_cq, wuq, cos, sin)


def kv_proj(h, g_ckv, wuk, wuv, cos, sin, kr_blk):
    s = h.shape[0]
    tm = _tile(s, 512)

    def body(h_ref, kr_ref, g_ref, wk_ref, wv_ref, c_ref, s_ref, k_ref, v_ref, n_ref):
        y, _, _ = _rms_fwd(h_ref[...], g_ref[...])
        yb = y.astype(BF)
        n_ref[...] = yb
        kn = _dot(yb, wk_ref[...])
        v = _dot(yb, wv_ref[...])
        kr = _rope128(kr_ref[...], c_ref[...], s_ref[...]).astype(BF)
        for hd in range(HEADS):
            k_ref[hd, :, 0:LANE] = kn[:, LANE * hd:LANE * (hd + 1)].astype(BF)
            k_ref[hd, :, LANE:2 * LANE] = kr
            v_ref[hd] = v[:, LANE * hd:LANE * (hd + 1)].astype(BF)

    tab = pl.BlockSpec((tm, LANE), lambda i: (i, 0))
    wsp = pl.BlockSpec((R_KV, MLA_W), lambda i: (0, 0))
    return _call(
        body, name="kv_proj", grid=(s // tm,),
        in_specs=[pl.BlockSpec((tm, R_KV), lambda i: (i, 1)), pl.BlockSpec((tm, LANE), lambda i: (i, kr_blk)),
                  pl.BlockSpec((1, R_KV), lambda i: (0, 0)), wsp, wsp, tab, tab],
        out_specs=[pl.BlockSpec((HEADS, tm, 2 * LANE), lambda i: (0, i, 0)),
                   pl.BlockSpec((HEADS, tm, LANE), lambda i: (0, i, 0)), pl.BlockSpec((tm, R_KV), lambda i: (i, 0))],
        out_shape=[_sds((HEADS, s, 2 * LANE), BF), _sds((HEADS, s, LANE), BF), _sds((s, R_KV), BF)],
        compiler_params=_cp(("arbitrary",)),
    )(h, h, g_ckv, wuk, wuv, cos, sin)


def attn_fwd(qc, kc, v):
    _, s, _ = qc.shape
    tq = _tile(s, 256)
    tk = _tile(s, 512)
    scale = D_QK ** -0.5
    c2 = scale * LOG2E
    nk = s // tk
    nb = tk // LANE
    un = 8

    def body(q_ref, k_ref, v_ref, o_ref, ob_ref, l_ref, s_scr, m_scr):
        q = q_ref[...]

        def scores(j, mpart):
            off = pl.multiple_of(j * tk, tk)
            sc = _dot_nt(q, k_ref[pl.ds(off, tk), :]) * c2
            s_scr[:, pl.ds(off, tk)] = sc
            for b in range(nb):
                mpart = jnp.maximum(mpart, sc[:, LANE * b:LANE * (b + 1)])
            return mpart

        mpart = _unrolled_loop(nk, un, scores, jnp.full((tq, LANE), -jnp.inf, F32))
        m = jnp.max(mpart, axis=-1, keepdims=True)
        m_scr[...] = jnp.broadcast_to(m, (tq, LANE))

        def weigh(j, carry):
            lpart, acc = carry
            off = pl.multiple_of(j * tk, tk)
            ps = []
            for b in range(nb):
                p = jnp.exp2(s_scr[:, pl.ds(off + LANE * b, LANE)] - m_scr[...])
                lpart = lpart + p
                ps.append(p.astype(BF))
            acc = acc + _dot(jnp.concatenate(ps, axis=1), v_ref[pl.ds(off, tk), :])
            return lpart, acc

        lpart, acc = _unrolled_loop(nk, un, weigh, (jnp.zeros((tq, LANE), F32), jnp.zeros((tq, D_V), F32)))
        l = jnp.sum(lpart, axis=-1, keepdims=True)
        o = acc / l
        o_ref[...] = o
        ob_ref[...] = o.astype(BF)
        l_ref[...] = _as_row(m + jnp.log(l) * LOG2E)

    return _call(
        body, name="attn_fwd", grid=(HEADS, s // tq),
        in_specs=[pl.BlockSpec((None, tq, 2 * LANE), lambda h, i: (h, i, 0)),
                  pl.BlockSpec((None, s, 2 * LANE), lambda h, i: (h, 0, 0)),
                  pl.BlockSpec((None, s, LANE), lambda h, i: (h, 0, 0))],
        out_specs=[pl.BlockSpec((tq, LANE), lambda h, i: (i, h)), pl.BlockSpec((tq, LANE), lambda h, i: (i, h)),
                   pl.BlockSpec((None, 1, tq), lambda h, i: (h, 0, i))],
        out_shape=[_sds((s, MLA_W), F32), _sds((s, MLA_W), BF), _sds((HEADS, 1, s), F32)],
        scratch_shapes=[pltpu.VMEM((tq, s + LANE), F32), pltpu.VMEM((tq, LANE), F32)],
        compiler_params=_cp(("arbitrary", "arbitrary")),
    )(qc, kc, v)


def _halo_specs(tm, s, width, col):
    r = tm // HALO
    nb = s // HALO
    cur = pl.BlockSpec((tm, width), lambda i: (i, col))
    prev = pl.BlockSpec((HALO, width), lambda i: (jnp.maximum(i * r - 1, 0), col))
    nxt = pl.BlockSpec((HALO, width), lambda i: (jnp.minimum((i + 1) * r, nb - 1), col))
    return cur, prev, nxt


def _slab_shapes(tm, c):
    return (tm + 2 * HALO, c + LANE), (SUB - 1, tm + 2 * HALO - SUB, c + LANE)


def _fill_slab(slab, tm, prev, cur, nxt):
    i = pl.program_id(0)
    last = pl.num_programs(0) - 1
    c = cur.shape[1]
    slab[0:HALO, 0:c] = jnp.where(i > 0, prev, 0.0)
    slab[HALO:HALO + tm, 0:c] = cur
    slab[HALO + tm:2 * HALO + tm, 0:c] = jnp.where(i < last, nxt, 0.0)


def _rotate_slab(slab, rot, tm):
    rows = tm + 2 * HALO - SUB
    c = slab.shape[1] - LANE
    for b in range(1, SUB):
        rot[b - 1, :, 0:c] = slab[pl.ds(b, rows), 0:c]


def _shifted(slab, rot, start, rc, cs):
    b = start % SUB
    if b == 0:
        return slab[pl.ds(start, rc), cs]
    return rot[b - 1, pl.ds(start - b, rc), cs]


def conv_fwd(h, conv_w, conv_b, g_ln, b_ln):
    s = h.shape[0]
    c = conv_w.shape[1]
    tm = _tile(s, 256)
    rc = _tile(tm, 64)

    def body(a_ref, ap_ref, an_ref, g_ref, gp_ref, gn_ref, w_ref, cb_ref, lg_ref, lb_ref, co_ref, uc_ref, slab, rot):
        _fill_slab(slab, tm, ap_ref[...] * _sigmoid(gp_ref[...]), a_ref[...] * _sigmoid(g_ref[...]),
                   an_ref[...] * _sigmoid(gn_ref[...]))
        _rotate_slab(slab, rot, tm)

        def lane_block(cb, carry):
            cs = pl.ds(pl.multiple_of(cb * LANE, LANE), LANE)
            for r0 in range(0, tm, rc):
                acc = jnp.zeros((rc, LANE), F32)
                for k in range(CONV_K):
                    acc = acc + w_ref[k:k + 1, cs] * _shifted(slab, rot, r0 + HALO - CONV_PAD + k, rc, cs)
                uc_ref[r0:r0 + rc, cs] = acc + cb_ref[:, cs]
            return carry

        lax.fori_loop(0, c // LANE, lane_block, 0)
        xhat, _ = _ln_stats(uc_ref[...])
        cl = xhat * lg_ref[...] + lb_ref[...]
        co_ref[...] = (cl * _sigmoid(cl)).astype(BF)

    a_specs = _halo_specs(tm, s, c, 1)
    g_specs = _halo_specs(tm, s, c, 2)
    row = pl.BlockSpec((1, c), lambda i: (0, 0))
    tok = pl.BlockSpec((tm, c), lambda i: (i, 0))
    return _call(
        body, name="conv_fwd", grid=(s // tm,),
        in_specs=[*a_specs, *g_specs, pl.BlockSpec(conv_w.shape, lambda i: (0, 0)), row, row, row],
        out_specs=[tok, tok], out_shape=[_sds((s, c), BF), _sds((s, c), F32)],
        scratch_shapes=[pltpu.VMEM(shp, F32) for shp in _slab_shapes(tm, c)],
        compiler_params=_cp(("arbitrary",)),
    )(h, h, h, h, h, h, conv_w, conv_b, g_ln, b_ln)


def out_proj_ln1(ob, co, wout, x0, g1, b1):
    s, d = x0.shape
    kh = ob.shape[1]
    tm = _tile(s, 256)

    def body(o_ref, c_ref, w_ref, x_ref, g_ref, b_ref, r_ref, x1_ref, x1b_ref, acc):
        k = pl.program_id(1)

        @pl.when(k == 0)
        def _():
            acc[...] = _dot(o_ref[...], w_ref[...])

        @pl.when(k == 1)
        def _():
            r = ALPHA * x_ref[...] + (acc[...] + _dot(c_ref[...], w_ref[...]))
            r_ref[...] = r
            xhat, _ = _ln_stats(r)
            y = xhat * g_ref[...] + b_ref[...]
            x1_ref[...] = y
            x1b_ref[...] = y.astype(BF)

    half = pl.BlockSpec((tm, kh), lambda i, k: (i, 0))
    tok = pl.BlockSpec((tm, d), lambda i, k: (i, 0))
    row = pl.BlockSpec((1, d), lambda i, k: (0, 0))
    return _call(
        body, name="out_proj_ln1", grid=(s // tm, 2),
        in_specs=[half, half, pl.BlockSpec((kh, d), lambda i, k: (k, 0)), tok, row, row],
        out_specs=[tok, tok, tok], out_shape=[_sds((s, d), F32), _sds((s, d), F32), _sds((s, d), BF)],
        scratch_shapes=[pltpu.VMEM((tm, d), F32)], compiler_params=_cp(("arbitrary", "arbitrary")),
    )(ob, co, wout, x0, g1, b1)


def ff1_fwd(x1b, wff1_g):
    s, d = x1b.shape
    nsh, _, fs = wff1_g.shape
    tm = _tile(s, 1024)
    tn = _tile(fs, 1024)
    per = fs // tn

    def body(a_ref, w_ref, r_ref, a1_ref):
        r = jnp.maximum(_dot(a_ref[...], w_ref[...]), 0.0)
        r_ref[...] = r.astype(BF)
        a1_ref[...] = (r * r).astype(BF)

    out = pl.BlockSpec((tm, tn), lambda i, j: (i, j))
    return _call(
        body, name="ff1_fwd", grid=(s // tm, nsh * per),
        in_specs=[pl.BlockSpec((tm, d), lambda i, j: (i, 0)),
                  pl.BlockSpec((None, d, tn), lambda i, j: (j // per, 0, j % per))],
        out_specs=[out, out], out_shape=[_sds((s, nsh * fs), BF)] * 2,
        compiler_params=_cp(("arbitrary", "arbitrary")),
    )(x1b, wff1_g)


def ff2_ln2_loss(a1b, wff2, x1, target, g2, b2):
    s, f = a1b.shape
    d = x1.shape[1]
    tm = _tile(s, 512)
    tk = _tile(f, 1024)
    nk = f // tk

    def body(a_ref, w_ref, x_ref, t_ref, g_ref, b_ref, dr_ref, drb_ref, loss_ref, dg_ref, db_ref, acc):
        i = pl.program_id(0)
        k = pl.program_id(1)

        @pl.when(k == 0)
        def _():
            acc[...] = _dot(a_ref[...], w_ref[...])

        @pl.when(k > 0)
        def _():
            acc[...] += _dot(a_ref[...], w_ref[...])

        @pl.when(jnp.logical_and(i == 0, k == 0))
        def _():
            loss_ref[...] = jnp.zeros_like(loss_ref)
            dg_ref[...] = jnp.zeros_like(dg_ref)
            db_ref[...] = jnp.zeros_like(db_ref)

        @pl.when(k == nk - 1)
        def _():
            g = g_ref[...]

            def chunk(rows):
                r = ALPHA * x_ref[rows, :] + acc[rows, :]
                xhat, rstd = _ln_stats(r)
                e = xhat * g + b_ref[...] - t_ref[rows, :]
                e2 = _rows8(e * e)
                part = e2[:, 0:LANE]
                for c in range(1, d // LANE):
                    part = part + e2[:, LANE * c:LANE * (c + 1)]
                loss_ref[...] += part * (0.5 / d)
                dy = e * (1.0 / d)
                dg_ref[...] += _rows8(dy * xhat)
                db_ref[...] += _rows8(dy)
                dr = _ln_bwd(dy, xhat, rstd, g)
                dr_ref[rows, :] = dr
                drb_ref[rows, :] = dr.astype(BF)

            _row_chunks(tm, chunk)

    tok = pl.BlockSpec((tm, d), lambda i, k: (i, 0))
    row = pl.BlockSpec((1, d), lambda i, k: (0, 0))
    accs = pl.BlockSpec((SUB, d), lambda i, k: (0, 0))
    return _call(
        body, name="ff2_ln2_loss", grid=(s // tm, nk),
        in_specs=[pl.BlockSpec((tm, tk), lambda i, k: (i, k)), pl.BlockSpec((tk, d), lambda i, k: (k, 0)),
                  tok, tok, row, row],
        out_specs=[tok, tok, pl.BlockSpec((SUB, LANE), lambda i, k: (0, 0)), accs, accs],
        out_shape=[_sds((s, d), F32), _sds((s, d), BF), _sds((SUB, LANE), F32), _sds((SUB, d), F32), _sds((SUB, d), F32)],
        scratch_shapes=[pltpu.VMEM((tm, d), F32)], compiler_params=_cp(("arbitrary", "arbitrary"), 56),
    )(a1b, wff2, x1, target, g2, b2)


def ff2_bwd_act(dr2b, wff2t, rb):
    s, d = dr2b.shape
    f = wff2t.shape[1]
    tm = _tile(s, 1024)
    tn = _tile(f, 1024)

    def body(a_ref, w_ref, r_ref, o_ref):
        o_ref[...] = (_dot(a_ref[...], w_ref[...]) * (2.0 * r_ref[...].astype(F32))).astype(BF)

    return _call(
        body, name="ff2_bwd_act", grid=(s // tm, f // tn),
        in_specs=[pl.BlockSpec((tm, d), lambda i, j: (i, 0)), pl.BlockSpec((d, tn), lambda i, j: (0, j)),
                  pl.BlockSpec((tm, tn), lambda i, j: (i, j))],
        out_specs=pl.BlockSpec((tm, tn), lambda i, j: (i, j)), out_shape=_sds((s, f), BF),
        compiler_params=_cp(("arbitrary", "arbitrary")),
    )(dr2b, wff2t, rb)


def wgrad(name, a, b, tm, tn, tk=2048, shards=1):
    s, m = a.shape
    n = b.shape[1]
    tm = _tile(m, tm)
    ns = n // shards
    tn = _tile(ns, tn)
    tk = _tile(s, tk)
    per = ns // tn

    def body(a_ref, b_ref, o_ref):
        k = pl.program_id(2)

        @pl.when(k == 0)
        def _():
            o_ref[...] = _dot_tn(a_ref[...], b_ref[...])

        @pl.when(k > 0)
        def _():
            o_ref[...] += _dot_tn(a_ref[...], b_ref[...])

    return _call(
        body, name=name, grid=(m // tm, n // tn, s // tk),
        in_specs=[pl.BlockSpec((tk, tm), lambda i, j, k: (k, i)), pl.BlockSpec((tk, tn), lambda i, j, k: (k, j))],
        out_specs=pl.BlockSpec((None, tm, tn), lambda i, j, k: (j // per, i, j % per)),
        out_shape=_sds((shards, m, ns), F32), compiler_params=_cp(("arbitrary", "arbitrary", "arbitrary")),
    )(a, b)


def ff1_bwd_ln1(df1b, wff1t, dr2, r1, g1):
    s, f = df1b.shape
    d = dr2.shape[1]
    tm = _tile(s, 512)
    tk = _tile(f, 1024)
    nk = f // tk

    def body(a_ref, w_ref, d2_ref, r_ref, g_ref, dr_ref, drb_ref, dg_ref, db_ref, acc):
        i = pl.program_id(0)
        k = pl.program_id(1)

        @pl.when(k == 0)
        def _():
            acc[...] = _dot(a_ref[...], w_ref[...])

        @pl.when(k > 0)
        def _():
            acc[...] += _dot(a_ref[...], w_ref[...])

        @pl.when(jnp.logical_and(i == 0, k == 0))
        def _():
            dg_ref[...] = jnp.zeros_like(dg_ref)
            db_ref[...] = jnp.zeros_like(db_ref)

        @pl.when(k == nk - 1)
        def _():
            g = g_ref[...]

            def chunk(rows):
                dy = ALPHA * d2_ref[rows, :] + acc[rows, :]
                xhat, rstd = _ln_stats(r_ref[rows, :])
                dg_ref[...] += _rows8(dy * xhat)
                db_ref[...] += _rows8(dy)
                dr = _ln_bwd(dy, xhat, rstd, g)
                dr_ref[rows, :] = dr
                drb_ref[rows, :] = dr.astype(BF)

            _row_chunks(tm, chunk)

    tok = pl.BlockSpec((tm, d), lambda i, k: (i, 0))
    accs = pl.BlockSpec((SUB, d), lambda i, k: (0, 0))
    return _call(
        body, name="ff1_bwd_ln1", grid=(s // tm, nk),
        in_specs=[pl.BlockSpec((tm, tk), lambda i, k: (i, k)), pl.BlockSpec((tk, d), lambda i, k: (k, 0)),
                  tok, tok, pl.BlockSpec((1, d), lambda i, k: (0, 0))],
        out_specs=[tok, tok, accs, accs],
        out_shape=[_sds((s, d), F32), _sds((s, d), BF), _sds((SUB, d), F32), _sds((SUB, d), F32)],
        scratch_shapes=[pltpu.VMEM((tm, d), F32)], compiler_params=_cp(("arbitrary", "arbitrary"), 56),
    )(df1b, wff1t, dr2, r1, g1)


def out_proj_bwd(dr1b, woutt, o):
    s, d = dr1b.shape
    tm = _tile(s, 256)

    def body(a_ref, w_ref, o_ref, do_ref, dot_ref, dc_ref, dl_ref):
        dcat = _dot(a_ref[...], w_ref[...])
        do = dcat[:, 0:MLA_W]
        do_ref[...] = do.astype(BF)
        dc_ref[...] = dcat[:, MLA_W:]
        prod = do * o_ref[...]
        for hd in range(HEADS):
            hs = slice(LANE * hd, LANE * (hd + 1))
            dl_ref[hd] = _as_row(jnp.sum(prod[:, hs], axis=-1, keepdims=True))
            dot_ref[hd] = do[:, hs].T.astype(BF)

    half = pl.BlockSpec((tm, MLA_W), lambda i: (i, 0))
    return _call(
        body, name="out_proj_bwd", grid=(s // tm,),
        in_specs=[pl.BlockSpec((tm, d), lambda i: (i, 0)), pl.BlockSpec((d, d), lambda i: (0, 0)), half],
        out_specs=[half, pl.BlockSpec((HEADS, LANE, tm), lambda i: (0, 0, i)),
                   pl.BlockSpec((tm, d - MLA_W), lambda i: (i, 0)), pl.BlockSpec((HEADS, 1, tm), lambda i: (0, 0, i))],
        out_shape=[_sds((s, MLA_W), BF), _sds((HEADS, LANE, s), BF), _sds((s, d - MLA_W), F32), _sds((HEADS, 1, s), F32)],
        compiler_params=_cp(("arbitrary",)),
    )(dr1b, woutt, o)


def conv_bwd_ln(uc, dco, g_ln, b_ln):
    s, c = uc.shape
    tm = _tile(s, 512)

    def body(u_ref, d_ref, g_ref, b_ref, du_ref, dg_ref, db_ref, dcb_ref):
        @pl.when(pl.program_id(0) == 0)
        def _():
            dg_ref[...] = jnp.zeros_like(dg_ref)
            db_ref[...] = jnp.zeros_like(db_ref)
            dcb_ref[...] = jnp.zeros_like(dcb_ref)

        xhat, rstd = _ln_stats(u_ref[...])
        g = g_ref[...]
        cl = xhat * g + b_ref[...]
        sg = _sigmoid(cl)
        dcl = d_ref[...] * (sg * (1.0 + cl * (1.0 - sg)))
        dg_ref[...] += _rows8(dcl * xhat)
        db_ref[...] += _rows8(dcl)
        du = _ln_bwd(dcl, xhat, rstd, g)
        du_ref[...] = du
        dcb_ref[...] += _rows8(du)

    tok = pl.BlockSpec((tm, c), lambda i: (i, 0))
    row = pl.BlockSpec((1, c), lambda i: (0, 0))
    accs = pl.BlockSpec((SUB, c), lambda i: (0, 0))
    return _call(
        body, name="conv_bwd_ln", grid=(s // tm,), in_specs=[tok, tok, row, row], out_specs=[tok, accs, accs, accs],
        out_shape=[_sds((s, c), F32)] + [_sds((SUB, c), F32)] * 3, compiler_params=_cp(("arbitrary",)),
    )(uc, dco, g_ln, b_ln)


def conv_bwd_taps(h, duc, conv_w):
    s, c = duc.shape
    tm = _tile(s, 256)
    rc = _tile(tm, 64)

    def body(a_ref, ap_ref, an_ref, g_ref, gp_ref, gn_ref, d_ref, dp_ref, dn_ref, w_ref, o_ref, dw_ref,
             uslab, dslab, du_s, urot, drot, dw8):
        @pl.when(pl.program_id(0) == 0)
        def _():
            dw8[...] = jnp.zeros_like(dw8)

        sg = _sigmoid(g_ref[...])
        a = a_ref[...]
        _fill_slab(uslab, tm, ap_ref[...] * _sigmoid(gp_ref[...]), a * sg, an_ref[...] * _sigmoid(gn_ref[...]))
        _fill_slab(dslab, tm, dp_ref[...], d_ref[...], dn_ref[...])
        _rotate_slab(uslab, urot, tm)
        _rotate_slab(dslab, drot, tm)

        def lane_block(cb, carry):
            cs = pl.ds(pl.multiple_of(cb * LANE, LANE), LANE)
            for r0 in range(0, tm, rc):
                acc = jnp.zeros((rc, LANE), F32)
                for k in range(CONV_K):
                    acc = acc + w_ref[k:k + 1, cs] * _shifted(dslab, drot, r0 + HALO + CONV_PAD - k, rc, cs)
                du_s[r0:r0 + rc, cs] = acc
            return carry

        def lane_block_taps(cb, carry):
            cs = pl.ds(pl.multiple_of(cb * LANE, LANE), LANE)
            parts = []
            for k in range(CONV_K):
                prod = None
                for r0 in range(0, tm, rc):
                    t = dslab[pl.ds(r0 + HALO, rc), cs] * _shifted(uslab, urot, r0 + HALO - CONV_PAD + k, rc, cs)
                    prod = t if prod is None else prod + t
                parts.append(_rows8(prod))
            rows = SUB * CONV_K
            dw8[0:rows, cs] = dw8[0:rows, cs] + jnp.concatenate(parts, axis=0)
            return carry

        lax.fori_loop(0, c // LANE, lane_block, 0)
        lax.fori_loop(0, c // LANE, lane_block_taps, 0)

        @pl.when(pl.program_id(0) == pl.num_programs(0) - 1)
        def _():
            dw_ref[...] = jnp.zeros_like(dw_ref)
            for k in range(CONV_K):
                dw_ref[k:k + 1, :] = jnp.sum(dw8[SUB * k:SUB * (k + 1), :], axis=0, keepdims=True)

        du = du_s[...]
        o_ref[:, 0:c] = (du * sg).astype(BF)
        o_ref[:, c:2 * c] = (du * a * sg * (1.0 - sg)).astype(BF)

    a_specs = _halo_specs(tm, s, c, 1)
    g_specs = _halo_specs(tm, s, c, 2)
    d_specs = _halo_specs(tm, s, c, 0)
    wsp = pl.BlockSpec(conv_w.shape, lambda i: (0, 0))
    return _call(
        body, name="conv_bwd_taps", grid=(s // tm,), in_specs=[*a_specs, *g_specs, *d_specs, wsp],
        out_specs=[pl.BlockSpec((tm, 2 * c), lambda i: (i, 0)), wsp],
        out_shape=[_sds((s, 2 * c), BF), _sds(conv_w.shape, F32)],
        scratch_shapes=[pltpu.VMEM(_slab_shapes(tm, c)[0], F32), pltpu.VMEM(_slab_shapes(tm, c)[0], F32), pltpu.VMEM((tm, c), F32),
                        pltpu.VMEM(_slab_shapes(tm, c)[1], F32), pltpu.VMEM(_slab_shapes(tm, c)[1], F32),
                        pltpu.VMEM((SUB * conv_w.shape[0], c), F32)],
        compiler_params=_cp(("arbitrary",)),
    )(h, h, h, h, h, h, duc, duc, duc, conv_w)


def attn_bwd(qc, kc, kct, v, dob, dot, lse_r, delta_r):
    _, s, _ = qc.shape
    tk = _tile(s, 512)
    tq = _tile(s, 512)
    scale = D_QK ** -0.5
    c2 = scale * LOG2E

    def body(k_ref, kt_ref, v_ref, q_ref, do_ref, dot_ref, l_ref, dl_ref, dqt_ref, dk_ref, dvt_ref):
        @pl.when(pl.program_id(1) == 0)
        def _():
            dqt_ref[...] = jnp.zeros_like(dqt_ref)

        k = k_ref[...]
        kt = kt_ref[...]
        vv = v_ref[...]

        def step(i, carry):
            dk, dvt = carry
            off = pl.multiple_of(i * tq, tq)
            q = q_ref[pl.ds(off, tq), :]
            do = do_ref[pl.ds(off, tq), :]
            pt = jnp.exp2(_dot_nt(k, q) * c2 - l_ref[:, pl.ds(off, tq)])
            dvt = dvt + _dot_nt(dot_ref[:, pl.ds(off, tq)], pt.astype(BF))
            dpt = _dot_nt(vv, do)
            dsb = (pt * (dpt - dl_ref[:, pl.ds(off, tq)]) * scale).astype(BF)
            dk = dk + _dot(dsb, q)
            dqt_ref[:, pl.ds(off, tq)] += _dot(kt, dsb)
            return dk, dvt

        dk, dvt = _unrolled_loop(s // tq, 4, step, (jnp.zeros((tk, 2 * LANE), F32), jnp.zeros((LANE, tk), F32)))
        dk_ref[...] = dk
        dvt_ref[...] = dvt

    rowv = pl.BlockSpec((None, 1, s), lambda h, j: (h, 0, 0))
    return _call(
        body, name="attn_bwd", grid=(HEADS, s // tk),
        in_specs=[pl.BlockSpec((None, tk, 2 * LANE), lambda h, j: (h, j, 0)),
                  pl.BlockSpec((None, 2 * LANE, tk), lambda h, j: (h, 0, j)),
                  pl.BlockSpec((None, tk, LANE), lambda h, j: (h, j, 0)),
                  pl.BlockSpec((None, s, 2 * LANE), lambda h, j: (h, 0, 0)),
                  pl.BlockSpec((s, LANE), lambda h, j: (0, h)),
                  pl.BlockSpec((None, LANE, s), lambda h, j: (h, 0, 0)), rowv, rowv],
        out_specs=[pl.BlockSpec((None, 2 * LANE, s), lambda h, j: (h, 0, 0)),
                   pl.BlockSpec((None, tk, 2 * LANE), lambda h, j: (h, j, 0)),
                   pl.BlockSpec((None, LANE, tk), lambda h, j: (h, 0, j))],
        out_shape=[_sds((HEADS, 2 * LANE, s), F32), _sds((HEADS, s, 2 * LANE), F32), _sds((HEADS, LANE, s), F32)],
        compiler_params=_cp(("arbitrary", "arbitrary"), 56),
    )(kc, kct, v, qc, dob, dot, lse_r, delta_r)


def q_bwd(dqt, h, g_cq, wuqt, cos, sin):
    s = h.shape[0]
    tm = _tile(s, 256)

    def body(d_ref, h_ref, g_ref, w_ref, c_ref, s_ref, dq_ref, dc_ref, dg_ref):
        @pl.when(pl.program_id(0) == 0)
        def _():
            dg_ref[...] = jnp.zeros_like(dg_ref)

        c = c_ref[...]
        sn = s_ref[...]
        for hd in range(HEADS):
            t = d_ref[hd].T
            dq_ref[:, LANE * hd:LANE * (hd + 1)] = t[:, 0:LANE].astype(BF)
            dq_ref[:, MLA_W + LANE * hd:MLA_W + LANE * (hd + 1)] = _unrope128(t[:, LANE:2 * LANE], c, sn).astype(BF)
        dy = _dot(dq_ref[...], w_ref[...])
        g = g_ref[...]
        _, xh, rr = _rms_fwd(h_ref[...], g)
        dg_ref[...] += _rows8(dy * xh)
        dc_ref[...] = _rms_bwd(dy, xh, rr, g).astype(BF)

    tab = pl.BlockSpec((tm, LANE), lambda i: (i, 0))
    return _call(
        body, name="q_bwd", grid=(s // tm,),
        in_specs=[pl.BlockSpec((HEADS, 2 * LANE, tm), lambda i: (0, 0, i)), pl.BlockSpec((tm, R_Q), lambda i: (i, 0)),
                  pl.BlockSpec((1, R_Q), lambda i: (0, 0)), pl.BlockSpec((2 * MLA_W, R_Q), lambda i: (0, 0)), tab, tab],
        out_specs=[pl.BlockSpec((tm, 2 * MLA_W), lambda i: (i, 0)), pl.BlockSpec((tm, R_Q), lambda i: (i, 0)),
                   pl.BlockSpec((SUB, R_Q), lambda i: (0, 0))],
        out_shape=[_sds((s, 2 * MLA_W), BF), _sds((s, R_Q), BF), _sds((SUB, R_Q), F32)],
        compiler_params=_cp(("arbitrary",)),
    )(dqt, h, g_cq, wuqt, cos, sin)


def kv_bwd(dk, dv, h, g_ckv, wukt, wuvt, cos, sin):
    s = h.shape[0]
    tm = _tile(s, 256)

    def body(dk_ref, dv_ref, h_ref, g_ref, wk_ref, wv_ref, c_ref, s_ref, dkn_ref, dvb_ref, dc_ref, dkr_ref, dg_ref):
        @pl.when(pl.program_id(0) == 0)
        def _():
            dg_ref[...] = jnp.zeros_like(dg_ref)

        dkr = dk_ref[0, :, LANE:2 * LANE]
        for hd in range(HEADS):
            dkn_ref[:, LANE * hd:LANE * (hd + 1)] = dk_ref[hd, :, 0:LANE].astype(BF)
            dvb_ref[:, LANE * hd:LANE * (hd + 1)] = dv_ref[hd].T.astype(BF)
            if hd > 0:
                dkr = dkr + dk_ref[hd, :, LANE:2 * LANE]
        dkr_ref[...] = _unrope128(dkr, c_ref[...], s_ref[...]).astype(BF)
        dy = _dot(dkn_ref[...], wk_ref[...]) + _dot(dvb_ref[...], wv_ref[...])
        g = g_ref[...]
        _, xh, rr = _rms_fwd(h_ref[...], g)
        dg_ref[...] += _rows8(dy * xh)
        dc_ref[...] = _rms_bwd(dy, xh, rr, g).astype(BF)

    tab = pl.BlockSpec((tm, LANE), lambda i: (i, 0))
    wsp = pl.BlockSpec((MLA_W, R_KV), lambda i: (0, 0))
    wide = pl.BlockSpec((tm, MLA_W), lambda i: (i, 0))
    return _call(
        body, name="kv_bwd", grid=(s // tm,),
        in_specs=[pl.BlockSpec((HEADS, tm, 2 * LANE), lambda i: (0, i, 0)), pl.BlockSpec((HEADS, LANE, tm), lambda i: (0, 0, i)),
                  pl.BlockSpec((tm, R_KV), lambda i: (i, 1)), pl.BlockSpec((1, R_KV), lambda i: (0, 0)), wsp, wsp, tab, tab],
        out_specs=[wide, wide, pl.BlockSpec((tm, R_KV), lambda i: (i, 0)), tab, pl.BlockSpec((SUB, R_KV), lambda i: (0, 0))],
        out_shape=[_sds((s, MLA_W), BF), _sds((s, MLA_W), BF), _sds((s, R_KV), BF), _sds((s, LANE), BF), _sds((SUB, R_KV), F32)],
        compiler_params=_cp(("arbitrary",)),
    )(dk, dv, h, g_ckv, wukt, wuvt, cos, sin)


def in_proj_bwd_ln(dh, wint, dr1, x, g_in):
    s, hc = dh.shape
    d = x.shape[1]
    tm = _tile(s, 512)
    tk = _tile(hc, 640)
    nk = hc // tk

    def body(a_ref, w_ref, d1_ref, x_ref, g_ref, gx_ref, dg_ref, db_ref, acc):
        i = pl.program_id(0)
        k = pl.program_id(1)

        @pl.when(k == 0)
        def _():
            acc[...] = _dot(a_ref[...], w_ref[...])

        @pl.when(k > 0)
        def _():
            acc[...] += _dot(a_ref[...], w_ref[...])

        @pl.when(jnp.logical_and(i == 0, k == 0))
        def _():
            dg_ref[...] = jnp.zeros_like(dg_ref)
            db_ref[...] = jnp.zeros_like(db_ref)

        @pl.when(k == nk - 1)
        def _():
            g = g_ref[...]

            def chunk(rows):
                dy = ALPHA * d1_ref[rows, :] + acc[rows, :]
                xhat, rstd = _ln_stats(x_ref[rows, :])
                dg_ref[...] += _rows8(dy * xhat)
                db_ref[...] += _rows8(dy)
                gx_ref[rows, :] = _ln_bwd(dy, xhat, rstd, g)

            _row_chunks(tm, chunk)

    tok = pl.BlockSpec((tm, d), lambda i, k: (i, 0))
    accs = pl.BlockSpec((SUB, d), lambda i, k: (0, 0))
    return _call(
        body, name="in_proj_bwd_ln", grid=(s // tm, nk),
        in_specs=[pl.BlockSpec((tm, tk), lambda i, k: (i, k)), pl.BlockSpec((tk, d), lambda i, k: (k, 0)),
                  tok, tok, pl.BlockSpec((1, d), lambda i, k: (0, 0))],
        out_specs=[tok, accs, accs], out_shape=[_sds((s, d), F32), _sds((SUB, d), F32), _sds((SUB, d), F32)],
        scratch_shapes=[pltpu.VMEM((tm, d), F32)], compiler_params=_cp(("arbitrary", "arbitrary")),
    )(dh, wint, dr1, x, g_in)


def _adamw_math(w, g, m, v):
    m = ADAM_B1 * m + (1.0 - ADAM_B1) * g
    v = ADAM_B2 * v + (1.0 - ADAM_B2) * (g * g)
    m_hat = m / (1.0 - ADAM_B1 ** ADAM_STEP)
    v_hat = v / (1.0 - ADAM_B2 ** ADAM_STEP)
    delta = -ADAM_LR * (m_hat / (jnp.sqrt(v_hat) + ADAM_EPS) + ADAM_WD * w)
    return delta, m, v


def adamw(name, w, g, m, v):
    r, c = w.shape
    tr = _row_tile(r, c)

    def body(w_ref, g_ref, m_ref, v_ref, d_ref, mo_ref, vo_ref):
        d_ref[...], mo_ref[...], vo_ref[...] = _adamw_math(w_ref[...], g_ref[...], m_ref[...], v_ref[...])

    blk = pl.BlockSpec((tr, c), lambda i: (i, 0))
    return _call(
        body, name=name, grid=(r // tr,), in_specs=[blk] * 4, out_specs=[blk] * 3,
        out_shape=[_sds((r, c), F32)] * 3, compiler_params=_cp(("arbitrary",)),
    )(w, g, m, v)


def _coords():
    return lax.axis_index("x"), lax.axis_index("y"), lax.axis_index("c")


def _other_chips(x, y):
    return [(1 - x, y, 2 * (1 - x) + y), (x, 1 - y, 2 * x + 1 - y), (1 - x, 1 - y, 2 * (1 - x) + 1 - y)]


ANY = pl.BlockSpec(memory_space=pl.ANY)
HBM = pl.BlockSpec(memory_space=pltpu.HBM)
SEM = pl.BlockSpec(memory_space=pltpu.SEMAPHORE)
EFFECT = pltpu.SideEffectType.DATAFLOW_SIDE_EFFECTING


def _in_hbm(a):
    return pltpu.with_memory_space_constraint(a, pltpu.HBM)


def _split_plan(mode, src, land, x, y, c):
    if mode == "pair":
        rh = src.shape[1] // 2
        return [((x, y, 1 - c), src.at[:, pl.ds((1 - c) * rh, rh)], land, land)]
    me = 2 * x + y
    plan = []
    for j, (px, py, pk) in enumerate(_other_chips(x, y)):
        if mode == "gather":
            plan.append(((px, py, c), src, land.at[me], land.at[pk]))
        else:
            plan.append(((px, py, c), src.at[pk], land.at[j], land.at[j]))
    return plan


def _plan_len(mode):
    return 1 if mode == "pair" else N_CHIP - 1


def split_send_start(name, mode, srcs, land_shapes, order_after):
    n = len(srcs)
    np_ = _plan_len(mode)

    def body(*refs):
        ins, lands = refs[:n], refs[n:2 * n]
        ss, rs = refs[2 * n + 1], refs[2 * n + 2]
        token = refs[-1]
        x, y, c = _coords()
        for a in range(n):
            for j, (peer, src, dst, _) in enumerate(_split_plan(mode, ins[a], lands[a], x, y, c)):
                pltpu.make_async_remote_copy(src_ref=src, dst_ref=dst, send_sem=ss.at[np_ * a + j], recv_sem=rs.at[np_ * a + j],
                                             device_id=peer, device_id_type=MESH).start()
        token[...] = jnp.zeros_like(token)

    lands = [lax.empty(shp, s.dtype) for shp, s in zip(land_shapes, srcs)]
    outs = _call(
        body, name=name,
        out_shape=(pltpu.SemaphoreType.DMA((np_ * n,)), pltpu.SemaphoreType.DMA((np_ * n,)),
                   *[pltpu.HBM(s.shape, s.dtype) for s in srcs], *[pltpu.HBM(l.shape, l.dtype) for l in lands],
                   _sds((SUB, LANE), F32)),
        in_specs=[HBM] * (2 * n) + [ANY], out_specs=(SEM, SEM, *[HBM] * (2 * n), pl.BlockSpec(memory_space=pltpu.VMEM)),
        input_output_aliases={a: 2 + a for a in range(2 * n)},
        compiler_params=pltpu.CompilerParams(has_side_effects=EFFECT),
    )(*[_in_hbm(s) for s in srcs], *[_in_hbm(l) for l in lands], order_after)
    return outs[0], outs[1], list(outs[2:2 + n]), list(outs[2 + n:2 + 2 * n]), outs[-1]


def split_send_wait(name, mode, ss, rs, srcs, lands, order_after):
    n = len(srcs)
    np_ = _plan_len(mode)

    def body(*refs):
        ins, lnd = refs[:n], refs[n:2 * n]
        s_ref, r_ref = refs[2 * n], refs[2 * n + 1]
        x, y, c = _coords()
        for a in range(n):
            for j, (peer, src, _, got) in enumerate(_split_plan(mode, ins[a], lnd[a], x, y, c)):
                cp = pltpu.make_async_remote_copy(src_ref=src, dst_ref=got, send_sem=s_ref.at[np_ * a + j], recv_sem=r_ref.at[np_ * a + j],
                                                  device_id=peer, device_id_type=MESH)
                cp.wait_send()
                cp.wait_recv()

    outs = _call(
        body, name=name, out_shape=tuple(pltpu.HBM(t.shape, t.dtype) for t in (*srcs, *lands)),
        in_specs=[HBM] * (2 * n) + [SEM, SEM, ANY], out_specs=tuple([HBM] * (2 * n)),
        input_output_aliases={a: a for a in range(2 * n)},
        compiler_params=pltpu.CompilerParams(has_side_effects=EFFECT),
    )(*srcs, *lands, ss, rs, order_after)
    return list(outs[:n]), list(outs[n:])


def all_gather_shards(shards):
    n = len(shards)

    def body(*refs):
        ins, outs = refs[:n], refs[n:2 * n]
        ici_s, ici_r, d2d_s, d2d_r = refs[2 * n:]
        x, y, c = _coords()
        me = 2 * x + y
        peers = _other_chips(x, y)
        sends, fwds = [], []
        for a in range(n):
            rh = ins[a].shape[0] // 2
            mine = pl.ds(c * rh, rh)
            for j, (px, py, pk) in enumerate(peers):
                cp = pltpu.make_async_remote_copy(
                    src_ref=ins[a].at[mine], dst_ref=outs[a].at[me, mine], send_sem=ici_s.at[a, j], recv_sem=ici_r.at[a, j],
                    device_id=(px, py, c), device_id_type=MESH)
                cp.start()
                sends.append(cp)
        for a in range(n):
            rh = ins[a].shape[0] // 2
            mine = pl.ds(c * rh, rh)
            for j, (px, py, pk) in enumerate(peers):
                got = outs[a].at[pk, mine]
                pltpu.make_async_remote_copy(
                    src_ref=got, dst_ref=got, send_sem=ici_s.at[a, j], recv_sem=ici_r.at[a, j],
                    device_id=(px, py, c), device_id_type=MESH).wait_recv()
                fw = pltpu.make_async_remote_copy(
                    src_ref=got, dst_ref=got, send_sem=d2d_s.at[a, j], recv_sem=d2d_r.at[a, j],
                    device_id=(x, y, 1 - c), device_id_type=MESH)
                fw.start()
                fwds.append(fw)
        for a in range(n):
            rh = ins[a].shape[0] // 2
            theirs = pl.ds((1 - c) * rh, rh)
            for j, (px, py, pk) in enumerate(peers):
                got = outs[a].at[pk, theirs]
                pltpu.make_async_remote_copy(
                    src_ref=got, dst_ref=got, send_sem=d2d_s.at[a, j], recv_sem=d2d_r.at[a, j],
                    device_id=(x, y, 1 - c), device_id_type=MESH).wait_recv()
        for cp in sends + fwds:
            cp.wait_send()

    got = _call(
        body, name="all_gather_shards", in_specs=[ANY] * n, out_specs=[ANY] * n,
        out_shape=[_sds((N_CHIP,) + w.shape, w.dtype) for w in shards],
        scratch_shapes=[pltpu.SemaphoreType.DMA((n, 3))] * 4,
    )(*shards)
    me = 2 * lax.axis_index("x") + lax.axis_index("y")
    return [lax.dynamic_update_slice(g, w[None], (me, 0, 0)) for g, w in zip(got, shards)]


def pair_exchange(grads, tag):
    n = len(grads)

    def body(*refs):
        ins, outs = refs[:n], refs[n:2 * n]
        ss, rs = refs[2 * n:]
        x, y, c = _coords()
        cps = []
        for a in range(n):
            rh = ins[a].shape[1] // 2
            cp = pltpu.make_async_remote_copy(
                src_ref=ins[a].at[:, pl.ds((1 - c) * rh, rh)], dst_ref=outs[a], send_sem=ss.at[a], recv_sem=rs.at[a],
                device_id=(x, y, 1 - c), device_id_type=MESH)
            cp.start()
            cps.append(cp)
        for cp in cps:
            cp.wait()

    return _call(
        body, name="pair_exchange_" + tag, in_specs=[ANY] * n, out_specs=[ANY] * n,
        out_shape=[_sds((N_CHIP, g.shape[1] // 2, g.shape[2]), F32) for g in grads],
        scratch_shapes=[pltpu.SemaphoreType.DMA((n,))] * 2,
    )(*grads)


def _row_tile(rows, cols, itemsize=4, budget=2 * VMEM_MB):
    t = rows
    while t * cols * itemsize > budget and t % (2 * SUB) == 0:
        t //= 2
    return t


def pair_add(g, r, cidx):
    _, rows, cols = g.shape
    rh = rows // 2
    tr = _row_tile(rh, cols)
    per = rh // tr

    def body(c_ref, g_ref, r_ref, o_ref):
        o_ref[...] = g_ref[...] + r_ref[...]

    return _call(
        body, name="pair_add",
        grid_spec=pltpu.PrefetchScalarGridSpec(
            num_scalar_prefetch=1, grid=(N_CHIP, per),
            in_specs=[pl.BlockSpec((None, tr, cols), lambda k, i, c: (k, c[0] * per + i, 0)),
                      pl.BlockSpec((None, tr, cols), lambda k, i, c: (k, i, 0))],
            out_specs=pl.BlockSpec((None, tr, cols), lambda k, i, c: (k, i, 0))),
        out_shape=_sds((N_CHIP, rh, cols), F32), compiler_params=_cp(("arbitrary", "arbitrary")),
    )(cidx, g, r)


def chip_add(p, r, kc):
    _, rh, cols = p.shape
    tr = _row_tile(rh, cols)
    per = rh // tr

    def body(k_ref, p_ref, r_ref, o_ref):
        o_ref[...] = ((p_ref[...] + r_ref[0]) + r_ref[1]) + r_ref[2]

    return _call(
        body, name="chip_add",
        grid_spec=pltpu.PrefetchScalarGridSpec(
            num_scalar_prefetch=1, grid=(per,),
            in_specs=[pl.BlockSpec((None, tr, cols), lambda i, k: (k[0], i, 0)),
                      pl.BlockSpec((N_CHIP - 1, tr, cols), lambda i, k: (0, i, 0))],
            out_specs=pl.BlockSpec((tr, cols), lambda i, k: (k[1] * per + i, 0))),
        out_shape=_sds((2 * rh, cols), F32), compiler_params=_cp(("arbitrary",)),
    )(kc, p, r)


def pair_share(fulls, tag):
    n = len(fulls)

    def body(*refs):
        outs = refs[n:2 * n]
        ss, rs = refs[2 * n:]
        x, y, c = _coords()
        cps = []
        for a in range(n):
            rh = outs[a].shape[0] // 2
            mine = outs[a].at[pl.ds(c * rh, rh)]
            cp = pltpu.make_async_remote_copy(
                src_ref=mine, dst_ref=mine, send_sem=ss.at[a], recv_sem=rs.at[a],
                device_id=(x, y, 1 - c), device_id_type=MESH)
            cp.start()
            cps.append(cp)
        for a, cp in enumerate(cps):
            rh = outs[a].shape[0] // 2
            theirs = outs[a].at[pl.ds((1 - c) * rh, rh)]
            cp.wait_send()
            pltpu.make_async_remote_copy(
                src_ref=theirs, dst_ref=theirs, send_sem=ss.at[a], recv_sem=rs.at[a],
                device_id=(x, y, 1 - c), device_id_type=MESH).wait_recv()

    return _call(
        body, name="pair_share_" + tag, in_specs=[ANY] * n, out_specs=[ANY] * n,
        out_shape=[_sds(f.shape, F32) for f in fulls], input_output_aliases={a: a for a in range(n)},
        scratch_shapes=[pltpu.SemaphoreType.DMA((n,))] * 2,
    )(*fulls)


def small_allreduce_adamw(part, w, m, v):
    n = part.shape[1]

    def body(p_ref, w_ref, m_ref, v_ref, g_ref, d_ref, mo_ref, vo_ref, mine, gath, ss, rs):
        x, y, c = _coords()
        me = 4 * x + 2 * y + c
        mine[...] = jnp.sum(p_ref[...], axis=0, keepdims=True)
        gath[me] = mine[...]
        cps = []
        for k in range(1, 8):
            px, py, pc = x ^ (k >> 2), y ^ ((k >> 1) & 1), c ^ (k & 1)
            cp = pltpu.make_async_remote_copy(
                src_ref=mine, dst_ref=gath.at[me], send_sem=ss.at[k - 1], recv_sem=rs.at[k - 1],
                device_id=(px, py, pc), device_id_type=MESH)
            cp.start()
            cps.append(cp)
        for k in range(1, 8):
            src = 4 * (x ^ (k >> 2)) + 2 * (y ^ ((k >> 1) & 1)) + (c ^ (k & 1))
            pltpu.make_async_remote_copy(
                src_ref=mine, dst_ref=gath.at[src], send_sem=ss.at[k - 1], recv_sem=rs.at[k - 1],
                device_id=(x, y, c), device_id_type=MESH).wait_recv()
        for cp in cps:
            cp.wait_send()
        g = gath[0]
        for dv in range(1, 8):
            g = g + gath[dv]
        g_ref[...] = g
        d_ref[...], mo_ref[...], vo_ref[...] = _adamw_math(w_ref[...], g, m_ref[...], v_ref[...])

    vm = pl.BlockSpec(memory_space=pltpu.VMEM)
    return _call(
        body, name="small_allreduce_adamw", in_specs=[vm] * 4, out_specs=[vm] * 4, out_shape=[_sds((1, n), F32)] * 4,
        scratch_shapes=[pltpu.VMEM((1, n), F32), pltpu.VMEM((8, 1, n), F32),
                        pltpu.SemaphoreType.DMA((7,)), pltpu.SemaphoreType.DMA((7,))],
    )(part, w, m, v)


def _unshard_cols(g):
    k, r, cs = g.shape
    return g.transpose(1, 0, 2).reshape(r, k * cs)


def _shard_cols(w):
    r, c = w.shape
    return w.reshape(r, N_CHIP, c // N_CHIP).transpose(1, 0, 2)


def local_step(x, positions, ln_in_g, ln_in_b, win_g, g_cq, wuq_g, g_ckv, wuk_g, wuv_g, convw_g, conv_b, g_conv_ln,
               b_conv_ln, g_ln1, b_ln1, g_ln2, b_ln2, target, start_token, hooks):
    s, d = x.shape
    c = d - MLA_W
    row = lambda a: a.reshape(1, -1)
    ln_in_g = row(ln_in_g) + start_token[0:1, 0:1]

    win = _unshard_cols(win_g)
    o_kr = R_Q + R_KV
    o_cv = o_kr + D_ROPE
    win_r = jnp.concatenate([win[:, :o_kr], win[:, o_cv:], win[:, o_kr:o_cv], jnp.zeros((d, LANE - D_ROPE), BF)], axis=1)
    hc = win_r.shape[1]
    kr_blk = (o_kr + 2 * c) // LANE
    wuq = _unshard_cols(wuq_g).reshape(R_Q, HEADS, D_QK)
    wuq_r = jnp.concatenate([wuq[:, :, :D_NOPE].reshape(R_Q, MLA_W),
                             jnp.pad(wuq[:, :, D_NOPE:], ((0, 0), (0, 0), (0, LANE - D_ROPE))).reshape(R_Q, MLA_W)], axis=1)
    wuk = _unshard_cols(wuk_g)
    wuv = _unshard_cols(wuv_g)
    conv_w = jnp.pad(_unshard_cols(convw_g), ((0, 1), (0, 0)))

    half = D_ROPE // 2
    inv_freq = ROPE_BASE ** (-jnp.arange(half, dtype=F32) * (2.0 / D_ROPE))
    invf = jnp.concatenate([inv_freq, inv_freq, jnp.zeros((LANE - D_ROPE,), F32)]).reshape(1, LANE)
    cos, sin = rope_tables(positions.astype(F32).reshape(s, 1), invf)
    x0, x0b = ln_in_fwd(x, ln_in_g, row(ln_in_b))
    h = matmul("in_proj", x0b, win_r, 1024, 640)
    qc, cqn = q_proj(h, g_cq, wuq_r, cos, sin)
    kc, v, ckvn = kv_proj(h, g_ckv, wuk, wuv, cos, sin, kr_blk)
    o, ob, lse = attn_fwd(qc, kc, v)
    co, uc = conv_fwd(h, conv_w, conv_b, g_conv_ln, b_conv_ln)
    wout_g, wff1_g, wff2_g = hooks.late_weights(ob)
    wout = wout_g.reshape(d, d)
    wff2 = wff2_g.reshape(-1, d)
    wff1t = wff1_g.transpose(0, 2, 1).reshape(-1, d)
    wff2t = wff2.T
    r1, x1, x1b = out_proj_ln1(ob, co, wout, x0, g_ln1, b_ln1)
    rb, a1b = ff1_fwd(x1b, wff1_g)
    dr2, dr2b, loss8, dg2, db2 = ff2_ln2_loss(a1b, wff2, x1, target, g_ln2, b_ln2)

    df1b = ff2_bwd_act(dr2b, wff2t, rb)
    gw_ff2 = wgrad("wgrad_ff2", a1b, dr2b, 1024, 1024).reshape(N_CHIP, -1, d)
    gw_ff1 = wgrad("wgrad_ff1", x1b, df1b, 1024, 1024, shards=N_CHIP)
    tok = hooks.ff_grads(gw_ff2, gw_ff1)
    dr1, dr1b, dg1, db1 = ff1_bwd_ln1(df1b, wff1t, dr2, r1, g_ln1 + tok[0:1, 0:1])
    tok = hooks.ff_grads_mid(dr1b)
    gw_out = jnp.concatenate([wgrad("wgrad_out_attn", ob, dr1b, 1024, 1024)[0],
                              wgrad("wgrad_out_conv", co, dr1b, 1024, 1024)[0]], axis=0).reshape(N_CHIP, -1, d)
    dob, dot, dco, delta = out_proj_bwd(dr1b, wout.T, o)
    duc, dgc, dbc, dcb = conv_bwd_ln(uc, dco, g_conv_ln + tok[0:1, 0:1], b_conv_ln)
    dconv, gconvw = conv_bwd_taps(h, duc, conv_w)
    dqt, dk, dv = attn_bwd(qc, kc, kc.transpose(0, 2, 1), v, dob, dot, lse, delta)
    dqb, dcq, dgq = q_bwd(dqt, h, g_cq, wuq_r.T, cos, sin)
    dknb, dvb, dckv, dkr, dgkv = kv_bwd(dk, dv, h, g_ckv, wuk.T, wuv.T, cos, sin)
    gwuq_r = wgrad("wgrad_uq", cqn, dqb, 512, 1024)[0]
    gw_uk = wgrad("wgrad_uk", ckvn, dknb, 512, 1024, shards=N_CHIP)
    gw_uv = wgrad("wgrad_uv", ckvn, dvb, 512, 1024, shards=N_CHIP)
    dh = jnp.concatenate([dcq, dckv, dconv, dkr], axis=1)
    gwin_r = wgrad("wgrad_in", x0b, dh, 1024, 640)[0]

    gwin = jnp.concatenate([gwin_r[:, :o_kr], gwin_r[:, o_kr + 2 * c:o_kr + 2 * c + D_ROPE], gwin_r[:, o_kr:o_kr + 2 * c]], axis=1)
    gwuq = jnp.concatenate([gwuq_r[:, :MLA_W].reshape(R_Q, HEADS, D_NOPE),
                            gwuq_r[:, MLA_W:].reshape(R_Q, HEADS, LANE)[:, :, :D_ROPE]], axis=2).reshape(R_Q, HEADS * D_QK)
    tok = hooks.rest_grads(dict(w_in=_shard_cols(gwin), w_uq=_shard_cols(gwuq), w_uk=gw_uk, w_uv=gw_uv,
                                conv_w=_shard_cols(gconvw), w_out=gw_out))
    gx, dgin, dbin = in_proj_bwd_ln(dh, win_r.T, dr1, x, ln_in_g + tok[0:1, 0:1])
    small = jnp.concatenate([dgin, dbin, dgq, dgkv, dcb, dgc, dbc, dg1, db1, dg2, db2, loss8], axis=1)
    return gx, small


BIG = ["w_in", "w_uq", "w_uk", "w_uv", "conv_w", "w_out", "w_ff1", "w_ff2"]
EARLY = ["w_in", "w_uq", "w_uk", "w_uv", "conv_w"]
LATE = ["w_out", "w_ff1", "w_ff2"]
SMALL = ["ln_in_g", "ln_in_b", "g_cq", "g_ckv", "conv_b", "g_conv_ln", "b_conv_ln", "g_ln1", "b_ln1", "g_ln2", "b_ln2"]
WEIGHTS = ["ln_in_g", "ln_in_b", "w_in", "g_cq", "w_uq", "g_ckv", "w_uk", "w_uv", "conv_w", "conv_b", "g_conv_ln",
           "b_conv_ln", "w_out", "g_ln1", "b_ln1", "w_ff1", "w_ff2", "g_ln2", "b_ln2"]


def _pad_rows(a, rows):
    return jnp.pad(a, ((0, rows - a.shape[0]), (0, 0)))


def kernel(x, positions, ln_in_g, ln_in_b, w_in, g_cq, w_uq, g_ckv, w_uk, w_uv, conv_w, conv_b, g_conv_ln, b_conv_ln, w_out, g_ln1, b_ln1, w_ff1, w_ff2, g_ln2, b_ln2, loss_target, m_ln_in_g, m_ln_in_b, m_w_in, m_g_cq, m_w_uq, m_g_ckv, m_w_uk, m_w_uv, m_conv_w, m_conv_b, m_g_conv_ln, m_b_conv_ln, m_w_out, m_g_ln1, m_b_ln1, m_w_ff1, m_w_ff2, m_g_ln2, m_b_ln2, v_ln_in_g, v_ln_in_b, v_w_in, v_g_cq, v_w_uq, v_g_ckv, v_w_uk, v_w_uv, v_conv_w, v_conv_b, v_g_conv_ln, v_b_conv_ln, v_w_out, v_g_ln1, v_b_ln1, v_w_ff1, v_w_ff2, v_g_ln2, v_b_ln2):
    w = dict(ln_in_g=ln_in_g, ln_in_b=ln_in_b, w_in=w_in, g_cq=g_cq, w_uq=w_uq, g_ckv=g_ckv, w_uk=w_uk, w_uv=w_uv,
             conv_w=conv_w, conv_b=conv_b, g_conv_ln=g_conv_ln, b_conv_ln=b_conv_ln, w_out=w_out, g_ln1=g_ln1,
             b_ln1=b_ln1, w_ff1=w_ff1, w_ff2=w_ff2, g_ln2=g_ln2, b_ln2=b_ln2)
    m = dict(ln_in_g=m_ln_in_g, ln_in_b=m_ln_in_b, w_in=m_w_in, g_cq=m_g_cq, w_uq=m_w_uq, g_ckv=m_g_ckv, w_uk=m_w_uk,
             w_uv=m_w_uv, conv_w=m_conv_w, conv_b=m_conv_b, g_conv_ln=m_g_conv_ln, b_conv_ln=m_b_conv_ln, w_out=m_w_out,
             g_ln1=m_g_ln1, b_ln1=m_b_ln1, w_ff1=m_w_ff1, w_ff2=m_w_ff2, g_ln2=m_g_ln2, b_ln2=m_b_ln2)
    v = dict(ln_in_g=v_ln_in_g, ln_in_b=v_ln_in_b, w_in=v_w_in, g_cq=v_g_cq, w_uq=v_w_uq, g_ckv=v_g_ckv, w_uk=v_w_uk,
             w_uv=v_w_uv, conv_w=v_conv_w, conv_b=v_conv_b, g_conv_ln=v_g_conv_ln, b_conv_ln=v_b_conv_ln, w_out=v_w_out,
             g_ln1=v_g_ln1, b_ln1=v_b_ln1, w_ff1=v_w_ff1, w_ff2=v_w_ff2, g_ln2=v_g_ln2, b_ln2=v_b_ln2)

    sh2 = {n: w[n][0] for n in BIG}
    cidx = lax.axis_index("c").astype(jnp.int32).reshape(1)
    me = 2 * lax.axis_index("x") + lax.axis_index("y")
    kc = jnp.stack([me, lax.axis_index("c")]).astype(jnp.int32)

    early = [sh2[n].astype(BF) if n != "conv_w" else _pad_rows(sh2[n], CONV_K + 1) for n in EARLY]
    gw = dict(zip(EARLY, all_gather_shards(early)))
    gw["conv_w"] = gw["conv_w"][:, :CONV_K]
    late = [sh2[n].astype(BF) for n in LATE]
    ag = split_send_start("late_weights_start", "gather", late, [(N_CHIP,) + a.shape for a in late], gw["w_uq"])
    rest = [n for n in BIG if n not in ("w_ff2", "w_ff1")]
    flight = {}

    class Hooks:
        @staticmethod
        def late_weights(after):
            mine, lands = split_send_wait("late_weights_wait", "gather", *ag[:4], after)
            return [lax.dynamic_update_slice(g, a[None], (me, 0, 0)) for g, a in zip(lands, mine)]

        @staticmethod
        def ff_grads(gw_ff2, gw_ff1):
            full = [gw_ff2, gw_ff1]
            st = split_send_start("ff_pair_start", "pair", full, [(N_CHIP, g.shape[1] // 2, g.shape[2]) for g in full], full[0])
            flight["ff_pair"] = st[:4]
            return st[4]

        @staticmethod
        def ff_grads_mid(after):
            full, recv = split_send_wait("ff_pair_wait", "pair", *flight["ff_pair"], after)
            psum = [pair_add(g, r, cidx) for g, r in zip(full, recv)]
            st = split_send_start("ff_grads_start", "scatter", psum, [(N_CHIP - 1,) + p.shape[1:] for p in psum], psum[0])
            flight["ff"] = st[:4]
            return st[4]

        @staticmethod
        def rest_grads(big):
            full = [big[n] for n in rest]
            psum = [pair_add(g, r, cidx) for g, r in zip(full, pair_exchange(full, "rest"))]
            st = split_send_start("rest_grads_start", "scatter", psum, [(N_CHIP - 1,) + p.shape[1:] for p in psum], psum[0])
            flight["rest"] = st[:4]
            return st[4]

    gx, small = local_step(
        x[0], positions[0], ln_in_g, ln_in_b, gw["w_in"], g_cq, gw["w_uq"], g_ckv, gw["w_uk"], gw["w_uv"], gw["conv_w"],
        conv_b, g_conv_ln, b_conv_ln, g_ln1, b_ln1, g_ln2, b_ln2, loss_target[0], ag[4], Hooks)

    ff_psum, ff_got = split_send_wait("ff_grads_wait", "scatter", *flight["ff"], gx)
    rest_psum, rest_got = split_send_wait("rest_grads_wait", "scatter", *flight["rest"], gx)
    summed = [chip_add(p, r, kc) for p, r in zip(rest_psum + ff_psum, rest_got + ff_got)]
    gsh = dict(zip(rest + ["w_ff2", "w_ff1"], pair_share(summed, "all")))
    gsh["conv_w"] = gsh["conv_w"][:CONV_K]

    grad, delta, new_m, new_v = {}, {}, {}, {}
    for n in BIG:
        grad[n] = gsh[n][None]
        d_, m_, v_ = adamw("adamw_" + n, sh2[n], gsh[n], m[n][0], v[n][0])
        delta[n], new_m[n], new_v[n] = d_[None], m_[None], v_[None]

    flat = lambda t: jnp.concatenate([t[n].reshape(1, -1) for n in SMALL] + [jnp.zeros((1, LANE), F32)], axis=1)
    g_s, d_s, m_s, v_s = small_allreduce_adamw(small, flat(w), flat(m), flat(v))
    off = 0
    for n in SMALL:
        sz = w[n].size
        for dst, src in ((grad, g_s), (delta, d_s), (new_m, m_s), (new_v, v_s)):
            dst[n] = src[0, off:off + sz].reshape(w[n].shape)
        off += sz
    loss = jnp.sum(g_s[0, off:off + LANE])

    return (loss, gx[None], *[grad[n] for n in WEIGHTS], *[delta[n] for n in WEIGHTS],
            *[new_m[n] for n in WEIGHTS], *[new_v[n] for n in WEIGHTS])
```

```python
import functools

import jax
import jax.numpy as jnp
from jax import lax
from jax.experimental import pallas as pl
from jax.experimental.pallas import tpu as pltpu

F32 = jnp.float32
BF = jnp.bfloat16

HEADS = 8
D_NOPE = 128
D_ROPE = 64
D_V = 128
D_QK = D_NOPE + D_ROPE
R_Q = 512
R_KV = 512
MLA_W = HEADS * D_V
CONV_K = 31
CONV_PAD = CONV_K // 2
ROPE_BASE = 10000.0
LOG2E = 1.4426950408889634
LN2 = 0.6931471805599453
LN_EPS = 1e-5
RMS_EPS = 1e-6
ALPHA = (2.0 * 1) ** 0.25
ADAM_LR = 0.001
ADAM_B1 = 0.9
ADAM_B2 = 0.999
ADAM_EPS = 1e-08
ADAM_WD = 0.01
ADAM_STEP = 10

LANE = 128
SUB = 8
HALO = 16
N_CHIP = 4
MESH = pl.DeviceIdType.MESH
VMEM_MB = 1024 * 1024


def _call(body, **kw):
    return pl.pallas_call(body, **kw)


def _cp(sem, mb=48):
    return pltpu.CompilerParams(dimension_semantics=sem, vmem_limit_bytes=mb * VMEM_MB)


def _sds(shape, dt):
    return jax.ShapeDtypeStruct(shape, dt)


def _dot(a, b):
    return jnp.dot(a, b, preferred_element_type=F32)


def _dot_nt(a, b):
    return lax.dot_general(a, b, (((1,), (1,)), ((), ())), preferred_element_type=F32)


def _dot_tn(a, b):
    return lax.dot_general(a, b, (((0,), (0,)), ((), ())), preferred_element_type=F32)


def _rows8(v):
    t, n = v.shape
    return v.reshape(t // SUB, SUB, n).sum(axis=0)


def _ln_stats(r):
    mu = jnp.mean(r, axis=-1, keepdims=True)
    xc = r - mu
    var = jnp.mean(xc * xc, axis=-1, keepdims=True)
    rstd = lax.rsqrt(var + LN_EPS)
    return xc * rstd, rstd


def _ln_bwd(dy, xhat, rstd, g):
    dyh = dy * g
    m1 = jnp.mean(dyh, axis=-1, keepdims=True)
    m2 = jnp.mean(dyh * xhat, axis=-1, keepdims=True)
    return rstd * (dyh - m1 - xhat * m2)


def _rms_fwd(x, g):
    rr = lax.rsqrt(jnp.mean(x * x, axis=-1, keepdims=True) + RMS_EPS)
    xh = x * rr
    return xh * g, xh, rr


def _rms_bwd(dy, xh, rr, g):
    dyg = dy * g
    return rr * (dyg - xh * jnp.mean(dyg * xh, axis=-1, keepdims=True))


def _rope128(x, cos, sin_signed):
    lane = lax.broadcasted_iota(jnp.int32, x.shape, 1)
    rot = jnp.where(lane < D_ROPE // 2, pltpu.roll(x, LANE - D_ROPE // 2, 1), pltpu.roll(x, D_ROPE // 2, 1))
    return x * cos + rot * sin_signed


def _unrope128(dy, cos, sin_signed):
    t = dy * sin_signed
    lane = lax.broadcasted_iota(jnp.int32, dy.shape, 1)
    rot = jnp.where(lane < D_ROPE // 2, pltpu.roll(t, LANE - D_ROPE // 2, 1), pltpu.roll(t, D_ROPE // 2, 1))
    return dy * cos + rot


def _as_row(col):
    return jnp.transpose(jnp.broadcast_to(col, (col.shape[0], LANE)))[0:1, :]


def _sigmoid(x):
    return 1.0 / (1.0 + jnp.exp(-x))


def _row_chunks(tm, fn, rc=128):
    rc = min(rc, tm)

    def step(ci, carry):
        fn(pl.ds(pl.multiple_of(ci * rc, rc), rc))
        return carry

    lax.fori_loop(0, tm // rc, step, 0)


def _unrolled_loop(n, unroll, fn, init):
    unroll = min(n, unroll)
    assert n % unroll == 0

    def body(t, carry):
        for u in range(unroll):
            carry = fn(t * unroll + u, carry)
        return carry

    return lax.fori_loop(0, n // unroll, body, init)


def _tile(s, want):
    t = min(s, want)
    assert s % t == 0
    return t


def rope_tables(pos_f, invf):
    s = pos_f.shape[0]
    tm = _tile(s, 1024)

    def body(p_ref, f_ref, c_ref, s_ref):
        ang = p_ref[...] * f_ref[...]
        lane = lax.broadcasted_iota(jnp.int32, ang.shape, 1)
        c = jnp.cos(ang)
        sn = jnp.sin(ang)
        c_ref[...] = jnp.where(lane < D_ROPE, c, 0.0)
        s_ref[...] = jnp.where(lane < D_ROPE // 2, -sn, jnp.where(lane < D_ROPE, sn, 0.0))

    return _call(
        body, name="rope_tables", grid=(s // tm,),
        in_specs=[pl.BlockSpec((tm, 1), lambda i: (i, 0)), pl.BlockSpec((1, LANE), lambda i: (0, 0))],
        out_specs=[pl.BlockSpec((tm, LANE), lambda i: (i, 0))] * 2,
        out_shape=[_sds((s, LANE), F32)] * 2,
        compiler_params=_cp(("arbitrary",)),
    )(pos_f, invf)


def ln_in_fwd(x, g, b):
    s, d = x.shape
    tm = _tile(s, 512)

    def body(x_ref, g_ref, b_ref, o_ref, ob_ref):
        xhat, _ = _ln_stats(x_ref[...])
        y = xhat * g_ref[...] + b_ref[...]
        o_ref[...] = y
        ob_ref[...] = y.astype(BF)

    row = pl.BlockSpec((1, d), lambda i: (0, 0))
    tok = pl.BlockSpec((tm, d), lambda i: (i, 0))
    return _call(
        body, name="ln_in_fwd", grid=(s // tm,), in_specs=[tok, row, row], out_specs=[tok, tok],
        out_shape=[_sds((s, d), F32), _sds((s, d), BF)], compiler_params=_cp(("arbitrary",)),
    )(x, g, b)


def matmul(name, a, w, tm, tn, out_dtype=F32):
    s, k = a.shape
    n = w.shape[1]
    tm = _tile(s, tm)
    tn = _tile(n, tn)

    def body(a_ref, w_ref, o_ref):
        o_ref[...] = _dot(a_ref[...], w_ref[...]).astype(o_ref.dtype)

    return _call(
        body, name=name, grid=(s // tm, n // tn),
        in_specs=[pl.BlockSpec((tm, k), lambda i, j: (i, 0)), pl.BlockSpec((k, tn), lambda i, j: (0, j))],
        out_specs=pl.BlockSpec((tm, tn), lambda i, j: (i, j)),
        out_shape=_sds((s, n), out_dtype), compiler_params=_cp(("arbitrary", "arbitrary")),
    )(a, w)


def q_proj(h, g_cq, wuq, cos, sin):
    s = h.shape[0]
    tm = _tile(s, 512)

    def body(h_ref, g_ref, w_ref, c_ref, s_ref, q_ref, n_ref):
        y, _, _ = _rms_fwd(h_ref[...], g_ref[...])
        yb = y.astype(BF)
        n_ref[...] = yb
        q = _dot(yb, w_ref[...])
        c = c_ref[...]
        sn = s_ref[...]
        for hd in range(HEADS):
            q_ref[hd, :, 0:LANE] = q[:, LANE * hd:LANE * (hd + 1)].astype(BF)
            qr = q[:, MLA_W + LANE * hd:MLA_W + LANE * (hd + 1)]
            q_ref[hd, :, LANE:2 * LANE] = _rope128(qr, c, sn).astype(BF)

    return _call(
        body, name="q_proj", grid=(s // tm,),
        in_specs=[pl.BlockSpec((tm, R_Q), lambda i: (i, 0)), pl.BlockSpec((1, R_Q), lambda i: (0, 0)),
                  pl.BlockSpec((R_Q, 2 * MLA_W), lambda i: (0, 0)),
                  pl.BlockSpec((tm, LANE), lambda i: (i, 0)), pl.BlockSpec((tm, LANE), lambda i: (i, 0))],
        out_specs=[pl.BlockSpec((HEADS, tm, 2 * LANE), lambda i: (0, i, 0)), pl.BlockSpec((tm, R_Q), lambda i: (i, 0))],
        out_shape=[_sds((HEADS, s, 2 * LANE), BF), _sds((s, R_Q), BF)], compiler_params=_cp(("arbitrary",)),
    )(h, g_cq, wuq, cos, sin)


def kv_proj(h, g_ckv, wuk, wuv, cos, sin, kr_blk):
    s = h.shape[0]
    tm = _tile(s, 512)

    def body(h_ref, kr_ref, g_ref, wk_ref, wv_ref, c_ref, s_ref, k_ref, kt_ref, v_ref, n_ref):
        y, _, _ = _rms_fwd(h_ref[...], g_ref[...])
        yb = y.astype(BF)
        n_ref[...] = yb
        kn = _dot(yb, wk_ref[...])
        v = _dot(yb, wv_ref[...])
        kr = _rope128(kr_ref[...], c_ref[...], s_ref[...])
        krb = kr.astype(BF)
        krt = kr.T.astype(BF)
        for hd in range(HEADS):
            knh = kn[:, LANE * hd:LANE * (hd + 1)]
            k_ref[hd, :, 0:LANE] = knh.astype(BF)
            k_ref[hd, :, LANE:2 * LANE] = krb
            kt_ref[hd, 0:LANE, :] = knh.T.astype(BF)
            kt_ref[hd, LANE:2 * LANE, :] = krt
            v_ref[hd] = v[:, LANE * hd:LANE * (hd + 1)].astype(BF)

    tab = pl.BlockSpec((tm, LANE), lambda i: (i, 0))
    wsp = pl.BlockSpec((R_KV, MLA_W), lambda i: (0, 0))
    return _call(
        body, name="kv_proj", grid=(s // tm,),
        in_specs=[pl.BlockSpec((tm, R_KV), lambda i: (i, 1)), pl.BlockSpec((tm, LANE), lambda i: (i, kr_blk)),
                  pl.BlockSpec((1, R_KV), lambda i: (0, 0)), wsp, wsp, tab, tab],
        out_specs=[pl.BlockSpec((HEADS, tm, 2 * LANE), lambda i: (0, i, 0)), pl.BlockSpec((HEADS, 2 * LANE, tm), lambda i: (0, 0, i)),
                   pl.BlockSpec((HEADS, tm, LANE), lambda i: (0, i, 0)), pl.BlockSpec((tm, R_KV), lambda i: (i, 0))],
        out_shape=[_sds((HEADS, s, 2 * LANE), BF), _sds((HEADS, 2 * LANE, s), BF), _sds((HEADS, s, LANE), BF), _sds((s, R_KV), BF)],
        compiler_params=_cp(("arbitrary",)),
    )(h, h, g_ckv, wuk, wuv, cos, sin)


def attn_fwd(qc, kc, v):
    _, s, _ = qc.shape
    tq = _tile(s, 256)
    tk = _tile(s, 512)
    scale = D_QK ** -0.5
    c2 = scale * LOG2E
    nk = s // tk
    nb = tk // LANE
    un = 8

    def body(q_ref, k_ref, v_ref, o_ref, ob_ref, l_ref, s_scr, m_scr):
        q = q_ref[...]

        def scores(j, mpart):
            off = pl.multiple_of(j * tk, tk)
            sc = _dot_nt(q, k_ref[pl.ds(off, tk), :]) * c2
            s_scr[:, pl.ds(off, tk)] = sc
            for b in range(nb):
                mpart = jnp.maximum(mpart, sc[:, LANE * b:LANE * (b + 1)])
            return mpart

        mpart = _unrolled_loop(nk, un, scores, jnp.full((tq, LANE), -jnp.inf, F32))
        m = jnp.max(mpart, axis=-1, keepdims=True)
        m_scr[...] = jnp.broadcast_to(m, (tq, LANE))

        def weigh(j, carry):
            lpart, acc = carry
            off = pl.multiple_of(j * tk, tk)
            ps = []
            for b in range(nb):
                p = jnp.exp2(s_scr[:, pl.ds(off + LANE * b, LANE)] - m_scr[...])
                lpart = lpart + p
                ps.append(p.astype(BF))
            acc = acc + _dot(jnp.concatenate(ps, axis=1), v_ref[pl.ds(off, tk), :])
            return lpart, acc

        lpart, acc = _unrolled_loop(nk, un, weigh, (jnp.zeros((tq, LANE), F32), jnp.zeros((tq, D_V), F32)))
        l = jnp.sum(lpart, axis=-1, keepdims=True)
        o = acc / l
        o_ref[...] = o
        ob_ref[...] = o.astype(BF)
        l_ref[...] = _as_row(m + jnp.log(l) * LOG2E)

    return _call(
        body, name="attn_fwd", grid=(HEADS, s // tq),
        in_specs=[pl.BlockSpec((None, tq, 2 * LANE), lambda h, i: (h, i, 0)),
                  pl.BlockSpec((None, s, 2 * LANE), lambda h, i: (h, 0, 0)),
                  pl.BlockSpec((None, s, LANE), lambda h, i: (h, 0, 0))],
        out_specs=[pl.BlockSpec((tq, LANE), lambda h, i: (i, h)), pl.BlockSpec((tq, LANE), lambda h, i: (i, h)),
                   pl.BlockSpec((None, 1, tq), lambda h, i: (h, 0, i))],
        out_shape=[_sds((s, MLA_W), F32), _sds((s, MLA_W), BF), _sds((HEADS, 1, s), F32)],
        scratch_shapes=[pltpu.VMEM((tq, s + LANE), F32), pltpu.VMEM((tq, LANE), F32)],
        compiler_params=_cp(("arbitrary", "arbitrary")),
    )(qc, kc, v)


def _halo_specs(tm, s, width, col):
    r = tm // HALO
    nb = s // HALO
    cur = pl.BlockSpec((tm, width), lambda i: (i, col))
    prev = pl.BlockSpec((HALO, width), lambda i: (jnp.maximum(i * r - 1, 0), col))
    nxt = pl.BlockSpec((HALO, width), lambda i: (jnp.minimum((i + 1) * r, nb - 1), col))
    return cur, prev, nxt


def _slab_shapes(tm, c):
    return (tm + 2 * HALO, c + LANE), (SUB - 1, tm + 2 * HALO - SUB, c + LANE)


def _fill_slab(slab, tm, prev, cur, nxt):
    i = pl.program_id(0)
    last = pl.num_programs(0) - 1
    c = cur.shape[1]
    slab[0:HALO, 0:c] = jnp.where(i > 0, prev, 0.0)
    slab[HALO:HALO + tm, 0:c] = cur
    slab[HALO + tm:2 * HALO + tm, 0:c] = jnp.where(i < last, nxt, 0.0)


def _rotate_slab(slab, rot, tm):
    rows = tm + 2 * HALO - SUB
    c = slab.shape[1] - LANE
    for b in range(1, SUB):
        rot[b - 1, :, 0:c] = slab[pl.ds(b, rows), 0:c]


def _shifted(slab, rot, start, rc, cs):
    b = start % SUB
    if b == 0:
        return slab[pl.ds(start, rc), cs]
    return rot[b - 1, pl.ds(start - b, rc), cs]


def conv_fwd(h, conv_w, conv_b, g_ln, b_ln):
    s = h.shape[0]
    c = conv_w.shape[1]
    tm = _tile(s, 256)
    rc = _tile(tm, 64)

    def body(a_ref, ap_ref, an_ref, g_ref, gp_ref, gn_ref, w_ref, cb_ref, lg_ref, lb_ref, co_ref, uc_ref, slab, rot):
        _fill_slab(slab, tm, ap_ref[...] * _sigmoid(gp_ref[...]), a_ref[...] * _sigmoid(g_ref[...]),
                   an_ref[...] * _sigmoid(gn_ref[...]))
        _rotate_slab(slab, rot, tm)

        def lane_block(cb, carry):
            cs = pl.ds(pl.multiple_of(cb * LANE, LANE), LANE)
            for r0 in range(0, tm, rc):
                acc = jnp.zeros((rc, LANE), F32)
                for k in range(CONV_K):
                    acc = acc + w_ref[k:k + 1, cs] * _shifted(slab, rot, r0 + HALO - CONV_PAD + k, rc, cs)
                uc_ref[r0:r0 + rc, cs] = acc + cb_ref[:, cs]
            return carry

        lax.fori_loop(0, c // LANE, lane_block, 0)
        xhat, _ = _ln_stats(uc_ref[...])
        cl = xhat * lg_ref[...] + lb_ref[...]
        co_ref[...] = (cl * _sigmoid(cl)).astype(BF)

    a_specs = _halo_specs(tm, s, c, 1)
    g_specs = _halo_specs(tm, s, c, 2)
    row = pl.BlockSpec((1, c), lambda i: (0, 0))
    tok = pl.BlockSpec((tm, c), lambda i: (i, 0))
    return _call(
        body, name="conv_fwd", grid=(s // tm,),
        in_specs=[*a_specs, *g_specs, pl.BlockSpec(conv_w.shape, lambda i: (0, 0)), row, row, row],
        out_specs=[tok, tok], out_shape=[_sds((s, c), BF), _sds((s, c), F32)],
        scratch_shapes=[pltpu.VMEM(shp, F32) for shp in _slab_shapes(tm, c)],
        compiler_params=_cp(("arbitrary",)),
    )(h, h, h, h, h, h, conv_w, conv_b, g_ln, b_ln)


def out_proj_ln1(ob, co, wout, x0, g1, b1):
    s, d = x0.shape
    kh = ob.shape[1]
    tm = _tile(s, 256)

    def body(o_ref, c_ref, w_ref, x_ref, g_ref, b_ref, r_ref, x1_ref, x1b_ref, acc):
        k = pl.program_id(1)

        @pl.when(k == 0)
        def _():
            acc[...] = _dot(o_ref[...], w_ref[...])

        @pl.when(k == 1)
        def _():
            r = ALPHA * x_ref[...] + (acc[...] + _dot(c_ref[...], w_ref[...]))
            r_ref[...] = r
            xhat, _ = _ln_stats(r)
            y = xhat * g_ref[...] + b_ref[...]
            x1_ref[...] = y
            x1b_ref[...] = y.astype(BF)

    half = pl.BlockSpec((tm, kh), lambda i, k: (i, 0))
    tok = pl.BlockSpec((tm, d), lambda i, k: (i, 0))
    row = pl.BlockSpec((1, d), lambda i, k: (0, 0))
    return _call(
        body, name="out_proj_ln1", grid=(s // tm, 2),
        in_specs=[half, half, pl.BlockSpec((kh, d), lambda i, k: (k, 0)), tok, row, row],
        out_specs=[tok, tok, tok], out_shape=[_sds((s, d), F32), _sds((s, d), F32), _sds((s, d), BF)],
        scratch_shapes=[pltpu.VMEM((tm, d), F32)], compiler_params=_cp(("arbitrary", "arbitrary")),
    )(ob, co, wout, x0, g1, b1)


def ff1_fwd(x1b, wff1_g):
    s, d = x1b.shape
    nsh, _, fs = wff1_g.shape
    tm = _tile(s, 1024)
    tn = _tile(fs, 1024)
    per = fs // tn

    def body(a_ref, w_ref, r_ref, a1_ref):
        r = jnp.maximum(_dot(a_ref[...], w_ref[...]), 0.0)
        r_ref[...] = r.astype(BF)
        a1_ref[...] = (r * r).astype(BF)

    out = pl.BlockSpec((tm, tn), lambda i, j: (i, j))
    return _call(
        body, name="ff1_fwd", grid=(s // tm, nsh * per),
        in_specs=[pl.BlockSpec((tm, d), lambda i, j: (i, 0)),
                  pl.BlockSpec((None, d, tn), lambda i, j: (j // per, 0, j % per))],
        out_specs=[out, out], out_shape=[_sds((s, nsh * fs), BF)] * 2,
        compiler_params=_cp(("arbitrary", "arbitrary")),
    )(x1b, wff1_g)


def ff2_ln2_loss(a1b, wff2, x1, target, g2, b2):
    s, f = a1b.shape
    d = x1.shape[1]
    tm = _tile(s, 512)
    tk = _tile(f, 1024)
    nk = f // tk

    def body(a_ref, w_ref, x_ref, t_ref, g_ref, b_ref, dr_ref, drb_ref, loss_ref, dg_ref, db_ref, acc):
        i = pl.program_id(0)
        k = pl.program_id(1)

        @pl.when(k == 0)
        def _():
            acc[...] = _dot(a_ref[...], w_ref[...])

        @pl.when(k > 0)
        def _():
            acc[...] += _dot(a_ref[...], w_ref[...])

        @pl.when(jnp.logical_and(i == 0, k == 0))
        def _():
            loss_ref[...] = jnp.zeros_like(loss_ref)
            dg_ref[...] = jnp.zeros_like(dg_ref)
            db_ref[...] = jnp.zeros_like(db_ref)

        @pl.when(k == nk - 1)
        def _():
            g = g_ref[...]

            def chunk(rows):
                r = ALPHA * x_ref[rows, :] + acc[rows, :]
                xhat, rstd = _ln_stats(r)
                e = xhat * g + b_ref[...] - t_ref[rows, :]
                e2 = _rows8(e * e)
                part = e2[:, 0:LANE]
                for c in range(1, d // LANE):
                    part = part + e2[:, LANE * c:LANE * (c + 1)]
                loss_ref[...] += part * (0.5 / d)
                dy = e * (1.0 / d)
                dg_ref[...] += _rows8(dy * xhat)
                db_ref[...] += _rows8(dy)
                dr = _ln_bwd(dy, xhat, rstd, g)
                dr_ref[rows, :] = dr
                drb_ref[rows, :] = dr.astype(BF)

            _row_chunks(tm, chunk)

    tok = pl.BlockSpec((tm, d), lambda i, k: (i, 0))
    row = pl.BlockSpec((1, d), lambda i, k: (0, 0))
    accs = pl.BlockSpec((SUB, d), lambda i, k: (0, 0))
    return _call(
        body, name="ff2_ln2_loss", grid=(s // tm, nk),
        in_specs=[pl.BlockSpec((tm, tk), lambda i, k: (i, k)), pl.BlockSpec((tk, d), lambda i, k: (k, 0)),
                  tok, tok, row, row],
        out_specs=[tok, tok, pl.BlockSpec((SUB, LANE), lambda i, k: (0, 0)), accs, accs],
        out_shape=[_sds((s, d), F32), _sds((s, d), BF), _sds((SUB, LANE), F32), _sds((SUB, d), F32), _sds((SUB, d), F32)],
        scratch_shapes=[pltpu.VMEM((tm, d), F32)], compiler_params=_cp(("arbitrary", "arbitrary"), 56),
    )(a1b, wff2, x1, target, g2, b2)


def ff2_bwd_act(dr2b, wff2t, rb):
    s, d = dr2b.shape
    f = wff2t.shape[1]
    tm = _tile(s, 1024)
    tn = _tile(f, 1024)

    def body(a_ref, w_ref, r_ref, o_ref):
        o_ref[...] = (_dot(a_ref[...], w_ref[...]) * (2.0 * r_ref[...].astype(F32))).astype(BF)

    return _call(
        body, name="ff2_bwd_act", grid=(s // tm, f // tn),
        in_specs=[pl.BlockSpec((tm, d), lambda i, j: (i, 0)), pl.BlockSpec((d, tn), lambda i, j: (0, j)),
                  pl.BlockSpec((tm, tn), lambda i, j: (i, j))],
        out_specs=pl.BlockSpec((tm, tn), lambda i, j: (i, j)), out_shape=_sds((s, f), BF),
        compiler_params=_cp(("arbitrary", "arbitrary")),
    )(dr2b, wff2t, rb)


def wgrad(name, a, b, tm, tn, tk=2048, shards=1):
    s, m = a.shape
    n = b.shape[1]
    tm = _tile(m, tm)
    ns = n // shards
    tn = _tile(ns, tn)
    tk = _tile(s, tk)
    per = ns // tn

    def body(a_ref, b_ref, o_ref):
        k = pl.program_id(2)

        @pl.when(k == 0)
        def _():
            o_ref[...] = _dot_tn(a_ref[...], b_ref[...])

        @pl.when(k > 0)
        def _():
            o_ref[...] += _dot_tn(a_ref[...], b_ref[...])

    return _call(
        body, name=name, grid=(m // tm, n // tn, s // tk),
        in_specs=[pl.BlockSpec((tk, tm), lambda i, j, k: (k, i)), pl.BlockSpec((tk, tn), lambda i, j, k: (k, j))],
        out_specs=pl.BlockSpec((None, tm, tn), lambda i, j, k: (j // per, i, j % per)),
        out_shape=_sds((shards, m, ns), F32), compiler_params=_cp(("arbitrary", "arbitrary", "arbitrary")),
    )(a, b)


def ff1_bwd_ln1(df1b, wff1t, dr2, r1, g1):
    s, f = df1b.shape
    d = dr2.shape[1]
    tm = _tile(s, 512)
    tk = _tile(f, 1024)
    nk = f // tk

    def body(a_ref, w_ref, d2_ref, r_ref, g_ref, dr_ref, drb_ref, dg_ref, db_ref, acc):
        i = pl.program_id(0)
        k = pl.program_id(1)

        @pl.when(k == 0)
        def _():
            acc[...] = _dot(a_ref[...], w_ref[...])

        @pl.when(k > 0)
        def _():
            acc[...] += _dot(a_ref[...], w_ref[...])

        @pl.when(jnp.logical_and(i == 0, k == 0))
        def _():
            dg_ref[...] = jnp.zeros_like(dg_ref)
            db_ref[...] = jnp.zeros_like(db_ref)

        @pl.when(k == nk - 1)
        def _():
            g = g_ref[...]

            def chunk(rows):
                dy = ALPHA * d2_ref[rows, :] + acc[rows, :]
                xhat, rstd = _ln_stats(r_ref[rows, :])
                dg_ref[...] += _rows8(dy * xhat)
                db_ref[...] += _rows8(dy)
                dr = _ln_bwd(dy, xhat, rstd, g)
                dr_ref[rows, :] = dr
                drb_ref[rows, :] = dr.astype(BF)

            _row_chunks(tm, chunk)

    tok = pl.BlockSpec((tm, d), lambda i, k: (i, 0))
    accs = pl.BlockSpec((SUB, d), lambda i, k: (0, 0))
    return _call(
        body, name="ff1_bwd_ln1", grid=(s // tm, nk),
        in_specs=[pl.BlockSpec((tm, tk), lambda i, k: (i, k)), pl.BlockSpec((tk, d), lambda i, k: (k, 0)),
                  tok, tok, pl.BlockSpec((1, d), lambda i, k: (0, 0))],
        out_specs=[tok, tok, accs, accs],
        out_shape=[_sds((s, d), F32), _sds((s, d), BF), _sds((SUB, d), F32), _sds((SUB, d), F32)],
        scratch_shapes=[pltpu.VMEM((tm, d), F32)], compiler_params=_cp(("arbitrary", "arbitrary"), 56),
    )(df1b, wff1t, dr2, r1, g1)


def out_proj_bwd(dr1b, woutt, o):
    s, d = dr1b.shape
    tm = _tile(s, 256)

    def body(a_ref, w_ref, o_ref, do_ref, dot_ref, dc_ref, dl_ref):
        dcat = _dot(a_ref[...], w_ref[...])
        do = dcat[:, 0:MLA_W]
        do_ref[...] = do.astype(BF)
        dc_ref[...] = dcat[:, MLA_W:]
        prod = do * o_ref[...]
        for hd in range(HEADS):
            hs = slice(LANE * hd, LANE * (hd + 1))
            dl_ref[hd] = _as_row(jnp.sum(prod[:, hs], axis=-1, keepdims=True))
            dot_ref[hd] = do[:, hs].T.astype(BF)

    half = pl.BlockSpec((tm, MLA_W), lambda i: (i, 0))
    return _call(
        body, name="out_proj_bwd", grid=(s // tm,),
        in_specs=[pl.BlockSpec((tm, d), lambda i: (i, 0)), pl.BlockSpec((d, d), lambda i: (0, 0)), half],
        out_specs=[half, pl.BlockSpec((HEADS, LANE, tm), lambda i: (0, 0, i)),
                   pl.BlockSpec((tm, d - MLA_W), lambda i: (i, 0)), pl.BlockSpec((HEADS, 1, tm), lambda i: (0, 0, i))],
        out_shape=[_sds((s, MLA_W), BF), _sds((HEADS, LANE, s), BF), _sds((s, d - MLA_W), F32), _sds((HEADS, 1, s), F32)],
        compiler_params=_cp(("arbitrary",)),
    )(dr1b, woutt, o)


def conv_bwd_ln(uc, dco, g_ln, b_ln):
    s, c = uc.shape
    tm = _tile(s, 512)

    def body(u_ref, d_ref, g_ref, b_ref, du_ref, dg_ref, db_ref, dcb_ref):
        @pl.when(pl.program_id(0) == 0)
        def _():
            dg_ref[...] = jnp.zeros_like(dg_ref)
            db_ref[...] = jnp.zeros_like(db_ref)
            dcb_ref[...] = jnp.zeros_like(dcb_ref)

        xhat, rstd = _ln_stats(u_ref[...])
        g = g_ref[...]
        cl = xhat * g + b_ref[...]
        sg = _sigmoid(cl)
        dcl = d_ref[...] * (sg * (1.0 + cl * (1.0 - sg)))
        dg_ref[...] += _rows8(dcl * xhat)
        db_ref[...] += _rows8(dcl)
        du = _ln_bwd(dcl, xhat, rstd, g)
        du_ref[...] = du
        dcb_ref[...] += _rows8(du)

    tok = pl.BlockSpec((tm, c), lambda i: (i, 0))
    row = pl.BlockSpec((1, c), lambda i: (0, 0))
    accs = pl.BlockSpec((SUB, c), lambda i: (0, 0))
    return _call(
        body, name="conv_bwd_ln", grid=(s // tm,), in_specs=[tok, tok, row, row], out_specs=[tok, accs, accs, accs],
        out_shape=[_sds((s, c), F32)] + [_sds((SUB, c), F32)] * 3, compiler_params=_cp(("arbitrary",)),
    )(uc, dco, g_ln, b_ln)


def conv_bwd_taps(h, duc, conv_w):
    s, c = duc.shape
    tm = _tile(s, 256)
    rc = _tile(tm, 64)

    def body(a_ref, ap_ref, an_ref, g_ref, gp_ref, gn_ref, d_ref, dp_ref, dn_ref, w_ref, o_ref, dw_ref,
             uslab, dslab, du_s, urot, drot, dw8):
        @pl.when(pl.program_id(0) == 0)
        def _():
            dw8[...] = jnp.zeros_like(dw8)

        sg = _sigmoid(g_ref[...])
        a = a_ref[...]
        _fill_slab(uslab, tm, ap_ref[...] * _sigmoid(gp_ref[...]), a * sg, an_ref[...] * _sigmoid(gn_ref[...]))
        _fill_slab(dslab, tm, dp_ref[...], d_ref[...], dn_ref[...])
        _rotate_slab(uslab, urot, tm)
        _rotate_slab(dslab, drot, tm)

        def lane_block(cb, carry):
            cs = pl.ds(pl.multiple_of(cb * LANE, LANE), LANE)
            for r0 in range(0, tm, rc):
                acc = jnp.zeros((rc, LANE), F32)
                for k in range(CONV_K):
                    acc = acc + w_ref[k:k + 1, cs] * _shifted(dslab, drot, r0 + HALO + CONV_PAD - k, rc, cs)
                du_s[r0:r0 + rc, cs] = acc
            return carry

        def lane_block_taps(cb, carry):
            cs = pl.ds(pl.multiple_of(cb * LANE, LANE), LANE)
            parts = []
            for k in range(CONV_K):
                prod = None
                for r0 in range(0, tm, rc):
                    t = dslab[pl.ds(r0 + HALO, rc), cs] * _shifted(uslab, urot, r0 + HALO - CONV_PAD + k, rc, cs)
                    prod = t if prod is None else prod + t
                parts.append(_rows8(prod))
            rows = SUB * CONV_K
            dw8[0:rows, cs] = dw8[0:rows, cs] + jnp.concatenate(parts, axis=0)
            return carry

        lax.fori_loop(0, c // LANE, lane_block, 0)
        lax.fori_loop(0, c // LANE, lane_block_taps, 0)

        @pl.when(pl.program_id(0) == pl.num_programs(0) - 1)
        def _():
            dw_ref[...] = jnp.zeros_like(dw_ref)
            for k in range(CONV_K):
                dw_ref[k:k + 1, :] = jnp.sum(dw8[SUB * k:SUB * (k + 1), :], axis=0, keepdims=True)

        du = du_s[...]
        o_ref[:, 0:c] = (du * sg).astype(BF)
        o_ref[:, c:2 * c] = (du * a * sg * (1.0 - sg)).astype(BF)

    a_specs = _halo_specs(tm, s, c, 1)
    g_specs = _halo_specs(tm, s, c, 2)
    d_specs = _halo_specs(tm, s, c, 0)
    wsp = pl.BlockSpec(conv_w.shape, lambda i: (0, 0))
    return _call(
        body, name="conv_bwd_taps", grid=(s // tm,), in_specs=[*a_specs, *g_specs, *d_specs, wsp],
        out_specs=[pl.BlockSpec((tm, 2 * c), lambda i: (i, 0)), wsp],
        out_shape=[_sds((s, 2 * c), BF), _sds(conv_w.shape, F32)],
        scratch_shapes=[pltpu.VMEM(_slab_shapes(tm, c)[0], F32), pltpu.VMEM(_slab_shapes(tm, c)[0], F32), pltpu.VMEM((tm, c), F32),
                        pltpu.VMEM(_slab_shapes(tm, c)[1], F32), pltpu.VMEM(_slab_shapes(tm, c)[1], F32),
                        pltpu.VMEM((SUB * conv_w.shape[0], c), F32)],
        compiler_params=_cp(("arbitrary",)),
    )(h, h, h, h, h, h, duc, duc, duc, conv_w)


def attn_bwd(qc, kc, kct, v, dob, dot, lse_r, delta_r):
    _, s, _ = qc.shape
    tk = _tile(s, 512)
    tq = _tile(s, 512)
    scale = D_QK ** -0.5
    c2 = scale * LOG2E

    def body(k_ref, kt_ref, v_ref, q_ref, do_ref, dot_ref, l_ref, dl_ref, dqt_ref, dk_ref, dvt_ref):
        @pl.when(pl.program_id(1) == 0)
        def _():
            dqt_ref[...] = jnp.zeros_like(dqt_ref)

        k = k_ref[...]
        kt = kt_ref[...]
        vv = v_ref[...]

        def step(i, carry):
            dk, dvt = carry
            off = pl.multiple_of(i * tq, tq)
            q = q_ref[pl.ds(off, tq), :]
            do = do_ref[pl.ds(off, tq), :]
            pt = jnp.exp2(_dot_nt(k, q) * c2 - l_ref[:, pl.ds(off, tq)])
            dvt = dvt + _dot_nt(dot_ref[:, pl.ds(off, tq)], pt.astype(BF))
            dpt = _dot_nt(vv, do)
            dsb = (pt * (dpt - dl_ref[:, pl.ds(off, tq)]) * scale).astype(BF)
            dk = dk + _dot(dsb, q)
            dqt_ref[:, pl.ds(off, tq)] += _dot(kt, dsb)
            return dk, dvt

        dk, dvt = _unrolled_loop(s // tq, 4, step, (jnp.zeros((tk, 2 * LANE), F32), jnp.zeros((LANE, tk), F32)))
        dk_ref[...] = dk
        dvt_ref[...] = dvt

    rowv = pl.BlockSpec((None, 1, s), lambda h, j: (h, 0, 0))
    return _call(
        body, name="attn_bwd", grid=(HEADS, s // tk),
        in_specs=[pl.BlockSpec((None, tk, 2 * LANE), lambda h, j: (h, j, 0)),
                  pl.BlockSpec((None, 2 * LANE, tk), lambda h, j: (h, 0, j)),
                  pl.BlockSpec((None, tk, LANE), lambda h, j: (h, j, 0)),
                  pl.BlockSpec((None, s, 2 * LANE), lambda h, j: (h, 0, 0)),
                  pl.BlockSpec((s, LANE), lambda h, j: (0, h)),
                  pl.BlockSpec((None, LANE, s), lambda h, j: (h, 0, 0)), rowv, rowv],
        out_specs=[pl.BlockSpec((None, 2 * LANE, s), lambda h, j: (h, 0, 0)),
                   pl.BlockSpec((None, tk, 2 * LANE), lambda h, j: (h, j, 0)),
                   pl.BlockSpec((None, LANE, tk), lambda h, j: (h, 0, j))],
        out_shape=[_sds((HEADS, 2 * LANE, s), F32), _sds((HEADS, s, 2 * LANE), F32), _sds((HEADS, LANE, s), F32)],
        compiler_params=_cp(("arbitrary", "arbitrary"), 56),
    )(kc, kct, v, qc, dob, dot, lse_r, delta_r)


def q_bwd(dqt, h, g_cq, wuqt, cos, sin):
    s = h.shape[0]
    tm = _tile(s, 256)

    def body(d_ref, h_ref, g_ref, w_ref, c_ref, s_ref, dq_ref, dc_ref, dg_ref):
        @pl.when(pl.program_id(0) == 0)
        def _():
            dg_ref[...] = jnp.zeros_like(dg_ref)

        c = c_ref[...]
        sn = s_ref[...]
        for hd in range(HEADS):
            t = d_ref[hd].T
            dq_ref[:, LANE * hd:LANE * (hd + 1)] = t[:, 0:LANE].astype(BF)
            dq_ref[:, MLA_W + LANE * hd:MLA_W + LANE * (hd + 1)] = _unrope128(t[:, LANE:2 * LANE], c, sn).astype(BF)
        dy = _dot(dq_ref[...], w_ref[...])
        g = g_ref[...]
        _, xh, rr = _rms_fwd(h_ref[...], g)
        dg_ref[...] += _rows8(dy * xh)
        dc_ref[...] = _rms_bwd(dy, xh, rr, g).astype(BF)

    tab = pl.BlockSpec((tm, LANE), lambda i: (i, 0))
    return _call(
        body, name="q_bwd", grid=(s // tm,),
        in_specs=[pl.BlockSpec((HEADS, 2 * LANE, tm), lambda i: (0, 0, i)), pl.BlockSpec((tm, R_Q), lambda i: (i, 0)),
                  pl.BlockSpec((1, R_Q), lambda i: (0, 0)), pl.BlockSpec((2 * MLA_W, R_Q), lambda i: (0, 0)), tab, tab],
        out_specs=[pl.BlockSpec((tm, 2 * MLA_W), lambda i: (i, 0)), pl.BlockSpec((tm, R_Q), lambda i: (i, 0)),
                   pl.BlockSpec((SUB, R_Q), lambda i: (0, 0))],
        out_shape=[_sds((s, 2 * MLA_W), BF), _sds((s, R_Q), BF), _sds((SUB, R_Q), F32)],
        compiler_params=_cp(("arbitrary",)),
    )(dqt, h, g_cq, wuqt, cos, sin)


def kv_bwd(dk, dv, h, g_ckv, wukt, wuvt, cos, sin):
    s = h.shape[0]
    tm = _tile(s, 256)

    def body(dk_ref, dv_ref, h_ref, g_ref, wk_ref, wv_ref, c_ref, s_ref, dkn_ref, dvb_ref, dc_ref, dkr_ref, dg_ref):
        @pl.when(pl.program_id(0) == 0)
        def _():
            dg_ref[...] = jnp.zeros_like(dg_ref)

        dkr = dk_ref[0, :, LANE:2 * LANE]
        for hd in range(HEADS):
            dkn_ref[:, LANE * hd:LANE * (hd + 1)] = dk_ref[hd, :, 0:LANE].astype(BF)
            dvb_ref[:, LANE * hd:LANE * (hd + 1)] = dv_ref[hd].T.astype(BF)
            if hd > 0:
                dkr = dkr + dk_ref[hd, :, LANE:2 * LANE]
        dkr_ref[...] = _unrope128(dkr, c_ref[...], s_ref[...]).astype(BF)
        dy = _dot(dkn_ref[...], wk_ref[...]) + _dot(dvb_ref[...], wv_ref[...])
        g = g_ref[...]
        _, xh, rr = _rms_fwd(h_ref[...], g)
        dg_ref[...] += _rows8(dy * xh)
        dc_ref[...] = _rms_bwd(dy, xh, rr, g).astype(BF)

    tab = pl.BlockSpec((tm, LANE), lambda i: (i, 0))
    wsp = pl.BlockSpec((MLA_W, R_KV), lambda i: (0, 0))
    wide = pl.BlockSpec((tm, MLA_W), lambda i: (i, 0))
    return _call(
        body, name="kv_bwd", grid=(s // tm,),
        in_specs=[pl.BlockSpec((HEADS, tm, 2 * LANE), lambda i: (0, i, 0)), pl.BlockSpec((HEADS, LANE, tm), lambda i: (0, 0, i)),
                  pl.BlockSpec((tm, R_KV), lambda i: (i, 1)), pl.BlockSpec((1, R_KV), lambda i: (0, 0)), wsp, wsp, tab, tab],
        out_specs=[wide, wide, pl.BlockSpec((tm, R_KV), lambda i: (i, 0)), tab, pl.BlockSpec((SUB, R_KV), lambda i: (0, 0))],
        out_shape=[_sds((s, MLA_W), BF), _sds((s, MLA_W), BF), _sds((s, R_KV), BF), _sds((s, LANE), BF), _sds((SUB, R_KV), F32)],
        compiler_params=_cp(("arbitrary",)),
    )(dk, dv, h, g_ckv, wukt, wuvt, cos, sin)


def in_proj_bwd_ln(dh, wint, dr1, x, g_in):
    s, hc = dh.shape
    d = x.shape[1]
    tm = _tile(s, 512)
    tk = _tile(hc, 640)
    nk = hc // tk

    def body(a_ref, w_ref, d1_ref, x_ref, g_ref, gx_ref, dg_ref, db_ref, acc):
        i = pl.program_id(0)
        k = pl.program_id(1)

        @pl.when(k == 0)
        def _():
            acc[...] = _dot(a_ref[...], w_ref[...])

        @pl.when(k > 0)
        def _():
            acc[...] += _dot(a_ref[...], w_ref[...])

        @pl.when(jnp.logical_and(i == 0, k == 0))
        def _():
            dg_ref[...] = jnp.zeros_like(dg_ref)
            db_ref[...] = jnp.zeros_like(db_ref)

        @pl.when(k == nk - 1)
        def _():
            g = g_ref[...]

            def chunk(rows):
                dy = ALPHA * d1_ref[rows, :] + acc[rows, :]
                xhat, rstd = _ln_stats(x_ref[rows, :])
                dg_ref[...] += _rows8(dy * xhat)
                db_ref[...] += _rows8(dy)
                gx_ref[rows, :] = _ln_bwd(dy, xhat, rstd, g)

            _row_chunks(tm, chunk)

    tok = pl.BlockSpec((tm, d), lambda i, k: (i, 0))
    accs = pl.BlockSpec((SUB, d), lambda i, k: (0, 0))
    return _call(
        body, name="in_proj_bwd_ln", grid=(s // tm, nk),
        in_specs=[pl.BlockSpec((tm, tk), lambda i, k: (i, k)), pl.BlockSpec((tk, d), lambda i, k: (k, 0)),
                  tok, tok, pl.BlockSpec((1, d), lambda i, k: (0, 0))],
        out_specs=[tok, accs, accs], out_shape=[_sds((s, d), F32), _sds((SUB, d), F32), _sds((SUB, d), F32)],
        scratch_shapes=[pltpu.VMEM((tm, d), F32)], compiler_params=_cp(("arbitrary", "arbitrary")),
    )(dh, wint, dr1, x, g_in)


def _adamw_math(w, g, m, v):
    m = ADAM_B1 * m + (1.0 - ADAM_B1) * g
    v = ADAM_B2 * v + (1.0 - ADAM_B2) * (g * g)
    m_hat = m / (1.0 - ADAM_B1 ** ADAM_STEP)
    v_hat = v / (1.0 - ADAM_B2 ** ADAM_STEP)
    delta = -ADAM_LR * (m_hat / (jnp.sqrt(v_hat) + ADAM_EPS) + ADAM_WD * w)
    return delta, m, v


def adamw(name, w, g, m, v):
    r, c = w.shape
    tr = _row_tile(r, c)

    def body(w_ref, g_ref, m_ref, v_ref, d_ref, mo_ref, vo_ref):
        d_ref[...], mo_ref[...], vo_ref[...] = _adamw_math(w_ref[...], g_ref[...], m_ref[...], v_ref[...])

    blk = pl.BlockSpec((tr, c), lambda i: (i, 0))
    return _call(
        body, name=name, grid=(r // tr,), in_specs=[blk] * 4, out_specs=[blk] * 3,
        out_shape=[_sds((r, c), F32)] * 3, compiler_params=_cp(("arbitrary",)),
    )(w, g, m, v)


def _coords():
    return lax.axis_index("x"), lax.axis_index("y"), lax.axis_index("c")


def _other_chips(x, y):
    return [(1 - x, y, 2 * (1 - x) + y), (x, 1 - y, 2 * x + 1 - y), (1 - x, 1 - y, 2 * (1 - x) + 1 - y)]


ANY = pl.BlockSpec(memory_space=pl.ANY)
HBM = pl.BlockSpec(memory_space=pltpu.HBM)
SEM = pl.BlockSpec(memory_space=pltpu.SEMAPHORE)
EFFECT = pltpu.SideEffectType.DATAFLOW_SIDE_EFFECTING


def _in_hbm(a):
    return pltpu.with_memory_space_constraint(a, pltpu.HBM)


def _split_plan(mode, src, land, x, y, c):
    if mode == "pair":
        rh = src.shape[1] // 2
        return [((x, y, 1 - c), src.at[:, pl.ds((1 - c) * rh, rh)], land, land)]
    me = 2 * x + y
    plan = []
    for j, (px, py, pk) in enumerate(_other_chips(x, y)):
        if mode == "gather":
            plan.append(((px, py, c), src, land.at[me], land.at[pk]))
        else:
            plan.append(((px, py, c), src.at[pk], land.at[j], land.at[j]))
    return plan


def _plan_len(mode):
    return 1 if mode == "pair" else N_CHIP - 1


def split_send_start(name, mode, srcs, land_shapes, order_after):
    n = len(srcs)
    np_ = _plan_len(mode)

    def body(*refs):
        ins, lands = refs[:n], refs[n:2 * n]
        ss, rs = refs[2 * n + 1], refs[2 * n + 2]
        token = refs[-1]
        x, y, c = _coords()
        for a in range(n):
            for j, (peer, src, dst, _) in enumerate(_split_plan(mode, ins[a], lands[a], x, y, c)):
                pltpu.make_async_remote_copy(src_ref=src, dst_ref=dst, send_sem=ss.at[np_ * a + j], recv_sem=rs.at[np_ * a + j],
                                             device_id=peer, device_id_type=MESH).start()
        token[...] = jnp.zeros_like(token)

    lands = [lax.empty(shp, s.dtype) for shp, s in zip(land_shapes, srcs)]
    outs = _call(
        body, name=name,
        out_shape=(pltpu.SemaphoreType.DMA((np_ * n,)), pltpu.SemaphoreType.DMA((np_ * n,)),
                   *[pltpu.HBM(s.shape, s.dtype) for s in srcs], *[pltpu.HBM(l.shape, l.dtype) for l in lands],
                   _sds((SUB, LANE), F32)),
        in_specs=[HBM] * (2 * n) + [ANY], out_specs=(SEM, SEM, *[HBM] * (2 * n), pl.BlockSpec(memory_space=pltpu.VMEM)),
        input_output_aliases={a: 2 + a for a in range(2 * n)},
        compiler_params=pltpu.CompilerParams(has_side_effects=EFFECT),
    )(*[_in_hbm(s) for s in srcs], *[_in_hbm(l) for l in lands], order_after)
    return outs[0], outs[1], list(outs[2:2 + n]), list(outs[2 + n:2 + 2 * n]), outs[-1]


def split_send_wait(name, mode, ss, rs, srcs, lands, order_after):
    n = len(srcs)
    np_ = _plan_len(mode)

    def body(*refs):
        ins, lnd = refs[:n], refs[n:2 * n]
        s_ref, r_ref = refs[2 * n], refs[2 * n + 1]
        x, y, c = _coords()
        for a in range(n):
            for j, (peer, src, _, got) in enumerate(_split_plan(mode, ins[a], lnd[a], x, y, c)):
                cp = pltpu.make_async_remote_copy(src_ref=src, dst_ref=got, send_sem=s_ref.at[np_ * a + j], recv_sem=r_ref.at[np_ * a + j],
                                                  device_id=peer, device_id_type=MESH)
                cp.wait_send()
                cp.wait_recv()

    outs = _call(
        body, name=name, out_shape=tuple(pltpu.HBM(t.shape, t.dtype) for t in (*srcs, *lands)),
        in_specs=[HBM] * (2 * n) + [SEM, SEM, ANY], out_specs=tuple([HBM] * (2 * n)),
        input_output_aliases={a: a for a in range(2 * n)},
        compiler_params=pltpu.CompilerParams(has_side_effects=EFFECT),
    )(*srcs, *lands, ss, rs, order_after)
    return list(outs[:n]), list(outs[n:])


def all_gather_shards(shards):
    n = len(shards)

    def body(*refs):
        ins, outs = refs[:n], refs[n:2 * n]
        ici_s, ici_r, d2d_s, d2d_r = refs[2 * n:]
        x, y, c = _coords()
        me = 2 * x + y
        peers = _other_chips(x, y)
        sends, fwds = [], []
        for a in range(n):
            rh = ins[a].shape[0] // 2
            mine = pl.ds(c * rh, rh)
            for j, (px, py, pk) in enumerate(peers):
                cp = pltpu.make_async_remote_copy(
                    src_ref=ins[a].at[mine], dst_ref=outs[a].at[me, mine], send_sem=ici_s.at[a, j], recv_sem=ici_r.at[a, j],
                    device_id=(px, py, c), device_id_type=MESH)
                cp.start()
                sends.append(cp)
        for a in range(n):
            rh = ins[a].shape[0] // 2
            mine = pl.ds(c * rh, rh)
            for j, (px, py, pk) in enumerate(peers):
                got = outs[a].at[pk, mine]
                pltpu.make_async_remote_copy(
                    src_ref=got, dst_ref=got, send_sem=ici_s.at[a, j], recv_sem=ici_r.at[a, j],
                    device_id=(px, py, c), device_id_type=MESH).wait_recv()
                fw = pltpu.make_async_remote_copy(
                    src_ref=got, dst_ref=got, send_sem=d2d_s.at[a, j], recv_sem=d2d_r.at[a, j],
                    device_id=(x, y, 1 - c), device_id_type=MESH)
                fw.start()
                fwds.append(fw)
        for a in range(n):
            rh = ins[a].shape[0] // 2
            theirs = pl.ds((1 - c) * rh, rh)
            for j, (px, py, pk) in enumerate(peers):
                got = outs[a].at[pk, theirs]
                pltpu.make_async_remote_copy(
                    src_ref=got, dst_ref=got, send_sem=d2d_s.at[a, j], recv_sem=d2d_r.at[a, j],
                    device_id=(x, y, 1 - c), device_id_type=MESH).wait_recv()
        for cp in sends + fwds:
            cp.wait_send()

    got = _call(
        body, name="all_gather_shards", in_specs=[ANY] * n, out_specs=[ANY] * n,
        out_shape=[_sds((N_CHIP,) + w.shape, w.dtype) for w in shards],
        scratch_shapes=[pltpu.SemaphoreType.DMA((n, 3))] * 4,
    )(*shards)
    me = 2 * lax.axis_index("x") + lax.axis_index("y")
    return [lax.dynamic_update_slice(g, w[None], (me, 0, 0)) for g, w in zip(got, shards)]


def pair_exchange(grads, tag):
    n = len(grads)

    def body(*refs):
        ins, outs = refs[:n], refs[n:2 * n]
        ss, rs = refs[2 * n:]
        x, y, c = _coords()
        cps = []
        for a in range(n):
            rh = ins[a].shape[1] // 2
            cp = pltpu.make_async_remote_copy(
                src_ref=ins[a].at[:, pl.ds((1 - c) * rh, rh)], dst_ref=outs[a], send_sem=ss.at[a], recv_sem=rs.at[a],
                device_id=(x, y, 1 - c), device_id_type=MESH)
            cp.start()
            cps.append(cp)
        for cp in cps:
            cp.wait()

    return _call(
        body, name="pair_exchange_" + tag, in_specs=[ANY] * n, out_specs=[ANY] * n,
        out_shape=[_sds((N_CHIP, g.shape[1] // 2, g.shape[2]), F32) for g in grads],
        scratch_shapes=[pltpu.SemaphoreType.DMA((n,))] * 2,
    )(*grads)


def _row_tile(rows, cols, itemsize=4, budget=2 * VMEM_MB):
    t = rows
    while t * cols * itemsize > budget and t % (2 * SUB) == 0:
        t //= 2
    return t


def pair_add(g, r, cidx):
    _, rows, cols = g.shape
    rh = rows // 2
    tr = _row_tile(rh, cols)
    per = rh // tr

    def body(c_ref, g_ref, r_ref, o_ref):
        o_ref[...] = g_ref[...] + r_ref[...]

    return _call(
        body, name="pair_add",
        grid_spec=pltpu.PrefetchScalarGridSpec(
            num_scalar_prefetch=1, grid=(N_CHIP, per),
            in_specs=[pl.BlockSpec((None, tr, cols), lambda k, i, c: (k, c[0] * per + i, 0)),
                      pl.BlockSpec((None, tr, cols), lambda k, i, c: (k, i, 0))],
            out_specs=pl.BlockSpec((None, tr, cols), lambda k, i, c: (k, i, 0))),
        out_shape=_sds((N_CHIP, rh, cols), F32), compiler_params=_cp(("arbitrary", "arbitrary")),
    )(cidx, g, r)


def chip_add(p, r, kc):
    _, rh, cols = p.shape
    tr = _row_tile(rh, cols)
    per = rh // tr

    def body(k_ref, p_ref, r_ref, o_ref):
        o_ref[...] = ((p_ref[...] + r_ref[0]) + r_ref[1]) + r_ref[2]

    return _call(
        body, name="chip_add",
        grid_spec=pltpu.PrefetchScalarGridSpec(
            num_scalar_prefetch=1, grid=(per,),
            in_specs=[pl.BlockSpec((None, tr, cols), lambda i, k: (k[0], i, 0)),
                      pl.BlockSpec((N_CHIP - 1, tr, cols), lambda i, k: (0, i, 0))],
            out_specs=pl.BlockSpec((tr, cols), lambda i, k: (k[1] * per + i, 0))),
        out_shape=_sds((2 * rh, cols), F32), compiler_params=_cp(("arbitrary",)),
    )(kc, p, r)


def pair_share(fulls, tag):
    n = len(fulls)

    def body(*refs):
        outs = refs[n:2 * n]
        ss, rs = refs[2 * n:]
        x, y, c = _coords()
        cps = []
        for a in range(n):
            rh = outs[a].shape[0] // 2
            mine = outs[a].at[pl.ds(c * rh, rh)]
            cp = pltpu.make_async_remote_copy(
                src_ref=mine, dst_ref=mine, send_sem=ss.at[a], recv_sem=rs.at[a],
                device_id=(x, y, 1 - c), device_id_type=MESH)
            cp.start()
            cps.append(cp)
        for a, cp in enumerate(cps):
            rh = outs[a].shape[0] // 2
            theirs = outs[a].at[pl.ds((1 - c) * rh, rh)]
            cp.wait_send()
            pltpu.make_async_remote_copy(
                src_ref=theirs, dst_ref=theirs, send_sem=ss.at[a], recv_sem=rs.at[a],
                device_id=(x, y, 1 - c), device_id_type=MESH).wait_recv()

    return _call(
        body, name="pair_share_" + tag, in_specs=[ANY] * n, out_specs=[ANY] * n,
        out_shape=[_sds(f.shape, F32) for f in fulls], input_output_aliases={a: a for a in range(n)},
        scratch_shapes=[pltpu.SemaphoreType.DMA((n,))] * 2,
    )(*fulls)


def small_allreduce_adamw(part, w, m, v):
    n = part.shape[1]

    def body(p_ref, w_ref, m_ref, v_ref, g_ref, d_ref, mo_ref, vo_ref, mine, gath, ss, rs):
        x, y, c = _coords()
        me = 4 * x + 2 * y + c
        mine[...] = jnp.sum(p_ref[...], axis=0, keepdims=True)
        gath[me] = mine[...]
        cps = []
        for k in range(1, 8):
            px, py, pc = x ^ (k >> 2), y ^ ((k >> 1) & 1), c ^ (k & 1)
            cp = pltpu.make_async_remote_copy(
                src_ref=mine, dst_ref=gath.at[me], send_sem=ss.at[k - 1], recv_sem=rs.at[k - 1],
                device_id=(px, py, pc), device_id_type=MESH)
            cp.start()
            cps.append(cp)
        for k in range(1, 8):
            src = 4 * (x ^ (k >> 2)) + 2 * (y ^ ((k >> 1) & 1)) + (c ^ (k & 1))
            pltpu.make_async_remote_copy(
                src_ref=mine, dst_ref=gath.at[src], send_sem=ss.at[k - 1], recv_sem=rs.at[k - 1],
                device_id=(x, y, c), device_id_type=MESH).wait_recv()
        for cp in cps:
            cp.wait_send()
        g = gath[0]
        for dv in range(1, 8):
            g = g + gath[dv]
        g_ref[...] = g
        d_ref[...], mo_ref[...], vo_ref[...] = _adamw_math(w_ref[...], g, m_ref[...], v_ref[...])

    vm = pl.BlockSpec(memory_space=pltpu.VMEM)
    return _call(
        body, name="small_allreduce_adamw", in_specs=[vm] * 4, out_specs=[vm] * 4, out_shape=[_sds((1, n), F32)] * 4,
        scratch_shapes=[pltpu.VMEM((1, n), F32), pltpu.VMEM((8, 1, n), F32),
                        pltpu.SemaphoreType.DMA((7,)), pltpu.SemaphoreType.DMA((7,))],
    )(part, w, m, v)


def _unshard_cols(g):
    k, r, cs = g.shape
    return g.transpose(1, 0, 2).reshape(r, k * cs)


def _shard_cols(w):
    r, c = w.shape
    return w.reshape(r, N_CHIP, c // N_CHIP).transpose(1, 0, 2)


def local_step(x, positions, ln_in_g, ln_in_b, win_g, g_cq, wuq_g, g_ckv, wuk_g, wuv_g, convw_g, conv_b, g_conv_ln,
               b_conv_ln, g_ln1, b_ln1, g_ln2, b_ln2, target, start_token, hooks):
    s, d = x.shape
    c = d - MLA_W
    row = lambda a: a.reshape(1, -1)
    ln_in_g = row(ln_in_g) + start_token[0:1, 0:1]

    win = _unshard_cols(win_g)
    o_kr = R_Q + R_KV
    o_cv = o_kr + D_ROPE
    win_r = jnp.concatenate([win[:, :o_kr], win[:, o_cv:], win[:, o_kr:o_cv], jnp.zeros((d, LANE - D_ROPE), BF)], axis=1)
    hc = win_r.shape[1]
    kr_blk = (o_kr + 2 * c) // LANE
    wuq = _unshard_cols(wuq_g).reshape(R_Q, HEADS, D_QK)
    wuq_r = jnp.concatenate([wuq[:, :, :D_NOPE].reshape(R_Q, MLA_W),
                             jnp.pad(wuq[:, :, D_NOPE:], ((0, 0), (0, 0), (0, LANE - D_ROPE))).reshape(R_Q, MLA_W)], axis=1)
    wuk = _unshard_cols(wuk_g)
    wuv = _unshard_cols(wuv_g)
    conv_w = jnp.pad(_unshard_cols(convw_g), ((0, 1), (0, 0)))

    half = D_ROPE // 2
    inv_freq = ROPE_BASE ** (-jnp.arange(half, dtype=F32) * (2.0 / D_ROPE))
    invf = jnp.concatenate([inv_freq, inv_freq, jnp.zeros((LANE - D_ROPE,), F32)]).reshape(1, LANE)
    cos, sin = rope_tables(positions.astype(F32).reshape(s, 1), invf)
    x0, x0b = ln_in_fwd(x, ln_in_g, row(ln_in_b))
    h = matmul("in_proj", x0b, win_r, 1024, 640)
    qc, cqn = q_proj(h, g_cq, wuq_r, cos, sin)
    kc, kct, v, ckvn = kv_proj(h, g_ckv, wuk, wuv, cos, sin, kr_blk)
    o, ob, lse = attn_fwd(qc, kc, v)
    co, uc = conv_fwd(h, conv_w, conv_b, g_conv_ln, b_conv_ln)
    wout_g, wff1_g, wff2_g = hooks.late_weights(ob)
    wout = wout_g.reshape(d, d)
    wff2 = wff2_g.reshape(-1, d)
    wff1t = wff1_g.transpose(0, 2, 1).reshape(-1, d)
    wff2t = wff2.T
    r1, x1, x1b = out_proj_ln1(ob, co, wout, x0, g_ln1, b_ln1)
    rb, a1b = ff1_fwd(x1b, wff1_g)
    dr2, dr2b, loss8, dg2, db2 = ff2_ln2_loss(a1b, wff2, x1, target, g_ln2, b_ln2)

    df1b = ff2_bwd_act(dr2b, wff2t, rb)
    gw_ff2 = wgrad("wgrad_ff2", a1b, dr2b, 1024, 1024).reshape(N_CHIP, -1, d)
    gw_ff1 = wgrad("wgrad_ff1", x1b, df1b, 1024, 1024, shards=N_CHIP)
    tok = hooks.ff_grads(gw_ff2, gw_ff1)
    dr1, dr1b, dg1, db1 = ff1_bwd_ln1(df1b, wff1t, dr2, r1, g_ln1 + tok[0:1, 0:1])
    tok = hooks.ff_grads_mid(dr1b)
    gw_out = jnp.concatenate([wgrad("wgrad_out_attn", ob, dr1b, 1024, 1024)[0],
                              wgrad("wgrad_out_conv", co, dr1b, 1024, 1024)[0]], axis=0).reshape(N_CHIP, -1, d)
    dob, dot, dco, delta = out_proj_bwd(dr1b, wout.T, o)
    duc, dgc, dbc, dcb = conv_bwd_ln(uc, dco, g_conv_ln + tok[0:1, 0:1], b_conv_ln)
    dconv, gconvw = conv_bwd_taps(h, duc, conv_w)
    dqt, dk, dv = attn_bwd(qc, kc, kct, v, dob, dot, lse, delta)
    dqb, dcq, dgq = q_bwd(dqt, h, g_cq, wuq_r.T, cos, sin)
    dknb, dvb, dckv, dkr, dgkv = kv_bwd(dk, dv, h, g_ckv, wuk.T, wuv.T, cos, sin)
    gwuq_r = wgrad("wgrad_uq", cqn, dqb, 512, 1024)[0]
    gw_uk = wgrad("wgrad_uk", ckvn, dknb, 512, 1024, shards=N_CHIP)
    gw_uv = wgrad("wgrad_uv", ckvn, dvb, 512, 1024, shards=N_CHIP)
    dh = jnp.concatenate([dcq, dckv, dconv, dkr], axis=1)
    gwin_r = wgrad("wgrad_in", x0b, dh, 1024, 640)[0]

    gwin = jnp.concatenate([gwin_r[:, :o_kr], gwin_r[:, o_kr + 2 * c:o_kr + 2 * c + D_ROPE], gwin_r[:, o_kr:o_kr + 2 * c]], axis=1)
    gwuq = jnp.concatenate([gwuq_r[:, :MLA_W].reshape(R_Q, HEADS, D_NOPE),
                            gwuq_r[:, MLA_W:].reshape(R_Q, HEADS, LANE)[:, :, :D_ROPE]], axis=2).reshape(R_Q, HEADS * D_QK)
    tok = hooks.rest_grads(dict(w_in=_shard_cols(gwin), w_uq=_shard_cols(gwuq), w_uk=gw_uk, w_uv=gw_uv,
                                conv_w=_shard_cols(gconvw), w_out=gw_out))
    gx, dgin, dbin = in_proj_bwd_ln(dh, win_r.T, dr1, x, ln_in_g + tok[0:1, 0:1])
    small = jnp.concatenate([dgin, dbin, dgq, dgkv, dcb, dgc, dbc, dg1, db1, dg2, db2, loss8], axis=1)
    return gx, small


BIG = ["w_in", "w_uq", "w_uk", "w_uv", "conv_w", "w_out", "w_ff1", "w_ff2"]
EARLY = ["w_in", "w_uq", "w_uk", "w_uv", "conv_w"]
LATE = ["w_out", "w_ff1", "w_ff2"]
SMALL = ["ln_in_g", "ln_in_b", "g_cq", "g_ckv", "conv_b", "g_conv_ln", "b_conv_ln", "g_ln1", "b_ln1", "g_ln2", "b_ln2"]
WEIGHTS = ["ln_in_g", "ln_in_b", "w_in", "g_cq", "w_uq", "g_ckv", "w_uk", "w_uv", "conv_w", "conv_b", "g_conv_ln",
           "b_conv_ln", "w_out", "g_ln1", "b_ln1", "w_ff1", "w_ff2", "g_ln2", "b_ln2"]


def _pad_rows(a, rows):
    return jnp.pad(a, ((0, rows - a.shape[0]), (0, 0)))


def kernel(x, positions, ln_in_g, ln_in_b, w_in, g_cq, w_uq, g_ckv, w_uk, w_uv, conv_w, conv_b, g_conv_ln, b_conv_ln, w_out, g_ln1, b_ln1, w_ff1, w_ff2, g_ln2, b_ln2, loss_target, m_ln_in_g, m_ln_in_b, m_w_in, m_g_cq, m_w_uq, m_g_ckv, m_w_uk, m_w_uv, m_conv_w, m_conv_b, m_g_conv_ln, m_b_conv_ln, m_w_out, m_g_ln1, m_b_ln1, m_w_ff1, m_w_ff2, m_g_ln2, m_b_ln2, v_ln_in_g, v_ln_in_b, v_w_in, v_g_cq, v_w_uq, v_g_ckv, v_w_uk, v_w_uv, v_conv_w, v_conv_b, v_g_conv_ln, v_b_conv_ln, v_w_out, v_g_ln1, v_b_ln1, v_w_ff1, v_w_ff2, v_g_ln2, v_b_ln2):
    w = dict(ln_in_g=ln_in_g, ln_in_b=ln_in_b, w_in=w_in, g_cq=g_cq, w_uq=w_uq, g_ckv=g_ckv, w_uk=w_uk, w_uv=w_uv,
             conv_w=conv_w, conv_b=conv_b, g_conv_ln=g_conv_ln, b_conv_ln=b_conv_ln, w_out=w_out, g_ln1=g_ln1,
             b_ln1=b_ln1, w_ff1=w_ff1, w_ff2=w_ff2, g_ln2=g_ln2, b_ln2=b_ln2)
    m = dict(ln_in_g=m_ln_in_g, ln_in_b=m_ln_in_b, w_in=m_w_in, g_cq=m_g_cq, w_uq=m_w_uq, g_ckv=m_g_ckv, w_uk=m_w_uk,
             w_uv=m_w_uv, conv_w=m_conv_w, conv_b=m_conv_b, g_conv_ln=m_g_conv_ln, b_conv_ln=m_b_conv_ln, w_out=m_w_out,
             g_ln1=m_g_ln1, b_ln1=m_b_ln1, w_ff1=m_w_ff1, w_ff2=m_w_ff2, g_ln2=m_g_ln2, b_ln2=m_b_ln2)
    v = dict(ln_in_g=v_ln_in_g, ln_in_b=v_ln_in_b, w_in=v_w_in, g_cq=v_g_cq, w_uq=v_w_uq, g_ckv=v_g_ckv, w_uk=v_w_uk,
             w_uv=v_w_uv, conv_w=v_conv_w, conv_b=v_conv_b, g_conv_ln=v_g_conv_ln, b_conv_ln=v_b_conv_ln, w_out=v_w_out,
             g_ln1=v_g_ln1, b_ln1=v_b_ln1, w_ff1=v_w_ff1, w_ff2=v_w_ff2, g_ln2=v_g_ln2, b_ln2=v_b_ln2)

    sh2 = {n: w[n][0] for n in BIG}
    cidx = lax.axis_index("c").astype(jnp.int32).reshape(1)
    me = 2 * lax.axis_index("x") + lax.axis_index("y")
    kc = jnp.stack([me, lax.axis_index("c")]).astype(jnp.int32)

    early = [sh2[n].astype(BF) if n != "conv_w" else _pad_rows(sh2[n], CONV_K + 1) for n in EARLY]
    gw = dict(zip(EARLY, all_gather_shards(early)))
    gw["conv_w"] = gw["conv_w"][:, :CONV_K]
    late = [sh2[n].astype(BF) for n in LATE]
    ag = split_send_start("late_weights_start", "gather", late, [(N_CHIP,) + a.shape for a in late], gw["w_uq"])
    rest = [n for n in BIG if n not in ("w_ff2", "w_ff1")]
    flight = {}

    class Hooks:
        @staticmethod
        def late_weights(after):
            mine, lands = split_send_wait("late_weights_wait", "gather", *ag[:4], after)
            return [lax.dynamic_update_slice(g, a[None], (me, 0, 0)) for g, a in zip(lands, mine)]

        @staticmethod
        def ff_grads(gw_ff2, gw_ff1):
            full = [gw_ff2, gw_ff1]
            st = split_send_start("ff_pair_start", "pair", full, [(N_CHIP, g.shape[1] // 2, g.shape[2]) for g in full], ag[4])
            flight["ff_pair"] = st[:4]
            flight["token"] = st[4]
            return st[4]

        @staticmethod
        def ff_grads_mid(after):
            full, recv = split_send_wait("ff_pair_wait", "pair", *flight["ff_pair"], after)
            psum = [pair_add(g, r, cidx) for g, r in zip(full, recv)]
            st = split_send_start("ff_grads_start", "scatter", psum, [(N_CHIP - 1,) + p.shape[1:] for p in psum], flight["token"])
            flight["ff"] = st[:4]
            flight["token"] = st[4]
            return st[4]

        @staticmethod
        def rest_grads(big):
            full = [big[n] for n in rest]
            psum = [pair_add(g, r, cidx) for g, r in zip(full, pair_exchange(full, "rest"))]
            st = split_send_start("rest_grads_start", "scatter", psum, [(N_CHIP - 1,) + p.shape[1:] for p in psum], flight["token"])
            flight["rest"] = st[:4]
            return st[4]

    gx, small = local_step(
        x[0], positions[0], ln_in_g, ln_in_b, gw["w_in"], g_cq, gw["w_uq"], g_ckv, gw["w_uk"], gw["w_uv"], gw["conv_w"],
        conv_b, g_conv_ln, b_conv_ln, g_ln1, b_ln1, g_ln2, b_ln2, loss_target[0], ag[4], Hooks)

    ff_psum, ff_got = split_send_wait("ff_grads_wait", "scatter", *flight["ff"], gx)
    rest_psum, rest_got = split_send_wait("rest_grads_wait", "scatter", *flight["rest"], gx)
    summed = [chip_add(p, r, kc) for p, r in zip(rest_psum + ff_psum, rest_got + ff_got)]
    gsh = dict(zip(rest + ["w_ff2", "w_ff1"], pair_share(summed, "all")))
    gsh["conv_w"] = gsh["conv_w"][:CONV_K]

    grad, delta, new_m, new_v = {}, {}, {}, {}
    for n in BIG:
        grad[n] = gsh[n][None]
        d_, m_, v_ = adamw("adamw_" + n, sh2[n], gsh[n], m[n][0], v[n][0])
        delta[n], new_m[n], new_v[n] = d_[None], m_[None], v_[None]

    flat = lambda t: jnp.concatenate([t[n].reshape(1, -1) for n in SMALL] + [jnp.zeros((1, LANE), F32)], axis=1)
    g_s, d_s, m_s, v_s = small_allreduce_adamw(small, flat(w), flat(m), flat(v))
    off = 0
    for n in SMALL:
        sz = w[n].size
        for dst, src in ((grad, g_s), (delta, d_s), (new_m, m_s), (new_v, v_s)):
            dst[n] = src[0, off:off + sz].reshape(w[n].shape)
        off += sz
    loss = jnp.sum(g_s[0, off:off + LANE])

    return (loss, gx[None], *[grad[n] for n in WEIGHTS], *[delta[n] for n in WEIGHTS],
            *[new_m[n] for n in WEIGHTS], *[new_v[n] for n in WEIGHTS])
```

```python
import functools

import jax
import jax.numpy as jnp
from jax import lax
from jax.experimental import pallas as pl
from jax.experimental.pallas import tpu as pltpu

F32 = jnp.float32
BF = jnp.bfloat16

HEADS = 8
D_NOPE = 128
D_ROPE = 64
D_V = 128
D_QK = D_NOPE + D_ROPE
R_Q = 512
R_KV = 512
MLA_W = HEADS * D_V
CONV_K = 31
CONV_PAD = CONV_K // 2
ROPE_BASE = 10000.0
LOG2E = 1.4426950408889634
LN2 = 0.6931471805599453
LN_EPS = 1e-5
RMS_EPS = 1e-6
ALPHA = (2.0 * 1) ** 0.25
ADAM_LR = 0.001
ADAM_B1 = 0.9
ADAM_B2 = 0.999
ADAM_EPS = 1e-08
ADAM_WD = 0.01
ADAM_STEP = 10

LANE = 128
SUB = 8
HALO = 16
N_CHIP = 4
MESH = pl.DeviceIdType.MESH
VMEM_MB = 1024 * 1024


def _call(body, **kw):
    return pl.pallas_call(body, **kw)


def _cp(sem, mb=48):
    return pltpu.CompilerParams(dimension_semantics=sem, vmem_limit_bytes=mb * VMEM_MB)


def _sds(shape, dt):
    return jax.ShapeDtypeStruct(shape, dt)


def _dot(a, b):
    return jnp.dot(a, b, preferred_element_type=F32)


def _dot_nt(a, b):
    return lax.dot_general(a, b, (((1,), (1,)), ((), ())), preferred_element_type=F32)


def _dot_tn(a, b):
    return lax.dot_general(a, b, (((0,), (0,)), ((), ())), preferred_element_type=F32)


def _rows8(v):
    t, n = v.shape
    return v.reshape(t // SUB, SUB, n).sum(axis=0)


def _ln_stats(r):
    mu = jnp.mean(r, axis=-1, keepdims=True)
    xc = r - mu
    var = jnp.mean(xc * xc, axis=-1, keepdims=True)
    rstd = lax.rsqrt(var + LN_EPS)
    return xc * rstd, rstd


def _ln_bwd(dy, xhat, rstd, g):
    dyh = dy * g
    m1 = jnp.mean(dyh, axis=-1, keepdims=True)
    m2 = jnp.mean(dyh * xhat, axis=-1, keepdims=True)
    return rstd * (dyh - m1 - xhat * m2)


def _rms_fwd(x, g):
    rr = lax.rsqrt(jnp.mean(x * x, axis=-1, keepdims=True) + RMS_EPS)
    xh = x * rr
    return xh * g, xh, rr


def _rms_bwd(dy, xh, rr, g):
    dyg = dy * g
    return rr * (dyg - xh * jnp.mean(dyg * xh, axis=-1, keepdims=True))


def _rope128(x, cos, sin_signed):
    lane = lax.broadcasted_iota(jnp.int32, x.shape, 1)
    rot = jnp.where(lane < D_ROPE // 2, pltpu.roll(x, LANE - D_ROPE // 2, 1), pltpu.roll(x, D_ROPE // 2, 1))
    return x * cos + rot * sin_signed


def _unrope128(dy, cos, sin_signed):
    t = dy * sin_signed
    lane = lax.broadcasted_iota(jnp.int32, dy.shape, 1)
    rot = jnp.where(lane < D_ROPE // 2, pltpu.roll(t, LANE - D_ROPE // 2, 1), pltpu.roll(t, D_ROPE // 2, 1))
    return dy * cos + rot


def _as_row(col):
    return jnp.transpose(jnp.broadcast_to(col, (col.shape[0], LANE)))[0:1, :]


def _sigmoid(x):
    return 1.0 / (1.0 + jnp.exp(-x))


def _row_chunks(tm, fn, rc=128):
    rc = min(rc, tm)

    def step(ci, carry):
        fn(pl.ds(pl.multiple_of(ci * rc, rc), rc))
        return carry

    lax.fori_loop(0, tm // rc, step, 0)


def _unrolled_loop(n, unroll, fn, init):
    unroll = min(n, unroll)
    assert n % unroll == 0

    def body(t, carry):
        for u in range(unroll):
            carry = fn(t * unroll + u, carry)
        return carry

    return lax.fori_loop(0, n // unroll, body, init)


def _tile(s, want):
    t = min(s, want)
    assert s % t == 0
    return t


def rope_tables(pos_f, invf):
    s = pos_f.shape[0]
    tm = _tile(s, 1024)

    def body(p_ref, f_ref, c_ref, s_ref):
        ang = p_ref[...] * f_ref[...]
        lane = lax.broadcasted_iota(jnp.int32, ang.shape, 1)
        c = jnp.cos(ang)
        sn = jnp.sin(ang)
        c_ref[...] = jnp.where(lane < D_ROPE, c, 0.0)
        s_ref[...] = jnp.where(lane < D_ROPE // 2, -sn, jnp.where(lane < D_ROPE, sn, 0.0))

    return _call(
        body, name="rope_tables", grid=(s // tm,),
        in_specs=[pl.BlockSpec((tm, 1), lambda i: (i, 0)), pl.BlockSpec((1, LANE), lambda i: (0, 0))],
        out_specs=[pl.BlockSpec((tm, LANE), lambda i: (i, 0))] * 2,
        out_shape=[_sds((s, LANE), F32)] * 2,
        compiler_params=_cp(("arbitrary",)),
    )(pos_f, invf)


def ln_in_fwd(x, g, b):
    s, d = x.shape
    tm = _tile(s, 512)

    def body(x_ref, g_ref, b_ref, o_ref, ob_ref):
        xhat, _ = _ln_stats(x_ref[...])
        y = xhat * g_ref[...] + b_ref[...]
        o_ref[...] = y
        ob_ref[...] = y.astype(BF)

    row = pl.BlockSpec((1, d), lambda i: (0, 0))
    tok = pl.BlockSpec((tm, d), lambda i: (i, 0))
    return _call(
        body, name="ln_in_fwd", grid=(s // tm,), in_specs=[tok, row, row], out_specs=[tok, tok],
        out_shape=[_sds((s, d), F32), _sds((s, d), BF)], compiler_params=_cp(("arbitrary",)),
    )(x, g, b)


def matmul(name, a, w, tm, tn, out_dtype=F32):
    s, k = a.shape
    n = w.shape[1]
    tm = _tile(s, tm)
    tn = _tile(n, tn)

    def body(a_ref, w_ref, o_ref):
        o_ref[...] = _dot(a_ref[...], w_ref[...]).astype(o_ref.dtype)

    return _call(
        body, name=name, grid=(s // tm, n // tn),
        in_specs=[pl.BlockSpec((tm, k), lambda i, j: (i, 0)), pl.BlockSpec((k, tn), lambda i, j: (0, j))],
        out_specs=pl.BlockSpec((tm, tn), lambda i, j: (i, j)),
        out_shape=_sds((s, n), out_dtype), compiler_params=_cp(("arbitrary", "arbitrary")),
    )(a, w)


def matmul_nt(name, a, wt, tm, tn, out_dtype=F32):
    s, k = a.shape
    n = wt.shape[0]
    tm = _tile(s, tm)
    tn = _tile(n, tn)

    def body(a_ref, w_ref, o_ref):
        o_ref[...] = _dot_nt(a_ref[...], w_ref[...]).astype(o_ref.dtype)

    return _call(
        body, name=name, grid=(s // tm, n // tn),
        in_specs=[pl.BlockSpec((tm, k), lambda i, j: (i, 0)), pl.BlockSpec((tn, k), lambda i, j: (j, 0))],
        out_specs=pl.BlockSpec((tm, tn), lambda i, j: (i, j)),
        out_shape=_sds((s, n), out_dtype), compiler_params=_cp(("arbitrary", "arbitrary")),
    )(a, wt)


def q_proj(h, g_cq, wuq, cos, sin):
    s = h.shape[0]
    tm = _tile(s, 512)

    def body(h_ref, g_ref, w_ref, c_ref, s_ref, q_ref, n_ref):
        y, _, _ = _rms_fwd(h_ref[...], g_ref[...])
        yb = y.astype(BF)
        n_ref[...] = yb
        q = _dot(yb, w_ref[...])
        c = c_ref[...]
        sn = s_ref[...]
        for hd in range(HEADS):
            q_ref[hd, :, 0:LANE] = q[:, LANE * hd:LANE * (hd + 1)].astype(BF)
            qr = q[:, MLA_W + LANE * hd:MLA_W + LANE * (hd + 1)]
            q_ref[hd, :, LANE:2 * LANE] = _rope128(qr, c, sn).astype(BF)

    return _call(
        body, name="q_proj", grid=(s // tm,),
        in_specs=[pl.BlockSpec((tm, R_Q), lambda i: (i, 0)), pl.BlockSpec((1, R_Q), lambda i: (0, 0)),
                  pl.BlockSpec((R_Q, 2 * MLA_W), lambda i: (0, 0)),
                  pl.BlockSpec((tm, LANE), lambda i: (i, 0)), pl.BlockSpec((tm, LANE), lambda i: (i, 0))],
        out_specs=[pl.BlockSpec((HEADS, tm, 2 * LANE), lambda i: (0, i, 0)), pl.BlockSpec((tm, R_Q), lambda i: (i, 0))],
        out_shape=[_sds((HEADS, s, 2 * LANE), BF), _sds((s, R_Q), BF)], compiler_params=_cp(("arbitrary",)),
    )(h, g_cq, wuq, cos, sin)


def kv_proj(h, g_ckv, wuk, wuv, cos, sin, kr_blk):
    s = h.shape[0]
    tm = _tile(s, 512)

    def body(h_ref, kr_ref, g_ref, wk_ref, wv_ref, c_ref, s_ref, k_ref, kt_ref, v_ref, n_ref):
        y, _, _ = _rms_fwd(h_ref[...], g_ref[...])
        yb = y.astype(BF)
        n_ref[...] = yb
        kn = _dot(yb, wk_ref[...])
        v = _dot(yb, wv_ref[...])
        kr = _rope128(kr_ref[...], c_ref[...], s_ref[...])
        krb = kr.astype(BF)
        krt = kr.T.astype(BF)
        for hd in range(HEADS):
            knh = kn[:, LANE * hd:LANE * (hd + 1)]
            k_ref[hd, :, 0:LANE] = knh.astype(BF)
            k_ref[hd, :, LANE:2 * LANE] = krb
            kt_ref[hd, 0:LANE, :] = knh.T.astype(BF)
            kt_ref[hd, LANE:2 * LANE, :] = krt
            v_ref[hd] = v[:, LANE * hd:LANE * (hd + 1)].astype(BF)

    tab = pl.BlockSpec((tm, LANE), lambda i: (i, 0))
    wsp = pl.BlockSpec((R_KV, MLA_W), lambda i: (0, 0))
    return _call(
        body, name="kv_proj", grid=(s // tm,),
        in_specs=[pl.BlockSpec((tm, R_KV), lambda i: (i, 1)), pl.BlockSpec((tm, LANE), lambda i: (i, kr_blk)),
                  pl.BlockSpec((1, R_KV), lambda i: (0, 0)), wsp, wsp, tab, tab],
        out_specs=[pl.BlockSpec((HEADS, tm, 2 * LANE), lambda i: (0, i, 0)), pl.BlockSpec((HEADS, 2 * LANE, tm), lambda i: (0, 0, i)),
                   pl.BlockSpec((HEADS, tm, LANE), lambda i: (0, i, 0)), pl.BlockSpec((tm, R_KV), lambda i: (i, 0))],
        out_shape=[_sds((HEADS, s, 2 * LANE), BF), _sds((HEADS, 2 * LANE, s), BF), _sds((HEADS, s, LANE), BF), _sds((s, R_KV), BF)],
        compiler_params=_cp(("arbitrary",)),
    )(h, h, g_ckv, wuk, wuv, cos, sin)


def attn_fwd(qc, kc, v):
    _, s, _ = qc.shape
    tq = _tile(s, 256)
    tk = _tile(s, 512)
    scale = D_QK ** -0.5
    c2 = scale * LOG2E
    nk = s // tk
    nb = tk // LANE
    un = 8

    def body(q_ref, k_ref, v_ref, o_ref, ob_ref, l_ref, s_scr, m_scr):
        q = q_ref[...]

        def scores(j, mpart):
            off = pl.multiple_of(j * tk, tk)
            sc = _dot_nt(q, k_ref[pl.ds(off, tk), :]) * c2
            s_scr[:, pl.ds(off, tk)] = sc
            for b in range(nb):
                mpart = jnp.maximum(mpart, sc[:, LANE * b:LANE * (b + 1)])
            return mpart

        mpart = _unrolled_loop(nk, un, scores, jnp.full((tq, LANE), -jnp.inf, F32))
        m = jnp.max(mpart, axis=-1, keepdims=True)
        m_scr[...] = jnp.broadcast_to(m, (tq, LANE))

        def weigh(j, carry):
            lpart, acc = carry
            off = pl.multiple_of(j * tk, tk)
            ps = []
            for b in range(nb):
                p = jnp.exp2(s_scr[:, pl.ds(off + LANE * b, LANE)] - m_scr[...])
                lpart = lpart + p
                ps.append(p.astype(BF))
            acc = acc + _dot(jnp.concatenate(ps, axis=1), v_ref[pl.ds(off, tk), :])
            return lpart, acc

        lpart, acc = _unrolled_loop(nk, un, weigh, (jnp.zeros((tq, LANE), F32), jnp.zeros((tq, D_V), F32)))
        l = jnp.sum(lpart, axis=-1, keepdims=True)
        o = acc / l
        o_ref[...] = o
        ob_ref[...] = o.astype(BF)
        l_ref[...] = _as_row(m + jnp.log(l) * LOG2E)

    return _call(
        body, name="attn_fwd", grid=(HEADS, s // tq),
        in_specs=[pl.BlockSpec((None, tq, 2 * LANE), lambda h, i: (h, i, 0)),
                  pl.BlockSpec((None, s, 2 * LANE), lambda h, i: (h, 0, 0)),
                  pl.BlockSpec((None, s, LANE), lambda h, i: (h, 0, 0))],
        out_specs=[pl.BlockSpec((tq, LANE), lambda h, i: (i, h)), pl.BlockSpec((tq, LANE), lambda h, i: (i, h)),
                   pl.BlockSpec((None, 1, tq), lambda h, i: (h, 0, i))],
        out_shape=[_sds((s, MLA_W), F32), _sds((s, MLA_W), BF), _sds((HEADS, 1, s), F32)],
        scratch_shapes=[pltpu.VMEM((tq, s + LANE), F32), pltpu.VMEM((tq, LANE), F32)],
        compiler_params=_cp(("arbitrary", "arbitrary")),
    )(qc, kc, v)


def _halo_specs(tm, s, width, col):
    r = tm // HALO
    nb = s // HALO
    cur = pl.BlockSpec((tm, width), lambda i: (i, col))
    prev = pl.BlockSpec((HALO, width), lambda i: (jnp.maximum(i * r - 1, 0), col))
    nxt = pl.BlockSpec((HALO, width), lambda i: (jnp.minimum((i + 1) * r, nb - 1), col))
    return cur, prev, nxt


def _slab_shapes(tm, c):
    return (tm + 2 * HALO, c + LANE), (SUB - 1, tm + 2 * HALO - SUB, c + LANE)


def _fill_slab(slab, tm, prev, cur, nxt):
    i = pl.program_id(0)
    last = pl.num_programs(0) - 1
    c = cur.shape[1]
    slab[0:HALO, 0:c] = jnp.where(i > 0, prev, 0.0)
    slab[HALO:HALO + tm, 0:c] = cur
    slab[HALO + tm:2 * HALO + tm, 0:c] = jnp.where(i < last, nxt, 0.0)


def _rotate_slab(slab, rot, tm):
    rows = tm + 2 * HALO - SUB
    c = slab.shape[1] - LANE
    for b in range(1, SUB):
        rot[b - 1, :, 0:c] = slab[pl.ds(b, rows), 0:c]


def _shifted(slab, rot, start, rc, cs):
    b = start % SUB
    if b == 0:
        return slab[pl.ds(start, rc), cs]
    return rot[b - 1, pl.ds(start - b, rc), cs]


def conv_fwd(h, conv_w, conv_b, g_ln, b_ln):
    s = h.shape[0]
    c = conv_w.shape[1]
    tm = _tile(s, 256)
    rc = _tile(tm, 64)

    def body(a_ref, ap_ref, an_ref, g_ref, gp_ref, gn_ref, w_ref, cb_ref, lg_ref, lb_ref, co_ref, uc_ref, slab, rot):
        _fill_slab(slab, tm, ap_ref[...] * _sigmoid(gp_ref[...]), a_ref[...] * _sigmoid(g_ref[...]),
                   an_ref[...] * _sigmoid(gn_ref[...]))
        _rotate_slab(slab, rot, tm)

        def lane_block(cb, carry):
            cs = pl.ds(pl.multiple_of(cb * LANE, LANE), LANE)
            for r0 in range(0, tm, rc):
                acc = jnp.zeros((rc, LANE), F32)
                for k in range(CONV_K):
                    acc = acc + w_ref[k:k + 1, cs] * _shifted(slab, rot, r0 + HALO - CONV_PAD + k, rc, cs)
                uc_ref[r0:r0 + rc, cs] = acc + cb_ref[:, cs]
            return carry

        lax.fori_loop(0, c // LANE, lane_block, 0)
        xhat, _ = _ln_stats(uc_ref[...])
        cl = xhat * lg_ref[...] + lb_ref[...]
        co_ref[...] = (cl * _sigmoid(cl)).astype(BF)

    a_specs = _halo_specs(tm, s, c, 1)
    g_specs = _halo_specs(tm, s, c, 2)
    row = pl.BlockSpec((1, c), lambda i: (0, 0))
    tok = pl.BlockSpec((tm, c), lambda i: (i, 0))
    return _call(
        body, name="conv_fwd", grid=(s // tm,),
        in_specs=[*a_specs, *g_specs, pl.BlockSpec(conv_w.shape, lambda i: (0, 0)), row, row, row],
        out_specs=[tok, tok], out_shape=[_sds((s, c), BF), _sds((s, c), F32)],
        scratch_shapes=[pltpu.VMEM(shp, F32) for shp in _slab_shapes(tm, c)],
        compiler_params=_cp(("arbitrary",)),
    )(h, h, h, h, h, h, conv_w, conv_b, g_ln, b_ln)


def out_proj_ln1(ob, co, wout, x0, g1, b1):
    s, d = x0.shape
    kh = ob.shape[1]
    tm = _tile(s, 256)

    def body(o_ref, c_ref, w_ref, x_ref, g_ref, b_ref, r_ref, x1_ref, x1b_ref, acc):
        k = pl.program_id(1)

        @pl.when(k == 0)
        def _():
            acc[...] = _dot(o_ref[...], w_ref[...])

        @pl.when(k == 1)
        def _():
            r = ALPHA * x_ref[...] + (acc[...] + _dot(c_ref[...], w_ref[...]))
            r_ref[...] = r
            xhat, _ = _ln_stats(r)
            y = xhat * g_ref[...] + b_ref[...]
            x1_ref[...] = y
            x1b_ref[...] = y.astype(BF)

    half = pl.BlockSpec((tm, kh), lambda i, k: (i, 0))
    tok = pl.BlockSpec((tm, d), lambda i, k: (i, 0))
    row = pl.BlockSpec((1, d), lambda i, k: (0, 0))
    return _call(
        body, name="out_proj_ln1", grid=(s // tm, 2),
        in_specs=[half, half, pl.BlockSpec((kh, d), lambda i, k: (k, 0)), tok, row, row],
        out_specs=[tok, tok, tok], out_shape=[_sds((s, d), F32), _sds((s, d), F32), _sds((s, d), BF)],
        scratch_shapes=[pltpu.VMEM((tm, d), F32)], compiler_params=_cp(("arbitrary", "arbitrary")),
    )(ob, co, wout, x0, g1, b1)


def ff1_fwd(x1b, wff1_g):
    s, d = x1b.shape
    nsh, _, fs = wff1_g.shape
    tm = _tile(s, 1024)
    tn = _tile(fs, 1024)
    per = fs // tn

    def body(a_ref, w_ref, r_ref, a1_ref):
        r = jnp.maximum(_dot(a_ref[...], w_ref[...]), 0.0)
        r_ref[...] = r.astype(BF)
        a1_ref[...] = (r * r).astype(BF)

    out = pl.BlockSpec((tm, tn), lambda i, j: (i, j))
    return _call(
        body, name="ff1_fwd", grid=(s // tm, nsh * per),
        in_specs=[pl.BlockSpec((tm, d), lambda i, j: (i, 0)),
                  pl.BlockSpec((None, d, tn), lambda i, j: (j // per, 0, j % per))],
        out_specs=[out, out], out_shape=[_sds((s, nsh * fs), BF)] * 2,
        compiler_params=_cp(("arbitrary", "arbitrary")),
    )(x1b, wff1_g)


def ff2_ln2_loss(a1b, wff2, x1, target, g2, b2):
    s, f = a1b.shape
    d = x1.shape[1]
    tm = _tile(s, 512)
    tk = _tile(f, 1024)
    nk = f // tk

    def body(a_ref, w_ref, x_ref, t_ref, g_ref, b_ref, dr_ref, drb_ref, loss_ref, dg_ref, db_ref, acc):
        i = pl.program_id(0)
        k = pl.program_id(1)

        @pl.when(k == 0)
        def _():
            acc[...] = _dot(a_ref[...], w_ref[...])

        @pl.when(k > 0)
        def _():
            acc[...] += _dot(a_ref[...], w_ref[...])

        @pl.when(jnp.logical_and(i == 0, k == 0))
        def _():
            loss_ref[...] = jnp.zeros_like(loss_ref)
            dg_ref[...] = jnp.zeros_like(dg_ref)
            db_ref[...] = jnp.zeros_like(db_ref)

        @pl.when(k == nk - 1)
        def _():
            g = g_ref[...]

            def chunk(rows):
                r = ALPHA * x_ref[rows, :] + acc[rows, :]
                xhat, rstd = _ln_stats(r)
                e = xhat * g + b_ref[...] - t_ref[rows, :]
                e2 = _rows8(e * e)
                part = e2[:, 0:LANE]
                for c in range(1, d // LANE):
                    part = part + e2[:, LANE * c:LANE * (c + 1)]
                loss_ref[...] += part * (0.5 / d)
                dy = e * (1.0 / d)
                dg_ref[...] += _rows8(dy * xhat)
                db_ref[...] += _rows8(dy)
                dr = _ln_bwd(dy, xhat, rstd, g)
                dr_ref[rows, :] = dr
                drb_ref[rows, :] = dr.astype(BF)

            _row_chunks(tm, chunk)

    tok = pl.BlockSpec((tm, d), lambda i, k: (i, 0))
    row = pl.BlockSpec((1, d), lambda i, k: (0, 0))
    accs = pl.BlockSpec((SUB, d), lambda i, k: (0, 0))
    return _call(
        body, name="ff2_ln2_loss", grid=(s // tm, nk),
        in_specs=[pl.BlockSpec((tm, tk), lambda i, k: (i, k)), pl.BlockSpec((tk, d), lambda i, k: (k, 0)),
                  tok, tok, row, row],
        out_specs=[tok, tok, pl.BlockSpec((SUB, LANE), lambda i, k: (0, 0)), accs, accs],
        out_shape=[_sds((s, d), F32), _sds((s, d), BF), _sds((SUB, LANE), F32), _sds((SUB, d), F32), _sds((SUB, d), F32)],
        scratch_shapes=[pltpu.VMEM((tm, d), F32)], compiler_params=_cp(("arbitrary", "arbitrary"), 56),
    )(a1b, wff2, x1, target, g2, b2)


def ff2_bwd_act(dr2b, wff2, rb):
    s, d = dr2b.shape
    f = wff2.shape[0]
    tm = _tile(s, 1024)
    tn = _tile(f, 1024)

    def body(a_ref, w_ref, r_ref, o_ref):
        o_ref[...] = (_dot_nt(a_ref[...], w_ref[...]) * (2.0 * r_ref[...].astype(F32))).astype(BF)

    return _call(
        body, name="ff2_bwd_act", grid=(s // tm, f // tn),
        in_specs=[pl.BlockSpec((tm, d), lambda i, j: (i, 0)), pl.BlockSpec((tn, d), lambda i, j: (j, 0)),
                  pl.BlockSpec((tm, tn), lambda i, j: (i, j))],
        out_specs=pl.BlockSpec((tm, tn), lambda i, j: (i, j)), out_shape=_sds((s, f), BF),
        compiler_params=_cp(("arbitrary", "arbitrary")),
    )(dr2b, wff2, rb)


def wgrad(name, a, b, tm, tn, tk=2048, shards=1):
    s, m = a.shape
    n = b.shape[1]
    tm = _tile(m, tm)
    ns = n // shards
    tn = _tile(ns, tn)
    tk = _tile(s, tk)
    per = ns // tn

    def body(a_ref, b_ref, o_ref):
        k = pl.program_id(2)

        @pl.when(k == 0)
        def _():
            o_ref[...] = _dot_tn(a_ref[...], b_ref[...])

        @pl.when(k > 0)
        def _():
            o_ref[...] += _dot_tn(a_ref[...], b_ref[...])

    return _call(
        body, name=name, grid=(m // tm, n // tn, s // tk),
        in_specs=[pl.BlockSpec((tk, tm), lambda i, j, k: (k, i)), pl.BlockSpec((tk, tn), lambda i, j, k: (k, j))],
        out_specs=pl.BlockSpec((None, tm, tn), lambda i, j, k: (j // per, i, j % per)),
        out_shape=_sds((shards, m, ns), F32), compiler_params=_cp(("arbitrary", "arbitrary", "arbitrary")),
    )(a, b)


def ff1_bwd_ln1(df1b, wff1_g, dr2, r1, g1):
    s, f = df1b.shape
    d = dr2.shape[1]
    tm = _tile(s, 512)
    tk = _tile(wff1_g.shape[2], 1024)
    per = wff1_g.shape[2] // tk
    nk = f // tk

    def body(a_ref, w_ref, d2_ref, r_ref, g_ref, dr_ref, drb_ref, dg_ref, db_ref, acc):
        i = pl.program_id(0)
        k = pl.program_id(1)

        @pl.when(k == 0)
        def _():
            acc[...] = _dot_nt(a_ref[...], w_ref[...])

        @pl.when(k > 0)
        def _():
            acc[...] += _dot_nt(a_ref[...], w_ref[...])

        @pl.when(jnp.logical_and(i == 0, k == 0))
        def _():
            dg_ref[...] = jnp.zeros_like(dg_ref)
            db_ref[...] = jnp.zeros_like(db_ref)

        @pl.when(k == nk - 1)
        def _():
            g = g_ref[...]

            def chunk(rows):
                dy = ALPHA * d2_ref[rows, :] + acc[rows, :]
                xhat, rstd = _ln_stats(r_ref[rows, :])
                dg_ref[...] += _rows8(dy * xhat)
                db_ref[...] += _rows8(dy)
                dr = _ln_bwd(dy, xhat, rstd, g)
                dr_ref[rows, :] = dr
                drb_ref[rows, :] = dr.astype(BF)

            _row_chunks(tm, chunk)

    tok = pl.BlockSpec((tm, d), lambda i, k: (i, 0))
    accs = pl.BlockSpec((SUB, d), lambda i, k: (0, 0))
    return _call(
        body, name="ff1_bwd_ln1", grid=(s // tm, nk),
        in_specs=[pl.BlockSpec((tm, tk), lambda i, k: (i, k)), pl.BlockSpec((None, d, tk), lambda i, k: (k // per, 0, k % per)),
                  tok, tok, pl.BlockSpec((1, d), lambda i, k: (0, 0))],
        out_specs=[tok, tok, accs, accs],
        out_shape=[_sds((s, d), F32), _sds((s, d), BF), _sds((SUB, d), F32), _sds((SUB, d), F32)],
        scratch_shapes=[pltpu.VMEM((tm, d), F32)], compiler_params=_cp(("arbitrary", "arbitrary"), 56),
    )(df1b, wff1_g, dr2, r1, g1)


def out_proj_bwd(dr1b, woutt, o):
    s, d = dr1b.shape
    tm = _tile(s, 256)

    def body(a_ref, w_ref, o_ref, do_ref, dot_ref, dc_ref, dl_ref):
        dcat = _dot(a_ref[...], w_ref[...])
        do = dcat[:, 0:MLA_W]
        do_ref[...] = do.astype(BF)
        dc_ref[...] = dcat[:, MLA_W:]
        prod = do * o_ref[...]
        for hd in range(HEADS):
            hs = slice(LANE * hd, LANE * (hd + 1))
            dl_ref[hd] = _as_row(jnp.sum(prod[:, hs], axis=-1, keepdims=True))
            dot_ref[hd] = do[:, hs].T.astype(BF)

    half = pl.BlockSpec((tm, MLA_W), lambda i: (i, 0))
    return _call(
        body, name="out_proj_bwd", grid=(s // tm,),
        in_specs=[pl.BlockSpec((tm, d), lambda i: (i, 0)), pl.BlockSpec((d, d), lambda i: (0, 0)), half],
        out_specs=[half, pl.BlockSpec((HEADS, LANE, tm), lambda i: (0, 0, i)),
                   pl.BlockSpec((tm, d - MLA_W), lambda i: (i, 0)), pl.BlockSpec((HEADS, 1, tm), lambda i: (0, 0, i))],
        out_shape=[_sds((s, MLA_W), BF), _sds((HEADS, LANE, s), BF), _sds((s, d - MLA_W), F32), _sds((HEADS, 1, s), F32)],
        compiler_params=_cp(("arbitrary",)),
    )(dr1b, woutt, o)


def conv_bwd_ln(uc, dco, g_ln, b_ln):
    s, c = uc.shape
    tm = _tile(s, 512)

    def body(u_ref, d_ref, g_ref, b_ref, du_ref, dg_ref, db_ref, dcb_ref):
        @pl.when(pl.program_id(0) == 0)
        def _():
            dg_ref[...] = jnp.zeros_like(dg_ref)
            db_ref[...] = jnp.zeros_like(db_ref)
            dcb_ref[...] = jnp.zeros_like(dcb_ref)

        xhat, rstd = _ln_stats(u_ref[...])
        g = g_ref[...]
        cl = xhat * g + b_ref[...]
        sg = _sigmoid(cl)
        dcl = d_ref[...] * (sg * (1.0 + cl * (1.0 - sg)))
        dg_ref[...] += _rows8(dcl * xhat)
        db_ref[...] += _rows8(dcl)
        du = _ln_bwd(dcl, xhat, rstd, g)
        du_ref[...] = du
        dcb_ref[...] += _rows8(du)

    tok = pl.BlockSpec((tm, c), lambda i: (i, 0))
    row = pl.BlockSpec((1, c), lambda i: (0, 0))
    accs = pl.BlockSpec((SUB, c), lambda i: (0, 0))
    return _call(
        body, name="conv_bwd_ln", grid=(s // tm,), in_specs=[tok, tok, row, row], out_specs=[tok, accs, accs, accs],
        out_shape=[_sds((s, c), F32)] + [_sds((SUB, c), F32)] * 3, compiler_params=_cp(("arbitrary",)),
    )(uc, dco, g_ln, b_ln)


def conv_bwd_taps(h, duc, conv_w):
    s, c = duc.shape
    tm = _tile(s, 256)
    rc = _tile(tm, 64)

    def body(a_ref, ap_ref, an_ref, g_ref, gp_ref, gn_ref, d_ref, dp_ref, dn_ref, w_ref, o_ref, dw_ref,
             uslab, dslab, du_s, urot, drot, dw8):
        @pl.when(pl.program_id(0) == 0)
        def _():
            dw8[...] = jnp.zeros_like(dw8)

        sg = _sigmoid(g_ref[...])
        a = a_ref[...]
        _fill_slab(uslab, tm, ap_ref[...] * _sigmoid(gp_ref[...]), a * sg, an_ref[...] * _sigmoid(gn_ref[...]))
        _fill_slab(dslab, tm, dp_ref[...], d_ref[...], dn_ref[...])
        _rotate_slab(uslab, urot, tm)
        _rotate_slab(dslab, drot, tm)

        def lane_block(cb, carry):
            cs = pl.ds(pl.multiple_of(cb * LANE, LANE), LANE)
            for r0 in range(0, tm, rc):
                acc = jnp.zeros((rc, LANE), F32)
                for k in range(CONV_K):
                    acc = acc + w_ref[k:k + 1, cs] * _shifted(dslab, drot, r0 + HALO + CONV_PAD - k, rc, cs)
                du_s[r0:r0 + rc, cs] = acc
            return carry

        def lane_block_taps(cb, carry):
            cs = pl.ds(pl.multiple_of(cb * LANE, LANE), LANE)
            parts = []
            for k in range(CONV_K):
                prod = None
                for r0 in range(0, tm, rc):
                    t = dslab[pl.ds(r0 + HALO, rc), cs] * _shifted(uslab, urot, r0 + HALO - CONV_PAD + k, rc, cs)
                    prod = t if prod is None else prod + t
                parts.append(_rows8(prod))
            rows = SUB * CONV_K
            dw8[0:rows, cs] = dw8[0:rows, cs] + jnp.concatenate(parts, axis=0)
            return carry

        lax.fori_loop(0, c // LANE, lane_block, 0)
        lax.fori_loop(0, c // LANE, lane_block_taps, 0)

        @pl.when(pl.program_id(0) == pl.num_programs(0) - 1)
        def _():
            dw_ref[...] = jnp.zeros_like(dw_ref)
            for k in range(CONV_K):
                dw_ref[k:k + 1, :] = jnp.sum(dw8[SUB * k:SUB * (k + 1), :], axis=0, keepdims=True)

        du = du_s[...]
        o_ref[:, 0:c] = (du * sg).astype(BF)
        o_ref[:, c:2 * c] = (du * a * sg * (1.0 - sg)).astype(BF)

    a_specs = _halo_specs(tm, s, c, 1)
    g_specs = _halo_specs(tm, s, c, 2)
    d_specs = _halo_specs(tm, s, c, 0)
    wsp = pl.BlockSpec(conv_w.shape, lambda i: (0, 0))
    return _call(
        body, name="conv_bwd_taps", grid=(s // tm,), in_specs=[*a_specs, *g_specs, *d_specs, wsp],
        out_specs=[pl.BlockSpec((tm, 2 * c), lambda i: (i, 0)), wsp],
        out_shape=[_sds((s, 2 * c), BF), _sds(conv_w.shape, F32)],
        scratch_shapes=[pltpu.VMEM(_slab_shapes(tm, c)[0], F32), pltpu.VMEM(_slab_shapes(tm, c)[0], F32), pltpu.VMEM((tm, c), F32),
                        pltpu.VMEM(_slab_shapes(tm, c)[1], F32), pltpu.VMEM(_slab_shapes(tm, c)[1], F32),
                        pltpu.VMEM((SUB * conv_w.shape[0], c), F32)],
        compiler_params=_cp(("arbitrary",)),
    )(h, h, h, h, h, h, duc, duc, duc, conv_w)


def attn_bwd(qc, kc, kct, v, dob, dot, lse_r, delta_r):
    _, s, _ = qc.shape
    tk = _tile(s, 512)
    tq = _tile(s, 512)
    scale = D_QK ** -0.5
    c2 = scale * LOG2E

    def body(k_ref, kt_ref, v_ref, q_ref, do_ref, dot_ref, l_ref, dl_ref, dqt_ref, dk_ref, dvt_ref):
        @pl.when(pl.program_id(1) == 0)
        def _():
            dqt_ref[...] = jnp.zeros_like(dqt_ref)

        k = k_ref[...]
        kt = kt_ref[...]
        vv = v_ref[...]

        def step(i, carry):
            dk, dvt = carry
            off = pl.multiple_of(i * tq, tq)
            q = q_ref[pl.ds(off, tq), :]
            do = do_ref[pl.ds(off, tq), :]
            pt = jnp.exp2(_dot_nt(k, q) * c2 - l_ref[:, pl.ds(off, tq)])
            dvt = dvt + _dot_nt(dot_ref[:, pl.ds(off, tq)], pt.astype(BF))
            dpt = _dot_nt(vv, do)
            dsb = (pt * (dpt - dl_ref[:, pl.ds(off, tq)]) * scale).astype(BF)
            dk = dk + _dot(dsb, q)
            dqt_ref[:, pl.ds(off, tq)] += _dot(kt, dsb)
            return dk, dvt

        dk, dvt = _unrolled_loop(s // tq, 4, step, (jnp.zeros((tk, 2 * LANE), F32), jnp.zeros((LANE, tk), F32)))
        dk_ref[...] = dk
        dvt_ref[...] = dvt

    rowv = pl.BlockSpec((None, 1, s), lambda h, j: (h, 0, 0))
    return _call(
        body, name="attn_bwd", grid=(HEADS, s // tk),
        in_specs=[pl.BlockSpec((None, tk, 2 * LANE), lambda h, j: (h, j, 0)),
                  pl.BlockSpec((None, 2 * LANE, tk), lambda h, j: (h, 0, j)),
                  pl.BlockSpec((None, tk, LANE), lambda h, j: (h, j, 0)),
                  pl.BlockSpec((None, s, 2 * LANE), lambda h, j: (h, 0, 0)),
                  pl.BlockSpec((s, LANE), lambda h, j: (0, h)),
                  pl.BlockSpec((None, LANE, s), lambda h, j: (h, 0, 0)), rowv, rowv],
        out_specs=[pl.BlockSpec((None, 2 * LANE, s), lambda h, j: (h, 0, 0)),
                   pl.BlockSpec((None, tk, 2 * LANE), lambda h, j: (h, j, 0)),
                   pl.BlockSpec((None, LANE, tk), lambda h, j: (h, 0, j))],
        out_shape=[_sds((HEADS, 2 * LANE, s), F32), _sds((HEADS, s, 2 * LANE), F32), _sds((HEADS, LANE, s), F32)],
        compiler_params=_cp(("arbitrary", "arbitrary"), 56),
    )(kc, kct, v, qc, dob, dot, lse_r, delta_r)


def q_bwd(dqt, h, g_cq, wuqt, cos, sin):
    s = h.shape[0]
    tm = _tile(s, 256)

    def body(d_ref, h_ref, g_ref, w_ref, c_ref, s_ref, dq_ref, dc_ref, dg_ref):
        @pl.when(pl.program_id(0) == 0)
        def _():
            dg_ref[...] = jnp.zeros_like(dg_ref)

        c = c_ref[...]
        sn = s_ref[...]
        for hd in range(HEADS):
            t = d_ref[hd].T
            dq_ref[:, LANE * hd:LANE * (hd + 1)] = t[:, 0:LANE].astype(BF)
            dq_ref[:, MLA_W + LANE * hd:MLA_W + LANE * (hd + 1)] = _unrope128(t[:, LANE:2 * LANE], c, sn).astype(BF)
        dy = _dot(dq_ref[...], w_ref[...])
        g = g_ref[...]
        _, xh, rr = _rms_fwd(h_ref[...], g)
        dg_ref[...] += _rows8(dy * xh)
        dc_ref[...] = _rms_bwd(dy, xh, rr, g).astype(BF)

    tab = pl.BlockSpec((tm, LANE), lambda i: (i, 0))
    return _call(
        body, name="q_bwd", grid=(s // tm,),
        in_specs=[pl.BlockSpec((HEADS, 2 * LANE, tm), lambda i: (0, 0, i)), pl.BlockSpec((tm, R_Q), lambda i: (i, 0)),
                  pl.BlockSpec((1, R_Q), lambda i: (0, 0)), pl.BlockSpec((2 * MLA_W, R_Q), lambda i: (0, 0)), tab, tab],
        out_specs=[pl.BlockSpec((tm, 2 * MLA_W), lambda i: (i, 0)), pl.BlockSpec((tm, R_Q), lambda i: (i, 0)),
                   pl.BlockSpec((SUB, R_Q), lambda i: (0, 0))],
        out_shape=[_sds((s, 2 * MLA_W), BF), _sds((s, R_Q), BF), _sds((SUB, R_Q), F32)],
        compiler_params=_cp(("arbitrary",)),
    )(dqt, h, g_cq, wuqt, cos, sin)


def kv_bwd(dk, dv, h, g_ckv, wukt, wuvt, cos, sin):
    s = h.shape[0]
    tm = _tile(s, 256)

    def body(dk_ref, dv_ref, h_ref, g_ref, wk_ref, wv_ref, c_ref, s_ref, dkn_ref, dvb_ref, dc_ref, dkr_ref, dg_ref):
        @pl.when(pl.program_id(0) == 0)
        def _():
            dg_ref[...] = jnp.zeros_like(dg_ref)

        dkr = dk_ref[0, :, LANE:2 * LANE]
        for hd in range(HEADS):
            dkn_ref[:, LANE * hd:LANE * (hd + 1)] = dk_ref[hd, :, 0:LANE].astype(BF)
            dvb_ref[:, LANE * hd:LANE * (hd + 1)] = dv_ref[hd].T.astype(BF)
            if hd > 0:
                dkr = dkr + dk_ref[hd, :, LANE:2 * LANE]
        dkr_ref[...] = _unrope128(dkr, c_ref[...], s_ref[...]).astype(BF)
        dy = _dot(dkn_ref[...], wk_ref[...]) + _dot(dvb_ref[...], wv_ref[...])
        g = g_ref[...]
        _, xh, rr = _rms_fwd(h_ref[...], g)
        dg_ref[...] += _rows8(dy * xh)
        dc_ref[...] = _rms_bwd(dy, xh, rr, g).astype(BF)

    tab = pl.BlockSpec((tm, LANE), lambda i: (i, 0))
    wsp = pl.BlockSpec((MLA_W, R_KV), lambda i: (0, 0))
    wide = pl.BlockSpec((tm, MLA_W), lambda i: (i, 0))
    return _call(
        body, name="kv_bwd", grid=(s // tm,),
        in_specs=[pl.BlockSpec((HEADS, tm, 2 * LANE), lambda i: (0, i, 0)), pl.BlockSpec((HEADS, LANE, tm), lambda i: (0, 0, i)),
                  pl.BlockSpec((tm, R_KV), lambda i: (i, 1)), pl.BlockSpec((1, R_KV), lambda i: (0, 0)), wsp, wsp, tab, tab],
        out_specs=[wide, wide, pl.BlockSpec((tm, R_KV), lambda i: (i, 0)), tab, pl.BlockSpec((SUB, R_KV), lambda i: (0, 0))],
        out_shape=[_sds((s, MLA_W), BF), _sds((s, MLA_W), BF), _sds((s, R_KV), BF), _sds((s, LANE), BF), _sds((SUB, R_KV), F32)],
        compiler_params=_cp(("arbitrary",)),
    )(dk, dv, h, g_ckv, wukt, wuvt, cos, sin)


def in_proj_bwd_ln(dh, wint, dr1, x, g_in):
    s, hc = dh.shape
    d = x.shape[1]
    tm = _tile(s, 512)
    tk = _tile(hc, 640)
    nk = hc // tk

    def body(a_ref, w_ref, d1_ref, x_ref, g_ref, gx_ref, dg_ref, db_ref, acc):
        i = pl.program_id(0)
        k = pl.program_id(1)

        @pl.when(k == 0)
        def _():
            acc[...] = _dot(a_ref[...], w_ref[...])

        @pl.when(k > 0)
        def _():
            acc[...] += _dot(a_ref[...], w_ref[...])

        @pl.when(jnp.logical_and(i == 0, k == 0))
        def _():
            dg_ref[...] = jnp.zeros_like(dg_ref)
            db_ref[...] = jnp.zeros_like(db_ref)

        @pl.when(k == nk - 1)
        def _():
            g = g_ref[...]

            def chunk(rows):
                dy = ALPHA * d1_ref[rows, :] + acc[rows, :]
                xhat, rstd = _ln_stats(x_ref[rows, :])
                dg_ref[...] += _rows8(dy * xhat)
                db_ref[...] += _rows8(dy)
                gx_ref[rows, :] = _ln_bwd(dy, xhat, rstd, g)

            _row_chunks(tm, chunk)

    tok = pl.BlockSpec((tm, d), lambda i, k: (i, 0))
    accs = pl.BlockSpec((SUB, d), lambda i, k: (0, 0))
    return _call(
        body, name="in_proj_bwd_ln", grid=(s // tm, nk),
        in_specs=[pl.BlockSpec((tm, tk), lambda i, k: (i, k)), pl.BlockSpec((tk, d), lambda i, k: (k, 0)),
                  tok, tok, pl.BlockSpec((1, d), lambda i, k: (0, 0))],
        out_specs=[tok, accs, accs], out_shape=[_sds((s, d), F32), _sds((SUB, d), F32), _sds((SUB, d), F32)],
        scratch_shapes=[pltpu.VMEM((tm, d), F32)], compiler_params=_cp(("arbitrary", "arbitrary")),
    )(dh, wint, dr1, x, g_in)


def _adamw_math(w, g, m, v):
    m = ADAM_B1 * m + (1.0 - ADAM_B1) * g
    v = ADAM_B2 * v + (1.0 - ADAM_B2) * (g * g)
    m_hat = m / (1.0 - ADAM_B1 ** ADAM_STEP)
    v_hat = v / (1.0 - ADAM_B2 ** ADAM_STEP)
    delta = -ADAM_LR * (m_hat / (jnp.sqrt(v_hat) + ADAM_EPS) + ADAM_WD * w)
    return delta, m, v


def adamw(name, w, g, m, v):
    r, c = w.shape
    tr = _row_tile(r, c)

    def body(w_ref, g_ref, m_ref, v_ref, d_ref, mo_ref, vo_ref):
        d_ref[...], mo_ref[...], vo_ref[...] = _adamw_math(w_ref[...], g_ref[...], m_ref[...], v_ref[...])

    blk = pl.BlockSpec((tr, c), lambda i: (i, 0))
    return _call(
        body, name=name, grid=(r // tr,), in_specs=[blk] * 4, out_specs=[blk] * 3,
        out_shape=[_sds((r, c), F32)] * 3, compiler_params=_cp(("arbitrary",)),
    )(w, g, m, v)


def _coords():
    return lax.axis_index("x"), lax.axis_index("y"), lax.axis_index("c")


def _other_chips(x, y):
    return [(1 - x, y, 2 * (1 - x) + y), (x, 1 - y, 2 * x + 1 - y), (1 - x, 1 - y, 2 * (1 - x) + 1 - y)]


ANY = pl.BlockSpec(memory_space=pl.ANY)
HBM = pl.BlockSpec(memory_space=pltpu.HBM)
SEM = pl.BlockSpec(memory_space=pltpu.SEMAPHORE)
EFFECT = pltpu.SideEffectType.DATAFLOW_SIDE_EFFECTING


def _in_hbm(a):
    return pltpu.with_memory_space_constraint(a, pltpu.HBM)


def _split_plan(mode, src, land, x, y, c):
    if mode == "pair":
        rh = src.shape[1] // 2
        return [((x, y, 1 - c), src.at[:, pl.ds((1 - c) * rh, rh)], land, land)]
    me = 2 * x + y
    plan = []
    for j, (px, py, pk) in enumerate(_other_chips(x, y)):
        if mode == "gather":
            plan.append(((px, py, c), src, land.at[me], land.at[pk]))
        else:
            plan.append(((px, py, c), src.at[pk], land.at[j], land.at[j]))
    return plan


def _plan_len(mode):
    return 1 if mode == "pair" else N_CHIP - 1


def split_send_start(name, mode, srcs, land_shapes, order_after):
    n = len(srcs)
    np_ = _plan_len(mode)

    def body(*refs):
        ins, lands = refs[:n], refs[n:2 * n]
        ss, rs = refs[2 * n + 1], refs[2 * n + 2]
        token = refs[-1]
        x, y, c = _coords()
        for a in range(n):
            for j, (peer, src, dst, _) in enumerate(_split_plan(mode, ins[a], lands[a], x, y, c)):
                pltpu.make_async_remote_copy(src_ref=src, dst_ref=dst, send_sem=ss.at[np_ * a + j], recv_sem=rs.at[np_ * a + j],
                                             device_id=peer, device_id_type=MESH).start()
        token[...] = jnp.zeros_like(token)

    lands = [lax.empty(shp, s.dtype) for shp, s in zip(land_shapes, srcs)]
    outs = _call(
        body, name=name,
        out_shape=(pltpu.SemaphoreType.DMA((np_ * n,)), pltpu.SemaphoreType.DMA((np_ * n,)),
                   *[pltpu.HBM(s.shape, s.dtype) for s in srcs], *[pltpu.HBM(l.shape, l.dtype) for l in lands],
                   _sds((SUB, LANE), F32)),
        in_specs=[HBM] * (2 * n) + [ANY], out_specs=(SEM, SEM, *[HBM] * (2 * n), pl.BlockSpec(memory_space=pltpu.VMEM)),
        input_output_aliases={a: 2 + a for a in range(2 * n)},
        compiler_params=pltpu.CompilerParams(has_side_effects=EFFECT),
    )(*[_in_hbm(s) for s in srcs], *[_in_hbm(l) for l in lands], order_after)
    return outs[0], outs[1], list(outs[2:2 + n]), list(outs[2 + n:2 + 2 * n]), outs[-1]


def split_send_wait(name, mode, ss, rs, srcs, lands, order_after):
    n = len(srcs)
    np_ = _plan_len(mode)

    def body(*refs):
        ins, lnd = refs[:n], refs[n:2 * n]
        s_ref, r_ref = refs[2 * n], refs[2 * n + 1]
        x, y, c = _coords()
        for a in range(n):
            for j, (peer, src, _, got) in enumerate(_split_plan(mode, ins[a], lnd[a], x, y, c)):
                cp = pltpu.make_async_remote_copy(src_ref=src, dst_ref=got, send_sem=s_ref.at[np_ * a + j], recv_sem=r_ref.at[np_ * a + j],
                                                  device_id=peer, device_id_type=MESH)
                cp.wait_send()
                cp.wait_recv()

    outs = _call(
        body, name=name, out_shape=tuple(pltpu.HBM(t.shape, t.dtype) for t in (*srcs, *lands)),
        in_specs=[HBM] * (2 * n) + [SEM, SEM, ANY], out_specs=tuple([HBM] * (2 * n)),
        input_output_aliases={a: a for a in range(2 * n)},
        compiler_params=pltpu.CompilerParams(has_side_effects=EFFECT),
    )(*srcs, *lands, ss, rs, order_after)
    return list(outs[:n]), list(outs[n:])


def all_gather_shards(shards):
    n = len(shards)

    def body(*refs):
        ins, outs = refs[:n], refs[n:2 * n]
        ici_s, ici_r, d2d_s, d2d_r = refs[2 * n:]
        x, y, c = _coords()
        me = 2 * x + y
        peers = _other_chips(x, y)
        sends, fwds = [], []
        for a in range(n):
            rh = ins[a].shape[0] // 2
            mine = pl.ds(c * rh, rh)
            for j, (px, py, pk) in enumerate(peers):
                cp = pltpu.make_async_remote_copy(
                    src_ref=ins[a].at[mine], dst_ref=outs[a].at[me, mine], send_sem=ici_s.at[a, j], recv_sem=ici_r.at[a, j],
                    device_id=(px, py, c), device_id_type=MESH)
                cp.start()
                sends.append(cp)
        for a in range(n):
            rh = ins[a].shape[0] // 2
            mine = pl.ds(c * rh, rh)
            for j, (px, py, pk) in enumerate(peers):
                got = outs[a].at[pk, mine]
                pltpu.make_async_remote_copy(
                    src_ref=got, dst_ref=got, send_sem=ici_s.at[a, j], recv_sem=ici_r.at[a, j],
                    device_id=(px, py, c), device_id_type=MESH).wait_recv()
                fw = pltpu.make_async_remote_copy(
                    src_ref=got, dst_ref=got, send_sem=d2d_s.at[a, j], recv_sem=d2d_r.at[a, j],
                    device_id=(x, y, 1 - c), device_id_type=MESH)
                fw.start()
                fwds.append(fw)
        for a in range(n):
            rh = ins[a].shape[0] // 2
            theirs = pl.ds((1 - c) * rh, rh)
            for j, (px, py, pk) in enumerate(peers):
                got = outs[a].at[pk, theirs]
                pltpu.make_async_remote_copy(
                    src_ref=got, dst_ref=got, send_sem=d2d_s.at[a, j], recv_sem=d2d_r.at[a, j],
                    device_id=(x, y, 1 - c), device_id_type=MESH).wait_recv()
        for cp in sends + fwds:
            cp.wait_send()

    got = _call(
        body, name="all_gather_shards", in_specs=[ANY] * n, out_specs=[ANY] * n,
        out_shape=[_sds((N_CHIP,) + w.shape, w.dtype) for w in shards],
        scratch_shapes=[pltpu.SemaphoreType.DMA((n, 3))] * 4,
    )(*shards)
    me = 2 * lax.axis_index("x") + lax.axis_index("y")
    return [lax.dynamic_update_slice(g, w[None], (me, 0, 0)) for g, w in zip(got, shards)]


def pair_exchange(grads, tag):
    n = len(grads)

    def body(*refs):
        ins, outs = refs[:n], refs[n:2 * n]
        ss, rs = refs[2 * n:]
        x, y, c = _coords()
        cps = []
        for a in range(n):
            rh = ins[a].shape[1] // 2
            cp = pltpu.make_async_remote_copy(
                src_ref=ins[a].at[:, pl.ds((1 - c) * rh, rh)], dst_ref=outs[a], send_sem=ss.at[a], recv_sem=rs.at[a],
                device_id=(x, y, 1 - c), device_id_type=MESH)
            cp.start()
            cps.append(cp)
        for cp in cps:
            cp.wait()

    return _call(
        body, name="pair_exchange_" + tag, in_specs=[ANY] * n, out_specs=[ANY] * n,
        out_shape=[_sds((N_CHIP, g.shape[1] // 2, g.shape[2]), F32) for g in grads],
        scratch_shapes=[pltpu.SemaphoreType.DMA((n,))] * 2,
    )(*grads)


def _row_tile(rows, cols, itemsize=4, budget=2 * VMEM_MB):
    fits = [t for t in range(SUB, rows + 1, SUB) if rows % t == 0 and t * cols * itemsize <= budget]
    return max(fits) if fits and rows * cols * itemsize > budget else rows


def pair_add(g, r, cidx):
    _, rows, cols = g.shape
    rh = rows // 2
    tr = _row_tile(rh, cols)
    per = rh // tr

    def body(c_ref, g_ref, r_ref, o_ref):
        o_ref[...] = g_ref[...] + r_ref[...]

    return _call(
        body, name="pair_add",
        grid_spec=pltpu.PrefetchScalarGridSpec(
            num_scalar_prefetch=1, grid=(N_CHIP, per),
            in_specs=[pl.BlockSpec((None, tr, cols), lambda k, i, c: (k, c[0] * per + i, 0)),
                      pl.BlockSpec((None, tr, cols), lambda k, i, c: (k, i, 0))],
            out_specs=pl.BlockSpec((None, tr, cols), lambda k, i, c: (k, i, 0))),
        out_shape=_sds((N_CHIP, rh, cols), F32), compiler_params=_cp(("arbitrary", "arbitrary")),
    )(cidx, g, r)


def chip_add(p, r, kc):
    _, rh, cols = p.shape
    tr = _row_tile(rh, cols)
    per = rh // tr

    def body(k_ref, p_ref, r_ref, o_ref):
        o_ref[...] = ((p_ref[...] + r_ref[0]) + r_ref[1]) + r_ref[2]

    return _call(
        body, name="chip_add",
        grid_spec=pltpu.PrefetchScalarGridSpec(
            num_scalar_prefetch=1, grid=(per,),
            in_specs=[pl.BlockSpec((None, tr, cols), lambda i, k: (k[0], i, 0)),
                      pl.BlockSpec((N_CHIP - 1, tr, cols), lambda i, k: (0, i, 0))],
            out_specs=pl.BlockSpec((tr, cols), lambda i, k: (k[1] * per + i, 0))),
        out_shape=_sds((2 * rh, cols), F32), compiler_params=_cp(("arbitrary",)),
    )(kc, p, r)


def pair_share(fulls, tag):
    n = len(fulls)

    def body(*refs):
        outs = refs[n:2 * n]
        ss, rs = refs[2 * n:]
        x, y, c = _coords()
        cps = []
        for a in range(n):
            rh = outs[a].shape[0] // 2
            mine = outs[a].at[pl.ds(c * rh, rh)]
            cp = pltpu.make_async_remote_copy(
                src_ref=mine, dst_ref=mine, send_sem=ss.at[a], recv_sem=rs.at[a],
                device_id=(x, y, 1 - c), device_id_type=MESH)
            cp.start()
            cps.append(cp)
        for a, cp in enumerate(cps):
            rh = outs[a].shape[0] // 2
            theirs = outs[a].at[pl.ds((1 - c) * rh, rh)]
            cp.wait_send()
            pltpu.make_async_remote_copy(
                src_ref=theirs, dst_ref=theirs, send_sem=ss.at[a], recv_sem=rs.at[a],
                device_id=(x, y, 1 - c), device_id_type=MESH).wait_recv()

    return _call(
        body, name="pair_share_" + tag, in_specs=[ANY] * n, out_specs=[ANY] * n,
        out_shape=[_sds(f.shape, F32) for f in fulls], input_output_aliases={a: a for a in range(n)},
        scratch_shapes=[pltpu.SemaphoreType.DMA((n,))] * 2,
    )(*fulls)


def small_allreduce_adamw(part, w, m, v):
    n = part.shape[1]

    def body(p_ref, w_ref, m_ref, v_ref, g_ref, d_ref, mo_ref, vo_ref, mine, gath, ss, rs):
        x, y, c = _coords()
        me = 4 * x + 2 * y + c
        mine[...] = jnp.sum(p_ref[...], axis=0, keepdims=True)
        gath[me] = mine[...]
        cps = []
        for k in range(1, 8):
            px, py, pc = x ^ (k >> 2), y ^ ((k >> 1) & 1), c ^ (k & 1)
            cp = pltpu.make_async_remote_copy(
                src_ref=mine, dst_ref=gath.at[me], send_sem=ss.at[k - 1], recv_sem=rs.at[k - 1],
                device_id=(px, py, pc), device_id_type=MESH)
            cp.start()
            cps.append(cp)
        for k in range(1, 8):
            src = 4 * (x ^ (k >> 2)) + 2 * (y ^ ((k >> 1) & 1)) + (c ^ (k & 1))
            pltpu.make_async_remote_copy(
                src_ref=mine, dst_ref=gath.at[src], send_sem=ss.at[k - 1], recv_sem=rs.at[k - 1],
                device_id=(x, y, c), device_id_type=MESH).wait_recv()
        for cp in cps:
            cp.wait_send()
        g = gath[0]
        for dv in range(1, 8):
            g = g + gath[dv]
        g_ref[...] = g
        d_ref[...], mo_ref[...], vo_ref[...] = _adamw_math(w_ref[...], g, m_ref[...], v_ref[...])

    vm = pl.BlockSpec(memory_space=pltpu.VMEM)
    return _call(
        body, name="small_allreduce_adamw", in_specs=[vm] * 4, out_specs=[vm] * 4, out_shape=[_sds((1, n), F32)] * 4,
        scratch_shapes=[pltpu.VMEM((1, n), F32), pltpu.VMEM((8, 1, n), F32),
                        pltpu.SemaphoreType.DMA((7,)), pltpu.SemaphoreType.DMA((7,))],
    )(part, w, m, v)


def _unshard_cols(g):
    k, r, cs = g.shape
    return g.transpose(1, 0, 2).reshape(r, k * cs)


def _shard_cols(w):
    r, c = w.shape
    return w.reshape(r, N_CHIP, c // N_CHIP).transpose(1, 0, 2)


def local_step(x, positions, ln_in_g, ln_in_b, win_g, g_cq, wuq_g, g_ckv, wuk_g, wuv_g, convw_g, conv_b, g_conv_ln,
               b_conv_ln, g_ln1, b_ln1, g_ln2, b_ln2, target, start_token, hooks):
    s, d = x.shape
    c = d - MLA_W
    row = lambda a: a.reshape(1, -1)
    ln_in_g = row(ln_in_g) + start_token[0:1, 0:1]

    o_kr = R_Q + R_KV
    o_cv = o_kr + D_ROPE
    n_in = o_cv + 2 * c
    per = n_in // N_CHIP

    def in_cols(a, b):
        return [win_g[k, max(a, per * k) - per * k:min(b, per * (k + 1)) - per * k]
                for k in range(N_CHIP) if max(a, per * k) < min(b, per * (k + 1))]

    win_rt = jnp.concatenate(in_cols(0, o_kr) + in_cols(o_cv, n_in) + in_cols(o_kr, o_cv)
                             + [jnp.zeros((LANE - D_ROPE, d), BF)], axis=0)
    kr_blk = (o_kr + 2 * c) // LANE
    wuq = _unshard_cols(wuq_g).reshape(R_Q, HEADS, D_QK)
    wuq_r = jnp.concatenate([wuq[:, :, :D_NOPE].reshape(R_Q, MLA_W),
                             jnp.pad(wuq[:, :, D_NOPE:], ((0, 0), (0, 0), (0, LANE - D_ROPE))).reshape(R_Q, MLA_W)], axis=1)
    wuk = _unshard_cols(wuk_g)
    wuv = _unshard_cols(wuv_g)
    conv_w = jnp.pad(_unshard_cols(convw_g), ((0, 1), (0, 0)))

    half = D_ROPE // 2
    inv_freq = ROPE_BASE ** (-jnp.arange(half, dtype=F32) * (2.0 / D_ROPE))
    invf = jnp.concatenate([inv_freq, inv_freq, jnp.zeros((LANE - D_ROPE,), F32)]).reshape(1, LANE)
    cos, sin = rope_tables(positions.astype(F32).reshape(s, 1), invf)
    x0, x0b = ln_in_fwd(x, ln_in_g, row(ln_in_b))
    h = matmul_nt("in_proj", x0b, win_rt, 1024, 640)
    qc, cqn = q_proj(h, g_cq, wuq_r, cos, sin)
    kc, kct, v, ckvn = kv_proj(h, g_ckv, wuk, wuv, cos, sin, kr_blk)
    o, ob, lse = attn_fwd(qc, kc, v)
    co, uc = conv_fwd(h, conv_w, conv_b, g_conv_ln, b_conv_ln)
    wout_g, wff1_g, wff2_g = hooks.late_weights(ob)
    wout = wout_g.reshape(d, d)
    wff2 = wff2_g.reshape(-1, d)
    r1, x1, x1b = out_proj_ln1(ob, co, wout, x0, g_ln1, b_ln1)
    rb, a1b = ff1_fwd(x1b, wff1_g)
    dr2, dr2b, loss8, dg2, db2 = ff2_ln2_loss(a1b, wff2, x1, target, g_ln2, b_ln2)

    df1b = ff2_bwd_act(dr2b, wff2, rb)
    gw_ff2 = wgrad("wgrad_ff2", a1b, dr2b, 1024, 1024).reshape(N_CHIP, -1, d)
    gw_ff1 = wgrad("wgrad_ff1", x1b, df1b, 1024, 1024, shards=N_CHIP)
    tok = hooks.ff_grads(gw_ff2, gw_ff1)
    dr1, dr1b, dg1, db1 = ff1_bwd_ln1(df1b, wff1_g, dr2, r1, g_ln1 + tok[0:1, 0:1])
    tok = hooks.ff_grads_mid(dr1b)
    gw_out = jnp.concatenate([wgrad("wgrad_out_attn", ob, dr1b, 1024, 1024)[0],
                              wgrad("wgrad_out_conv", co, dr1b, 1024, 1024)[0]], axis=0).reshape(N_CHIP, -1, d)
    dob, dot, dco, delta = out_proj_bwd(dr1b, wout.T, o)
    duc, dgc, dbc, dcb = conv_bwd_ln(uc, dco, g_conv_ln + tok[0:1, 0:1], b_conv_ln)
    dconv, gconvw = conv_bwd_taps(h, duc, conv_w)
    dqt, dk, dv = attn_bwd(qc, kc, kct, v, dob, dot, lse, delta)
    dqb, dcq, dgq = q_bwd(dqt, h, g_cq, wuq_r.T, cos, sin)
    dknb, dvb, dckv, dkr, dgkv = kv_bwd(dk, dv, h, g_ckv, wuk.T, wuv.T, cos, sin)
    gwuq_r = wgrad("wgrad_uq", cqn, dqb, 512, 1024)[0]
    gw_uk = wgrad("wgrad_uk", ckvn, dknb, 512, 1024, shards=N_CHIP)
    gw_uv = wgrad("wgrad_uv", ckvn, dvb, 512, 1024, shards=N_CHIP)
    dh = jnp.concatenate([dcq, dckv, dconv, dkr], axis=1)
    gwin_rt = wgrad("wgrad_in", dh, x0b, 640, 1024)[0]

    gwin_t = jnp.concatenate([gwin_rt[:o_kr], gwin_rt[o_kr + 2 * c:o_kr + 2 * c + D_ROPE], gwin_rt[o_kr:o_kr + 2 * c]], axis=0)
    gwin_t = jnp.pad(gwin_t.reshape(N_CHIP, per, d), ((0, 0), (0, win_g.shape[1] - per), (0, 0)))
    gwuq = jnp.concatenate([gwuq_r[:, :MLA_W].reshape(R_Q, HEADS, D_NOPE),
                            gwuq_r[:, MLA_W:].reshape(R_Q, HEADS, LANE)[:, :, :D_ROPE]], axis=2).reshape(R_Q, HEADS * D_QK)
    tok = hooks.rest_grads(dict(w_in=gwin_t, w_uq=_shard_cols(gwuq), w_uk=gw_uk, w_uv=gw_uv,
                                conv_w=_shard_cols(gconvw), w_out=gw_out))
    gx, dgin, dbin = in_proj_bwd_ln(dh, win_rt, dr1, x, ln_in_g + tok[0:1, 0:1])
    small = jnp.concatenate([dgin, dbin, dgq, dgkv, dcb, dgc, dbc, dg1, db1, dg2, db2, loss8], axis=1)
    return gx, small


BIG = ["w_in", "w_uq", "w_uk", "w_uv", "conv_w", "w_out", "w_ff1", "w_ff2"]
EARLY = ["w_in", "w_uq", "w_uk", "w_uv", "conv_w"]
LATE = ["w_out", "w_ff1", "w_ff2"]
SMALL = ["ln_in_g", "ln_in_b", "g_cq", "g_ckv", "conv_b", "g_conv_ln", "b_conv_ln", "g_ln1", "b_ln1", "g_ln2", "b_ln2"]
WEIGHTS = ["ln_in_g", "ln_in_b", "w_in", "g_cq", "w_uq", "g_ckv", "w_uk", "w_uv", "conv_w", "conv_b", "g_conv_ln",
           "b_conv_ln", "w_out", "g_ln1", "b_ln1", "w_ff1", "w_ff2", "g_ln2", "b_ln2"]


def _pad_rows(a, rows):
    return jnp.pad(a, ((0, rows - a.shape[0]), (0, 0)))


def kernel(x, positions, ln_in_g, ln_in_b, w_in, g_cq, w_uq, g_ckv, w_uk, w_uv, conv_w, conv_b, g_conv_ln, b_conv_ln, w_out, g_ln1, b_ln1, w_ff1, w_ff2, g_ln2, b_ln2, loss_target, m_ln_in_g, m_ln_in_b, m_w_in, m_g_cq, m_w_uq, m_g_ckv, m_w_uk, m_w_uv, m_conv_w, m_conv_b, m_g_conv_ln, m_b_conv_ln, m_w_out, m_g_ln1, m_b_ln1, m_w_ff1, m_w_ff2, m_g_ln2, m_b_ln2, v_ln_in_g, v_ln_in_b, v_w_in, v_g_cq, v_w_uq, v_g_ckv, v_w_uk, v_w_uv, v_conv_w, v_conv_b, v_g_conv_ln, v_b_conv_ln, v_w_out, v_g_ln1, v_b_ln1, v_w_ff1, v_w_ff2, v_g_ln2, v_b_ln2):
    w = dict(ln_in_g=ln_in_g, ln_in_b=ln_in_b, w_in=w_in, g_cq=g_cq, w_uq=w_uq, g_ckv=g_ckv, w_uk=w_uk, w_uv=w_uv,
             conv_w=conv_w, conv_b=conv_b, g_conv_ln=g_conv_ln, b_conv_ln=b_conv_ln, w_out=w_out, g_ln1=g_ln1,
             b_ln1=b_ln1, w_ff1=w_ff1, w_ff2=w_ff2, g_ln2=g_ln2, b_ln2=b_ln2)
    m = dict(ln_in_g=m_ln_in_g, ln_in_b=m_ln_in_b, w_in=m_w_in, g_cq=m_g_cq, w_uq=m_w_uq, g_ckv=m_g_ckv, w_uk=m_w_uk,
             w_uv=m_w_uv, conv_w=m_conv_w, conv_b=m_conv_b, g_conv_ln=m_g_conv_ln, b_conv_ln=m_b_conv_ln, w_out=m_w_out,
             g_ln1=m_g_ln1, b_ln1=m_b_ln1, w_ff1=m_w_ff1, w_ff2=m_w_ff2, g_ln2=m_g_ln2, b_ln2=m_b_ln2)
    v = dict(ln_in_g=v_ln_in_g, ln_in_b=v_ln_in_b, w_in=v_w_in, g_cq=v_g_cq, w_uq=v_w_uq, g_ckv=v_g_ckv, w_uk=v_w_uk,
             w_uv=v_w_uv, conv_w=v_conv_w, conv_b=v_conv_b, g_conv_ln=v_g_conv_ln, b_conv_ln=v_b_conv_ln, w_out=v_w_out,
             g_ln1=v_g_ln1, b_ln1=v_b_ln1, w_ff1=v_w_ff1, w_ff2=v_w_ff2, g_ln2=v_g_ln2, b_ln2=v_b_ln2)

    as2d = lambda t, n: t[n][0].T if n == "w_in" else t[n][0]
    sh2 = {n: as2d(w, n) for n in BIG}
    cidx = lax.axis_index("c").astype(jnp.int32).reshape(1)
    me = 2 * lax.axis_index("x") + lax.axis_index("y")
    kc = jnp.stack([me, lax.axis_index("c")]).astype(jnp.int32)

    pad_to = {"conv_w": CONV_K + 1, "w_in": -(-sh2["w_in"].shape[0] // (4 * SUB)) * (4 * SUB)}
    early = [_pad_rows(sh2[n] if n == "conv_w" else sh2[n].astype(BF), pad_to.get(n, sh2[n].shape[0])) for n in EARLY]
    gw = dict(zip(EARLY, all_gather_shards(early)))
    gw["conv_w"] = gw["conv_w"][:, :CONV_K]
    late = [sh2[n].astype(BF) for n in LATE]
    ag = split_send_start("late_weights_start", "gather", late, [(N_CHIP,) + a.shape for a in late], gw["w_uq"])
    rest = [n for n in BIG if n not in ("w_ff2", "w_ff1")]
    flight = {}

    class Hooks:
        @staticmethod
        def late_weights(after):
            mine, lands = split_send_wait("late_weights_wait", "gather", *ag[:4], after)
            return [lax.dynamic_update_slice(g, a[None], (me, 0, 0)) for g, a in zip(lands, mine)]

        @staticmethod
        def ff_grads(gw_ff2, gw_ff1):
            full = [gw_ff2, gw_ff1]
            st = split_send_start("ff_pair_start", "pair", full, [(N_CHIP, g.shape[1] // 2, g.shape[2]) for g in full], ag[4])
            flight["ff_pair"] = st[:4]
            flight["token"] = st[4]
            return st[4]

        @staticmethod
        def ff_grads_mid(after):
            full, recv = split_send_wait("ff_pair_wait", "pair", *flight["ff_pair"], after)
            psum = [pair_add(g, r, cidx) for g, r in zip(full, recv)]
            st = split_send_start("ff_grads_start", "scatter", psum, [(N_CHIP - 1,) + p.shape[1:] for p in psum], flight["token"])
            flight["ff"] = st[:4]
            flight["token"] = st[4]
            return st[4]

        @staticmethod
        def rest_grads(big):
            full = [big[n] for n in rest]
            psum = [pair_add(g, r, cidx) for g, r in zip(full, pair_exchange(full, "rest"))]
            st = split_send_start("rest_grads_start", "scatter", psum, [(N_CHIP - 1,) + p.shape[1:] for p in psum], flight["token"])
            flight["rest"] = st[:4]
            return st[4]

    gx, small = local_step(
        x[0], positions[0], ln_in_g, ln_in_b, gw["w_in"], g_cq, gw["w_uq"], g_ckv, gw["w_uk"], gw["w_uv"], gw["conv_w"],
        conv_b, g_conv_ln, b_conv_ln, g_ln1, b_ln1, g_ln2, b_ln2, loss_target[0], ag[4], Hooks)

    ff_psum, ff_got = split_send_wait("ff_grads_wait", "scatter", *flight["ff"], gx)
    rest_psum, rest_got = split_send_wait("rest_grads_wait", "scatter", *flight["rest"], gx)
    summed = [chip_add(p, r, kc) for p, r in zip(rest_psum + ff_psum, rest_got + ff_got)]
    gsh = dict(zip(rest + ["w_ff2", "w_ff1"], pair_share(summed, "all")))
    for n in pad_to:
        gsh[n] = gsh[n][:sh2[n].shape[0]]

    grad, delta, new_m, new_v = {}, {}, {}, {}
    for n in BIG:
        back = (lambda a: a.T[None]) if n == "w_in" else (lambda a: a[None])
        d_, m_, v_ = adamw("adamw_" + n, sh2[n], gsh[n], as2d(m, n), as2d(v, n))
        grad[n], delta[n], new_m[n], new_v[n] = back(gsh[n]), back(d_), back(m_), back(v_)

    flat = lambda t: jnp.concatenate([t[n].reshape(1, -1) for n in SMALL] + [jnp.zeros((1, LANE), F32)], axis=1)
    g_s, d_s, m_s, v_s = small_allreduce_adamw(small, flat(w), flat(m), flat(v))
    off = 0
    for n in SMALL:
        sz = w[n].size
        for dst, src in ((grad, g_s), (delta, d_s), (new_m, m_s), (new_v, v_s)):
            dst[n] = src[0, off:off + sz].reshape(w[n].shape)
        off += sz
    loss = jnp.sum(g_s[0, off:off + LANE])

    return (loss, gx[None], *[grad[n] for n in WEIGHTS], *[delta[n] for n in WEIGHTS],
            *[new_m[n] for n in WEIGHTS], *[new_v[n] for n in WEIGHTS])
```

```python
import functools

import jax
import jax.numpy as jnp
from jax import lax
from jax.experimental import pallas as pl
from jax.experimental.pallas import tpu as pltpu

F32 = jnp.float32
BF = jnp.bfloat16

HEADS = 8
D_NOPE = 128
D_ROPE = 64
D_V = 128
D_QK = D_NOPE + D_ROPE
R_Q = 512
R_KV = 512
MLA_W = HEADS * D_V
CONV_K = 31
CONV_PAD = CONV_K // 2
ROPE_BASE = 10000.0
LOG2E = 1.4426950408889634
LN2 = 0.6931471805599453
LN_EPS = 1e-5
RMS_EPS = 1e-6
ALPHA = (2.0 * 1) ** 0.25
ADAM_LR = 0.001
ADAM_B1 = 0.9
ADAM_B2 = 0.999
ADAM_EPS = 1e-08
ADAM_WD = 0.01
ADAM_STEP = 10

LANE = 128
SUB = 8
HALO = 16
N_CHIP = 4
MESH = pl.DeviceIdType.MESH
VMEM_MB = 1024 * 1024


def _call(body, **kw):
    return pl.pallas_call(body, **kw)


def _cp(sem, mb=48):
    return pltpu.CompilerParams(dimension_semantics=sem, vmem_limit_bytes=mb * VMEM_MB)


def _sds(shape, dt):
    return jax.ShapeDtypeStruct(shape, dt)


def _dot(a, b):
    return jnp.dot(a, b, preferred_element_type=F32)


def _dot_nt(a, b):
    return lax.dot_general(a, b, (((1,), (1,)), ((), ())), preferred_element_type=F32)


def _dot_tn(a, b):
    return lax.dot_general(a, b, (((0,), (0,)), ((), ())), preferred_element_type=F32)


def _rows8(v):
    t, n = v.shape
    return v.reshape(t // SUB, SUB, n).sum(axis=0)


def _ln_stats(r):
    mu = jnp.mean(r, axis=-1, keepdims=True)
    xc = r - mu
    var = jnp.mean(xc * xc, axis=-1, keepdims=True)
    rstd = lax.rsqrt(var + LN_EPS)
    return xc * rstd, rstd


def _ln_bwd(dy, xhat, rstd, g):
    dyh = dy * g
    m1 = jnp.mean(dyh, axis=-1, keepdims=True)
    m2 = jnp.mean(dyh * xhat, axis=-1, keepdims=True)
    return rstd * (dyh - m1 - xhat * m2)


def _rms_fwd(x, g):
    rr = lax.rsqrt(jnp.mean(x * x, axis=-1, keepdims=True) + RMS_EPS)
    xh = x * rr
    return xh * g, xh, rr


def _rms_bwd(dy, xh, rr, g):
    dyg = dy * g
    return rr * (dyg - xh * jnp.mean(dyg * xh, axis=-1, keepdims=True))


def _rope128(x, cos, sin_signed):
    lane = lax.broadcasted_iota(jnp.int32, x.shape, 1)
    rot = jnp.where(lane < D_ROPE // 2, pltpu.roll(x, LANE - D_ROPE // 2, 1), pltpu.roll(x, D_ROPE // 2, 1))
    return x * cos + rot * sin_signed


def _unrope128(dy, cos, sin_signed):
    t = dy * sin_signed
    lane = lax.broadcasted_iota(jnp.int32, dy.shape, 1)
    rot = jnp.where(lane < D_ROPE // 2, pltpu.roll(t, LANE - D_ROPE // 2, 1), pltpu.roll(t, D_ROPE // 2, 1))
    return dy * cos + rot


def _as_row(col):
    return jnp.transpose(jnp.broadcast_to(col, (col.shape[0], LANE)))[0:1, :]


def _sigmoid(x):
    return 1.0 / (1.0 + jnp.exp(-x))


def _row_chunks(tm, fn, rc=128):
    rc = min(rc, tm)

    def step(ci, carry):
        fn(pl.ds(pl.multiple_of(ci * rc, rc), rc))
        return carry

    lax.fori_loop(0, tm // rc, step, 0)


def _unrolled_loop(n, unroll, fn, init):
    unroll = min(n, unroll)
    assert n % unroll == 0

    def body(t, carry):
        for u in range(unroll):
            carry = fn(t * unroll + u, carry)
        return carry

    return lax.fori_loop(0, n // unroll, body, init)


def _tile(s, want):
    t = min(s, want)
    assert s % t == 0
    return t


def rope_tables(pos_f, invf):
    s = pos_f.shape[0]
    tm = _tile(s, 1024)

    def body(p_ref, f_ref, c_ref, s_ref):
        ang = p_ref[...] * f_ref[...]
        lane = lax.broadcasted_iota(jnp.int32, ang.shape, 1)
        c = jnp.cos(ang)
        sn = jnp.sin(ang)
        c_ref[...] = jnp.where(lane < D_ROPE, c, 0.0)
        s_ref[...] = jnp.where(lane < D_ROPE // 2, -sn, jnp.where(lane < D_ROPE, sn, 0.0))

    return _call(
        body, name="rope_tables", grid=(s // tm,),
        in_specs=[pl.BlockSpec((tm, 1), lambda i: (i, 0)), pl.BlockSpec((1, LANE), lambda i: (0, 0))],
        out_specs=[pl.BlockSpec((tm, LANE), lambda i: (i, 0))] * 2,
        out_shape=[_sds((s, LANE), F32)] * 2,
        compiler_params=_cp(("arbitrary",)),
    )(pos_f, invf)


def ln_in_fwd(x, g, b):
    s, d = x.shape
    tm = _tile(s, 512)

    def body(x_ref, g_ref, b_ref, o_ref, ob_ref):
        xhat, _ = _ln_stats(x_ref[...])
        y = xhat * g_ref[...] + b_ref[...]
        o_ref[...] = y
        ob_ref[...] = y.astype(BF)

    row = pl.BlockSpec((1, d), lambda i: (0, 0))
    tok = pl.BlockSpec((tm, d), lambda i: (i, 0))
    return _call(
        body, name="ln_in_fwd", grid=(s // tm,), in_specs=[tok, row, row], out_specs=[tok, tok],
        out_shape=[_sds((s, d), F32), _sds((s, d), BF)], compiler_params=_cp(("arbitrary",)),
    )(x, g, b)


def matmul(name, a, w, tm, tn, out_dtype=F32):
    s, k = a.shape
    n = w.shape[1]
    tm = _tile(s, tm)
    tn = _tile(n, tn)

    def body(a_ref, w_ref, o_ref):
        o_ref[...] = _dot(a_ref[...], w_ref[...]).astype(o_ref.dtype)

    return _call(
        body, name=name, grid=(s // tm, n // tn),
        in_specs=[pl.BlockSpec((tm, k), lambda i, j: (i, 0)), pl.BlockSpec((k, tn), lambda i, j: (0, j))],
        out_specs=pl.BlockSpec((tm, tn), lambda i, j: (i, j)),
        out_shape=_sds((s, n), out_dtype), compiler_params=_cp(("arbitrary", "arbitrary")),
    )(a, w)


def matmul_nt(name, a, wt, tm, tn, out_dtype=F32):
    s, k = a.shape
    n = wt.shape[0]
    tm = _tile(s, tm)
    tn = _tile(n, tn)

    def body(a_ref, w_ref, o_ref):
        o_ref[...] = _dot_nt(a_ref[...], w_ref[...]).astype(o_ref.dtype)

    return _call(
        body, name=name, grid=(s // tm, n // tn),
        in_specs=[pl.BlockSpec((tm, k), lambda i, j: (i, 0)), pl.BlockSpec((tn, k), lambda i, j: (j, 0))],
        out_specs=pl.BlockSpec((tm, tn), lambda i, j: (i, j)),
        out_shape=_sds((s, n), out_dtype), compiler_params=_cp(("arbitrary", "arbitrary")),
    )(a, wt)


def q_proj(h, g_cq, wuq, cos, sin):
    s = h.shape[0]
    tm = _tile(s, 512)

    def body(h_ref, g_ref, w_ref, c_ref, s_ref, q_ref, n_ref):
        y, _, _ = _rms_fwd(h_ref[...], g_ref[...])
        yb = y.astype(BF)
        n_ref[...] = yb
        q = _dot(yb, w_ref[...])
        c = c_ref[...]
        sn = s_ref[...]
        for hd in range(HEADS):
            q_ref[hd, :, 0:LANE] = q[:, LANE * hd:LANE * (hd + 1)].astype(BF)
            qr = q[:, MLA_W + LANE * hd:MLA_W + LANE * (hd + 1)]
            q_ref[hd, :, LANE:2 * LANE] = _rope128(qr, c, sn).astype(BF)

    return _call(
        body, name="q_proj", grid=(s // tm,),
        in_specs=[pl.BlockSpec((tm, R_Q), lambda i: (i, 0)), pl.BlockSpec((1, R_Q), lambda i: (0, 0)),
                  pl.BlockSpec((R_Q, 2 * MLA_W), lambda i: (0, 0)),
                  pl.BlockSpec((tm, LANE), lambda i: (i, 0)), pl.BlockSpec((tm, LANE), lambda i: (i, 0))],
        out_specs=[pl.BlockSpec((HEADS, tm, 2 * LANE), lambda i: (0, i, 0)), pl.BlockSpec((tm, R_Q), lambda i: (i, 0))],
        out_shape=[_sds((HEADS, s, 2 * LANE), BF), _sds((s, R_Q), BF)], compiler_params=_cp(("arbitrary",)),
    )(h, g_cq, wuq, cos, sin)


def kv_proj(h, g_ckv, wuk, wuv, cos, sin, kr_blk):
    s = h.shape[0]
    tm = _tile(s, 512)

    def body(h_ref, kr_ref, g_ref, wk_ref, wv_ref, c_ref, s_ref, k_ref, kt_ref, v_ref, n_ref):
        y, _, _ = _rms_fwd(h_ref[...], g_ref[...])
        yb = y.astype(BF)
        n_ref[...] = yb
        kn = _dot(yb, wk_ref[...])
        v = _dot(yb, wv_ref[...])
        kr = _rope128(kr_ref[...], c_ref[...], s_ref[...])
        krb = kr.astype(BF)
        krt = kr.T.astype(BF)
        for hd in range(HEADS):
            knh = kn[:, LANE * hd:LANE * (hd + 1)]
            k_ref[hd, :, 0:LANE] = knh.astype(BF)
            k_ref[hd, :, LANE:2 * LANE] = krb
            kt_ref[hd, 0:LANE, :] = knh.T.astype(BF)
            kt_ref[hd, LANE:2 * LANE, :] = krt
            v_ref[hd] = v[:, LANE * hd:LANE * (hd + 1)].astype(BF)

    tab = pl.BlockSpec((tm, LANE), lambda i: (i, 0))
    wsp = pl.BlockSpec((R_KV, MLA_W), lambda i: (0, 0))
    return _call(
        body, name="kv_proj", grid=(s // tm,),
        in_specs=[pl.BlockSpec((tm, R_KV), lambda i: (i, 1)), pl.BlockSpec((tm, LANE), lambda i: (i, kr_blk)),
                  pl.BlockSpec((1, R_KV), lambda i: (0, 0)), wsp, wsp, tab, tab],
        out_specs=[pl.BlockSpec((HEADS, tm, 2 * LANE), lambda i: (0, i, 0)), pl.BlockSpec((HEADS, 2 * LANE, tm), lambda i: (0, 0, i)),
                   pl.BlockSpec((HEADS, tm, LANE), lambda i: (0, i, 0)), pl.BlockSpec((tm, R_KV), lambda i: (i, 0))],
        out_shape=[_sds((HEADS, s, 2 * LANE), BF), _sds((HEADS, 2 * LANE, s), BF), _sds((HEADS, s, LANE), BF), _sds((s, R_KV), BF)],
        compiler_params=_cp(("arbitrary",)),
    )(h, h, g_ckv, wuk, wuv, cos, sin)


def attn_fwd(qc, kc, v):
    _, s, _ = qc.shape
    tq = _tile(s, 256)
    tk = _tile(s, 512)
    scale = D_QK ** -0.5
    c2 = scale * LOG2E
    nk = s // tk
    nb = tk // LANE
    un = 8

    def body(q_ref, k_ref, v_ref, o_ref, ob_ref, l_ref, s_scr, m_scr):
        q = q_ref[...]

        def scores(j, mpart):
            off = pl.multiple_of(j * tk, tk)
            sc = _dot_nt(q, k_ref[pl.ds(off, tk), :]) * c2
            s_scr[:, pl.ds(off, tk)] = sc
            for b in range(nb):
                mpart = jnp.maximum(mpart, sc[:, LANE * b:LANE * (b + 1)])
            return mpart

        mpart = _unrolled_loop(nk, un, scores, jnp.full((tq, LANE), -jnp.inf, F32))
        m = jnp.max(mpart, axis=-1, keepdims=True)
        m_scr[...] = jnp.broadcast_to(m, (tq, LANE))

        def weigh(j, carry):
            lpart, acc = carry
            off = pl.multiple_of(j * tk, tk)
            ps = []
            for b in range(nb):
                p = jnp.exp2(s_scr[:, pl.ds(off + LANE * b, LANE)] - m_scr[...])
                lpart = lpart + p
                ps.append(p.astype(BF))
            acc = acc + _dot(jnp.concatenate(ps, axis=1), v_ref[pl.ds(off, tk), :])
            return lpart, acc

        lpart, acc = _unrolled_loop(nk, un, weigh, (jnp.zeros((tq, LANE), F32), jnp.zeros((tq, D_V), F32)))
        l = jnp.sum(lpart, axis=-1, keepdims=True)
        o = acc / l
        o_ref[...] = o
        ob_ref[...] = o.astype(BF)
        l_ref[...] = _as_row(m + jnp.log(l) * LOG2E)

    return _call(
        body, name="attn_fwd", grid=(HEADS, s // tq),
        in_specs=[pl.BlockSpec((None, tq, 2 * LANE), lambda h, i: (h, i, 0)),
                  pl.BlockSpec((None, s, 2 * LANE), lambda h, i: (h, 0, 0)),
                  pl.BlockSpec((None, s, LANE), lambda h, i: (h, 0, 0))],
        out_specs=[pl.BlockSpec((tq, LANE), lambda h, i: (i, h)), pl.BlockSpec((tq, LANE), lambda h, i: (i, h)),
                   pl.BlockSpec((None, 1, tq), lambda h, i: (h, 0, i))],
        out_shape=[_sds((s, MLA_W), F32), _sds((s, MLA_W), BF), _sds((HEADS, 1, s), F32)],
        scratch_shapes=[pltpu.VMEM((tq, s + LANE), F32), pltpu.VMEM((tq, LANE), F32)],
        compiler_params=_cp(("arbitrary", "arbitrary")),
    )(qc, kc, v)


def _halo_specs(tm, s, width, col):
    r = tm // HALO
    nb = s // HALO
    cur = pl.BlockSpec((tm, width), lambda i: (i, col))
    prev = pl.BlockSpec((HALO, width), lambda i: (jnp.maximum(i * r - 1, 0), col))
    nxt = pl.BlockSpec((HALO, width), lambda i: (jnp.minimum((i + 1) * r, nb - 1), col))
    return cur, prev, nxt


def _slab_shapes(tm, c):
    return (tm + 2 * HALO, c + LANE), (SUB - 1, tm + 2 * HALO - SUB, c + LANE)


def _fill_slab(slab, tm, prev, cur, nxt):
    i = pl.program_id(0)
    last = pl.num_programs(0) - 1
    c = cur.shape[1]
    slab[0:HALO, 0:c] = jnp.where(i > 0, prev, 0.0)
    slab[HALO:HALO + tm, 0:c] = cur
    slab[HALO + tm:2 * HALO + tm, 0:c] = jnp.where(i < last, nxt, 0.0)


def _rotate_slab(slab, rot, tm):
    rows = tm + 2 * HALO - SUB
    c = slab.shape[1] - LANE
    for b in range(1, SUB):
        rot[b - 1, :, 0:c] = slab[pl.ds(b, rows), 0:c]


def _shifted(slab, rot, start, rc, cs):
    b = start % SUB
    if b == 0:
        return slab[pl.ds(start, rc), cs]
    return rot[b - 1, pl.ds(start - b, rc), cs]


def conv_fwd(h, conv_w, conv_b, g_ln, b_ln):
    s = h.shape[0]
    c = conv_w.shape[1]
    tm = _tile(s, 256)
    rc = _tile(tm, 64)

    def body(a_ref, ap_ref, an_ref, g_ref, gp_ref, gn_ref, w_ref, cb_ref, lg_ref, lb_ref, co_ref, uc_ref, slab, rot):
        _fill_slab(slab, tm, ap_ref[...] * _sigmoid(gp_ref[...]), a_ref[...] * _sigmoid(g_ref[...]),
                   an_ref[...] * _sigmoid(gn_ref[...]))
        _rotate_slab(slab, rot, tm)

        def lane_block(cb, carry):
            cs = pl.ds(pl.multiple_of(cb * LANE, LANE), LANE)
            for r0 in range(0, tm, rc):
                acc = jnp.zeros((rc, LANE), F32)
                for k in range(CONV_K):
                    acc = acc + w_ref[k:k + 1, cs] * _shifted(slab, rot, r0 + HALO - CONV_PAD + k, rc, cs)
                uc_ref[r0:r0 + rc, cs] = acc + cb_ref[:, cs]
            return carry

        lax.fori_loop(0, c // LANE, lane_block, 0)
        xhat, _ = _ln_stats(uc_ref[...])
        cl = xhat * lg_ref[...] + lb_ref[...]
        co_ref[...] = (cl * _sigmoid(cl)).astype(BF)

    a_specs = _halo_specs(tm, s, c, 1)
    g_specs = _halo_specs(tm, s, c, 2)
    row = pl.BlockSpec((1, c), lambda i: (0, 0))
    tok = pl.BlockSpec((tm, c), lambda i: (i, 0))
    return _call(
        body, name="conv_fwd", grid=(s // tm,),
        in_specs=[*a_specs, *g_specs, pl.BlockSpec(conv_w.shape, lambda i: (0, 0)), row, row, row],
        out_specs=[tok, tok], out_shape=[_sds((s, c), BF), _sds((s, c), F32)],
        scratch_shapes=[pltpu.VMEM(shp, F32) for shp in _slab_shapes(tm, c)],
        compiler_params=_cp(("arbitrary",)),
    )(h, h, h, h, h, h, conv_w, conv_b, g_ln, b_ln)


def out_proj_ln1(ob, co, wout, x0, g1, b1):
    s, d = x0.shape
    kh = ob.shape[1]
    tm = _tile(s, 256)

    def body(o_ref, c_ref, w_ref, x_ref, g_ref, b_ref, r_ref, x1_ref, x1b_ref, acc):
        acc[...] = _dot(o_ref[...], w_ref[0:kh, :]) + _dot(c_ref[...], w_ref[kh:2 * kh, :])
        g = g_ref[...]
        b = b_ref[...]

        def chunk(rows):
            r = ALPHA * x_ref[rows, :] + acc[rows, :]
            r_ref[rows, :] = r
            xhat, _ = _ln_stats(r)
            y = xhat * g + b
            x1_ref[rows, :] = y
            x1b_ref[rows, :] = y.astype(BF)

        _row_chunks(tm, chunk)

    half = pl.BlockSpec((tm, kh), lambda i: (i, 0))
    tok = pl.BlockSpec((tm, d), lambda i: (i, 0))
    row = pl.BlockSpec((1, d), lambda i: (0, 0))
    return _call(
        body, name="out_proj_ln1", grid=(s // tm,),
        in_specs=[half, half, pl.BlockSpec((2 * kh, d), lambda i: (0, 0)), tok, row, row],
        out_specs=[tok, tok, tok], out_shape=[_sds((s, d), F32), _sds((s, d), F32), _sds((s, d), BF)],
        scratch_shapes=[pltpu.VMEM((tm, d), F32)], compiler_params=_cp(("arbitrary",)),
    )(ob, co, wout, x0, g1, b1)


def ff1_fwd(x1b, wff1_g):
    s, d = x1b.shape
    nsh, _, fs = wff1_g.shape
    tm = _tile(s, 1024)
    tn = _tile(fs, 1024)
    per = fs // tn

    def body(a_ref, w_ref, r_ref, a1_ref):
        r = jnp.maximum(_dot(a_ref[...], w_ref[...]), 0.0)
        r_ref[...] = r.astype(BF)
        a1_ref[...] = (r * r).astype(BF)

    out = pl.BlockSpec((tm, tn), lambda i, j: (i, j))
    return _call(
        body, name="ff1_fwd", grid=(s // tm, nsh * per),
        in_specs=[pl.BlockSpec((tm, d), lambda i, j: (i, 0)),
                  pl.BlockSpec((None, d, tn), lambda i, j: (j // per, 0, j % per))],
        out_specs=[out, out], out_shape=[_sds((s, nsh * fs), BF)] * 2,
        compiler_params=_cp(("arbitrary", "arbitrary")),
    )(x1b, wff1_g)


def ff2_ln2_loss(a1b, wff2, x1, target, g2, b2):
    s, f = a1b.shape
    d = x1.shape[1]
    tm = _tile(s, 512)
    tk = _tile(f, 1024)
    nk = f // tk

    def body(a_ref, w_ref, x_ref, t_ref, g_ref, b_ref, dr_ref, drb_ref, loss_ref, dg_ref, db_ref, acc):
        i = pl.program_id(0)
        k = pl.program_id(1)

        @pl.when(k == 0)
        def _():
            acc[...] = _dot(a_ref[...], w_ref[...])

        @pl.when(k > 0)
        def _():
            acc[...] += _dot(a_ref[...], w_ref[...])

        @pl.when(jnp.logical_and(i == 0, k == 0))
        def _():
            loss_ref[...] = jnp.zeros_like(loss_ref)
            dg_ref[...] = jnp.zeros_like(dg_ref)
            db_ref[...] = jnp.zeros_like(db_ref)

        @pl.when(k == nk - 1)
        def _():
            g = g_ref[...]

            def chunk(rows):
                r = ALPHA * x_ref[rows, :] + acc[rows, :]
                xhat, rstd = _ln_stats(r)
                e = xhat * g + b_ref[...] - t_ref[rows, :]
                e2 = _rows8(e * e)
                part = e2[:, 0:LANE]
                for c in range(1, d // LANE):
                    part = part + e2[:, LANE * c:LANE * (c + 1)]
                loss_ref[...] += part * (0.5 / d)
                dy = e * (1.0 / d)
                dg_ref[...] += _rows8(dy * xhat)
                db_ref[...] += _rows8(dy)
                dr = _ln_bwd(dy, xhat, rstd, g)
                dr_ref[rows, :] = dr
                drb_ref[rows, :] = dr.astype(BF)

            _row_chunks(tm, chunk)

    tok = pl.BlockSpec((tm, d), lambda i, k: (i, 0))
    row = pl.BlockSpec((1, d), lambda i, k: (0, 0))
    accs = pl.BlockSpec((SUB, d), lambda i, k: (0, 0))
    return _call(
        body, name="ff2_ln2_loss", grid=(s // tm, nk),
        in_specs=[pl.BlockSpec((tm, tk), lambda i, k: (i, k)), pl.BlockSpec((tk, d), lambda i, k: (k, 0)),
                  tok, tok, row, row],
        out_specs=[tok, tok, pl.BlockSpec((SUB, LANE), lambda i, k: (0, 0)), accs, accs],
        out_shape=[_sds((s, d), F32), _sds((s, d), BF), _sds((SUB, LANE), F32), _sds((SUB, d), F32), _sds((SUB, d), F32)],
        scratch_shapes=[pltpu.VMEM((tm, d), F32)], compiler_params=_cp(("arbitrary", "arbitrary"), 56),
    )(a1b, wff2, x1, target, g2, b2)


def ff2_bwd_act(dr2b, wff2, rb):
    s, d = dr2b.shape
    f = wff2.shape[0]
    tm = _tile(s, 1024)
    tn = _tile(f, 1024)

    def body(a_ref, w_ref, r_ref, o_ref):
        o_ref[...] = (_dot_nt(a_ref[...], w_ref[...]) * (2.0 * r_ref[...].astype(F32))).astype(BF)

    return _call(
        body, name="ff2_bwd_act", grid=(s // tm, f // tn),
        in_specs=[pl.BlockSpec((tm, d), lambda i, j: (i, 0)), pl.BlockSpec((tn, d), lambda i, j: (j, 0)),
                  pl.BlockSpec((tm, tn), lambda i, j: (i, j))],
        out_specs=pl.BlockSpec((tm, tn), lambda i, j: (i, j)), out_shape=_sds((s, f), BF),
        compiler_params=_cp(("arbitrary", "arbitrary")),
    )(dr2b, wff2, rb)


def wgrad(name, a, b, tm, tn, tk=2048, shards=1):
    s, m = a.shape
    n = b.shape[1]
    tm = _tile(m, tm)
    ns = n // shards
    tn = _tile(ns, tn)
    tk = _tile(s, tk)
    per = ns // tn

    def body(a_ref, b_ref, o_ref):
        k = pl.program_id(2)

        @pl.when(k == 0)
        def _():
            o_ref[...] = _dot_tn(a_ref[...], b_ref[...])

        @pl.when(k > 0)
        def _():
            o_ref[...] += _dot_tn(a_ref[...], b_ref[...])

    return _call(
        body, name=name, grid=(m // tm, n // tn, s // tk),
        in_specs=[pl.BlockSpec((tk, tm), lambda i, j, k: (k, i)), pl.BlockSpec((tk, tn), lambda i, j, k: (k, j))],
        out_specs=pl.BlockSpec((None, tm, tn), lambda i, j, k: (j // per, i, j % per)),
        out_shape=_sds((shards, m, ns), F32), compiler_params=_cp(("arbitrary", "arbitrary", "arbitrary")),
    )(a, b)


def ff1_bwd_ln1(df1b, wff1_g, dr2, r1, g1):
    s, f = df1b.shape
    d = dr2.shape[1]
    tm = _tile(s, 512)
    tk = _tile(wff1_g.shape[2], 1024)
    per = wff1_g.shape[2] // tk
    nk = f // tk

    def body(a_ref, w_ref, d2_ref, r_ref, g_ref, dr_ref, drb_ref, dg_ref, db_ref, acc):
        i = pl.program_id(0)
        k = pl.program_id(1)

        @pl.when(k == 0)
        def _():
            acc[...] = _dot_nt(a_ref[...], w_ref[...])

        @pl.when(k > 0)
        def _():
            acc[...] += _dot_nt(a_ref[...], w_ref[...])

        @pl.when(jnp.logical_and(i == 0, k == 0))
        def _():
            dg_ref[...] = jnp.zeros_like(dg_ref)
            db_ref[...] = jnp.zeros_like(db_ref)

        @pl.when(k == nk - 1)
        def _():
            g = g_ref[...]

            def chunk(rows):
                dy = ALPHA * d2_ref[rows, :] + acc[rows, :]
                xhat, rstd = _ln_stats(r_ref[rows, :])
                dg_ref[...] += _rows8(dy * xhat)
                db_ref[...] += _rows8(dy)
                dr = _ln_bwd(dy, xhat, rstd, g)
                dr_ref[rows, :] = dr
                drb_ref[rows, :] = dr.astype(BF)

            _row_chunks(tm, chunk)

    tok = pl.BlockSpec((tm, d), lambda i, k: (i, 0))
    accs = pl.BlockSpec((SUB, d), lambda i, k: (0, 0))
    return _call(
        body, name="ff1_bwd_ln1", grid=(s // tm, nk),
        in_specs=[pl.BlockSpec((tm, tk), lambda i, k: (i, k)), pl.BlockSpec((None, d, tk), lambda i, k: (k // per, 0, k % per)),
                  tok, tok, pl.BlockSpec((1, d), lambda i, k: (0, 0))],
        out_specs=[tok, tok, accs, accs],
        out_shape=[_sds((s, d), F32), _sds((s, d), BF), _sds((SUB, d), F32), _sds((SUB, d), F32)],
        scratch_shapes=[pltpu.VMEM((tm, d), F32)], compiler_params=_cp(("arbitrary", "arbitrary"), 56),
    )(df1b, wff1_g, dr2, r1, g1)


def out_proj_bwd(dr1b, woutt, o):
    s, d = dr1b.shape
    tm = _tile(s, 256)

    def body(a_ref, w_ref, o_ref, do_ref, dot_ref, dc_ref, dl_ref):
        dcat = _dot(a_ref[...], w_ref[...])
        do = dcat[:, 0:MLA_W]
        do_ref[...] = do.astype(BF)
        dc_ref[...] = dcat[:, MLA_W:]
        prod = do * o_ref[...]
        for hd in range(HEADS):
            hs = slice(LANE * hd, LANE * (hd + 1))
            dl_ref[hd] = _as_row(jnp.sum(prod[:, hs], axis=-1, keepdims=True))
            dot_ref[hd] = do[:, hs].T.astype(BF)

    half = pl.BlockSpec((tm, MLA_W), lambda i: (i, 0))
    return _call(
        body, name="out_proj_bwd", grid=(s // tm,),
        in_specs=[pl.BlockSpec((tm, d), lambda i: (i, 0)), pl.BlockSpec((d, d), lambda i: (0, 0)), half],
        out_specs=[half, pl.BlockSpec((HEADS, LANE, tm), lambda i: (0, 0, i)),
                   pl.BlockSpec((tm, d - MLA_W), lambda i: (i, 0)), pl.BlockSpec((HEADS, 1, tm), lambda i: (0, 0, i))],
        out_shape=[_sds((s, MLA_W), BF), _sds((HEADS, LANE, s), BF), _sds((s, d - MLA_W), F32), _sds((HEADS, 1, s), F32)],
        compiler_params=_cp(("arbitrary",)),
    )(dr1b, woutt, o)


def conv_bwd_ln(uc, dco, g_ln, b_ln):
    s, c = uc.shape
    tm = _tile(s, 512)

    def body(u_ref, d_ref, g_ref, b_ref, du_ref, dg_ref, db_ref, dcb_ref):
        @pl.when(pl.program_id(0) == 0)
        def _():
            dg_ref[...] = jnp.zeros_like(dg_ref)
            db_ref[...] = jnp.zeros_like(db_ref)
            dcb_ref[...] = jnp.zeros_like(dcb_ref)

        xhat, rstd = _ln_stats(u_ref[...])
        g = g_ref[...]
        cl = xhat * g + b_ref[...]
        sg = _sigmoid(cl)
        dcl = d_ref[...] * (sg * (1.0 + cl * (1.0 - sg)))
        dg_ref[...] += _rows8(dcl * xhat)
        db_ref[...] += _rows8(dcl)
        du = _ln_bwd(dcl, xhat, rstd, g)
        du_ref[...] = du
        dcb_ref[...] += _rows8(du)

    tok = pl.BlockSpec((tm, c), lambda i: (i, 0))
    row = pl.BlockSpec((1, c), lambda i: (0, 0))
    accs = pl.BlockSpec((SUB, c), lambda i: (0, 0))
    return _call(
        body, name="conv_bwd_ln", grid=(s // tm,), in_specs=[tok, tok, row, row], out_specs=[tok, accs, accs, accs],
        out_shape=[_sds((s, c), F32)] + [_sds((SUB, c), F32)] * 3, compiler_params=_cp(("arbitrary",)),
    )(uc, dco, g_ln, b_ln)


def conv_bwd_taps(h, duc, conv_w):
    s, c = duc.shape
    tm = _tile(s, 256)
    rc = _tile(tm, 64)

    def body(a_ref, ap_ref, an_ref, g_ref, gp_ref, gn_ref, d_ref, dp_ref, dn_ref, w_ref, o_ref, dw_ref,
             uslab, dslab, du_s, urot, drot, dw8):
        @pl.when(pl.program_id(0) == 0)
        def _():
            dw8[...] = jnp.zeros_like(dw8)

        sg = _sigmoid(g_ref[...])
        a = a_ref[...]
        _fill_slab(uslab, tm, ap_ref[...] * _sigmoid(gp_ref[...]), a * sg, an_ref[...] * _sigmoid(gn_ref[...]))
        _fill_slab(dslab, tm, dp_ref[...], d_ref[...], dn_ref[...])
        _rotate_slab(uslab, urot, tm)
        _rotate_slab(dslab, drot, tm)

        def lane_block(cb, carry):
            cs = pl.ds(pl.multiple_of(cb * LANE, LANE), LANE)
            for r0 in range(0, tm, rc):
                acc = jnp.zeros((rc, LANE), F32)
                for k in range(CONV_K):
                    acc = acc + w_ref[k:k + 1, cs] * _shifted(dslab, drot, r0 + HALO + CONV_PAD - k, rc, cs)
                du_s[r0:r0 + rc, cs] = acc
            return carry

        def lane_block_taps(cb, carry):
            cs = pl.ds(pl.multiple_of(cb * LANE, LANE), LANE)
            parts = []
            for k in range(CONV_K):
                prod = None
                for r0 in range(0, tm, rc):
                    t = dslab[pl.ds(r0 + HALO, rc), cs] * _shifted(uslab, urot, r0 + HALO - CONV_PAD + k, rc, cs)
                    prod = t if prod is None else prod + t
                parts.append(_rows8(prod))
            rows = SUB * CONV_K
            dw8[0:rows, cs] = dw8[0:rows, cs] + jnp.concatenate(parts, axis=0)
            return carry

        lax.fori_loop(0, c // LANE, lane_block, 0)
        lax.fori_loop(0, c // LANE, lane_block_taps, 0)

        @pl.when(pl.program_id(0) == pl.num_programs(0) - 1)
        def _():
            dw_ref[...] = jnp.zeros_like(dw_ref)
            for k in range(CONV_K):
                dw_ref[k:k + 1, :] = jnp.sum(dw8[SUB * k:SUB * (k + 1), :], axis=0, keepdims=True)

        du = du_s[...]
        o_ref[:, 0:c] = (du * sg).astype(BF)
        o_ref[:, c:2 * c] = (du * a * sg * (1.0 - sg)).astype(BF)

    a_specs = _halo_specs(tm, s, c, 1)
    g_specs = _halo_specs(tm, s, c, 2)
    d_specs = _halo_specs(tm, s, c, 0)
    wsp = pl.BlockSpec(conv_w.shape, lambda i: (0, 0))
    return _call(
        body, name="conv_bwd_taps", grid=(s // tm,), in_specs=[*a_specs, *g_specs, *d_specs, wsp],
        out_specs=[pl.BlockSpec((tm, 2 * c), lambda i: (i, 0)), wsp],
        out_shape=[_sds((s, 2 * c), BF), _sds(conv_w.shape, F32)],
        scratch_shapes=[pltpu.VMEM(_slab_shapes(tm, c)[0], F32), pltpu.VMEM(_slab_shapes(tm, c)[0], F32), pltpu.VMEM((tm, c), F32),
                        pltpu.VMEM(_slab_shapes(tm, c)[1], F32), pltpu.VMEM(_slab_shapes(tm, c)[1], F32),
                        pltpu.VMEM((SUB * conv_w.shape[0], c), F32)],
        compiler_params=_cp(("arbitrary",)),
    )(h, h, h, h, h, h, duc, duc, duc, conv_w)


def attn_bwd(qc, kc, kct, v, dob, dot, lse_r, delta_r):
    _, s, _ = qc.shape
    tk = _tile(s, 512)
    tq = _tile(s, 512)
    scale = D_QK ** -0.5
    c2 = scale * LOG2E

    def body(k_ref, kt_ref, v_ref, q_ref, do_ref, dot_ref, l_ref, dl_ref, dqt_ref, dk_ref, dvt_ref):
        @pl.when(pl.program_id(1) == 0)
        def _():
            dqt_ref[...] = jnp.zeros_like(dqt_ref)

        k = k_ref[...]
        kt = kt_ref[...]
        vv = v_ref[...]

        def step(i, carry):
            dk, dvt = carry
            off = pl.multiple_of(i * tq, tq)
            q = q_ref[pl.ds(off, tq), :]
            do = do_ref[pl.ds(off, tq), :]
            pt = jnp.exp2(_dot_nt(k, q) * c2 - l_ref[:, pl.ds(off, tq)])
            dvt = dvt + _dot_nt(dot_ref[:, pl.ds(off, tq)], pt.astype(BF))
            dpt = _dot_nt(vv, do)
            dsb = (pt * (dpt - dl_ref[:, pl.ds(off, tq)]) * scale).astype(BF)
            dk = dk + _dot(dsb, q)
            dqt_ref[:, pl.ds(off, tq)] += _dot(kt, dsb)
            return dk, dvt

        dk, dvt = _unrolled_loop(s // tq, 8, step, (jnp.zeros((tk, 2 * LANE), F32), jnp.zeros((LANE, tk), F32)))
        dk_ref[...] = dk
        dvt_ref[...] = dvt

    rowv = pl.BlockSpec((None, 1, s), lambda h, j: (h, 0, 0))
    return _call(
        body, name="attn_bwd", grid=(HEADS, s // tk),
        in_specs=[pl.BlockSpec((None, tk, 2 * LANE), lambda h, j: (h, j, 0)),
                  pl.BlockSpec((None, 2 * LANE, tk), lambda h, j: (h, 0, j)),
                  pl.BlockSpec((None, tk, LANE), lambda h, j: (h, j, 0)),
                  pl.BlockSpec((None, s, 2 * LANE), lambda h, j: (h, 0, 0)),
                  pl.BlockSpec((s, LANE), lambda h, j: (0, h)),
                  pl.BlockSpec((None, LANE, s), lambda h, j: (h, 0, 0)), rowv, rowv],
        out_specs=[pl.BlockSpec((None, 2 * LANE, s), lambda h, j: (h, 0, 0)),
                   pl.BlockSpec((None, tk, 2 * LANE), lambda h, j: (h, j, 0)),
                   pl.BlockSpec((None, LANE, tk), lambda h, j: (h, 0, j))],
        out_shape=[_sds((HEADS, 2 * LANE, s), F32), _sds((HEADS, s, 2 * LANE), F32), _sds((HEADS, LANE, s), F32)],
        compiler_params=_cp(("arbitrary", "arbitrary"), 56),
    )(kc, kct, v, qc, dob, dot, lse_r, delta_r)


def q_bwd(dqt, h, g_cq, wuqt, cos, sin):
    s = h.shape[0]
    tm = _tile(s, 256)

    def body(d_ref, h_ref, g_ref, w_ref, c_ref, s_ref, dq_ref, dc_ref, dg_ref):
        @pl.when(pl.program_id(0) == 0)
        def _():
            dg_ref[...] = jnp.zeros_like(dg_ref)

        c = c_ref[...]
        sn = s_ref[...]
        for hd in range(HEADS):
            t = d_ref[hd].T
            dq_ref[:, LANE * hd:LANE * (hd + 1)] = t[:, 0:LANE].astype(BF)
            dq_ref[:, MLA_W + LANE * hd:MLA_W + LANE * (hd + 1)] = _unrope128(t[:, LANE:2 * LANE], c, sn).astype(BF)
        dy = _dot(dq_ref[...], w_ref[...])
        g = g_ref[...]
        _, xh, rr = _rms_fwd(h_ref[...], g)
        dg_ref[...] += _rows8(dy * xh)
        dc_ref[...] = _rms_bwd(dy, xh, rr, g).astype(BF)

    tab = pl.BlockSpec((tm, LANE), lambda i: (i, 0))
    return _call(
        body, name="q_bwd", grid=(s // tm,),
        in_specs=[pl.BlockSpec((HEADS, 2 * LANE, tm), lambda i: (0, 0, i)), pl.BlockSpec((tm, R_Q), lambda i: (i, 0)),
                  pl.BlockSpec((1, R_Q), lambda i: (0, 0)), pl.BlockSpec((2 * MLA_W, R_Q), lambda i: (0, 0)), tab, tab],
        out_specs=[pl.BlockSpec((tm, 2 * MLA_W), lambda i: (i, 0)), pl.BlockSpec((tm, R_Q), lambda i: (i, 0)),
                   pl.BlockSpec((SUB, R_Q), lambda i: (0, 0))],
        out_shape=[_sds((s, 2 * MLA_W), BF), _sds((s, R_Q), BF), _sds((SUB, R_Q), F32)],
        compiler_params=_cp(("arbitrary",)),
    )(dqt, h, g_cq, wuqt, cos, sin)


def kv_bwd(dk, dv, h, g_ckv, wukt, wuvt, cos, sin):
    s = h.shape[0]
    tm = _tile(s, 256)

    def body(dk_ref, dv_ref, h_ref, g_ref, wk_ref, wv_ref, c_ref, s_ref, dkn_ref, dvb_ref, dc_ref, dkr_ref, dg_ref):
        @pl.when(pl.program_id(0) == 0)
        def _():
            dg_ref[...] = jnp.zeros_like(dg_ref)

        dkr = dk_ref[0, :, LANE:2 * LANE]
        for hd in range(HEADS):
            dkn_ref[:, LANE * hd:LANE * (hd + 1)] = dk_ref[hd, :, 0:LANE].astype(BF)
            dvb_ref[:, LANE * hd:LANE * (hd + 1)] = dv_ref[hd].T.astype(BF)
            if hd > 0:
                dkr = dkr + dk_ref[hd, :, LANE:2 * LANE]
        dkr_ref[...] = _unrope128(dkr, c_ref[...], s_ref[...]).astype(BF)
        dy = _dot(dkn_ref[...], wk_ref[...]) + _dot(dvb_ref[...], wv_ref[...])
        g = g_ref[...]
        _, xh, rr = _rms_fwd(h_ref[...], g)
        dg_ref[...] += _rows8(dy * xh)
        dc_ref[...] = _rms_bwd(dy, xh, rr, g).astype(BF)

    tab = pl.BlockSpec((tm, LANE), lambda i: (i, 0))
    wsp = pl.BlockSpec((MLA_W, R_KV), lambda i: (0, 0))
    wide = pl.BlockSpec((tm, MLA_W), lambda i: (i, 0))
    return _call(
        body, name="kv_bwd", grid=(s // tm,),
        in_specs=[pl.BlockSpec((HEADS, tm, 2 * LANE), lambda i: (0, i, 0)), pl.BlockSpec((HEADS, LANE, tm), lambda i: (0, 0, i)),
                  pl.BlockSpec((tm, R_KV), lambda i: (i, 1)), pl.BlockSpec((1, R_KV), lambda i: (0, 0)), wsp, wsp, tab, tab],
        out_specs=[wide, wide, pl.BlockSpec((tm, R_KV), lambda i: (i, 0)), tab, pl.BlockSpec((SUB, R_KV), lambda i: (0, 0))],
        out_shape=[_sds((s, MLA_W), BF), _sds((s, MLA_W), BF), _sds((s, R_KV), BF), _sds((s, LANE), BF), _sds((SUB, R_KV), F32)],
        compiler_params=_cp(("arbitrary",)),
    )(dk, dv, h, g_ckv, wukt, wuvt, cos, sin)


def in_proj_bwd_ln(dh, wint, dr1, x, g_in):
    s, hc = dh.shape
    d = x.shape[1]
    tm = _tile(s, 256)

    def body(a_ref, w_ref, d1_ref, x_ref, g_ref, gx_ref, dg_ref, db_ref, acc):
        @pl.when(pl.program_id(0) == 0)
        def _():
            dg_ref[...] = jnp.zeros_like(dg_ref)
            db_ref[...] = jnp.zeros_like(db_ref)

        acc[...] = _dot(a_ref[...], w_ref[...])
        g = g_ref[...]

        def chunk(rows):
            dy = ALPHA * d1_ref[rows, :] + acc[rows, :]
            xhat, rstd = _ln_stats(x_ref[rows, :])
            dg_ref[...] += _rows8(dy * xhat)
            db_ref[...] += _rows8(dy)
            gx_ref[rows, :] = _ln_bwd(dy, xhat, rstd, g)

        _row_chunks(tm, chunk)

    tok = pl.BlockSpec((tm, d), lambda i: (i, 0))
    accs = pl.BlockSpec((SUB, d), lambda i: (0, 0))
    return _call(
        body, name="in_proj_bwd_ln", grid=(s // tm,),
        in_specs=[pl.BlockSpec((tm, hc), lambda i: (i, 0)), pl.BlockSpec((hc, d), lambda i: (0, 0)),
                  tok, tok, pl.BlockSpec((1, d), lambda i: (0, 0))],
        out_specs=[tok, accs, accs], out_shape=[_sds((s, d), F32), _sds((SUB, d), F32), _sds((SUB, d), F32)],
        scratch_shapes=[pltpu.VMEM((tm, d), F32)], compiler_params=_cp(("arbitrary",), 56),
    )(dh, wint, dr1, x, g_in)


def _adamw_math(w, g, m, v):
    m = ADAM_B1 * m + (1.0 - ADAM_B1) * g
    v = ADAM_B2 * v + (1.0 - ADAM_B2) * (g * g)
    m_hat = m / (1.0 - ADAM_B1 ** ADAM_STEP)
    v_hat = v / (1.0 - ADAM_B2 ** ADAM_STEP)
    delta = -ADAM_LR * (m_hat / (jnp.sqrt(v_hat) + ADAM_EPS) + ADAM_WD * w)
    return delta, m, v


def adamw(name, w, g, m, v):
    r, c = w.shape
    tr = _row_tile(r, c)

    def body(w_ref, g_ref, m_ref, v_ref, d_ref, mo_ref, vo_ref):
        d_ref[...], mo_ref[...], vo_ref[...] = _adamw_math(w_ref[...], g_ref[...], m_ref[...], v_ref[...])

    blk = pl.BlockSpec((tr, c), lambda i: (i, 0))
    return _call(
        body, name=name, grid=(r // tr,), in_specs=[blk] * 4, out_specs=[blk] * 3,
        out_shape=[_sds((r, c), F32)] * 3, compiler_params=_cp(("arbitrary",)),
    )(w, g, m, v)


def _coords():
    return lax.axis_index("x"), lax.axis_index("y"), lax.axis_index("c")


def _other_chips(x, y):
    return [(1 - x, y, 2 * (1 - x) + y), (x, 1 - y, 2 * x + 1 - y), (1 - x, 1 - y, 2 * (1 - x) + 1 - y)]


ANY = pl.BlockSpec(memory_space=pl.ANY)
HBM = pl.BlockSpec(memory_space=pltpu.HBM)
SEM = pl.BlockSpec(memory_space=pltpu.SEMAPHORE)
EFFECT = pltpu.SideEffectType.DATAFLOW_SIDE_EFFECTING


def _in_hbm(a):
    return pltpu.with_memory_space_constraint(a, pltpu.HBM)


def _split_plan(mode, src, land, x, y, c):
    if mode == "pair":
        rh = src.shape[1] // 2
        return [((x, y, 1 - c), src.at[:, pl.ds((1 - c) * rh, rh)], land, land)]
    me = 2 * x + y
    plan = []
    for j, (px, py, pk) in enumerate(_other_chips(x, y)):
        if mode == "gather":
            plan.append(((px, py, c), src, land.at[me], land.at[pk]))
        else:
            plan.append(((px, py, c), src.at[pk], land.at[j], land.at[j]))
    return plan


def _plan_len(mode):
    return 1 if mode == "pair" else N_CHIP - 1


def split_send_start(name, mode, srcs, land_shapes, order_after):
    n = len(srcs)
    np_ = _plan_len(mode)

    def body(*refs):
        ins, lands = refs[:n], refs[n:2 * n]
        ss, rs = refs[2 * n + 1], refs[2 * n + 2]
        token = refs[-1]
        x, y, c = _coords()
        for a in range(n):
            for j, (peer, src, dst, _) in enumerate(_split_plan(mode, ins[a], lands[a], x, y, c)):
                pltpu.make_async_remote_copy(src_ref=src, dst_ref=dst, send_sem=ss.at[np_ * a + j], recv_sem=rs.at[np_ * a + j],
                                             device_id=peer, device_id_type=MESH).start()
        token[...] = jnp.zeros_like(token)

    lands = [lax.empty(shp, s.dtype) for shp, s in zip(land_shapes, srcs)]
    outs = _call(
        body, name=name,
        out_shape=(pltpu.SemaphoreType.DMA((np_ * n,)), pltpu.SemaphoreType.DMA((np_ * n,)),
                   *[pltpu.HBM(s.shape, s.dtype) for s in srcs], *[pltpu.HBM(l.shape, l.dtype) for l in lands],
                   _sds((SUB, LANE), F32)),
        in_specs=[HBM] * (2 * n) + [ANY], out_specs=(SEM, SEM, *[HBM] * (2 * n), pl.BlockSpec(memory_space=pltpu.VMEM)),
        input_output_aliases={a: 2 + a for a in range(2 * n)},
        compiler_params=pltpu.CompilerParams(has_side_effects=EFFECT),
    )(*[_in_hbm(s) for s in srcs], *[_in_hbm(l) for l in lands], order_after)
    return outs[0], outs[1], list(outs[2:2 + n]), list(outs[2 + n:2 + 2 * n]), outs[-1]


def split_send_wait(name, mode, ss, rs, srcs, lands, order_after):
    n = len(srcs)
    np_ = _plan_len(mode)

    def body(*refs):
        ins, lnd = refs[:n], refs[n:2 * n]
        s_ref, r_ref = refs[2 * n], refs[2 * n + 1]
        x, y, c = _coords()
        for a in range(n):
            for j, (peer, src, _, got) in enumerate(_split_plan(mode, ins[a], lnd[a], x, y, c)):
                cp = pltpu.make_async_remote_copy(src_ref=src, dst_ref=got, send_sem=s_ref.at[np_ * a + j], recv_sem=r_ref.at[np_ * a + j],
                                                  device_id=peer, device_id_type=MESH)
                cp.wait_send()
                cp.wait_recv()

    outs = _call(
        body, name=name, out_shape=tuple(pltpu.HBM(t.shape, t.dtype) for t in (*srcs, *lands)),
        in_specs=[HBM] * (2 * n) + [SEM, SEM, ANY], out_specs=tuple([HBM] * (2 * n)),
        input_output_aliases={a: a for a in range(2 * n)},
        compiler_params=pltpu.CompilerParams(has_side_effects=EFFECT),
    )(*srcs, *lands, ss, rs, order_after)
    return list(outs[:n]), list(outs[n:])


def all_gather_shards(shards):
    n = len(shards)

    def body(*refs):
        ins, outs = refs[:n], refs[n:2 * n]
        ici_s, ici_r, d2d_s, d2d_r = refs[2 * n:]
        x, y, c = _coords()
        me = 2 * x + y
        peers = _other_chips(x, y)
        sends, fwds = [], []
        for a in range(n):
            rh = ins[a].shape[0] // 2
            mine = pl.ds(c * rh, rh)
            for j, (px, py, pk) in enumerate(peers):
                cp = pltpu.make_async_remote_copy(
                    src_ref=ins[a].at[mine], dst_ref=outs[a].at[me, mine], send_sem=ici_s.at[a, j], recv_sem=ici_r.at[a, j],
                    device_id=(px, py, c), device_id_type=MESH)
                cp.start()
                sends.append(cp)
        for a in range(n):
            rh = ins[a].shape[0] // 2
            mine = pl.ds(c * rh, rh)
            for j, (px, py, pk) in enumerate(peers):
                got = outs[a].at[pk, mine]
                pltpu.make_async_remote_copy(
                    src_ref=got, dst_ref=got, send_sem=ici_s.at[a, j], recv_sem=ici_r.at[a, j],
                    device_id=(px, py, c), device_id_type=MESH).wait_recv()
                fw = pltpu.make_async_remote_copy(
                    src_ref=got, dst_ref=got, send_sem=d2d_s.at[a, j], recv_sem=d2d_r.at[a, j],
                    device_id=(x, y, 1 - c), device_id_type=MESH)
                fw.start()
                fwds.append(fw)
        for a in range(n):
            rh = ins[a].shape[0] // 2
            theirs = pl.ds((1 - c) * rh, rh)
            for j, (px, py, pk) in enumerate(peers):
                got = outs[a].at[pk, theirs]
                pltpu.make_async_remote_copy(
                    src_ref=got, dst_ref=got, send_sem=d2d_s.at[a, j], recv_sem=d2d_r.at[a, j],
                    device_id=(x, y, 1 - c), device_id_type=MESH).wait_recv()
        for cp in sends + fwds:
            cp.wait_send()

    got = _call(
        body, name="all_gather_shards", in_specs=[ANY] * n, out_specs=[ANY] * n,
        out_shape=[_sds((N_CHIP,) + w.shape, w.dtype) for w in shards],
        scratch_shapes=[pltpu.SemaphoreType.DMA((n, 3))] * 4,
    )(*shards)
    me = 2 * lax.axis_index("x") + lax.axis_index("y")
    return [lax.dynamic_update_slice(g, w[None], (me, 0, 0)) for g, w in zip(got, shards)]


def pair_exchange(grads, tag):
    n = len(grads)

    def body(*refs):
        ins, outs = refs[:n], refs[n:2 * n]
        ss, rs = refs[2 * n:]
        x, y, c = _coords()
        cps = []
        for a in range(n):
            rh = ins[a].shape[1] // 2
            cp = pltpu.make_async_remote_copy(
                src_ref=ins[a].at[:, pl.ds((1 - c) * rh, rh)], dst_ref=outs[a], send_sem=ss.at[a], recv_sem=rs.at[a],
                device_id=(x, y, 1 - c), device_id_type=MESH)
            cp.start()
            cps.append(cp)
        for cp in cps:
            cp.wait()

    return _call(
        body, name="pair_exchange_" + tag, in_specs=[ANY] * n, out_specs=[ANY] * n,
        out_shape=[_sds((N_CHIP, g.shape[1] // 2, g.shape[2]), F32) for g in grads],
        scratch_shapes=[pltpu.SemaphoreType.DMA((n,))] * 2,
    )(*grads)


def _row_tile(rows, cols, itemsize=4, budget=2 * VMEM_MB):
    fits = [t for t in range(SUB, rows + 1, SUB) if rows % t == 0 and t * cols * itemsize <= budget]
    return max(fits) if fits and rows * cols * itemsize > budget else rows


def pair_add(g, r, cidx):
    _, rows, cols = g.shape
    rh = rows // 2
    tr = _row_tile(rh, cols)
    per = rh // tr

    def body(c_ref, g_ref, r_ref, o_ref):
        o_ref[...] = g_ref[...] + r_ref[...]

    return _call(
        body, name="pair_add",
        grid_spec=pltpu.PrefetchScalarGridSpec(
            num_scalar_prefetch=1, grid=(N_CHIP, per),
            in_specs=[pl.BlockSpec((None, tr, cols), lambda k, i, c: (k, c[0] * per + i, 0)),
                      pl.BlockSpec((None, tr, cols), lambda k, i, c: (k, i, 0))],
            out_specs=pl.BlockSpec((None, tr, cols), lambda k, i, c: (k, i, 0))),
        out_shape=_sds((N_CHIP, rh, cols), F32), compiler_params=_cp(("arbitrary", "arbitrary")),
    )(cidx, g, r)


def chip_add(p, r, kc):
    _, rh, cols = p.shape
    tr = _row_tile(rh, cols)
    per = rh // tr

    def body(k_ref, p_ref, r_ref, o_ref):
        o_ref[...] = ((p_ref[...] + r_ref[0]) + r_ref[1]) + r_ref[2]

    return _call(
        body, name="chip_add",
        grid_spec=pltpu.PrefetchScalarGridSpec(
            num_scalar_prefetch=1, grid=(per,),
            in_specs=[pl.BlockSpec((None, tr, cols), lambda i, k: (k[0], i, 0)),
                      pl.BlockSpec((N_CHIP - 1, tr, cols), lambda i, k: (0, i, 0))],
            out_specs=pl.BlockSpec((tr, cols), lambda i, k: (k[1] * per + i, 0))),
        out_shape=_sds((2 * rh, cols), F32), compiler_params=_cp(("arbitrary",)),
    )(kc, p, r)


def pair_share(fulls, tag):
    n = len(fulls)

    def body(*refs):
        outs = refs[n:2 * n]
        ss, rs = refs[2 * n:]
        x, y, c = _coords()
        cps = []
        for a in range(n):
            rh = outs[a].shape[0] // 2
            mine = outs[a].at[pl.ds(c * rh, rh)]
            cp = pltpu.make_async_remote_copy(
                src_ref=mine, dst_ref=mine, send_sem=ss.at[a], recv_sem=rs.at[a],
                device_id=(x, y, 1 - c), device_id_type=MESH)
            cp.start()
            cps.append(cp)
        for a, cp in enumerate(cps):
            rh = outs[a].shape[0] // 2
            theirs = outs[a].at[pl.ds((1 - c) * rh, rh)]
            cp.wait_send()
            pltpu.make_async_remote_copy(
                src_ref=theirs, dst_ref=theirs, send_sem=ss.at[a], recv_sem=rs.at[a],
                device_id=(x, y, 1 - c), device_id_type=MESH).wait_recv()

    return _call(
        body, name="pair_share_" + tag, in_specs=[ANY] * n, out_specs=[ANY] * n,
        out_shape=[_sds(f.shape, F32) for f in fulls], input_output_aliases={a: a for a in range(n)},
        scratch_shapes=[pltpu.SemaphoreType.DMA((n,))] * 2,
    )(*fulls)


def small_allreduce_adamw(part, w, m, v):
    n = part.shape[1]

    def body(p_ref, w_ref, m_ref, v_ref, g_ref, d_ref, mo_ref, vo_ref, mine, gath, ss, rs):
        x, y, c = _coords()
        me = 4 * x + 2 * y + c
        mine[...] = jnp.sum(p_ref[...], axis=0, keepdims=True)
        gath[me] = mine[...]
        cps = []
        for k in range(1, 8):
            px, py, pc = x ^ (k >> 2), y ^ ((k >> 1) & 1), c ^ (k & 1)
            cp = pltpu.make_async_remote_copy(
                src_ref=mine, dst_ref=gath.at[me], send_sem=ss.at[k - 1], recv_sem=rs.at[k - 1],
                device_id=(px, py, pc), device_id_type=MESH)
            cp.start()
            cps.append(cp)
        for k in range(1, 8):
            src = 4 * (x ^ (k >> 2)) + 2 * (y ^ ((k >> 1) & 1)) + (c ^ (k & 1))
            pltpu.make_async_remote_copy(
                src_ref=mine, dst_ref=gath.at[src], send_sem=ss.at[k - 1], recv_sem=rs.at[k - 1],
                device_id=(x, y, c), device_id_type=MESH).wait_recv()
        for cp in cps:
            cp.wait_send()
        g = gath[0]
        for dv in range(1, 8):
            g = g + gath[dv]
        g_ref[...] = g
        d_ref[...], mo_ref[...], vo_ref[...] = _adamw_math(w_ref[...], g, m_ref[...], v_ref[...])

    vm = pl.BlockSpec(memory_space=pltpu.VMEM)
    return _call(
        body, name="small_allreduce_adamw", in_specs=[vm] * 4, out_specs=[vm] * 4, out_shape=[_sds((1, n), F32)] * 4,
        scratch_shapes=[pltpu.VMEM((1, n), F32), pltpu.VMEM((8, 1, n), F32),
                        pltpu.SemaphoreType.DMA((7,)), pltpu.SemaphoreType.DMA((7,))],
    )(part, w, m, v)


def _unshard_cols(g):
    k, r, cs = g.shape
    return g.transpose(1, 0, 2).reshape(r, k * cs)


def _shard_cols(w):
    r, c = w.shape
    return w.reshape(r, N_CHIP, c // N_CHIP).transpose(1, 0, 2)


def local_step(x, positions, ln_in_g, ln_in_b, win_g, g_cq, wuq_g, g_ckv, wuk_g, wuv_g, convw_g, conv_b, g_conv_ln,
               b_conv_ln, g_ln1, b_ln1, g_ln2, b_ln2, target, start_token, hooks):
    s, d = x.shape
    c = d - MLA_W
    row = lambda a: a.reshape(1, -1)
    ln_in_g = row(ln_in_g) + start_token[0:1, 0:1]

    o_kr = R_Q + R_KV
    o_cv = o_kr + D_ROPE
    n_in = o_cv + 2 * c
    per = n_in // N_CHIP

    def in_cols(a, b):
        return [win_g[k, max(a, per * k) - per * k:min(b, per * (k + 1)) - per * k]
                for k in range(N_CHIP) if max(a, per * k) < min(b, per * (k + 1))]

    win_rt = jnp.concatenate(in_cols(0, o_kr) + in_cols(o_cv, n_in) + in_cols(o_kr, o_cv)
                             + [jnp.zeros((LANE - D_ROPE, d), BF)], axis=0)
    kr_blk = (o_kr + 2 * c) // LANE
    wuq = _unshard_cols(wuq_g).reshape(R_Q, HEADS, D_QK)
    wuq_r = jnp.concatenate([wuq[:, :, :D_NOPE].reshape(R_Q, MLA_W),
                             jnp.pad(wuq[:, :, D_NOPE:], ((0, 0), (0, 0), (0, LANE - D_ROPE))).reshape(R_Q, MLA_W)], axis=1)
    wuk = _unshard_cols(wuk_g)
    wuv = _unshard_cols(wuv_g)
    conv_w = jnp.pad(_unshard_cols(convw_g), ((0, 1), (0, 0)))

    half = D_ROPE // 2
    inv_freq = ROPE_BASE ** (-jnp.arange(half, dtype=F32) * (2.0 / D_ROPE))
    invf = jnp.concatenate([inv_freq, inv_freq, jnp.zeros((LANE - D_ROPE,), F32)]).reshape(1, LANE)
    cos, sin = rope_tables(positions.astype(F32).reshape(s, 1), invf)
    x0, x0b = ln_in_fwd(x, ln_in_g, row(ln_in_b))
    h = matmul_nt("in_proj", x0b, win_rt, 1024, 640)
    qc, cqn = q_proj(h, g_cq, wuq_r, cos, sin)
    kc, kct, v, ckvn = kv_proj(h, g_ckv, wuk, wuv, cos, sin, kr_blk)
    o, ob, lse = attn_fwd(qc, kc, v)
    co, uc = conv_fwd(h, conv_w, conv_b, g_conv_ln, b_conv_ln)
    wout_g, wff1_g, wff2_g = hooks.late_weights(ob)
    wout = wout_g.reshape(d, d)
    wff2 = wff2_g.reshape(-1, d)
    r1, x1, x1b = out_proj_ln1(ob, co, wout, x0, g_ln1, b_ln1)
    rb, a1b = ff1_fwd(x1b, wff1_g)
    dr2, dr2b, loss8, dg2, db2 = ff2_ln2_loss(a1b, wff2, x1, target, g_ln2, b_ln2)

    df1b = ff2_bwd_act(dr2b, wff2, rb)
    gw_ff2 = wgrad("wgrad_ff2", a1b, dr2b, 1024, 1024).reshape(N_CHIP, -1, d)
    gw_ff1 = wgrad("wgrad_ff1", x1b, df1b, 1024, 1024, shards=N_CHIP)
    tok = hooks.ff_grads(gw_ff2, gw_ff1)
    dr1, dr1b, dg1, db1 = ff1_bwd_ln1(df1b, wff1_g, dr2, r1, g_ln1 + tok[0:1, 0:1])
    tok = hooks.ff_grads_mid(dr1b)
    gw_out = jnp.concatenate([wgrad("wgrad_out_attn", ob, dr1b, 1024, 1024)[0],
                              wgrad("wgrad_out_conv", co, dr1b, 1024, 1024)[0]], axis=0).reshape(N_CHIP, -1, d)
    dob, dot, dco, delta = out_proj_bwd(dr1b, wout.T, o)
    duc, dgc, dbc, dcb = conv_bwd_ln(uc, dco, g_conv_ln + tok[0:1, 0:1], b_conv_ln)
    dconv, gconvw = conv_bwd_taps(h, duc, conv_w)
    dqt, dk, dv = attn_bwd(qc, kc, kct, v, dob, dot, lse, delta)
    dqb, dcq, dgq = q_bwd(dqt, h, g_cq, wuq_r.T, cos, sin)
    dknb, dvb, dckv, dkr, dgkv = kv_bwd(dk, dv, h, g_ckv, wuk.T, wuv.T, cos, sin)
    gwuq_r = wgrad("wgrad_uq", cqn, dqb, 512, 1024)[0]
    gw_uk = wgrad("wgrad_uk", ckvn, dknb, 512, 1024, shards=N_CHIP)
    gw_uv = wgrad("wgrad_uv", ckvn, dvb, 512, 1024, shards=N_CHIP)
    dh = jnp.concatenate([dcq, dckv, dconv, dkr], axis=1)
    gwin_rt = wgrad("wgrad_in", dh, x0b, 640, 1024)[0]

    gwin_t = jnp.concatenate([gwin_rt[:o_kr], gwin_rt[o_kr + 2 * c:o_kr + 2 * c + D_ROPE], gwin_rt[o_kr:o_kr + 2 * c]], axis=0)
    gwin_t = jnp.pad(gwin_t.reshape(N_CHIP, per, d), ((0, 0), (0, win_g.shape[1] - per), (0, 0)))
    gwuq = jnp.concatenate([gwuq_r[:, :MLA_W].reshape(R_Q, HEADS, D_NOPE),
                            gwuq_r[:, MLA_W:].reshape(R_Q, HEADS, LANE)[:, :, :D_ROPE]], axis=2).reshape(R_Q, HEADS * D_QK)
    tok = hooks.rest_grads(dict(w_in=gwin_t, w_uq=_shard_cols(gwuq), w_uk=gw_uk, w_uv=gw_uv,
                                conv_w=_shard_cols(gconvw), w_out=gw_out))
    gx, dgin, dbin = in_proj_bwd_ln(dh, win_rt, dr1, x, ln_in_g + tok[0:1, 0:1])
    small = jnp.concatenate([dgin, dbin, dgq, dgkv, dcb, dgc, dbc, dg1, db1, dg2, db2, loss8], axis=1)
    return gx, small


BIG = ["w_in", "w_uq", "w_uk", "w_uv", "conv_w", "w_out", "w_ff1", "w_ff2"]
EARLY = ["w_in", "w_uq", "w_uk", "w_uv", "conv_w"]
LATE = ["w_out", "w_ff1", "w_ff2"]
SMALL = ["ln_in_g", "ln_in_b", "g_cq", "g_ckv", "conv_b", "g_conv_ln", "b_conv_ln", "g_ln1", "b_ln1", "g_ln2", "b_ln2"]
WEIGHTS = ["ln_in_g", "ln_in_b", "w_in", "g_cq", "w_uq", "g_ckv", "w_uk", "w_uv", "conv_w", "conv_b", "g_conv_ln",
           "b_conv_ln", "w_out", "g_ln1", "b_ln1", "w_ff1", "w_ff2", "g_ln2", "b_ln2"]


def _pad_rows(a, rows):
    return jnp.pad(a, ((0, rows - a.shape[0]), (0, 0)))


def kernel(x, positions, ln_in_g, ln_in_b, w_in, g_cq, w_uq, g_ckv, w_uk, w_uv, conv_w, conv_b, g_conv_ln, b_conv_ln, w_out, g_ln1, b_ln1, w_ff1, w_ff2, g_ln2, b_ln2, loss_target, m_ln_in_g, m_ln_in_b, m_w_in, m_g_cq, m_w_uq, m_g_ckv, m_w_uk, m_w_uv, m_conv_w, m_conv_b, m_g_conv_ln, m_b_conv_ln, m_w_out, m_g_ln1, m_b_ln1, m_w_ff1, m_w_ff2, m_g_ln2, m_b_ln2, v_ln_in_g, v_ln_in_b, v_w_in, v_g_cq, v_w_uq, v_g_ckv, v_w_uk, v_w_uv, v_conv_w, v_conv_b, v_g_conv_ln, v_b_conv_ln, v_w_out, v_g_ln1, v_b_ln1, v_w_ff1, v_w_ff2, v_g_ln2, v_b_ln2):
    w = dict(ln_in_g=ln_in_g, ln_in_b=ln_in_b, w_in=w_in, g_cq=g_cq, w_uq=w_uq, g_ckv=g_ckv, w_uk=w_uk, w_uv=w_uv,
             conv_w=conv_w, conv_b=conv_b, g_conv_ln=g_conv_ln, b_conv_ln=b_conv_ln, w_out=w_out, g_ln1=g_ln1,
             b_ln1=b_ln1, w_ff1=w_ff1, w_ff2=w_ff2, g_ln2=g_ln2, b_ln2=b_ln2)
    m = dict(ln_in_g=m_ln_in_g, ln_in_b=m_ln_in_b, w_in=m_w_in, g_cq=m_g_cq, w_uq=m_w_uq, g_ckv=m_g_ckv, w_uk=m_w_uk,
             w_uv=m_w_uv, conv_w=m_conv_w, conv_b=m_conv_b, g_conv_ln=m_g_conv_ln, b_conv_ln=m_b_conv_ln, w_out=m_w_out,
             g_ln1=m_g_ln1, b_ln1=m_b_ln1, w_ff1=m_w_ff1, w_ff2=m_w_ff2, g_ln2=m_g_ln2, b_ln2=m_b_ln2)
    v = dict(ln_in_g=v_ln_in_g, ln_in_b=v_ln_in_b, w_in=v_w_in, g_cq=v_g_cq, w_uq=v_w_uq, g_ckv=v_g_ckv, w_uk=v_w_uk,
             w_uv=v_w_uv, conv_w=v_conv_w, conv_b=v_conv_b, g_conv_ln=v_g_conv_ln, b_conv_ln=v_b_conv_ln, w_out=v_w_out,
             g_ln1=v_g_ln1, b_ln1=v_b_ln1, w_ff1=v_w_ff1, w_ff2=v_w_ff2, g_ln2=v_g_ln2, b_ln2=v_b_ln2)

    as2d = lambda t, n: t[n][0].T if n == "w_in" else t[n][0]
    sh2 = {n: as2d(w, n) for n in BIG}
    cidx = lax.axis_index("c").astype(jnp.int32).reshape(1)
    me = 2 * lax.axis_index("x") + lax.axis_index("y")
    kc = jnp.stack([me, lax.axis_index("c")]).astype(jnp.int32)

    pad_to = {"conv_w": CONV_K + 1, "w_in": -(-sh2["w_in"].shape[0] // (4 * SUB)) * (4 * SUB)}
    early = [_pad_rows(sh2[n] if n == "conv_w" else sh2[n].astype(BF), pad_to.get(n, sh2[n].shape[0])) for n in EARLY]
    gw = dict(zip(EARLY, all_gather_shards(early)))
    gw["conv_w"] = gw["conv_w"][:, :CONV_K]
    late = [sh2[n].astype(BF) for n in LATE]
    ag = split_send_start("late_weights_start", "gather", late, [(N_CHIP,) + a.shape for a in late], gw["w_uq"])
    rest = [n for n in BIG if n not in ("w_ff2", "w_ff1")]
    flight = {}

    class Hooks:
        @staticmethod
        def late_weights(after):
            mine, lands = split_send_wait("late_weights_wait", "gather", *ag[:4], after)
            return [lax.dynamic_update_slice(g, a[None], (me, 0, 0)) for g, a in zip(lands, mine)]

        @staticmethod
        def ff_grads(gw_ff2, gw_ff1):
            full = [gw_ff2, gw_ff1]
            st = split_send_start("ff_pair_start", "pair", full, [(N_CHIP, g.shape[1] // 2, g.shape[2]) for g in full], ag[4])
            flight["ff_pair"] = st[:4]
            flight["token"] = st[4]
            return st[4]

        @staticmethod
        def ff_grads_mid(after):
            full, recv = split_send_wait("ff_pair_wait", "pair", *flight["ff_pair"], after)
            psum = [pair_add(g, r, cidx) for g, r in zip(full, recv)]
            st = split_send_start("ff_grads_start", "scatter", psum, [(N_CHIP - 1,) + p.shape[1:] for p in psum], flight["token"])
            flight["ff"] = st[:4]
            flight["token"] = st[4]
            return st[4]

        @staticmethod
        def rest_grads(big):
            full = [big[n] for n in rest]
            psum = [pair_add(g, r, cidx) for g, r in zip(full, pair_exchange(full, "rest"))]
            st = split_send_start("rest_grads_start", "scatter", psum, [(N_CHIP - 1,) + p.shape[1:] for p in psum], flight["token"])
            flight["rest"] = st[:4]
            return st[4]

    gx, small = local_step(
        x[0], positions[0], ln_in_g, ln_in_b, gw["w_in"], g_cq, gw["w_uq"], g_ckv, gw["w_uk"], gw["w_uv"], gw["conv_w"],
        conv_b, g_conv_ln, b_conv_ln, g_ln1, b_ln1, g_ln2, b_ln2, loss_target[0], ag[4], Hooks)

    ff_psum, ff_got = split_send_wait("ff_grads_wait", "scatter", *flight["ff"], gx)
    rest_psum, rest_got = split_send_wait("rest_grads_wait", "scatter", *flight["rest"], gx)
    summed = [chip_add(p, r, kc) for p, r in zip(rest_psum + ff_psum, rest_got + ff_got)]
    gsh = dict(zip(rest + ["w_ff2", "w_ff1"], pair_share(summed, "all")))
    for n in pad_to:
        gsh[n] = gsh[n][:sh2[n].shape[0]]

    grad, delta, new_m, new_v = {}, {}, {}, {}
    for n in BIG:
        back = (lambda a: a.T[None]) if n == "w_in" else (lambda a: a[None])
        d_, m_, v_ = adamw("adamw_" + n, sh2[n], gsh[n], as2d(m, n), as2d(v, n))
        grad[n], delta[n], new_m[n], new_v[n] = back(gsh[n]), back(d_), back(m_), back(v_)

    flat = lambda t: jnp.concatenate([t[n].reshape(1, -1) for n in SMALL] + [jnp.zeros((1, LANE), F32)], axis=1)
    g_s, d_s, m_s, v_s = small_allreduce_adamw(small, flat(w), flat(m), flat(v))
    off = 0
    for n in SMALL:
        sz = w[n].size
        for dst, src in ((grad, g_s), (delta, d_s), (new_m, m_s), (new_v, v_s)):
            dst[n] = src[0, off:off + sz].reshape(w[n].shape)
        off += sz
    loss = jnp.sum(g_s[0, off:off + LANE])

    return (loss, gx[None], *[grad[n] for n in WEIGHTS], *[delta[n] for n in WEIGHTS],
            *[new_m[n] for n in WEIGHTS], *[new_v[n] for n in WEIGHTS])
```

```python
import functools

import jax
import jax.numpy as jnp
from jax import lax
from jax.experimental import pallas as pl
from jax.experimental.pallas import tpu as pltpu

F32 = jnp.float32
BF = jnp.bfloat16

HEADS = 8
D_NOPE = 128
D_ROPE = 64
D_V = 128
D_QK = D_NOPE + D_ROPE
R_Q = 512
R_KV = 512
MLA_W = HEADS * D_V
CONV_K = 31
CONV_PAD = CONV_K // 2
ROPE_BASE = 10000.0
LOG2E = 1.4426950408889634
LN2 = 0.6931471805599453
LN_EPS = 1e-5
RMS_EPS = 1e-6
ALPHA = (2.0 * 1) ** 0.25
ADAM_LR = 0.001
ADAM_B1 = 0.9
ADAM_B2 = 0.999
ADAM_EPS = 1e-08
ADAM_WD = 0.01
ADAM_STEP = 10

LANE = 128
SUB = 8
HALO = 16
N_CHIP = 4
MESH = pl.DeviceIdType.MESH
VMEM_MB = 1024 * 1024


def _call(body, **kw):
    return pl.pallas_call(body, **kw)


def _cp(sem, mb=48):
    return pltpu.CompilerParams(dimension_semantics=sem, vmem_limit_bytes=mb * VMEM_MB)


def _sds(shape, dt):
    return jax.ShapeDtypeStruct(shape, dt)


def _dot(a, b):
    return jnp.dot(a, b, preferred_element_type=F32)


def _dot_nt(a, b):
    return lax.dot_general(a, b, (((1,), (1,)), ((), ())), preferred_element_type=F32)


def _dot_tn(a, b):
    return lax.dot_general(a, b, (((0,), (0,)), ((), ())), preferred_element_type=F32)


def _rows8(v):
    t, n = v.shape
    return v.reshape(t // SUB, SUB, n).sum(axis=0)


def _ln_stats(r):
    mu = jnp.mean(r, axis=-1, keepdims=True)
    xc = r - mu
    var = jnp.mean(xc * xc, axis=-1, keepdims=True)
    rstd = lax.rsqrt(var + LN_EPS)
    return xc * rstd, rstd


def _ln_bwd(dy, xhat, rstd, g):
    dyh = dy * g
    m1 = jnp.mean(dyh, axis=-1, keepdims=True)
    m2 = jnp.mean(dyh * xhat, axis=-1, keepdims=True)
    return rstd * (dyh - m1 - xhat * m2)


def _rms_fwd(x, g):
    rr = lax.rsqrt(jnp.mean(x * x, axis=-1, keepdims=True) + RMS_EPS)
    xh = x * rr
    return xh * g, xh, rr


def _rms_bwd(dy, xh, rr, g):
    dyg = dy * g
    return rr * (dyg - xh * jnp.mean(dyg * xh, axis=-1, keepdims=True))


def _rope128(x, cos, sin_signed):
    lane = lax.broadcasted_iota(jnp.int32, x.shape, 1)
    rot = jnp.where(lane < D_ROPE // 2, pltpu.roll(x, LANE - D_ROPE // 2, 1), pltpu.roll(x, D_ROPE // 2, 1))
    return x * cos + rot * sin_signed


def _unrope128(dy, cos, sin_signed):
    t = dy * sin_signed
    lane = lax.broadcasted_iota(jnp.int32, dy.shape, 1)
    rot = jnp.where(lane < D_ROPE // 2, pltpu.roll(t, LANE - D_ROPE // 2, 1), pltpu.roll(t, D_ROPE // 2, 1))
    return dy * cos + rot


def _as_row(col):
    return jnp.transpose(jnp.broadcast_to(col, (col.shape[0], LANE)))[0:1, :]


def _sigmoid(x):
    return 1.0 / (1.0 + jnp.exp(-x))


def _row_chunks(tm, fn, rc=128):
    rc = min(rc, tm)

    def step(ci, carry):
        fn(pl.ds(pl.multiple_of(ci * rc, rc), rc))
        return carry

    lax.fori_loop(0, tm // rc, step, 0)


def _unrolled_loop(n, unroll, fn, init):
    unroll = min(n, unroll)
    assert n % unroll == 0

    def body(t, carry):
        for u in range(unroll):
            carry = fn(t * unroll + u, carry)
        return carry

    return lax.fori_loop(0, n // unroll, body, init)


def _tile(s, want):
    t = min(s, want)
    assert s % t == 0
    return t


def rope_tables(pos_f, invf):
    s = pos_f.shape[0]
    tm = _tile(s, 1024)

    def body(p_ref, f_ref, c_ref, s_ref):
        ang = p_ref[...] * f_ref[...]
        lane = lax.broadcasted_iota(jnp.int32, ang.shape, 1)
        c = jnp.cos(ang)
        sn = jnp.sin(ang)
        c_ref[...] = jnp.where(lane < D_ROPE, c, 0.0)
        s_ref[...] = jnp.where(lane < D_ROPE // 2, -sn, jnp.where(lane < D_ROPE, sn, 0.0))

    return _call(
        body, name="rope_tables", grid=(s // tm,),
        in_specs=[pl.BlockSpec((tm, 1), lambda i: (i, 0)), pl.BlockSpec((1, LANE), lambda i: (0, 0))],
        out_specs=[pl.BlockSpec((tm, LANE), lambda i: (i, 0))] * 2,
        out_shape=[_sds((s, LANE), F32)] * 2,
        compiler_params=_cp(("arbitrary",)),
    )(pos_f, invf)


def ln_in_fwd(x, g, b):
    s, d = x.shape
    tm = _tile(s, 512)

    def body(x_ref, g_ref, b_ref, o_ref, ob_ref):
        xhat, _ = _ln_stats(x_ref[...])
        y = xhat * g_ref[...] + b_ref[...]
        o_ref[...] = y
        ob_ref[...] = y.astype(BF)

    row = pl.BlockSpec((1, d), lambda i: (0, 0))
    tok = pl.BlockSpec((tm, d), lambda i: (i, 0))
    return _call(
        body, name="ln_in_fwd", grid=(s // tm,), in_specs=[tok, row, row], out_specs=[tok, tok],
        out_shape=[_sds((s, d), F32), _sds((s, d), BF)], compiler_params=_cp(("arbitrary",)),
    )(x, g, b)


def matmul_nt(name, a, wt, tm, out_dtype=F32):
    s, k = a.shape
    n = wt.shape[0]
    tm = _tile(s, tm)

    def body(a_ref, w_ref, o_ref):
        o_ref[...] = _dot_nt(a_ref[...], w_ref[...]).astype(o_ref.dtype)

    return _call(
        body, name=name, grid=(s // tm,),
        in_specs=[pl.BlockSpec((tm, k), lambda i: (i, 0)), pl.BlockSpec((n, k), lambda i: (0, 0))],
        out_specs=pl.BlockSpec((tm, n), lambda i: (i, 0)),
        out_shape=_sds((s, n), out_dtype), compiler_params=_cp(("arbitrary",)),
    )(a, wt)


def q_proj(h, g_cq, wuq, cos, sin):
    s = h.shape[0]
    tm = _tile(s, 512)

    def body(h_ref, g_ref, w_ref, c_ref, s_ref, q_ref, n_ref):
        y, _, _ = _rms_fwd(h_ref[...], g_ref[...])
        yb = y.astype(BF)
        n_ref[...] = yb
        q = _dot(yb, w_ref[...])
        c = c_ref[...]
        sn = s_ref[...]
        for hd in range(HEADS):
            q_ref[hd, :, 0:LANE] = q[:, LANE * hd:LANE * (hd + 1)].astype(BF)
            qr = q[:, MLA_W + LANE * hd:MLA_W + LANE * (hd + 1)]
            q_ref[hd, :, LANE:2 * LANE] = _rope128(qr, c, sn).astype(BF)

    return _call(
        body, name="q_proj", grid=(s // tm,),
        in_specs=[pl.BlockSpec((tm, R_Q), lambda i: (i, 0)), pl.BlockSpec((1, R_Q), lambda i: (0, 0)),
                  pl.BlockSpec((R_Q, 2 * MLA_W), lambda i: (0, 0)),
                  pl.BlockSpec((tm, LANE), lambda i: (i, 0)), pl.BlockSpec((tm, LANE), lambda i: (i, 0))],
        out_specs=[pl.BlockSpec((HEADS, tm, 2 * LANE), lambda i: (0, i, 0)), pl.BlockSpec((tm, R_Q), lambda i: (i, 0))],
        out_shape=[_sds((HEADS, s, 2 * LANE), BF), _sds((s, R_Q), BF)], compiler_params=_cp(("arbitrary",)),
    )(h, g_cq, wuq, cos, sin)


def kv_proj(h, g_ckv, wuk, wuv, cos, sin, kr_blk):
    s = h.shape[0]
    tm = _tile(s, 512)

    def body(h_ref, kr_ref, g_ref, wk_ref, wv_ref, c_ref, s_ref, k_ref, kt_ref, v_ref, n_ref):
        y, _, _ = _rms_fwd(h_ref[...], g_ref[...])
        yb = y.astype(BF)
        n_ref[...] = yb
        kn = _dot(yb, wk_ref[...])
        v = _dot(yb, wv_ref[...])
        kr = _rope128(kr_ref[...], c_ref[...], s_ref[...])
        krb = kr.astype(BF)
        krt = kr.T.astype(BF)
        for hd in range(HEADS):
            knh = kn[:, LANE * hd:LANE * (hd + 1)]
            k_ref[hd, :, 0:LANE] = knh.astype(BF)
            k_ref[hd, :, LANE:2 * LANE] = krb
            kt_ref[hd, 0:LANE, :] = knh.T.astype(BF)
            kt_ref[hd, LANE:2 * LANE, :] = krt
            v_ref[hd] = v[:, LANE * hd:LANE * (hd + 1)].astype(BF)

    tab = pl.BlockSpec((tm, LANE), lambda i: (i, 0))
    wsp = pl.BlockSpec((R_KV, MLA_W), lambda i: (0, 0))
    return _call(
        body, name="kv_proj", grid=(s // tm,),
        in_specs=[pl.BlockSpec((tm, R_KV), lambda i: (i, 1)), pl.BlockSpec((tm, LANE), lambda i: (i, kr_blk)),
                  pl.BlockSpec((1, R_KV), lambda i: (0, 0)), wsp, wsp, tab, tab],
        out_specs=[pl.BlockSpec((HEADS, tm, 2 * LANE), lambda i: (0, i, 0)), pl.BlockSpec((HEADS, 2 * LANE, tm), lambda i: (0, 0, i)),
                   pl.BlockSpec((HEADS, tm, LANE), lambda i: (0, i, 0)), pl.BlockSpec((tm, R_KV), lambda i: (i, 0))],
        out_shape=[_sds((HEADS, s, 2 * LANE), BF), _sds((HEADS, 2 * LANE, s), BF), _sds((HEADS, s, LANE), BF), _sds((s, R_KV), BF)],
        compiler_params=_cp(("arbitrary",)),
    )(h, h, g_ckv, wuk, wuv, cos, sin)


def attn_fwd(qc, kc, v):
    _, s, _ = qc.shape
    tq = _tile(s, 256)
    tk = _tile(s, 512)
    scale = D_QK ** -0.5
    c2 = scale * LOG2E
    nk = s // tk
    nb = tk // LANE
    un = 8

    def body(q_ref, k_ref, v_ref, o_ref, ob_ref, l_ref, s_scr, m_scr):
        q = q_ref[...]

        def scores(j, mpart):
            off = pl.multiple_of(j * tk, tk)
            sc = _dot_nt(q, k_ref[pl.ds(off, tk), :]) * c2
            s_scr[:, pl.ds(off, tk)] = sc
            for b in range(nb):
                mpart = jnp.maximum(mpart, sc[:, LANE * b:LANE * (b + 1)])
            return mpart

        mpart = _unrolled_loop(nk, un, scores, jnp.full((tq, LANE), -jnp.inf, F32))
        m = jnp.max(mpart, axis=-1, keepdims=True)
        m_scr[...] = jnp.broadcast_to(m, (tq, LANE))

        def weigh(j, carry):
            lpart, acc = carry
            off = pl.multiple_of(j * tk, tk)
            ps = []
            for b in range(nb):
                p = jnp.exp2(s_scr[:, pl.ds(off + LANE * b, LANE)] - m_scr[...])
                lpart = lpart + p
                ps.append(p.astype(BF))
            acc = acc + _dot(jnp.concatenate(ps, axis=1), v_ref[pl.ds(off, tk), :])
            return lpart, acc

        lpart, acc = _unrolled_loop(nk, un, weigh, (jnp.zeros((tq, LANE), F32), jnp.zeros((tq, D_V), F32)))
        l = jnp.sum(lpart, axis=-1, keepdims=True)
        o = acc / l
        o_ref[...] = o
        ob_ref[...] = o.astype(BF)
        l_ref[...] = _as_row(m + jnp.log(l) * LOG2E)

    return _call(
        body, name="attn_fwd", grid=(HEADS, s // tq),
        in_specs=[pl.BlockSpec((None, tq, 2 * LANE), lambda h, i: (h, i, 0)),
                  pl.BlockSpec((None, s, 2 * LANE), lambda h, i: (h, 0, 0)),
                  pl.BlockSpec((None, s, LANE), lambda h, i: (h, 0, 0))],
        out_specs=[pl.BlockSpec((tq, LANE), lambda h, i: (i, h)), pl.BlockSpec((tq, LANE), lambda h, i: (i, h)),
                   pl.BlockSpec((None, 1, tq), lambda h, i: (h, 0, i))],
        out_shape=[_sds((s, MLA_W), F32), _sds((s, MLA_W), BF), _sds((HEADS, 1, s), F32)],
        scratch_shapes=[pltpu.VMEM((tq, s + LANE), F32), pltpu.VMEM((tq, LANE), F32)],
        compiler_params=_cp(("arbitrary", "arbitrary")),
    )(qc, kc, v)


def _halo_specs(tm, s, width, col):
    r = tm // HALO
    nb = s // HALO
    cur = pl.BlockSpec((tm, width), lambda i: (i, col))
    prev = pl.BlockSpec((HALO, width), lambda i: (jnp.maximum(i * r - 1, 0), col))
    nxt = pl.BlockSpec((HALO, width), lambda i: (jnp.minimum((i + 1) * r, nb - 1), col))
    return cur, prev, nxt


def _slab_shapes(tm, c):
    return (tm + 2 * HALO, c + LANE), (SUB - 1, tm + 2 * HALO - SUB, c + LANE)


def _fill_slab(slab, tm, prev, cur, nxt):
    i = pl.program_id(0)
    last = pl.num_programs(0) - 1
    c = cur.shape[1]
    slab[0:HALO, 0:c] = jnp.where(i > 0, prev, 0.0)
    slab[HALO:HALO + tm, 0:c] = cur
    slab[HALO + tm:2 * HALO + tm, 0:c] = jnp.where(i < last, nxt, 0.0)


def _rotate_slab(slab, rot, tm):
    rows = tm + 2 * HALO - SUB
    c = slab.shape[1] - LANE
    for b in range(1, SUB):
        rot[b - 1, :, 0:c] = slab[pl.ds(b, rows), 0:c]


def _shifted(slab, rot, start, rc, cs):
    b = start % SUB
    if b == 0:
        return slab[pl.ds(start, rc), cs]
    return rot[b - 1, pl.ds(start - b, rc), cs]


def conv_fwd(h, conv_w, conv_b, g_ln, b_ln):
    s = h.shape[0]
    c = conv_w.shape[1]
    tm = _tile(s, 256)
    rc = _tile(tm, 64)

    def body(a_ref, ap_ref, an_ref, g_ref, gp_ref, gn_ref, w_ref, cb_ref, lg_ref, lb_ref, co_ref, uc_ref, slab, rot):
        _fill_slab(slab, tm, ap_ref[...] * _sigmoid(gp_ref[...]), a_ref[...] * _sigmoid(g_ref[...]),
                   an_ref[...] * _sigmoid(gn_ref[...]))
        _rotate_slab(slab, rot, tm)

        def lane_block(cb, carry):
            cs = pl.ds(pl.multiple_of(cb * LANE, LANE), LANE)
            for r0 in range(0, tm, rc):
                acc = jnp.zeros((rc, LANE), F32)
                for k in range(CONV_K):
                    acc = acc + w_ref[k:k + 1, cs] * _shifted(slab, rot, r0 + HALO - CONV_PAD + k, rc, cs)
                uc_ref[r0:r0 + rc, cs] = acc + cb_ref[:, cs]
            return carry

        lax.fori_loop(0, c // LANE, lane_block, 0)
        xhat, _ = _ln_stats(uc_ref[...])
        cl = xhat * lg_ref[...] + lb_ref[...]
        co_ref[...] = (cl * _sigmoid(cl)).astype(BF)

    a_specs = _halo_specs(tm, s, c, 1)
    g_specs = _halo_specs(tm, s, c, 2)
    row = pl.BlockSpec((1, c), lambda i: (0, 0))
    tok = pl.BlockSpec((tm, c), lambda i: (i, 0))
    return _call(
        body, name="conv_fwd", grid=(s // tm,),
        in_specs=[*a_specs, *g_specs, pl.BlockSpec(conv_w.shape, lambda i: (0, 0)), row, row, row],
        out_specs=[tok, tok], out_shape=[_sds((s, c), BF), _sds((s, c), F32)],
        scratch_shapes=[pltpu.VMEM(shp, F32) for shp in _slab_shapes(tm, c)],
        compiler_params=_cp(("arbitrary",)),
    )(h, h, h, h, h, h, conv_w, conv_b, g_ln, b_ln)


def out_proj_ln1(ob, co, wout, x0, g1, b1):
    s, d = x0.shape
    kh = ob.shape[1]
    tm = _tile(s, 256)

    def body(o_ref, c_ref, w_ref, x_ref, g_ref, b_ref, r_ref, x1_ref, x1b_ref, acc):
        acc[...] = _dot(o_ref[...], w_ref[0:kh, :]) + _dot(c_ref[...], w_ref[kh:2 * kh, :])
        g = g_ref[...]
        b = b_ref[...]

        def chunk(rows):
            r = ALPHA * x_ref[rows, :] + acc[rows, :]
            r_ref[rows, :] = r
            xhat, _ = _ln_stats(r)
            y = xhat * g + b
            x1_ref[rows, :] = y
            x1b_ref[rows, :] = y.astype(BF)

        _row_chunks(tm, chunk)

    half = pl.BlockSpec((tm, kh), lambda i: (i, 0))
    tok = pl.BlockSpec((tm, d), lambda i: (i, 0))
    row = pl.BlockSpec((1, d), lambda i: (0, 0))
    return _call(
        body, name="out_proj_ln1", grid=(s // tm,),
        in_specs=[half, half, pl.BlockSpec((2 * kh, d), lambda i: (0, 0)), tok, row, row],
        out_specs=[tok, tok, tok], out_shape=[_sds((s, d), F32), _sds((s, d), F32), _sds((s, d), BF)],
        scratch_shapes=[pltpu.VMEM((tm, d), F32)], compiler_params=_cp(("arbitrary",)),
    )(ob, co, wout, x0, g1, b1)


def ff1_fwd(x1b, wff1_g):
    s, d = x1b.shape
    nsh, _, fs = wff1_g.shape
    tm = _tile(s, 1024)
    tn = _tile(fs, 1024)
    per = fs // tn

    def body(a_ref, w_ref, r_ref, a1_ref):
        r = jnp.maximum(_dot(a_ref[...], w_ref[...]), 0.0)
        r_ref[...] = r.astype(BF)
        a1_ref[...] = (r * r).astype(BF)

    out = pl.BlockSpec((tm, tn), lambda i, j: (i, j))
    return _call(
        body, name="ff1_fwd", grid=(s // tm, nsh * per),
        in_specs=[pl.BlockSpec((tm, d), lambda i, j: (i, 0)),
                  pl.BlockSpec((None, d, tn), lambda i, j: (j // per, 0, j % per))],
        out_specs=[out, out], out_shape=[_sds((s, nsh * fs), BF)] * 2,
        compiler_params=_cp(("arbitrary", "arbitrary")),
    )(x1b, wff1_g)


def ff2_ln2_loss(a1b, wff2, x1, target, g2, b2):
    s, f = a1b.shape
    d = x1.shape[1]
    tm = _tile(s, 512)
    tk = _tile(f, 2048)
    nk = f // tk

    def body(a_ref, w_ref, x_ref, t_ref, g_ref, b_ref, dr_ref, drb_ref, loss_ref, dg_ref, db_ref, acc):
        i = pl.program_id(0)
        k = pl.program_id(1)

        @pl.when(k == 0)
        def _():
            acc[...] = _dot(a_ref[...], w_ref[...])

        @pl.when(k > 0)
        def _():
            acc[...] += _dot(a_ref[...], w_ref[...])

        @pl.when(jnp.logical_and(i == 0, k == 0))
        def _():
            loss_ref[...] = jnp.zeros_like(loss_ref)
            dg_ref[...] = jnp.zeros_like(dg_ref)
            db_ref[...] = jnp.zeros_like(db_ref)

        @pl.when(k == nk - 1)
        def _():
            g = g_ref[...]

            def chunk(rows):
                r = ALPHA * x_ref[rows, :] + acc[rows, :]
                xhat, rstd = _ln_stats(r)
                e = xhat * g + b_ref[...] - t_ref[rows, :]
                e2 = _rows8(e * e)
                part = e2[:, 0:LANE]
                for c in range(1, d // LANE):
                    part = part + e2[:, LANE * c:LANE * (c + 1)]
                loss_ref[...] += part * (0.5 / d)
                dy = e * (1.0 / d)
                dg_ref[...] += _rows8(dy * xhat)
                db_ref[...] += _rows8(dy)
                dr = _ln_bwd(dy, xhat, rstd, g)
                dr_ref[rows, :] = dr
                drb_ref[rows, :] = dr.astype(BF)

            _row_chunks(tm, chunk)

    tok = pl.BlockSpec((tm, d), lambda i, k: (i, 0))
    row = pl.BlockSpec((1, d), lambda i, k: (0, 0))
    accs = pl.BlockSpec((SUB, d), lambda i, k: (0, 0))
    return _call(
        body, name="ff2_ln2_loss", grid=(s // tm, nk),
        in_specs=[pl.BlockSpec((tm, tk), lambda i, k: (i, k)), pl.BlockSpec((tk, d), lambda i, k: (k, 0)),
                  tok, tok, row, row],
        out_specs=[tok, tok, pl.BlockSpec((SUB, LANE), lambda i, k: (0, 0)), accs, accs],
        out_shape=[_sds((s, d), F32), _sds((s, d), BF), _sds((SUB, LANE), F32), _sds((SUB, d), F32), _sds((SUB, d), F32)],
        scratch_shapes=[pltpu.VMEM((tm, d), F32)], compiler_params=_cp(("arbitrary", "arbitrary"), 60),
    )(a1b, wff2, x1, target, g2, b2)


def ff2_bwd_act(dr2b, wff2, rb):
    s, d = dr2b.shape
    f = wff2.shape[0]
    tm = _tile(s, 1024)
    tn = _tile(f, 1024)

    def body(a_ref, w_ref, r_ref, o_ref):
        o_ref[...] = (_dot_nt(a_ref[...], w_ref[...]) * (2.0 * r_ref[...].astype(F32))).astype(BF)

    return _call(
        body, name="ff2_bwd_act", grid=(s // tm, f // tn),
        in_specs=[pl.BlockSpec((tm, d), lambda i, j: (i, 0)), pl.BlockSpec((tn, d), lambda i, j: (j, 0)),
                  pl.BlockSpec((tm, tn), lambda i, j: (i, j))],
        out_specs=pl.BlockSpec((tm, tn), lambda i, j: (i, j)), out_shape=_sds((s, f), BF),
        compiler_params=_cp(("arbitrary", "arbitrary")),
    )(dr2b, wff2, rb)


def wgrad(name, a, b, tm, tn, tk=2048, shards=1):
    s, m = a.shape
    n = b.shape[1]
    tm = _tile(m, tm)
    ns = n // shards
    tn = _tile(ns, tn)
    tk = _tile(s, tk)
    per = ns // tn

    def body(a_ref, b_ref, o_ref):
        k = pl.program_id(2)

        @pl.when(k == 0)
        def _():
            o_ref[...] = _dot_tn(a_ref[...], b_ref[...])

        @pl.when(k > 0)
        def _():
            o_ref[...] += _dot_tn(a_ref[...], b_ref[...])

    return _call(
        body, name=name, grid=(m // tm, n // tn, s // tk),
        in_specs=[pl.BlockSpec((tk, tm), lambda i, j, k: (k, i)), pl.BlockSpec((tk, tn), lambda i, j, k: (k, j))],
        out_specs=pl.BlockSpec((None, tm, tn), lambda i, j, k: (j // per, i, j % per)),
        out_shape=_sds((shards, m, ns), F32), compiler_params=_cp(("arbitrary", "arbitrary", "arbitrary")),
    )(a, b)


def ff1_bwd_ln1(df1b, wff1_g, dr2, r1, g1):
    s, f = df1b.shape
    d = dr2.shape[1]
    tm = _tile(s, 512)
    tk = _tile(wff1_g.shape[2], 2048)
    per = wff1_g.shape[2] // tk
    nk = f // tk

    def body(a_ref, w_ref, d2_ref, r_ref, g_ref, dr_ref, drb_ref, dg_ref, db_ref, acc):
        i = pl.program_id(0)
        k = pl.program_id(1)

        @pl.when(k == 0)
        def _():
            acc[...] = _dot_nt(a_ref[...], w_ref[...])

        @pl.when(k > 0)
        def _():
            acc[...] += _dot_nt(a_ref[...], w_ref[...])

        @pl.when(jnp.logical_and(i == 0, k == 0))
        def _():
            dg_ref[...] = jnp.zeros_like(dg_ref)
            db_ref[...] = jnp.zeros_like(db_ref)

        @pl.when(k == nk - 1)
        def _():
            g = g_ref[...]

            def chunk(rows):
                dy = ALPHA * d2_ref[rows, :] + acc[rows, :]
                xhat, rstd = _ln_stats(r_ref[rows, :])
                dg_ref[...] += _rows8(dy * xhat)
                db_ref[...] += _rows8(dy)
                dr = _ln_bwd(dy, xhat, rstd, g)
                dr_ref[rows, :] = dr
                drb_ref[rows, :] = dr.astype(BF)

            _row_chunks(tm, chunk)

    tok = pl.BlockSpec((tm, d), lambda i, k: (i, 0))
    accs = pl.BlockSpec((SUB, d), lambda i, k: (0, 0))
    return _call(
        body, name="ff1_bwd_ln1", grid=(s // tm, nk),
        in_specs=[pl.BlockSpec((tm, tk), lambda i, k: (i, k)), pl.BlockSpec((None, d, tk), lambda i, k: (k // per, 0, k % per)),
                  tok, tok, pl.BlockSpec((1, d), lambda i, k: (0, 0))],
        out_specs=[tok, tok, accs, accs],
        out_shape=[_sds((s, d), F32), _sds((s, d), BF), _sds((SUB, d), F32), _sds((SUB, d), F32)],
        scratch_shapes=[pltpu.VMEM((tm, d), F32)], compiler_params=_cp(("arbitrary", "arbitrary"), 60),
    )(df1b, wff1_g, dr2, r1, g1)


def out_proj_bwd(dr1b, woutt, o):
    s, d = dr1b.shape
    tm = _tile(s, 256)

    def body(a_ref, w_ref, o_ref, do_ref, dot_ref, dc_ref, dl_ref):
        dcat = _dot(a_ref[...], w_ref[...])
        do = dcat[:, 0:MLA_W]
        do_ref[...] = do.astype(BF)
        dc_ref[...] = dcat[:, MLA_W:]
        prod = do * o_ref[...]
        for hd in range(HEADS):
            hs = slice(LANE * hd, LANE * (hd + 1))
            dl_ref[hd] = _as_row(jnp.sum(prod[:, hs], axis=-1, keepdims=True))
            dot_ref[hd] = do[:, hs].T.astype(BF)

    half = pl.BlockSpec((tm, MLA_W), lambda i: (i, 0))
    return _call(
        body, name="out_proj_bwd", grid=(s // tm,),
        in_specs=[pl.BlockSpec((tm, d), lambda i: (i, 0)), pl.BlockSpec((d, d), lambda i: (0, 0)), half],
        out_specs=[half, pl.BlockSpec((HEADS, LANE, tm), lambda i: (0, 0, i)),
                   pl.BlockSpec((tm, d - MLA_W), lambda i: (i, 0)), pl.BlockSpec((HEADS, 1, tm), lambda i: (0, 0, i))],
        out_shape=[_sds((s, MLA_W), BF), _sds((HEADS, LANE, s), BF), _sds((s, d - MLA_W), F32), _sds((HEADS, 1, s), F32)],
        compiler_params=_cp(("arbitrary",)),
    )(dr1b, woutt, o)


def conv_bwd_ln(uc, dco, g_ln, b_ln):
    s, c = uc.shape
    tm = _tile(s, 512)

    def body(u_ref, d_ref, g_ref, b_ref, du_ref, dg_ref, db_ref, dcb_ref):
        @pl.when(pl.program_id(0) == 0)
        def _():
            dg_ref[...] = jnp.zeros_like(dg_ref)
            db_ref[...] = jnp.zeros_like(db_ref)
            dcb_ref[...] = jnp.zeros_like(dcb_ref)

        xhat, rstd = _ln_stats(u_ref[...])
        g = g_ref[...]
        cl = xhat * g + b_ref[...]
        sg = _sigmoid(cl)
        dcl = d_ref[...] * (sg * (1.0 + cl * (1.0 - sg)))
        dg_ref[...] += _rows8(dcl * xhat)
        db_ref[...] += _rows8(dcl)
        du = _ln_bwd(dcl, xhat, rstd, g)
        du_ref[...] = du
        dcb_ref[...] += _rows8(du)

    tok = pl.BlockSpec((tm, c), lambda i: (i, 0))
    row = pl.BlockSpec((1, c), lambda i: (0, 0))
    accs = pl.BlockSpec((SUB, c), lambda i: (0, 0))
    return _call(
        body, name="conv_bwd_ln", grid=(s // tm,), in_specs=[tok, tok, row, row], out_specs=[tok, accs, accs, accs],
        out_shape=[_sds((s, c), F32)] + [_sds((SUB, c), F32)] * 3, compiler_params=_cp(("arbitrary",)),
    )(uc, dco, g_ln, b_ln)


def conv_bwd_taps(h, duc, conv_w):
    s, c = duc.shape
    tm = _tile(s, 256)
    rc = _tile(tm, 64)

    def body(a_ref, ap_ref, an_ref, g_ref, gp_ref, gn_ref, d_ref, dp_ref, dn_ref, w_ref, o_ref, dw_ref,
             uslab, dslab, du_s, urot, drot, dw8):
        @pl.when(pl.program_id(0) == 0)
        def _():
            dw8[...] = jnp.zeros_like(dw8)

        sg = _sigmoid(g_ref[...])
        a = a_ref[...]
        _fill_slab(uslab, tm, ap_ref[...] * _sigmoid(gp_ref[...]), a * sg, an_ref[...] * _sigmoid(gn_ref[...]))
        _fill_slab(dslab, tm, dp_ref[...], d_ref[...], dn_ref[...])
        _rotate_slab(uslab, urot, tm)
        _rotate_slab(dslab, drot, tm)

        def lane_block(cb, carry):
            cs = pl.ds(pl.multiple_of(cb * LANE, LANE), LANE)
            for r0 in range(0, tm, rc):
                acc = jnp.zeros((rc, LANE), F32)
                for k in range(CONV_K):
                    acc = acc + w_ref[k:k + 1, cs] * _shifted(dslab, drot, r0 + HALO + CONV_PAD - k, rc, cs)
                du_s[r0:r0 + rc, cs] = acc
            return carry

        def lane_block_taps(cb, carry):
            cs = pl.ds(pl.multiple_of(cb * LANE, LANE), LANE)
            parts = []
            for k in range(CONV_K):
                prod = None
                for r0 in range(0, tm, rc):
                    t = dslab[pl.ds(r0 + HALO, rc), cs] * _shifted(uslab, urot, r0 + HALO - CONV_PAD + k, rc, cs)
                    prod = t if prod is None else prod + t
                parts.append(_rows8(prod))
            rows = SUB * CONV_K
            dw8[0:rows, cs] = dw8[0:rows, cs] + jnp.concatenate(parts, axis=0)
            return carry

        lax.fori_loop(0, c // LANE, lane_block, 0)
        lax.fori_loop(0, c // LANE, lane_block_taps, 0)

        @pl.when(pl.program_id(0) == pl.num_programs(0) - 1)
        def _():
            dw_ref[...] = jnp.zeros_like(dw_ref)
            for k in range(CONV_K):
                dw_ref[k:k + 1, :] = jnp.sum(dw8[SUB * k:SUB * (k + 1), :], axis=0, keepdims=True)

        du = du_s[...]
        o_ref[:, 0:c] = (du * sg).astype(BF)
        o_ref[:, c:2 * c] = (du * a * sg * (1.0 - sg)).astype(BF)

    a_specs = _halo_specs(tm, s, c, 1)
    g_specs = _halo_specs(tm, s, c, 2)
    d_specs = _halo_specs(tm, s, c, 0)
    wsp = pl.BlockSpec(conv_w.shape, lambda i: (0, 0))
    return _call(
        body, name="conv_bwd_taps", grid=(s // tm,), in_specs=[*a_specs, *g_specs, *d_specs, wsp],
        out_specs=[pl.BlockSpec((tm, 2 * c), lambda i: (i, 0)), wsp],
        out_shape=[_sds((s, 2 * c), BF), _sds(conv_w.shape, F32)],
        scratch_shapes=[pltpu.VMEM(_slab_shapes(tm, c)[0], F32), pltpu.VMEM(_slab_shapes(tm, c)[0], F32), pltpu.VMEM((tm, c), F32),
                        pltpu.VMEM(_slab_shapes(tm, c)[1], F32), pltpu.VMEM(_slab_shapes(tm, c)[1], F32),
                        pltpu.VMEM((SUB * conv_w.shape[0], c), F32)],
        compiler_params=_cp(("arbitrary",)),
    )(h, h, h, h, h, h, duc, duc, duc, conv_w)


def attn_bwd(qc, kc, kct, v, dob, dot, lse_r, delta_r):
    _, s, _ = qc.shape
    tk = _tile(s, 512)
    tq = _tile(s, 512)
    scale = D_QK ** -0.5
    c2 = scale * LOG2E

    def body(k_ref, kt_ref, v_ref, q_ref, do_ref, dot_ref, l_ref, dl_ref, dqt_ref, dk_ref, dvt_ref):
        @pl.when(pl.program_id(1) == 0)
        def _():
            dqt_ref[...] = jnp.zeros_like(dqt_ref)

        k = k_ref[...]
        kt = kt_ref[...]
        vv = v_ref[...]

        def step(i, carry):
            dk, dvt = carry
            off = pl.multiple_of(i * tq, tq)
            q = q_ref[pl.ds(off, tq), :]
            do = do_ref[pl.ds(off, tq), :]
            pt = jnp.exp2(_dot_nt(k, q) * c2 - l_ref[:, pl.ds(off, tq)])
            dvt = dvt + _dot_nt(dot_ref[:, pl.ds(off, tq)], pt.astype(BF))
            dpt = _dot_nt(vv, do)
            dsb = (pt * (dpt - dl_ref[:, pl.ds(off, tq)]) * scale).astype(BF)
            dk = dk + _dot(dsb, q)
            dqt_ref[:, pl.ds(off, tq)] += _dot(kt, dsb)
            return dk, dvt

        dk, dvt = _unrolled_loop(s // tq, 16, step, (jnp.zeros((tk, 2 * LANE), F32), jnp.zeros((LANE, tk), F32)))
        dk_ref[...] = dk
        dvt_ref[...] = dvt

    rowv = pl.BlockSpec((None, 1, s), lambda h, j: (h, 0, 0))
    return _call(
        body, name="attn_bwd", grid=(HEADS, s // tk),
        in_specs=[pl.BlockSpec((None, tk, 2 * LANE), lambda h, j: (h, j, 0)),
                  pl.BlockSpec((None, 2 * LANE, tk), lambda h, j: (h, 0, j)),
                  pl.BlockSpec((None, tk, LANE), lambda h, j: (h, j, 0)),
                  pl.BlockSpec((None, s, 2 * LANE), lambda h, j: (h, 0, 0)),
                  pl.BlockSpec((s, LANE), lambda h, j: (0, h)),
                  pl.BlockSpec((None, LANE, s), lambda h, j: (h, 0, 0)), rowv, rowv],
        out_specs=[pl.BlockSpec((None, 2 * LANE, s), lambda h, j: (h, 0, 0)),
                   pl.BlockSpec((None, tk, 2 * LANE), lambda h, j: (h, j, 0)),
                   pl.BlockSpec((None, LANE, tk), lambda h, j: (h, 0, j))],
        out_shape=[_sds((HEADS, 2 * LANE, s), F32), _sds((HEADS, s, 2 * LANE), F32), _sds((HEADS, LANE, s), F32)],
        compiler_params=_cp(("arbitrary", "arbitrary"), 56),
    )(kc, kct, v, qc, dob, dot, lse_r, delta_r)


def q_bwd(dqt, h, g_cq, wuqt, cos, sin):
    s = h.shape[0]
    tm = _tile(s, 256)

    def body(d_ref, h_ref, g_ref, w_ref, c_ref, s_ref, dq_ref, dc_ref, dg_ref):
        @pl.when(pl.program_id(0) == 0)
        def _():
            dg_ref[...] = jnp.zeros_like(dg_ref)

        c = c_ref[...]
        sn = s_ref[...]
        for hd in range(HEADS):
            t = d_ref[hd].T
            dq_ref[:, LANE * hd:LANE * (hd + 1)] = t[:, 0:LANE].astype(BF)
            dq_ref[:, MLA_W + LANE * hd:MLA_W + LANE * (hd + 1)] = _unrope128(t[:, LANE:2 * LANE], c, sn).astype(BF)
        dy = _dot(dq_ref[...], w_ref[...])
        g = g_ref[...]
        _, xh, rr = _rms_fwd(h_ref[...], g)
        dg_ref[...] += _rows8(dy * xh)
        dc_ref[...] = _rms_bwd(dy, xh, rr, g).astype(BF)

    tab = pl.BlockSpec((tm, LANE), lambda i: (i, 0))
    return _call(
        body, name="q_bwd", grid=(s // tm,),
        in_specs=[pl.BlockSpec((HEADS, 2 * LANE, tm), lambda i: (0, 0, i)), pl.BlockSpec((tm, R_Q), lambda i: (i, 0)),
                  pl.BlockSpec((1, R_Q), lambda i: (0, 0)), pl.BlockSpec((2 * MLA_W, R_Q), lambda i: (0, 0)), tab, tab],
        out_specs=[pl.BlockSpec((tm, 2 * MLA_W), lambda i: (i, 0)), pl.BlockSpec((tm, R_Q), lambda i: (i, 0)),
                   pl.BlockSpec((SUB, R_Q), lambda i: (0, 0))],
        out_shape=[_sds((s, 2 * MLA_W), BF), _sds((s, R_Q), BF), _sds((SUB, R_Q), F32)],
        compiler_params=_cp(("arbitrary",)),
    )(dqt, h, g_cq, wuqt, cos, sin)


def kv_bwd(dk, dv, h, g_ckv, wukt, wuvt, cos, sin):
    s = h.shape[0]
    tm = _tile(s, 256)

    def body(dk_ref, dv_ref, h_ref, g_ref, wk_ref, wv_ref, c_ref, s_ref, dkn_ref, dvb_ref, dc_ref, dkr_ref, dg_ref):
        @pl.when(pl.program_id(0) == 0)
        def _():
            dg_ref[...] = jnp.zeros_like(dg_ref)

        dkr = dk_ref[0, :, LANE:2 * LANE]
        for hd in range(HEADS):
            dkn_ref[:, LANE * hd:LANE * (hd + 1)] = dk_ref[hd, :, 0:LANE].astype(BF)
            dvb_ref[:, LANE * hd:LANE * (hd + 1)] = dv_ref[hd].T.astype(BF)
            if hd > 0:
                dkr = dkr + dk_ref[hd, :, LANE:2 * LANE]
        dkr_ref[...] = _unrope128(dkr, c_ref[...], s_ref[...]).astype(BF)
        dy = _dot(dkn_ref[...], wk_ref[...]) + _dot(dvb_ref[...], wv_ref[...])
        g = g_ref[...]
        _, xh, rr = _rms_fwd(h_ref[...], g)
        dg_ref[...] += _rows8(dy * xh)
        dc_ref[...] = _rms_bwd(dy, xh, rr, g).astype(BF)

    tab = pl.BlockSpec((tm, LANE), lambda i: (i, 0))
    wsp = pl.BlockSpec((MLA_W, R_KV), lambda i: (0, 0))
    wide = pl.BlockSpec((tm, MLA_W), lambda i: (i, 0))
    return _call(
        body, name="kv_bwd", grid=(s // tm,),
        in_specs=[pl.BlockSpec((HEADS, tm, 2 * LANE), lambda i: (0, i, 0)), pl.BlockSpec((HEADS, LANE, tm), lambda i: (0, 0, i)),
                  pl.BlockSpec((tm, R_KV), lambda i: (i, 1)), pl.BlockSpec((1, R_KV), lambda i: (0, 0)), wsp, wsp, tab, tab],
        out_specs=[wide, wide, pl.BlockSpec((tm, R_KV), lambda i: (i, 0)), tab, pl.BlockSpec((SUB, R_KV), lambda i: (0, 0))],
        out_shape=[_sds((s, MLA_W), BF), _sds((s, MLA_W), BF), _sds((s, R_KV), BF), _sds((s, LANE), BF), _sds((SUB, R_KV), F32)],
        compiler_params=_cp(("arbitrary",)),
    )(dk, dv, h, g_ckv, wukt, wuvt, cos, sin)


def in_proj_bwd_ln(dh, wint, dr1, x, g_in):
    s, hc = dh.shape
    d = x.shape[1]
    tm = _tile(s, 256)

    def body(a_ref, w_ref, d1_ref, x_ref, g_ref, gx_ref, dg_ref, db_ref, acc):
        @pl.when(pl.program_id(0) == 0)
        def _():
            dg_ref[...] = jnp.zeros_like(dg_ref)
            db_ref[...] = jnp.zeros_like(db_ref)

        acc[...] = _dot(a_ref[...], w_ref[...])
        g = g_ref[...]

        def chunk(rows):
            dy = ALPHA * d1_ref[rows, :] + acc[rows, :]
            xhat, rstd = _ln_stats(x_ref[rows, :])
            dg_ref[...] += _rows8(dy * xhat)
            db_ref[...] += _rows8(dy)
            gx_ref[rows, :] = _ln_bwd(dy, xhat, rstd, g)

        _row_chunks(tm, chunk)

    tok = pl.BlockSpec((tm, d), lambda i: (i, 0))
    accs = pl.BlockSpec((SUB, d), lambda i: (0, 0))
    return _call(
        body, name="in_proj_bwd_ln", grid=(s // tm,),
        in_specs=[pl.BlockSpec((tm, hc), lambda i: (i, 0)), pl.BlockSpec((hc, d), lambda i: (0, 0)),
                  tok, tok, pl.BlockSpec((1, d), lambda i: (0, 0))],
        out_specs=[tok, accs, accs], out_shape=[_sds((s, d), F32), _sds((SUB, d), F32), _sds((SUB, d), F32)],
        scratch_shapes=[pltpu.VMEM((tm, d), F32)], compiler_params=_cp(("arbitrary",), 56),
    )(dh, wint, dr1, x, g_in)


def _adamw_math(w, g, m, v):
    m = ADAM_B1 * m + (1.0 - ADAM_B1) * g
    v = ADAM_B2 * v + (1.0 - ADAM_B2) * (g * g)
    m_hat = m / (1.0 - ADAM_B1 ** ADAM_STEP)
    v_hat = v / (1.0 - ADAM_B2 ** ADAM_STEP)
    delta = -ADAM_LR * (m_hat / (jnp.sqrt(v_hat) + ADAM_EPS) + ADAM_WD * w)
    return delta, m, v


def adamw(name, w, g, m, v):
    r, c = w.shape
    tr = _row_tile(r, c)

    def body(w_ref, g_ref, m_ref, v_ref, d_ref, mo_ref, vo_ref):
        d_ref[...], mo_ref[...], vo_ref[...] = _adamw_math(w_ref[...], g_ref[...], m_ref[...], v_ref[...])

    blk = pl.BlockSpec((tr, c), lambda i: (i, 0))
    return _call(
        body, name=name, grid=(r // tr,), in_specs=[blk] * 4, out_specs=[blk] * 3,
        out_shape=[_sds((r, c), F32)] * 3, compiler_params=_cp(("arbitrary",)),
    )(w, g, m, v)


def _coords():
    return lax.axis_index("x"), lax.axis_index("y"), lax.axis_index("c")


def _other_chips(x, y):
    return [(1 - x, y, 2 * (1 - x) + y), (x, 1 - y, 2 * x + 1 - y), (1 - x, 1 - y, 2 * (1 - x) + 1 - y)]


ANY = pl.BlockSpec(memory_space=pl.ANY)
HBM = pl.BlockSpec(memory_space=pltpu.HBM)
SEM = pl.BlockSpec(memory_space=pltpu.SEMAPHORE)
EFFECT = pltpu.SideEffectType.DATAFLOW_SIDE_EFFECTING


def _in_hbm(a):
    return pltpu.with_memory_space_constraint(a, pltpu.HBM)


def _split_plan(mode, src, land, x, y, c):
    if mode == "pair":
        rh = src.shape[1] // 2
        return [((x, y, 1 - c), src.at[:, pl.ds((1 - c) * rh, rh)], land, land)]
    me = 2 * x + y
    plan = []
    for j, (px, py, pk) in enumerate(_other_chips(x, y)):
        if mode == "gather":
            plan.append(((px, py, c), src, land.at[me], land.at[pk]))
        else:
            plan.append(((px, py, c), src.at[pk], land.at[j], land.at[j]))
    return plan


def _plan_len(mode):
    return 1 if mode == "pair" else N_CHIP - 1


def split_send_start(name, mode, srcs, land_shapes, order_after):
    n = len(srcs)
    np_ = _plan_len(mode)

    def body(*refs):
        ins, lands = refs[:n], refs[n:2 * n]
        ss, rs = refs[2 * n + 1], refs[2 * n + 2]
        token = refs[-1]
        x, y, c = _coords()
        for a in range(n):
            for j, (peer, src, dst, _) in enumerate(_split_plan(mode, ins[a], lands[a], x, y, c)):
                pltpu.make_async_remote_copy(src_ref=src, dst_ref=dst, send_sem=ss.at[np_ * a + j], recv_sem=rs.at[np_ * a + j],
                                             device_id=peer, device_id_type=MESH).start()
        token[...] = jnp.zeros_like(token)

    lands = [lax.empty(shp, s.dtype) for shp, s in zip(land_shapes, srcs)]
    outs = _call(
        body, name=name,
        out_shape=(pltpu.SemaphoreType.DMA((np_ * n,)), pltpu.SemaphoreType.DMA((np_ * n,)),
                   *[pltpu.HBM(s.shape, s.dtype) for s in srcs], *[pltpu.HBM(l.shape, l.dtype) for l in lands],
                   _sds((SUB, LANE), F32)),
        in_specs=[HBM] * (2 * n) + [ANY], out_specs=(SEM, SEM, *[HBM] * (2 * n), pl.BlockSpec(memory_space=pltpu.VMEM)),
        input_output_aliases={a: 2 + a for a in range(2 * n)},
        compiler_params=pltpu.CompilerParams(has_side_effects=EFFECT),
    )(*[_in_hbm(s) for s in srcs], *[_in_hbm(l) for l in lands], order_after)
    return outs[0], outs[1], list(outs[2:2 + n]), list(outs[2 + n:2 + 2 * n]), outs[-1]


def split_send_wait(name, mode, ss, rs, srcs, lands, order_after):
    n = len(srcs)
    np_ = _plan_len(mode)

    def body(*refs):
        ins, lnd = refs[:n], refs[n:2 * n]
        s_ref, r_ref = refs[2 * n], refs[2 * n + 1]
        x, y, c = _coords()
        for a in range(n):
            for j, (peer, src, _, got) in enumerate(_split_plan(mode, ins[a], lnd[a], x, y, c)):
                cp = pltpu.make_async_remote_copy(src_ref=src, dst_ref=got, send_sem=s_ref.at[np_ * a + j], recv_sem=r_ref.at[np_ * a + j],
                                                  device_id=peer, device_id_type=MESH)
                cp.wait_send()
                cp.wait_recv()

    outs = _call(
        body, name=name, out_shape=tuple(pltpu.HBM(t.shape, t.dtype) for t in (*srcs, *lands)),
        in_specs=[HBM] * (2 * n) + [SEM, SEM, ANY], out_specs=tuple([HBM] * (2 * n)),
        input_output_aliases={a: a for a in range(2 * n)},
        compiler_params=pltpu.CompilerParams(has_side_effects=EFFECT),
    )(*srcs, *lands, ss, rs, order_after)
    return list(outs[:n]), list(outs[n:])


def all_gather_shards(shards):
    n = len(shards)

    def body(*refs):
        ins, outs = refs[:n], refs[n:2 * n]
        ici_s, ici_r, d2d_s, d2d_r = refs[2 * n:]
        x, y, c = _coords()
        me = 2 * x + y
        peers = _other_chips(x, y)
        sends, fwds = [], []
        for a in range(n):
            rh = ins[a].shape[0] // 2
            mine = pl.ds(c * rh, rh)
            for j, (px, py, pk) in enumerate(peers):
                cp = pltpu.make_async_remote_copy(
                    src_ref=ins[a].at[mine], dst_ref=outs[a].at[me, mine], send_sem=ici_s.at[a, j], recv_sem=ici_r.at[a, j],
                    device_id=(px, py, c), device_id_type=MESH)
                cp.start()
                sends.append(cp)
        for a in range(n):
            rh = ins[a].shape[0] // 2
            mine = pl.ds(c * rh, rh)
            for j, (px, py, pk) in enumerate(peers):
                got = outs[a].at[pk, mine]
                pltpu.make_async_remote_copy(
                    src_ref=got, dst_ref=got, send_sem=ici_s.at[a, j], recv_sem=ici_r.at[a, j],
                    device_id=(px, py, c), device_id_type=MESH).wait_recv()
                fw = pltpu.make_async_remote_copy(
                    src_ref=got, dst_ref=got, send_sem=d2d_s.at[a, j], recv_sem=d2d_r.at[a, j],
                    device_id=(x, y, 1 - c), device_id_type=MESH)
                fw.start()
                fwds.append(fw)
        for a in range(n):
            rh = ins[a].shape[0] // 2
            theirs = pl.ds((1 - c) * rh, rh)
            for j, (px, py, pk) in enumerate(peers):
                got = outs[a].at[pk, theirs]
                pltpu.make_async_remote_copy(
                    src_ref=got, dst_ref=got, send_sem=d2d_s.at[a, j], recv_sem=d2d_r.at[a, j],
                    device_id=(x, y, 1 - c), device_id_type=MESH).wait_recv()
        for cp in sends + fwds:
            cp.wait_send()

    got = _call(
        body, name="all_gather_shards", in_specs=[ANY] * n, out_specs=[ANY] * n,
        out_shape=[_sds((N_CHIP,) + w.shape, w.dtype) for w in shards],
        scratch_shapes=[pltpu.SemaphoreType.DMA((n, 3))] * 4,
    )(*shards)
    me = 2 * lax.axis_index("x") + lax.axis_index("y")
    return [lax.dynamic_update_slice(g, w[None], (me, 0, 0)) for g, w in zip(got, shards)]


def pair_exchange(grads, tag):
    n = len(grads)

    def body(*refs):
        ins, outs = refs[:n], refs[n:2 * n]
        ss, rs = refs[2 * n:]
        x, y, c = _coords()
        cps = []
        for a in range(n):
            rh = ins[a].shape[1] // 2
            cp = pltpu.make_async_remote_copy(
                src_ref=ins[a].at[:, pl.ds((1 - c) * rh, rh)], dst_ref=outs[a], send_sem=ss.at[a], recv_sem=rs.at[a],
                device_id=(x, y, 1 - c), device_id_type=MESH)
            cp.start()
            cps.append(cp)
        for cp in cps:
            cp.wait()

    return _call(
        body, name="pair_exchange_" + tag, in_specs=[ANY] * n, out_specs=[ANY] * n,
        out_shape=[_sds((N_CHIP, g.shape[1] // 2, g.shape[2]), F32) for g in grads],
        scratch_shapes=[pltpu.SemaphoreType.DMA((n,))] * 2,
    )(*grads)


def _row_tile(rows, cols, itemsize=4, budget=2 * VMEM_MB):
    fits = [t for t in range(SUB, rows + 1, SUB) if rows % t == 0 and t * cols * itemsize <= budget]
    return max(fits) if fits and rows * cols * itemsize > budget else rows


def pair_add(g, r, cidx):
    _, rows, cols = g.shape
    rh = rows // 2
    tr = _row_tile(rh, cols)
    per = rh // tr

    def body(c_ref, g_ref, r_ref, o_ref):
        o_ref[...] = g_ref[...] + r_ref[...]

    return _call(
        body, name="pair_add",
        grid_spec=pltpu.PrefetchScalarGridSpec(
            num_scalar_prefetch=1, grid=(N_CHIP, per),
            in_specs=[pl.BlockSpec((None, tr, cols), lambda k, i, c: (k, c[0] * per + i, 0)),
                      pl.BlockSpec((None, tr, cols), lambda k, i, c: (k, i, 0))],
            out_specs=pl.BlockSpec((None, tr, cols), lambda k, i, c: (k, i, 0))),
        out_shape=_sds((N_CHIP, rh, cols), F32), compiler_params=_cp(("arbitrary", "arbitrary")),
    )(cidx, g, r)


def chip_add(p, r, kc):
    _, rh, cols = p.shape
    tr = _row_tile(rh, cols)
    per = rh // tr

    def body(k_ref, p_ref, r_ref, o_ref):
        o_ref[...] = ((p_ref[...] + r_ref[0]) + r_ref[1]) + r_ref[2]

    return _call(
        body, name="chip_add",
        grid_spec=pltpu.PrefetchScalarGridSpec(
            num_scalar_prefetch=1, grid=(per,),
            in_specs=[pl.BlockSpec((None, tr, cols), lambda i, k: (k[0], i, 0)),
                      pl.BlockSpec((N_CHIP - 1, tr, cols), lambda i, k: (0, i, 0))],
            out_specs=pl.BlockSpec((tr, cols), lambda i, k: (k[1] * per + i, 0))),
        out_shape=_sds((2 * rh, cols), F32), compiler_params=_cp(("arbitrary",)),
    )(kc, p, r)


def pair_share(fulls, tag):
    n = len(fulls)

    def body(*refs):
        outs = refs[n:2 * n]
        ss, rs = refs[2 * n:]
        x, y, c = _coords()
        cps = []
        for a in range(n):
            rh = outs[a].shape[0] // 2
            mine = outs[a].at[pl.ds(c * rh, rh)]
            cp = pltpu.make_async_remote_copy(
                src_ref=mine, dst_ref=mine, send_sem=ss.at[a], recv_sem=rs.at[a],
                device_id=(x, y, 1 - c), device_id_type=MESH)
            cp.start()
            cps.append(cp)
        for a, cp in enumerate(cps):
            rh = outs[a].shape[0] // 2
            theirs = outs[a].at[pl.ds((1 - c) * rh, rh)]
            cp.wait_send()
            pltpu.make_async_remote_copy(
                src_ref=theirs, dst_ref=theirs, send_sem=ss.at[a], recv_sem=rs.at[a],
                device_id=(x, y, 1 - c), device_id_type=MESH).wait_recv()

    return _call(
        body, name="pair_share_" + tag, in_specs=[ANY] * n, out_specs=[ANY] * n,
        out_shape=[_sds(f.shape, F32) for f in fulls], input_output_aliases={a: a for a in range(n)},
        scratch_shapes=[pltpu.SemaphoreType.DMA((n,))] * 2,
    )(*fulls)


def small_allreduce_adamw(part, w, m, v):
    n = part.shape[1]

    def body(p_ref, w_ref, m_ref, v_ref, g_ref, d_ref, mo_ref, vo_ref, mine, gath, ss, rs):
        x, y, c = _coords()
        me = 4 * x + 2 * y + c
        mine[...] = jnp.sum(p_ref[...], axis=0, keepdims=True)
        gath[me] = mine[...]
        cps = []
        for k in range(1, 8):
            px, py, pc = x ^ (k >> 2), y ^ ((k >> 1) & 1), c ^ (k & 1)
            cp = pltpu.make_async_remote_copy(
                src_ref=mine, dst_ref=gath.at[me], send_sem=ss.at[k - 1], recv_sem=rs.at[k - 1],
                device_id=(px, py, pc), device_id_type=MESH)
            cp.start()
            cps.append(cp)
        for k in range(1, 8):
            src = 4 * (x ^ (k >> 2)) + 2 * (y ^ ((k >> 1) & 1)) + (c ^ (k & 1))
            pltpu.make_async_remote_copy(
                src_ref=mine, dst_ref=gath.at[src], send_sem=ss.at[k - 1], recv_sem=rs.at[k - 1],
                device_id=(x, y, c), device_id_type=MESH).wait_recv()
        for cp in cps:
            cp.wait_send()
        g = gath[0]
        for dv in range(1, 8):
            g = g + gath[dv]
        g_ref[...] = g
        d_ref[...], mo_ref[...], vo_ref[...] = _adamw_math(w_ref[...], g, m_ref[...], v_ref[...])

    vm = pl.BlockSpec(memory_space=pltpu.VMEM)
    return _call(
        body, name="small_allreduce_adamw", in_specs=[vm] * 4, out_specs=[vm] * 4, out_shape=[_sds((1, n), F32)] * 4,
        scratch_shapes=[pltpu.VMEM((1, n), F32), pltpu.VMEM((8, 1, n), F32),
                        pltpu.SemaphoreType.DMA((7,)), pltpu.SemaphoreType.DMA((7,))],
    )(part, w, m, v)


def _unshard_cols(g):
    k, r, cs = g.shape
    return g.transpose(1, 0, 2).reshape(r, k * cs)


def _shard_cols(w):
    r, c = w.shape
    return w.reshape(r, N_CHIP, c // N_CHIP).transpose(1, 0, 2)


def local_step(x, positions, ln_in_g, ln_in_b, win_g, g_cq, wuq_g, g_ckv, wuk_g, wuv_g, convw_g, conv_b, g_conv_ln,
               b_conv_ln, g_ln1, b_ln1, g_ln2, b_ln2, target, start_token, hooks):
    s, d = x.shape
    c = d - MLA_W
    row = lambda a: a.reshape(1, -1)
    ln_in_g = row(ln_in_g) + start_token[0:1, 0:1]

    o_kr = R_Q + R_KV
    o_cv = o_kr + D_ROPE
    n_in = o_cv + 2 * c
    per = n_in // N_CHIP

    def in_cols(a, b):
        return [win_g[k, max(a, per * k) - per * k:min(b, per * (k + 1)) - per * k]
                for k in range(N_CHIP) if max(a, per * k) < min(b, per * (k + 1))]

    win_rt = jnp.concatenate(in_cols(0, o_kr) + in_cols(o_cv, n_in) + in_cols(o_kr, o_cv)
                             + [jnp.zeros((LANE - D_ROPE, d), BF)], axis=0)
    kr_blk = (o_kr + 2 * c) // LANE
    wuq = _unshard_cols(wuq_g).reshape(R_Q, HEADS, D_QK)
    wuq_r = jnp.concatenate([wuq[:, :, :D_NOPE].reshape(R_Q, MLA_W),
                             jnp.pad(wuq[:, :, D_NOPE:], ((0, 0), (0, 0), (0, LANE - D_ROPE))).reshape(R_Q, MLA_W)], axis=1)
    wuk = _unshard_cols(wuk_g)
    wuv = _unshard_cols(wuv_g)
    conv_w = jnp.pad(_unshard_cols(convw_g), ((0, 1), (0, 0)))

    half = D_ROPE // 2
    inv_freq = ROPE_BASE ** (-jnp.arange(half, dtype=F32) * (2.0 / D_ROPE))
    invf = jnp.concatenate([inv_freq, inv_freq, jnp.zeros((LANE - D_ROPE,), F32)]).reshape(1, LANE)
    cos, sin = rope_tables(positions.astype(F32).reshape(s, 1), invf)
    x0, x0b = ln_in_fwd(x, ln_in_g, row(ln_in_b))
    h = matmul_nt("in_proj", x0b, win_rt, 256)
    qc, cqn = q_proj(h, g_cq, wuq_r, cos, sin)
    kc, kct, v, ckvn = kv_proj(h, g_ckv, wuk, wuv, cos, sin, kr_blk)
    o, ob, lse = attn_fwd(qc, kc, v)
    co, uc = conv_fwd(h, conv_w, conv_b, g_conv_ln, b_conv_ln)
    wout_g, wff1_g, wff2_g = hooks.late_weights(ob)
    wout = wout_g.reshape(d, d)
    wff2 = wff2_g.reshape(-1, d)
    r1, x1, x1b = out_proj_ln1(ob, co, wout, x0, g_ln1, b_ln1)
    rb, a1b = ff1_fwd(x1b, wff1_g)
    dr2, dr2b, loss8, dg2, db2 = ff2_ln2_loss(a1b, wff2, x1, target, g_ln2, b_ln2)

    df1b = ff2_bwd_act(dr2b, wff2, rb)
    gw_ff2 = wgrad("wgrad_ff2", a1b, dr2b, 1024, 1024).reshape(N_CHIP, -1, d)
    gw_ff1 = wgrad("wgrad_ff1", x1b, df1b, 1024, 1024, shards=N_CHIP)
    tok = hooks.ff_grads(gw_ff2, gw_ff1)
    dr1, dr1b, dg1, db1 = ff1_bwd_ln1(df1b, wff1_g, dr2, r1, g_ln1 + tok[0:1, 0:1])
    tok = hooks.ff_grads_mid(dr1b)
    gw_out = jnp.concatenate([wgrad("wgrad_out_attn", ob, dr1b, 1024, 1024)[0],
                              wgrad("wgrad_out_conv", co, dr1b, 1024, 1024)[0]], axis=0).reshape(N_CHIP, -1, d)
    dob, dot, dco, delta = out_proj_bwd(dr1b, wout.T, o)
    duc, dgc, dbc, dcb = conv_bwd_ln(uc, dco, g_conv_ln + tok[0:1, 0:1], b_conv_ln)
    dconv, gconvw = conv_bwd_taps(h, duc, conv_w)
    dqt, dk, dv = attn_bwd(qc, kc, kct, v, dob, dot, lse, delta)
    dqb, dcq, dgq = q_bwd(dqt, h, g_cq, wuq_r.T, cos, sin)
    dknb, dvb, dckv, dkr, dgkv = kv_bwd(dk, dv, h, g_ckv, wuk.T, wuv.T, cos, sin)
    gwuq_r = wgrad("wgrad_uq", cqn, dqb, 512, 1024)[0]
    gw_uk = wgrad("wgrad_uk", ckvn, dknb, 512, 1024, shards=N_CHIP)
    gw_uv = wgrad("wgrad_uv", ckvn, dvb, 512, 1024, shards=N_CHIP)
    dh = jnp.concatenate([dcq, dckv, dconv, dkr], axis=1)
    gwin_rt = wgrad("wgrad_in", dh, x0b, 640, 1024)[0]

    gwin_t = jnp.concatenate([gwin_rt[:o_kr], gwin_rt[o_kr + 2 * c:o_kr + 2 * c + D_ROPE], gwin_rt[o_kr:o_kr + 2 * c]], axis=0)
    gwin_t = jnp.pad(gwin_t.reshape(N_CHIP, per, d), ((0, 0), (0, win_g.shape[1] - per), (0, 0)))
    gwuq = jnp.concatenate([gwuq_r[:, :MLA_W].reshape(R_Q, HEADS, D_NOPE),
                            gwuq_r[:, MLA_W:].reshape(R_Q, HEADS, LANE)[:, :, :D_ROPE]], axis=2).reshape(R_Q, HEADS * D_QK)
    tok = hooks.rest_grads(dict(w_in=gwin_t, w_uq=_shard_cols(gwuq), w_uk=gw_uk, w_uv=gw_uv,
                                conv_w=_shard_cols(gconvw), w_out=gw_out))
    gx, dgin, dbin = in_proj_bwd_ln(dh, win_rt, dr1, x, ln_in_g + tok[0:1, 0:1])
    small = jnp.concatenate([dgin, dbin, dgq, dgkv, dcb, dgc, dbc, dg1, db1, dg2, db2, loss8], axis=1)
    return gx, small


BIG = ["w_in", "w_uq", "w_uk", "w_uv", "conv_w", "w_out", "w_ff1", "w_ff2"]
EARLY = ["w_in", "w_uq", "w_uk", "w_uv", "conv_w"]
LATE = ["w_out", "w_ff1", "w_ff2"]
SMALL = ["ln_in_g", "ln_in_b", "g_cq", "g_ckv", "conv_b", "g_conv_ln", "b_conv_ln", "g_ln1", "b_ln1", "g_ln2", "b_ln2"]
WEIGHTS = ["ln_in_g", "ln_in_b", "w_in", "g_cq", "w_uq", "g_ckv", "w_uk", "w_uv", "conv_w", "conv_b", "g_conv_ln",
           "b_conv_ln", "w_out", "g_ln1", "b_ln1", "w_ff1", "w_ff2", "g_ln2", "b_ln2"]


def _pad_rows(a, rows):
    return jnp.pad(a, ((0, rows - a.shape[0]), (0, 0)))


def kernel(x, positions, ln_in_g, ln_in_b, w_in, g_cq, w_uq, g_ckv, w_uk, w_uv, conv_w, conv_b, g_conv_ln, b_conv_ln, w_out, g_ln1, b_ln1, w_ff1, w_ff2, g_ln2, b_ln2, loss_target, m_ln_in_g, m_ln_in_b, m_w_in, m_g_cq, m_w_uq, m_g_ckv, m_w_uk, m_w_uv, m_conv_w, m_conv_b, m_g_conv_ln, m_b_conv_ln, m_w_out, m_g_ln1, m_b_ln1, m_w_ff1, m_w_ff2, m_g_ln2, m_b_ln2, v_ln_in_g, v_ln_in_b, v_w_in, v_g_cq, v_w_uq, v_g_ckv, v_w_uk, v_w_uv, v_conv_w, v_conv_b, v_g_conv_ln, v_b_conv_ln, v_w_out, v_g_ln1, v_b_ln1, v_w_ff1, v_w_ff2, v_g_ln2, v_b_ln2):
    w = dict(ln_in_g=ln_in_g, ln_in_b=ln_in_b, w_in=w_in, g_cq=g_cq, w_uq=w_uq, g_ckv=g_ckv, w_uk=w_uk, w_uv=w_uv,
             conv_w=conv_w, conv_b=conv_b, g_conv_ln=g_conv_ln, b_conv_ln=b_conv_ln, w_out=w_out, g_ln1=g_ln1,
             b_ln1=b_ln1, w_ff1=w_ff1, w_ff2=w_ff2, g_ln2=g_ln2, b_ln2=b_ln2)
    m = dict(ln_in_g=m_ln_in_g, ln_in_b=m_ln_in_b, w_in=m_w_in, g_cq=m_g_cq, w_uq=m_w_uq, g_ckv=m_g_ckv, w_uk=m_w_uk,
             w_uv=m_w_uv, conv_w=m_conv_w, conv_b=m_conv_b, g_conv_ln=m_g_conv_ln, b_conv_ln=m_b_conv_ln, w_out=m_w_out,
             g_ln1=m_g_ln1, b_ln1=m_b_ln1, w_ff1=m_w_ff1, w_ff2=m_w_ff2, g_ln2=m_g_ln2, b_ln2=m_b_ln2)
    v = dict(ln_in_g=v_ln_in_g, ln_in_b=v_ln_in_b, w_in=v_w_in, g_cq=v_g_cq, w_uq=v_w_uq, g_ckv=v_g_ckv, w_uk=v_w_uk,
             w_uv=v_w_uv, conv_w=v_conv_w, conv_b=v_conv_b, g_conv_ln=v_g_conv_ln, b_conv_ln=v_b_conv_ln, w_out=v_w_out,
             g_ln1=v_g_ln1, b_ln1=v_b_ln1, w_ff1=v_w_ff1, w_ff2=v_w_ff2, g_ln2=v_g_ln2, b_ln2=v_b_ln2)

    as2d = lambda t, n: t[n][0].T if n == "w_in" else t[n][0]
    sh2 = {n: as2d(w, n) for n in BIG}
    cidx = lax.axis_index("c").astype(jnp.int32).reshape(1)
    me = 2 * lax.axis_index("x") + lax.axis_index("y")
    kc = jnp.stack([me, lax.axis_index("c")]).astype(jnp.int32)

    pad_to = {"conv_w": CONV_K + 1, "w_in": -(-sh2["w_in"].shape[0] // (4 * SUB)) * (4 * SUB)}
    early = [_pad_rows(sh2[n] if n == "conv_w" else sh2[n].astype(BF), pad_to.get(n, sh2[n].shape[0])) for n in EARLY]
    gw = dict(zip(EARLY, all_gather_shards(early)))
    gw["conv_w"] = gw["conv_w"][:, :CONV_K]
    late = [sh2[n].astype(BF) for n in LATE]
    ag = split_send_start("late_weights_start", "gather", late, [(N_CHIP,) + a.shape for a in late], gw["w_uq"])
    rest = [n for n in BIG if n not in ("w_ff2", "w_ff1")]
    flight = {}

    class Hooks:
        @staticmethod
        def late_weights(after):
            mine, lands = split_send_wait("late_weights_wait", "gather", *ag[:4], after)
            return [lax.dynamic_update_slice(g, a[None], (me, 0, 0)) for g, a in zip(lands, mine)]

        @staticmethod
        def ff_grads(gw_ff2, gw_ff1):
            full = [gw_ff2, gw_ff1]
            st = split_send_start("ff_pair_start", "pair", full, [(N_CHIP, g.shape[1] // 2, g.shape[2]) for g in full], ag[4])
            flight["ff_pair"] = st[:4]
            flight["token"] = st[4]
            return st[4]

        @staticmethod
        def ff_grads_mid(after):
            full, recv = split_send_wait("ff_pair_wait", "pair", *flight["ff_pair"], after)
            psum = [pair_add(g, r, cidx) for g, r in zip(full, recv)]
            st = split_send_start("ff_grads_start", "scatter", psum, [(N_CHIP - 1,) + p.shape[1:] for p in psum], flight["token"])
            flight["ff"] = st[:4]
            flight["token"] = st[4]
            return st[4]

        @staticmethod
        def rest_grads(big):
            full = [big[n] for n in rest]
            psum = [pair_add(g, r, cidx) for g, r in zip(full, pair_exchange(full, "rest"))]
            st = split_send_start("rest_grads_start", "scatter", psum, [(N_CHIP - 1,) + p.shape[1:] for p in psum], flight["token"])
            flight["rest"] = st[:4]
            return st[4]

    gx, small = local_step(
        x[0], positions[0], ln_in_g, ln_in_b, gw["w_in"], g_cq, gw["w_uq"], g_ckv, gw["w_uk"], gw["w_uv"], gw["conv_w"],
        conv_b, g_conv_ln, b_conv_ln, g_ln1, b_ln1, g_ln2, b_ln2, loss_target[0], ag[4], Hooks)

    ff_psum, ff_got = split_send_wait("ff_grads_wait", "scatter", *flight["ff"], gx)
    rest_psum, rest_got = split_send_wait("rest_grads_wait", "scatter", *flight["rest"], gx)
    summed = [chip_add(p, r, kc) for p, r in zip(rest_psum + ff_psum, rest_got + ff_got)]
    gsh = dict(zip(rest + ["w_ff2", "w_ff1"], pair_share(summed, "all")))
    for n in pad_to:
        gsh[n] = gsh[n][:sh2[n].shape[0]]

    grad, delta, new_m, new_v = {}, {}, {}, {}
    for n in BIG:
        back = (lambda a: a.T[None]) if n == "w_in" else (lambda a: a[None])
        d_, m_, v_ = adamw("adamw_" + n, sh2[n], gsh[n], as2d(m, n), as2d(v, n))
        grad[n], delta[n], new_m[n], new_v[n] = back(gsh[n]), back(d_), back(m_), back(v_)

    flat = lambda t: jnp.concatenate([t[n].reshape(1, -1) for n in SMALL] + [jnp.zeros((1, LANE), F32)], axis=1)
    g_s, d_s, m_s, v_s = small_allreduce_adamw(small, flat(w), flat(m), flat(v))
    off = 0
    for n in SMALL:
        sz = w[n].size
        for dst, src in ((grad, g_s), (delta, d_s), (new_m, m_s), (new_v, v_s)):
            dst[n] = src[0, off:off + sz].reshape(w[n].shape)
        off += sz
    loss = jnp.sum(g_s[0, off:off + LANE])

    return (loss, gx[None], *[grad[n] for n in WEIGHTS], *[delta[n] for n in WEIGHTS],
            *[new_m[n] for n in WEIGHTS], *[new_v[n] for n in WEIGHTS])
```

```python
import functools

import jax
import jax.numpy as jnp
from jax import lax
from jax.experimental import pallas as pl
from jax.experimental.pallas import tpu as pltpu

F32 = jnp.float32
BF = jnp.bfloat16

HEADS = 8
D_NOPE = 128
D_ROPE = 64
D_V = 128
D_QK = D_NOPE + D_ROPE
R_Q = 512
R_KV = 512
MLA_W = HEADS * D_V
CONV_K = 31
CONV_PAD = CONV_K // 2
ROPE_BASE = 10000.0
LOG2E = 1.4426950408889634
LN2 = 0.6931471805599453
LN_EPS = 1e-5
RMS_EPS = 1e-6
ALPHA = (2.0 * 1) ** 0.25
ADAM_LR = 0.001
ADAM_B1 = 0.9
ADAM_B2 = 0.999
ADAM_EPS = 1e-08
ADAM_WD = 0.01
ADAM_STEP = 10

LANE = 128
SUB = 8
HALO = 16
N_CHIP = 4
MESH = pl.DeviceIdType.MESH
VMEM_MB = 1024 * 1024


def _call(body, **kw):
    return pl.pallas_call(body, **kw)


def _cp(sem, mb=48):
    return pltpu.CompilerParams(dimension_semantics=sem, vmem_limit_bytes=mb * VMEM_MB)


def _sds(shape, dt):
    return jax.ShapeDtypeStruct(shape, dt)


def _dot(a, b):
    return jnp.dot(a, b, preferred_element_type=F32)


def _dot_nt(a, b):
    return lax.dot_general(a, b, (((1,), (1,)), ((), ())), preferred_element_type=F32)


def _dot_tn(a, b):
    return lax.dot_general(a, b, (((0,), (0,)), ((), ())), preferred_element_type=F32)


def _rows8(v):
    t, n = v.shape
    return v.reshape(t // SUB, SUB, n).sum(axis=0)


def _ln_stats(r):
    mu = jnp.mean(r, axis=-1, keepdims=True)
    xc = r - mu
    var = jnp.mean(xc * xc, axis=-1, keepdims=True)
    rstd = lax.rsqrt(var + LN_EPS)
    return xc * rstd, rstd


def _ln_bwd(dy, xhat, rstd, g):
    dyh = dy * g
    m1 = jnp.mean(dyh, axis=-1, keepdims=True)
    m2 = jnp.mean(dyh * xhat, axis=-1, keepdims=True)
    return rstd * (dyh - m1 - xhat * m2)


def _rms_fwd(x, g):
    rr = lax.rsqrt(jnp.mean(x * x, axis=-1, keepdims=True) + RMS_EPS)
    xh = x * rr
    return xh * g, xh, rr


def _rms_bwd(dy, xh, rr, g):
    dyg = dy * g
    return rr * (dyg - xh * jnp.mean(dyg * xh, axis=-1, keepdims=True))


def _rope128(x, cos, sin_signed):
    lane = lax.broadcasted_iota(jnp.int32, x.shape, 1)
    rot = jnp.where(lane < D_ROPE // 2, pltpu.roll(x, LANE - D_ROPE // 2, 1), pltpu.roll(x, D_ROPE // 2, 1))
    return x * cos + rot * sin_signed


def _unrope128(dy, cos, sin_signed):
    t = dy * sin_signed
    lane = lax.broadcasted_iota(jnp.int32, dy.shape, 1)
    rot = jnp.where(lane < D_ROPE // 2, pltpu.roll(t, LANE - D_ROPE // 2, 1), pltpu.roll(t, D_ROPE // 2, 1))
    return dy * cos + rot


def _as_row(col):
    return jnp.transpose(jnp.broadcast_to(col, (col.shape[0], LANE)))[0:1, :]


def _sigmoid(x):
    return 1.0 / (1.0 + jnp.exp(-x))


def _row_chunks(tm, fn, rc=128):
    rc = min(rc, tm)

    def step(ci, carry):
        fn(pl.ds(pl.multiple_of(ci * rc, rc), rc))
        return carry

    lax.fori_loop(0, tm // rc, step, 0)


def _unrolled_loop(n, unroll, fn, init):
    unroll = min(n, unroll)
    assert n % unroll == 0

    def body(t, carry):
        for u in range(unroll):
            carry = fn(t * unroll + u, carry)
        return carry

    return lax.fori_loop(0, n // unroll, body, init)


def _tile(s, want):
    t = min(s, want)
    assert s % t == 0
    return t


def rope_tables(pos_f, invf):
    s = pos_f.shape[0]
    tm = _tile(s, 1024)

    def body(p_ref, f_ref, c_ref, s_ref):
        ang = p_ref[...] * f_ref[...]
        lane = lax.broadcasted_iota(jnp.int32, ang.shape, 1)
        c = jnp.cos(ang)
        sn = jnp.sin(ang)
        c_ref[...] = jnp.where(lane < D_ROPE, c, 0.0)
        s_ref[...] = jnp.where(lane < D_ROPE // 2, -sn, jnp.where(lane < D_ROPE, sn, 0.0))

    return _call(
        body, name="rope_tables", grid=(s // tm,),
        in_specs=[pl.BlockSpec((tm, 1), lambda i: (i, 0)), pl.BlockSpec((1, LANE), lambda i: (0, 0))],
        out_specs=[pl.BlockSpec((tm, LANE), lambda i: (i, 0))] * 2,
        out_shape=[_sds((s, LANE), F32)] * 2,
        compiler_params=_cp(("arbitrary",)),
    )(pos_f, invf)


def ln_in_fwd(x, g, b):
    s, d = x.shape
    tm = _tile(s, 512)

    def body(x_ref, g_ref, b_ref, o_ref, ob_ref):
        xhat, _ = _ln_stats(x_ref[...])
        y = xhat * g_ref[...] + b_ref[...]
        o_ref[...] = y
        ob_ref[...] = y.astype(BF)

    row = pl.BlockSpec((1, d), lambda i: (0, 0))
    tok = pl.BlockSpec((tm, d), lambda i: (i, 0))
    return _call(
        body, name="ln_in_fwd", grid=(s // tm,), in_specs=[tok, row, row], out_specs=[tok, tok],
        out_shape=[_sds((s, d), F32), _sds((s, d), BF)], compiler_params=_cp(("arbitrary",)),
    )(x, g, b)


def matmul_nt(name, a, wt, tm, out_dtype=F32):
    s, k = a.shape
    n = wt.shape[0]
    tm = _tile(s, tm)

    def body(a_ref, w_ref, o_ref):
        o_ref[...] = _dot_nt(a_ref[...], w_ref[...]).astype(o_ref.dtype)

    return _call(
        body, name=name, grid=(s // tm,),
        in_specs=[pl.BlockSpec((tm, k), lambda i: (i, 0)), pl.BlockSpec((n, k), lambda i: (0, 0))],
        out_specs=pl.BlockSpec((tm, n), lambda i: (i, 0)),
        out_shape=_sds((s, n), out_dtype), compiler_params=_cp(("arbitrary",)),
    )(a, wt)


def q_proj(h, g_cq, wuq, cos, sin):
    s = h.shape[0]
    tm = _tile(s, 512)

    def body(h_ref, g_ref, w_ref, c_ref, s_ref, q_ref, n_ref):
        y, _, _ = _rms_fwd(h_ref[...], g_ref[...])
        yb = y.astype(BF)
        n_ref[...] = yb
        q = _dot(yb, w_ref[...])
        c = c_ref[...]
        sn = s_ref[...]
        for hd in range(HEADS):
            q_ref[hd, :, 0:LANE] = q[:, LANE * hd:LANE * (hd + 1)].astype(BF)
            qr = q[:, MLA_W + LANE * hd:MLA_W + LANE * (hd + 1)]
            q_ref[hd, :, LANE:2 * LANE] = _rope128(qr, c, sn).astype(BF)

    return _call(
        body, name="q_proj", grid=(s // tm,),
        in_specs=[pl.BlockSpec((tm, R_Q), lambda i: (i, 0)), pl.BlockSpec((1, R_Q), lambda i: (0, 0)),
                  pl.BlockSpec((R_Q, 2 * MLA_W), lambda i: (0, 0)),
                  pl.BlockSpec((tm, LANE), lambda i: (i, 0)), pl.BlockSpec((tm, LANE), lambda i: (i, 0))],
        out_specs=[pl.BlockSpec((HEADS, tm, 2 * LANE), lambda i: (0, i, 0)), pl.BlockSpec((tm, R_Q), lambda i: (i, 0))],
        out_shape=[_sds((HEADS, s, 2 * LANE), BF), _sds((s, R_Q), BF)], compiler_params=_cp(("arbitrary",)),
    )(h, g_cq, wuq, cos, sin)


def kv_proj(h, g_ckv, wuk, wuv, cos, sin, kr_blk):
    s = h.shape[0]
    tm = _tile(s, 512)

    def body(h_ref, kr_ref, g_ref, wk_ref, wv_ref, c_ref, s_ref, k_ref, kt_ref, v_ref, n_ref):
        y, _, _ = _rms_fwd(h_ref[...], g_ref[...])
        yb = y.astype(BF)
        n_ref[...] = yb
        kn = _dot(yb, wk_ref[...])
        v = _dot(yb, wv_ref[...])
        kr = _rope128(kr_ref[...], c_ref[...], s_ref[...])
        krb = kr.astype(BF)
        krt = kr.T.astype(BF)
        for hd in range(HEADS):
            knh = kn[:, LANE * hd:LANE * (hd + 1)]
            k_ref[hd, :, 0:LANE] = knh.astype(BF)
            k_ref[hd, :, LANE:2 * LANE] = krb
            kt_ref[hd, 0:LANE, :] = knh.T.astype(BF)
            kt_ref[hd, LANE:2 * LANE, :] = krt
            v_ref[hd] = v[:, LANE * hd:LANE * (hd + 1)].astype(BF)

    tab = pl.BlockSpec((tm, LANE), lambda i: (i, 0))
    wsp = pl.BlockSpec((R_KV, MLA_W), lambda i: (0, 0))
    return _call(
        body, name="kv_proj", grid=(s // tm,),
        in_specs=[pl.BlockSpec((tm, R_KV), lambda i: (i, 1)), pl.BlockSpec((tm, LANE), lambda i: (i, kr_blk)),
                  pl.BlockSpec((1, R_KV), lambda i: (0, 0)), wsp, wsp, tab, tab],
        out_specs=[pl.BlockSpec((HEADS, tm, 2 * LANE), lambda i: (0, i, 0)), pl.BlockSpec((HEADS, 2 * LANE, tm), lambda i: (0, 0, i)),
                   pl.BlockSpec((HEADS, tm, LANE), lambda i: (0, i, 0)), pl.BlockSpec((tm, R_KV), lambda i: (i, 0))],
        out_shape=[_sds((HEADS, s, 2 * LANE), BF), _sds((HEADS, 2 * LANE, s), BF), _sds((HEADS, s, LANE), BF), _sds((s, R_KV), BF)],
        compiler_params=_cp(("arbitrary",)),
    )(h, h, g_ckv, wuk, wuv, cos, sin)


def attn_fwd(qc, kc, v):
    _, s, _ = qc.shape
    tq = _tile(s, 256)
    tk = _tile(s, 512)
    scale = D_QK ** -0.5
    c2 = scale * LOG2E
    nk = s // tk
    nb = tk // LANE
    un = 8

    def body(q_ref, k_ref, v_ref, o_ref, ob_ref, l_ref, s_scr, m_scr):
        q = q_ref[...]

        def scores(j, mpart):
            off = pl.multiple_of(j * tk, tk)
            sc = _dot_nt(q, k_ref[pl.ds(off, tk), :]) * c2
            s_scr[:, pl.ds(off, tk)] = sc
            for b in range(nb):
                mpart = jnp.maximum(mpart, sc[:, LANE * b:LANE * (b + 1)])
            return mpart

        mpart = _unrolled_loop(nk, un, scores, jnp.full((tq, LANE), -jnp.inf, F32))
        m = jnp.max(mpart, axis=-1, keepdims=True)
        m_scr[...] = jnp.broadcast_to(m, (tq, LANE))

        def weigh(j, carry):
            lpart, acc = carry
            off = pl.multiple_of(j * tk, tk)
            ps = []
            for b in range(nb):
                p = jnp.exp2(s_scr[:, pl.ds(off + LANE * b, LANE)] - m_scr[...])
                lpart = lpart + p
                ps.append(p.astype(BF))
            acc = acc + _dot(jnp.concatenate(ps, axis=1), v_ref[pl.ds(off, tk), :])
            return lpart, acc

        lpart, acc = _unrolled_loop(nk, un, weigh, (jnp.zeros((tq, LANE), F32), jnp.zeros((tq, D_V), F32)))
        l = jnp.sum(lpart, axis=-1, keepdims=True)
        o = acc / l
        o_ref[...] = o
        ob_ref[...] = o.astype(BF)
        l_ref[...] = _as_row(m + jnp.log(l) * LOG2E)

    return _call(
        body, name="attn_fwd", grid=(HEADS, s // tq),
        in_specs=[pl.BlockSpec((None, tq, 2 * LANE), lambda h, i: (h, i, 0)),
                  pl.BlockSpec((None, s, 2 * LANE), lambda h, i: (h, 0, 0)),
                  pl.BlockSpec((None, s, LANE), lambda h, i: (h, 0, 0))],
        out_specs=[pl.BlockSpec((tq, LANE), lambda h, i: (i, h)), pl.BlockSpec((tq, LANE), lambda h, i: (i, h)),
                   pl.BlockSpec((None, 1, tq), lambda h, i: (h, 0, i))],
        out_shape=[_sds((s, MLA_W), F32), _sds((s, MLA_W), BF), _sds((HEADS, 1, s), F32)],
        scratch_shapes=[pltpu.VMEM((tq, s + LANE), F32), pltpu.VMEM((tq, LANE), F32)],
        compiler_params=_cp(("arbitrary", "arbitrary")),
    )(qc, kc, v)


def _halo_specs(tm, s, width, col):
    r = tm // HALO
    nb = s // HALO
    cur = pl.BlockSpec((tm, width), lambda i: (i, col))
    prev = pl.BlockSpec((HALO, width), lambda i: (jnp.maximum(i * r - 1, 0), col))
    nxt = pl.BlockSpec((HALO, width), lambda i: (jnp.minimum((i + 1) * r, nb - 1), col))
    return cur, prev, nxt


def _slab_shapes(tm, c):
    return (tm + 2 * HALO, c + LANE), (SUB - 1, tm + 2 * HALO - SUB, c + LANE)


def _fill_slab(slab, tm, prev, cur, nxt):
    i = pl.program_id(0)
    last = pl.num_programs(0) - 1
    c = cur.shape[1]
    slab[0:HALO, 0:c] = jnp.where(i > 0, prev, 0.0)
    slab[HALO:HALO + tm, 0:c] = cur
    slab[HALO + tm:2 * HALO + tm, 0:c] = jnp.where(i < last, nxt, 0.0)


def _rotate_slab(slab, rot, tm):
    rows = tm + 2 * HALO - SUB
    c = slab.shape[1] - LANE
    for b in range(1, SUB):
        rot[b - 1, :, 0:c] = slab[pl.ds(b, rows), 0:c]


def _shifted(slab, rot, start, rc, cs):
    b = start % SUB
    if b == 0:
        return slab[pl.ds(start, rc), cs]
    return rot[b - 1, pl.ds(start - b, rc), cs]


def conv_fwd(h, conv_w, conv_b, g_ln, b_ln):
    s = h.shape[0]
    c = conv_w.shape[1]
    tm = _tile(s, 256)
    rc = _tile(tm, 64)

    def body(a_ref, ap_ref, an_ref, g_ref, gp_ref, gn_ref, w_ref, cb_ref, lg_ref, lb_ref, co_ref, uc_ref, slab, rot):
        _fill_slab(slab, tm, ap_ref[...] * _sigmoid(gp_ref[...]), a_ref[...] * _sigmoid(g_ref[...]),
                   an_ref[...] * _sigmoid(gn_ref[...]))
        _rotate_slab(slab, rot, tm)

        def lane_block(cb, carry):
            cs = pl.ds(pl.multiple_of(cb * LANE, LANE), LANE)
            for r0 in range(0, tm, rc):
                acc = jnp.zeros((rc, LANE), F32)
                for k in range(CONV_K):
                    acc = acc + w_ref[k:k + 1, cs] * _shifted(slab, rot, r0 + HALO - CONV_PAD + k, rc, cs)
                uc_ref[r0:r0 + rc, cs] = acc + cb_ref[:, cs]
            return carry

        lax.fori_loop(0, c // LANE, lane_block, 0)
        xhat, _ = _ln_stats(uc_ref[...])
        cl = xhat * lg_ref[...] + lb_ref[...]
        co_ref[...] = (cl * _sigmoid(cl)).astype(BF)

    a_specs = _halo_specs(tm, s, c, 1)
    g_specs = _halo_specs(tm, s, c, 2)
    row = pl.BlockSpec((1, c), lambda i: (0, 0))
    tok = pl.BlockSpec((tm, c), lambda i: (i, 0))
    return _call(
        body, name="conv_fwd", grid=(s // tm,),
        in_specs=[*a_specs, *g_specs, pl.BlockSpec(conv_w.shape, lambda i: (0, 0)), row, row, row],
        out_specs=[tok, tok], out_shape=[_sds((s, c), BF), _sds((s, c), F32)],
        scratch_shapes=[pltpu.VMEM(shp, F32) for shp in _slab_shapes(tm, c)],
        compiler_params=_cp(("arbitrary",)),
    )(h, h, h, h, h, h, conv_w, conv_b, g_ln, b_ln)


def out_proj_ln1(ob, co, wout, x0, g1, b1):
    s, d = x0.shape
    kh = ob.shape[1]
    tm = _tile(s, 256)

    def body(o_ref, c_ref, w_ref, x_ref, g_ref, b_ref, r_ref, x1_ref, x1b_ref, acc):
        acc[...] = _dot(o_ref[...], w_ref[0:kh, :]) + _dot(c_ref[...], w_ref[kh:2 * kh, :])
        g = g_ref[...]
        b = b_ref[...]

        def chunk(rows):
            r = ALPHA * x_ref[rows, :] + acc[rows, :]
            r_ref[rows, :] = r
            xhat, _ = _ln_stats(r)
            y = xhat * g + b
            x1_ref[rows, :] = y
            x1b_ref[rows, :] = y.astype(BF)

        _row_chunks(tm, chunk)

    half = pl.BlockSpec((tm, kh), lambda i: (i, 0))
    tok = pl.BlockSpec((tm, d), lambda i: (i, 0))
    row = pl.BlockSpec((1, d), lambda i: (0, 0))
    return _call(
        body, name="out_proj_ln1", grid=(s // tm,),
        in_specs=[half, half, pl.BlockSpec((2 * kh, d), lambda i: (0, 0)), tok, row, row],
        out_specs=[tok, tok, tok], out_shape=[_sds((s, d), F32), _sds((s, d), F32), _sds((s, d), BF)],
        scratch_shapes=[pltpu.VMEM((tm, d), F32)], compiler_params=_cp(("arbitrary",)),
    )(ob, co, wout, x0, g1, b1)


def ff1_fwd(x1b, wff1_g):
    s, d = x1b.shape
    nsh, _, fs = wff1_g.shape
    tm = _tile(s, 1024)
    tn = _tile(fs, 1024)
    per = fs // tn

    def body(a_ref, w_ref, r_ref, a1_ref):
        r = jnp.maximum(_dot(a_ref[...], w_ref[...]), 0.0)
        r_ref[...] = r.astype(BF)
        a1_ref[...] = (r * r).astype(BF)

    out = pl.BlockSpec((tm, tn), lambda i, j: (i, j))
    return _call(
        body, name="ff1_fwd", grid=(s // tm, nsh * per),
        in_specs=[pl.BlockSpec((tm, d), lambda i, j: (i, 0)),
                  pl.BlockSpec((None, d, tn), lambda i, j: (j // per, 0, j % per))],
        out_specs=[out, out], out_shape=[_sds((s, nsh * fs), BF)] * 2,
        compiler_params=_cp(("arbitrary", "arbitrary")),
    )(x1b, wff1_g)


def ff2_ln2_loss(a1b, wff2, x1, target, g2, b2):
    s, f = a1b.shape
    d = x1.shape[1]
    tm = _tile(s, 512)
    tk = _tile(f, 2048)
    nk = f // tk

    def body(a_ref, w_ref, x_ref, t_ref, g_ref, b_ref, dr_ref, drb_ref, loss_ref, dg_ref, db_ref, acc):
        i = pl.program_id(0)
        k = pl.program_id(1)

        @pl.when(k == 0)
        def _():
            acc[...] = _dot(a_ref[...], w_ref[...])

        @pl.when(k > 0)
        def _():
            acc[...] += _dot(a_ref[...], w_ref[...])

        @pl.when(jnp.logical_and(i == 0, k == 0))
        def _():
            loss_ref[...] = jnp.zeros_like(loss_ref)
            dg_ref[...] = jnp.zeros_like(dg_ref)
            db_ref[...] = jnp.zeros_like(db_ref)

        @pl.when(k == nk - 1)
        def _():
            g = g_ref[...]

            def chunk(rows):
                r = ALPHA * x_ref[rows, :] + acc[rows, :]
                xhat, rstd = _ln_stats(r)
                e = xhat * g + b_ref[...] - t_ref[rows, :]
                e2 = _rows8(e * e)
                part = e2[:, 0:LANE]
                for c in range(1, d // LANE):
                    part = part + e2[:, LANE * c:LANE * (c + 1)]
                loss_ref[...] += part * (0.5 / d)
                dy = e * (1.0 / d)
                dg_ref[...] += _rows8(dy * xhat)
                db_ref[...] += _rows8(dy)
                dr = _ln_bwd(dy, xhat, rstd, g)
                dr_ref[rows, :] = dr
                drb_ref[rows, :] = dr.astype(BF)

            _row_chunks(tm, chunk)

    tok = pl.BlockSpec((tm, d), lambda i, k: (i, 0))
    row = pl.BlockSpec((1, d), lambda i, k: (0, 0))
    accs = pl.BlockSpec((SUB, d), lambda i, k: (0, 0))
    return _call(
        body, name="ff2_ln2_loss", grid=(s // tm, nk),
        in_specs=[pl.BlockSpec((tm, tk), lambda i, k: (i, k)), pl.BlockSpec((tk, d), lambda i, k: (k, 0)),
                  tok, tok, row, row],
        out_specs=[tok, tok, pl.BlockSpec((SUB, LANE), lambda i, k: (0, 0)), accs, accs],
        out_shape=[_sds((s, d), F32), _sds((s, d), BF), _sds((SUB, LANE), F32), _sds((SUB, d), F32), _sds((SUB, d), F32)],
        scratch_shapes=[pltpu.VMEM((tm, d), F32)], compiler_params=_cp(("arbitrary", "arbitrary"), 60),
    )(a1b, wff2, x1, target, g2, b2)


def ff2_bwd_act(dr2b, wff2, rb):
    s, d = dr2b.shape
    f = wff2.shape[0]
    tm = _tile(s, 1024)
    tn = _tile(f, 1024)

    def body(a_ref, w_ref, r_ref, o_ref):
        o_ref[...] = (_dot_nt(a_ref[...], w_ref[...]) * (2.0 * r_ref[...].astype(F32))).astype(BF)

    return _call(
        body, name="ff2_bwd_act", grid=(s // tm, f // tn),
        in_specs=[pl.BlockSpec((tm, d), lambda i, j: (i, 0)), pl.BlockSpec((tn, d), lambda i, j: (j, 0)),
                  pl.BlockSpec((tm, tn), lambda i, j: (i, j))],
        out_specs=pl.BlockSpec((tm, tn), lambda i, j: (i, j)), out_shape=_sds((s, f), BF),
        compiler_params=_cp(("arbitrary", "arbitrary")),
    )(dr2b, wff2, rb)


def wgrad(name, a, b, tm, tn, tk=2048, shards=1):
    s, m = a.shape
    n = b.shape[1]
    tm = _tile(m, tm)
    ns = n // shards
    tn = _tile(ns, tn)
    tk = _tile(s, tk)
    per = ns // tn

    def body(a_ref, b_ref, o_ref):
        k = pl.program_id(2)

        @pl.when(k == 0)
        def _():
            o_ref[...] = _dot_tn(a_ref[...], b_ref[...])

        @pl.when(k > 0)
        def _():
            o_ref[...] += _dot_tn(a_ref[...], b_ref[...])

    return _call(
        body, name=name, grid=(m // tm, n // tn, s // tk),
        in_specs=[pl.BlockSpec((tk, tm), lambda i, j, k: (k, i)), pl.BlockSpec((tk, tn), lambda i, j, k: (k, j))],
        out_specs=pl.BlockSpec((None, tm, tn), lambda i, j, k: (j // per, i, j % per)),
        out_shape=_sds((shards, m, ns), F32), compiler_params=_cp(("arbitrary", "arbitrary", "arbitrary")),
    )(a, b)


def ff1_bwd_ln1(df1b, wff1_g, dr2, r1, g1):
    s, f = df1b.shape
    d = dr2.shape[1]
    tm = _tile(s, 512)
    tk = _tile(wff1_g.shape[2], 2048)
    per = wff1_g.shape[2] // tk
    nk = f // tk

    def body(a_ref, w_ref, d2_ref, r_ref, g_ref, dr_ref, drb_ref, dg_ref, db_ref, acc):
        i = pl.program_id(0)
        k = pl.program_id(1)

        @pl.when(k == 0)
        def _():
            acc[...] = _dot_nt(a_ref[...], w_ref[...])

        @pl.when(k > 0)
        def _():
            acc[...] += _dot_nt(a_ref[...], w_ref[...])

        @pl.when(jnp.logical_and(i == 0, k == 0))
        def _():
            dg_ref[...] = jnp.zeros_like(dg_ref)
            db_ref[...] = jnp.zeros_like(db_ref)

        @pl.when(k == nk - 1)
        def _():
            g = g_ref[...]

            def chunk(rows):
                dy = ALPHA * d2_ref[rows, :] + acc[rows, :]
                xhat, rstd = _ln_stats(r_ref[rows, :])
                dg_ref[...] += _rows8(dy * xhat)
                db_ref[...] += _rows8(dy)
                dr = _ln_bwd(dy, xhat, rstd, g)
                dr_ref[rows, :] = dr
                drb_ref[rows, :] = dr.astype(BF)

            _row_chunks(tm, chunk)

    tok = pl.BlockSpec((tm, d), lambda i, k: (i, 0))
    accs = pl.BlockSpec((SUB, d), lambda i, k: (0, 0))
    return _call(
        body, name="ff1_bwd_ln1", grid=(s // tm, nk),
        in_specs=[pl.BlockSpec((tm, tk), lambda i, k: (i, k)), pl.BlockSpec((None, d, tk), lambda i, k: (k // per, 0, k % per)),
                  tok, tok, pl.BlockSpec((1, d), lambda i, k: (0, 0))],
        out_specs=[tok, tok, accs, accs],
        out_shape=[_sds((s, d), F32), _sds((s, d), BF), _sds((SUB, d), F32), _sds((SUB, d), F32)],
        scratch_shapes=[pltpu.VMEM((tm, d), F32)], compiler_params=_cp(("arbitrary", "arbitrary"), 60),
    )(df1b, wff1_g, dr2, r1, g1)


def out_proj_bwd(dr1b, woutt, o):
    s, d = dr1b.shape
    tm = _tile(s, 256)

    def body(a_ref, w_ref, o_ref, do_ref, dot_ref, dc_ref, dl_ref):
        dcat = _dot(a_ref[...], w_ref[...])
        do = dcat[:, 0:MLA_W]
        do_ref[...] = do.astype(BF)
        dc_ref[...] = dcat[:, MLA_W:]
        prod = do * o_ref[...]
        for hd in range(HEADS):
            hs = slice(LANE * hd, LANE * (hd + 1))
            dl_ref[hd] = _as_row(jnp.sum(prod[:, hs], axis=-1, keepdims=True))
            dot_ref[hd] = do[:, hs].T.astype(BF)

    half = pl.BlockSpec((tm, MLA_W), lambda i: (i, 0))
    return _call(
        body, name="out_proj_bwd", grid=(s // tm,),
        in_specs=[pl.BlockSpec((tm, d), lambda i: (i, 0)), pl.BlockSpec((d, d), lambda i: (0, 0)), half],
        out_specs=[half, pl.BlockSpec((HEADS, LANE, tm), lambda i: (0, 0, i)),
                   pl.BlockSpec((tm, d - MLA_W), lambda i: (i, 0)), pl.BlockSpec((HEADS, 1, tm), lambda i: (0, 0, i))],
        out_shape=[_sds((s, MLA_W), BF), _sds((HEADS, LANE, s), BF), _sds((s, d - MLA_W), F32), _sds((HEADS, 1, s), F32)],
        compiler_params=_cp(("arbitrary",)),
    )(dr1b, woutt, o)


def conv_bwd_ln(uc, dco, g_ln, b_ln):
    s, c = uc.shape
    tm = _tile(s, 512)

    def body(u_ref, d_ref, g_ref, b_ref, du_ref, dg_ref, db_ref, dcb_ref):
        @pl.when(pl.program_id(0) == 0)
        def _():
            dg_ref[...] = jnp.zeros_like(dg_ref)
            db_ref[...] = jnp.zeros_like(db_ref)
            dcb_ref[...] = jnp.zeros_like(dcb_ref)

        xhat, rstd = _ln_stats(u_ref[...])
        g = g_ref[...]
        cl = xhat * g + b_ref[...]
        sg = _sigmoid(cl)
        dcl = d_ref[...] * (sg * (1.0 + cl * (1.0 - sg)))
        dg_ref[...] += _rows8(dcl * xhat)
        db_ref[...] += _rows8(dcl)
        du = _ln_bwd(dcl, xhat, rstd, g)
        du_ref[...] = du
        dcb_ref[...] += _rows8(du)

    tok = pl.BlockSpec((tm, c), lambda i: (i, 0))
    row = pl.BlockSpec((1, c), lambda i: (0, 0))
    accs = pl.BlockSpec((SUB, c), lambda i: (0, 0))
    return _call(
        body, name="conv_bwd_ln", grid=(s // tm,), in_specs=[tok, tok, row, row], out_specs=[tok, accs, accs, accs],
        out_shape=[_sds((s, c), F32)] + [_sds((SUB, c), F32)] * 3, compiler_params=_cp(("arbitrary",)),
    )(uc, dco, g_ln, b_ln)


def conv_bwd_taps(h, duc, conv_w):
    s, c = duc.shape
    tm = _tile(s, 256)
    rc = _tile(tm, 64)

    def body(a_ref, ap_ref, an_ref, g_ref, gp_ref, gn_ref, d_ref, dp_ref, dn_ref, w_ref, o_ref, dw_ref,
             uslab, dslab, du_s, urot, drot, dw8):
        @pl.when(pl.program_id(0) == 0)
        def _():
            dw8[...] = jnp.zeros_like(dw8)

        sg = _sigmoid(g_ref[...])
        a = a_ref[...]
        _fill_slab(uslab, tm, ap_ref[...] * _sigmoid(gp_ref[...]), a * sg, an_ref[...] * _sigmoid(gn_ref[...]))
        _fill_slab(dslab, tm, dp_ref[...], d_ref[...], dn_ref[...])
        _rotate_slab(uslab, urot, tm)
        _rotate_slab(dslab, drot, tm)

        def lane_block(cb, carry):
            cs = pl.ds(pl.multiple_of(cb * LANE, LANE), LANE)
            for r0 in range(0, tm, rc):
                acc = jnp.zeros((rc, LANE), F32)
                for k in range(CONV_K):
                    acc = acc + w_ref[k:k + 1, cs] * _shifted(dslab, drot, r0 + HALO + CONV_PAD - k, rc, cs)
                du_s[r0:r0 + rc, cs] = acc
            return carry

        def lane_block_taps(cb, carry):
            cs = pl.ds(pl.multiple_of(cb * LANE, LANE), LANE)
            parts = []
            for k in range(CONV_K):
                prod = None
                for r0 in range(0, tm, rc):
                    t = dslab[pl.ds(r0 + HALO, rc), cs] * _shifted(uslab, urot, r0 + HALO - CONV_PAD + k, rc, cs)
                    prod = t if prod is None else prod + t
                parts.append(_rows8(prod))
            rows = SUB * CONV_K
            dw8[0:rows, cs] = dw8[0:rows, cs] + jnp.concatenate(parts, axis=0)
            return carry

        lax.fori_loop(0, c // LANE, lane_block, 0)
        lax.fori_loop(0, c // LANE, lane_block_taps, 0)

        @pl.when(pl.program_id(0) == pl.num_programs(0) - 1)
        def _():
            dw_ref[...] = jnp.zeros_like(dw_ref)
            for k in range(CONV_K):
                dw_ref[k:k + 1, :] = jnp.sum(dw8[SUB * k:SUB * (k + 1), :], axis=0, keepdims=True)

        du = du_s[...]
        o_ref[:, 0:c] = (du * sg).astype(BF)
        o_ref[:, c:2 * c] = (du * a * sg * (1.0 - sg)).astype(BF)

    a_specs = _halo_specs(tm, s, c, 1)
    g_specs = _halo_specs(tm, s, c, 2)
    d_specs = _halo_specs(tm, s, c, 0)
    wsp = pl.BlockSpec(conv_w.shape, lambda i: (0, 0))
    return _call(
        body, name="conv_bwd_taps", grid=(s // tm,), in_specs=[*a_specs, *g_specs, *d_specs, wsp],
        out_specs=[pl.BlockSpec((tm, 2 * c), lambda i: (i, 0)), wsp],
        out_shape=[_sds((s, 2 * c), BF), _sds(conv_w.shape, F32)],
        scratch_shapes=[pltpu.VMEM(_slab_shapes(tm, c)[0], F32), pltpu.VMEM(_slab_shapes(tm, c)[0], F32), pltpu.VMEM((tm, c), F32),
                        pltpu.VMEM(_slab_shapes(tm, c)[1], F32), pltpu.VMEM(_slab_shapes(tm, c)[1], F32),
                        pltpu.VMEM((SUB * conv_w.shape[0], c), F32)],
        compiler_params=_cp(("arbitrary",)),
    )(h, h, h, h, h, h, duc, duc, duc, conv_w)


def attn_bwd(qc, kc, kct, v, dob, dot, lse_r, delta_r):
    _, s, _ = qc.shape
    tk = _tile(s, 512)
    tq = _tile(s, 512)
    scale = D_QK ** -0.5
    c2 = scale * LOG2E

    def body(k_ref, kt_ref, v_ref, q_ref, do_ref, dot_ref, l_ref, dl_ref, dqt_ref, dk_ref, dvt_ref):
        @pl.when(pl.program_id(1) == 0)
        def _():
            dqt_ref[...] = jnp.zeros_like(dqt_ref)

        k = k_ref[...]
        kt = kt_ref[0:D_QK, :]
        vv = v_ref[...]

        def step(i, carry):
            dk, dvt = carry
            off = pl.multiple_of(i * tq, tq)
            q = q_ref[pl.ds(off, tq), :]
            do = do_ref[pl.ds(off, tq), :]
            pt = jnp.exp2(_dot_nt(k, q) * c2 - l_ref[:, pl.ds(off, tq)])
            dvt = dvt + _dot_nt(dot_ref[:, pl.ds(off, tq)], pt.astype(BF))
            dpt = _dot_nt(vv, do)
            dsb = (pt * (dpt - dl_ref[:, pl.ds(off, tq)]) * scale).astype(BF)
            dk = dk + _dot(dsb, q)
            dqt_ref[:, pl.ds(off, tq)] += _dot(kt, dsb)
            return dk, dvt

        dk, dvt = _unrolled_loop(s // tq, 16, step, (jnp.zeros((tk, 2 * LANE), F32), jnp.zeros((LANE, tk), F32)))
        dk_ref[...] = dk
        dvt_ref[...] = dvt

    rowv = pl.BlockSpec((None, 1, s), lambda h, j: (h, 0, 0))
    return _call(
        body, name="attn_bwd", grid=(HEADS, s // tk),
        in_specs=[pl.BlockSpec((None, tk, 2 * LANE), lambda h, j: (h, j, 0)),
                  pl.BlockSpec((None, 2 * LANE, tk), lambda h, j: (h, 0, j)),
                  pl.BlockSpec((None, tk, LANE), lambda h, j: (h, j, 0)),
                  pl.BlockSpec((None, s, 2 * LANE), lambda h, j: (h, 0, 0)),
                  pl.BlockSpec((s, LANE), lambda h, j: (0, h)),
                  pl.BlockSpec((None, LANE, s), lambda h, j: (h, 0, 0)), rowv, rowv],
        out_specs=[pl.BlockSpec((None, D_QK, s), lambda h, j: (h, 0, 0)),
                   pl.BlockSpec((None, tk, 2 * LANE), lambda h, j: (h, j, 0)),
                   pl.BlockSpec((None, LANE, tk), lambda h, j: (h, 0, j))],
        out_shape=[_sds((HEADS, D_QK, s), F32), _sds((HEADS, s, 2 * LANE), F32), _sds((HEADS, LANE, s), F32)],
        compiler_params=_cp(("arbitrary", "arbitrary"), 56),
    )(kc, kct, v, qc, dob, dot, lse_r, delta_r)


def q_bwd(dqt, h, g_cq, wuqt, cos, sin):
    s = h.shape[0]
    tm = _tile(s, 256)

    def body(d_ref, h_ref, g_ref, w_ref, c_ref, s_ref, dq_ref, dc_ref, dg_ref):
        @pl.when(pl.program_id(0) == 0)
        def _():
            dg_ref[...] = jnp.zeros_like(dg_ref)

        c = c_ref[...]
        sn = s_ref[...]
        pad = jnp.zeros((LANE - D_ROPE, tm), F32)
        for hd in range(HEADS):
            dq_ref[:, LANE * hd:LANE * (hd + 1)] = d_ref[hd, 0:D_NOPE, :].T.astype(BF)
            dqr = jnp.concatenate([d_ref[hd, D_NOPE:D_QK, :], pad], axis=0).T
            dq_ref[:, MLA_W + LANE * hd:MLA_W + LANE * (hd + 1)] = _unrope128(dqr, c, sn).astype(BF)
        dy = _dot(dq_ref[...], w_ref[...])
        g = g_ref[...]
        _, xh, rr = _rms_fwd(h_ref[...], g)
        dg_ref[...] += _rows8(dy * xh)
        dc_ref[...] = _rms_bwd(dy, xh, rr, g).astype(BF)

    tab = pl.BlockSpec((tm, LANE), lambda i: (i, 0))
    return _call(
        body, name="q_bwd", grid=(s // tm,),
        in_specs=[pl.BlockSpec((HEADS, D_QK, tm), lambda i: (0, 0, i)), pl.BlockSpec((tm, R_Q), lambda i: (i, 0)),
                  pl.BlockSpec((1, R_Q), lambda i: (0, 0)), pl.BlockSpec((2 * MLA_W, R_Q), lambda i: (0, 0)), tab, tab],
        out_specs=[pl.BlockSpec((tm, 2 * MLA_W), lambda i: (i, 0)), pl.BlockSpec((tm, R_Q), lambda i: (i, 0)),
                   pl.BlockSpec((SUB, R_Q), lambda i: (0, 0))],
        out_shape=[_sds((s, 2 * MLA_W), BF), _sds((s, R_Q), BF), _sds((SUB, R_Q), F32)],
        compiler_params=_cp(("arbitrary",)),
    )(dqt, h, g_cq, wuqt, cos, sin)


def kv_bwd(dk, dv, h, g_ckv, wukt, wuvt, cos, sin):
    s = h.shape[0]
    tm = _tile(s, 256)

    def body(dk_ref, dv_ref, h_ref, g_ref, wk_ref, wv_ref, c_ref, s_ref, dkn_ref, dvb_ref, dc_ref, dkr_ref, dg_ref):
        @pl.when(pl.program_id(0) == 0)
        def _():
            dg_ref[...] = jnp.zeros_like(dg_ref)

        dkr = dk_ref[0, :, LANE:2 * LANE]
        for hd in range(HEADS):
            dkn_ref[:, LANE * hd:LANE * (hd + 1)] = dk_ref[hd, :, 0:LANE].astype(BF)
            dvb_ref[:, LANE * hd:LANE * (hd + 1)] = dv_ref[hd].T.astype(BF)
            if hd > 0:
                dkr = dkr + dk_ref[hd, :, LANE:2 * LANE]
        dkr_ref[...] = _unrope128(dkr, c_ref[...], s_ref[...]).astype(BF)
        dy = _dot(dkn_ref[...], wk_ref[...]) + _dot(dvb_ref[...], wv_ref[...])
        g = g_ref[...]
        _, xh, rr = _rms_fwd(h_ref[...], g)
        dg_ref[...] += _rows8(dy * xh)
        dc_ref[...] = _rms_bwd(dy, xh, rr, g).astype(BF)

    tab = pl.BlockSpec((tm, LANE), lambda i: (i, 0))
    wsp = pl.BlockSpec((MLA_W, R_KV), lambda i: (0, 0))
    wide = pl.BlockSpec((tm, MLA_W), lambda i: (i, 0))
    return _call(
        body, name="kv_bwd", grid=(s // tm,),
        in_specs=[pl.BlockSpec((HEADS, tm, 2 * LANE), lambda i: (0, i, 0)), pl.BlockSpec((HEADS, LANE, tm), lambda i: (0, 0, i)),
                  pl.BlockSpec((tm, R_KV), lambda i: (i, 1)), pl.BlockSpec((1, R_KV), lambda i: (0, 0)), wsp, wsp, tab, tab],
        out_specs=[wide, wide, pl.BlockSpec((tm, R_KV), lambda i: (i, 0)), tab, pl.BlockSpec((SUB, R_KV), lambda i: (0, 0))],
        out_shape=[_sds((s, MLA_W), BF), _sds((s, MLA_W), BF), _sds((s, R_KV), BF), _sds((s, LANE), BF), _sds((SUB, R_KV), F32)],
        compiler_params=_cp(("arbitrary",)),
    )(dk, dv, h, g_ckv, wukt, wuvt, cos, sin)


def in_proj_bwd_ln(dh, wint, dr1, x, g_in):
    s, hc = dh.shape
    d = x.shape[1]
    tm = _tile(s, 256)

    def body(a_ref, w_ref, d1_ref, x_ref, g_ref, gx_ref, dg_ref, db_ref, acc):
        @pl.when(pl.program_id(0) == 0)
        def _():
            dg_ref[...] = jnp.zeros_like(dg_ref)
            db_ref[...] = jnp.zeros_like(db_ref)

        acc[...] = _dot(a_ref[...], w_ref[...])
        g = g_ref[...]

        def chunk(rows):
            dy = ALPHA * d1_ref[rows, :] + acc[rows, :]
            xhat, rstd = _ln_stats(x_ref[rows, :])
            dg_ref[...] += _rows8(dy * xhat)
            db_ref[...] += _rows8(dy)
            gx_ref[rows, :] = _ln_bwd(dy, xhat, rstd, g)

        _row_chunks(tm, chunk)

    tok = pl.BlockSpec((tm, d), lambda i: (i, 0))
    accs = pl.BlockSpec((SUB, d), lambda i: (0, 0))
    return _call(
        body, name="in_proj_bwd_ln", grid=(s // tm,),
        in_specs=[pl.BlockSpec((tm, hc), lambda i: (i, 0)), pl.BlockSpec((hc, d), lambda i: (0, 0)),
                  tok, tok, pl.BlockSpec((1, d), lambda i: (0, 0))],
        out_specs=[tok, accs, accs], out_shape=[_sds((s, d), F32), _sds((SUB, d), F32), _sds((SUB, d), F32)],
        scratch_shapes=[pltpu.VMEM((tm, d), F32)], compiler_params=_cp(("arbitrary",), 56),
    )(dh, wint, dr1, x, g_in)


def _adamw_math(w, g, m, v):
    m = ADAM_B1 * m + (1.0 - ADAM_B1) * g
    v = ADAM_B2 * v + (1.0 - ADAM_B2) * (g * g)
    m_hat = m / (1.0 - ADAM_B1 ** ADAM_STEP)
    v_hat = v / (1.0 - ADAM_B2 ** ADAM_STEP)
    delta = -ADAM_LR * (m_hat / (jnp.sqrt(v_hat) + ADAM_EPS) + ADAM_WD * w)
    return delta, m, v


def adamw(name, w, g, m, v):
    r, c = w.shape
    tr = _row_tile(r, c)

    def body(w_ref, g_ref, m_ref, v_ref, d_ref, mo_ref, vo_ref):
        d_ref[...], mo_ref[...], vo_ref[...] = _adamw_math(w_ref[...], g_ref[...], m_ref[...], v_ref[...])

    blk = pl.BlockSpec((tr, c), lambda i: (i, 0))
    return _call(
        body, name=name, grid=(r // tr,), in_specs=[blk] * 4, out_specs=[blk] * 3,
        out_shape=[_sds((r, c), F32)] * 3, compiler_params=_cp(("arbitrary",)),
    )(w, g, m, v)


def _coords():
    return lax.axis_index("x"), lax.axis_index("y"), lax.axis_index("c")


def _other_chips(x, y):
    return [(1 - x, y, 2 * (1 - x) + y), (x, 1 - y, 2 * x + 1 - y), (1 - x, 1 - y, 2 * (1 - x) + 1 - y)]


ANY = pl.BlockSpec(memory_space=pl.ANY)
HBM = pl.BlockSpec(memory_space=pltpu.HBM)
SEM = pl.BlockSpec(memory_space=pltpu.SEMAPHORE)
EFFECT = pltpu.SideEffectType.DATAFLOW_SIDE_EFFECTING


def _in_hbm(a):
    return pltpu.with_memory_space_constraint(a, pltpu.HBM)


def _split_plan(mode, src, land, x, y, c):
    if mode == "pair":
        rh = src.shape[1] // 2
        return [((x, y, 1 - c), src.at[:, pl.ds((1 - c) * rh, rh)], land, land)]
    me = 2 * x + y
    plan = []
    for j, (px, py, pk) in enumerate(_other_chips(x, y)):
        if mode == "gather":
            plan.append(((px, py, c), src, land.at[me], land.at[pk]))
        elif mode == "gather_half":
            mine = pl.ds(c * (src.shape[0] // 2), src.shape[0] // 2)
            plan.append(((px, py, c), src.at[mine], land.at[me, mine], land.at[pk, mine]))
        else:
            plan.append(((px, py, c), src.at[pk], land.at[j], land.at[j]))
    return plan


def _plan_len(mode):
    return 1 if mode == "pair" else N_CHIP - 1


def split_send_start(name, mode, srcs, land_shapes, order_after):
    n = len(srcs)
    np_ = _plan_len(mode)

    def body(*refs):
        ins, lands = refs[:n], refs[n:2 * n]
        ss, rs = refs[2 * n + 1], refs[2 * n + 2]
        token = refs[-1]
        x, y, c = _coords()
        for a in range(n):
            for j, (peer, src, dst, _) in enumerate(_split_plan(mode, ins[a], lands[a], x, y, c)):
                pltpu.make_async_remote_copy(src_ref=src, dst_ref=dst, send_sem=ss.at[np_ * a + j], recv_sem=rs.at[np_ * a + j],
                                             device_id=peer, device_id_type=MESH).start()
        token[...] = jnp.zeros_like(token)

    lands = [lax.empty(shp, s.dtype) for shp, s in zip(land_shapes, srcs)]
    outs = _call(
        body, name=name,
        out_shape=(pltpu.SemaphoreType.DMA((np_ * n,)), pltpu.SemaphoreType.DMA((np_ * n,)),
                   *[pltpu.HBM(s.shape, s.dtype) for s in srcs], *[pltpu.HBM(l.shape, l.dtype) for l in lands],
                   _sds((SUB, LANE), F32)),
        in_specs=[HBM] * (2 * n) + [ANY], out_specs=(SEM, SEM, *[HBM] * (2 * n), pl.BlockSpec(memory_space=pltpu.VMEM)),
        input_output_aliases={a: 2 + a for a in range(2 * n)},
        compiler_params=pltpu.CompilerParams(has_side_effects=EFFECT),
    )(*[_in_hbm(s) for s in srcs], *[_in_hbm(l) for l in lands], order_after)
    return outs[0], outs[1], list(outs[2:2 + n]), list(outs[2 + n:2 + 2 * n]), outs[-1]


def split_send_wait(name, mode, ss, rs, srcs, lands, order_after):
    n = len(srcs)
    np_ = _plan_len(mode)

    def body(*refs):
        ins, lnd = refs[:n], refs[n:2 * n]
        s_ref, r_ref = refs[2 * n], refs[2 * n + 1]
        x, y, c = _coords()
        for a in range(n):
            for j, (peer, src, _, got) in enumerate(_split_plan(mode, ins[a], lnd[a], x, y, c)):
                cp = pltpu.make_async_remote_copy(src_ref=src, dst_ref=got, send_sem=s_ref.at[np_ * a + j], recv_sem=r_ref.at[np_ * a + j],
                                                  device_id=peer, device_id_type=MESH)
                cp.wait_send()
                cp.wait_recv()

    outs = _call(
        body, name=name, out_shape=tuple(pltpu.HBM(t.shape, t.dtype) for t in (*srcs, *lands)),
        in_specs=[HBM] * (2 * n) + [SEM, SEM, ANY], out_specs=tuple([HBM] * (2 * n)),
        input_output_aliases={a: a for a in range(2 * n)},
        compiler_params=pltpu.CompilerParams(has_side_effects=EFFECT),
    )(*srcs, *lands, ss, rs, order_after)
    return list(outs[:n]), list(outs[n:])


def swap_gathered_halves(lands):
    n = len(lands)

    def body(*refs):
        outs = refs[n:2 * n]
        ss, rs = refs[2 * n:]
        x, y, c = _coords()
        cps = []
        for a in range(n):
            rh = outs[a].shape[1] // 2
            for j, (px, py, pk) in enumerate(_other_chips(x, y)):
                held = outs[a].at[pk, pl.ds(c * rh, rh)]
                cp = pltpu.make_async_remote_copy(src_ref=held, dst_ref=held, send_sem=ss.at[a, j], recv_sem=rs.at[a, j],
                                                  device_id=(x, y, 1 - c), device_id_type=MESH)
                cp.start()
                cps.append(cp)
        for a in range(n):
            rh = outs[a].shape[1] // 2
            for j, (px, py, pk) in enumerate(_other_chips(x, y)):
                theirs = outs[a].at[pk, pl.ds((1 - c) * rh, rh)]
                pltpu.make_async_remote_copy(src_ref=theirs, dst_ref=theirs, send_sem=ss.at[a, j], recv_sem=rs.at[a, j],
                                             device_id=(x, y, 1 - c), device_id_type=MESH).wait_recv()
        for cp in cps:
            cp.wait_send()

    return _call(
        body, name="swap_gathered_halves", in_specs=[ANY] * n, out_specs=[ANY] * n,
        out_shape=[_sds(l.shape, l.dtype) for l in lands], input_output_aliases={a: a for a in range(n)},
        scratch_shapes=[pltpu.SemaphoreType.DMA((n, 3))] * 2,
    )(*lands)


def pair_exchange(grads, tag):
    n = len(grads)

    def body(*refs):
        ins, outs = refs[:n], refs[n:2 * n]
        ss, rs = refs[2 * n:]
        x, y, c = _coords()
        cps = []
        for a in range(n):
            rh = ins[a].shape[1] // 2
            cp = pltpu.make_async_remote_copy(
                src_ref=ins[a].at[:, pl.ds((1 - c) * rh, rh)], dst_ref=outs[a], send_sem=ss.at[a], recv_sem=rs.at[a],
                device_id=(x, y, 1 - c), device_id_type=MESH)
            cp.start()
            cps.append(cp)
        for cp in cps:
            cp.wait()

    return _call(
        body, name="pair_exchange_" + tag, in_specs=[ANY] * n, out_specs=[ANY] * n,
        out_shape=[_sds((N_CHIP, g.shape[1] // 2, g.shape[2]), F32) for g in grads],
        scratch_shapes=[pltpu.SemaphoreType.DMA((n,))] * 2,
    )(*grads)


def _row_tile(rows, cols, itemsize=4, budget=2 * VMEM_MB):
    fits = [t for t in range(SUB, rows + 1, SUB) if rows % t == 0 and t * cols * itemsize <= budget]
    return max(fits) if fits and rows * cols * itemsize > budget else rows


def pair_add(g, r, cidx):
    _, rows, cols = g.shape
    rh = rows // 2
    tr = _row_tile(rh, cols)
    per = rh // tr

    def body(c_ref, g_ref, r_ref, o_ref):
        o_ref[...] = g_ref[...] + r_ref[...]

    return _call(
        body, name="pair_add",
        grid_spec=pltpu.PrefetchScalarGridSpec(
            num_scalar_prefetch=1, grid=(N_CHIP, per),
            in_specs=[pl.BlockSpec((None, tr, cols), lambda k, i, c: (k, c[0] * per + i, 0)),
                      pl.BlockSpec((None, tr, cols), lambda k, i, c: (k, i, 0))],
            out_specs=pl.BlockSpec((None, tr, cols), lambda k, i, c: (k, i, 0))),
        out_shape=_sds((N_CHIP, rh, cols), F32), compiler_params=_cp(("arbitrary", "arbitrary")),
    )(cidx, g, r)


def chip_add(p, r, kc):
    _, rh, cols = p.shape
    tr = _row_tile(rh, cols)
    per = rh // tr

    def body(k_ref, p_ref, r_ref, o_ref):
        o_ref[...] = ((p_ref[...] + r_ref[0]) + r_ref[1]) + r_ref[2]

    return _call(
        body, name="chip_add",
        grid_spec=pltpu.PrefetchScalarGridSpec(
            num_scalar_prefetch=1, grid=(per,),
            in_specs=[pl.BlockSpec((None, tr, cols), lambda i, k: (k[0], i, 0)),
                      pl.BlockSpec((N_CHIP - 1, tr, cols), lambda i, k: (0, i, 0))],
            out_specs=pl.BlockSpec((tr, cols), lambda i, k: (k[1] * per + i, 0))),
        out_shape=_sds((2 * rh, cols), F32), compiler_params=_cp(("arbitrary",)),
    )(kc, p, r)


def pair_share(fulls, tag):
    n = len(fulls)

    def body(*refs):
        outs = refs[n:2 * n]
        ss, rs = refs[2 * n:]
        x, y, c = _coords()
        cps = []
        for a in range(n):
            rh = outs[a].shape[0] // 2
            mine = outs[a].at[pl.ds(c * rh, rh)]
            cp = pltpu.make_async_remote_copy(
                src_ref=mine, dst_ref=mine, send_sem=ss.at[a], recv_sem=rs.at[a],
                device_id=(x, y, 1 - c), device_id_type=MESH)
            cp.start()
            cps.append(cp)
        for a, cp in enumerate(cps):
            rh = outs[a].shape[0] // 2
            theirs = outs[a].at[pl.ds((1 - c) * rh, rh)]
            cp.wait_send()
            pltpu.make_async_remote_copy(
                src_ref=theirs, dst_ref=theirs, send_sem=ss.at[a], recv_sem=rs.at[a],
                device_id=(x, y, 1 - c), device_id_type=MESH).wait_recv()

    return _call(
        body, name="pair_share_" + tag, in_specs=[ANY] * n, out_specs=[ANY] * n,
        out_shape=[_sds(f.shape, F32) for f in fulls], input_output_aliases={a: a for a in range(n)},
        scratch_shapes=[pltpu.SemaphoreType.DMA((n,))] * 2,
    )(*fulls)


def small_allreduce_adamw(part, w, m, v):
    n = part.shape[1]

    def body(p_ref, w_ref, m_ref, v_ref, g_ref, d_ref, mo_ref, vo_ref, mine, gath, ss, rs):
        x, y, c = _coords()
        me = 4 * x + 2 * y + c
        mine[...] = jnp.sum(p_ref[...], axis=0, keepdims=True)
        gath[me] = mine[...]
        cps = []
        for k in range(1, 8):
            px, py, pc = x ^ (k >> 2), y ^ ((k >> 1) & 1), c ^ (k & 1)
            cp = pltpu.make_async_remote_copy(
                src_ref=mine, dst_ref=gath.at[me], send_sem=ss.at[k - 1], recv_sem=rs.at[k - 1],
                device_id=(px, py, pc), device_id_type=MESH)
            cp.start()
            cps.append(cp)
        for k in range(1, 8):
            src = 4 * (x ^ (k >> 2)) + 2 * (y ^ ((k >> 1) & 1)) + (c ^ (k & 1))
            pltpu.make_async_remote_copy(
                src_ref=mine, dst_ref=gath.at[src], send_sem=ss.at[k - 1], recv_sem=rs.at[k - 1],
                device_id=(x, y, c), device_id_type=MESH).wait_recv()
        for cp in cps:
            cp.wait_send()
        g = gath[0]
        for dv in range(1, 8):
            g = g + gath[dv]
        g_ref[...] = g
        d_ref[...], mo_ref[...], vo_ref[...] = _adamw_math(w_ref[...], g, m_ref[...], v_ref[...])

    vm = pl.BlockSpec(memory_space=pltpu.VMEM)
    return _call(
        body, name="small_allreduce_adamw", in_specs=[vm] * 4, out_specs=[vm] * 4, out_shape=[_sds((1, n), F32)] * 4,
        scratch_shapes=[pltpu.VMEM((1, n), F32), pltpu.VMEM((8, 1, n), F32),
                        pltpu.SemaphoreType.DMA((7,)), pltpu.SemaphoreType.DMA((7,))],
    )(part, w, m, v)


def _unshard_cols(g):
    k, r, cs = g.shape
    return g.transpose(1, 0, 2).reshape(r, k * cs)


def _shard_cols(w):
    r, c = w.shape
    return w.reshape(r, N_CHIP, c // N_CHIP).transpose(1, 0, 2)


def local_step(x, positions, ln_in_g, ln_in_b, g_cq, g_ckv, conv_b, g_conv_ln, b_conv_ln, g_ln1, b_ln1, g_ln2, b_ln2,
               target, start_token, hooks):
    s, d = x.shape
    c = d - MLA_W
    row = lambda a: a.reshape(1, -1)
    ln_in_g = row(ln_in_g) + start_token[0:1, 0:1]

    half = D_ROPE // 2
    inv_freq = ROPE_BASE ** (-jnp.arange(half, dtype=F32) * (2.0 / D_ROPE))
    invf = jnp.concatenate([inv_freq, inv_freq, jnp.zeros((LANE - D_ROPE,), F32)]).reshape(1, LANE)
    cos, sin = rope_tables(positions.astype(F32).reshape(s, 1), invf)
    x0, x0b = ln_in_fwd(x, ln_in_g, row(ln_in_b))
    win_g, wuq_g, wuk_g, wuv_g, convw_g = hooks.early_weights(x0b)

    o_kr = R_Q + R_KV
    o_cv = o_kr + D_ROPE
    n_in = o_cv + 2 * c
    per = n_in // N_CHIP

    def in_cols(a, b):
        return [win_g[k, max(a, per * k) - per * k:min(b, per * (k + 1)) - per * k]
                for k in range(N_CHIP) if max(a, per * k) < min(b, per * (k + 1))]

    win_rt = jnp.concatenate(in_cols(0, o_kr) + in_cols(o_cv, n_in) + in_cols(o_kr, o_cv)
                             + [jnp.zeros((LANE - D_ROPE, d), BF)], axis=0)
    kr_blk = (o_kr + 2 * c) // LANE
    wuq = _unshard_cols(wuq_g).reshape(R_Q, HEADS, D_QK)
    wuq_r = jnp.concatenate([wuq[:, :, :D_NOPE].reshape(R_Q, MLA_W),
                             jnp.pad(wuq[:, :, D_NOPE:], ((0, 0), (0, 0), (0, LANE - D_ROPE))).reshape(R_Q, MLA_W)], axis=1)
    wuk = _unshard_cols(wuk_g)
    wuv = _unshard_cols(wuv_g)
    conv_w = jnp.pad(_unshard_cols(convw_g), ((0, 1), (0, 0)))

    h = matmul_nt("in_proj", x0b, win_rt, 256)
    qc, cqn = q_proj(h, g_cq, wuq_r, cos, sin)
    kc, kct, v, ckvn = kv_proj(h, g_ckv, wuk, wuv, cos, sin, kr_blk)
    o, ob, lse = attn_fwd(qc, kc, v)
    co, uc = conv_fwd(h, conv_w, conv_b, g_conv_ln, b_conv_ln)
    wout_g, wff1_g, wff2_g = hooks.late_weights(ob)
    wout = wout_g.reshape(d, d)
    wff2 = wff2_g.reshape(-1, d)
    r1, x1, x1b = out_proj_ln1(ob, co, wout, x0, g_ln1, b_ln1)
    rb, a1b = ff1_fwd(x1b, wff1_g)
    dr2, dr2b, loss8, dg2, db2 = ff2_ln2_loss(a1b, wff2, x1, target, g_ln2, b_ln2)

    df1b = ff2_bwd_act(dr2b, wff2, rb)
    gw_ff2 = wgrad("wgrad_ff2", a1b, dr2b, 1024, 1024).reshape(N_CHIP, -1, d)
    gw_ff1 = wgrad("wgrad_ff1", x1b, df1b, 1024, 1024, shards=N_CHIP)
    tok = hooks.ff_grads(gw_ff2, gw_ff1)
    dr1, dr1b, dg1, db1 = ff1_bwd_ln1(df1b, wff1_g, dr2, r1, g_ln1 + tok[0:1, 0:1])
    tok = hooks.ff_grads_mid(dr1b)
    gw_out = jnp.concatenate([wgrad("wgrad_out_attn", ob, dr1b, 1024, 1024)[0],
                              wgrad("wgrad_out_conv", co, dr1b, 1024, 1024)[0]], axis=0).reshape(N_CHIP, -1, d)
    dob, dot, dco, delta = out_proj_bwd(dr1b, wout.T, o)
    duc, dgc, dbc, dcb = conv_bwd_ln(uc, dco, g_conv_ln + tok[0:1, 0:1], b_conv_ln)
    dconv, gconvw = conv_bwd_taps(h, duc, conv_w)
    dqt, dk, dv = attn_bwd(qc, kc, kct, v, dob, dot, lse, delta)
    dqb, dcq, dgq = q_bwd(dqt, h, g_cq, wuq_r.T, cos, sin)
    dknb, dvb, dckv, dkr, dgkv = kv_bwd(dk, dv, h, g_ckv, wuk.T, wuv.T, cos, sin)
    gwuq_r = wgrad("wgrad_uq", cqn, dqb, 512, 1024)[0]
    gw_uk = wgrad("wgrad_uk", ckvn, dknb, 512, 1024, shards=N_CHIP)
    gw_uv = wgrad("wgrad_uv", ckvn, dvb, 512, 1024, shards=N_CHIP)
    dh = jnp.concatenate([dcq, dckv, dconv, dkr], axis=1)
    gwin_rt = wgrad("wgrad_in", dh, x0b, 640, 1024)[0]

    gwin_t = jnp.concatenate([gwin_rt[:o_kr], gwin_rt[o_kr + 2 * c:o_kr + 2 * c + D_ROPE], gwin_rt[o_kr:o_kr + 2 * c]], axis=0)
    gwin_t = jnp.pad(gwin_t.reshape(N_CHIP, per, d), ((0, 0), (0, win_g.shape[1] - per), (0, 0)))
    gwuq = jnp.concatenate([gwuq_r[:, :MLA_W].reshape(R_Q, HEADS, D_NOPE),
                            gwuq_r[:, MLA_W:].reshape(R_Q, HEADS, LANE)[:, :, :D_ROPE]], axis=2).reshape(R_Q, HEADS * D_QK)
    tok = hooks.rest_grads(dict(w_in=gwin_t, w_uq=_shard_cols(gwuq), w_uk=gw_uk, w_uv=gw_uv,
                                conv_w=_shard_cols(gconvw), w_out=gw_out))
    gx, dgin, dbin = in_proj_bwd_ln(dh, win_rt, dr1, x, ln_in_g + tok[0:1, 0:1])
    small = jnp.concatenate([dgin, dbin, dgq, dgkv, dcb, dgc, dbc, dg1, db1, dg2, db2, loss8], axis=1)
    return gx, small


BIG = ["w_in", "w_uq", "w_uk", "w_uv", "conv_w", "w_out", "w_ff1", "w_ff2"]
EARLY = ["w_in", "w_uq", "w_uk", "w_uv", "conv_w"]
LATE = ["w_out", "w_ff1", "w_ff2"]
SMALL = ["ln_in_g", "ln_in_b", "g_cq", "g_ckv", "conv_b", "g_conv_ln", "b_conv_ln", "g_ln1", "b_ln1", "g_ln2", "b_ln2"]
WEIGHTS = ["ln_in_g", "ln_in_b", "w_in", "g_cq", "w_uq", "g_ckv", "w_uk", "w_uv", "conv_w", "conv_b", "g_conv_ln",
           "b_conv_ln", "w_out", "g_ln1", "b_ln1", "w_ff1", "w_ff2", "g_ln2", "b_ln2"]


def _pad_rows(a, rows):
    return jnp.pad(a, ((0, rows - a.shape[0]), (0, 0)))


def kernel(x, positions, ln_in_g, ln_in_b, w_in, g_cq, w_uq, g_ckv, w_uk, w_uv, conv_w, conv_b, g_conv_ln, b_conv_ln, w_out, g_ln1, b_ln1, w_ff1, w_ff2, g_ln2, b_ln2, loss_target, m_ln_in_g, m_ln_in_b, m_w_in, m_g_cq, m_w_uq, m_g_ckv, m_w_uk, m_w_uv, m_conv_w, m_conv_b, m_g_conv_ln, m_b_conv_ln, m_w_out, m_g_ln1, m_b_ln1, m_w_ff1, m_w_ff2, m_g_ln2, m_b_ln2, v_ln_in_g, v_ln_in_b, v_w_in, v_g_cq, v_w_uq, v_g_ckv, v_w_uk, v_w_uv, v_conv_w, v_conv_b, v_g_conv_ln, v_b_conv_ln, v_w_out, v_g_ln1, v_b_ln1, v_w_ff1, v_w_ff2, v_g_ln2, v_b_ln2):
    w = dict(ln_in_g=ln_in_g, ln_in_b=ln_in_b, w_in=w_in, g_cq=g_cq, w_uq=w_uq, g_ckv=g_ckv, w_uk=w_uk, w_uv=w_uv,
             conv_w=conv_w, conv_b=conv_b, g_conv_ln=g_conv_ln, b_conv_ln=b_conv_ln, w_out=w_out, g_ln1=g_ln1,
             b_ln1=b_ln1, w_ff1=w_ff1, w_ff2=w_ff2, g_ln2=g_ln2, b_ln2=b_ln2)
    m = dict(ln_in_g=m_ln_in_g, ln_in_b=m_ln_in_b, w_in=m_w_in, g_cq=m_g_cq, w_uq=m_w_uq, g_ckv=m_g_ckv, w_uk=m_w_uk,
             w_uv=m_w_uv, conv_w=m_conv_w, conv_b=m_conv_b, g_conv_ln=m_g_conv_ln, b_conv_ln=m_b_conv_ln, w_out=m_w_out,
             g_ln1=m_g_ln1, b_ln1=m_b_ln1, w_ff1=m_w_ff1, w_ff2=m_w_ff2, g_ln2=m_g_ln2, b_ln2=m_b_ln2)
    v = dict(ln_in_g=v_ln_in_g, ln_in_b=v_ln_in_b, w_in=v_w_in, g_cq=v_g_cq, w_uq=v_w_uq, g_ckv=v_g_ckv, w_uk=v_w_uk,
             w_uv=v_w_uv, conv_w=v_conv_w, conv_b=v_conv_b, g_conv_ln=v_g_conv_ln, b_conv_ln=v_b_conv_ln, w_out=v_w_out,
             g_ln1=v_g_ln1, b_ln1=v_b_ln1, w_ff1=v_w_ff1, w_ff2=v_w_ff2, g_ln2=v_g_ln2, b_ln2=v_b_ln2)

    as2d = lambda t, n: t[n][0].T if n == "w_in" else t[n][0]
    sh2 = {n: as2d(w, n) for n in BIG}
    cidx = lax.axis_index("c").astype(jnp.int32).reshape(1)
    me = 2 * lax.axis_index("x") + lax.axis_index("y")
    kc = jnp.stack([me, lax.axis_index("c")]).astype(jnp.int32)

    pad_to = {"conv_w": CONV_K + 1, "w_in": -(-sh2["w_in"].shape[0] // (4 * SUB)) * (4 * SUB)}
    early = [_pad_rows(sh2[n] if n == "conv_w" else sh2[n].astype(BF), pad_to.get(n, sh2[n].shape[0])) for n in EARLY]
    eg = split_send_start("early_weights_start", "gather_half", early, [(N_CHIP,) + a.shape for a in early], ln_in_g)
    late = [sh2[n].astype(BF) for n in LATE]
    ag = split_send_start("late_weights_start", "gather", late, [(N_CHIP,) + a.shape for a in late], eg[4])
    rest = [n for n in BIG if n not in ("w_ff2", "w_ff1")]
    flight = {}

    class Hooks:
        @staticmethod
        def early_weights(after):
            mine, lands = split_send_wait("early_weights_wait", "gather_half", *eg[:4], after)
            full = [lax.dynamic_update_slice(g, a[None], (me, 0, 0)) for g, a in zip(swap_gathered_halves(lands), mine)]
            return [g[:, :CONV_K] if n == "conv_w" else g for n, g in zip(EARLY, full)]

        @staticmethod
        def late_weights(after):
            mine, lands = split_send_wait("late_weights_wait", "gather", *ag[:4], after)
            return [lax.dynamic_update_slice(g, a[None], (me, 0, 0)) for g, a in zip(lands, mine)]

        @staticmethod
        def ff_grads(gw_ff2, gw_ff1):
            full = [gw_ff2, gw_ff1]
            st = split_send_start("ff_pair_start", "pair", full, [(N_CHIP, g.shape[1] // 2, g.shape[2]) for g in full], ag[4])
            flight["ff_pair"] = st[:4]
            flight["token"] = st[4]
            return st[4]

        @staticmethod
        def ff_grads_mid(after):
            full, recv = split_send_wait("ff_pair_wait", "pair", *flight["ff_pair"], after)
            psum = [pair_add(g, r, cidx) for g, r in zip(full, recv)]
            st = split_send_start("ff_grads_start", "scatter", psum, [(N_CHIP - 1,) + p.shape[1:] for p in psum], flight["token"])
            flight["ff"] = st[:4]
            flight["token"] = st[4]
            return st[4]

        @staticmethod
        def rest_grads(big):
            full = [big[n] for n in rest]
            psum = [pair_add(g, r, cidx) for g, r in zip(full, pair_exchange(full, "rest"))]
            st = split_send_start("rest_grads_start", "scatter", psum, [(N_CHIP - 1,) + p.shape[1:] for p in psum], flight["token"])
            flight["rest"] = st[:4]
            return st[4]

    gx, small = local_step(x[0], positions[0], ln_in_g, ln_in_b, g_cq, g_ckv, conv_b, g_conv_ln, b_conv_ln, g_ln1, b_ln1,
                           g_ln2, b_ln2, loss_target[0], ag[4], Hooks)

    ff_psum, ff_got = split_send_wait("ff_grads_wait", "scatter", *flight["ff"], gx)
    rest_psum, rest_got = split_send_wait("rest_grads_wait", "scatter", *flight["rest"], gx)
    summed = [chip_add(p, r, kc) for p, r in zip(rest_psum + ff_psum, rest_got + ff_got)]
    gsh = dict(zip(rest + ["w_ff2", "w_ff1"], pair_share(summed, "all")))
    for n in pad_to:
        gsh[n] = gsh[n][:sh2[n].shape[0]]

    grad, delta, new_m, new_v = {}, {}, {}, {}
    for n in BIG:
        back = (lambda a: a.T[None]) if n == "w_in" else (lambda a: a[None])
        d_, m_, v_ = adamw("adamw_" + n, sh2[n], gsh[n], as2d(m, n), as2d(v, n))
        grad[n], delta[n], new_m[n], new_v[n] = back(gsh[n]), back(d_), back(m_), back(v_)

    flat = lambda t: jnp.concatenate([t[n].reshape(1, -1) for n in SMALL] + [jnp.zeros((1, LANE), F32)], axis=1)
    g_s, d_s, m_s, v_s = small_allreduce_adamw(small, flat(w), flat(m), flat(v))
    off = 0
    for n in SMALL:
        sz = w[n].size
        for dst, src in ((grad, g_s), (delta, d_s), (new_m, m_s), (new_v, v_s)):
            dst[n] = src[0, off:off + sz].reshape(w[n].shape)
        off += sz
    loss = jnp.sum(g_s[0, off:off + LANE])

    return (loss, gx[None], *[grad[n] for n in WEIGHTS], *[delta[n] for n in WEIGHTS],
            *[new_m[n] for n in WEIGHTS], *[new_v[n] for n in WEIGHTS])
```

```python
import jax
import jax.numpy as jnp
from jax import lax
from jax.experimental import pallas as pl
from jax.experimental.pallas import tpu as pltpu

F32 = jnp.float32
BF = jnp.bfloat16

HEADS = 8
D_NOPE = 128
D_ROPE = 64
D_V = 128
D_QK = D_NOPE + D_ROPE
R_Q = 512
R_KV = 512
MLA_W = HEADS * D_V
CONV_K = 31
CONV_PAD = CONV_K // 2
ROPE_BASE = 10000.0
LOG2E = 1.4426950408889634
LN2 = 0.6931471805599453
LN_EPS = 1e-5
RMS_EPS = 1e-6
ALPHA = (2.0 * 1) ** 0.25
ADAM_LR = 0.001
ADAM_B1 = 0.9
ADAM_B2 = 0.999
ADAM_EPS = 1e-08
ADAM_WD = 0.01
ADAM_STEP = 10

LANE = 128
SUB = 8
HALO = 16
N_CHIP = 4
MESH = pl.DeviceIdType.MESH
VMEM_MB = 1024 * 1024


def _call(body, **kw):
    return pl.pallas_call(body, **kw)


def _cp(sem, mb=48):
    return pltpu.CompilerParams(dimension_semantics=sem, vmem_limit_bytes=mb * VMEM_MB)


def _sds(shape, dt):
    return jax.ShapeDtypeStruct(shape, dt)


def _dot(a, b):
    return jnp.dot(a, b, preferred_element_type=F32)


def _dot_nt(a, b):
    return lax.dot_general(a, b, (((1,), (1,)), ((), ())), preferred_element_type=F32)


def _dot_tn(a, b):
    return lax.dot_general(a, b, (((0,), (0,)), ((), ())), preferred_element_type=F32)


def _rows8(v):
    t, n = v.shape
    return v.reshape(t // SUB, SUB, n).sum(axis=0)


def _ln_stats(r):
    mu = jnp.mean(r, axis=-1, keepdims=True)
    xc = r - mu
    var = jnp.mean(xc * xc, axis=-1, keepdims=True)
    rstd = lax.rsqrt(var + LN_EPS)
    return xc * rstd, rstd


def _ln_bwd(dy, xhat, rstd, g):
    dyh = dy * g
    m1 = jnp.mean(dyh, axis=-1, keepdims=True)
    m2 = jnp.mean(dyh * xhat, axis=-1, keepdims=True)
    return rstd * (dyh - m1 - xhat * m2)


def _rms_fwd(x, g):
    rr = lax.rsqrt(jnp.mean(x * x, axis=-1, keepdims=True) + RMS_EPS)
    xh = x * rr
    return xh * g, xh, rr


def _rms_bwd(dy, xh, rr, g):
    dyg = dy * g
    return rr * (dyg - xh * jnp.mean(dyg * xh, axis=-1, keepdims=True))


def _rope128(x, cos, sin_signed):
    lane = lax.broadcasted_iota(jnp.int32, x.shape, 1)
    rot = jnp.where(lane < D_ROPE // 2, pltpu.roll(x, LANE - D_ROPE // 2, 1), pltpu.roll(x, D_ROPE // 2, 1))
    return x * cos + rot * sin_signed


def _unrope128(dy, cos, sin_signed):
    t = dy * sin_signed
    lane = lax.broadcasted_iota(jnp.int32, dy.shape, 1)
    rot = jnp.where(lane < D_ROPE // 2, pltpu.roll(t, LANE - D_ROPE // 2, 1), pltpu.roll(t, D_ROPE // 2, 1))
    return dy * cos + rot


def _as_row(col):
    return jnp.transpose(jnp.broadcast_to(col, (col.shape[0], LANE)))[0:1, :]


def _sigmoid(x):
    return 1.0 / (1.0 + jnp.exp(-x))


def _row_chunks(tm, fn, rc=128):
    rc = min(rc, tm)

    def step(ci, carry):
        fn(pl.ds(pl.multiple_of(ci * rc, rc), rc))
        return carry

    lax.fori_loop(0, tm // rc, step, 0)


def _unrolled_loop(n, unroll, fn, init):
    unroll = min(n, unroll)
    assert n % unroll == 0

    def body(t, carry):
        for u in range(unroll):
            carry = fn(t * unroll + u, carry)
        return carry

    return lax.fori_loop(0, n // unroll, body, init)


def _tile(s, want):
    t = min(s, want)
    assert s % t == 0
    return t


def rope_tables(pos_f, invf):
    s = pos_f.shape[0]
    tm = _tile(s, 1024)

    def body(p_ref, f_ref, c_ref, s_ref):
        ang = p_ref[...] * f_ref[...]
        lane = lax.broadcasted_iota(jnp.int32, ang.shape, 1)
        c = jnp.cos(ang)
        sn = jnp.sin(ang)
        c_ref[...] = jnp.where(lane < D_ROPE, c, 0.0)
        s_ref[...] = jnp.where(lane < D_ROPE // 2, -sn, jnp.where(lane < D_ROPE, sn, 0.0))

    return _call(
        body, name="rope_tables", grid=(s // tm,),
        in_specs=[pl.BlockSpec((tm, 1), lambda i: (i, 0)), pl.BlockSpec((1, LANE), lambda i: (0, 0))],
        out_specs=[pl.BlockSpec((tm, LANE), lambda i: (i, 0))] * 2,
        out_shape=[_sds((s, LANE), F32)] * 2,
        compiler_params=_cp(("arbitrary",)),
    )(pos_f, invf)


def ln_in_fwd(x, g, b):
    s, d = x.shape
    tm = _tile(s, 512)

    def body(x_ref, g_ref, b_ref, o_ref, ob_ref):
        xhat, _ = _ln_stats(x_ref[...])
        y = xhat * g_ref[...] + b_ref[...]
        o_ref[...] = y
        ob_ref[...] = y.astype(BF)

    row = pl.BlockSpec((1, d), lambda i: (0, 0))
    tok = pl.BlockSpec((tm, d), lambda i: (i, 0))
    return _call(
        body, name="ln_in_fwd", grid=(s // tm,), in_specs=[tok, row, row], out_specs=[tok, tok],
        out_shape=[_sds((s, d), F32), _sds((s, d), BF)], compiler_params=_cp(("arbitrary",)),
    )(x, g, b)


def matmul_nt(name, a, wt, tm, out_dtype=F32):
    s, k = a.shape
    n = wt.shape[0]
    tm = _tile(s, tm)

    def body(a_ref, w_ref, o_ref):
        o_ref[...] = _dot_nt(a_ref[...], w_ref[...]).astype(o_ref.dtype)

    return _call(
        body, name=name, grid=(s // tm,),
        in_specs=[pl.BlockSpec((tm, k), lambda i: (i, 0)), pl.BlockSpec((n, k), lambda i: (0, 0))],
        out_specs=pl.BlockSpec((tm, n), lambda i: (i, 0)),
        out_shape=_sds((s, n), out_dtype), compiler_params=_cp(("arbitrary",)),
    )(a, wt)


def q_proj(h, g_cq, wuq, cos, sin):
    s = h.shape[0]
    tm = _tile(s, 512)

    def body(h_ref, g_ref, w_ref, c_ref, s_ref, q_ref, n_ref):
        y, _, _ = _rms_fwd(h_ref[...], g_ref[...])
        yb = y.astype(BF)
        n_ref[...] = yb
        q = _dot(yb, w_ref[...])
        c = c_ref[...]
        sn = s_ref[...]
        for hd in range(HEADS):
            q_ref[hd, :, 0:LANE] = q[:, LANE * hd:LANE * (hd + 1)].astype(BF)
            qr = q[:, MLA_W + LANE * hd:MLA_W + LANE * (hd + 1)]
            q_ref[hd, :, LANE:2 * LANE] = _rope128(qr, c, sn).astype(BF)

    return _call(
        body, name="q_proj", grid=(s // tm,),
        in_specs=[pl.BlockSpec((tm, R_Q), lambda i: (i, 0)), pl.BlockSpec((1, R_Q), lambda i: (0, 0)),
                  pl.BlockSpec((R_Q, 2 * MLA_W), lambda i: (0, 0)),
                  pl.BlockSpec((tm, LANE), lambda i: (i, 0)), pl.BlockSpec((tm, LANE), lambda i: (i, 0))],
        out_specs=[pl.BlockSpec((HEADS, tm, 2 * LANE), lambda i: (0, i, 0)), pl.BlockSpec((tm, R_Q), lambda i: (i, 0))],
        out_shape=[_sds((HEADS, s, 2 * LANE), BF), _sds((s, R_Q), BF)], compiler_params=_cp(("arbitrary",)),
    )(h, g_cq, wuq, cos, sin)


def kv_proj(h, g_ckv, wuk, wuv, cos, sin, kr_blk):
    s = h.shape[0]
    tm = _tile(s, 512)

    def body(h_ref, kr_ref, g_ref, wk_ref, wv_ref, c_ref, s_ref, k_ref, kt_ref, v_ref, n_ref):
        y, _, _ = _rms_fwd(h_ref[...], g_ref[...])
        yb = y.astype(BF)
        n_ref[...] = yb
        kn = _dot(yb, wk_ref[...])
        v = _dot(yb, wv_ref[...])
        kr = _rope128(kr_ref[...], c_ref[...], s_ref[...])
        krb = kr.astype(BF)
        krt = kr.T.astype(BF)
        for hd in range(HEADS):
            knh = kn[:, LANE * hd:LANE * (hd + 1)]
            k_ref[hd, :, 0:LANE] = knh.astype(BF)
            k_ref[hd, :, LANE:2 * LANE] = krb
            kt_ref[hd, 0:LANE, :] = knh.T.astype(BF)
            kt_ref[hd, LANE:2 * LANE, :] = krt
            v_ref[hd] = v[:, LANE * hd:LANE * (hd + 1)].astype(BF)

    tab = pl.BlockSpec((tm, LANE), lambda i: (i, 0))
    wsp = pl.BlockSpec((R_KV, MLA_W), lambda i: (0, 0))
    return _call(
        body, name="kv_proj", grid=(s // tm,),
        in_specs=[pl.BlockSpec((tm, R_KV), lambda i: (i, 1)), pl.BlockSpec((tm, LANE), lambda i: (i, kr_blk)),
                  pl.BlockSpec((1, R_KV), lambda i: (0, 0)), wsp, wsp, tab, tab],
        out_specs=[pl.BlockSpec((HEADS, tm, 2 * LANE), lambda i: (0, i, 0)), pl.BlockSpec((HEADS, 2 * LANE, tm), lambda i: (0, 0, i)),
                   pl.BlockSpec((HEADS, tm, LANE), lambda i: (0, i, 0)), pl.BlockSpec((tm, R_KV), lambda i: (i, 0))],
        out_shape=[_sds((HEADS, s, 2 * LANE), BF), _sds((HEADS, 2 * LANE, s), BF), _sds((HEADS, s, LANE), BF), _sds((s, R_KV), BF)],
        compiler_params=_cp(("arbitrary",)),
    )(h, h, g_ckv, wuk, wuv, cos, sin)


def attn_fwd(qc, kc, v):
    _, s, _ = qc.shape
    tq = _tile(s, 256)
    tk = _tile(s, 512)
    scale = D_QK ** -0.5
    c2 = scale * LOG2E
    nk = s // tk
    nb = tk // LANE
    un = 8

    def body(q_ref, k_ref, v_ref, o_ref, ob_ref, l_ref, s_scr, m_scr):
        q = q_ref[...]

        def scores(j, mpart):
            off = pl.multiple_of(j * tk, tk)
            sc = _dot_nt(q, k_ref[pl.ds(off, tk), :]) * c2
            s_scr[:, pl.ds(off, tk)] = sc
            for b in range(nb):
                mpart = jnp.maximum(mpart, sc[:, LANE * b:LANE * (b + 1)])
            return mpart

        mpart = _unrolled_loop(nk, un, scores, jnp.full((tq, LANE), -jnp.inf, F32))
        m = jnp.max(mpart, axis=-1, keepdims=True)
        m_scr[...] = jnp.broadcast_to(m, (tq, LANE))

        def weigh(j, carry):
            lpart, acc = carry
            off = pl.multiple_of(j * tk, tk)
            ps = []
            for b in range(nb):
                p = jnp.exp2(s_scr[:, pl.ds(off + LANE * b, LANE)] - m_scr[...])
                lpart = lpart + p
                ps.append(p.astype(BF))
            acc = acc + _dot(jnp.concatenate(ps, axis=1), v_ref[pl.ds(off, tk), :])
            return lpart, acc

        lpart, acc = _unrolled_loop(nk, un, weigh, (jnp.zeros((tq, LANE), F32), jnp.zeros((tq, D_V), F32)))
        l = jnp.sum(lpart, axis=-1, keepdims=True)
        o = acc / l
        o_ref[...] = o
        ob_ref[...] = o.astype(BF)
        l_ref[...] = _as_row(m + jnp.log(l) * LOG2E)

    return _call(
        body, name="attn_fwd", grid=(HEADS, s // tq),
        in_specs=[pl.BlockSpec((None, tq, 2 * LANE), lambda h, i: (h, i, 0)),
                  pl.BlockSpec((None, s, 2 * LANE), lambda h, i: (h, 0, 0)),
                  pl.BlockSpec((None, s, LANE), lambda h, i: (h, 0, 0))],
        out_specs=[pl.BlockSpec((tq, LANE), lambda h, i: (i, h)), pl.BlockSpec((tq, LANE), lambda h, i: (i, h)),
                   pl.BlockSpec((None, 1, tq), lambda h, i: (h, 0, i))],
        out_shape=[_sds((s, MLA_W), F32), _sds((s, MLA_W), BF), _sds((HEADS, 1, s), F32)],
        scratch_shapes=[pltpu.VMEM((tq, s + LANE), F32), pltpu.VMEM((tq, LANE), F32)],
        compiler_params=_cp(("arbitrary", "arbitrary")),
    )(qc, kc, v)


def _halo_specs(tm, s, width, col):
    r = tm // HALO
    nb = s // HALO
    cur = pl.BlockSpec((tm, width), lambda i: (i, col))
    prev = pl.BlockSpec((HALO, width), lambda i: (jnp.maximum(i * r - 1, 0), col))
    nxt = pl.BlockSpec((HALO, width), lambda i: (jnp.minimum((i + 1) * r, nb - 1), col))
    return cur, prev, nxt


def _slab_shapes(tm, c):
    return (tm + 2 * HALO, c + LANE), (SUB - 1, tm + 2 * HALO - SUB, c + LANE)


def _fill_slab(slab, tm, prev, cur, nxt):
    i = pl.program_id(0)
    last = pl.num_programs(0) - 1
    c = cur.shape[1]
    slab[0:HALO, 0:c] = jnp.where(i > 0, prev, 0.0)
    slab[HALO:HALO + tm, 0:c] = cur
    slab[HALO + tm:2 * HALO + tm, 0:c] = jnp.where(i < last, nxt, 0.0)


def _rotate_slab(slab, rot, tm):
    rows = tm + 2 * HALO - SUB
    c = slab.shape[1] - LANE
    for b in range(1, SUB):
        rot[b - 1, :, 0:c] = slab[pl.ds(b, rows), 0:c]


def _shifted(slab, rot, start, rc, cs):
    b = start % SUB
    if b == 0:
        return slab[pl.ds(start, rc), cs]
    return rot[b - 1, pl.ds(start - b, rc), cs]


def conv_fwd(h, conv_w, conv_b, g_ln, b_ln):
    s = h.shape[0]
    c = conv_w.shape[1]
    tm = _tile(s, 256)
    rc = _tile(tm, 64)

    def body(a_ref, ap_ref, an_ref, g_ref, gp_ref, gn_ref, w_ref, cb_ref, lg_ref, lb_ref, co_ref, uc_ref, slab, rot):
        _fill_slab(slab, tm, ap_ref[...] * _sigmoid(gp_ref[...]), a_ref[...] * _sigmoid(g_ref[...]),
                   an_ref[...] * _sigmoid(gn_ref[...]))
        _rotate_slab(slab, rot, tm)

        def lane_block(cb, carry):
            cs = pl.ds(pl.multiple_of(cb * LANE, LANE), LANE)
            for r0 in range(0, tm, rc):
                acc = jnp.zeros((rc, LANE), F32)
                for k in range(CONV_K):
                    acc = acc + w_ref[k:k + 1, cs] * _shifted(slab, rot, r0 + HALO - CONV_PAD + k, rc, cs)
                uc_ref[r0:r0 + rc, cs] = acc + cb_ref[:, cs]
            return carry

        lax.fori_loop(0, c // LANE, lane_block, 0)
        xhat, _ = _ln_stats(uc_ref[...])
        cl = xhat * lg_ref[...] + lb_ref[...]
        co_ref[...] = (cl * _sigmoid(cl)).astype(BF)

    a_specs = _halo_specs(tm, s, c, 1)
    g_specs = _halo_specs(tm, s, c, 2)
    row = pl.BlockSpec((1, c), lambda i: (0, 0))
    tok = pl.BlockSpec((tm, c), lambda i: (i, 0))
    return _call(
        body, name="conv_fwd", grid=(s // tm,),
        in_specs=[*a_specs, *g_specs, pl.BlockSpec(conv_w.shape, lambda i: (0, 0)), row, row, row],
        out_specs=[tok, tok], out_shape=[_sds((s, c), BF), _sds((s, c), F32)],
        scratch_shapes=[pltpu.VMEM(shp, F32) for shp in _slab_shapes(tm, c)],
        compiler_params=_cp(("arbitrary",)),
    )(h, h, h, h, h, h, conv_w, conv_b, g_ln, b_ln)


def out_proj_ln1(ob, co, wout, x0, g1, b1):
    s, d = x0.shape
    kh = ob.shape[1]
    tm = _tile(s, 256)

    def body(o_ref, c_ref, w_ref, x_ref, g_ref, b_ref, r_ref, x1_ref, x1b_ref, acc):
        acc[...] = _dot(o_ref[...], w_ref[0:kh, :]) + _dot(c_ref[...], w_ref[kh:2 * kh, :])
        g = g_ref[...]
        b = b_ref[...]

        def chunk(rows):
            r = ALPHA * x_ref[rows, :] + acc[rows, :]
            r_ref[rows, :] = r
            xhat, _ = _ln_stats(r)
            y = xhat * g + b
            x1_ref[rows, :] = y
            x1b_ref[rows, :] = y.astype(BF)

        _row_chunks(tm, chunk)

    half = pl.BlockSpec((tm, kh), lambda i: (i, 0))
    tok = pl.BlockSpec((tm, d), lambda i: (i, 0))
    row = pl.BlockSpec((1, d), lambda i: (0, 0))
    return _call(
        body, name="out_proj_ln1", grid=(s // tm,),
        in_specs=[half, half, pl.BlockSpec((2 * kh, d), lambda i: (0, 0)), tok, row, row],
        out_specs=[tok, tok, tok], out_shape=[_sds((s, d), F32), _sds((s, d), F32), _sds((s, d), BF)],
        scratch_shapes=[pltpu.VMEM((tm, d), F32)], compiler_params=_cp(("arbitrary",)),
    )(ob, co, wout, x0, g1, b1)


def ff1_fwd(x1b, wff1_g):
    s, d = x1b.shape
    nsh, _, fs = wff1_g.shape
    tm = _tile(s, 1024)
    tn = _tile(fs, 1024)
    per = fs // tn

    def body(a_ref, w_ref, r_ref, a1_ref):
        r = jnp.maximum(_dot(a_ref[...], w_ref[...]), 0.0)
        r_ref[...] = r.astype(BF)
        a1_ref[...] = (r * r).astype(BF)

    out = pl.BlockSpec((tm, tn), lambda i, j: (i, j))
    return _call(
        body, name="ff1_fwd", grid=(s // tm, nsh * per),
        in_specs=[pl.BlockSpec((tm, d), lambda i, j: (i, 0)),
                  pl.BlockSpec((None, d, tn), lambda i, j: (j // per, 0, j % per))],
        out_specs=[out, out], out_shape=[_sds((s, nsh * fs), BF)] * 2,
        compiler_params=_cp(("arbitrary", "arbitrary")),
    )(x1b, wff1_g)


def ff2_ln2_loss(a1b, wff2, x1, target, g2, b2):
    s, f = a1b.shape
    d = x1.shape[1]
    tm = _tile(s, 512)
    tk = _tile(f, 2048)
    nk = f // tk

    def body(a_ref, w_ref, x_ref, t_ref, g_ref, b_ref, dr_ref, drb_ref, loss_ref, dg_ref, db_ref, acc):
        i = pl.program_id(0)
        k = pl.program_id(1)

        @pl.when(k == 0)
        def _():
            acc[...] = _dot(a_ref[...], w_ref[...])

        @pl.when(k > 0)
        def _():
            acc[...] += _dot(a_ref[...], w_ref[...])

        @pl.when(jnp.logical_and(i == 0, k == 0))
        def _():
            loss_ref[...] = jnp.zeros_like(loss_ref)
            dg_ref[...] = jnp.zeros_like(dg_ref)
            db_ref[...] = jnp.zeros_like(db_ref)

        @pl.when(k == nk - 1)
        def _():
            g = g_ref[...]

            def chunk(rows):
                r = ALPHA * x_ref[rows, :] + acc[rows, :]
                xhat, rstd = _ln_stats(r)
                e = xhat * g + b_ref[...] - t_ref[rows, :]
                e2 = _rows8(e * e)
                part = e2[:, 0:LANE]
                for c in range(1, d // LANE):
                    part = part + e2[:, LANE * c:LANE * (c + 1)]
                loss_ref[...] += part * (0.5 / d)
                dy = e * (1.0 / d)
                dg_ref[...] += _rows8(dy * xhat)
                db_ref[...] += _rows8(dy)
                dr = _ln_bwd(dy, xhat, rstd, g)
                dr_ref[rows, :] = dr
                drb_ref[rows, :] = dr.astype(BF)

            _row_chunks(tm, chunk)

    tok = pl.BlockSpec((tm, d), lambda i, k: (i, 0))
    row = pl.BlockSpec((1, d), lambda i, k: (0, 0))
    accs = pl.BlockSpec((SUB, d), lambda i, k: (0, 0))
    return _call(
        body, name="ff2_ln2_loss", grid=(s // tm, nk),
        in_specs=[pl.BlockSpec((tm, tk), lambda i, k: (i, k)), pl.BlockSpec((tk, d), lambda i, k: (k, 0)),
                  tok, tok, row, row],
        out_specs=[tok, tok, pl.BlockSpec((SUB, LANE), lambda i, k: (0, 0)), accs, accs],
        out_shape=[_sds((s, d), F32), _sds((s, d), BF), _sds((SUB, LANE), F32), _sds((SUB, d), F32), _sds((SUB, d), F32)],
        scratch_shapes=[pltpu.VMEM((tm, d), F32)], compiler_params=_cp(("arbitrary", "arbitrary"), 60),
    )(a1b, wff2, x1, target, g2, b2)


def ff2_bwd_act(dr2b, wff2, rb):
    s, d = dr2b.shape
    f = wff2.shape[0]
    tm = _tile(s, 1024)
    tn = _tile(f, 1024)

    def body(a_ref, w_ref, r_ref, o_ref):
        o_ref[...] = (_dot_nt(a_ref[...], w_ref[...]) * (2.0 * r_ref[...].astype(F32))).astype(BF)

    return _call(
        body, name="ff2_bwd_act", grid=(s // tm, f // tn),
        in_specs=[pl.BlockSpec((tm, d), lambda i, j: (i, 0)), pl.BlockSpec((tn, d), lambda i, j: (j, 0)),
                  pl.BlockSpec((tm, tn), lambda i, j: (i, j))],
        out_specs=pl.BlockSpec((tm, tn), lambda i, j: (i, j)), out_shape=_sds((s, f), BF),
        compiler_params=_cp(("arbitrary", "arbitrary")),
    )(dr2b, wff2, rb)


def wgrad(name, a, b, tm, tn, tk=2048, shards=1):
    s, m = a.shape
    n = b.shape[1]
    tm = _tile(m, tm)
    ns = n // shards
    tn = _tile(ns, tn)
    tk = _tile(s, tk)
    per = ns // tn

    def body(a_ref, b_ref, o_ref):
        k = pl.program_id(2)

        @pl.when(k == 0)
        def _():
            o_ref[...] = _dot_tn(a_ref[...], b_ref[...])

        @pl.when(k > 0)
        def _():
            o_ref[...] += _dot_tn(a_ref[...], b_ref[...])

    return _call(
        body, name=name, grid=(m // tm, n // tn, s // tk),
        in_specs=[pl.BlockSpec((tk, tm), lambda i, j, k: (k, i)), pl.BlockSpec((tk, tn), lambda i, j, k: (k, j))],
        out_specs=pl.BlockSpec((None, tm, tn), lambda i, j, k: (j // per, i, j % per)),
        out_shape=_sds((shards, m, ns), F32), compiler_params=_cp(("arbitrary", "arbitrary", "arbitrary")),
    )(a, b)


def ff1_bwd_ln1(df1b, wff1_g, dr2, r1, g1):
    s, f = df1b.shape
    d = dr2.shape[1]
    tm = _tile(s, 512)
    tk = _tile(wff1_g.shape[2], 2048)
    per = wff1_g.shape[2] // tk
    nk = f // tk

    def body(a_ref, w_ref, d2_ref, r_ref, g_ref, dr_ref, drb_ref, dg_ref, db_ref, acc):
        i = pl.program_id(0)
        k = pl.program_id(1)

        @pl.when(k == 0)
        def _():
            acc[...] = _dot_nt(a_ref[...], w_ref[...])

        @pl.when(k > 0)
        def _():
            acc[...] += _dot_nt(a_ref[...], w_ref[...])

        @pl.when(jnp.logical_and(i == 0, k == 0))
        def _():
            dg_ref[...] = jnp.zeros_like(dg_ref)
            db_ref[...] = jnp.zeros_like(db_ref)

        @pl.when(k == nk - 1)
        def _():
            g = g_ref[...]

            def chunk(rows):
                dy = ALPHA * d2_ref[rows, :] + acc[rows, :]
                xhat, rstd = _ln_stats(r_ref[rows, :])
                dg_ref[...] += _rows8(dy * xhat)
                db_ref[...] += _rows8(dy)
                dr = _ln_bwd(dy, xhat, rstd, g)
                dr_ref[rows, :] = dr
                drb_ref[rows, :] = dr.astype(BF)

            _row_chunks(tm, chunk)

    tok = pl.BlockSpec((tm, d), lambda i, k: (i, 0))
    accs = pl.BlockSpec((SUB, d), lambda i, k: (0, 0))
    return _call(
        body, name="ff1_bwd_ln1", grid=(s // tm, nk),
        in_specs=[pl.BlockSpec((tm, tk), lambda i, k: (i, k)), pl.BlockSpec((None, d, tk), lambda i, k: (k // per, 0, k % per)),
                  tok, tok, pl.BlockSpec((1, d), lambda i, k: (0, 0))],
        out_specs=[tok, tok, accs, accs],
        out_shape=[_sds((s, d), F32), _sds((s, d), BF), _sds((SUB, d), F32), _sds((SUB, d), F32)],
        scratch_shapes=[pltpu.VMEM((tm, d), F32)], compiler_params=_cp(("arbitrary", "arbitrary"), 60),
    )(df1b, wff1_g, dr2, r1, g1)


def out_proj_bwd(dr1b, woutt, o):
    s, d = dr1b.shape
    tm = _tile(s, 256)

    def body(a_ref, w_ref, o_ref, do_ref, dot_ref, dc_ref, dl_ref):
        dcat = _dot(a_ref[...], w_ref[...])
        do = dcat[:, 0:MLA_W]
        do_ref[...] = do.astype(BF)
        dc_ref[...] = dcat[:, MLA_W:]
        prod = do * o_ref[...]
        for hd in range(HEADS):
            hs = slice(LANE * hd, LANE * (hd + 1))
            dl_ref[hd] = _as_row(jnp.sum(prod[:, hs], axis=-1, keepdims=True))
            dot_ref[hd] = do[:, hs].T.astype(BF)

    half = pl.BlockSpec((tm, MLA_W), lambda i: (i, 0))
    return _call(
        body, name="out_proj_bwd", grid=(s // tm,),
        in_specs=[pl.BlockSpec((tm, d), lambda i: (i, 0)), pl.BlockSpec((d, d), lambda i: (0, 0)), half],
        out_specs=[half, pl.BlockSpec((HEADS, LANE, tm), lambda i: (0, 0, i)),
                   pl.BlockSpec((tm, d - MLA_W), lambda i: (i, 0)), pl.BlockSpec((HEADS, 1, tm), lambda i: (0, 0, i))],
        out_shape=[_sds((s, MLA_W), BF), _sds((HEADS, LANE, s), BF), _sds((s, d - MLA_W), F32), _sds((HEADS, 1, s), F32)],
        compiler_params=_cp(("arbitrary",)),
    )(dr1b, woutt, o)


def conv_bwd_ln(uc, dco, g_ln, b_ln):
    s, c = uc.shape
    tm = _tile(s, 512)

    def body(u_ref, d_ref, g_ref, b_ref, du_ref, dg_ref, db_ref, dcb_ref):
        @pl.when(pl.program_id(0) == 0)
        def _():
            dg_ref[...] = jnp.zeros_like(dg_ref)
            db_ref[...] = jnp.zeros_like(db_ref)
            dcb_ref[...] = jnp.zeros_like(dcb_ref)

        xhat, rstd = _ln_stats(u_ref[...])
        g = g_ref[...]
        cl = xhat * g + b_ref[...]
        sg = _sigmoid(cl)
        dcl = d_ref[...] * (sg * (1.0 + cl * (1.0 - sg)))
        dg_ref[...] += _rows8(dcl * xhat)
        db_ref[...] += _rows8(dcl)
        du = _ln_bwd(dcl, xhat, rstd, g)
        du_ref[...] = du
        dcb_ref[...] += _rows8(du)

    tok = pl.BlockSpec((tm, c), lambda i: (i, 0))
    row = pl.BlockSpec((1, c), lambda i: (0, 0))
    accs = pl.BlockSpec((SUB, c), lambda i: (0, 0))
    return _call(
        body, name="conv_bwd_ln", grid=(s // tm,), in_specs=[tok, tok, row, row], out_specs=[tok, accs, accs, accs],
        out_shape=[_sds((s, c), F32)] + [_sds((SUB, c), F32)] * 3, compiler_params=_cp(("arbitrary",)),
    )(uc, dco, g_ln, b_ln)


def conv_bwd_taps(h, duc, conv_w):
    s, c = duc.shape
    tm = _tile(s, 256)
    rc = _tile(tm, 64)

    def body(a_ref, ap_ref, an_ref, g_ref, gp_ref, gn_ref, d_ref, dp_ref, dn_ref, w_ref, o_ref, dw_ref,
             uslab, dslab, du_s, urot, drot, dw8):
        @pl.when(pl.program_id(0) == 0)
        def _():
            dw8[...] = jnp.zeros_like(dw8)

        sg = _sigmoid(g_ref[...])
        a = a_ref[...]
        _fill_slab(uslab, tm, ap_ref[...] * _sigmoid(gp_ref[...]), a * sg, an_ref[...] * _sigmoid(gn_ref[...]))
        _fill_slab(dslab, tm, dp_ref[...], d_ref[...], dn_ref[...])
        _rotate_slab(uslab, urot, tm)
        _rotate_slab(dslab, drot, tm)

        def lane_block(cb, carry):
            cs = pl.ds(pl.multiple_of(cb * LANE, LANE), LANE)
            for r0 in range(0, tm, rc):
                acc = jnp.zeros((rc, LANE), F32)
                for k in range(CONV_K):
                    acc = acc + w_ref[k:k + 1, cs] * _shifted(dslab, drot, r0 + HALO + CONV_PAD - k, rc, cs)
                du_s[r0:r0 + rc, cs] = acc
            return carry

        def lane_block_taps(cb, carry):
            cs = pl.ds(pl.multiple_of(cb * LANE, LANE), LANE)
            parts = []
            for k in range(CONV_K):
                prod = None
                for r0 in range(0, tm, rc):
                    t = dslab[pl.ds(r0 + HALO, rc), cs] * _shifted(uslab, urot, r0 + HALO - CONV_PAD + k, rc, cs)
                    prod = t if prod is None else prod + t
                parts.append(_rows8(prod))
            rows = SUB * CONV_K
            dw8[0:rows, cs] = dw8[0:rows, cs] + jnp.concatenate(parts, axis=0)
            return carry

        lax.fori_loop(0, c // LANE, lane_block, 0)
        lax.fori_loop(0, c // LANE, lane_block_taps, 0)

        @pl.when(pl.program_id(0) == pl.num_programs(0) - 1)
        def _():
            dw_ref[...] = jnp.zeros_like(dw_ref)
            for k in range(CONV_K):
                dw_ref[k:k + 1, :] = jnp.sum(dw8[SUB * k:SUB * (k + 1), :], axis=0, keepdims=True)

        du = du_s[...]
        o_ref[:, 0:c] = (du * sg).astype(BF)
        o_ref[:, c:2 * c] = (du * a * sg * (1.0 - sg)).astype(BF)

    a_specs = _halo_specs(tm, s, c, 1)
    g_specs = _halo_specs(tm, s, c, 2)
    d_specs = _halo_specs(tm, s, c, 0)
    wsp = pl.BlockSpec(conv_w.shape, lambda i: (0, 0))
    return _call(
        body, name="conv_bwd_taps", grid=(s // tm,), in_specs=[*a_specs, *g_specs, *d_specs, wsp],
        out_specs=[pl.BlockSpec((tm, 2 * c), lambda i: (i, 0)), wsp],
        out_shape=[_sds((s, 2 * c), BF), _sds(conv_w.shape, F32)],
        scratch_shapes=[pltpu.VMEM(_slab_shapes(tm, c)[0], F32), pltpu.VMEM(_slab_shapes(tm, c)[0], F32), pltpu.VMEM((tm, c), F32),
                        pltpu.VMEM(_slab_shapes(tm, c)[1], F32), pltpu.VMEM(_slab_shapes(tm, c)[1], F32),
                        pltpu.VMEM((SUB * conv_w.shape[0], c), F32)],
        compiler_params=_cp(("arbitrary",)),
    )(h, h, h, h, h, h, duc, duc, duc, conv_w)


def attn_bwd(qc, kc, kct, v, dob, dot, lse_r, delta_r):
    _, s, _ = qc.shape
    tk = _tile(s, 512)
    tq = _tile(s, 512)
    scale = D_QK ** -0.5
    c2 = scale * LOG2E

    def body(k_ref, kt_ref, v_ref, q_ref, do_ref, dot_ref, l_ref, dl_ref, dqt_ref, dk_ref, dvt_ref):
        @pl.when(pl.program_id(1) == 0)
        def _():
            dqt_ref[...] = jnp.zeros_like(dqt_ref)

        k = k_ref[...]
        kt = kt_ref[...]
        vv = v_ref[...]

        def step(i, carry):
            dk, dvt = carry
            off = pl.multiple_of(i * tq, tq)
            q = q_ref[pl.ds(off, tq), :]
            do = do_ref[pl.ds(off, tq), :]
            pt = jnp.exp2(_dot_nt(k, q) * c2 - l_ref[:, pl.ds(off, tq)])
            dvt = dvt + _dot_nt(dot_ref[:, pl.ds(off, tq)], pt.astype(BF))
            dpt = _dot_nt(vv, do)
            dsb = (pt * (dpt - dl_ref[:, pl.ds(off, tq)]) * scale).astype(BF)
            dk = dk + _dot(dsb, q)
            dqt_ref[:, pl.ds(off, tq)] += _dot(kt, dsb)
            return dk, dvt

        dk, dvt = _unrolled_loop(s // tq, 16, step, (jnp.zeros((tk, 2 * LANE), F32), jnp.zeros((LANE, tk), F32)))
        dk_ref[...] = dk
        dvt_ref[...] = dvt

    rowv = pl.BlockSpec((None, 1, s), lambda h, j: (h, 0, 0))
    return _call(
        body, name="attn_bwd", grid=(HEADS, s // tk),
        in_specs=[pl.BlockSpec((None, tk, 2 * LANE), lambda h, j: (h, j, 0)),
                  pl.BlockSpec((None, 2 * LANE, tk), lambda h, j: (h, 0, j)),
                  pl.BlockSpec((None, tk, LANE), lambda h, j: (h, j, 0)),
                  pl.BlockSpec((None, s, 2 * LANE), lambda h, j: (h, 0, 0)),
                  pl.BlockSpec((s, LANE), lambda h, j: (0, h)),
                  pl.BlockSpec((None, LANE, s), lambda h, j: (h, 0, 0)), rowv, rowv],
        out_specs=[pl.BlockSpec((None, 2 * LANE, s), lambda h, j: (h, 0, 0)),
                   pl.BlockSpec((None, tk, 2 * LANE), lambda h, j: (h, j, 0)),
                   pl.BlockSpec((None, LANE, tk), lambda h, j: (h, 0, j))],
        out_shape=[_sds((HEADS, 2 * LANE, s), F32), _sds((HEADS, s, 2 * LANE), F32), _sds((HEADS, LANE, s), F32)],
        compiler_params=_cp(("arbitrary", "arbitrary"), 56),
    )(kc, kct, v, qc, dob, dot, lse_r, delta_r)


def q_bwd(dqt, h, g_cq, wuqt, cos, sin):
    s = h.shape[0]
    tm = _tile(s, 256)

    def body(d_ref, h_ref, g_ref, w_ref, c_ref, s_ref, dq_ref, dc_ref, dg_ref):
        @pl.when(pl.program_id(0) == 0)
        def _():
            dg_ref[...] = jnp.zeros_like(dg_ref)

        c = c_ref[...]
        sn = s_ref[...]
        for hd in range(HEADS):
            t = d_ref[hd].T
            dq_ref[:, LANE * hd:LANE * (hd + 1)] = t[:, 0:LANE].astype(BF)
            dq_ref[:, MLA_W + LANE * hd:MLA_W + LANE * (hd + 1)] = _unrope128(t[:, LANE:2 * LANE], c, sn).astype(BF)
        dy = _dot(dq_ref[...], w_ref[...])
        g = g_ref[...]
        _, xh, rr = _rms_fwd(h_ref[...], g)
        dg_ref[...] += _rows8(dy * xh)
        dc_ref[...] = _rms_bwd(dy, xh, rr, g).astype(BF)

    tab = pl.BlockSpec((tm, LANE), lambda i: (i, 0))
    return _call(
        body, name="q_bwd", grid=(s // tm,),
        in_specs=[pl.BlockSpec((HEADS, 2 * LANE, tm), lambda i: (0, 0, i)), pl.BlockSpec((tm, R_Q), lambda i: (i, 0)),
                  pl.BlockSpec((1, R_Q), lambda i: (0, 0)), pl.BlockSpec((2 * MLA_W, R_Q), lambda i: (0, 0)), tab, tab],
        out_specs=[pl.BlockSpec((tm, 2 * MLA_W), lambda i: (i, 0)), pl.BlockSpec((tm, R_Q), lambda i: (i, 0)),
                   pl.BlockSpec((SUB, R_Q), lambda i: (0, 0))],
        out_shape=[_sds((s, 2 * MLA_W), BF), _sds((s, R_Q), BF), _sds((SUB, R_Q), F32)],
        compiler_params=_cp(("arbitrary",)),
    )(dqt, h, g_cq, wuqt, cos, sin)


def kv_bwd(dk, dv, h, g_ckv, wukt, wuvt, cos, sin):
    s = h.shape[0]
    tm = _tile(s, 256)

    def body(dk_ref, dv_ref, h_ref, g_ref, wk_ref, wv_ref, c_ref, s_ref, dkn_ref, dvb_ref, dc_ref, dkr_ref, dg_ref):
        @pl.when(pl.program_id(0) == 0)
        def _():
            dg_ref[...] = jnp.zeros_like(dg_ref)

        dkr = dk_ref[0, :, LANE:2 * LANE]
        for hd in range(HEADS):
            dkn_ref[:, LANE * hd:LANE * (hd + 1)] = dk_ref[hd, :, 0:LANE].astype(BF)
            dvb_ref[:, LANE * hd:LANE * (hd + 1)] = dv_ref[hd].T.astype(BF)
            if hd > 0:
                dkr = dkr + dk_ref[hd, :, LANE:2 * LANE]
        dkr_ref[...] = _unrope128(dkr, c_ref[...], s_ref[...]).astype(BF)
        dy = _dot(dkn_ref[...], wk_ref[...]) + _dot(dvb_ref[...], wv_ref[...])
        g = g_ref[...]
        _, xh, rr = _rms_fwd(h_ref[...], g)
        dg_ref[...] += _rows8(dy * xh)
        dc_ref[...] = _rms_bwd(dy, xh, rr, g).astype(BF)

    tab = pl.BlockSpec((tm, LANE), lambda i: (i, 0))
    wsp = pl.BlockSpec((MLA_W, R_KV), lambda i: (0, 0))
    wide = pl.BlockSpec((tm, MLA_W), lambda i: (i, 0))
    return _call(
        body, name="kv_bwd", grid=(s // tm,),
        in_specs=[pl.BlockSpec((HEADS, tm, 2 * LANE), lambda i: (0, i, 0)), pl.BlockSpec((HEADS, LANE, tm), lambda i: (0, 0, i)),
                  pl.BlockSpec((tm, R_KV), lambda i: (i, 1)), pl.BlockSpec((1, R_KV), lambda i: (0, 0)), wsp, wsp, tab, tab],
        out_specs=[wide, wide, pl.BlockSpec((tm, R_KV), lambda i: (i, 0)), tab, pl.BlockSpec((SUB, R_KV), lambda i: (0, 0))],
        out_shape=[_sds((s, MLA_W), BF), _sds((s, MLA_W), BF), _sds((s, R_KV), BF), _sds((s, LANE), BF), _sds((SUB, R_KV), F32)],
        compiler_params=_cp(("arbitrary",)),
    )(dk, dv, h, g_ckv, wukt, wuvt, cos, sin)


def in_proj_bwd_ln(dh, wint, dr1, x, g_in):
    s, hc = dh.shape
    d = x.shape[1]
    tm = _tile(s, 256)

    def body(a_ref, w_ref, d1_ref, x_ref, g_ref, gx_ref, dg_ref, db_ref, acc):
        @pl.when(pl.program_id(0) == 0)
        def _():
            dg_ref[...] = jnp.zeros_like(dg_ref)
            db_ref[...] = jnp.zeros_like(db_ref)

        acc[...] = _dot(a_ref[...], w_ref[...])
        g = g_ref[...]

        def chunk(rows):
            dy = ALPHA * d1_ref[rows, :] + acc[rows, :]
            xhat, rstd = _ln_stats(x_ref[rows, :])
            dg_ref[...] += _rows8(dy * xhat)
            db_ref[...] += _rows8(dy)
            gx_ref[rows, :] = _ln_bwd(dy, xhat, rstd, g)

        _row_chunks(tm, chunk)

    tok = pl.BlockSpec((tm, d), lambda i: (i, 0))
    accs = pl.BlockSpec((SUB, d), lambda i: (0, 0))
    return _call(
        body, name="in_proj_bwd_ln", grid=(s // tm,),
        in_specs=[pl.BlockSpec((tm, hc), lambda i: (i, 0)), pl.BlockSpec((hc, d), lambda i: (0, 0)),
                  tok, tok, pl.BlockSpec((1, d), lambda i: (0, 0))],
        out_specs=[tok, accs, accs], out_shape=[_sds((s, d), F32), _sds((SUB, d), F32), _sds((SUB, d), F32)],
        scratch_shapes=[pltpu.VMEM((tm, d), F32)], compiler_params=_cp(("arbitrary",), 56),
    )(dh, wint, dr1, x, g_in)


def _adamw_math(w, g, m, v):
    m = ADAM_B1 * m + (1.0 - ADAM_B1) * g
    v = ADAM_B2 * v + (1.0 - ADAM_B2) * (g * g)
    m_hat = m / (1.0 - ADAM_B1 ** ADAM_STEP)
    v_hat = v / (1.0 - ADAM_B2 ** ADAM_STEP)
    delta = -ADAM_LR * (m_hat / (jnp.sqrt(v_hat) + ADAM_EPS) + ADAM_WD * w)
    return delta, m, v


def adamw(name, w, g, m, v):
    r, c = w.shape
    tr = _row_tile(r, c)

    def body(w_ref, g_ref, m_ref, v_ref, d_ref, mo_ref, vo_ref):
        d_ref[...], mo_ref[...], vo_ref[...] = _adamw_math(w_ref[...], g_ref[...], m_ref[...], v_ref[...])

    blk = pl.BlockSpec((tr, c), lambda i: (i, 0))
    return _call(
        body, name=name, grid=(r // tr,), in_specs=[blk] * 4, out_specs=[blk] * 3,
        out_shape=[_sds((r, c), F32)] * 3, compiler_params=_cp(("arbitrary",)),
    )(w, g, m, v)


def _coords():
    return lax.axis_index("x"), lax.axis_index("y"), lax.axis_index("c")


def _other_chips(x, y):
    return [(1 - x, y, 2 * (1 - x) + y), (x, 1 - y, 2 * x + 1 - y), (1 - x, 1 - y, 2 * (1 - x) + 1 - y)]


ANY = pl.BlockSpec(memory_space=pl.ANY)
HBM = pl.BlockSpec(memory_space=pltpu.HBM)
SEM = pl.BlockSpec(memory_space=pltpu.SEMAPHORE)
EFFECT = pltpu.SideEffectType.DATAFLOW_SIDE_EFFECTING


def _in_hbm(a):
    return pltpu.with_memory_space_constraint(a, pltpu.HBM)


def _split_plan(mode, src, land, x, y, c):
    if mode == "pair":
        rh = src.shape[1] // 2
        return [((x, y, 1 - c), src.at[:, pl.ds((1 - c) * rh, rh)], land, land)]
    me = 2 * x + y
    plan = []
    for j, (px, py, pk) in enumerate(_other_chips(x, y)):
        if mode == "gather":
            plan.append(((px, py, c), src, land.at[me], land.at[pk]))
        elif mode == "gather_half":
            mine = pl.ds(c * (src.shape[0] // 2), src.shape[0] // 2)
            plan.append(((px, py, c), src.at[mine], land.at[me, mine], land.at[pk, mine]))
        else:
            plan.append(((px, py, c), src.at[pk], land.at[j], land.at[j]))
    return plan


def _plan_len(mode):
    return 1 if mode == "pair" else N_CHIP - 1


def split_send_start(name, mode, srcs, land_shapes, order_after):
    n = len(srcs)
    np_ = _plan_len(mode)

    def body(*refs):
        ins, lands = refs[:n], refs[n:2 * n]
        ss, rs = refs[2 * n + 1], refs[2 * n + 2]
        token = refs[-1]
        x, y, c = _coords()
        for a in range(n):
            for j, (peer, src, dst, _) in enumerate(_split_plan(mode, ins[a], lands[a], x, y, c)):
                pltpu.make_async_remote_copy(src_ref=src, dst_ref=dst, send_sem=ss.at[np_ * a + j], recv_sem=rs.at[np_ * a + j],
                                             device_id=peer, device_id_type=MESH).start()
        token[...] = jnp.zeros_like(token)

    lands = [lax.empty(shp, s.dtype) for shp, s in zip(land_shapes, srcs)]
    outs = _call(
        body, name=name,
        out_shape=(pltpu.SemaphoreType.DMA((np_ * n,)), pltpu.SemaphoreType.DMA((np_ * n,)),
                   *[pltpu.HBM(s.shape, s.dtype) for s in srcs], *[pltpu.HBM(l.shape, l.dtype) for l in lands],
                   _sds((SUB, LANE), F32)),
        in_specs=[HBM] * (2 * n) + [ANY], out_specs=(SEM, SEM, *[HBM] * (2 * n), pl.BlockSpec(memory_space=pltpu.VMEM)),
        input_output_aliases={a: 2 + a for a in range(2 * n)},
        compiler_params=pltpu.CompilerParams(has_side_effects=EFFECT),
    )(*[_in_hbm(s) for s in srcs], *[_in_hbm(l) for l in lands], order_after)
    return outs[0], outs[1], list(outs[2:2 + n]), list(outs[2 + n:2 + 2 * n]), outs[-1]


def split_send_wait(name, mode, ss, rs, srcs, lands, order_after):
    n = len(srcs)
    np_ = _plan_len(mode)

    def body(*refs):
        ins, lnd = refs[:n], refs[n:2 * n]
        s_ref, r_ref = refs[2 * n], refs[2 * n + 1]
        x, y, c = _coords()
        for a in range(n):
            for j, (peer, src, _, got) in enumerate(_split_plan(mode, ins[a], lnd[a], x, y, c)):
                cp = pltpu.make_async_remote_copy(src_ref=src, dst_ref=got, send_sem=s_ref.at[np_ * a + j], recv_sem=r_ref.at[np_ * a + j],
                                                  device_id=peer, device_id_type=MESH)
                cp.wait_send()
                cp.wait_recv()

    outs = _call(
        body, name=name, out_shape=tuple(pltpu.HBM(t.shape, t.dtype) for t in (*srcs, *lands)),
        in_specs=[HBM] * (2 * n) + [SEM, SEM, ANY], out_specs=tuple([HBM] * (2 * n)),
        input_output_aliases={a: a for a in range(2 * n)},
        compiler_params=pltpu.CompilerParams(has_side_effects=EFFECT),
    )(*srcs, *lands, ss, rs, order_after)
    return list(outs[:n]), list(outs[n:])


def swap_gathered_halves(lands):
    n = len(lands)

    def body(*refs):
        outs = refs[n:2 * n]
        ss, rs = refs[2 * n:]
        x, y, c = _coords()
        cps = []
        for a in range(n):
            rh = outs[a].shape[1] // 2
            for j, (px, py, pk) in enumerate(_other_chips(x, y)):
                held = outs[a].at[pk, pl.ds(c * rh, rh)]
                cp = pltpu.make_async_remote_copy(src_ref=held, dst_ref=held, send_sem=ss.at[a, j], recv_sem=rs.at[a, j],
                                                  device_id=(x, y, 1 - c), device_id_type=MESH)
                cp.start()
                cps.append(cp)
        for a in range(n):
            rh = outs[a].shape[1] // 2
            for j, (px, py, pk) in enumerate(_other_chips(x, y)):
                theirs = outs[a].at[pk, pl.ds((1 - c) * rh, rh)]
                pltpu.make_async_remote_copy(src_ref=theirs, dst_ref=theirs, send_sem=ss.at[a, j], recv_sem=rs.at[a, j],
                                             device_id=(x, y, 1 - c), device_id_type=MESH).wait_recv()
        for cp in cps:
            cp.wait_send()

    return _call(
        body, name="swap_gathered_halves", in_specs=[ANY] * n, out_specs=[ANY] * n,
        out_shape=[_sds(l.shape, l.dtype) for l in lands], input_output_aliases={a: a for a in range(n)},
        scratch_shapes=[pltpu.SemaphoreType.DMA((n, 3))] * 2,
    )(*lands)


def pair_exchange(grads, tag):
    n = len(grads)

    def body(*refs):
        ins, outs = refs[:n], refs[n:2 * n]
        ss, rs = refs[2 * n:]
        x, y, c = _coords()
        cps = []
        for a in range(n):
            rh = ins[a].shape[1] // 2
            cp = pltpu.make_async_remote_copy(
                src_ref=ins[a].at[:, pl.ds((1 - c) * rh, rh)], dst_ref=outs[a], send_sem=ss.at[a], recv_sem=rs.at[a],
                device_id=(x, y, 1 - c), device_id_type=MESH)
            cp.start()
            cps.append(cp)
        for cp in cps:
            cp.wait()

    return _call(
        body, name="pair_exchange_" + tag, in_specs=[ANY] * n, out_specs=[ANY] * n,
        out_shape=[_sds((N_CHIP, g.shape[1] // 2, g.shape[2]), F32) for g in grads],
        scratch_shapes=[pltpu.SemaphoreType.DMA((n,))] * 2,
    )(*grads)


def _row_tile(rows, cols, itemsize=4, budget=2 * VMEM_MB):
    fits = [t for t in range(SUB, rows + 1, SUB) if rows % t == 0 and t * cols * itemsize <= budget]
    return max(fits) if fits and rows * cols * itemsize > budget else rows


def pair_add(g, r, cidx):
    _, rows, cols = g.shape
    rh = rows // 2
    tr = _row_tile(rh, cols)
    per = rh // tr

    def body(c_ref, g_ref, r_ref, o_ref):
        o_ref[...] = g_ref[...] + r_ref[...]

    return _call(
        body, name="pair_add",
        grid_spec=pltpu.PrefetchScalarGridSpec(
            num_scalar_prefetch=1, grid=(N_CHIP, per),
            in_specs=[pl.BlockSpec((None, tr, cols), lambda k, i, c: (k, c[0] * per + i, 0)),
                      pl.BlockSpec((None, tr, cols), lambda k, i, c: (k, i, 0))],
            out_specs=pl.BlockSpec((None, tr, cols), lambda k, i, c: (k, i, 0))),
        out_shape=_sds((N_CHIP, rh, cols), F32), compiler_params=_cp(("arbitrary", "arbitrary")),
    )(cidx, g, r)


def chip_add(p, r, kc):
    _, rh, cols = p.shape
    tr = _row_tile(rh, cols)
    per = rh // tr

    def body(k_ref, p_ref, r_ref, o_ref):
        o_ref[...] = ((p_ref[...] + r_ref[0]) + r_ref[1]) + r_ref[2]

    return _call(
        body, name="chip_add",
        grid_spec=pltpu.PrefetchScalarGridSpec(
            num_scalar_prefetch=1, grid=(per,),
            in_specs=[pl.BlockSpec((None, tr, cols), lambda i, k: (k[0], i, 0)),
                      pl.BlockSpec((N_CHIP - 1, tr, cols), lambda i, k: (0, i, 0))],
            out_specs=pl.BlockSpec((tr, cols), lambda i, k: (k[1] * per + i, 0))),
        out_shape=_sds((2 * rh, cols), F32), compiler_params=_cp(("arbitrary",)),
    )(kc, p, r)


def pair_share(fulls, tag):
    n = len(fulls)

    def body(*refs):
        outs = refs[n:2 * n]
        ss, rs = refs[2 * n:]
        x, y, c = _coords()
        cps = []
        for a in range(n):
            rh = outs[a].shape[0] // 2
            mine = outs[a].at[pl.ds(c * rh, rh)]
            cp = pltpu.make_async_remote_copy(
                src_ref=mine, dst_ref=mine, send_sem=ss.at[a], recv_sem=rs.at[a],
                device_id=(x, y, 1 - c), device_id_type=MESH)
            cp.start()
            cps.append(cp)
        for a, cp in enumerate(cps):
            rh = outs[a].shape[0] // 2
            theirs = outs[a].at[pl.ds((1 - c) * rh, rh)]
            cp.wait_send()
            pltpu.make_async_remote_copy(
                src_ref=theirs, dst_ref=theirs, send_sem=ss.at[a], recv_sem=rs.at[a],
                device_id=(x, y, 1 - c), device_id_type=MESH).wait_recv()

    return _call(
        body, name="pair_share_" + tag, in_specs=[ANY] * n, out_specs=[ANY] * n,
        out_shape=[_sds(f.shape, F32) for f in fulls], input_output_aliases={a: a for a in range(n)},
        scratch_shapes=[pltpu.SemaphoreType.DMA((n,))] * 2,
    )(*fulls)


def small_allreduce_adamw(part, w, m, v):
    n = part.shape[1]

    def body(p_ref, w_ref, m_ref, v_ref, g_ref, d_ref, mo_ref, vo_ref, mine, gath, ss, rs):
        x, y, c = _coords()
        me = 4 * x + 2 * y + c
        mine[...] = jnp.sum(p_ref[...], axis=0, keepdims=True)
        gath[me] = mine[...]
        cps = []
        for k in range(1, 8):
            px, py, pc = x ^ (k >> 2), y ^ ((k >> 1) & 1), c ^ (k & 1)
            cp = pltpu.make_async_remote_copy(
                src_ref=mine, dst_ref=gath.at[me], send_sem=ss.at[k - 1], recv_sem=rs.at[k - 1],
                device_id=(px, py, pc), device_id_type=MESH)
            cp.start()
            cps.append(cp)
        for k in range(1, 8):
            src = 4 * (x ^ (k >> 2)) + 2 * (y ^ ((k >> 1) & 1)) + (c ^ (k & 1))
            pltpu.make_async_remote_copy(
                src_ref=mine, dst_ref=gath.at[src], send_sem=ss.at[k - 1], recv_sem=rs.at[k - 1],
                device_id=(x, y, c), device_id_type=MESH).wait_recv()
        for cp in cps:
            cp.wait_send()
        g = gath[0]
        for dv in range(1, 8):
            g = g + gath[dv]
        g_ref[...] = g
        d_ref[...], mo_ref[...], vo_ref[...] = _adamw_math(w_ref[...], g, m_ref[...], v_ref[...])

    vm = pl.BlockSpec(memory_space=pltpu.VMEM)
    return _call(
        body, name="small_allreduce_adamw", in_specs=[vm] * 4, out_specs=[vm] * 4, out_shape=[_sds((1, n), F32)] * 4,
        scratch_shapes=[pltpu.VMEM((1, n), F32), pltpu.VMEM((8, 1, n), F32),
                        pltpu.SemaphoreType.DMA((7,)), pltpu.SemaphoreType.DMA((7,))],
    )(part, w, m, v)


def _unshard_cols(g):
    k, r, cs = g.shape
    return g.transpose(1, 0, 2).reshape(r, k * cs)


def _shard_cols(w):
    r, c = w.shape
    return w.reshape(r, N_CHIP, c // N_CHIP).transpose(1, 0, 2)


def local_step(x, positions, ln_in_g, ln_in_b, g_cq, g_ckv, conv_b, g_conv_ln, b_conv_ln, g_ln1, b_ln1, g_ln2, b_ln2,
               target, start_token, hooks):
    s, d = x.shape
    c = d - MLA_W
    row = lambda a: a.reshape(1, -1)
    ln_in_g = row(ln_in_g) + start_token[0:1, 0:1]

    half = D_ROPE // 2
    inv_freq = ROPE_BASE ** (-jnp.arange(half, dtype=F32) * (2.0 / D_ROPE))
    invf = jnp.concatenate([inv_freq, inv_freq, jnp.zeros((LANE - D_ROPE,), F32)]).reshape(1, LANE)
    cos, sin = rope_tables(positions.astype(F32).reshape(s, 1), invf)
    x0, x0b = ln_in_fwd(x, ln_in_g, row(ln_in_b))
    win_g, wuq_g, wuk_g, wuv_g, convw_g = hooks.early_weights(x0b)

    o_kr = R_Q + R_KV
    o_cv = o_kr + D_ROPE
    n_in = o_cv + 2 * c
    per = n_in // N_CHIP

    def in_cols(a, b):
        return [win_g[k, max(a, per * k) - per * k:min(b, per * (k + 1)) - per * k]
                for k in range(N_CHIP) if max(a, per * k) < min(b, per * (k + 1))]

    win_rt = jnp.concatenate(in_cols(0, o_kr) + in_cols(o_cv, n_in) + in_cols(o_kr, o_cv)
                             + [jnp.zeros((LANE - D_ROPE, d), BF)], axis=0)
    kr_blk = (o_kr + 2 * c) // LANE
    wuq = _unshard_cols(wuq_g).reshape(R_Q, HEADS, D_QK)
    wuq_r = jnp.concatenate([wuq[:, :, :D_NOPE].reshape(R_Q, MLA_W),
                             jnp.pad(wuq[:, :, D_NOPE:], ((0, 0), (0, 0), (0, LANE - D_ROPE))).reshape(R_Q, MLA_W)], axis=1)
    wuk = _unshard_cols(wuk_g)
    wuv = _unshard_cols(wuv_g)
    conv_w = jnp.pad(_unshard_cols(convw_g), ((0, 1), (0, 0)))

    h = matmul_nt("in_proj", x0b, win_rt, 256)
    qc, cqn = q_proj(h, g_cq, wuq_r, cos, sin)
    kc, kct, v, ckvn = kv_proj(h, g_ckv, wuk, wuv, cos, sin, kr_blk)
    o, ob, lse = attn_fwd(qc, kc, v)
    co, uc = conv_fwd(h, conv_w, conv_b, g_conv_ln, b_conv_ln)
    wout_g, wff1_g, wff2_g = hooks.late_weights(ob)
    wout = wout_g.reshape(d, d)
    wff2 = wff2_g.reshape(-1, d)
    r1, x1, x1b = out_proj_ln1(ob, co, wout, x0, g_ln1, b_ln1)
    rb, a1b = ff1_fwd(x1b, wff1_g)
    dr2, dr2b, loss8, dg2, db2 = ff2_ln2_loss(a1b, wff2, x1, target, g_ln2, b_ln2)

    df1b = ff2_bwd_act(dr2b, wff2, rb)
    gw_ff2 = wgrad("wgrad_ff2", a1b, dr2b, 1024, 1024).reshape(N_CHIP, -1, d)
    gw_ff1 = wgrad("wgrad_ff1", x1b, df1b, 1024, 1024, shards=N_CHIP)
    tok = hooks.ff_grads(gw_ff2, gw_ff1)
    dr1, dr1b, dg1, db1 = ff1_bwd_ln1(df1b, wff1_g, dr2, r1, g_ln1 + tok[0:1, 0:1])
    tok = hooks.ff_grads_mid(dr1b)
    gw_out = jnp.concatenate([wgrad("wgrad_out_attn", ob, dr1b, 1024, 1024)[0],
                              wgrad("wgrad_out_conv", co, dr1b, 1024, 1024)[0]], axis=0).reshape(N_CHIP, -1, d)
    dob, dot, dco, delta = out_proj_bwd(dr1b, wout.T, o)
    duc, dgc, dbc, dcb = conv_bwd_ln(uc, dco, g_conv_ln + tok[0:1, 0:1], b_conv_ln)
    dconv, gconvw = conv_bwd_taps(h, duc, conv_w)
    dqt, dk, dv = attn_bwd(qc, kc, kct, v, dob, dot, lse, delta)
    dqb, dcq, dgq = q_bwd(dqt, h, g_cq, wuq_r.T, cos, sin)
    dknb, dvb, dckv, dkr, dgkv = kv_bwd(dk, dv, h, g_ckv, wuk.T, wuv.T, cos, sin)
    gwuq_r = wgrad("wgrad_uq", cqn, dqb, 512, 1024)[0]
    gw_uk = wgrad("wgrad_uk", ckvn, dknb, 512, 1024, shards=N_CHIP)
    gw_uv = wgrad("wgrad_uv", ckvn, dvb, 512, 1024, shards=N_CHIP)
    dh = jnp.concatenate([dcq, dckv, dconv, dkr], axis=1)
    gwin_rt = wgrad("wgrad_in", dh, x0b, 640, 1024)[0]

    gwin_t = jnp.concatenate([gwin_rt[:o_kr], gwin_rt[o_kr + 2 * c:o_kr + 2 * c + D_ROPE], gwin_rt[o_kr:o_kr + 2 * c]], axis=0)
    gwin_t = jnp.pad(gwin_t.reshape(N_CHIP, per, d), ((0, 0), (0, win_g.shape[1] - per), (0, 0)))
    gwuq = jnp.concatenate([gwuq_r[:, :MLA_W].reshape(R_Q, HEADS, D_NOPE),
                            gwuq_r[:, MLA_W:].reshape(R_Q, HEADS, LANE)[:, :, :D_ROPE]], axis=2).reshape(R_Q, HEADS * D_QK)
    tok = hooks.rest_grads(dict(w_in=gwin_t, w_uq=_shard_cols(gwuq), w_uk=gw_uk, w_uv=gw_uv,
                                conv_w=_shard_cols(gconvw), w_out=gw_out))
    gx, dgin, dbin = in_proj_bwd_ln(dh, win_rt, dr1, x, ln_in_g + tok[0:1, 0:1])
    small = jnp.concatenate([dgin, dbin, dgq, dgkv, dcb, dgc, dbc, dg1, db1, dg2, db2, loss8], axis=1)
    return gx, small


BIG = ["w_in", "w_uq", "w_uk", "w_uv", "conv_w", "w_out", "w_ff1", "w_ff2"]
EARLY = ["w_in", "w_uq", "w_uk", "w_uv", "conv_w"]
LATE = ["w_out", "w_ff1", "w_ff2"]
SMALL = ["ln_in_g", "ln_in_b", "g_cq", "g_ckv", "conv_b", "g_conv_ln", "b_conv_ln", "g_ln1", "b_ln1", "g_ln2", "b_ln2"]
WEIGHTS = ["ln_in_g", "ln_in_b", "w_in", "g_cq", "w_uq", "g_ckv", "w_uk", "w_uv", "conv_w", "conv_b", "g_conv_ln",
           "b_conv_ln", "w_out", "g_ln1", "b_ln1", "w_ff1", "w_ff2", "g_ln2", "b_ln2"]


def _pad_rows(a, rows):
    return jnp.pad(a, ((0, rows - a.shape[0]), (0, 0)))


def kernel(x, positions, ln_in_g, ln_in_b, w_in, g_cq, w_uq, g_ckv, w_uk, w_uv, conv_w, conv_b, g_conv_ln, b_conv_ln, w_out, g_ln1, b_ln1, w_ff1, w_ff2, g_ln2, b_ln2, loss_target, m_ln_in_g, m_ln_in_b, m_w_in, m_g_cq, m_w_uq, m_g_ckv, m_w_uk, m_w_uv, m_conv_w, m_conv_b, m_g_conv_ln, m_b_conv_ln, m_w_out, m_g_ln1, m_b_ln1, m_w_ff1, m_w_ff2, m_g_ln2, m_b_ln2, v_ln_in_g, v_ln_in_b, v_w_in, v_g_cq, v_w_uq, v_g_ckv, v_w_uk, v_w_uv, v_conv_w, v_conv_b, v_g_conv_ln, v_b_conv_ln, v_w_out, v_g_ln1, v_b_ln1, v_w_ff1, v_w_ff2, v_g_ln2, v_b_ln2):
    w = dict(ln_in_g=ln_in_g, ln_in_b=ln_in_b, w_in=w_in, g_cq=g_cq, w_uq=w_uq, g_ckv=g_ckv, w_uk=w_uk, w_uv=w_uv,
             conv_w=conv_w, conv_b=conv_b, g_conv_ln=g_conv_ln, b_conv_ln=b_conv_ln, w_out=w_out, g_ln1=g_ln1,
             b_ln1=b_ln1, w_ff1=w_ff1, w_ff2=w_ff2, g_ln2=g_ln2, b_ln2=b_ln2)
    m = dict(ln_in_g=m_ln_in_g, ln_in_b=m_ln_in_b, w_in=m_w_in, g_cq=m_g_cq, w_uq=m_w_uq, g_ckv=m_g_ckv, w_uk=m_w_uk,
             w_uv=m_w_uv, conv_w=m_conv_w, conv_b=m_conv_b, g_conv_ln=m_g_conv_ln, b_conv_ln=m_b_conv_ln, w_out=m_w_out,
             g_ln1=m_g_ln1, b_ln1=m_b_ln1, w_ff1=m_w_ff1, w_ff2=m_w_ff2, g_ln2=m_g_ln2, b_ln2=m_b_ln2)
    v = dict(ln_in_g=v_ln_in_g, ln_in_b=v_ln_in_b, w_in=v_w_in, g_cq=v_g_cq, w_uq=v_w_uq, g_ckv=v_g_ckv, w_uk=v_w_uk,
             w_uv=v_w_uv, conv_w=v_conv_w, conv_b=v_conv_b, g_conv_ln=v_g_conv_ln, b_conv_ln=v_b_conv_ln, w_out=v_w_out,
             g_ln1=v_g_ln1, b_ln1=v_b_ln1, w_ff1=v_w_ff1, w_ff2=v_w_ff2, g_ln2=v_g_ln2, b_ln2=v_b_ln2)

    as2d = lambda t, n: t[n][0].T if n == "w_in" else t[n][0]
    sh2 = {n: as2d(w, n) for n in BIG}
    cidx = lax.axis_index("c").astype(jnp.int32).reshape(1)
    me = 2 * lax.axis_index("x") + lax.axis_index("y")
    kc = jnp.stack([me, lax.axis_index("c")]).astype(jnp.int32)

    pad_to = {"conv_w": CONV_K + 1, "w_in": -(-sh2["w_in"].shape[0] // (4 * SUB)) * (4 * SUB)}
    early = [_pad_rows(sh2[n] if n == "conv_w" else sh2[n].astype(BF), pad_to.get(n, sh2[n].shape[0])) for n in EARLY]
    eg = split_send_start("early_weights_start", "gather_half", early, [(N_CHIP,) + a.shape for a in early], ln_in_g)
    late = [sh2[n].astype(BF) for n in LATE]
    ag = split_send_start("late_weights_start", "gather", late, [(N_CHIP,) + a.shape for a in late], eg[4])
    rest = [n for n in BIG if n not in ("w_ff2", "w_ff1")]
    flight = {}

    class Hooks:
        @staticmethod
        def early_weights(after):
            mine, lands = split_send_wait("early_weights_wait", "gather_half", *eg[:4], after)
            full = [lax.dynamic_update_slice(g, a[None], (me, 0, 0)) for g, a in zip(swap_gathered_halves(lands), mine)]
            return [g[:, :CONV_K] if n == "conv_w" else g for n, g in zip(EARLY, full)]

        @staticmethod
        def late_weights(after):
            mine, lands = split_send_wait("late_weights_wait", "gather", *ag[:4], after)
            return [lax.dynamic_update_slice(g, a[None], (me, 0, 0)) for g, a in zip(lands, mine)]

        @staticmethod
        def ff_grads(gw_ff2, gw_ff1):
            full = [gw_ff2, gw_ff1]
            st = split_send_start("ff_pair_start", "pair", full, [(N_CHIP, g.shape[1] // 2, g.shape[2]) for g in full], ag[4])
            flight["ff_pair"] = st[:4]
            flight["token"] = st[4]
            return st[4]

        @staticmethod
        def ff_grads_mid(after):
            full, recv = split_send_wait("ff_pair_wait", "pair", *flight["ff_pair"], after)
            psum = [pair_add(g, r, cidx) for g, r in zip(full, recv)]
            st = split_send_start("ff_grads_start", "scatter", psum, [(N_CHIP - 1,) + p.shape[1:] for p in psum], flight["token"])
            flight["ff"] = st[:4]
            flight["token"] = st[4]
            return st[4]

        @staticmethod
        def rest_grads(big):
            full = [big[n] for n in rest]
            psum = [pair_add(g, r, cidx) for g, r in zip(full, pair_exchange(full, "rest"))]
            st = split_send_start("rest_grads_start", "scatter", psum, [(N_CHIP - 1,) + p.shape[1:] for p in psum], flight["token"])
            flight["rest"] = st[:4]
            return st[4]

    gx, small = local_step(x[0], positions[0], ln_in_g, ln_in_b, g_cq, g_ckv, conv_b, g_conv_ln, b_conv_ln, g_ln1, b_ln1,
                           g_ln2, b_ln2, loss_target[0], ag[4], Hooks)

    ff_psum, ff_got = split_send_wait("ff_grads_wait", "scatter", *flight["ff"], gx)
    rest_psum, rest_got = split_send_wait("rest_grads_wait", "scatter", *flight["rest"], gx)
    summed = [chip_add(p, r, kc) for p, r in zip(rest_psum + ff_psum, rest_got + ff_got)]
    gsh = dict(zip(rest + ["w_ff2", "w_ff1"], pair_share(summed, "all")))
    for n in pad_to:
        gsh[n] = gsh[n][:sh2[n].shape[0]]

    grad, delta, new_m, new_v = {}, {}, {}, {}
    for n in BIG:
        back = (lambda a: a.T[None]) if n == "w_in" else (lambda a: a[None])
        d_, m_, v_ = adamw("adamw_" + n, sh2[n], gsh[n], as2d(m, n), as2d(v, n))
        grad[n], delta[n], new_m[n], new_v[n] = back(gsh[n]), back(d_), back(m_), back(v_)

    flat = lambda t: jnp.concatenate([t[n].reshape(1, -1) for n in SMALL] + [jnp.zeros((1, LANE), F32)], axis=1)
    g_s, d_s, m_s, v_s = small_allreduce_adamw(small, flat(w), flat(m), flat(v))
    off = 0
    for n in SMALL:
        sz = w[n].size
        for dst, src in ((grad, g_s), (delta, d_s), (new_m, m_s), (new_v, v_s)):
            dst[n] = src[0, off:off + sz].reshape(w[n].shape)
        off += sz
    loss = jnp.sum(g_s[0, off:off + LANE])

    return (loss, gx[None], *[grad[n] for n in WEIGHTS], *[delta[n] for n in WEIGHTS],
            *[new_m[n] for n in WEIGHTS], *[new_v[n] for n in WEIGHTS])
```

```python
import jax
import jax.numpy as jnp
from jax import lax
from jax.experimental import pallas as pl
from jax.experimental.pallas import tpu as pltpu

F32 = jnp.float32
BF = jnp.bfloat16

HEADS = 8
D_NOPE = 128
D_ROPE = 64
D_V = 128
D_QK = D_NOPE + D_ROPE
R_Q = 512
R_KV = 512
MLA_W = HEADS * D_V
CONV_K = 31
CONV_PAD = CONV_K // 2
ROPE_BASE = 10000.0
LOG2E = 1.4426950408889634
LN2 = 0.6931471805599453
LN_EPS = 1e-5
RMS_EPS = 1e-6
ALPHA = (2.0 * 1) ** 0.25
ADAM_LR = 0.001
ADAM_B1 = 0.9
ADAM_B2 = 0.999
ADAM_EPS = 1e-08
ADAM_WD = 0.01
ADAM_STEP = 10

LANE = 128
SUB = 8
HALO = 16
N_CHIP = 4
MESH = pl.DeviceIdType.MESH
VMEM_MB = 1024 * 1024


def _call(body, **kw):
    return pl.pallas_call(body, **kw)


def _cp(sem, mb=48):
    return pltpu.CompilerParams(dimension_semantics=sem, vmem_limit_bytes=mb * VMEM_MB)


def _sds(shape, dt):
    return jax.ShapeDtypeStruct(shape, dt)


def _dot(a, b):
    return jnp.dot(a, b, preferred_element_type=F32)


def _dot_nt(a, b):
    return lax.dot_general(a, b, (((1,), (1,)), ((), ())), preferred_element_type=F32)


def _dot_tn(a, b):
    return lax.dot_general(a, b, (((0,), (0,)), ((), ())), preferred_element_type=F32)


def _rows8(v):
    t, n = v.shape
    return v.reshape(t // SUB, SUB, n).sum(axis=0)


def _ln_stats(r):
    mu = jnp.mean(r, axis=-1, keepdims=True)
    xc = r - mu
    var = jnp.mean(xc * xc, axis=-1, keepdims=True)
    rstd = lax.rsqrt(var + LN_EPS)
    return xc * rstd, rstd


def _ln_bwd(dy, xhat, rstd, g):
    dyh = dy * g
    m1 = jnp.mean(dyh, axis=-1, keepdims=True)
    m2 = jnp.mean(dyh * xhat, axis=-1, keepdims=True)
    return rstd * (dyh - m1 - xhat * m2)


def _rms_fwd(x, g):
    rr = lax.rsqrt(jnp.mean(x * x, axis=-1, keepdims=True) + RMS_EPS)
    xh = x * rr
    return xh * g, xh, rr


def _rms_bwd(dy, xh, rr, g):
    dyg = dy * g
    return rr * (dyg - xh * jnp.mean(dyg * xh, axis=-1, keepdims=True))


def _rope128(x, cos, sin_signed):
    lane = lax.broadcasted_iota(jnp.int32, x.shape, 1)
    rot = jnp.where(lane < D_ROPE // 2, pltpu.roll(x, LANE - D_ROPE // 2, 1), pltpu.roll(x, D_ROPE // 2, 1))
    return x * cos + rot * sin_signed


def _unrope128(dy, cos, sin_signed):
    t = dy * sin_signed
    lane = lax.broadcasted_iota(jnp.int32, dy.shape, 1)
    rot = jnp.where(lane < D_ROPE // 2, pltpu.roll(t, LANE - D_ROPE // 2, 1), pltpu.roll(t, D_ROPE // 2, 1))
    return dy * cos + rot


def _as_row(col):
    return jnp.transpose(jnp.broadcast_to(col, (col.shape[0], LANE)))[0:1, :]


def _sigmoid(x):
    return 1.0 / (1.0 + jnp.exp(-x))


def _row_chunks(tm, fn, rc=128):
    rc = min(rc, tm)

    def step(ci, carry):
        fn(pl.ds(pl.multiple_of(ci * rc, rc), rc))
        return carry

    lax.fori_loop(0, tm // rc, step, 0)


def _unrolled_loop(n, unroll, fn, init):
    unroll = min(n, unroll)
    assert n % unroll == 0

    def body(t, carry):
        for u in range(unroll):
            carry = fn(t * unroll + u, carry)
        return carry

    return lax.fori_loop(0, n // unroll, body, init)


def _tile(s, want):
    t = min(s, want)
    assert s % t == 0
    return t


def rope_tables(pos_f, invf):
    s = pos_f.shape[0]
    tm = _tile(s, 1024)

    def body(p_ref, f_ref, c_ref, s_ref):
        ang = p_ref[...] * f_ref[...]
        lane = lax.broadcasted_iota(jnp.int32, ang.shape, 1)
        c = jnp.cos(ang)
        sn = jnp.sin(ang)
        c_ref[...] = jnp.where(lane < D_ROPE, c, 0.0)
        s_ref[...] = jnp.where(lane < D_ROPE // 2, -sn, jnp.where(lane < D_ROPE, sn, 0.0))

    return _call(
        body, name="rope_tables", grid=(s // tm,),
        in_specs=[pl.BlockSpec((tm, 1), lambda i: (i, 0)), pl.BlockSpec((1, LANE), lambda i: (0, 0))],
        out_specs=[pl.BlockSpec((tm, LANE), lambda i: (i, 0))] * 2,
        out_shape=[_sds((s, LANE), F32)] * 2,
        compiler_params=_cp(("arbitrary",)),
    )(pos_f, invf)


def ln_in_fwd(x, g, b):
    s, d = x.shape
    tm = _tile(s, 512)

    def body(x_ref, g_ref, b_ref, o_ref, ob_ref):
        xhat, _ = _ln_stats(x_ref[...])
        y = xhat * g_ref[...] + b_ref[...]
        o_ref[...] = y
        ob_ref[...] = y.astype(BF)

    row = pl.BlockSpec((1, d), lambda i: (0, 0))
    tok = pl.BlockSpec((tm, d), lambda i: (i, 0))
    return _call(
        body, name="ln_in_fwd", grid=(s // tm,), in_specs=[tok, row, row], out_specs=[tok, tok],
        out_shape=[_sds((s, d), F32), _sds((s, d), BF)], compiler_params=_cp(("arbitrary",)),
    )(x, g, b)


def matmul_nt(name, a, wt, tm, out_dtype=F32):
    s, k = a.shape
    n = wt.shape[0]
    tm = _tile(s, tm)

    def body(a_ref, w_ref, o_ref):
        o_ref[...] = _dot_nt(a_ref[...], w_ref[...]).astype(o_ref.dtype)

    return _call(
        body, name=name, grid=(s // tm,),
        in_specs=[pl.BlockSpec((tm, k), lambda i: (i, 0)), pl.BlockSpec((n, k), lambda i: (0, 0))],
        out_specs=pl.BlockSpec((tm, n), lambda i: (i, 0)),
        out_shape=_sds((s, n), out_dtype), compiler_params=_cp(("arbitrary",)),
    )(a, wt)


def q_proj(h, g_cq, wuq, cos, sin):
    s = h.shape[0]
    tm = _tile(s, 512)

    def body(h_ref, g_ref, w_ref, c_ref, s_ref, q_ref, n_ref):
        y, _, _ = _rms_fwd(h_ref[...], g_ref[...])
        yb = y.astype(BF)
        n_ref[...] = yb
        q = _dot(yb, w_ref[...])
        c = c_ref[...]
        sn = s_ref[...]
        for hd in range(HEADS):
            q_ref[hd, :, 0:LANE] = q[:, LANE * hd:LANE * (hd + 1)].astype(BF)
            qr = q[:, MLA_W + LANE * hd:MLA_W + LANE * (hd + 1)]
            q_ref[hd, :, LANE:2 * LANE] = _rope128(qr, c, sn).astype(BF)

    return _call(
        body, name="q_proj", grid=(s // tm,),
        in_specs=[pl.BlockSpec((tm, R_Q), lambda i: (i, 0)), pl.BlockSpec((1, R_Q), lambda i: (0, 0)),
                  pl.BlockSpec((R_Q, 2 * MLA_W), lambda i: (0, 0)),
                  pl.BlockSpec((tm, LANE), lambda i: (i, 0)), pl.BlockSpec((tm, LANE), lambda i: (i, 0))],
        out_specs=[pl.BlockSpec((HEADS, tm, 2 * LANE), lambda i: (0, i, 0)), pl.BlockSpec((tm, R_Q), lambda i: (i, 0))],
        out_shape=[_sds((HEADS, s, 2 * LANE), BF), _sds((s, R_Q), BF)], compiler_params=_cp(("arbitrary",)),
    )(h, g_cq, wuq, cos, sin)


def kv_proj(h, g_ckv, wuk, wuv, cos, sin, kr_blk):
    s = h.shape[0]
    tm = _tile(s, 512)

    def body(h_ref, kr_ref, g_ref, wk_ref, wv_ref, c_ref, s_ref, k_ref, kt_ref, v_ref, n_ref):
        y, _, _ = _rms_fwd(h_ref[...], g_ref[...])
        yb = y.astype(BF)
        n_ref[...] = yb
        kn = _dot(yb, wk_ref[...])
        v = _dot(yb, wv_ref[...])
        kr = _rope128(kr_ref[...], c_ref[...], s_ref[...])
        krb = kr.astype(BF)
        krt = kr.T.astype(BF)
        for hd in range(HEADS):
            knh = kn[:, LANE * hd:LANE * (hd + 1)]
            k_ref[hd, :, 0:LANE] = knh.astype(BF)
            k_ref[hd, :, LANE:2 * LANE] = krb
            kt_ref[hd, 0:LANE, :] = knh.T.astype(BF)
            kt_ref[hd, LANE:2 * LANE, :] = krt
            v_ref[hd] = v[:, LANE * hd:LANE * (hd + 1)].astype(BF)

    tab = pl.BlockSpec((tm, LANE), lambda i: (i, 0))
    wsp = pl.BlockSpec((R_KV, MLA_W), lambda i: (0, 0))
    return _call(
        body, name="kv_proj", grid=(s // tm,),
        in_specs=[pl.BlockSpec((tm, R_KV), lambda i: (i, 1)), pl.BlockSpec((tm, LANE), lambda i: (i, kr_blk)),
                  pl.BlockSpec((1, R_KV), lambda i: (0, 0)), wsp, wsp, tab, tab],
        out_specs=[pl.BlockSpec((HEADS, tm, 2 * LANE), lambda i: (0, i, 0)), pl.BlockSpec((HEADS, 2 * LANE, tm), lambda i: (0, 0, i)),
                   pl.BlockSpec((HEADS, tm, LANE), lambda i: (0, i, 0)), pl.BlockSpec((tm, R_KV), lambda i: (i, 0))],
        out_shape=[_sds((HEADS, s, 2 * LANE), BF), _sds((HEADS, 2 * LANE, s), BF), _sds((HEADS, s, LANE), BF), _sds((s, R_KV), BF)],
        compiler_params=_cp(("arbitrary",)),
    )(h, h, g_ckv, wuk, wuv, cos, sin)


def attn_fwd(qc, kc, v):
    _, s, _ = qc.shape
    tq = _tile(s, 256)
    tk = _tile(s, 512)
    scale = D_QK ** -0.5
    c2 = scale * LOG2E
    nk = s // tk
    nb = tk // LANE
    un = 8

    def body(q_ref, k_ref, v_ref, o_ref, ob_ref, l_ref, s_scr, m_scr):
        q = q_ref[...]

        def scores(j, mpart):
            off = pl.multiple_of(j * tk, tk)
            sc = _dot_nt(q, k_ref[pl.ds(off, tk), :]) * c2
            s_scr[:, pl.ds(off, tk)] = sc
            for b in range(nb):
                mpart = jnp.maximum(mpart, sc[:, LANE * b:LANE * (b + 1)])
            return mpart

        mpart = _unrolled_loop(nk, un, scores, jnp.full((tq, LANE), -jnp.inf, F32))
        m = jnp.max(mpart, axis=-1, keepdims=True)
        m_scr[...] = jnp.broadcast_to(m, (tq, LANE))

        def weigh(j, carry):
            lpart, acc = carry
            off = pl.multiple_of(j * tk, tk)
            ps = []
            for b in range(nb):
                p = jnp.exp2(s_scr[:, pl.ds(off + LANE * b, LANE)] - m_scr[...])
                lpart = lpart + p
                ps.append(p.astype(BF))
            acc = acc + _dot(jnp.concatenate(ps, axis=1), v_ref[pl.ds(off, tk), :])
            return lpart, acc

        lpart, acc = _unrolled_loop(nk, un, weigh, (jnp.zeros((tq, LANE), F32), jnp.zeros((tq, D_V), F32)))
        l = jnp.sum(lpart, axis=-1, keepdims=True)
        o = acc / l
        o_ref[...] = o
        ob_ref[...] = o.astype(BF)
        l_ref[...] = _as_row(m + jnp.log(l) * LOG2E)

    return _call(
        body, name="attn_fwd", grid=(HEADS, s // tq),
        in_specs=[pl.BlockSpec((None, tq, 2 * LANE), lambda h, i: (h, i, 0)),
                  pl.BlockSpec((None, s, 2 * LANE), lambda h, i: (h, 0, 0)),
                  pl.BlockSpec((None, s, LANE), lambda h, i: (h, 0, 0))],
        out_specs=[pl.BlockSpec((tq, LANE), lambda h, i: (i, h)), pl.BlockSpec((tq, LANE), lambda h, i: (i, h)),
                   pl.BlockSpec((None, 1, tq), lambda h, i: (h, 0, i))],
        out_shape=[_sds((s, MLA_W), F32), _sds((s, MLA_W), BF), _sds((HEADS, 1, s), F32)],
        scratch_shapes=[pltpu.VMEM((tq, s + LANE), F32), pltpu.VMEM((tq, LANE), F32)],
        compiler_params=_cp(("arbitrary", "arbitrary")),
    )(qc, kc, v)


def _halo_specs(tm, s, width, col):
    r = tm // HALO
    nb = s // HALO
    cur = pl.BlockSpec((tm, width), lambda i: (i, col))
    prev = pl.BlockSpec((HALO, width), lambda i: (jnp.maximum(i * r - 1, 0), col))
    nxt = pl.BlockSpec((HALO, width), lambda i: (jnp.minimum((i + 1) * r, nb - 1), col))
    return cur, prev, nxt


def _slab_shapes(tm, c):
    return (tm + 2 * HALO, c + LANE), (SUB - 1, tm + 2 * HALO - SUB, c + LANE)


def _fill_slab(slab, tm, prev, cur, nxt):
    i = pl.program_id(0)
    last = pl.num_programs(0) - 1
    c = cur.shape[1]
    slab[0:HALO, 0:c] = jnp.where(i > 0, prev, 0.0)
    slab[HALO:HALO + tm, 0:c] = cur
    slab[HALO + tm:2 * HALO + tm, 0:c] = jnp.where(i < last, nxt, 0.0)


def _rotate_slab(slab, rot, tm):
    rows = tm + 2 * HALO - SUB
    c = slab.shape[1] - LANE
    for b in range(1, SUB):
        rot[b - 1, :, 0:c] = slab[pl.ds(b, rows), 0:c]


def _shifted(slab, rot, start, rc, cs):
    b = start % SUB
    if b == 0:
        return slab[pl.ds(start, rc), cs]
    return rot[b - 1, pl.ds(start - b, rc), cs]


def conv_fwd(h, conv_w, conv_b, g_ln, b_ln):
    s = h.shape[0]
    c = conv_w.shape[1]
    tm = _tile(s, 256)
    rc = _tile(tm, 64)

    def body(a_ref, ap_ref, an_ref, g_ref, gp_ref, gn_ref, w_ref, cb_ref, lg_ref, lb_ref, co_ref, uc_ref, slab, rot):
        _fill_slab(slab, tm, ap_ref[...] * _sigmoid(gp_ref[...]), a_ref[...] * _sigmoid(g_ref[...]),
                   an_ref[...] * _sigmoid(gn_ref[...]))
        _rotate_slab(slab, rot, tm)

        def lane_block(cb, carry):
            cs = pl.ds(pl.multiple_of(cb * LANE, LANE), LANE)
            for r0 in range(0, tm, rc):
                acc = jnp.zeros((rc, LANE), F32)
                for k in range(CONV_K):
                    acc = acc + w_ref[k:k + 1, cs] * _shifted(slab, rot, r0 + HALO - CONV_PAD + k, rc, cs)
                uc_ref[r0:r0 + rc, cs] = acc + cb_ref[:, cs]
            return carry

        lax.fori_loop(0, c // LANE, lane_block, 0)
        xhat, _ = _ln_stats(uc_ref[...])
        cl = xhat * lg_ref[...] + lb_ref[...]
        co_ref[...] = (cl * _sigmoid(cl)).astype(BF)

    a_specs = _halo_specs(tm, s, c, 1)
    g_specs = _halo_specs(tm, s, c, 2)
    row = pl.BlockSpec((1, c), lambda i: (0, 0))
    tok = pl.BlockSpec((tm, c), lambda i: (i, 0))
    return _call(
        body, name="conv_fwd", grid=(s // tm,),
        in_specs=[*a_specs, *g_specs, pl.BlockSpec(conv_w.shape, lambda i: (0, 0)), row, row, row],
        out_specs=[tok, tok], out_shape=[_sds((s, c), BF), _sds((s, c), F32)],
        scratch_shapes=[pltpu.VMEM(shp, F32) for shp in _slab_shapes(tm, c)],
        compiler_params=_cp(("arbitrary",)),
    )(h, h, h, h, h, h, conv_w, conv_b, g_ln, b_ln)


def out_proj_ln1(ob, co, wout, x0, g1, b1):
    s, d = x0.shape
    kh = ob.shape[1]
    tm = _tile(s, 256)

    def body(o_ref, c_ref, w_ref, x_ref, g_ref, b_ref, r_ref, x1_ref, x1b_ref, acc):
        acc[...] = _dot(o_ref[...], w_ref[0:kh, :]) + _dot(c_ref[...], w_ref[kh:2 * kh, :])
        g = g_ref[...]
        b = b_ref[...]

        def chunk(rows):
            r = ALPHA * x_ref[rows, :] + acc[rows, :]
            r_ref[rows, :] = r
            xhat, _ = _ln_stats(r)
            y = xhat * g + b
            x1_ref[rows, :] = y
            x1b_ref[rows, :] = y.astype(BF)

        _row_chunks(tm, chunk)

    half = pl.BlockSpec((tm, kh), lambda i: (i, 0))
    tok = pl.BlockSpec((tm, d), lambda i: (i, 0))
    row = pl.BlockSpec((1, d), lambda i: (0, 0))
    return _call(
        body, name="out_proj_ln1", grid=(s // tm,),
        in_specs=[half, half, pl.BlockSpec((2 * kh, d), lambda i: (0, 0)), tok, row, row],
        out_specs=[tok, tok, tok], out_shape=[_sds((s, d), F32), _sds((s, d), F32), _sds((s, d), BF)],
        scratch_shapes=[pltpu.VMEM((tm, d), F32)], compiler_params=_cp(("arbitrary",)),
    )(ob, co, wout, x0, g1, b1)


def ff1_fwd(x1b, wff1_g):
    s, d = x1b.shape
    nsh, _, fs = wff1_g.shape
    tm = _tile(s, 1024)
    tn = _tile(fs, 1024)
    per = fs // tn

    def body(a_ref, w_ref, r_ref, a1_ref):
        r = jnp.maximum(_dot(a_ref[...], w_ref[...]), 0.0)
        r_ref[...] = r.astype(BF)
        a1_ref[...] = (r * r).astype(BF)

    out = pl.BlockSpec((tm, tn), lambda i, j: (i, j))
    return _call(
        body, name="ff1_fwd", grid=(s // tm, nsh * per),
        in_specs=[pl.BlockSpec((tm, d), lambda i, j: (i, 0)),
                  pl.BlockSpec((None, d, tn), lambda i, j: (j // per, 0, j % per))],
        out_specs=[out, out], out_shape=[_sds((s, nsh * fs), BF)] * 2,
        compiler_params=_cp(("arbitrary", "arbitrary")),
    )(x1b, wff1_g)


def ff2_ln2_loss(a1b, wff2, x1, target, g2, b2):
    s, f = a1b.shape
    d = x1.shape[1]
    tm = _tile(s, 512)
    tk = _tile(f, 2048)
    nk = f // tk

    def body(a_ref, w_ref, x_ref, t_ref, g_ref, b_ref, dr_ref, drb_ref, loss_ref, dg_ref, db_ref, acc):
        i = pl.program_id(0)
        k = pl.program_id(1)

        @pl.when(k == 0)
        def _():
            acc[...] = _dot(a_ref[...], w_ref[...])

        @pl.when(k > 0)
        def _():
            acc[...] += _dot(a_ref[...], w_ref[...])

        @pl.when(jnp.logical_and(i == 0, k == 0))
        def _():
            loss_ref[...] = jnp.zeros_like(loss_ref)
            dg_ref[...] = jnp.zeros_like(dg_ref)
            db_ref[...] = jnp.zeros_like(db_ref)

        @pl.when(k == nk - 1)
        def _():
            g = g_ref[...]

            def chunk(rows):
                r = ALPHA * x_ref[rows, :] + acc[rows, :]
                xhat, rstd = _ln_stats(r)
                e = xhat * g + b_ref[...] - t_ref[rows, :]
                e2 = _rows8(e * e)
                part = e2[:, 0:LANE]
                for c in range(1, d // LANE):
                    part = part + e2[:, LANE * c:LANE * (c + 1)]
                loss_ref[...] += part * (0.5 / d)
                dy = e * (1.0 / d)
                dg_ref[...] += _rows8(dy * xhat)
                db_ref[...] += _rows8(dy)
                dr = _ln_bwd(dy, xhat, rstd, g)
                dr_ref[rows, :] = dr
                drb_ref[rows, :] = dr.astype(BF)

            _row_chunks(tm, chunk)

    tok = pl.BlockSpec((tm, d), lambda i, k: (i, 0))
    row = pl.BlockSpec((1, d), lambda i, k: (0, 0))
    accs = pl.BlockSpec((SUB, d), lambda i, k: (0, 0))
    return _call(
        body, name="ff2_ln2_loss", grid=(s // tm, nk),
        in_specs=[pl.BlockSpec((tm, tk), lambda i, k: (i, k)), pl.BlockSpec((tk, d), lambda i, k: (k, 0)),
                  tok, tok, row, row],
        out_specs=[tok, tok, pl.BlockSpec((SUB, LANE), lambda i, k: (0, 0)), accs, accs],
        out_shape=[_sds((s, d), F32), _sds((s, d), BF), _sds((SUB, LANE), F32), _sds((SUB, d), F32), _sds((SUB, d), F32)],
        scratch_shapes=[pltpu.VMEM((tm, d), F32)], compiler_params=_cp(("arbitrary", "arbitrary"), 60),
    )(a1b, wff2, x1, target, g2, b2)


def ff2_bwd_act(dr2b, wff2, rb):
    s, d = dr2b.shape
    f = wff2.shape[0]
    tm = _tile(s, 1024)
    tn = _tile(f, 1024)

    def body(a_ref, w_ref, r_ref, o_ref):
        o_ref[...] = (_dot_nt(a_ref[...], w_ref[...]) * (2.0 * r_ref[...].astype(F32))).astype(BF)

    return _call(
        body, name="ff2_bwd_act", grid=(s // tm, f // tn),
        in_specs=[pl.BlockSpec((tm, d), lambda i, j: (i, 0)), pl.BlockSpec((tn, d), lambda i, j: (j, 0)),
                  pl.BlockSpec((tm, tn), lambda i, j: (i, j))],
        out_specs=pl.BlockSpec((tm, tn), lambda i, j: (i, j)), out_shape=_sds((s, f), BF),
        compiler_params=_cp(("arbitrary", "arbitrary")),
    )(dr2b, wff2, rb)


def wgrad(name, a, b, tm, tn, tk=2048, shards=1):
    s, m = a.shape
    n = b.shape[1]
    tm = _tile(m, tm)
    ns = n // shards
    tn = _tile(ns, tn)
    tk = _tile(s, tk)
    per = ns // tn

    def body(a_ref, b_ref, o_ref):
        k = pl.program_id(2)

        @pl.when(k == 0)
        def _():
            o_ref[...] = _dot_tn(a_ref[...], b_ref[...])

        @pl.when(k > 0)
        def _():
            o_ref[...] += _dot_tn(a_ref[...], b_ref[...])

    return _call(
        body, name=name, grid=(m // tm, n // tn, s // tk),
        in_specs=[pl.BlockSpec((tk, tm), lambda i, j, k: (k, i)), pl.BlockSpec((tk, tn), lambda i, j, k: (k, j))],
        out_specs=pl.BlockSpec((None, tm, tn), lambda i, j, k: (j // per, i, j % per)),
        out_shape=_sds((shards, m, ns), F32), compiler_params=_cp(("arbitrary", "arbitrary", "arbitrary")),
    )(a, b)


def ff1_bwd_ln1(df1b, wff1_g, dr2, r1, g1):
    s, f = df1b.shape
    d = dr2.shape[1]
    tm = _tile(s, 512)
    tk = _tile(wff1_g.shape[2], 2048)
    per = wff1_g.shape[2] // tk
    nk = f // tk

    def body(a_ref, w_ref, d2_ref, r_ref, g_ref, dr_ref, drb_ref, dg_ref, db_ref, acc):
        i = pl.program_id(0)
        k = pl.program_id(1)

        @pl.when(k == 0)
        def _():
            acc[...] = _dot_nt(a_ref[...], w_ref[...])

        @pl.when(k > 0)
        def _():
            acc[...] += _dot_nt(a_ref[...], w_ref[...])

        @pl.when(jnp.logical_and(i == 0, k == 0))
        def _():
            dg_ref[...] = jnp.zeros_like(dg_ref)
            db_ref[...] = jnp.zeros_like(db_ref)

        @pl.when(k == nk - 1)
        def _():
            g = g_ref[...]

            def chunk(rows):
                dy = ALPHA * d2_ref[rows, :] + acc[rows, :]
                xhat, rstd = _ln_stats(r_ref[rows, :])
                dg_ref[...] += _rows8(dy * xhat)
                db_ref[...] += _rows8(dy)
                dr = _ln_bwd(dy, xhat, rstd, g)
                dr_ref[rows, :] = dr
                drb_ref[rows, :] = dr.astype(BF)

            _row_chunks(tm, chunk)

    tok = pl.BlockSpec((tm, d), lambda i, k: (i, 0))
    accs = pl.BlockSpec((SUB, d), lambda i, k: (0, 0))
    return _call(
        body, name="ff1_bwd_ln1", grid=(s // tm, nk),
        in_specs=[pl.BlockSpec((tm, tk), lambda i, k: (i, k)), pl.BlockSpec((None, d, tk), lambda i, k: (k // per, 0, k % per)),
                  tok, tok, pl.BlockSpec((1, d), lambda i, k: (0, 0))],
        out_specs=[tok, tok, accs, accs],
        out_shape=[_sds((s, d), F32), _sds((s, d), BF), _sds((SUB, d), F32), _sds((SUB, d), F32)],
        scratch_shapes=[pltpu.VMEM((tm, d), F32)], compiler_params=_cp(("arbitrary", "arbitrary"), 60),
    )(df1b, wff1_g, dr2, r1, g1)


def out_proj_bwd(dr1b, woutt, o):
    s, d = dr1b.shape
    tm = _tile(s, 256)

    def body(a_ref, w_ref, o_ref, do_ref, dot_ref, dc_ref, dl_ref):
        dcat = _dot(a_ref[...], w_ref[...])
        do = dcat[:, 0:MLA_W]
        do_ref[...] = do.astype(BF)
        dc_ref[...] = dcat[:, MLA_W:]
        prod = do * o_ref[...]
        for hd in range(HEADS):
            hs = slice(LANE * hd, LANE * (hd + 1))
            dl_ref[hd] = _as_row(jnp.sum(prod[:, hs], axis=-1, keepdims=True))
            dot_ref[hd] = do[:, hs].T.astype(BF)

    half = pl.BlockSpec((tm, MLA_W), lambda i: (i, 0))
    return _call(
        body, name="out_proj_bwd", grid=(s // tm,),
        in_specs=[pl.BlockSpec((tm, d), lambda i: (i, 0)), pl.BlockSpec((d, d), lambda i: (0, 0)), half],
        out_specs=[half, pl.BlockSpec((HEADS, LANE, tm), lambda i: (0, 0, i)),
                   pl.BlockSpec((tm, d - MLA_W), lambda i: (i, 0)), pl.BlockSpec((HEADS, 1, tm), lambda i: (0, 0, i))],
        out_shape=[_sds((s, MLA_W), BF), _sds((HEADS, LANE, s), BF), _sds((s, d - MLA_W), F32), _sds((HEADS, 1, s), F32)],
        compiler_params=_cp(("arbitrary",)),
    )(dr1b, woutt, o)


def conv_bwd_ln(uc, dco, g_ln, b_ln):
    s, c = uc.shape
    tm = _tile(s, 512)

    def body(u_ref, d_ref, g_ref, b_ref, du_ref, dg_ref, db_ref, dcb_ref):
        @pl.when(pl.program_id(0) == 0)
        def _():
            dg_ref[...] = jnp.zeros_like(dg_ref)
            db_ref[...] = jnp.zeros_like(db_ref)
            dcb_ref[...] = jnp.zeros_like(dcb_ref)

        xhat, rstd = _ln_stats(u_ref[...])
        g = g_ref[...]
        cl = xhat * g + b_ref[...]
        sg = _sigmoid(cl)
        dcl = d_ref[...] * (sg * (1.0 + cl * (1.0 - sg)))
        dg_ref[...] += _rows8(dcl * xhat)
        db_ref[...] += _rows8(dcl)
        du = _ln_bwd(dcl, xhat, rstd, g)
        du_ref[...] = du
        dcb_ref[...] += _rows8(du)

    tok = pl.BlockSpec((tm, c), lambda i: (i, 0))
    row = pl.BlockSpec((1, c), lambda i: (0, 0))
    accs = pl.BlockSpec((SUB, c), lambda i: (0, 0))
    return _call(
        body, name="conv_bwd_ln", grid=(s // tm,), in_specs=[tok, tok, row, row], out_specs=[tok, accs, accs, accs],
        out_shape=[_sds((s, c), F32)] + [_sds((SUB, c), F32)] * 3, compiler_params=_cp(("arbitrary",)),
    )(uc, dco, g_ln, b_ln)


def conv_bwd_taps(h, duc, conv_w):
    s, c = duc.shape
    tm = _tile(s, 256)
    rc = _tile(tm, 64)

    def body(a_ref, ap_ref, an_ref, g_ref, gp_ref, gn_ref, d_ref, dp_ref, dn_ref, w_ref, o_ref, dw_ref,
             uslab, dslab, du_s, urot, drot, dw8):
        @pl.when(pl.program_id(0) == 0)
        def _():
            dw8[...] = jnp.zeros_like(dw8)

        sg = _sigmoid(g_ref[...])
        a = a_ref[...]
        _fill_slab(uslab, tm, ap_ref[...] * _sigmoid(gp_ref[...]), a * sg, an_ref[...] * _sigmoid(gn_ref[...]))
        _fill_slab(dslab, tm, dp_ref[...], d_ref[...], dn_ref[...])
        _rotate_slab(uslab, urot, tm)
        _rotate_slab(dslab, drot, tm)

        def lane_block(cb, carry):
            cs = pl.ds(pl.multiple_of(cb * LANE, LANE), LANE)
            for r0 in range(0, tm, rc):
                acc = jnp.zeros((rc, LANE), F32)
                for k in range(CONV_K):
                    acc = acc + w_ref[k:k + 1, cs] * _shifted(dslab, drot, r0 + HALO + CONV_PAD - k, rc, cs)
                du_s[r0:r0 + rc, cs] = acc
            return carry

        def lane_block_taps(cb, carry):
            cs = pl.ds(pl.multiple_of(cb * LANE, LANE), LANE)
            parts = []
            for k in range(CONV_K):
                prod = None
                for r0 in range(0, tm, rc):
                    t = dslab[pl.ds(r0 + HALO, rc), cs] * _shifted(uslab, urot, r0 + HALO - CONV_PAD + k, rc, cs)
                    prod = t if prod is None else prod + t
                parts.append(_rows8(prod))
            rows = SUB * CONV_K
            dw8[0:rows, cs] = dw8[0:rows, cs] + jnp.concatenate(parts, axis=0)
            return carry

        lax.fori_loop(0, c // LANE, lane_block, 0)
        lax.fori_loop(0, c // LANE, lane_block_taps, 0)

        @pl.when(pl.program_id(0) == pl.num_programs(0) - 1)
        def _():
            dw_ref[...] = jnp.zeros_like(dw_ref)
            for k in range(CONV_K):
                dw_ref[k:k + 1, :] = jnp.sum(dw8[SUB * k:SUB * (k + 1), :], axis=0, keepdims=True)

        du = du_s[...]
        o_ref[:, 0:c] = (du * sg).astype(BF)
        o_ref[:, c:2 * c] = (du * a * sg * (1.0 - sg)).astype(BF)

    a_specs = _halo_specs(tm, s, c, 1)
    g_specs = _halo_specs(tm, s, c, 2)
    d_specs = _halo_specs(tm, s, c, 0)
    wsp = pl.BlockSpec(conv_w.shape, lambda i: (0, 0))
    return _call(
        body, name="conv_bwd_taps", grid=(s // tm,), in_specs=[*a_specs, *g_specs, *d_specs, wsp],
        out_specs=[pl.BlockSpec((tm, 2 * c), lambda i: (i, 0)), wsp],
        out_shape=[_sds((s, 2 * c), BF), _sds(conv_w.shape, F32)],
        scratch_shapes=[pltpu.VMEM(_slab_shapes(tm, c)[0], F32), pltpu.VMEM(_slab_shapes(tm, c)[0], F32), pltpu.VMEM((tm, c), F32),
                        pltpu.VMEM(_slab_shapes(tm, c)[1], F32), pltpu.VMEM(_slab_shapes(tm, c)[1], F32),
                        pltpu.VMEM((SUB * conv_w.shape[0], c), F32)],
        compiler_params=_cp(("arbitrary",)),
    )(h, h, h, h, h, h, duc, duc, duc, conv_w)


def attn_bwd(qc, kc, kct, v, dob, dot, lse_r, delta_r):
    _, s, _ = qc.shape
    tk = _tile(s, 1024)
    tq = _tile(s, 512)
    scale = D_QK ** -0.5
    c2 = scale * LOG2E

    def body(k_ref, kt_ref, v_ref, q_ref, do_ref, dot_ref, l_ref, dl_ref, dqt_ref, dk_ref, dvt_ref):
        @pl.when(pl.program_id(1) == 0)
        def _():
            dqt_ref[...] = jnp.zeros_like(dqt_ref)

        k = k_ref[...]
        kt = kt_ref[...]
        vv = v_ref[...]

        def step(i, carry):
            dk, dvt = carry
            off = pl.multiple_of(i * tq, tq)
            q = q_ref[pl.ds(off, tq), :]
            do = do_ref[pl.ds(off, tq), :]
            pt = jnp.exp2(_dot_nt(k, q) * c2 - l_ref[:, pl.ds(off, tq)])
            dvt = dvt + _dot_nt(dot_ref[:, pl.ds(off, tq)], pt.astype(BF))
            dpt = _dot_nt(vv, do)
            dsb = (pt * (dpt - dl_ref[:, pl.ds(off, tq)]) * scale).astype(BF)
            dk = dk + _dot(dsb, q)
            dqt_ref[:, pl.ds(off, tq)] += _dot(kt, dsb)
            return dk, dvt

        dk, dvt = _unrolled_loop(s // tq, 16, step, (jnp.zeros((tk, 2 * LANE), F32), jnp.zeros((LANE, tk), F32)))
        dk_ref[...] = dk
        dvt_ref[...] = dvt

    rowv = pl.BlockSpec((None, 1, s), lambda h, j: (h, 0, 0))
    return _call(
        body, name="attn_bwd", grid=(HEADS, s // tk),
        in_specs=[pl.BlockSpec((None, tk, 2 * LANE), lambda h, j: (h, j, 0)),
                  pl.BlockSpec((None, 2 * LANE, tk), lambda h, j: (h, 0, j)),
                  pl.BlockSpec((None, tk, LANE), lambda h, j: (h, j, 0)),
                  pl.BlockSpec((None, s, 2 * LANE), lambda h, j: (h, 0, 0)),
                  pl.BlockSpec((s, LANE), lambda h, j: (0, h)),
                  pl.BlockSpec((None, LANE, s), lambda h, j: (h, 0, 0)), rowv, rowv],
        out_specs=[pl.BlockSpec((None, 2 * LANE, s), lambda h, j: (h, 0, 0)),
                   pl.BlockSpec((None, tk, 2 * LANE), lambda h, j: (h, j, 0)),
                   pl.BlockSpec((None, LANE, tk), lambda h, j: (h, 0, j))],
        out_shape=[_sds((HEADS, 2 * LANE, s), F32), _sds((HEADS, s, 2 * LANE), F32), _sds((HEADS, LANE, s), F32)],
        compiler_params=_cp(("arbitrary", "arbitrary"), 56),
    )(kc, kct, v, qc, dob, dot, lse_r, delta_r)


def q_bwd(dqt, h, g_cq, wuqt, cos, sin):
    s = h.shape[0]
    tm = _tile(s, 256)

    def body(d_ref, h_ref, g_ref, w_ref, c_ref, s_ref, dq_ref, dc_ref, dg_ref):
        @pl.when(pl.program_id(0) == 0)
        def _():
            dg_ref[...] = jnp.zeros_like(dg_ref)

        c = c_ref[...]
        sn = s_ref[...]
        for hd in range(HEADS):
            t = d_ref[hd].T
            dq_ref[:, LANE * hd:LANE * (hd + 1)] = t[:, 0:LANE].astype(BF)
            dq_ref[:, MLA_W + LANE * hd:MLA_W + LANE * (hd + 1)] = _unrope128(t[:, LANE:2 * LANE], c, sn).astype(BF)
        dy = _dot(dq_ref[...], w_ref[...])
        g = g_ref[...]
        _, xh, rr = _rms_fwd(h_ref[...], g)
        dg_ref[...] += _rows8(dy * xh)
        dc_ref[...] = _rms_bwd(dy, xh, rr, g).astype(BF)

    tab = pl.BlockSpec((tm, LANE), lambda i: (i, 0))
    return _call(
        body, name="q_bwd", grid=(s // tm,),
        in_specs=[pl.BlockSpec((HEADS, 2 * LANE, tm), lambda i: (0, 0, i)), pl.BlockSpec((tm, R_Q), lambda i: (i, 0)),
                  pl.BlockSpec((1, R_Q), lambda i: (0, 0)), pl.BlockSpec((2 * MLA_W, R_Q), lambda i: (0, 0)), tab, tab],
        out_specs=[pl.BlockSpec((tm, 2 * MLA_W), lambda i: (i, 0)), pl.BlockSpec((tm, R_Q), lambda i: (i, 0)),
                   pl.BlockSpec((SUB, R_Q), lambda i: (0, 0))],
        out_shape=[_sds((s, 2 * MLA_W), BF), _sds((s, R_Q), BF), _sds((SUB, R_Q), F32)],
        compiler_params=_cp(("arbitrary",)),
    )(dqt, h, g_cq, wuqt, cos, sin)


def kv_bwd(dk, dv, h, g_ckv, wukt, wuvt, cos, sin):
    s = h.shape[0]
    tm = _tile(s, 256)

    def body(dk_ref, dv_ref, h_ref, g_ref, wk_ref, wv_ref, c_ref, s_ref, dkn_ref, dvb_ref, dc_ref, dkr_ref, dg_ref):
        @pl.when(pl.program_id(0) == 0)
        def _():
            dg_ref[...] = jnp.zeros_like(dg_ref)

        dkr = dk_ref[0, :, LANE:2 * LANE]
        for hd in range(HEADS):
            dkn_ref[:, LANE * hd:LANE * (hd + 1)] = dk_ref[hd, :, 0:LANE].astype(BF)
            dvb_ref[:, LANE * hd:LANE * (hd + 1)] = dv_ref[hd].T.astype(BF)
            if hd > 0:
                dkr = dkr + dk_ref[hd, :, LANE:2 * LANE]
        dkr_ref[...] = _unrope128(dkr, c_ref[...], s_ref[...]).astype(BF)
        dy = _dot(dkn_ref[...], wk_ref[...]) + _dot(dvb_ref[...], wv_ref[...])
        g = g_ref[...]
        _, xh, rr = _rms_fwd(h_ref[...], g)
        dg_ref[...] += _rows8(dy * xh)
        dc_ref[...] = _rms_bwd(dy, xh, rr, g).astype(BF)

    tab = pl.BlockSpec((tm, LANE), lambda i: (i, 0))
    wsp = pl.BlockSpec((MLA_W, R_KV), lambda i: (0, 0))
    wide = pl.BlockSpec((tm, MLA_W), lambda i: (i, 0))
    return _call(
        body, name="kv_bwd", grid=(s // tm,),
        in_specs=[pl.BlockSpec((HEADS, tm, 2 * LANE), lambda i: (0, i, 0)), pl.BlockSpec((HEADS, LANE, tm), lambda i: (0, 0, i)),
                  pl.BlockSpec((tm, R_KV), lambda i: (i, 1)), pl.BlockSpec((1, R_KV), lambda i: (0, 0)), wsp, wsp, tab, tab],
        out_specs=[wide, wide, pl.BlockSpec((tm, R_KV), lambda i: (i, 0)), tab, pl.BlockSpec((SUB, R_KV), lambda i: (0, 0))],
        out_shape=[_sds((s, MLA_W), BF), _sds((s, MLA_W), BF), _sds((s, R_KV), BF), _sds((s, LANE), BF), _sds((SUB, R_KV), F32)],
        compiler_params=_cp(("arbitrary",)),
    )(dk, dv, h, g_ckv, wukt, wuvt, cos, sin)


def in_proj_bwd_ln(dh, wint, dr1, x, g_in):
    s, hc = dh.shape
    d = x.shape[1]
    tm = _tile(s, 256)

    def body(a_ref, w_ref, d1_ref, x_ref, g_ref, gx_ref, dg_ref, db_ref, acc):
        @pl.when(pl.program_id(0) == 0)
        def _():
            dg_ref[...] = jnp.zeros_like(dg_ref)
            db_ref[...] = jnp.zeros_like(db_ref)

        acc[...] = _dot(a_ref[...], w_ref[...])
        g = g_ref[...]

        def chunk(rows):
            dy = ALPHA * d1_ref[rows, :] + acc[rows, :]
            xhat, rstd = _ln_stats(x_ref[rows, :])
            dg_ref[...] += _rows8(dy * xhat)
            db_ref[...] += _rows8(dy)
            gx_ref[rows, :] = _ln_bwd(dy, xhat, rstd, g)

        _row_chunks(tm, chunk)

    tok = pl.BlockSpec((tm, d), lambda i: (i, 0))
    accs = pl.BlockSpec((SUB, d), lambda i: (0, 0))
    return _call(
        body, name="in_proj_bwd_ln", grid=(s // tm,),
        in_specs=[pl.BlockSpec((tm, hc), lambda i: (i, 0)), pl.BlockSpec((hc, d), lambda i: (0, 0)),
                  tok, tok, pl.BlockSpec((1, d), lambda i: (0, 0))],
        out_specs=[tok, accs, accs], out_shape=[_sds((s, d), F32), _sds((SUB, d), F32), _sds((SUB, d), F32)],
        scratch_shapes=[pltpu.VMEM((tm, d), F32)], compiler_params=_cp(("arbitrary",), 56),
    )(dh, wint, dr1, x, g_in)


def _adamw_math(w, g, m, v):
    m = ADAM_B1 * m + (1.0 - ADAM_B1) * g
    v = ADAM_B2 * v + (1.0 - ADAM_B2) * (g * g)
    m_hat = m / (1.0 - ADAM_B1 ** ADAM_STEP)
    v_hat = v / (1.0 - ADAM_B2 ** ADAM_STEP)
    delta = -ADAM_LR * (m_hat / (jnp.sqrt(v_hat) + ADAM_EPS) + ADAM_WD * w)
    return delta, m, v


def adamw(name, w, g, m, v):
    r, c = w.shape
    tr = _row_tile(r, c)

    def body(w_ref, g_ref, m_ref, v_ref, d_ref, mo_ref, vo_ref):
        d_ref[...], mo_ref[...], vo_ref[...] = _adamw_math(w_ref[...], g_ref[...], m_ref[...], v_ref[...])

    blk = pl.BlockSpec((tr, c), lambda i: (i, 0))
    return _call(
        body, name=name, grid=(r // tr,), in_specs=[blk] * 4, out_specs=[blk] * 3,
        out_shape=[_sds((r, c), F32)] * 3, compiler_params=_cp(("arbitrary",)),
    )(w, g, m, v)


def _coords():
    return lax.axis_index("x"), lax.axis_index("y"), lax.axis_index("c")


def _other_chips(x, y):
    return [(1 - x, y, 2 * (1 - x) + y), (x, 1 - y, 2 * x + 1 - y), (1 - x, 1 - y, 2 * (1 - x) + 1 - y)]


ANY = pl.BlockSpec(memory_space=pl.ANY)
HBM = pl.BlockSpec(memory_space=pltpu.HBM)
SEM = pl.BlockSpec(memory_space=pltpu.SEMAPHORE)
EFFECT = pltpu.SideEffectType.DATAFLOW_SIDE_EFFECTING


def _in_hbm(a):
    return pltpu.with_memory_space_constraint(a, pltpu.HBM)


def _split_plan(mode, src, land, x, y, c):
    if mode == "pair":
        rh = src.shape[1] // 2
        return [((x, y, 1 - c), src.at[:, pl.ds((1 - c) * rh, rh)], land, land)]
    me = 2 * x + y
    plan = []
    for j, (px, py, pk) in enumerate(_other_chips(x, y)):
        if mode == "gather":
            plan.append(((px, py, c), src, land.at[me], land.at[pk]))
        elif mode == "gather_half":
            mine = pl.ds(c * (src.shape[0] // 2), src.shape[0] // 2)
            plan.append(((px, py, c), src.at[mine], land.at[me, mine], land.at[pk, mine]))
        else:
            plan.append(((px, py, c), src.at[pk], land.at[j], land.at[j]))
    return plan


def _plan_len(mode):
    return 1 if mode == "pair" else N_CHIP - 1


def split_send_start(name, mode, srcs, land_shapes, order_after):
    n = len(srcs)
    np_ = _plan_len(mode)

    def body(*refs):
        ins, lands = refs[:n], refs[n:2 * n]
        ss, rs = refs[2 * n + 1], refs[2 * n + 2]
        token = refs[-1]
        x, y, c = _coords()
        for a in range(n):
            for j, (peer, src, dst, _) in enumerate(_split_plan(mode, ins[a], lands[a], x, y, c)):
                pltpu.make_async_remote_copy(src_ref=src, dst_ref=dst, send_sem=ss.at[np_ * a + j], recv_sem=rs.at[np_ * a + j],
                                             device_id=peer, device_id_type=MESH).start()
        token[...] = jnp.zeros_like(token)

    lands = [lax.empty(shp, s.dtype) for shp, s in zip(land_shapes, srcs)]
    outs = _call(
        body, name=name,
        out_shape=(pltpu.SemaphoreType.DMA((np_ * n,)), pltpu.SemaphoreType.DMA((np_ * n,)),
                   *[pltpu.HBM(s.shape, s.dtype) for s in srcs], *[pltpu.HBM(l.shape, l.dtype) for l in lands],
                   _sds((SUB, LANE), F32)),
        in_specs=[HBM] * (2 * n) + [ANY], out_specs=(SEM, SEM, *[HBM] * (2 * n), pl.BlockSpec(memory_space=pltpu.VMEM)),
        input_output_aliases={a: 2 + a for a in range(2 * n)},
        compiler_params=pltpu.CompilerParams(has_side_effects=EFFECT),
    )(*[_in_hbm(s) for s in srcs], *[_in_hbm(l) for l in lands], order_after)
    return outs[0], outs[1], list(outs[2:2 + n]), list(outs[2 + n:2 + 2 * n]), outs[-1]


def split_send_wait(name, mode, ss, rs, srcs, lands, order_after):
    n = len(srcs)
    np_ = _plan_len(mode)

    def body(*refs):
        ins, lnd = refs[:n], refs[n:2 * n]
        s_ref, r_ref = refs[2 * n], refs[2 * n + 1]
        x, y, c = _coords()
        for a in range(n):
            for j, (peer, src, _, got) in enumerate(_split_plan(mode, ins[a], lnd[a], x, y, c)):
                cp = pltpu.make_async_remote_copy(src_ref=src, dst_ref=got, send_sem=s_ref.at[np_ * a + j], recv_sem=r_ref.at[np_ * a + j],
                                                  device_id=peer, device_id_type=MESH)
                cp.wait_send()
                cp.wait_recv()

    outs = _call(
        body, name=name, out_shape=tuple(pltpu.HBM(t.shape, t.dtype) for t in (*srcs, *lands)),
        in_specs=[HBM] * (2 * n) + [SEM, SEM, ANY], out_specs=tuple([HBM] * (2 * n)),
        input_output_aliases={a: a for a in range(2 * n)},
        compiler_params=pltpu.CompilerParams(has_side_effects=EFFECT),
    )(*srcs, *lands, ss, rs, order_after)
    return list(outs[:n]), list(outs[n:])


def swap_gathered_halves(lands):
    n = len(lands)

    def body(*refs):
        outs = refs[n:2 * n]
        ss, rs = refs[2 * n:]
        x, y, c = _coords()
        cps = []
        for a in range(n):
            rh = outs[a].shape[1] // 2
            for j, (px, py, pk) in enumerate(_other_chips(x, y)):
                held = outs[a].at[pk, pl.ds(c * rh, rh)]
                cp = pltpu.make_async_remote_copy(src_ref=held, dst_ref=held, send_sem=ss.at[a, j], recv_sem=rs.at[a, j],
                                                  device_id=(x, y, 1 - c), device_id_type=MESH)
                cp.start()
                cps.append(cp)
        for a in range(n):
            rh = outs[a].shape[1] // 2
            for j, (px, py, pk) in enumerate(_other_chips(x, y)):
                theirs = outs[a].at[pk, pl.ds((1 - c) * rh, rh)]
                pltpu.make_async_remote_copy(src_ref=theirs, dst_ref=theirs, send_sem=ss.at[a, j], recv_sem=rs.at[a, j],
                                             device_id=(x, y, 1 - c), device_id_type=MESH).wait_recv()
        for cp in cps:
            cp.wait_send()

    return _call(
        body, name="swap_gathered_halves", in_specs=[ANY] * n, out_specs=[ANY] * n,
        out_shape=[_sds(l.shape, l.dtype) for l in lands], input_output_aliases={a: a for a in range(n)},
        scratch_shapes=[pltpu.SemaphoreType.DMA((n, 3))] * 2,
    )(*lands)


def pair_exchange(grads, tag):
    n = len(grads)

    def body(*refs):
        ins, outs = refs[:n], refs[n:2 * n]
        ss, rs = refs[2 * n:]
        x, y, c = _coords()
        cps = []
        for a in range(n):
            rh = ins[a].shape[1] // 2
            cp = pltpu.make_async_remote_copy(
                src_ref=ins[a].at[:, pl.ds((1 - c) * rh, rh)], dst_ref=outs[a], send_sem=ss.at[a], recv_sem=rs.at[a],
                device_id=(x, y, 1 - c), device_id_type=MESH)
            cp.start()
            cps.append(cp)
        for cp in cps:
            cp.wait()

    return _call(
        body, name="pair_exchange_" + tag, in_specs=[ANY] * n, out_specs=[ANY] * n,
        out_shape=[_sds((N_CHIP, g.shape[1] // 2, g.shape[2]), F32) for g in grads],
        scratch_shapes=[pltpu.SemaphoreType.DMA((n,))] * 2,
    )(*grads)


def _row_tile(rows, cols, itemsize=4, budget=2 * VMEM_MB):
    fits = [t for t in range(SUB, rows + 1, SUB) if rows % t == 0 and t * cols * itemsize <= budget]
    return max(fits) if fits and rows * cols * itemsize > budget else rows


def pair_add(g, r, cidx):
    _, rows, cols = g.shape
    rh = rows // 2
    tr = _row_tile(rh, cols)
    per = rh // tr

    def body(c_ref, g_ref, r_ref, o_ref):
        o_ref[...] = g_ref[...] + r_ref[...]

    return _call(
        body, name="pair_add",
        grid_spec=pltpu.PrefetchScalarGridSpec(
            num_scalar_prefetch=1, grid=(N_CHIP, per),
            in_specs=[pl.BlockSpec((None, tr, cols), lambda k, i, c: (k, c[0] * per + i, 0)),
                      pl.BlockSpec((None, tr, cols), lambda k, i, c: (k, i, 0))],
            out_specs=pl.BlockSpec((None, tr, cols), lambda k, i, c: (k, i, 0))),
        out_shape=_sds((N_CHIP, rh, cols), F32), compiler_params=_cp(("arbitrary", "arbitrary")),
    )(cidx, g, r)


def chip_add(p, r, kc):
    _, rh, cols = p.shape
    tr = _row_tile(rh, cols)
    per = rh // tr

    def body(k_ref, p_ref, r_ref, o_ref):
        o_ref[...] = ((p_ref[...] + r_ref[0]) + r_ref[1]) + r_ref[2]

    return _call(
        body, name="chip_add",
        grid_spec=pltpu.PrefetchScalarGridSpec(
            num_scalar_prefetch=1, grid=(per,),
            in_specs=[pl.BlockSpec((None, tr, cols), lambda i, k: (k[0], i, 0)),
                      pl.BlockSpec((N_CHIP - 1, tr, cols), lambda i, k: (0, i, 0))],
            out_specs=pl.BlockSpec((tr, cols), lambda i, k: (k[1] * per + i, 0))),
        out_shape=_sds((2 * rh, cols), F32), compiler_params=_cp(("arbitrary",)),
    )(kc, p, r)


def pair_share(fulls, tag):
    n = len(fulls)

    def body(*refs):
        outs = refs[n:2 * n]
        ss, rs = refs[2 * n:]
        x, y, c = _coords()
        cps = []
        for a in range(n):
            rh = outs[a].shape[0] // 2
            mine = outs[a].at[pl.ds(c * rh, rh)]
            cp = pltpu.make_async_remote_copy(
                src_ref=mine, dst_ref=mine, send_sem=ss.at[a], recv_sem=rs.at[a],
                device_id=(x, y, 1 - c), device_id_type=MESH)
            cp.start()
            cps.append(cp)
        for a, cp in enumerate(cps):
            rh = outs[a].shape[0] // 2
            theirs = outs[a].at[pl.ds((1 - c) * rh, rh)]
            cp.wait_send()
            pltpu.make_async_remote_copy(
                src_ref=theirs, dst_ref=theirs, send_sem=ss.at[a], recv_sem=rs.at[a],
                device_id=(x, y, 1 - c), device_id_type=MESH).wait_recv()

    return _call(
        body, name="pair_share_" + tag, in_specs=[ANY] * n, out_specs=[ANY] * n,
        out_shape=[_sds(f.shape, F32) for f in fulls], input_output_aliases={a: a for a in range(n)},
        scratch_shapes=[pltpu.SemaphoreType.DMA((n,))] * 2,
    )(*fulls)


def small_allreduce_adamw(part, w, m, v):
    n = part.shape[1]

    def body(p_ref, w_ref, m_ref, v_ref, g_ref, d_ref, mo_ref, vo_ref, mine, gath, ss, rs):
        x, y, c = _coords()
        me = 4 * x + 2 * y + c
        mine[...] = jnp.sum(p_ref[...], axis=0, keepdims=True)
        gath[me] = mine[...]
        cps = []
        for k in range(1, 8):
            px, py, pc = x ^ (k >> 2), y ^ ((k >> 1) & 1), c ^ (k & 1)
            cp = pltpu.make_async_remote_copy(
                src_ref=mine, dst_ref=gath.at[me], send_sem=ss.at[k - 1], recv_sem=rs.at[k - 1],
                device_id=(px, py, pc), device_id_type=MESH)
            cp.start()
            cps.append(cp)
        for k in range(1, 8):
            src = 4 * (x ^ (k >> 2)) + 2 * (y ^ ((k >> 1) & 1)) + (c ^ (k & 1))
            pltpu.make_async_remote_copy(
                src_ref=mine, dst_ref=gath.at[src], send_sem=ss.at[k - 1], recv_sem=rs.at[k - 1],
                device_id=(x, y, c), device_id_type=MESH).wait_recv()
        for cp in cps:
            cp.wait_send()
        g = gath[0]
        for dv in range(1, 8):
            g = g + gath[dv]
        g_ref[...] = g
        d_ref[...], mo_ref[...], vo_ref[...] = _adamw_math(w_ref[...], g, m_ref[...], v_ref[...])

    vm = pl.BlockSpec(memory_space=pltpu.VMEM)
    return _call(
        body, name="small_allreduce_adamw", in_specs=[vm] * 4, out_specs=[vm] * 4, out_shape=[_sds((1, n), F32)] * 4,
        scratch_shapes=[pltpu.VMEM((1, n), F32), pltpu.VMEM((8, 1, n), F32),
                        pltpu.SemaphoreType.DMA((7,)), pltpu.SemaphoreType.DMA((7,))],
    )(part, w, m, v)


def _unshard_cols(g):
    k, r, cs = g.shape
    return g.transpose(1, 0, 2).reshape(r, k * cs)


def _shard_cols(w):
    r, c = w.shape
    return w.reshape(r, N_CHIP, c // N_CHIP).transpose(1, 0, 2)


def local_step(x, positions, ln_in_g, ln_in_b, g_cq, g_ckv, conv_b, g_conv_ln, b_conv_ln, g_ln1, b_ln1, g_ln2, b_ln2,
               target, start_token, hooks):
    s, d = x.shape
    c = d - MLA_W
    row = lambda a: a.reshape(1, -1)
    ln_in_g = row(ln_in_g) + start_token[0:1, 0:1]

    half = D_ROPE // 2
    inv_freq = ROPE_BASE ** (-jnp.arange(half, dtype=F32) * (2.0 / D_ROPE))
    invf = jnp.concatenate([inv_freq, inv_freq, jnp.zeros((LANE - D_ROPE,), F32)]).reshape(1, LANE)
    cos, sin = rope_tables(positions.astype(F32).reshape(s, 1), invf)
    x0, x0b = ln_in_fwd(x, ln_in_g, row(ln_in_b))
    win_g, wuq_g, wuk_g, wuv_g, convw_g = hooks.early_weights(x0b)

    o_kr = R_Q + R_KV
    o_cv = o_kr + D_ROPE
    n_in = o_cv + 2 * c
    per = n_in // N_CHIP

    def in_cols(a, b):
        return [win_g[k, max(a, per * k) - per * k:min(b, per * (k + 1)) - per * k]
                for k in range(N_CHIP) if max(a, per * k) < min(b, per * (k + 1))]

    win_rt = jnp.concatenate(in_cols(0, o_kr) + in_cols(o_cv, n_in) + in_cols(o_kr, o_cv)
                             + [jnp.zeros((LANE - D_ROPE, d), BF)], axis=0)
    kr_blk = (o_kr + 2 * c) // LANE
    wuq = _unshard_cols(wuq_g).reshape(R_Q, HEADS, D_QK)
    wuq_r = jnp.concatenate([wuq[:, :, :D_NOPE].reshape(R_Q, MLA_W),
                             jnp.pad(wuq[:, :, D_NOPE:], ((0, 0), (0, 0), (0, LANE - D_ROPE))).reshape(R_Q, MLA_W)], axis=1)
    wuk = _unshard_cols(wuk_g)
    wuv = _unshard_cols(wuv_g)
    conv_w = jnp.pad(_unshard_cols(convw_g), ((0, 1), (0, 0)))

    h = matmul_nt("in_proj", x0b, win_rt, 256)
    qc, cqn = q_proj(h, g_cq, wuq_r, cos, sin)
    kc, kct, v, ckvn = kv_proj(h, g_ckv, wuk, wuv, cos, sin, kr_blk)
    o, ob, lse = attn_fwd(qc, kc, v)
    co, uc = conv_fwd(h, conv_w, conv_b, g_conv_ln, b_conv_ln)
    wout_g, wff1_g, wff2_g = hooks.late_weights(ob)
    wout = wout_g.reshape(d, d)
    wff2 = wff2_g.reshape(-1, d)
    r1, x1, x1b = out_proj_ln1(ob, co, wout, x0, g_ln1, b_ln1)
    rb, a1b = ff1_fwd(x1b, wff1_g)
    dr2, dr2b, loss8, dg2, db2 = ff2_ln2_loss(a1b, wff2, x1, target, g_ln2, b_ln2)

    df1b = ff2_bwd_act(dr2b, wff2, rb)
    gw_ff2 = wgrad("wgrad_ff2", a1b, dr2b, 1024, 1024).reshape(N_CHIP, -1, d)
    gw_ff1 = wgrad("wgrad_ff1", x1b, df1b, 1024, 1024, shards=N_CHIP)
    tok = hooks.ff_grads(gw_ff2, gw_ff1)
    dr1, dr1b, dg1, db1 = ff1_bwd_ln1(df1b, wff1_g, dr2, r1, g_ln1 + tok[0:1, 0:1])
    tok = hooks.ff_grads_mid(dr1b)
    gw_out = jnp.concatenate([wgrad("wgrad_out_attn", ob, dr1b, 1024, 1024)[0],
                              wgrad("wgrad_out_conv", co, dr1b, 1024, 1024)[0]], axis=0).reshape(N_CHIP, -1, d)
    dob, dot, dco, delta = out_proj_bwd(dr1b, wout.T, o)
    duc, dgc, dbc, dcb = conv_bwd_ln(uc, dco, g_conv_ln + tok[0:1, 0:1], b_conv_ln)
    dconv, gconvw = conv_bwd_taps(h, duc, conv_w)
    dqt, dk, dv = attn_bwd(qc, kc, kct, v, dob, dot, lse, delta)
    dqb, dcq, dgq = q_bwd(dqt, h, g_cq, wuq_r.T, cos, sin)
    dknb, dvb, dckv, dkr, dgkv = kv_bwd(dk, dv, h, g_ckv, wuk.T, wuv.T, cos, sin)
    gwuq_r = wgrad("wgrad_uq", cqn, dqb, 512, 1024)[0]
    gw_uk = wgrad("wgrad_uk", ckvn, dknb, 512, 1024, shards=N_CHIP)
    gw_uv = wgrad("wgrad_uv", ckvn, dvb, 512, 1024, shards=N_CHIP)
    dh = jnp.concatenate([dcq, dckv, dconv, dkr], axis=1)
    gwin_rt = wgrad("wgrad_in", dh, x0b, 640, 1024)[0]

    gwin_t = jnp.concatenate([gwin_rt[:o_kr], gwin_rt[o_kr + 2 * c:o_kr + 2 * c + D_ROPE], gwin_rt[o_kr:o_kr + 2 * c]], axis=0)
    gwin_t = jnp.pad(gwin_t.reshape(N_CHIP, per, d), ((0, 0), (0, win_g.shape[1] - per), (0, 0)))
    gwuq = jnp.concatenate([gwuq_r[:, :MLA_W].reshape(R_Q, HEADS, D_NOPE),
                            gwuq_r[:, MLA_W:].reshape(R_Q, HEADS, LANE)[:, :, :D_ROPE]], axis=2).reshape(R_Q, HEADS * D_QK)
    tok = hooks.rest_grads(dict(w_in=gwin_t, w_uq=_shard_cols(gwuq), w_uk=gw_uk, w_uv=gw_uv,
                                conv_w=_shard_cols(gconvw), w_out=gw_out))
    gx, dgin, dbin = in_proj_bwd_ln(dh, win_rt, dr1, x, ln_in_g + tok[0:1, 0:1])
    small = jnp.concatenate([dgin, dbin, dgq, dgkv, dcb, dgc, dbc, dg1, db1, dg2, db2, loss8], axis=1)
    return gx, small


BIG = ["w_in", "w_uq", "w_uk", "w_uv", "conv_w", "w_out", "w_ff1", "w_ff2"]
EARLY = ["w_in", "w_uq", "w_uk", "w_uv", "conv_w"]
LATE = ["w_out", "w_ff1", "w_ff2"]
SMALL = ["ln_in_g", "ln_in_b", "g_cq", "g_ckv", "conv_b", "g_conv_ln", "b_conv_ln", "g_ln1", "b_ln1", "g_ln2", "b_ln2"]
WEIGHTS = ["ln_in_g", "ln_in_b", "w_in", "g_cq", "w_uq", "g_ckv", "w_uk", "w_uv", "conv_w", "conv_b", "g_conv_ln",
           "b_conv_ln", "w_out", "g_ln1", "b_ln1", "w_ff1", "w_ff2", "g_ln2", "b_ln2"]


def _pad_rows(a, rows):
    return jnp.pad(a, ((0, rows - a.shape[0]), (0, 0)))


def kernel(x, positions, ln_in_g, ln_in_b, w_in, g_cq, w_uq, g_ckv, w_uk, w_uv, conv_w, conv_b, g_conv_ln, b_conv_ln, w_out, g_ln1, b_ln1, w_ff1, w_ff2, g_ln2, b_ln2, loss_target, m_ln_in_g, m_ln_in_b, m_w_in, m_g_cq, m_w_uq, m_g_ckv, m_w_uk, m_w_uv, m_conv_w, m_conv_b, m_g_conv_ln, m_b_conv_ln, m_w_out, m_g_ln1, m_b_ln1, m_w_ff1, m_w_ff2, m_g_ln2, m_b_ln2, v_ln_in_g, v_ln_in_b, v_w_in, v_g_cq, v_w_uq, v_g_ckv, v_w_uk, v_w_uv, v_conv_w, v_conv_b, v_g_conv_ln, v_b_conv_ln, v_w_out, v_g_ln1, v_b_ln1, v_w_ff1, v_w_ff2, v_g_ln2, v_b_ln2):
    w = dict(ln_in_g=ln_in_g, ln_in_b=ln_in_b, w_in=w_in, g_cq=g_cq, w_uq=w_uq, g_ckv=g_ckv, w_uk=w_uk, w_uv=w_uv,
             conv_w=conv_w, conv_b=conv_b, g_conv_ln=g_conv_ln, b_conv_ln=b_conv_ln, w_out=w_out, g_ln1=g_ln1,
             b_ln1=b_ln1, w_ff1=w_ff1, w_ff2=w_ff2, g_ln2=g_ln2, b_ln2=b_ln2)
    m = dict(ln_in_g=m_ln_in_g, ln_in_b=m_ln_in_b, w_in=m_w_in, g_cq=m_g_cq, w_uq=m_w_uq, g_ckv=m_g_ckv, w_uk=m_w_uk,
             w_uv=m_w_uv, conv_w=m_conv_w, conv_b=m_conv_b, g_conv_ln=m_g_conv_ln, b_conv_ln=m_b_conv_ln, w_out=m_w_out,
             g_ln1=m_g_ln1, b_ln1=m_b_ln1, w_ff1=m_w_ff1, w_ff2=m_w_ff2, g_ln2=m_g_ln2, b_ln2=m_b_ln2)
    v = dict(ln_in_g=v_ln_in_g, ln_in_b=v_ln_in_b, w_in=v_w_in, g_cq=v_g_cq, w_uq=v_w_uq, g_ckv=v_g_ckv, w_uk=v_w_uk,
             w_uv=v_w_uv, conv_w=v_conv_w, conv_b=v_conv_b, g_conv_ln=v_g_conv_ln, b_conv_ln=v_b_conv_ln, w_out=v_w_out,
             g_ln1=v_g_ln1, b_ln1=v_b_ln1, w_ff1=v_w_ff1, w_ff2=v_w_ff2, g_ln2=v_g_ln2, b_ln2=v_b_ln2)

    as2d = lambda t, n: t[n][0].T if n == "w_in" else t[n][0]
    sh2 = {n: as2d(w, n) for n in BIG}
    cidx = lax.axis_index("c").astype(jnp.int32).reshape(1)
    me = 2 * lax.axis_index("x") + lax.axis_index("y")
    kc = jnp.stack([me, lax.axis_index("c")]).astype(jnp.int32)

    pad_to = {"conv_w": CONV_K + 1, "w_in": -(-sh2["w_in"].shape[0] // (4 * SUB)) * (4 * SUB)}
    early = [_pad_rows(sh2[n] if n == "conv_w" else sh2[n].astype(BF), pad_to.get(n, sh2[n].shape[0])) for n in EARLY]
    eg = split_send_start("early_weights_start", "gather_half", early, [(N_CHIP,) + a.shape for a in early], ln_in_g)
    late = [sh2[n].astype(BF) for n in LATE]
    ag = split_send_start("late_weights_start", "gather", late, [(N_CHIP,) + a.shape for a in late], eg[4])
    rest = [n for n in BIG if n not in ("w_ff2", "w_ff1")]
    flight = {}

    class Hooks:
        @staticmethod
        def early_weights(after):
            mine, lands = split_send_wait("early_weights_wait", "gather_half", *eg[:4], after)
            full = [lax.dynamic_update_slice(g, a[None], (me, 0, 0)) for g, a in zip(swap_gathered_halves(lands), mine)]
            return [g[:, :CONV_K] if n == "conv_w" else g for n, g in zip(EARLY, full)]

        @staticmethod
        def late_weights(after):
            mine, lands = split_send_wait("late_weights_wait", "gather", *ag[:4], after)
            return [lax.dynamic_update_slice(g, a[None], (me, 0, 0)) for g, a in zip(lands, mine)]

        @staticmethod
        def ff_grads(gw_ff2, gw_ff1):
            full = [gw_ff2, gw_ff1]
            st = split_send_start("ff_pair_start", "pair", full, [(N_CHIP, g.shape[1] // 2, g.shape[2]) for g in full], ag[4])
            flight["ff_pair"] = st[:4]
            flight["token"] = st[4]
            return st[4]

        @staticmethod
        def ff_grads_mid(after):
            full, recv = split_send_wait("ff_pair_wait", "pair", *flight["ff_pair"], after)
            psum = [pair_add(g, r, cidx) for g, r in zip(full, recv)]
            st = split_send_start("ff_grads_start", "scatter", psum, [(N_CHIP - 1,) + p.shape[1:] for p in psum], flight["token"])
            flight["ff"] = st[:4]
            flight["token"] = st[4]
            return st[4]

        @staticmethod
        def rest_grads(big):
            full = [big[n] for n in rest]
            psum = [pair_add(g, r, cidx) for g, r in zip(full, pair_exchange(full, "rest"))]
            st = split_send_start("rest_grads_start", "scatter", psum, [(N_CHIP - 1,) + p.shape[1:] for p in psum], flight["token"])
            flight["rest"] = st[:4]
            return st[4]

    gx, small = local_step(x[0], positions[0], ln_in_g, ln_in_b, g_cq, g_ckv, conv_b, g_conv_ln, b_conv_ln, g_ln1, b_ln1,
                           g_ln2, b_ln2, loss_target[0], ag[4], Hooks)

    ff_psum, ff_got = split_send_wait("ff_grads_wait", "scatter", *flight["ff"], gx)
    rest_psum, rest_got = split_send_wait("rest_grads_wait", "scatter", *flight["rest"], gx)
    summed = [chip_add(p, r, kc) for p, r in zip(rest_psum + ff_psum, rest_got + ff_got)]
    gsh = dict(zip(rest + ["w_ff2", "w_ff1"], pair_share(summed, "all")))
    for n in pad_to:
        gsh[n] = gsh[n][:sh2[n].shape[0]]

    grad, delta, new_m, new_v = {}, {}, {}, {}
    for n in BIG:
        back = (lambda a: a.T[None]) if n == "w_in" else (lambda a: a[None])
        d_, m_, v_ = adamw("adamw_" + n, sh2[n], gsh[n], as2d(m, n), as2d(v, n))
        grad[n], delta[n], new_m[n], new_v[n] = back(gsh[n]), back(d_), back(m_), back(v_)

    flat = lambda t: jnp.concatenate([t[n].reshape(1, -1) for n in SMALL] + [jnp.zeros((1, LANE), F32)], axis=1)
    g_s, d_s, m_s, v_s = small_allreduce_adamw(small, flat(w), flat(m), flat(v))
    off = 0
    for n in SMALL:
        sz = w[n].size
        for dst, src in ((grad, g_s), (delta, d_s), (new_m, m_s), (new_v, v_s)):
            dst[n] = src[0, off:off + sz].reshape(w[n].shape)
        off += sz
    loss = jnp.sum(g_s[0, off:off + LANE])

    return (loss, gx[None], *[grad[n] for n in WEIGHTS], *[delta[n] for n in WEIGHTS],
            *[new_m[n] for n in WEIGHTS], *[new_v[n] for n in WEIGHTS])
```

```python
import jax
import jax.numpy as jnp
from jax import lax
from jax.experimental import pallas as pl
from jax.experimental.pallas import tpu as pltpu

F32 = jnp.float32
BF = jnp.bfloat16

HEADS = 8
D_NOPE = 128
D_ROPE = 64
D_V = 128
D_QK = D_NOPE + D_ROPE
R_Q = 512
R_KV = 512
MLA_W = HEADS * D_V
CONV_K = 31
CONV_PAD = CONV_K // 2
ROPE_BASE = 10000.0
LOG2E = 1.4426950408889634
LN2 = 0.6931471805599453
LN_EPS = 1e-5
RMS_EPS = 1e-6
ALPHA = (2.0 * 1) ** 0.25
ADAM_LR = 0.001
ADAM_B1 = 0.9
ADAM_B2 = 0.999
ADAM_EPS = 1e-08
ADAM_WD = 0.01
ADAM_STEP = 10

LANE = 128
SUB = 8
HALO = 16
N_CHIP = 4
MESH = pl.DeviceIdType.MESH
VMEM_MB = 1024 * 1024


def _call(body, **kw):
    return pl.pallas_call(body, **kw)


def _cp(sem, mb=48):
    return pltpu.CompilerParams(dimension_semantics=sem, vmem_limit_bytes=mb * VMEM_MB)


def _sds(shape, dt):
    return jax.ShapeDtypeStruct(shape, dt)


def _dot(a, b):
    return jnp.dot(a, b, preferred_element_type=F32)


def _dot_nt(a, b):
    return lax.dot_general(a, b, (((1,), (1,)), ((), ())), preferred_element_type=F32)


def _dot_tn(a, b):
    return lax.dot_general(a, b, (((0,), (0,)), ((), ())), preferred_element_type=F32)


def _rows8(v):
    t, n = v.shape
    return v.reshape(t // SUB, SUB, n).sum(axis=0)


def _ln_stats(r):
    mu = jnp.mean(r, axis=-1, keepdims=True)
    xc = r - mu
    var = jnp.mean(xc * xc, axis=-1, keepdims=True)
    rstd = lax.rsqrt(var + LN_EPS)
    return xc * rstd, rstd


def _ln_bwd(dy, xhat, rstd, g):
    dyh = dy * g
    m1 = jnp.mean(dyh, axis=-1, keepdims=True)
    m2 = jnp.mean(dyh * xhat, axis=-1, keepdims=True)
    return rstd * (dyh - m1 - xhat * m2)


def _rms_fwd(x, g):
    rr = lax.rsqrt(jnp.mean(x * x, axis=-1, keepdims=True) + RMS_EPS)
    xh = x * rr
    return xh * g, xh, rr


def _rms_bwd(dy, xh, rr, g):
    dyg = dy * g
    return rr * (dyg - xh * jnp.mean(dyg * xh, axis=-1, keepdims=True))


def _rope128(x, cos, sin_signed):
    lane = lax.broadcasted_iota(jnp.int32, x.shape, 1)
    rot = jnp.where(lane < D_ROPE // 2, pltpu.roll(x, LANE - D_ROPE // 2, 1), pltpu.roll(x, D_ROPE // 2, 1))
    return x * cos + rot * sin_signed


def _unrope128(dy, cos, sin_signed):
    t = dy * sin_signed
    lane = lax.broadcasted_iota(jnp.int32, dy.shape, 1)
    rot = jnp.where(lane < D_ROPE // 2, pltpu.roll(t, LANE - D_ROPE // 2, 1), pltpu.roll(t, D_ROPE // 2, 1))
    return dy * cos + rot


def _as_row(col):
    return jnp.transpose(jnp.broadcast_to(col, (col.shape[0], LANE)))[0:1, :]


def _sigmoid(x):
    return 1.0 / (1.0 + jnp.exp(-x))


def _row_chunks(tm, fn, rc=128):
    rc = min(rc, tm)

    def step(ci, carry):
        fn(pl.ds(pl.multiple_of(ci * rc, rc), rc))
        return carry

    lax.fori_loop(0, tm // rc, step, 0)


def _unrolled_loop(n, unroll, fn, init):
    unroll = min(n, unroll)
    assert n % unroll == 0

    def body(t, carry):
        for u in range(unroll):
            carry = fn(t * unroll + u, carry)
        return carry

    return lax.fori_loop(0, n // unroll, body, init)


def _tile(s, want):
    t = min(s, want)
    assert s % t == 0
    return t


def rope_tables(pos_f, invf):
    s = pos_f.shape[0]
    tm = _tile(s, 1024)

    def body(p_ref, f_ref, c_ref, s_ref):
        ang = p_ref[...] * f_ref[...]
        lane = lax.broadcasted_iota(jnp.int32, ang.shape, 1)
        c = jnp.cos(ang)
        sn = jnp.sin(ang)
        c_ref[...] = jnp.where(lane < D_ROPE, c, 0.0)
        s_ref[...] = jnp.where(lane < D_ROPE // 2, -sn, jnp.where(lane < D_ROPE, sn, 0.0))

    return _call(
        body, name="rope_tables", grid=(s // tm,),
        in_specs=[pl.BlockSpec((tm, 1), lambda i: (i, 0)), pl.BlockSpec((1, LANE), lambda i: (0, 0))],
        out_specs=[pl.BlockSpec((tm, LANE), lambda i: (i, 0))] * 2,
        out_shape=[_sds((s, LANE), F32)] * 2,
        compiler_params=_cp(("arbitrary",)),
    )(pos_f, invf)


def ln_in_fwd(x, g, b):
    s, d = x.shape
    tm = _tile(s, 512)

    def body(x_ref, g_ref, b_ref, o_ref, ob_ref):
        xhat, _ = _ln_stats(x_ref[...])
        y = xhat * g_ref[...] + b_ref[...]
        o_ref[...] = y
        ob_ref[...] = y.astype(BF)

    row = pl.BlockSpec((1, d), lambda i: (0, 0))
    tok = pl.BlockSpec((tm, d), lambda i: (i, 0))
    return _call(
        body, name="ln_in_fwd", grid=(s // tm,), in_specs=[tok, row, row], out_specs=[tok, tok],
        out_shape=[_sds((s, d), F32), _sds((s, d), BF)], compiler_params=_cp(("arbitrary",)),
    )(x, g, b)


def matmul_nt(name, a, wt, tm, out_dtype=F32):
    s, k = a.shape
    n = wt.shape[0]
    tm = _tile(s, tm)

    def body(a_ref, w_ref, o_ref):
        o_ref[...] = _dot_nt(a_ref[...], w_ref[...]).astype(o_ref.dtype)

    return _call(
        body, name=name, grid=(s // tm,),
        in_specs=[pl.BlockSpec((tm, k), lambda i: (i, 0)), pl.BlockSpec((n, k), lambda i: (0, 0))],
        out_specs=pl.BlockSpec((tm, n), lambda i: (i, 0)),
        out_shape=_sds((s, n), out_dtype), compiler_params=_cp(("arbitrary",)),
    )(a, wt)


def q_proj(h, g_cq, wuq, cos, sin):
    s = h.shape[0]
    tm = _tile(s, 512)

    def body(h_ref, g_ref, w_ref, c_ref, s_ref, q_ref, n_ref):
        y, _, _ = _rms_fwd(h_ref[...], g_ref[...])
        yb = y.astype(BF)
        n_ref[...] = yb
        q = _dot(yb, w_ref[...])
        c = c_ref[...]
        sn = s_ref[...]
        for hd in range(HEADS):
            q_ref[hd, :, 0:LANE] = q[:, LANE * hd:LANE * (hd + 1)].astype(BF)
            qr = q[:, MLA_W + LANE * hd:MLA_W + LANE * (hd + 1)]
            q_ref[hd, :, LANE:2 * LANE] = _rope128(qr, c, sn).astype(BF)

    return _call(
        body, name="q_proj", grid=(s // tm,),
        in_specs=[pl.BlockSpec((tm, R_Q), lambda i: (i, 0)), pl.BlockSpec((1, R_Q), lambda i: (0, 0)),
                  pl.BlockSpec((R_Q, 2 * MLA_W), lambda i: (0, 0)),
                  pl.BlockSpec((tm, LANE), lambda i: (i, 0)), pl.BlockSpec((tm, LANE), lambda i: (i, 0))],
        out_specs=[pl.BlockSpec((HEADS, tm, 2 * LANE), lambda i: (0, i, 0)), pl.BlockSpec((tm, R_Q), lambda i: (i, 0))],
        out_shape=[_sds((HEADS, s, 2 * LANE), BF), _sds((s, R_Q), BF)], compiler_params=_cp(("arbitrary",)),
    )(h, g_cq, wuq, cos, sin)


def kv_proj(h, g_ckv, wuk, wuv, cos, sin, kr_blk):
    s = h.shape[0]
    tm = _tile(s, 512)

    def body(h_ref, kr_ref, g_ref, wk_ref, wv_ref, c_ref, s_ref, k_ref, kt_ref, v_ref, n_ref):
        y, _, _ = _rms_fwd(h_ref[...], g_ref[...])
        yb = y.astype(BF)
        n_ref[...] = yb
        kn = _dot(yb, wk_ref[...])
        v = _dot(yb, wv_ref[...])
        kr = _rope128(kr_ref[...], c_ref[...], s_ref[...])
        krb = kr.astype(BF)
        krt = kr.T.astype(BF)
        for hd in range(HEADS):
            knh = kn[:, LANE * hd:LANE * (hd + 1)]
            k_ref[hd, :, 0:LANE] = knh.astype(BF)
            k_ref[hd, :, LANE:2 * LANE] = krb
            kt_ref[hd, 0:LANE, :] = knh.T.astype(BF)
            kt_ref[hd, LANE:2 * LANE, :] = krt
            v_ref[hd] = v[:, LANE * hd:LANE * (hd + 1)].astype(BF)

    tab = pl.BlockSpec((tm, LANE), lambda i: (i, 0))
    wsp = pl.BlockSpec((R_KV, MLA_W), lambda i: (0, 0))
    return _call(
        body, name="kv_proj", grid=(s // tm,),
        in_specs=[pl.BlockSpec((tm, R_KV), lambda i: (i, 1)), pl.BlockSpec((tm, LANE), lambda i: (i, kr_blk)),
                  pl.BlockSpec((1, R_KV), lambda i: (0, 0)), wsp, wsp, tab, tab],
        out_specs=[pl.BlockSpec((HEADS, tm, 2 * LANE), lambda i: (0, i, 0)), pl.BlockSpec((HEADS, 2 * LANE, tm), lambda i: (0, 0, i)),
                   pl.BlockSpec((HEADS, tm, LANE), lambda i: (0, i, 0)), pl.BlockSpec((tm, R_KV), lambda i: (i, 0))],
        out_shape=[_sds((HEADS, s, 2 * LANE), BF), _sds((HEADS, 2 * LANE, s), BF), _sds((HEADS, s, LANE), BF), _sds((s, R_KV), BF)],
        compiler_params=_cp(("arbitrary",)),
    )(h, h, g_ckv, wuk, wuv, cos, sin)


def attn_fwd(qc, kc, v):
    _, s, _ = qc.shape
    tq = _tile(s, 512)
    tk = _tile(s, 512)
    scale = D_QK ** -0.5
    c2 = scale * LOG2E
    nk = s // tk
    nb = tk // LANE
    un = 8

    def body(q_ref, k_ref, v_ref, o_ref, ob_ref, l_ref, s_scr, m_scr):
        q = q_ref[...]

        def scores(j, mpart):
            off = pl.multiple_of(j * tk, tk)
            sc = _dot_nt(q, k_ref[pl.ds(off, tk), :]) * c2
            s_scr[:, pl.ds(off, tk)] = sc
            for b in range(nb):
                mpart = jnp.maximum(mpart, sc[:, LANE * b:LANE * (b + 1)])
            return mpart

        mpart = _unrolled_loop(nk, un, scores, jnp.full((tq, LANE), -jnp.inf, F32))
        m = jnp.max(mpart, axis=-1, keepdims=True)
        m_scr[...] = jnp.broadcast_to(m, (tq, LANE))

        def weigh(j, carry):
            lpart, acc = carry
            off = pl.multiple_of(j * tk, tk)
            ps = []
            for b in range(nb):
                p = jnp.exp2(s_scr[:, pl.ds(off + LANE * b, LANE)] - m_scr[...])
                lpart = lpart + p
                ps.append(p.astype(BF))
            acc = acc + _dot(jnp.concatenate(ps, axis=1), v_ref[pl.ds(off, tk), :])
            return lpart, acc

        lpart, acc = _unrolled_loop(nk, un, weigh, (jnp.zeros((tq, LANE), F32), jnp.zeros((tq, D_V), F32)))
        l = jnp.sum(lpart, axis=-1, keepdims=True)
        o = acc / l
        o_ref[...] = o
        ob_ref[...] = o.astype(BF)
        l_ref[...] = _as_row(m + jnp.log(l) * LOG2E)

    return _call(
        body, name="attn_fwd", grid=(HEADS, s // tq),
        in_specs=[pl.BlockSpec((None, tq, 2 * LANE), lambda h, i: (h, i, 0)),
                  pl.BlockSpec((None, s, 2 * LANE), lambda h, i: (h, 0, 0)),
                  pl.BlockSpec((None, s, LANE), lambda h, i: (h, 0, 0))],
        out_specs=[pl.BlockSpec((tq, LANE), lambda h, i: (i, h)), pl.BlockSpec((tq, LANE), lambda h, i: (i, h)),
                   pl.BlockSpec((None, 1, tq), lambda h, i: (h, 0, i))],
        out_shape=[_sds((s, MLA_W), F32), _sds((s, MLA_W), BF), _sds((HEADS, 1, s), F32)],
        scratch_shapes=[pltpu.VMEM((tq, s + LANE), F32), pltpu.VMEM((tq, LANE), F32)],
        compiler_params=_cp(("arbitrary", "arbitrary")),
    )(qc, kc, v)


def _halo_specs(tm, s, width, col):
    r = tm // HALO
    nb = s // HALO
    cur = pl.BlockSpec((tm, width), lambda i: (i, col))
    prev = pl.BlockSpec((HALO, width), lambda i: (jnp.maximum(i * r - 1, 0), col))
    nxt = pl.BlockSpec((HALO, width), lambda i: (jnp.minimum((i + 1) * r, nb - 1), col))
    return cur, prev, nxt


def _slab_shapes(tm, c):
    return (tm + 2 * HALO, c + LANE), (SUB - 1, tm + 2 * HALO - SUB, c + LANE)


def _fill_slab(slab, tm, prev, cur, nxt):
    i = pl.program_id(0)
    last = pl.num_programs(0) - 1
    c = cur.shape[1]
    slab[0:HALO, 0:c] = jnp.where(i > 0, prev, 0.0)
    slab[HALO:HALO + tm, 0:c] = cur
    slab[HALO + tm:2 * HALO + tm, 0:c] = jnp.where(i < last, nxt, 0.0)


def _rotate_slab(slab, rot, tm):
    rows = tm + 2 * HALO - SUB
    c = slab.shape[1] - LANE
    for b in range(1, SUB):
        rot[b - 1, :, 0:c] = slab[pl.ds(b, rows), 0:c]


def _shifted(slab, rot, start, rc, cs):
    b = start % SUB
    if b == 0:
        return slab[pl.ds(start, rc), cs]
    return rot[b - 1, pl.ds(start - b, rc), cs]


def conv_fwd(h, conv_w, conv_b, g_ln, b_ln):
    s = h.shape[0]
    c = conv_w.shape[1]
    tm = _tile(s, 256)
    rc = _tile(tm, 64)

    def body(a_ref, ap_ref, an_ref, g_ref, gp_ref, gn_ref, w_ref, cb_ref, lg_ref, lb_ref, co_ref, uc_ref, slab, rot):
        _fill_slab(slab, tm, ap_ref[...] * _sigmoid(gp_ref[...]), a_ref[...] * _sigmoid(g_ref[...]),
                   an_ref[...] * _sigmoid(gn_ref[...]))
        _rotate_slab(slab, rot, tm)

        def lane_block(cb, carry):
            cs = pl.ds(pl.multiple_of(cb * LANE, LANE), LANE)
            for r0 in range(0, tm, rc):
                acc = jnp.zeros((rc, LANE), F32)
                for k in range(CONV_K):
                    acc = acc + w_ref[k:k + 1, cs] * _shifted(slab, rot, r0 + HALO - CONV_PAD + k, rc, cs)
                uc_ref[r0:r0 + rc, cs] = acc + cb_ref[:, cs]
            return carry

        lax.fori_loop(0, c // LANE, lane_block, 0)
        xhat, _ = _ln_stats(uc_ref[...])
        cl = xhat * lg_ref[...] + lb_ref[...]
        co_ref[...] = (cl * _sigmoid(cl)).astype(BF)

    a_specs = _halo_specs(tm, s, c, 1)
    g_specs = _halo_specs(tm, s, c, 2)
    row = pl.BlockSpec((1, c), lambda i: (0, 0))
    tok = pl.BlockSpec((tm, c), lambda i: (i, 0))
    return _call(
        body, name="conv_fwd", grid=(s // tm,),
        in_specs=[*a_specs, *g_specs, pl.BlockSpec(conv_w.shape, lambda i: (0, 0)), row, row, row],
        out_specs=[tok, tok], out_shape=[_sds((s, c), BF), _sds((s, c), F32)],
        scratch_shapes=[pltpu.VMEM(shp, F32) for shp in _slab_shapes(tm, c)],
        compiler_params=_cp(("arbitrary",)),
    )(h, h, h, h, h, h, conv_w, conv_b, g_ln, b_ln)


def out_proj_ln1(ob, co, wout, x0, g1, b1):
    s, d = x0.shape
    kh = ob.shape[1]
    tm = _tile(s, 256)

    def body(o_ref, c_ref, w_ref, x_ref, g_ref, b_ref, r_ref, x1_ref, x1b_ref, acc):
        acc[...] = _dot(o_ref[...], w_ref[0:kh, :]) + _dot(c_ref[...], w_ref[kh:2 * kh, :])
        g = g_ref[...]
        b = b_ref[...]

        def chunk(rows):
            r = ALPHA * x_ref[rows, :] + acc[rows, :]
            r_ref[rows, :] = r
            xhat, _ = _ln_stats(r)
            y = xhat * g + b
            x1_ref[rows, :] = y
            x1b_ref[rows, :] = y.astype(BF)

        _row_chunks(tm, chunk)

    half = pl.BlockSpec((tm, kh), lambda i: (i, 0))
    tok = pl.BlockSpec((tm, d), lambda i: (i, 0))
    row = pl.BlockSpec((1, d), lambda i: (0, 0))
    return _call(
        body, name="out_proj_ln1", grid=(s // tm,),
        in_specs=[half, half, pl.BlockSpec((2 * kh, d), lambda i: (0, 0)), tok, row, row],
        out_specs=[tok, tok, tok], out_shape=[_sds((s, d), F32), _sds((s, d), F32), _sds((s, d), BF)],
        scratch_shapes=[pltpu.VMEM((tm, d), F32)], compiler_params=_cp(("arbitrary",)),
    )(ob, co, wout, x0, g1, b1)


def ff1_fwd(x1b, wff1_g):
    s, d = x1b.shape
    nsh, _, fs = wff1_g.shape
    tm = _tile(s, 1024)
    tn = _tile(fs, 1024)
    per = fs // tn

    def body(a_ref, w_ref, r_ref, a1_ref):
        r = jnp.maximum(_dot(a_ref[...], w_ref[...]), 0.0)
        r_ref[...] = r.astype(BF)
        a1_ref[...] = (r * r).astype(BF)

    out = pl.BlockSpec((tm, tn), lambda i, j: (i, j))
    return _call(
        body, name="ff1_fwd", grid=(s // tm, nsh * per),
        in_specs=[pl.BlockSpec((tm, d), lambda i, j: (i, 0)),
                  pl.BlockSpec((None, d, tn), lambda i, j: (j // per, 0, j % per))],
        out_specs=[out, out], out_shape=[_sds((s, nsh * fs), BF)] * 2,
        compiler_params=_cp(("arbitrary", "arbitrary")),
    )(x1b, wff1_g)


def ff2_ln2_loss(a1b, wff2, x1, target, g2, b2):
    s, f = a1b.shape
    d = x1.shape[1]
    tm = _tile(s, 512)
    tk = _tile(f, 2048)
    nk = f // tk

    def body(a_ref, w_ref, x_ref, t_ref, g_ref, b_ref, dr_ref, drb_ref, loss_ref, dg_ref, db_ref, acc):
        i = pl.program_id(0)
        k = pl.program_id(1)

        @pl.when(k == 0)
        def _():
            acc[...] = _dot(a_ref[...], w_ref[...])

        @pl.when(k > 0)
        def _():
            acc[...] += _dot(a_ref[...], w_ref[...])

        @pl.when(jnp.logical_and(i == 0, k == 0))
        def _():
            loss_ref[...] = jnp.zeros_like(loss_ref)
            dg_ref[...] = jnp.zeros_like(dg_ref)
            db_ref[...] = jnp.zeros_like(db_ref)

        @pl.when(k == nk - 1)
        def _():
            g = g_ref[...]

            def chunk(rows):
                r = ALPHA * x_ref[rows, :] + acc[rows, :]
                xhat, rstd = _ln_stats(r)
                e = xhat * g + b_ref[...] - t_ref[rows, :]
                e2 = _rows8(e * e)
                part = e2[:, 0:LANE]
                for c in range(1, d // LANE):
                    part = part + e2[:, LANE * c:LANE * (c + 1)]
                loss_ref[...] += part * (0.5 / d)
                dy = e * (1.0 / d)
                dg_ref[...] += _rows8(dy * xhat)
                db_ref[...] += _rows8(dy)
                dr = _ln_bwd(dy, xhat, rstd, g)
                dr_ref[rows, :] = dr
                drb_ref[rows, :] = dr.astype(BF)

            _row_chunks(tm, chunk)

    tok = pl.BlockSpec((tm, d), lambda i, k: (i, 0))
    row = pl.BlockSpec((1, d), lambda i, k: (0, 0))
    accs = pl.BlockSpec((SUB, d), lambda i, k: (0, 0))
    return _call(
        body, name="ff2_ln2_loss", grid=(s // tm, nk),
        in_specs=[pl.BlockSpec((tm, tk), lambda i, k: (i, k)), pl.BlockSpec((tk, d), lambda i, k: (k, 0)),
                  tok, tok, row, row],
        out_specs=[tok, tok, pl.BlockSpec((SUB, LANE), lambda i, k: (0, 0)), accs, accs],
        out_shape=[_sds((s, d), F32), _sds((s, d), BF), _sds((SUB, LANE), F32), _sds((SUB, d), F32), _sds((SUB, d), F32)],
        scratch_shapes=[pltpu.VMEM((tm, d), F32)], compiler_params=_cp(("arbitrary", "arbitrary"), 60),
    )(a1b, wff2, x1, target, g2, b2)


def ff2_bwd_act(dr2b, wff2, rb):
    s, d = dr2b.shape
    f = wff2.shape[0]
    tm = _tile(s, 1024)
    tn = _tile(f, 1024)

    def body(a_ref, w_ref, r_ref, o_ref):
        o_ref[...] = (_dot_nt(a_ref[...], w_ref[...]) * (2.0 * r_ref[...].astype(F32))).astype(BF)

    return _call(
        body, name="ff2_bwd_act", grid=(s // tm, f // tn),
        in_specs=[pl.BlockSpec((tm, d), lambda i, j: (i, 0)), pl.BlockSpec((tn, d), lambda i, j: (j, 0)),
                  pl.BlockSpec((tm, tn), lambda i, j: (i, j))],
        out_specs=pl.BlockSpec((tm, tn), lambda i, j: (i, j)), out_shape=_sds((s, f), BF),
        compiler_params=_cp(("arbitrary", "arbitrary")),
    )(dr2b, wff2, rb)


def wgrad(name, a, b, tm, tn, tk=2048, shards=1):
    s, m = a.shape
    n = b.shape[1]
    tm = _tile(m, tm)
    ns = n // shards
    tn = _tile(ns, tn)
    tk = _tile(s, tk)
    per = ns // tn

    def body(a_ref, b_ref, o_ref):
        k = pl.program_id(2)

        @pl.when(k == 0)
        def _():
            o_ref[...] = _dot_tn(a_ref[...], b_ref[...])

        @pl.when(k > 0)
        def _():
            o_ref[...] += _dot_tn(a_ref[...], b_ref[...])

    return _call(
        body, name=name, grid=(m // tm, n // tn, s // tk),
        in_specs=[pl.BlockSpec((tk, tm), lambda i, j, k: (k, i)), pl.BlockSpec((tk, tn), lambda i, j, k: (k, j))],
        out_specs=pl.BlockSpec((None, tm, tn), lambda i, j, k: (j // per, i, j % per)),
        out_shape=_sds((shards, m, ns), F32), compiler_params=_cp(("arbitrary", "arbitrary", "arbitrary")),
    )(a, b)


def ff1_bwd_ln1(df1b, wff1_g, dr2, r1, g1):
    s, f = df1b.shape
    d = dr2.shape[1]
    tm = _tile(s, 512)
    tk = _tile(wff1_g.shape[2], 2048)
    per = wff1_g.shape[2] // tk
    nk = f // tk

    def body(a_ref, w_ref, d2_ref, r_ref, g_ref, dr_ref, drb_ref, dg_ref, db_ref, acc):
        i = pl.program_id(0)
        k = pl.program_id(1)

        @pl.when(k == 0)
        def _():
            acc[...] = _dot_nt(a_ref[...], w_ref[...])

        @pl.when(k > 0)
        def _():
            acc[...] += _dot_nt(a_ref[...], w_ref[...])

        @pl.when(jnp.logical_and(i == 0, k == 0))
        def _():
            dg_ref[...] = jnp.zeros_like(dg_ref)
            db_ref[...] = jnp.zeros_like(db_ref)

        @pl.when(k == nk - 1)
        def _():
            g = g_ref[...]

            def chunk(rows):
                dy = ALPHA * d2_ref[rows, :] + acc[rows, :]
                xhat, rstd = _ln_stats(r_ref[rows, :])
                dg_ref[...] += _rows8(dy * xhat)
                db_ref[...] += _rows8(dy)
                dr = _ln_bwd(dy, xhat, rstd, g)
                dr_ref[rows, :] = dr
                drb_ref[rows, :] = dr.astype(BF)

            _row_chunks(tm, chunk)

    tok = pl.BlockSpec((tm, d), lambda i, k: (i, 0))
    accs = pl.BlockSpec((SUB, d), lambda i, k: (0, 0))
    return _call(
        body, name="ff1_bwd_ln1", grid=(s // tm, nk),
        in_specs=[pl.BlockSpec((tm, tk), lambda i, k: (i, k)), pl.BlockSpec((None, d, tk), lambda i, k: (k // per, 0, k % per)),
                  tok, tok, pl.BlockSpec((1, d), lambda i, k: (0, 0))],
        out_specs=[tok, tok, accs, accs],
        out_shape=[_sds((s, d), F32), _sds((s, d), BF), _sds((SUB, d), F32), _sds((SUB, d), F32)],
        scratch_shapes=[pltpu.VMEM((tm, d), F32)], compiler_params=_cp(("arbitrary", "arbitrary"), 60),
    )(df1b, wff1_g, dr2, r1, g1)


def out_proj_bwd(dr1b, woutt, o):
    s, d = dr1b.shape
    tm = _tile(s, 256)

    def body(a_ref, w_ref, o_ref, do_ref, dot_ref, dc_ref, dl_ref):
        dcat = _dot(a_ref[...], w_ref[...])
        do = dcat[:, 0:MLA_W]
        do_ref[...] = do.astype(BF)
        dc_ref[...] = dcat[:, MLA_W:]
        prod = do * o_ref[...]
        for hd in range(HEADS):
            hs = slice(LANE * hd, LANE * (hd + 1))
            dl_ref[hd] = _as_row(jnp.sum(prod[:, hs], axis=-1, keepdims=True))
            dot_ref[hd] = do[:, hs].T.astype(BF)

    half = pl.BlockSpec((tm, MLA_W), lambda i: (i, 0))
    return _call(
        body, name="out_proj_bwd", grid=(s // tm,),
        in_specs=[pl.BlockSpec((tm, d), lambda i: (i, 0)), pl.BlockSpec((d, d), lambda i: (0, 0)), half],
        out_specs=[half, pl.BlockSpec((HEADS, LANE, tm), lambda i: (0, 0, i)),
                   pl.BlockSpec((tm, d - MLA_W), lambda i: (i, 0)), pl.BlockSpec((HEADS, 1, tm), lambda i: (0, 0, i))],
        out_shape=[_sds((s, MLA_W), BF), _sds((HEADS, LANE, s), BF), _sds((s, d - MLA_W), F32), _sds((HEADS, 1, s), F32)],
        compiler_params=_cp(("arbitrary",)),
    )(dr1b, woutt, o)


def conv_bwd_ln(uc, dco, g_ln, b_ln):
    s, c = uc.shape
    tm = _tile(s, 512)

    def body(u_ref, d_ref, g_ref, b_ref, du_ref, dg_ref, db_ref, dcb_ref):
        @pl.when(pl.program_id(0) == 0)
        def _():
            dg_ref[...] = jnp.zeros_like(dg_ref)
            db_ref[...] = jnp.zeros_like(db_ref)
            dcb_ref[...] = jnp.zeros_like(dcb_ref)

        xhat, rstd = _ln_stats(u_ref[...])
        g = g_ref[...]
        cl = xhat * g + b_ref[...]
        sg = _sigmoid(cl)
        dcl = d_ref[...] * (sg * (1.0 + cl * (1.0 - sg)))
        dg_ref[...] += _rows8(dcl * xhat)
        db_ref[...] += _rows8(dcl)
        du = _ln_bwd(dcl, xhat, rstd, g)
        du_ref[...] = du
        dcb_ref[...] += _rows8(du)

    tok = pl.BlockSpec((tm, c), lambda i: (i, 0))
    row = pl.BlockSpec((1, c), lambda i: (0, 0))
    accs = pl.BlockSpec((SUB, c), lambda i: (0, 0))
    return _call(
        body, name="conv_bwd_ln", grid=(s // tm,), in_specs=[tok, tok, row, row], out_specs=[tok, accs, accs, accs],
        out_shape=[_sds((s, c), F32)] + [_sds((SUB, c), F32)] * 3, compiler_params=_cp(("arbitrary",)),
    )(uc, dco, g_ln, b_ln)


def conv_bwd_taps(h, duc, conv_w):
    s, c = duc.shape
    tm = _tile(s, 256)
    rc = _tile(tm, 64)

    def body(a_ref, ap_ref, an_ref, g_ref, gp_ref, gn_ref, d_ref, dp_ref, dn_ref, w_ref, o_ref, dw_ref,
             uslab, dslab, du_s, urot, drot, dw8):
        @pl.when(pl.program_id(0) == 0)
        def _():
            dw8[...] = jnp.zeros_like(dw8)

        sg = _sigmoid(g_ref[...])
        a = a_ref[...]
        _fill_slab(uslab, tm, ap_ref[...] * _sigmoid(gp_ref[...]), a * sg, an_ref[...] * _sigmoid(gn_ref[...]))
        _fill_slab(dslab, tm, dp_ref[...], d_ref[...], dn_ref[...])
        _rotate_slab(uslab, urot, tm)
        _rotate_slab(dslab, drot, tm)

        def lane_block(cb, carry):
            cs = pl.ds(pl.multiple_of(cb * LANE, LANE), LANE)
            for r0 in range(0, tm, rc):
                acc = jnp.zeros((rc, LANE), F32)
                for k in range(CONV_K):
                    acc = acc + w_ref[k:k + 1, cs] * _shifted(dslab, drot, r0 + HALO + CONV_PAD - k, rc, cs)
                du_s[r0:r0 + rc, cs] = acc
            return carry

        def lane_block_taps(cb, carry):
            cs = pl.ds(pl.multiple_of(cb * LANE, LANE), LANE)
            parts = []
            for k in range(CONV_K):
                prod = None
                for r0 in range(0, tm, rc):
                    t = dslab[pl.ds(r0 + HALO, rc), cs] * _shifted(uslab, urot, r0 + HALO - CONV_PAD + k, rc, cs)
                    prod = t if prod is None else prod + t
                parts.append(_rows8(prod))
            rows = SUB * CONV_K
            dw8[0:rows, cs] = dw8[0:rows, cs] + jnp.concatenate(parts, axis=0)
            return carry

        lax.fori_loop(0, c // LANE, lane_block, 0)
        lax.fori_loop(0, c // LANE, lane_block_taps, 0)

        @pl.when(pl.program_id(0) == pl.num_programs(0) - 1)
        def _():
            dw_ref[...] = jnp.zeros_like(dw_ref)
            for k in range(CONV_K):
                dw_ref[k:k + 1, :] = jnp.sum(dw8[SUB * k:SUB * (k + 1), :], axis=0, keepdims=True)

        du = du_s[...]
        o_ref[:, 0:c] = (du * sg).astype(BF)
        o_ref[:, c:2 * c] = (du * a * sg * (1.0 - sg)).astype(BF)

    a_specs = _halo_specs(tm, s, c, 1)
    g_specs = _halo_specs(tm, s, c, 2)
    d_specs = _halo_specs(tm, s, c, 0)
    wsp = pl.BlockSpec(conv_w.shape, lambda i: (0, 0))
    return _call(
        body, name="conv_bwd_taps", grid=(s // tm,), in_specs=[*a_specs, *g_specs, *d_specs, wsp],
        out_specs=[pl.BlockSpec((tm, 2 * c), lambda i: (i, 0)), wsp],
        out_shape=[_sds((s, 2 * c), BF), _sds(conv_w.shape, F32)],
        scratch_shapes=[pltpu.VMEM(_slab_shapes(tm, c)[0], F32), pltpu.VMEM(_slab_shapes(tm, c)[0], F32), pltpu.VMEM((tm, c), F32),
                        pltpu.VMEM(_slab_shapes(tm, c)[1], F32), pltpu.VMEM(_slab_shapes(tm, c)[1], F32),
                        pltpu.VMEM((SUB * conv_w.shape[0], c), F32)],
        compiler_params=_cp(("arbitrary",)),
    )(h, h, h, h, h, h, duc, duc, duc, conv_w)


def attn_bwd(qc, kc, kct, v, dob, dot, lse_r, delta_r):
    _, s, _ = qc.shape
    tk = _tile(s, 1024)
    tq = _tile(s, 512)
    scale = D_QK ** -0.5
    c2 = scale * LOG2E

    def body(k_ref, kt_ref, v_ref, q_ref, do_ref, dot_ref, l_ref, dl_ref, dqt_ref, dk_ref, dvt_ref):
        @pl.when(pl.program_id(1) == 0)
        def _():
            dqt_ref[...] = jnp.zeros_like(dqt_ref)

        k = k_ref[...]
        kt = kt_ref[...]
        vv = v_ref[...]

        def step(i, carry):
            dk, dvt = carry
            off = pl.multiple_of(i * tq, tq)
            q = q_ref[pl.ds(off, tq), :]
            do = do_ref[pl.ds(off, tq), :]
            pt = jnp.exp2(_dot_nt(k, q) * c2 - l_ref[:, pl.ds(off, tq)])
            dvt = dvt + _dot_nt(dot_ref[:, pl.ds(off, tq)], pt.astype(BF))
            dpt = _dot_nt(vv, do)
            dsb = (pt * (dpt - dl_ref[:, pl.ds(off, tq)]) * scale).astype(BF)
            dk = dk + _dot(dsb, q)
            dqt_ref[:, pl.ds(off, tq)] += _dot(kt, dsb)
            return dk, dvt

        dk, dvt = _unrolled_loop(s // tq, 16, step, (jnp.zeros((tk, 2 * LANE), F32), jnp.zeros((LANE, tk), F32)))
        dk_ref[...] = dk
        dvt_ref[...] = dvt

    rowv = pl.BlockSpec((None, 1, s), lambda h, j: (h, 0, 0))
    return _call(
        body, name="attn_bwd", grid=(HEADS, s // tk),
        in_specs=[pl.BlockSpec((None, tk, 2 * LANE), lambda h, j: (h, j, 0)),
                  pl.BlockSpec((None, 2 * LANE, tk), lambda h, j: (h, 0, j)),
                  pl.BlockSpec((None, tk, LANE), lambda h, j: (h, j, 0)),
                  pl.BlockSpec((None, s, 2 * LANE), lambda h, j: (h, 0, 0)),
                  pl.BlockSpec((s, LANE), lambda h, j: (0, h)),
                  pl.BlockSpec((None, LANE, s), lambda h, j: (h, 0, 0)), rowv, rowv],
        out_specs=[pl.BlockSpec((None, 2 * LANE, s), lambda h, j: (h, 0, 0)),
                   pl.BlockSpec((None, tk, 2 * LANE), lambda h, j: (h, j, 0)),
                   pl.BlockSpec((None, LANE, tk), lambda h, j: (h, 0, j))],
        out_shape=[_sds((HEADS, 2 * LANE, s), F32), _sds((HEADS, s, 2 * LANE), F32), _sds((HEADS, LANE, s), F32)],
        compiler_params=_cp(("arbitrary", "arbitrary"), 56),
    )(kc, kct, v, qc, dob, dot, lse_r, delta_r)


def q_bwd(dqt, h, g_cq, wuqt, cos, sin):
    s = h.shape[0]
    tm = _tile(s, 256)

    def body(d_ref, h_ref, g_ref, w_ref, c_ref, s_ref, dq_ref, dc_ref, dg_ref):
        @pl.when(pl.program_id(0) == 0)
        def _():
            dg_ref[...] = jnp.zeros_like(dg_ref)

        c = c_ref[...]
        sn = s_ref[...]
        for hd in range(HEADS):
            t = d_ref[hd].T
            dq_ref[:, LANE * hd:LANE * (hd + 1)] = t[:, 0:LANE].astype(BF)
            dq_ref[:, MLA_W + LANE * hd:MLA_W + LANE * (hd + 1)] = _unrope128(t[:, LANE:2 * LANE], c, sn).astype(BF)
        dy = _dot(dq_ref[...], w_ref[...])
        g = g_ref[...]
        _, xh, rr = _rms_fwd(h_ref[...], g)
        dg_ref[...] += _rows8(dy * xh)
        dc_ref[...] = _rms_bwd(dy, xh, rr, g).astype(BF)

    tab = pl.BlockSpec((tm, LANE), lambda i: (i, 0))
    return _call(
        body, name="q_bwd", grid=(s // tm,),
        in_specs=[pl.BlockSpec((HEADS, 2 * LANE, tm), lambda i: (0, 0, i)), pl.BlockSpec((tm, R_Q), lambda i: (i, 0)),
                  pl.BlockSpec((1, R_Q), lambda i: (0, 0)), pl.BlockSpec((2 * MLA_W, R_Q), lambda i: (0, 0)), tab, tab],
        out_specs=[pl.BlockSpec((tm, 2 * MLA_W), lambda i: (i, 0)), pl.BlockSpec((tm, R_Q), lambda i: (i, 0)),
                   pl.BlockSpec((SUB, R_Q), lambda i: (0, 0))],
        out_shape=[_sds((s, 2 * MLA_W), BF), _sds((s, R_Q), BF), _sds((SUB, R_Q), F32)],
        compiler_params=_cp(("arbitrary",)),
    )(dqt, h, g_cq, wuqt, cos, sin)


def kv_bwd(dk, dv, h, g_ckv, wukt, wuvt, cos, sin):
    s = h.shape[0]
    tm = _tile(s, 256)

    def body(dk_ref, dv_ref, h_ref, g_ref, wk_ref, wv_ref, c_ref, s_ref, dkn_ref, dvb_ref, dc_ref, dkr_ref, dg_ref):
        @pl.when(pl.program_id(0) == 0)
        def _():
            dg_ref[...] = jnp.zeros_like(dg_ref)

        dkr = dk_ref[0, :, LANE:2 * LANE]
        for hd in range(HEADS):
            dkn_ref[:, LANE * hd:LANE * (hd + 1)] = dk_ref[hd, :, 0:LANE].astype(BF)
            dvb_ref[:, LANE * hd:LANE * (hd + 1)] = dv_ref[hd].T.astype(BF)
            if hd > 0:
                dkr = dkr + dk_ref[hd, :, LANE:2 * LANE]
        dkr_ref[...] = _unrope128(dkr, c_ref[...], s_ref[...]).astype(BF)
        dy = _dot(dkn_ref[...], wk_ref[...]) + _dot(dvb_ref[...], wv_ref[...])
        g = g_ref[...]
        _, xh, rr = _rms_fwd(h_ref[...], g)
        dg_ref[...] += _rows8(dy * xh)
        dc_ref[...] = _rms_bwd(dy, xh, rr, g).astype(BF)

    tab = pl.BlockSpec((tm, LANE), lambda i: (i, 0))
    wsp = pl.BlockSpec((MLA_W, R_KV), lambda i: (0, 0))
    wide = pl.BlockSpec((tm, MLA_W), lambda i: (i, 0))
    return _call(
        body, name="kv_bwd", grid=(s // tm,),
        in_specs=[pl.BlockSpec((HEADS, tm, 2 * LANE), lambda i: (0, i, 0)), pl.BlockSpec((HEADS, LANE, tm), lambda i: (0, 0, i)),
                  pl.BlockSpec((tm, R_KV), lambda i: (i, 1)), pl.BlockSpec((1, R_KV), lambda i: (0, 0)), wsp, wsp, tab, tab],
        out_specs=[wide, wide, pl.BlockSpec((tm, R_KV), lambda i: (i, 0)), tab, pl.BlockSpec((SUB, R_KV), lambda i: (0, 0))],
        out_shape=[_sds((s, MLA_W), BF), _sds((s, MLA_W), BF), _sds((s, R_KV), BF), _sds((s, LANE), BF), _sds((SUB, R_KV), F32)],
        compiler_params=_cp(("arbitrary",)),
    )(dk, dv, h, g_ckv, wukt, wuvt, cos, sin)


def in_proj_bwd_ln(dh, wint, dr1, x, g_in):
    s, hc = dh.shape
    d = x.shape[1]
    tm = _tile(s, 256)

    def body(a_ref, w_ref, d1_ref, x_ref, g_ref, gx_ref, dg_ref, db_ref, acc):
        @pl.when(pl.program_id(0) == 0)
        def _():
            dg_ref[...] = jnp.zeros_like(dg_ref)
            db_ref[...] = jnp.zeros_like(db_ref)

        acc[...] = _dot(a_ref[...], w_ref[...])
        g = g_ref[...]

        def chunk(rows):
            dy = ALPHA * d1_ref[rows, :] + acc[rows, :]
            xhat, rstd = _ln_stats(x_ref[rows, :])
            dg_ref[...] += _rows8(dy * xhat)
            db_ref[...] += _rows8(dy)
            gx_ref[rows, :] = _ln_bwd(dy, xhat, rstd, g)

        _row_chunks(tm, chunk)

    tok = pl.BlockSpec((tm, d), lambda i: (i, 0))
    accs = pl.BlockSpec((SUB, d), lambda i: (0, 0))
    return _call(
        body, name="in_proj_bwd_ln", grid=(s // tm,),
        in_specs=[pl.BlockSpec((tm, hc), lambda i: (i, 0)), pl.BlockSpec((hc, d), lambda i: (0, 0)),
                  tok, tok, pl.BlockSpec((1, d), lambda i: (0, 0))],
        out_specs=[tok, accs, accs], out_shape=[_sds((s, d), F32), _sds((SUB, d), F32), _sds((SUB, d), F32)],
        scratch_shapes=[pltpu.VMEM((tm, d), F32)], compiler_params=_cp(("arbitrary",), 56),
    )(dh, wint, dr1, x, g_in)


def _adamw_math(w, g, m, v):
    m = ADAM_B1 * m + (1.0 - ADAM_B1) * g
    v = ADAM_B2 * v + (1.0 - ADAM_B2) * (g * g)
    m_hat = m / (1.0 - ADAM_B1 ** ADAM_STEP)
    v_hat = v / (1.0 - ADAM_B2 ** ADAM_STEP)
    delta = -ADAM_LR * (m_hat / (jnp.sqrt(v_hat) + ADAM_EPS) + ADAM_WD * w)
    return delta, m, v


def adamw(name, w, g, m, v):
    r, c = w.shape
    tr = _row_tile(r, c)

    def body(w_ref, g_ref, m_ref, v_ref, d_ref, mo_ref, vo_ref):
        d_ref[...], mo_ref[...], vo_ref[...] = _adamw_math(w_ref[...], g_ref[...], m_ref[...], v_ref[...])

    blk = pl.BlockSpec((tr, c), lambda i: (i, 0))
    return _call(
        body, name=name, grid=(r // tr,), in_specs=[blk] * 4, out_specs=[blk] * 3,
        out_shape=[_sds((r, c), F32)] * 3, compiler_params=_cp(("arbitrary",)),
    )(w, g, m, v)


def _coords():
    return lax.axis_index("x"), lax.axis_index("y"), lax.axis_index("c")


def _other_chips(x, y):
    return [(1 - x, y, 2 * (1 - x) + y), (x, 1 - y, 2 * x + 1 - y), (1 - x, 1 - y, 2 * (1 - x) + 1 - y)]


ANY = pl.BlockSpec(memory_space=pl.ANY)
HBM = pl.BlockSpec(memory_space=pltpu.HBM)
SEM = pl.BlockSpec(memory_space=pltpu.SEMAPHORE)
EFFECT = pltpu.SideEffectType.DATAFLOW_SIDE_EFFECTING


def _in_hbm(a):
    return pltpu.with_memory_space_constraint(a, pltpu.HBM)


def _split_plan(mode, src, land, x, y, c):
    if mode == "pair":
        rh = src.shape[1] // 2
        return [((x, y, 1 - c), src.at[:, pl.ds((1 - c) * rh, rh)], land, land)]
    me = 2 * x + y
    plan = []
    for j, (px, py, pk) in enumerate(_other_chips(x, y)):
        if mode == "gather":
            plan.append(((px, py, c), src, land.at[me], land.at[pk]))
        elif mode == "gather_half":
            mine = pl.ds(c * (src.shape[0] // 2), src.shape[0] // 2)
            plan.append(((px, py, c), src.at[mine], land.at[me, mine], land.at[pk, mine]))
        else:
            plan.append(((px, py, c), src.at[pk], land.at[j], land.at[j]))
    return plan


def _plan_len(mode):
    return 1 if mode == "pair" else N_CHIP - 1


def split_send_start(name, mode, srcs, land_shapes, order_after):
    n = len(srcs)
    np_ = _plan_len(mode)

    def body(*refs):
        ins, lands = refs[:n], refs[n:2 * n]
        ss, rs = refs[2 * n + 1], refs[2 * n + 2]
        token = refs[-1]
        x, y, c = _coords()
        for a in range(n):
            for j, (peer, src, dst, _) in enumerate(_split_plan(mode, ins[a], lands[a], x, y, c)):
                pltpu.make_async_remote_copy(src_ref=src, dst_ref=dst, send_sem=ss.at[np_ * a + j], recv_sem=rs.at[np_ * a + j],
                                             device_id=peer, device_id_type=MESH).start()
        token[...] = jnp.zeros_like(token)

    lands = [lax.empty(shp, s.dtype) for shp, s in zip(land_shapes, srcs)]
    outs = _call(
        body, name=name,
        out_shape=(pltpu.SemaphoreType.DMA((np_ * n,)), pltpu.SemaphoreType.DMA((np_ * n,)),
                   *[pltpu.HBM(s.shape, s.dtype) for s in srcs], *[pltpu.HBM(l.shape, l.dtype) for l in lands],
                   _sds((SUB, LANE), F32)),
        in_specs=[HBM] * (2 * n) + [ANY], out_specs=(SEM, SEM, *[HBM] * (2 * n), pl.BlockSpec(memory_space=pltpu.VMEM)),
        input_output_aliases={a: 2 + a for a in range(2 * n)},
        compiler_params=pltpu.CompilerParams(has_side_effects=EFFECT),
    )(*[_in_hbm(s) for s in srcs], *[_in_hbm(l) for l in lands], order_after)
    return outs[0], outs[1], list(outs[2:2 + n]), list(outs[2 + n:2 + 2 * n]), outs[-1]


def split_send_wait(name, mode, ss, rs, srcs, lands, order_after):
    n = len(srcs)
    np_ = _plan_len(mode)

    def body(*refs):
        ins, lnd = refs[:n], refs[n:2 * n]
        s_ref, r_ref = refs[2 * n], refs[2 * n + 1]
        x, y, c = _coords()
        for a in range(n):
            for j, (peer, src, _, got) in enumerate(_split_plan(mode, ins[a], lnd[a], x, y, c)):
                cp = pltpu.make_async_remote_copy(src_ref=src, dst_ref=got, send_sem=s_ref.at[np_ * a + j], recv_sem=r_ref.at[np_ * a + j],
                                                  device_id=peer, device_id_type=MESH)
                cp.wait_send()
                cp.wait_recv()

    outs = _call(
        body, name=name, out_shape=tuple(pltpu.HBM(t.shape, t.dtype) for t in (*srcs, *lands)),
        in_specs=[HBM] * (2 * n) + [SEM, SEM, ANY], out_specs=tuple([HBM] * (2 * n)),
        input_output_aliases={a: a for a in range(2 * n)},
        compiler_params=pltpu.CompilerParams(has_side_effects=EFFECT),
    )(*srcs, *lands, ss, rs, order_after)
    return list(outs[:n]), list(outs[n:])


def swap_gathered_halves(lands):
    n = len(lands)

    def body(*refs):
        outs = refs[n:2 * n]
        ss, rs = refs[2 * n:]
        x, y, c = _coords()
        cps = []
        for a in range(n):
            rh = outs[a].shape[1] // 2
            for j, (px, py, pk) in enumerate(_other_chips(x, y)):
                held = outs[a].at[pk, pl.ds(c * rh, rh)]
                cp = pltpu.make_async_remote_copy(src_ref=held, dst_ref=held, send_sem=ss.at[a, j], recv_sem=rs.at[a, j],
                                                  device_id=(x, y, 1 - c), device_id_type=MESH)
                cp.start()
                cps.append(cp)
        for a in range(n):
            rh = outs[a].shape[1] // 2
            for j, (px, py, pk) in enumerate(_other_chips(x, y)):
                theirs = outs[a].at[pk, pl.ds((1 - c) * rh, rh)]
                pltpu.make_async_remote_copy(src_ref=theirs, dst_ref=theirs, send_sem=ss.at[a, j], recv_sem=rs.at[a, j],
                                             device_id=(x, y, 1 - c), device_id_type=MESH).wait_recv()
        for cp in cps:
            cp.wait_send()

    return _call(
        body, name="swap_gathered_halves", in_specs=[ANY] * n, out_specs=[ANY] * n,
        out_shape=[_sds(l.shape, l.dtype) for l in lands], input_output_aliases={a: a for a in range(n)},
        scratch_shapes=[pltpu.SemaphoreType.DMA((n, 3))] * 2,
    )(*lands)


def pair_exchange(grads, tag):
    n = len(grads)

    def body(*refs):
        ins, outs = refs[:n], refs[n:2 * n]
        ss, rs = refs[2 * n:]
        x, y, c = _coords()
        cps = []
        for a in range(n):
            rh = ins[a].shape[1] // 2
            cp = pltpu.make_async_remote_copy(
                src_ref=ins[a].at[:, pl.ds((1 - c) * rh, rh)], dst_ref=outs[a], send_sem=ss.at[a], recv_sem=rs.at[a],
                device_id=(x, y, 1 - c), device_id_type=MESH)
            cp.start()
            cps.append(cp)
        for cp in cps:
            cp.wait()

    return _call(
        body, name="pair_exchange_" + tag, in_specs=[ANY] * n, out_specs=[ANY] * n,
        out_shape=[_sds((N_CHIP, g.shape[1] // 2, g.shape[2]), F32) for g in grads],
        scratch_shapes=[pltpu.SemaphoreType.DMA((n,))] * 2,
    )(*grads)


def _row_tile(rows, cols, itemsize=4, budget=2 * VMEM_MB):
    fits = [t for t in range(SUB, rows + 1, SUB) if rows % t == 0 and t * cols * itemsize <= budget]
    return max(fits) if fits and rows * cols * itemsize > budget else rows


def pair_add(g, r, cidx):
    _, rows, cols = g.shape
    rh = rows // 2
    tr = _row_tile(rh, cols)
    per = rh // tr

    def body(c_ref, g_ref, r_ref, o_ref):
        o_ref[...] = g_ref[...] + r_ref[...]

    return _call(
        body, name="pair_add",
        grid_spec=pltpu.PrefetchScalarGridSpec(
            num_scalar_prefetch=1, grid=(N_CHIP, per),
            in_specs=[pl.BlockSpec((None, tr, cols), lambda k, i, c: (k, c[0] * per + i, 0)),
                      pl.BlockSpec((None, tr, cols), lambda k, i, c: (k, i, 0))],
            out_specs=pl.BlockSpec((None, tr, cols), lambda k, i, c: (k, i, 0))),
        out_shape=_sds((N_CHIP, rh, cols), F32), compiler_params=_cp(("arbitrary", "arbitrary")),
    )(cidx, g, r)


def chip_add(p, r, kc):
    _, rh, cols = p.shape
    tr = _row_tile(rh, cols)
    per = rh // tr

    def body(k_ref, p_ref, r_ref, o_ref):
        o_ref[...] = ((p_ref[...] + r_ref[0]) + r_ref[1]) + r_ref[2]

    return _call(
        body, name="chip_add",
        grid_spec=pltpu.PrefetchScalarGridSpec(
            num_scalar_prefetch=1, grid=(per,),
            in_specs=[pl.BlockSpec((None, tr, cols), lambda i, k: (k[0], i, 0)),
                      pl.BlockSpec((N_CHIP - 1, tr, cols), lambda i, k: (0, i, 0))],
            out_specs=pl.BlockSpec((tr, cols), lambda i, k: (k[1] * per + i, 0))),
        out_shape=_sds((2 * rh, cols), F32), compiler_params=_cp(("arbitrary",)),
    )(kc, p, r)


def pair_share(fulls, tag):
    n = len(fulls)

    def body(*refs):
        outs = refs[n:2 * n]
        ss, rs = refs[2 * n:]
        x, y, c = _coords()
        cps = []
        for a in range(n):
            rh = outs[a].shape[0] // 2
            mine = outs[a].at[pl.ds(c * rh, rh)]
            cp = pltpu.make_async_remote_copy(
                src_ref=mine, dst_ref=mine, send_sem=ss.at[a], recv_sem=rs.at[a],
                device_id=(x, y, 1 - c), device_id_type=MESH)
            cp.start()
            cps.append(cp)
        for a, cp in enumerate(cps):
            rh = outs[a].shape[0] // 2
            theirs = outs[a].at[pl.ds((1 - c) * rh, rh)]
            cp.wait_send()
            pltpu.make_async_remote_copy(
                src_ref=theirs, dst_ref=theirs, send_sem=ss.at[a], recv_sem=rs.at[a],
                device_id=(x, y, 1 - c), device_id_type=MESH).wait_recv()

    return _call(
        body, name="pair_share_" + tag, in_specs=[ANY] * n, out_specs=[ANY] * n,
        out_shape=[_sds(f.shape, F32) for f in fulls], input_output_aliases={a: a for a in range(n)},
        scratch_shapes=[pltpu.SemaphoreType.DMA((n,))] * 2,
    )(*fulls)


def small_allreduce_adamw(part, w, m, v):
    n = part.shape[1]

    def body(p_ref, w_ref, m_ref, v_ref, g_ref, d_ref, mo_ref, vo_ref, mine, gath, ss, rs):
        x, y, c = _coords()
        me = 4 * x + 2 * y + c
        mine[...] = jnp.sum(p_ref[...], axis=0, keepdims=True)
        gath[me] = mine[...]
        cps = []
        for k in range(1, 8):
            px, py, pc = x ^ (k >> 2), y ^ ((k >> 1) & 1), c ^ (k & 1)
            cp = pltpu.make_async_remote_copy(
                src_ref=mine, dst_ref=gath.at[me], send_sem=ss.at[k - 1], recv_sem=rs.at[k - 1],
                device_id=(px, py, pc), device_id_type=MESH)
            cp.start()
            cps.append(cp)
        for k in range(1, 8):
            src = 4 * (x ^ (k >> 2)) + 2 * (y ^ ((k >> 1) & 1)) + (c ^ (k & 1))
            pltpu.make_async_remote_copy(
                src_ref=mine, dst_ref=gath.at[src], send_sem=ss.at[k - 1], recv_sem=rs.at[k - 1],
                device_id=(x, y, c), device_id_type=MESH).wait_recv()
        for cp in cps:
            cp.wait_send()
        g = gath[0]
        for dv in range(1, 8):
            g = g + gath[dv]
        g_ref[...] = g
        d_ref[...], mo_ref[...], vo_ref[...] = _adamw_math(w_ref[...], g, m_ref[...], v_ref[...])

    vm = pl.BlockSpec(memory_space=pltpu.VMEM)
    return _call(
        body, name="small_allreduce_adamw", in_specs=[vm] * 4, out_specs=[vm] * 4, out_shape=[_sds((1, n), F32)] * 4,
        scratch_shapes=[pltpu.VMEM((1, n), F32), pltpu.VMEM((8, 1, n), F32),
                        pltpu.SemaphoreType.DMA((7,)), pltpu.SemaphoreType.DMA((7,))],
    )(part, w, m, v)


def _unshard_cols(g):
    k, r, cs = g.shape
    return g.transpose(1, 0, 2).reshape(r, k * cs)


def _shard_cols(w):
    r, c = w.shape
    return w.reshape(r, N_CHIP, c // N_CHIP).transpose(1, 0, 2)


def local_step(x, positions, ln_in_g, ln_in_b, g_cq, g_ckv, conv_b, g_conv_ln, b_conv_ln, g_ln1, b_ln1, g_ln2, b_ln2,
               target, start_token, hooks):
    s, d = x.shape
    c = d - MLA_W
    row = lambda a: a.reshape(1, -1)
    ln_in_g = row(ln_in_g) + start_token[0:1, 0:1]

    half = D_ROPE // 2
    inv_freq = ROPE_BASE ** (-jnp.arange(half, dtype=F32) * (2.0 / D_ROPE))
    invf = jnp.concatenate([inv_freq, inv_freq, jnp.zeros((LANE - D_ROPE,), F32)]).reshape(1, LANE)
    cos, sin = rope_tables(positions.astype(F32).reshape(s, 1), invf)
    x0, x0b = ln_in_fwd(x, ln_in_g, row(ln_in_b))
    win_g, wuq_g, wuk_g, wuv_g, convw_g = hooks.early_weights(x0b)

    o_kr = R_Q + R_KV
    o_cv = o_kr + D_ROPE
    n_in = o_cv + 2 * c
    per = n_in // N_CHIP

    def in_cols(a, b):
        return [win_g[k, max(a, per * k) - per * k:min(b, per * (k + 1)) - per * k]
                for k in range(N_CHIP) if max(a, per * k) < min(b, per * (k + 1))]

    win_rt = jnp.concatenate(in_cols(0, o_kr) + in_cols(o_cv, n_in) + in_cols(o_kr, o_cv)
                             + [jnp.zeros((LANE - D_ROPE, d), BF)], axis=0)
    kr_blk = (o_kr + 2 * c) // LANE
    wuq = _unshard_cols(wuq_g).reshape(R_Q, HEADS, D_QK)
    wuq_r = jnp.concatenate([wuq[:, :, :D_NOPE].reshape(R_Q, MLA_W),
                             jnp.pad(wuq[:, :, D_NOPE:], ((0, 0), (0, 0), (0, LANE - D_ROPE))).reshape(R_Q, MLA_W)], axis=1)
    wuk = _unshard_cols(wuk_g)
    wuv = _unshard_cols(wuv_g)
    conv_w = jnp.pad(_unshard_cols(convw_g), ((0, 1), (0, 0)))

    h = matmul_nt("in_proj", x0b, win_rt, 256)
    qc, cqn = q_proj(h, g_cq, wuq_r, cos, sin)
    kc, kct, v, ckvn = kv_proj(h, g_ckv, wuk, wuv, cos, sin, kr_blk)
    o, ob, lse = attn_fwd(qc, kc, v)
    co, uc = conv_fwd(h, conv_w, conv_b, g_conv_ln, b_conv_ln)
    wout_g, wff1_g, wff2_g = hooks.late_weights(ob)
    wout = wout_g.reshape(d, d)
    wff2 = wff2_g.reshape(-1, d)
    r1, x1, x1b = out_proj_ln1(ob, co, wout, x0, g_ln1, b_ln1)
    rb, a1b = ff1_fwd(x1b, wff1_g)
    dr2, dr2b, loss8, dg2, db2 = ff2_ln2_loss(a1b, wff2, x1, target, g_ln2, b_ln2)

    df1b = ff2_bwd_act(dr2b, wff2, rb)
    gw_ff2 = wgrad("wgrad_ff2", a1b, dr2b, 1024, 1024).reshape(N_CHIP, -1, d)
    gw_ff1 = wgrad("wgrad_ff1", x1b, df1b, 1024, 1024, shards=N_CHIP)
    tok = hooks.ff_grads(gw_ff2, gw_ff1)
    dr1, dr1b, dg1, db1 = ff1_bwd_ln1(df1b, wff1_g, dr2, r1, g_ln1 + tok[0:1, 0:1])
    tok = hooks.ff_grads_mid(dr1b)
    gw_out = jnp.concatenate([wgrad("wgrad_out_attn", ob, dr1b, 1024, 1024)[0],
                              wgrad("wgrad_out_conv", co, dr1b, 1024, 1024)[0]], axis=0).reshape(N_CHIP, -1, d)
    dob, dot, dco, delta = out_proj_bwd(dr1b, wout.T, o)
    duc, dgc, dbc, dcb = conv_bwd_ln(uc, dco, g_conv_ln + tok[0:1, 0:1], b_conv_ln)
    dconv, gconvw = conv_bwd_taps(h, duc, conv_w)
    dqt, dk, dv = attn_bwd(qc, kc, kct, v, dob, dot, lse, delta)
    dqb, dcq, dgq = q_bwd(dqt, h, g_cq, wuq_r.T, cos, sin)
    dknb, dvb, dckv, dkr, dgkv = kv_bwd(dk, dv, h, g_ckv, wuk.T, wuv.T, cos, sin)
    gwuq_r = wgrad("wgrad_uq", cqn, dqb, 512, 1024)[0]
    gw_uk = wgrad("wgrad_uk", ckvn, dknb, 512, 1024, shards=N_CHIP)
    gw_uv = wgrad("wgrad_uv", ckvn, dvb, 512, 1024, shards=N_CHIP)
    dh = jnp.concatenate([dcq, dckv, dconv, dkr], axis=1)
    gwin_rt = wgrad("wgrad_in", dh, x0b, 640, 1024)[0]

    gwin_t = jnp.concatenate([gwin_rt[:o_kr], gwin_rt[o_kr + 2 * c:o_kr + 2 * c + D_ROPE], gwin_rt[o_kr:o_kr + 2 * c]], axis=0)
    gwin_t = jnp.pad(gwin_t.reshape(N_CHIP, per, d), ((0, 0), (0, win_g.shape[1] - per), (0, 0)))
    gwuq = jnp.concatenate([gwuq_r[:, :MLA_W].reshape(R_Q, HEADS, D_NOPE),
                            gwuq_r[:, MLA_W:].reshape(R_Q, HEADS, LANE)[:, :, :D_ROPE]], axis=2).reshape(R_Q, HEADS * D_QK)
    tok = hooks.rest_grads(dict(w_in=gwin_t, w_uq=_shard_cols(gwuq), w_uk=gw_uk, w_uv=gw_uv,
                                conv_w=_shard_cols(gconvw), w_out=gw_out))
    gx, dgin, dbin = in_proj_bwd_ln(dh, win_rt, dr1, x, ln_in_g + tok[0:1, 0:1])
    small = jnp.concatenate([dgin, dbin, dgq, dgkv, dcb, dgc, dbc, dg1, db1, dg2, db2, loss8], axis=1)
    return gx, small


BIG = ["w_in", "w_uq", "w_uk", "w_uv", "conv_w", "w_out", "w_ff1", "w_ff2"]
EARLY = ["w_in", "w_uq", "w_uk", "w_uv", "conv_w"]
LATE = ["w_out", "w_ff1", "w_ff2"]
SMALL = ["ln_in_g", "ln_in_b", "g_cq", "g_ckv", "conv_b", "g_conv_ln", "b_conv_ln", "g_ln1", "b_ln1", "g_ln2", "b_ln2"]
WEIGHTS = ["ln_in_g", "ln_in_b", "w_in", "g_cq", "w_uq", "g_ckv", "w_uk", "w_uv", "conv_w", "conv_b", "g_conv_ln",
           "b_conv_ln", "w_out", "g_ln1", "b_ln1", "w_ff1", "w_ff2", "g_ln2", "b_ln2"]


def _pad_rows(a, rows):
    return jnp.pad(a, ((0, rows - a.shape[0]), (0, 0)))


def kernel(x, positions, ln_in_g, ln_in_b, w_in, g_cq, w_uq, g_ckv, w_uk, w_uv, conv_w, conv_b, g_conv_ln, b_conv_ln, w_out, g_ln1, b_ln1, w_ff1, w_ff2, g_ln2, b_ln2, loss_target, m_ln_in_g, m_ln_in_b, m_w_in, m_g_cq, m_w_uq, m_g_ckv, m_w_uk, m_w_uv, m_conv_w, m_conv_b, m_g_conv_ln, m_b_conv_ln, m_w_out, m_g_ln1, m_b_ln1, m_w_ff1, m_w_ff2, m_g_ln2, m_b_ln2, v_ln_in_g, v_ln_in_b, v_w_in, v_g_cq, v_w_uq, v_g_ckv, v_w_uk, v_w_uv, v_conv_w, v_conv_b, v_g_conv_ln, v_b_conv_ln, v_w_out, v_g_ln1, v_b_ln1, v_w_ff1, v_w_ff2, v_g_ln2, v_b_ln2):
    w = dict(ln_in_g=ln_in_g, ln_in_b=ln_in_b, w_in=w_in, g_cq=g_cq, w_uq=w_uq, g_ckv=g_ckv, w_uk=w_uk, w_uv=w_uv,
             conv_w=conv_w, conv_b=conv_b, g_conv_ln=g_conv_ln, b_conv_ln=b_conv_ln, w_out=w_out, g_ln1=g_ln1,
             b_ln1=b_ln1, w_ff1=w_ff1, w_ff2=w_ff2, g_ln2=g_ln2, b_ln2=b_ln2)
    m = dict(ln_in_g=m_ln_in_g, ln_in_b=m_ln_in_b, w_in=m_w_in, g_cq=m_g_cq, w_uq=m_w_uq, g_ckv=m_g_ckv, w_uk=m_w_uk,
             w_uv=m_w_uv, conv_w=m_conv_w, conv_b=m_conv_b, g_conv_ln=m_g_conv_ln, b_conv_ln=m_b_conv_ln, w_out=m_w_out,
             g_ln1=m_g_ln1, b_ln1=m_b_ln1, w_ff1=m_w_ff1, w_ff2=m_w_ff2, g_ln2=m_g_ln2, b_ln2=m_b_ln2)
    v = dict(ln_in_g=v_ln_in_g, ln_in_b=v_ln_in_b, w_in=v_w_in, g_cq=v_g_cq, w_uq=v_w_uq, g_ckv=v_g_ckv, w_uk=v_w_uk,
             w_uv=v_w_uv, conv_w=v_conv_w, conv_b=v_conv_b, g_conv_ln=v_g_conv_ln, b_conv_ln=v_b_conv_ln, w_out=v_w_out,
             g_ln1=v_g_ln1, b_ln1=v_b_ln1, w_ff1=v_w_ff1, w_ff2=v_w_ff2, g_ln2=v_g_ln2, b_ln2=v_b_ln2)

    as2d = lambda t, n: t[n][0].T if n == "w_in" else t[n][0]
    sh2 = {n: as2d(w, n) for n in BIG}
    cidx = lax.axis_index("c").astype(jnp.int32).reshape(1)
    me = 2 * lax.axis_index("x") + lax.axis_index("y")
    kc = jnp.stack([me, lax.axis_index("c")]).astype(jnp.int32)

    pad_to = {"conv_w": CONV_K + 1, "w_in": -(-sh2["w_in"].shape[0] // (4 * SUB)) * (4 * SUB)}
    early = [_pad_rows(sh2[n] if n == "conv_w" else sh2[n].astype(BF), pad_to.get(n, sh2[n].shape[0])) for n in EARLY]
    eg = split_send_start("early_weights_start", "gather_half", early, [(N_CHIP,) + a.shape for a in early], ln_in_g)
    late = [sh2[n].astype(BF) for n in LATE]
    ag = split_send_start("late_weights_start", "gather", late, [(N_CHIP,) + a.shape for a in late], eg[4])
    rest = [n for n in BIG if n not in ("w_ff2", "w_ff1")]
    flight = {}

    class Hooks:
        @staticmethod
        def early_weights(after):
            mine, lands = split_send_wait("early_weights_wait", "gather_half", *eg[:4], after)
            full = [lax.dynamic_update_slice(g, a[None], (me, 0, 0)) for g, a in zip(swap_gathered_halves(lands), mine)]
            return [g[:, :CONV_K] if n == "conv_w" else g for n, g in zip(EARLY, full)]

        @staticmethod
        def late_weights(after):
            mine, lands = split_send_wait("late_weights_wait", "gather", *ag[:4], after)
            return [lax.dynamic_update_slice(g, a[None], (me, 0, 0)) for g, a in zip(lands, mine)]

        @staticmethod
        def ff_grads(gw_ff2, gw_ff1):
            full = [gw_ff2, gw_ff1]
            st = split_send_start("ff_pair_start", "pair", full, [(N_CHIP, g.shape[1] // 2, g.shape[2]) for g in full], ag[4])
            flight["ff_pair"] = st[:4]
            flight["token"] = st[4]
            return st[4]

        @staticmethod
        def ff_grads_mid(after):
            full, recv = split_send_wait("ff_pair_wait", "pair", *flight["ff_pair"], after)
            psum = [pair_add(g, r, cidx) for g, r in zip(full, recv)]
            st = split_send_start("ff_grads_start", "scatter", psum, [(N_CHIP - 1,) + p.shape[1:] for p in psum], flight["token"])
            flight["ff"] = st[:4]
            flight["token"] = st[4]
            return st[4]

        @staticmethod
        def rest_grads(big):
            full = [big[n] for n in rest]
            psum = [pair_add(g, r, cidx) for g, r in zip(full, pair_exchange(full, "rest"))]
            st = split_send_start("rest_grads_start", "scatter", psum, [(N_CHIP - 1,) + p.shape[1:] for p in psum], flight["token"])
            flight["rest"] = st[:4]
            return st[4]

    gx, small = local_step(x[0], positions[0], ln_in_g, ln_in_b, g_cq, g_ckv, conv_b, g_conv_ln, b_conv_ln, g_ln1, b_ln1,
                           g_ln2, b_ln2, loss_target[0], ag[4], Hooks)

    ff_psum, ff_got = split_send_wait("ff_grads_wait", "scatter", *flight["ff"], gx)
    rest_psum, rest_got = split_send_wait("rest_grads_wait", "scatter", *flight["rest"], gx)
    summed = [chip_add(p, r, kc) for p, r in zip(rest_psum + ff_psum, rest_got + ff_got)]
    gsh = dict(zip(rest + ["w_ff2", "w_ff1"], pair_share(summed, "all")))
    for n in pad_to:
        gsh[n] = gsh[n][:sh2[n].shape[0]]

    grad, delta, new_m, new_v = {}, {}, {}, {}
    for n in BIG:
        back = (lambda a: a.T[None]) if n == "w_in" else (lambda a: a[None])
        d_, m_, v_ = adamw("adamw_" + n, sh2[n], gsh[n], as2d(m, n), as2d(v, n))
        grad[n], delta[n], new_m[n], new_v[n] = back(gsh[n]), back(d_), back(m_), back(v_)

    flat = lambda t: jnp.concatenate([t[n].reshape(1, -1) for n in SMALL] + [jnp.zeros((1, LANE), F32)], axis=1)
    g_s, d_s, m_s, v_s = small_allreduce_adamw(small, flat(w), flat(m), flat(v))
    off = 0
    for n in SMALL:
        sz = w[n].size
        for dst, src in ((grad, g_s), (delta, d_s), (new_m, m_s), (new_v, v_s)):
            dst[n] = src[0, off:off + sz].reshape(w[n].shape)
        off += sz
    loss = jnp.sum(g_s[0, off:off + LANE])

    return (loss, gx[None], *[grad[n] for n in WEIGHTS], *[delta[n] for n in WEIGHTS],
            *[new_m[n] for n in WEIGHTS], *[new_v[n] for n in WEIGHTS])
```

```python
import jax
import jax.numpy as jnp
from jax import lax
from jax.experimental import pallas as pl
from jax.experimental.pallas import tpu as pltpu

F32 = jnp.float32
BF = jnp.bfloat16

HEADS = 8
D_NOPE = 128
D_ROPE = 64
D_V = 128
D_QK = D_NOPE + D_ROPE
R_Q = 512
R_KV = 512
MLA_W = HEADS * D_V
CONV_K = 31
CONV_PAD = CONV_K // 2
ROPE_BASE = 10000.0
LOG2E = 1.4426950408889634
LN2 = 0.6931471805599453
LN_EPS = 1e-5
RMS_EPS = 1e-6
ALPHA = (2.0 * 1) ** 0.25
ADAM_LR = 0.001
ADAM_B1 = 0.9
ADAM_B2 = 0.999
ADAM_EPS = 1e-08
ADAM_WD = 0.01
ADAM_STEP = 10

LANE = 128
SUB = 8
HALO = 16
N_CHIP = 4
MESH = pl.DeviceIdType.MESH
VMEM_MB = 1024 * 1024


def _call(body, **kw):
    return pl.pallas_call(body, **kw)


def _cp(sem, mb=48):
    return pltpu.CompilerParams(dimension_semantics=sem, vmem_limit_bytes=mb * VMEM_MB)


def _sds(shape, dt):
    return jax.ShapeDtypeStruct(shape, dt)


def _dot(a, b):
    return jnp.dot(a, b, preferred_element_type=F32)


def _dot_nt(a, b):
    return lax.dot_general(a, b, (((1,), (1,)), ((), ())), preferred_element_type=F32)


def _dot_tn(a, b):
    return lax.dot_general(a, b, (((0,), (0,)), ((), ())), preferred_element_type=F32)


def _rows8(v):
    t, n = v.shape
    return v.reshape(t // SUB, SUB, n).sum(axis=0)


def _ln_stats(r):
    mu = jnp.mean(r, axis=-1, keepdims=True)
    xc = r - mu
    var = jnp.mean(xc * xc, axis=-1, keepdims=True)
    rstd = lax.rsqrt(var + LN_EPS)
    return xc * rstd, rstd


def _ln_bwd(dy, xhat, rstd, g):
    dyh = dy * g
    m1 = jnp.mean(dyh, axis=-1, keepdims=True)
    m2 = jnp.mean(dyh * xhat, axis=-1, keepdims=True)
    return rstd * (dyh - m1 - xhat * m2)


def _rms_fwd(x, g):
    rr = lax.rsqrt(jnp.mean(x * x, axis=-1, keepdims=True) + RMS_EPS)
    xh = x * rr
    return xh * g, xh, rr


def _rms_bwd(dy, xh, rr, g):
    dyg = dy * g
    return rr * (dyg - xh * jnp.mean(dyg * xh, axis=-1, keepdims=True))


def _rope128(x, cos, sin_signed):
    lane = lax.broadcasted_iota(jnp.int32, x.shape, 1)
    rot = jnp.where(lane < D_ROPE // 2, pltpu.roll(x, LANE - D_ROPE // 2, 1), pltpu.roll(x, D_ROPE // 2, 1))
    return x * cos + rot * sin_signed


def _unrope128(dy, cos, sin_signed):
    t = dy * sin_signed
    lane = lax.broadcasted_iota(jnp.int32, dy.shape, 1)
    rot = jnp.where(lane < D_ROPE // 2, pltpu.roll(t, LANE - D_ROPE // 2, 1), pltpu.roll(t, D_ROPE // 2, 1))
    return dy * cos + rot


def _as_row(col):
    return jnp.transpose(jnp.broadcast_to(col, (col.shape[0], LANE)))[0:1, :]


def _sigmoid(x):
    return 1.0 / (1.0 + jnp.exp(-x))


def _row_chunks(tm, fn, rc=128):
    rc = min(rc, tm)

    def step(ci, carry):
        fn(pl.ds(pl.multiple_of(ci * rc, rc), rc))
        return carry

    lax.fori_loop(0, tm // rc, step, 0)


def _unrolled_loop(n, unroll, fn, init):
    unroll = min(n, unroll)
    assert n % unroll == 0

    def body(t, carry):
        for u in range(unroll):
            carry = fn(t * unroll + u, carry)
        return carry

    return lax.fori_loop(0, n // unroll, body, init)


def _tile(s, want):
    t = min(s, want)
    assert s % t == 0
    return t


def rope_tables(pos_f, invf):
    s = pos_f.shape[0]
    tm = _tile(s, 1024)

    def body(p_ref, f_ref, c_ref, s_ref):
        ang = p_ref[...] * f_ref[...]
        lane = lax.broadcasted_iota(jnp.int32, ang.shape, 1)
        c = jnp.cos(ang)
        sn = jnp.sin(ang)
        c_ref[...] = jnp.where(lane < D_ROPE, c, 0.0)
        s_ref[...] = jnp.where(lane < D_ROPE // 2, -sn, jnp.where(lane < D_ROPE, sn, 0.0))

    return _call(
        body, name="rope_tables", grid=(s // tm,),
        in_specs=[pl.BlockSpec((tm, 1), lambda i: (i, 0)), pl.BlockSpec((1, LANE), lambda i: (0, 0))],
        out_specs=[pl.BlockSpec((tm, LANE), lambda i: (i, 0))] * 2,
        out_shape=[_sds((s, LANE), F32)] * 2,
        compiler_params=_cp(("arbitrary",)),
    )(pos_f, invf)


def ln_in_fwd(x, g, b):
    s, d = x.shape
    tm = _tile(s, 512)

    def body(x_ref, g_ref, b_ref, o_ref, ob_ref):
        xhat, _ = _ln_stats(x_ref[...])
        y = xhat * g_ref[...] + b_ref[...]
        o_ref[...] = y
        ob_ref[...] = y.astype(BF)

    row = pl.BlockSpec((1, d), lambda i: (0, 0))
    tok = pl.BlockSpec((tm, d), lambda i: (i, 0))
    return _call(
        body, name="ln_in_fwd", grid=(s // tm,), in_specs=[tok, row, row], out_specs=[tok, tok],
        out_shape=[_sds((s, d), F32), _sds((s, d), BF)], compiler_params=_cp(("arbitrary",)),
    )(x, g, b)


def matmul_nt(name, a, wt, tm, out_dtype=F32):
    s, k = a.shape
    n = wt.shape[0]
    tm = _tile(s, tm)

    def body(a_ref, w_ref, o_ref):
        o_ref[...] = _dot_nt(a_ref[...], w_ref[...]).astype(o_ref.dtype)

    return _call(
        body, name=name, grid=(s // tm,),
        in_specs=[pl.BlockSpec((tm, k), lambda i: (i, 0)), pl.BlockSpec((n, k), lambda i: (0, 0))],
        out_specs=pl.BlockSpec((tm, n), lambda i: (i, 0)),
        out_shape=_sds((s, n), out_dtype), compiler_params=_cp(("arbitrary",)),
    )(a, wt)


def q_proj(h, g_cq, wuq, cos, sin):
    s = h.shape[0]
    tm = _tile(s, 512)

    def body(h_ref, g_ref, w_ref, c_ref, s_ref, q_ref, n_ref):
        y, _, _ = _rms_fwd(h_ref[...], g_ref[...])
        yb = y.astype(BF)
        n_ref[...] = yb
        q = _dot(yb, w_ref[...])
        c = c_ref[...]
        sn = s_ref[...]
        for hd in range(HEADS):
            q_ref[hd, :, 0:LANE] = q[:, LANE * hd:LANE * (hd + 1)].astype(BF)
            qr = q[:, MLA_W + LANE * hd:MLA_W + LANE * (hd + 1)]
            q_ref[hd, :, LANE:2 * LANE] = _rope128(qr, c, sn).astype(BF)

    return _call(
        body, name="q_proj", grid=(s // tm,),
        in_specs=[pl.BlockSpec((tm, R_Q), lambda i: (i, 0)), pl.BlockSpec((1, R_Q), lambda i: (0, 0)),
                  pl.BlockSpec((R_Q, 2 * MLA_W), lambda i: (0, 0)),
                  pl.BlockSpec((tm, LANE), lambda i: (i, 0)), pl.BlockSpec((tm, LANE), lambda i: (i, 0))],
        out_specs=[pl.BlockSpec((HEADS, tm, 2 * LANE), lambda i: (0, i, 0)), pl.BlockSpec((tm, R_Q), lambda i: (i, 0))],
        out_shape=[_sds((HEADS, s, 2 * LANE), BF), _sds((s, R_Q), BF)], compiler_params=_cp(("arbitrary",)),
    )(h, g_cq, wuq, cos, sin)


def kv_proj(h, g_ckv, wuk, wuv, cos, sin, kr_blk):
    s = h.shape[0]
    tm = _tile(s, 512)

    def body(h_ref, kr_ref, g_ref, wk_ref, wv_ref, c_ref, s_ref, k_ref, kt_ref, v_ref, n_ref):
        y, _, _ = _rms_fwd(h_ref[...], g_ref[...])
        yb = y.astype(BF)
        n_ref[...] = yb
        kn = _dot(yb, wk_ref[...])
        v = _dot(yb, wv_ref[...])
        kr = _rope128(kr_ref[...], c_ref[...], s_ref[...])
        krb = kr.astype(BF)
        krt = kr.T.astype(BF)
        for hd in range(HEADS):
            knh = kn[:, LANE * hd:LANE * (hd + 1)]
            k_ref[hd, :, 0:LANE] = knh.astype(BF)
            k_ref[hd, :, LANE:2 * LANE] = krb
            kt_ref[hd, 0:LANE, :] = knh.T.astype(BF)
            kt_ref[hd, LANE:2 * LANE, :] = krt
            v_ref[hd] = v[:, LANE * hd:LANE * (hd + 1)].astype(BF)

    tab = pl.BlockSpec((tm, LANE), lambda i: (i, 0))
    wsp = pl.BlockSpec((R_KV, MLA_W), lambda i: (0, 0))
    return _call(
        body, name="kv_proj", grid=(s // tm,),
        in_specs=[pl.BlockSpec((tm, R_KV), lambda i: (i, 1)), pl.BlockSpec((tm, LANE), lambda i: (i, kr_blk)),
                  pl.BlockSpec((1, R_KV), lambda i: (0, 0)), wsp, wsp, tab, tab],
        out_specs=[pl.BlockSpec((HEADS, tm, 2 * LANE), lambda i: (0, i, 0)), pl.BlockSpec((HEADS, 2 * LANE, tm), lambda i: (0, 0, i)),
                   pl.BlockSpec((HEADS, tm, LANE), lambda i: (0, i, 0)), pl.BlockSpec((tm, R_KV), lambda i: (i, 0))],
        out_shape=[_sds((HEADS, s, 2 * LANE), BF), _sds((HEADS, 2 * LANE, s), BF), _sds((HEADS, s, LANE), BF), _sds((s, R_KV), BF)],
        compiler_params=_cp(("arbitrary",)),
    )(h, h, g_ckv, wuk, wuv, cos, sin)


def attn_fwd(qc, kc, v):
    _, s, _ = qc.shape
    tq = _tile(s, 512)
    tk = _tile(s, 512)
    scale = D_QK ** -0.5
    c2 = scale * LOG2E
    nk = s // tk
    nb = tk // LANE
    un = 8

    def body(q_ref, k_ref, v_ref, o_ref, ob_ref, l_ref, s_scr, m_scr):
        q = q_ref[...]

        def scores(j, mpart):
            off = pl.multiple_of(j * tk, tk)
            sc = _dot_nt(q, k_ref[pl.ds(off, tk), :]) * c2
            s_scr[:, pl.ds(off, tk)] = sc
            for b in range(nb):
                mpart = jnp.maximum(mpart, sc[:, LANE * b:LANE * (b + 1)])
            return mpart

        mpart = _unrolled_loop(nk, un, scores, jnp.full((tq, LANE), -jnp.inf, F32))
        m = jnp.max(mpart, axis=-1, keepdims=True)
        m_scr[...] = jnp.broadcast_to(m, (tq, LANE))

        def weigh(j, carry):
            lpart, acc = carry
            off = pl.multiple_of(j * tk, tk)
            ps = []
            for b in range(nb):
                p = jnp.exp2(s_scr[:, pl.ds(off + LANE * b, LANE)] - m_scr[...])
                lpart = lpart + p
                ps.append(p.astype(BF))
            acc = acc + _dot(jnp.concatenate(ps, axis=1), v_ref[pl.ds(off, tk), :])
            return lpart, acc

        lpart, acc = _unrolled_loop(nk, un, weigh, (jnp.zeros((tq, LANE), F32), jnp.zeros((tq, D_V), F32)))
        l = jnp.sum(lpart, axis=-1, keepdims=True)
        o = acc / l
        o_ref[...] = o
        ob_ref[...] = o.astype(BF)
        l_ref[...] = _as_row(m + jnp.log(l) * LOG2E)

    return _call(
        body, name="attn_fwd", grid=(HEADS, s // tq),
        in_specs=[pl.BlockSpec((None, tq, 2 * LANE), lambda h, i: (h, i, 0)),
                  pl.BlockSpec((None, s, 2 * LANE), lambda h, i: (h, 0, 0)),
                  pl.BlockSpec((None, s, LANE), lambda h, i: (h, 0, 0))],
        out_specs=[pl.BlockSpec((tq, LANE), lambda h, i: (i, h)), pl.BlockSpec((tq, LANE), lambda h, i: (i, h)),
                   pl.BlockSpec((None, 1, tq), lambda h, i: (h, 0, i))],
        out_shape=[_sds((s, MLA_W), F32), _sds((s, MLA_W), BF), _sds((HEADS, 1, s), F32)],
        scratch_shapes=[pltpu.VMEM((tq, s + LANE), F32), pltpu.VMEM((tq, LANE), F32)],
        compiler_params=_cp(("arbitrary", "arbitrary")),
    )(qc, kc, v)


def _halo_specs(tm, s, width, col):
    r = tm // HALO
    nb = s // HALO
    cur = pl.BlockSpec((tm, width), lambda i: (i, col))
    prev = pl.BlockSpec((HALO, width), lambda i: (jnp.maximum(i * r - 1, 0), col))
    nxt = pl.BlockSpec((HALO, width), lambda i: (jnp.minimum((i + 1) * r, nb - 1), col))
    return cur, prev, nxt


def _slab_shapes(tm, c):
    return (tm + 2 * HALO, c + LANE), (SUB - 1, tm + 2 * HALO - SUB, c + LANE)


def _fill_slab(slab, tm, prev, cur, nxt):
    i = pl.program_id(0)
    last = pl.num_programs(0) - 1
    c = cur.shape[1]
    slab[0:HALO, 0:c] = jnp.where(i > 0, prev, 0.0)
    slab[HALO:HALO + tm, 0:c] = cur
    slab[HALO + tm:2 * HALO + tm, 0:c] = jnp.where(i < last, nxt, 0.0)


def _rotate_slab(slab, rot, tm):
    rows = tm + 2 * HALO - SUB
    c = slab.shape[1] - LANE
    for b in range(1, SUB):
        rot[b - 1, :, 0:c] = slab[pl.ds(b, rows), 0:c]


def _shifted(slab, rot, start, rc, cs):
    b = start % SUB
    if b == 0:
        return slab[pl.ds(start, rc), cs]
    return rot[b - 1, pl.ds(start - b, rc), cs]


def conv_fwd(h, conv_w, conv_b, g_ln, b_ln):
    s = h.shape[0]
    c = conv_w.shape[1]
    tm = _tile(s, 256)
    rc = _tile(tm, 64)

    def body(a_ref, ap_ref, an_ref, g_ref, gp_ref, gn_ref, w_ref, cb_ref, lg_ref, lb_ref, co_ref, uc_ref, slab, rot):
        _fill_slab(slab, tm, ap_ref[...] * _sigmoid(gp_ref[...]), a_ref[...] * _sigmoid(g_ref[...]),
                   an_ref[...] * _sigmoid(gn_ref[...]))
        _rotate_slab(slab, rot, tm)

        def lane_block(cb, carry):
            cs = pl.ds(pl.multiple_of(cb * LANE, LANE), LANE)
            for r0 in range(0, tm, rc):
                acc = jnp.zeros((rc, LANE), F32)
                for k in range(CONV_K):
                    acc = acc + w_ref[k:k + 1, cs] * _shifted(slab, rot, r0 + HALO - CONV_PAD + k, rc, cs)
                uc_ref[r0:r0 + rc, cs] = acc + cb_ref[:, cs]
            return carry

        lax.fori_loop(0, c // LANE, lane_block, 0)
        xhat, _ = _ln_stats(uc_ref[...])
        cl = xhat * lg_ref[...] + lb_ref[...]
        co_ref[...] = (cl * _sigmoid(cl)).astype(BF)

    a_specs = _halo_specs(tm, s, c, 1)
    g_specs = _halo_specs(tm, s, c, 2)
    row = pl.BlockSpec((1, c), lambda i: (0, 0))
    tok = pl.BlockSpec((tm, c), lambda i: (i, 0))
    return _call(
        body, name="conv_fwd", grid=(s // tm,),
        in_specs=[*a_specs, *g_specs, pl.BlockSpec(conv_w.shape, lambda i: (0, 0)), row, row, row],
        out_specs=[tok, tok], out_shape=[_sds((s, c), BF), _sds((s, c), F32)],
        scratch_shapes=[pltpu.VMEM(shp, F32) for shp in _slab_shapes(tm, c)],
        compiler_params=_cp(("arbitrary",)),
    )(h, h, h, h, h, h, conv_w, conv_b, g_ln, b_ln)


def out_proj_ln1(ob, co, wout, x0, g1, b1):
    s, d = x0.shape
    kh = ob.shape[1]
    tm = _tile(s, 256)

    def body(o_ref, c_ref, w_ref, x_ref, g_ref, b_ref, r_ref, x1_ref, x1b_ref, acc):
        acc[...] = _dot(o_ref[...], w_ref[0:kh, :]) + _dot(c_ref[...], w_ref[kh:2 * kh, :])
        g = g_ref[...]
        b = b_ref[...]

        def chunk(rows):
            r = ALPHA * x_ref[rows, :] + acc[rows, :]
            r_ref[rows, :] = r
            xhat, _ = _ln_stats(r)
            y = xhat * g + b
            x1_ref[rows, :] = y
            x1b_ref[rows, :] = y.astype(BF)

        _row_chunks(tm, chunk)

    half = pl.BlockSpec((tm, kh), lambda i: (i, 0))
    tok = pl.BlockSpec((tm, d), lambda i: (i, 0))
    row = pl.BlockSpec((1, d), lambda i: (0, 0))
    return _call(
        body, name="out_proj_ln1", grid=(s // tm,),
        in_specs=[half, half, pl.BlockSpec((2 * kh, d), lambda i: (0, 0)), tok, row, row],
        out_specs=[tok, tok, tok], out_shape=[_sds((s, d), F32), _sds((s, d), F32), _sds((s, d), BF)],
        scratch_shapes=[pltpu.VMEM((tm, d), F32)], compiler_params=_cp(("arbitrary",)),
    )(ob, co, wout, x0, g1, b1)


def ff1_fwd(x1b, wff1_g):
    s, d = x1b.shape
    nsh, _, fs = wff1_g.shape
    tm = _tile(s, 1024)
    tn = _tile(fs, 2048)
    per = fs // tn

    def body(a_ref, w_ref, r_ref, a1_ref):
        r = jnp.maximum(_dot(a_ref[...], w_ref[...]), 0.0)
        r_ref[...] = r.astype(BF)
        a1_ref[...] = (r * r).astype(BF)

    out = pl.BlockSpec((tm, tn), lambda i, j: (i, j))
    return _call(
        body, name="ff1_fwd", grid=(s // tm, nsh * per),
        in_specs=[pl.BlockSpec((tm, d), lambda i, j: (i, 0)),
                  pl.BlockSpec((None, d, tn), lambda i, j: (j // per, 0, j % per))],
        out_specs=[out, out], out_shape=[_sds((s, nsh * fs), BF)] * 2,
        compiler_params=_cp(("arbitrary", "arbitrary")),
    )(x1b, wff1_g)


def ff2_ln2_loss(a1b, wff2, x1, target, g2, b2):
    s, f = a1b.shape
    d = x1.shape[1]
    tm = _tile(s, 512)
    tk = _tile(f, 2048)
    nk = f // tk

    def body(a_ref, w_ref, x_ref, t_ref, g_ref, b_ref, dr_ref, drb_ref, loss_ref, dg_ref, db_ref, acc):
        i = pl.program_id(0)
        k = pl.program_id(1)

        @pl.when(k == 0)
        def _():
            acc[...] = _dot(a_ref[...], w_ref[...])

        @pl.when(k > 0)
        def _():
            acc[...] += _dot(a_ref[...], w_ref[...])

        @pl.when(jnp.logical_and(i == 0, k == 0))
        def _():
            loss_ref[...] = jnp.zeros_like(loss_ref)
            dg_ref[...] = jnp.zeros_like(dg_ref)
            db_ref[...] = jnp.zeros_like(db_ref)

        @pl.when(k == nk - 1)
        def _():
            g = g_ref[...]

            def chunk(rows):
                r = ALPHA * x_ref[rows, :] + acc[rows, :]
                xhat, rstd = _ln_stats(r)
                e = xhat * g + b_ref[...] - t_ref[rows, :]
                e2 = _rows8(e * e)
                part = e2[:, 0:LANE]
                for c in range(1, d // LANE):
                    part = part + e2[:, LANE * c:LANE * (c + 1)]
                loss_ref[...] += part * (0.5 / d)
                dy = e * (1.0 / d)
                dg_ref[...] += _rows8(dy * xhat)
                db_ref[...] += _rows8(dy)
                dr = _ln_bwd(dy, xhat, rstd, g)
                dr_ref[rows, :] = dr
                drb_ref[rows, :] = dr.astype(BF)

            _row_chunks(tm, chunk)

    tok = pl.BlockSpec((tm, d), lambda i, k: (i, 0))
    row = pl.BlockSpec((1, d), lambda i, k: (0, 0))
    accs = pl.BlockSpec((SUB, d), lambda i, k: (0, 0))
    return _call(
        body, name="ff2_ln2_loss", grid=(s // tm, nk),
        in_specs=[pl.BlockSpec((tm, tk), lambda i, k: (i, k)), pl.BlockSpec((tk, d), lambda i, k: (k, 0)),
                  tok, tok, row, row],
        out_specs=[tok, tok, pl.BlockSpec((SUB, LANE), lambda i, k: (0, 0)), accs, accs],
        out_shape=[_sds((s, d), F32), _sds((s, d), BF), _sds((SUB, LANE), F32), _sds((SUB, d), F32), _sds((SUB, d), F32)],
        scratch_shapes=[pltpu.VMEM((tm, d), F32)], compiler_params=_cp(("arbitrary", "arbitrary"), 60),
    )(a1b, wff2, x1, target, g2, b2)


def ff2_bwd_act(dr2b, wff2, rb):
    s, d = dr2b.shape
    f = wff2.shape[0]
    tm = _tile(s, 1024)
    tn = _tile(f, 2048)

    def body(a_ref, w_ref, r_ref, o_ref):
        o_ref[...] = (_dot_nt(a_ref[...], w_ref[...]) * (2.0 * r_ref[...].astype(F32))).astype(BF)

    return _call(
        body, name="ff2_bwd_act", grid=(s // tm, f // tn),
        in_specs=[pl.BlockSpec((tm, d), lambda i, j: (i, 0)), pl.BlockSpec((tn, d), lambda i, j: (j, 0)),
                  pl.BlockSpec((tm, tn), lambda i, j: (i, j))],
        out_specs=pl.BlockSpec((tm, tn), lambda i, j: (i, j)), out_shape=_sds((s, f), BF),
        compiler_params=_cp(("arbitrary", "arbitrary")),
    )(dr2b, wff2, rb)


def wgrad(name, a, b, tm, tn, tk=2048, shards=1):
    s, m = a.shape
    n = b.shape[1]
    tm = _tile(m, tm)
    ns = n // shards
    tn = _tile(ns, tn)
    tk = _tile(s, tk)
    per = ns // tn

    def body(a_ref, b_ref, o_ref):
        k = pl.program_id(2)

        @pl.when(k == 0)
        def _():
            o_ref[...] = _dot_tn(a_ref[...], b_ref[...])

        @pl.when(k > 0)
        def _():
            o_ref[...] += _dot_tn(a_ref[...], b_ref[...])

    return _call(
        body, name=name, grid=(m // tm, n // tn, s // tk),
        in_specs=[pl.BlockSpec((tk, tm), lambda i, j, k: (k, i)), pl.BlockSpec((tk, tn), lambda i, j, k: (k, j))],
        out_specs=pl.BlockSpec((None, tm, tn), lambda i, j, k: (j // per, i, j % per)),
        out_shape=_sds((shards, m, ns), F32), compiler_params=_cp(("arbitrary", "arbitrary", "arbitrary")),
    )(a, b)


def ff1_bwd_ln1(df1b, wff1_g, dr2, r1, g1):
    s, f = df1b.shape
    d = dr2.shape[1]
    tm = _tile(s, 512)
    tk = _tile(wff1_g.shape[2], 2048)
    per = wff1_g.shape[2] // tk
    nk = f // tk

    def body(a_ref, w_ref, d2_ref, r_ref, g_ref, dr_ref, drb_ref, dg_ref, db_ref, acc):
        i = pl.program_id(0)
        k = pl.program_id(1)

        @pl.when(k == 0)
        def _():
            acc[...] = _dot_nt(a_ref[...], w_ref[...])

        @pl.when(k > 0)
        def _():
            acc[...] += _dot_nt(a_ref[...], w_ref[...])

        @pl.when(jnp.logical_and(i == 0, k == 0))
        def _():
            dg_ref[...] = jnp.zeros_like(dg_ref)
            db_ref[...] = jnp.zeros_like(db_ref)

        @pl.when(k == nk - 1)
        def _():
            g = g_ref[...]

            def chunk(rows):
                dy = ALPHA * d2_ref[rows, :] + acc[rows, :]
                xhat, rstd = _ln_stats(r_ref[rows, :])
                dg_ref[...] += _rows8(dy * xhat)
                db_ref[...] += _rows8(dy)
                dr = _ln_bwd(dy, xhat, rstd, g)
                dr_ref[rows, :] = dr
                drb_ref[rows, :] = dr.astype(BF)

            _row_chunks(tm, chunk)

    tok = pl.BlockSpec((tm, d), lambda i, k: (i, 0))
    accs = pl.BlockSpec((SUB, d), lambda i, k: (0, 0))
    return _call(
        body, name="ff1_bwd_ln1", grid=(s // tm, nk),
        in_specs=[pl.BlockSpec((tm, tk), lambda i, k: (i, k)), pl.BlockSpec((None, d, tk), lambda i, k: (k // per, 0, k % per)),
                  tok, tok, pl.BlockSpec((1, d), lambda i, k: (0, 0))],
        out_specs=[tok, tok, accs, accs],
        out_shape=[_sds((s, d), F32), _sds((s, d), BF), _sds((SUB, d), F32), _sds((SUB, d), F32)],
        scratch_shapes=[pltpu.VMEM((tm, d), F32)], compiler_params=_cp(("arbitrary", "arbitrary"), 60),
    )(df1b, wff1_g, dr2, r1, g1)


def out_proj_bwd(dr1b, woutt, o):
    s, d = dr1b.shape
    tm = _tile(s, 256)

    def body(a_ref, w_ref, o_ref, do_ref, dot_ref, dc_ref, dl_ref):
        dcat = _dot(a_ref[...], w_ref[...])
        do = dcat[:, 0:MLA_W]
        do_ref[...] = do.astype(BF)
        dc_ref[...] = dcat[:, MLA_W:]
        prod = do * o_ref[...]
        for hd in range(HEADS):
            hs = slice(LANE * hd, LANE * (hd + 1))
            dl_ref[hd] = _as_row(jnp.sum(prod[:, hs], axis=-1, keepdims=True))
            dot_ref[hd] = do[:, hs].T.astype(BF)

    half = pl.BlockSpec((tm, MLA_W), lambda i: (i, 0))
    return _call(
        body, name="out_proj_bwd", grid=(s // tm,),
        in_specs=[pl.BlockSpec((tm, d), lambda i: (i, 0)), pl.BlockSpec((d, d), lambda i: (0, 0)), half],
        out_specs=[half, pl.BlockSpec((HEADS, LANE, tm), lambda i: (0, 0, i)),
                   pl.BlockSpec((tm, d - MLA_W), lambda i: (i, 0)), pl.BlockSpec((HEADS, 1, tm), lambda i: (0, 0, i))],
        out_shape=[_sds((s, MLA_W), BF), _sds((HEADS, LANE, s), BF), _sds((s, d - MLA_W), F32), _sds((HEADS, 1, s), F32)],
        compiler_params=_cp(("arbitrary",)),
    )(dr1b, woutt, o)


def conv_bwd_ln(uc, dco, g_ln, b_ln):
    s, c = uc.shape
    tm = _tile(s, 512)

    def body(u_ref, d_ref, g_ref, b_ref, du_ref, dg_ref, db_ref, dcb_ref):
        @pl.when(pl.program_id(0) == 0)
        def _():
            dg_ref[...] = jnp.zeros_like(dg_ref)
            db_ref[...] = jnp.zeros_like(db_ref)
            dcb_ref[...] = jnp.zeros_like(dcb_ref)

        xhat, rstd = _ln_stats(u_ref[...])
        g = g_ref[...]
        cl = xhat * g + b_ref[...]
        sg = _sigmoid(cl)
        dcl = d_ref[...] * (sg * (1.0 + cl * (1.0 - sg)))
        dg_ref[...] += _rows8(dcl * xhat)
        db_ref[...] += _rows8(dcl)
        du = _ln_bwd(dcl, xhat, rstd, g)
        du_ref[...] = du
        dcb_ref[...] += _rows8(du)

    tok = pl.BlockSpec((tm, c), lambda i: (i, 0))
    row = pl.BlockSpec((1, c), lambda i: (0, 0))
    accs = pl.BlockSpec((SUB, c), lambda i: (0, 0))
    return _call(
        body, name="conv_bwd_ln", grid=(s // tm,), in_specs=[tok, tok, row, row], out_specs=[tok, accs, accs, accs],
        out_shape=[_sds((s, c), F32)] + [_sds((SUB, c), F32)] * 3, compiler_params=_cp(("arbitrary",)),
    )(uc, dco, g_ln, b_ln)


def conv_bwd_taps(h, duc, conv_w):
    s, c = duc.shape
    tm = _tile(s, 256)
    rc = _tile(tm, 64)

    def body(a_ref, ap_ref, an_ref, g_ref, gp_ref, gn_ref, d_ref, dp_ref, dn_ref, w_ref, o_ref, dw_ref,
             uslab, dslab, du_s, urot, drot, dw8):
        @pl.when(pl.program_id(0) == 0)
        def _():
            dw8[...] = jnp.zeros_like(dw8)

        sg = _sigmoid(g_ref[...])
        a = a_ref[...]
        _fill_slab(uslab, tm, ap_ref[...] * _sigmoid(gp_ref[...]), a * sg, an_ref[...] * _sigmoid(gn_ref[...]))
        _fill_slab(dslab, tm, dp_ref[...], d_ref[...], dn_ref[...])
        _rotate_slab(uslab, urot, tm)
        _rotate_slab(dslab, drot, tm)

        def lane_block(cb, carry):
            cs = pl.ds(pl.multiple_of(cb * LANE, LANE), LANE)
            for r0 in range(0, tm, rc):
                acc = jnp.zeros((rc, LANE), F32)
                for k in range(CONV_K):
                    acc = acc + w_ref[k:k + 1, cs] * _shifted(dslab, drot, r0 + HALO + CONV_PAD - k, rc, cs)
                du_s[r0:r0 + rc, cs] = acc
            return carry

        def lane_block_taps(cb, carry):
            cs = pl.ds(pl.multiple_of(cb * LANE, LANE), LANE)
            parts = []
            for k in range(CONV_K):
                prod = None
                for r0 in range(0, tm, rc):
                    t = dslab[pl.ds(r0 + HALO, rc), cs] * _shifted(uslab, urot, r0 + HALO - CONV_PAD + k, rc, cs)
                    prod = t if prod is None else prod + t
                parts.append(_rows8(prod))
            rows = SUB * CONV_K
            dw8[0:rows, cs] = dw8[0:rows, cs] + jnp.concatenate(parts, axis=0)
            return carry

        lax.fori_loop(0, c // LANE, lane_block, 0)
        lax.fori_loop(0, c // LANE, lane_block_taps, 0)

        @pl.when(pl.program_id(0) == pl.num_programs(0) - 1)
        def _():
            dw_ref[...] = jnp.zeros_like(dw_ref)
            for k in range(CONV_K):
                dw_ref[k:k + 1, :] = jnp.sum(dw8[SUB * k:SUB * (k + 1), :], axis=0, keepdims=True)

        du = du_s[...]
        o_ref[:, 0:c] = (du * sg).astype(BF)
        o_ref[:, c:2 * c] = (du * a * sg * (1.0 - sg)).astype(BF)

    a_specs = _halo_specs(tm, s, c, 1)
    g_specs = _halo_specs(tm, s, c, 2)
    d_specs = _halo_specs(tm, s, c, 0)
    wsp = pl.BlockSpec(conv_w.shape, lambda i: (0, 0))
    return _call(
        body, name="conv_bwd_taps", grid=(s // tm,), in_specs=[*a_specs, *g_specs, *d_specs, wsp],
        out_specs=[pl.BlockSpec((tm, 2 * c), lambda i: (i, 0)), wsp],
        out_shape=[_sds((s, 2 * c), BF), _sds(conv_w.shape, F32)],
        scratch_shapes=[pltpu.VMEM(_slab_shapes(tm, c)[0], F32), pltpu.VMEM(_slab_shapes(tm, c)[0], F32), pltpu.VMEM((tm, c), F32),
                        pltpu.VMEM(_slab_shapes(tm, c)[1], F32), pltpu.VMEM(_slab_shapes(tm, c)[1], F32),
                        pltpu.VMEM((SUB * conv_w.shape[0], c), F32)],
        compiler_params=_cp(("arbitrary",)),
    )(h, h, h, h, h, h, duc, duc, duc, conv_w)


def attn_bwd(qc, kc, kct, v, dob, dot, lse_r, delta_r):
    _, s, _ = qc.shape
    tk = _tile(s, 1024)
    tq = _tile(s, 512)
    scale = D_QK ** -0.5
    c2 = scale * LOG2E

    def body(k_ref, kt_ref, v_ref, q_ref, do_ref, dot_ref, l_ref, dl_ref, dqt_ref, dk_ref, dvt_ref):
        @pl.when(pl.program_id(1) == 0)
        def _():
            dqt_ref[...] = jnp.zeros_like(dqt_ref)

        k = k_ref[...]
        kt = kt_ref[...]
        vv = v_ref[...]

        def step(i, carry):
            dk, dvt = carry
            off = pl.multiple_of(i * tq, tq)
            q = q_ref[pl.ds(off, tq), :]
            do = do_ref[pl.ds(off, tq), :]
            pt = jnp.exp2(_dot_nt(k, q) * c2 - l_ref[:, pl.ds(off, tq)])
            dvt = dvt + _dot_nt(dot_ref[:, pl.ds(off, tq)], pt.astype(BF))
            dpt = _dot_nt(vv, do)
            dsb = (pt * (dpt - dl_ref[:, pl.ds(off, tq)]) * scale).astype(BF)
            dk = dk + _dot(dsb, q)
            dqt_ref[:, pl.ds(off, tq)] += _dot(kt, dsb)
            return dk, dvt

        dk, dvt = _unrolled_loop(s // tq, 16, step, (jnp.zeros((tk, 2 * LANE), F32), jnp.zeros((LANE, tk), F32)))
        dk_ref[...] = dk
        dvt_ref[...] = dvt

    rowv = pl.BlockSpec((None, 1, s), lambda h, j: (h, 0, 0))
    return _call(
        body, name="attn_bwd", grid=(HEADS, s // tk),
        in_specs=[pl.BlockSpec((None, tk, 2 * LANE), lambda h, j: (h, j, 0)),
                  pl.BlockSpec((None, 2 * LANE, tk), lambda h, j: (h, 0, j)),
                  pl.BlockSpec((None, tk, LANE), lambda h, j: (h, j, 0)),
                  pl.BlockSpec((None, s, 2 * LANE), lambda h, j: (h, 0, 0)),
                  pl.BlockSpec((s, LANE), lambda h, j: (0, h)),
                  pl.BlockSpec((None, LANE, s), lambda h, j: (h, 0, 0)), rowv, rowv],
        out_specs=[pl.BlockSpec((None, 2 * LANE, s), lambda h, j: (h, 0, 0)),
                   pl.BlockSpec((None, tk, 2 * LANE), lambda h, j: (h, j, 0)),
                   pl.BlockSpec((None, LANE, tk), lambda h, j: (h, 0, j))],
        out_shape=[_sds((HEADS, 2 * LANE, s), F32), _sds((HEADS, s, 2 * LANE), F32), _sds((HEADS, LANE, s), F32)],
        compiler_params=_cp(("arbitrary", "arbitrary"), 56),
    )(kc, kct, v, qc, dob, dot, lse_r, delta_r)


def q_bwd(dqt, h, g_cq, wuqt, cos, sin):
    s = h.shape[0]
    tm = _tile(s, 256)

    def body(d_ref, h_ref, g_ref, w_ref, c_ref, s_ref, dq_ref, dc_ref, dg_ref):
        @pl.when(pl.program_id(0) == 0)
        def _():
            dg_ref[...] = jnp.zeros_like(dg_ref)

        c = c_ref[...]
        sn = s_ref[...]
        for hd in range(HEADS):
            t = d_ref[hd].T
            dq_ref[:, LANE * hd:LANE * (hd + 1)] = t[:, 0:LANE].astype(BF)
            dq_ref[:, MLA_W + LANE * hd:MLA_W + LANE * (hd + 1)] = _unrope128(t[:, LANE:2 * LANE], c, sn).astype(BF)
        dy = _dot(dq_ref[...], w_ref[...])
        g = g_ref[...]
        _, xh, rr = _rms_fwd(h_ref[...], g)
        dg_ref[...] += _rows8(dy * xh)
        dc_ref[...] = _rms_bwd(dy, xh, rr, g).astype(BF)

    tab = pl.BlockSpec((tm, LANE), lambda i: (i, 0))
    return _call(
        body, name="q_bwd", grid=(s // tm,),
        in_specs=[pl.BlockSpec((HEADS, 2 * LANE, tm), lambda i: (0, 0, i)), pl.BlockSpec((tm, R_Q), lambda i: (i, 0)),
                  pl.BlockSpec((1, R_Q), lambda i: (0, 0)), pl.BlockSpec((2 * MLA_W, R_Q), lambda i: (0, 0)), tab, tab],
        out_specs=[pl.BlockSpec((tm, 2 * MLA_W), lambda i: (i, 0)), pl.BlockSpec((tm, R_Q), lambda i: (i, 0)),
                   pl.BlockSpec((SUB, R_Q), lambda i: (0, 0))],
        out_shape=[_sds((s, 2 * MLA_W), BF), _sds((s, R_Q), BF), _sds((SUB, R_Q), F32)],
        compiler_params=_cp(("arbitrary",)),
    )(dqt, h, g_cq, wuqt, cos, sin)


def kv_bwd(dk, dv, h, g_ckv, wukt, wuvt, cos, sin):
    s = h.shape[0]
    tm = _tile(s, 256)

    def body(dk_ref, dv_ref, h_ref, g_ref, wk_ref, wv_ref, c_ref, s_ref, dkn_ref, dvb_ref, dc_ref, dkr_ref, dg_ref):
        @pl.when(pl.program_id(0) == 0)
        def _():
            dg_ref[...] = jnp.zeros_like(dg_ref)

        dkr = dk_ref[0, :, LANE:2 * LANE]
        for hd in range(HEADS):
            dkn_ref[:, LANE * hd:LANE * (hd + 1)] = dk_ref[hd, :, 0:LANE].astype(BF)
            dvb_ref[:, LANE * hd:LANE * (hd + 1)] = dv_ref[hd].T.astype(BF)
            if hd > 0:
                dkr = dkr + dk_ref[hd, :, LANE:2 * LANE]
        dkr_ref[...] = _unrope128(dkr, c_ref[...], s_ref[...]).astype(BF)
        dy = _dot(dkn_ref[...], wk_ref[...]) + _dot(dvb_ref[...], wv_ref[...])
        g = g_ref[...]
        _, xh, rr = _rms_fwd(h_ref[...], g)
        dg_ref[...] += _rows8(dy * xh)
        dc_ref[...] = _rms_bwd(dy, xh, rr, g).astype(BF)

    tab = pl.BlockSpec((tm, LANE), lambda i: (i, 0))
    wsp = pl.BlockSpec((MLA_W, R_KV), lambda i: (0, 0))
    wide = pl.BlockSpec((tm, MLA_W), lambda i: (i, 0))
    return _call(
        body, name="kv_bwd", grid=(s // tm,),
        in_specs=[pl.BlockSpec((HEADS, tm, 2 * LANE), lambda i: (0, i, 0)), pl.BlockSpec((HEADS, LANE, tm), lambda i: (0, 0, i)),
                  pl.BlockSpec((tm, R_KV), lambda i: (i, 1)), pl.BlockSpec((1, R_KV), lambda i: (0, 0)), wsp, wsp, tab, tab],
        out_specs=[wide, wide, pl.BlockSpec((tm, R_KV), lambda i: (i, 0)), tab, pl.BlockSpec((SUB, R_KV), lambda i: (0, 0))],
        out_shape=[_sds((s, MLA_W), BF), _sds((s, MLA_W), BF), _sds((s, R_KV), BF), _sds((s, LANE), BF), _sds((SUB, R_KV), F32)],
        compiler_params=_cp(("arbitrary",)),
    )(dk, dv, h, g_ckv, wukt, wuvt, cos, sin)


def in_proj_bwd_ln(dh, wint, dr1, x, g_in):
    s, hc = dh.shape
    d = x.shape[1]
    tm = _tile(s, 256)

    def body(a_ref, w_ref, d1_ref, x_ref, g_ref, gx_ref, dg_ref, db_ref, acc):
        @pl.when(pl.program_id(0) == 0)
        def _():
            dg_ref[...] = jnp.zeros_like(dg_ref)
            db_ref[...] = jnp.zeros_like(db_ref)

        acc[...] = _dot(a_ref[...], w_ref[...])
        g = g_ref[...]

        def chunk(rows):
            dy = ALPHA * d1_ref[rows, :] + acc[rows, :]
            xhat, rstd = _ln_stats(x_ref[rows, :])
            dg_ref[...] += _rows8(dy * xhat)
            db_ref[...] += _rows8(dy)
            gx_ref[rows, :] = _ln_bwd(dy, xhat, rstd, g)

        _row_chunks(tm, chunk)

    tok = pl.BlockSpec((tm, d), lambda i: (i, 0))
    accs = pl.BlockSpec((SUB, d), lambda i: (0, 0))
    return _call(
        body, name="in_proj_bwd_ln", grid=(s // tm,),
        in_specs=[pl.BlockSpec((tm, hc), lambda i: (i, 0)), pl.BlockSpec((hc, d), lambda i: (0, 0)),
                  tok, tok, pl.BlockSpec((1, d), lambda i: (0, 0))],
        out_specs=[tok, accs, accs], out_shape=[_sds((s, d), F32), _sds((SUB, d), F32), _sds((SUB, d), F32)],
        scratch_shapes=[pltpu.VMEM((tm, d), F32)], compiler_params=_cp(("arbitrary",), 56),
    )(dh, wint, dr1, x, g_in)


def _adamw_math(w, g, m, v):
    m = ADAM_B1 * m + (1.0 - ADAM_B1) * g
    v = ADAM_B2 * v + (1.0 - ADAM_B2) * (g * g)
    m_hat = m / (1.0 - ADAM_B1 ** ADAM_STEP)
    v_hat = v / (1.0 - ADAM_B2 ** ADAM_STEP)
    delta = -ADAM_LR * (m_hat / (jnp.sqrt(v_hat) + ADAM_EPS) + ADAM_WD * w)
    return delta, m, v


def adamw(name, w, g, m, v):
    r, c = w.shape
    tr = _row_tile(r, c)

    def body(w_ref, g_ref, m_ref, v_ref, d_ref, mo_ref, vo_ref):
        d_ref[...], mo_ref[...], vo_ref[...] = _adamw_math(w_ref[...], g_ref[...], m_ref[...], v_ref[...])

    blk = pl.BlockSpec((tr, c), lambda i: (i, 0))
    return _call(
        body, name=name, grid=(r // tr,), in_specs=[blk] * 4, out_specs=[blk] * 3,
        out_shape=[_sds((r, c), F32)] * 3, compiler_params=_cp(("arbitrary",)),
    )(w, g, m, v)


def _coords():
    return lax.axis_index("x"), lax.axis_index("y"), lax.axis_index("c")


def _other_chips(x, y):
    return [(1 - x, y, 2 * (1 - x) + y), (x, 1 - y, 2 * x + 1 - y), (1 - x, 1 - y, 2 * (1 - x) + 1 - y)]


ANY = pl.BlockSpec(memory_space=pl.ANY)
HBM = pl.BlockSpec(memory_space=pltpu.HBM)
SEM = pl.BlockSpec(memory_space=pltpu.SEMAPHORE)
EFFECT = pltpu.SideEffectType.DATAFLOW_SIDE_EFFECTING


def _in_hbm(a):
    return pltpu.with_memory_space_constraint(a, pltpu.HBM)


def _split_plan(mode, src, land, x, y, c):
    if mode == "pair":
        rh = src.shape[1] // 2
        return [((x, y, 1 - c), src.at[:, pl.ds((1 - c) * rh, rh)], land, land)]
    me = 2 * x + y
    plan = []
    for j, (px, py, pk) in enumerate(_other_chips(x, y)):
        if mode == "gather":
            plan.append(((px, py, c), src, land.at[me], land.at[pk]))
        elif mode == "gather_half":
            mine = pl.ds(c * (src.shape[0] // 2), src.shape[0] // 2)
            plan.append(((px, py, c), src.at[mine], land.at[me, mine], land.at[pk, mine]))
        else:
            plan.append(((px, py, c), src.at[pk], land.at[j], land.at[j]))
    return plan


def _plan_len(mode):
    return 1 if mode == "pair" else N_CHIP - 1


def split_send_start(name, mode, srcs, land_shapes, order_after):
    n = len(srcs)
    np_ = _plan_len(mode)

    def body(*refs):
        ins, lands = refs[:n], refs[n:2 * n]
        ss, rs = refs[2 * n + 1], refs[2 * n + 2]
        token = refs[-1]
        x, y, c = _coords()
        for a in range(n):
            for j, (peer, src, dst, _) in enumerate(_split_plan(mode, ins[a], lands[a], x, y, c)):
                pltpu.make_async_remote_copy(src_ref=src, dst_ref=dst, send_sem=ss.at[np_ * a + j], recv_sem=rs.at[np_ * a + j],
                                             device_id=peer, device_id_type=MESH).start()
        token[...] = jnp.zeros_like(token)

    lands = [lax.empty(shp, s.dtype) for shp, s in zip(land_shapes, srcs)]
    outs = _call(
        body, name=name,
        out_shape=(pltpu.SemaphoreType.DMA((np_ * n,)), pltpu.SemaphoreType.DMA((np_ * n,)),
                   *[pltpu.HBM(s.shape, s.dtype) for s in srcs], *[pltpu.HBM(l.shape, l.dtype) for l in lands],
                   _sds((SUB, LANE), F32)),
        in_specs=[HBM] * (2 * n) + [ANY], out_specs=(SEM, SEM, *[HBM] * (2 * n), pl.BlockSpec(memory_space=pltpu.VMEM)),
        input_output_aliases={a: 2 + a for a in range(2 * n)},
        compiler_params=pltpu.CompilerParams(has_side_effects=EFFECT),
    )(*[_in_hbm(s) for s in srcs], *[_in_hbm(l) for l in lands], order_after)
    return outs[0], outs[1], list(outs[2:2 + n]), list(outs[2 + n:2 + 2 * n]), outs[-1]


def split_send_wait(name, mode, ss, rs, srcs, lands, order_after):
    n = len(srcs)
    np_ = _plan_len(mode)

    def body(*refs):
        ins, lnd = refs[:n], refs[n:2 * n]
        s_ref, r_ref = refs[2 * n], refs[2 * n + 1]
        x, y, c = _coords()
        for a in range(n):
            for j, (peer, src, _, got) in enumerate(_split_plan(mode, ins[a], lnd[a], x, y, c)):
                cp = pltpu.make_async_remote_copy(src_ref=src, dst_ref=got, send_sem=s_ref.at[np_ * a + j], recv_sem=r_ref.at[np_ * a + j],
                                                  device_id=peer, device_id_type=MESH)
                cp.wait_send()
                cp.wait_recv()

    outs = _call(
        body, name=name, out_shape=tuple(pltpu.HBM(t.shape, t.dtype) for t in (*srcs, *lands)),
        in_specs=[HBM] * (2 * n) + [SEM, SEM, ANY], out_specs=tuple([HBM] * (2 * n)),
        input_output_aliases={a: a for a in range(2 * n)},
        compiler_params=pltpu.CompilerParams(has_side_effects=EFFECT),
    )(*srcs, *lands, ss, rs, order_after)
    return list(outs[:n]), list(outs[n:])


def swap_gathered_halves(lands):
    n = len(lands)

    def body(*refs):
        outs = refs[n:2 * n]
        ss, rs = refs[2 * n:]
        x, y, c = _coords()
        cps = []
        for a in range(n):
            rh = outs[a].shape[1] // 2
            for j, (px, py, pk) in enumerate(_other_chips(x, y)):
                held = outs[a].at[pk, pl.ds(c * rh, rh)]
                cp = pltpu.make_async_remote_copy(src_ref=held, dst_ref=held, send_sem=ss.at[a, j], recv_sem=rs.at[a, j],
                                                  device_id=(x, y, 1 - c), device_id_type=MESH)
                cp.start()
                cps.append(cp)
        for a in range(n):
            rh = outs[a].shape[1] // 2
            for j, (px, py, pk) in enumerate(_other_chips(x, y)):
                theirs = outs[a].at[pk, pl.ds((1 - c) * rh, rh)]
                pltpu.make_async_remote_copy(src_ref=theirs, dst_ref=theirs, send_sem=ss.at[a, j], recv_sem=rs.at[a, j],
                                             device_id=(x, y, 1 - c), device_id_type=MESH).wait_recv()
        for cp in cps:
            cp.wait_send()

    return _call(
        body, name="swap_gathered_halves", in_specs=[ANY] * n, out_specs=[ANY] * n,
        out_shape=[_sds(l.shape, l.dtype) for l in lands], input_output_aliases={a: a for a in range(n)},
        scratch_shapes=[pltpu.SemaphoreType.DMA((n, 3))] * 2,
    )(*lands)


def pair_exchange(grads, tag):
    n = len(grads)

    def body(*refs):
        ins, outs = refs[:n], refs[n:2 * n]
        ss, rs = refs[2 * n:]
        x, y, c = _coords()
        cps = []
        for a in range(n):
            rh = ins[a].shape[1] // 2
            cp = pltpu.make_async_remote_copy(
                src_ref=ins[a].at[:, pl.ds((1 - c) * rh, rh)], dst_ref=outs[a], send_sem=ss.at[a], recv_sem=rs.at[a],
                device_id=(x, y, 1 - c), device_id_type=MESH)
            cp.start()
            cps.append(cp)
        for cp in cps:
            cp.wait()

    return _call(
        body, name="pair_exchange_" + tag, in_specs=[ANY] * n, out_specs=[ANY] * n,
        out_shape=[_sds((N_CHIP, g.shape[1] // 2, g.shape[2]), F32) for g in grads],
        scratch_shapes=[pltpu.SemaphoreType.DMA((n,))] * 2,
    )(*grads)


def _row_tile(rows, cols, itemsize=4, budget=2 * VMEM_MB):
    fits = [t for t in range(SUB, rows + 1, SUB) if rows % t == 0 and t * cols * itemsize <= budget]
    return max(fits) if fits and rows * cols * itemsize > budget else rows


def pair_add(g, r, cidx):
    _, rows, cols = g.shape
    rh = rows // 2
    tr = _row_tile(rh, cols)
    per = rh // tr

    def body(c_ref, g_ref, r_ref, o_ref):
        o_ref[...] = g_ref[...] + r_ref[...]

    return _call(
        body, name="pair_add",
        grid_spec=pltpu.PrefetchScalarGridSpec(
            num_scalar_prefetch=1, grid=(N_CHIP, per),
            in_specs=[pl.BlockSpec((None, tr, cols), lambda k, i, c: (k, c[0] * per + i, 0)),
                      pl.BlockSpec((None, tr, cols), lambda k, i, c: (k, i, 0))],
            out_specs=pl.BlockSpec((None, tr, cols), lambda k, i, c: (k, i, 0))),
        out_shape=_sds((N_CHIP, rh, cols), F32), compiler_params=_cp(("arbitrary", "arbitrary")),
    )(cidx, g, r)


def chip_add(p, r, kc):
    _, rh, cols = p.shape
    tr = _row_tile(rh, cols)
    per = rh // tr

    def body(k_ref, p_ref, r_ref, o_ref):
        o_ref[...] = ((p_ref[...] + r_ref[0]) + r_ref[1]) + r_ref[2]

    return _call(
        body, name="chip_add",
        grid_spec=pltpu.PrefetchScalarGridSpec(
            num_scalar_prefetch=1, grid=(per,),
            in_specs=[pl.BlockSpec((None, tr, cols), lambda i, k: (k[0], i, 0)),
                      pl.BlockSpec((N_CHIP - 1, tr, cols), lambda i, k: (0, i, 0))],
            out_specs=pl.BlockSpec((tr, cols), lambda i, k: (k[1] * per + i, 0))),
        out_shape=_sds((2 * rh, cols), F32), compiler_params=_cp(("arbitrary",)),
    )(kc, p, r)


def pair_share(fulls, tag):
    n = len(fulls)

    def body(*refs):
        outs = refs[n:2 * n]
        ss, rs = refs[2 * n:]
        x, y, c = _coords()
        cps = []
        for a in range(n):
            rh = outs[a].shape[0] // 2
            mine = outs[a].at[pl.ds(c * rh, rh)]
            cp = pltpu.make_async_remote_copy(
                src_ref=mine, dst_ref=mine, send_sem=ss.at[a], recv_sem=rs.at[a],
                device_id=(x, y, 1 - c), device_id_type=MESH)
            cp.start()
            cps.append(cp)
        for a, cp in enumerate(cps):
            rh = outs[a].shape[0] // 2
            theirs = outs[a].at[pl.ds((1 - c) * rh, rh)]
            cp.wait_send()
            pltpu.make_async_remote_copy(
                src_ref=theirs, dst_ref=theirs, send_sem=ss.at[a], recv_sem=rs.at[a],
                device_id=(x, y, 1 - c), device_id_type=MESH).wait_recv()

    return _call(
        body, name="pair_share_" + tag, in_specs=[ANY] * n, out_specs=[ANY] * n,
        out_shape=[_sds(f.shape, F32) for f in fulls], input_output_aliases={a: a for a in range(n)},
        scratch_shapes=[pltpu.SemaphoreType.DMA((n,))] * 2,
    )(*fulls)


def small_allreduce_adamw(part, w, m, v):
    n = part.shape[1]

    def body(p_ref, w_ref, m_ref, v_ref, g_ref, d_ref, mo_ref, vo_ref, mine, gath, ss, rs):
        x, y, c = _coords()
        me = 4 * x + 2 * y + c
        mine[...] = jnp.sum(p_ref[...], axis=0, keepdims=True)
        gath[me] = mine[...]
        cps = []
        for k in range(1, 8):
            px, py, pc = x ^ (k >> 2), y ^ ((k >> 1) & 1), c ^ (k & 1)
            cp = pltpu.make_async_remote_copy(
                src_ref=mine, dst_ref=gath.at[me], send_sem=ss.at[k - 1], recv_sem=rs.at[k - 1],
                device_id=(px, py, pc), device_id_type=MESH)
            cp.start()
            cps.append(cp)
        for k in range(1, 8):
            src = 4 * (x ^ (k >> 2)) + 2 * (y ^ ((k >> 1) & 1)) + (c ^ (k & 1))
            pltpu.make_async_remote_copy(
                src_ref=mine, dst_ref=gath.at[src], send_sem=ss.at[k - 1], recv_sem=rs.at[k - 1],
                device_id=(x, y, c), device_id_type=MESH).wait_recv()
        for cp in cps:
            cp.wait_send()
        g = gath[0]
        for dv in range(1, 8):
            g = g + gath[dv]
        g_ref[...] = g
        d_ref[...], mo_ref[...], vo_ref[...] = _adamw_math(w_ref[...], g, m_ref[...], v_ref[...])

    vm = pl.BlockSpec(memory_space=pltpu.VMEM)
    return _call(
        body, name="small_allreduce_adamw", in_specs=[vm] * 4, out_specs=[vm] * 4, out_shape=[_sds((1, n), F32)] * 4,
        scratch_shapes=[pltpu.VMEM((1, n), F32), pltpu.VMEM((8, 1, n), F32),
                        pltpu.SemaphoreType.DMA((7,)), pltpu.SemaphoreType.DMA((7,))],
    )(part, w, m, v)


def _unshard_cols(g):
    k, r, cs = g.shape
    return g.transpose(1, 0, 2).reshape(r, k * cs)


def _shard_cols(w):
    r, c = w.shape
    return w.reshape(r, N_CHIP, c // N_CHIP).transpose(1, 0, 2)


def local_step(x, positions, ln_in_g, ln_in_b, g_cq, g_ckv, conv_b, g_conv_ln, b_conv_ln, g_ln1, b_ln1, g_ln2, b_ln2,
               target, start_token, hooks):
    s, d = x.shape
    c = d - MLA_W
    row = lambda a: a.reshape(1, -1)
    ln_in_g = row(ln_in_g) + start_token[0:1, 0:1]

    half = D_ROPE // 2
    inv_freq = ROPE_BASE ** (-jnp.arange(half, dtype=F32) * (2.0 / D_ROPE))
    invf = jnp.concatenate([inv_freq, inv_freq, jnp.zeros((LANE - D_ROPE,), F32)]).reshape(1, LANE)
    cos, sin = rope_tables(positions.astype(F32).reshape(s, 1), invf)
    x0, x0b = ln_in_fwd(x, ln_in_g, row(ln_in_b))
    win_g, wuq_g, wuk_g, wuv_g, convw_g = hooks.early_weights(x0b)

    o_kr = R_Q + R_KV
    o_cv = o_kr + D_ROPE
    n_in = o_cv + 2 * c
    per = n_in // N_CHIP

    def in_cols(a, b):
        return [win_g[k, max(a, per * k) - per * k:min(b, per * (k + 1)) - per * k]
                for k in range(N_CHIP) if max(a, per * k) < min(b, per * (k + 1))]

    win_rt = jnp.concatenate(in_cols(0, o_kr) + in_cols(o_cv, n_in) + in_cols(o_kr, o_cv)
                             + [jnp.zeros((LANE - D_ROPE, d), BF)], axis=0)
    kr_blk = (o_kr + 2 * c) // LANE
    wuq = _unshard_cols(wuq_g).reshape(R_Q, HEADS, D_QK)
    wuq_r = jnp.concatenate([wuq[:, :, :D_NOPE].reshape(R_Q, MLA_W),
                             jnp.pad(wuq[:, :, D_NOPE:], ((0, 0), (0, 0), (0, LANE - D_ROPE))).reshape(R_Q, MLA_W)], axis=1)
    wuk = _unshard_cols(wuk_g)
    wuv = _unshard_cols(wuv_g)
    conv_w = jnp.pad(_unshard_cols(convw_g), ((0, 1), (0, 0)))

    h = matmul_nt("in_proj", x0b, win_rt, 256)
    qc, cqn = q_proj(h, g_cq, wuq_r, cos, sin)
    kc, kct, v, ckvn = kv_proj(h, g_ckv, wuk, wuv, cos, sin, kr_blk)
    o, ob, lse = attn_fwd(qc, kc, v)
    co, uc = conv_fwd(h, conv_w, conv_b, g_conv_ln, b_conv_ln)
    wout_g, wff1_g, wff2_g = hooks.late_weights(ob)
    wout = wout_g.reshape(d, d)
    wff2 = wff2_g.reshape(-1, d)
    r1, x1, x1b = out_proj_ln1(ob, co, wout, x0, g_ln1, b_ln1)
    rb, a1b = ff1_fwd(x1b, wff1_g)
    dr2, dr2b, loss8, dg2, db2 = ff2_ln2_loss(a1b, wff2, x1, target, g_ln2, b_ln2)

    df1b = ff2_bwd_act(dr2b, wff2, rb)
    gw_ff2 = wgrad("wgrad_ff2", a1b, dr2b, 1024, 2048).reshape(N_CHIP, -1, d)
    gw_ff1 = wgrad("wgrad_ff1", x1b, df1b, 1024, 2048, shards=N_CHIP)
    tok = hooks.ff_grads(gw_ff2, gw_ff1)
    dr1, dr1b, dg1, db1 = ff1_bwd_ln1(df1b, wff1_g, dr2, r1, g_ln1 + tok[0:1, 0:1])
    tok = hooks.ff_grads_mid(dr1b)
    gw_out = jnp.concatenate([wgrad("wgrad_out_attn", ob, dr1b, 1024, 1024)[0],
                              wgrad("wgrad_out_conv", co, dr1b, 1024, 1024)[0]], axis=0).reshape(N_CHIP, -1, d)
    dob, dot, dco, delta = out_proj_bwd(dr1b, wout.T, o)
    duc, dgc, dbc, dcb = conv_bwd_ln(uc, dco, g_conv_ln + tok[0:1, 0:1], b_conv_ln)
    dconv, gconvw = conv_bwd_taps(h, duc, conv_w)
    dqt, dk, dv = attn_bwd(qc, kc, kct, v, dob, dot, lse, delta)
    dqb, dcq, dgq = q_bwd(dqt, h, g_cq, wuq_r.T, cos, sin)
    dknb, dvb, dckv, dkr, dgkv = kv_bwd(dk, dv, h, g_ckv, wuk.T, wuv.T, cos, sin)
    gwuq_r = wgrad("wgrad_uq", cqn, dqb, 512, 1024)[0]
    gw_uk = wgrad("wgrad_uk", ckvn, dknb, 512, 1024, shards=N_CHIP)
    gw_uv = wgrad("wgrad_uv", ckvn, dvb, 512, 1024, shards=N_CHIP)
    dh = jnp.concatenate([dcq, dckv, dconv, dkr], axis=1)
    gwin_rt = wgrad("wgrad_in", dh, x0b, 640, 1024)[0]

    gwin_t = jnp.concatenate([gwin_rt[:o_kr], gwin_rt[o_kr + 2 * c:o_kr + 2 * c + D_ROPE], gwin_rt[o_kr:o_kr + 2 * c]], axis=0)
    gwin_t = jnp.pad(gwin_t.reshape(N_CHIP, per, d), ((0, 0), (0, win_g.shape[1] - per), (0, 0)))
    gwuq = jnp.concatenate([gwuq_r[:, :MLA_W].reshape(R_Q, HEADS, D_NOPE),
                            gwuq_r[:, MLA_W:].reshape(R_Q, HEADS, LANE)[:, :, :D_ROPE]], axis=2).reshape(R_Q, HEADS * D_QK)
    tok = hooks.rest_grads(dict(w_in=gwin_t, w_uq=_shard_cols(gwuq), w_uk=gw_uk, w_uv=gw_uv,
                                conv_w=_shard_cols(gconvw), w_out=gw_out))
    gx, dgin, dbin = in_proj_bwd_ln(dh, win_rt, dr1, x, ln_in_g + tok[0:1, 0:1])
    small = jnp.concatenate([dgin, dbin, dgq, dgkv, dcb, dgc, dbc, dg1, db1, dg2, db2, loss8], axis=1)
    return gx, small


BIG = ["w_in", "w_uq", "w_uk", "w_uv", "conv_w", "w_out", "w_ff1", "w_ff2"]
EARLY = ["w_in", "w_uq", "w_uk", "w_uv", "conv_w"]
LATE = ["w_out", "w_ff1", "w_ff2"]
SMALL = ["ln_in_g", "ln_in_b", "g_cq", "g_ckv", "conv_b", "g_conv_ln", "b_conv_ln", "g_ln1", "b_ln1", "g_ln2", "b_ln2"]
WEIGHTS = ["ln_in_g", "ln_in_b", "w_in", "g_cq", "w_uq", "g_ckv", "w_uk", "w_uv", "conv_w", "conv_b", "g_conv_ln",
           "b_conv_ln", "w_out", "g_ln1", "b_ln1", "w_ff1", "w_ff2", "g_ln2", "b_ln2"]


def _pad_rows(a, rows):
    return jnp.pad(a, ((0, rows - a.shape[0]), (0, 0)))


def kernel(x, positions, ln_in_g, ln_in_b, w_in, g_cq, w_uq, g_ckv, w_uk, w_uv, conv_w, conv_b, g_conv_ln, b_conv_ln, w_out, g_ln1, b_ln1, w_ff1, w_ff2, g_ln2, b_ln2, loss_target, m_ln_in_g, m_ln_in_b, m_w_in, m_g_cq, m_w_uq, m_g_ckv, m_w_uk, m_w_uv, m_conv_w, m_conv_b, m_g_conv_ln, m_b_conv_ln, m_w_out, m_g_ln1, m_b_ln1, m_w_ff1, m_w_ff2, m_g_ln2, m_b_ln2, v_ln_in_g, v_ln_in_b, v_w_in, v_g_cq, v_w_uq, v_g_ckv, v_w_uk, v_w_uv, v_conv_w, v_conv_b, v_g_conv_ln, v_b_conv_ln, v_w_out, v_g_ln1, v_b_ln1, v_w_ff1, v_w_ff2, v_g_ln2, v_b_ln2):
    w = dict(ln_in_g=ln_in_g, ln_in_b=ln_in_b, w_in=w_in, g_cq=g_cq, w_uq=w_uq, g_ckv=g_ckv, w_uk=w_uk, w_uv=w_uv,
             conv_w=conv_w, conv_b=conv_b, g_conv_ln=g_conv_ln, b_conv_ln=b_conv_ln, w_out=w_out, g_ln1=g_ln1,
             b_ln1=b_ln1, w_ff1=w_ff1, w_ff2=w_ff2, g_ln2=g_ln2, b_ln2=b_ln2)
    m = dict(ln_in_g=m_ln_in_g, ln_in_b=m_ln_in_b, w_in=m_w_in, g_cq=m_g_cq, w_uq=m_w_uq, g_ckv=m_g_ckv, w_uk=m_w_uk,
             w_uv=m_w_uv, conv_w=m_conv_w, conv_b=m_conv_b, g_conv_ln=m_g_conv_ln, b_conv_ln=m_b_conv_ln, w_out=m_w_out,
             g_ln1=m_g_ln1, b_ln1=m_b_ln1, w_ff1=m_w_ff1, w_ff2=m_w_ff2, g_ln2=m_g_ln2, b_ln2=m_b_ln2)
    v = dict(ln_in_g=v_ln_in_g, ln_in_b=v_ln_in_b, w_in=v_w_in, g_cq=v_g_cq, w_uq=v_w_uq, g_ckv=v_g_ckv, w_uk=v_w_uk,
             w_uv=v_w_uv, conv_w=v_conv_w, conv_b=v_conv_b, g_conv_ln=v_g_conv_ln, b_conv_ln=v_b_conv_ln, w_out=v_w_out,
             g_ln1=v_g_ln1, b_ln1=v_b_ln1, w_ff1=v_w_ff1, w_ff2=v_w_ff2, g_ln2=v_g_ln2, b_ln2=v_b_ln2)

    as2d = lambda t, n: t[n][0].T if n == "w_in" else t[n][0]
    sh2 = {n: as2d(w, n) for n in BIG}
    cidx = lax.axis_index("c").astype(jnp.int32).reshape(1)
    me = 2 * lax.axis_index("x") + lax.axis_index("y")
    kc = jnp.stack([me, lax.axis_index("c")]).astype(jnp.int32)

    pad_to = {"conv_w": CONV_K + 1, "w_in": -(-sh2["w_in"].shape[0] // (4 * SUB)) * (4 * SUB)}
    early = [_pad_rows(sh2[n] if n == "conv_w" else sh2[n].astype(BF), pad_to.get(n, sh2[n].shape[0])) for n in EARLY]
    eg = split_send_start("early_weights_start", "gather_half", early, [(N_CHIP,) + a.shape for a in early], ln_in_g)
    late = [sh2[n].astype(BF) for n in LATE]
    ag = split_send_start("late_weights_start", "gather", late, [(N_CHIP,) + a.shape for a in late], eg[4])
    rest = [n for n in BIG if n not in ("w_ff2", "w_ff1")]
    flight = {}

    class Hooks:
        @staticmethod
        def early_weights(after):
            mine, lands = split_send_wait("early_weights_wait", "gather_half", *eg[:4], after)
            full = [lax.dynamic_update_slice(g, a[None], (me, 0, 0)) for g, a in zip(swap_gathered_halves(lands), mine)]
            return [g[:, :CONV_K] if n == "conv_w" else g for n, g in zip(EARLY, full)]

        @staticmethod
        def late_weights(after):
            mine, lands = split_send_wait("late_weights_wait", "gather", *ag[:4], after)
            return [lax.dynamic_update_slice(g, a[None], (me, 0, 0)) for g, a in zip(lands, mine)]

        @staticmethod
        def ff_grads(gw_ff2, gw_ff1):
            full = [gw_ff2, gw_ff1]
            st = split_send_start("ff_pair_start", "pair", full, [(N_CHIP, g.shape[1] // 2, g.shape[2]) for g in full], ag[4])
            flight["ff_pair"] = st[:4]
            flight["token"] = st[4]
            return st[4]

        @staticmethod
        def ff_grads_mid(after):
            full, recv = split_send_wait("ff_pair_wait", "pair", *flight["ff_pair"], after)
            psum = [pair_add(g, r, cidx) for g, r in zip(full, recv)]
            st = split_send_start("ff_grads_start", "scatter", psum, [(N_CHIP - 1,) + p.shape[1:] for p in psum], flight["token"])
            flight["ff"] = st[:4]
            flight["token"] = st[4]
            return st[4]

        @staticmethod
        def rest_grads(big):
            full = [big[n] for n in rest]
            psum = [pair_add(g, r, cidx) for g, r in zip(full, pair_exchange(full, "rest"))]
            st = split_send_start("rest_grads_start", "scatter", psum, [(N_CHIP - 1,) + p.shape[1:] for p in psum], flight["token"])
            flight["rest"] = st[:4]
            return st[4]

    gx, small = local_step(x[0], positions[0], ln_in_g, ln_in_b, g_cq, g_ckv, conv_b, g_conv_ln, b_conv_ln, g_ln1, b_ln1,
                           g_ln2, b_ln2, loss_target[0], ag[4], Hooks)

    ff_psum, ff_got = split_send_wait("ff_grads_wait", "scatter", *flight["ff"], gx)
    rest_psum, rest_got = split_send_wait("rest_grads_wait", "scatter", *flight["rest"], gx)
    summed = [chip_add(p, r, kc) for p, r in zip(rest_psum + ff_psum, rest_got + ff_got)]
    gsh = dict(zip(rest + ["w_ff2", "w_ff1"], pair_share(summed, "all")))
    for n in pad_to:
        gsh[n] = gsh[n][:sh2[n].shape[0]]

    grad, delta, new_m, new_v = {}, {}, {}, {}
    for n in BIG:
        back = (lambda a: a.T[None]) if n == "w_in" else (lambda a: a[None])
        d_, m_, v_ = adamw("adamw_" + n, sh2[n], gsh[n], as2d(m, n), as2d(v, n))
        grad[n], delta[n], new_m[n], new_v[n] = back(gsh[n]), back(d_), back(m_), back(v_)

    flat = lambda t: jnp.concatenate([t[n].reshape(1, -1) for n in SMALL] + [jnp.zeros((1, LANE), F32)], axis=1)
    g_s, d_s, m_s, v_s = small_allreduce_adamw(small, flat(w), flat(m), flat(v))
    off = 0
    for n in SMALL:
        sz = w[n].size
        for dst, src in ((grad, g_s), (delta, d_s), (new_m, m_s), (new_v, v_s)):
            dst[n] = src[0, off:off + sz].reshape(w[n].shape)
        off += sz
    loss = jnp.sum(g_s[0, off:off + LANE])

    return (loss, gx[None], *[grad[n] for n in WEIGHTS], *[delta[n] for n in WEIGHTS],
            *[new_m[n] for n in WEIGHTS], *[new_v[n] for n in WEIGHTS])
```

```python
import jax
import jax.numpy as jnp
from jax import lax
from jax.experimental import pallas as pl
from jax.experimental.pallas import tpu as pltpu

F32 = jnp.float32
BF = jnp.bfloat16

HEADS = 8
D_NOPE = 128
D_ROPE = 64
D_V = 128
D_QK = D_NOPE + D_ROPE
R_Q = 512
R_KV = 512
MLA_W = HEADS * D_V
CONV_K = 31
CONV_PAD = CONV_K // 2
ROPE_BASE = 10000.0
LOG2E = 1.4426950408889634
LN2 = 0.6931471805599453
LN_EPS = 1e-5
RMS_EPS = 1e-6
ALPHA = (2.0 * 1) ** 0.25
ADAM_LR = 0.001
ADAM_B1 = 0.9
ADAM_B2 = 0.999
ADAM_EPS = 1e-08
ADAM_WD = 0.01
ADAM_STEP = 10

LANE = 128
SUB = 8
HALO = 16
N_CHIP = 4
MESH = pl.DeviceIdType.MESH
VMEM_MB = 1024 * 1024


def _call(body, **kw):
    return pl.pallas_call(body, **kw)


def _cp(sem, mb=48):
    return pltpu.CompilerParams(dimension_semantics=sem, vmem_limit_bytes=mb * VMEM_MB)


def _sds(shape, dt):
    return jax.ShapeDtypeStruct(shape, dt)


def _dot(a, b):
    return jnp.dot(a, b, preferred_element_type=F32)


def _dot_nt(a, b):
    return lax.dot_general(a, b, (((1,), (1,)), ((), ())), preferred_element_type=F32)


def _dot_tn(a, b):
    return lax.dot_general(a, b, (((0,), (0,)), ((), ())), preferred_element_type=F32)


def _rows8(v):
    t, n = v.shape
    return v.reshape(t // SUB, SUB, n).sum(axis=0)


def _ln_stats(r):
    mu = jnp.mean(r, axis=-1, keepdims=True)
    xc = r - mu
    var = jnp.mean(xc * xc, axis=-1, keepdims=True)
    rstd = lax.rsqrt(var + LN_EPS)
    return xc * rstd, rstd


def _ln_bwd(dy, xhat, rstd, g):
    dyh = dy * g
    m1 = jnp.mean(dyh, axis=-1, keepdims=True)
    m2 = jnp.mean(dyh * xhat, axis=-1, keepdims=True)
    return rstd * (dyh - m1 - xhat * m2)


def _rms_fwd(x, g):
    rr = lax.rsqrt(jnp.mean(x * x, axis=-1, keepdims=True) + RMS_EPS)
    xh = x * rr
    return xh * g, xh, rr


def _rms_bwd(dy, xh, rr, g):
    dyg = dy * g
    return rr * (dyg - xh * jnp.mean(dyg * xh, axis=-1, keepdims=True))


def _rope128(x, cos, sin_signed):
    lane = lax.broadcasted_iota(jnp.int32, x.shape, 1)
    rot = jnp.where(lane < D_ROPE // 2, pltpu.roll(x, LANE - D_ROPE // 2, 1), pltpu.roll(x, D_ROPE // 2, 1))
    return x * cos + rot * sin_signed


def _unrope128(dy, cos, sin_signed):
    t = dy * sin_signed
    lane = lax.broadcasted_iota(jnp.int32, dy.shape, 1)
    rot = jnp.where(lane < D_ROPE // 2, pltpu.roll(t, LANE - D_ROPE // 2, 1), pltpu.roll(t, D_ROPE // 2, 1))
    return dy * cos + rot


def _as_row(col):
    return jnp.transpose(jnp.broadcast_to(col, (col.shape[0], LANE)))[0:1, :]


def _sigmoid(x):
    return 1.0 / (1.0 + jnp.exp(-x))


def _row_chunks(tm, fn, rc=128):
    rc = min(rc, tm)

    def step(ci, carry):
        fn(pl.ds(pl.multiple_of(ci * rc, rc), rc))
        return carry

    lax.fori_loop(0, tm // rc, step, 0)


def _unrolled_loop(n, unroll, fn, init):
    unroll = min(n, unroll)
    assert n % unroll == 0

    def body(t, carry):
        for u in range(unroll):
            carry = fn(t * unroll + u, carry)
        return carry

    return lax.fori_loop(0, n // unroll, body, init)


def _tile(s, want):
    t = min(s, want)
    assert s % t == 0
    return t


def rope_tables(pos_f, invf):
    s = pos_f.shape[0]
    tm = _tile(s, 1024)

    def body(p_ref, f_ref, c_ref, s_ref):
        ang = p_ref[...] * f_ref[...]
        lane = lax.broadcasted_iota(jnp.int32, ang.shape, 1)
        c = jnp.cos(ang)
        sn = jnp.sin(ang)
        c_ref[...] = jnp.where(lane < D_ROPE, c, 0.0)
        s_ref[...] = jnp.where(lane < D_ROPE // 2, -sn, jnp.where(lane < D_ROPE, sn, 0.0))

    return _call(
        body, name="rope_tables", grid=(s // tm,),
        in_specs=[pl.BlockSpec((tm, 1), lambda i: (i, 0)), pl.BlockSpec((1, LANE), lambda i: (0, 0))],
        out_specs=[pl.BlockSpec((tm, LANE), lambda i: (i, 0))] * 2,
        out_shape=[_sds((s, LANE), F32)] * 2,
        compiler_params=_cp(("arbitrary",)),
    )(pos_f, invf)


def ln_in_fwd(x, g, b):
    s, d = x.shape
    tm = _tile(s, 512)

    def body(x_ref, g_ref, b_ref, o_ref, ob_ref):
        xhat, _ = _ln_stats(x_ref[...])
        y = xhat * g_ref[...] + b_ref[...]
        o_ref[...] = y
        ob_ref[...] = y.astype(BF)

    row = pl.BlockSpec((1, d), lambda i: (0, 0))
    tok = pl.BlockSpec((tm, d), lambda i: (i, 0))
    return _call(
        body, name="ln_in_fwd", grid=(s // tm,), in_specs=[tok, row, row], out_specs=[tok, tok],
        out_shape=[_sds((s, d), F32), _sds((s, d), BF)], compiler_params=_cp(("arbitrary",)),
    )(x, g, b)


def matmul_nt(name, a, wt, tm, out_dtype=F32):
    s, k = a.shape
    n = wt.shape[0]
    tm = _tile(s, tm)

    def body(a_ref, w_ref, o_ref):
        o_ref[...] = _dot_nt(a_ref[...], w_ref[...]).astype(o_ref.dtype)

    return _call(
        body, name=name, grid=(s // tm,),
        in_specs=[pl.BlockSpec((tm, k), lambda i: (i, 0)), pl.BlockSpec((n, k), lambda i: (0, 0))],
        out_specs=pl.BlockSpec((tm, n), lambda i: (i, 0)),
        out_shape=_sds((s, n), out_dtype), compiler_params=_cp(("arbitrary",)),
    )(a, wt)


def q_proj(h, g_cq, wuq, cos, sin):
    s = h.shape[0]
    tm = _tile(s, 512)

    def body(h_ref, g_ref, w_ref, c_ref, s_ref, q_ref, n_ref):
        y, _, _ = _rms_fwd(h_ref[...], g_ref[...])
        yb = y.astype(BF)
        n_ref[...] = yb
        q = _dot(yb, w_ref[...])
        c = c_ref[...]
        sn = s_ref[...]
        for hd in range(HEADS):
            q_ref[hd, :, 0:LANE] = q[:, LANE * hd:LANE * (hd + 1)].astype(BF)
            qr = q[:, MLA_W + LANE * hd:MLA_W + LANE * (hd + 1)]
            q_ref[hd, :, LANE:2 * LANE] = _rope128(qr, c, sn).astype(BF)

    return _call(
        body, name="q_proj", grid=(s // tm,),
        in_specs=[pl.BlockSpec((tm, R_Q), lambda i: (i, 0)), pl.BlockSpec((1, R_Q), lambda i: (0, 0)),
                  pl.BlockSpec((R_Q, 2 * MLA_W), lambda i: (0, 0)),
                  pl.BlockSpec((tm, LANE), lambda i: (i, 0)), pl.BlockSpec((tm, LANE), lambda i: (i, 0))],
        out_specs=[pl.BlockSpec((HEADS, tm, 2 * LANE), lambda i: (0, i, 0)), pl.BlockSpec((tm, R_Q), lambda i: (i, 0))],
        out_shape=[_sds((HEADS, s, 2 * LANE), BF), _sds((s, R_Q), BF)], compiler_params=_cp(("arbitrary",)),
    )(h, g_cq, wuq, cos, sin)


def kv_proj(h, g_ckv, wuk, wuv, cos, sin, kr_blk):
    s = h.shape[0]
    tm = _tile(s, 512)

    def body(h_ref, kr_ref, g_ref, wk_ref, wv_ref, c_ref, s_ref, k_ref, kt_ref, v_ref, n_ref):
        y, _, _ = _rms_fwd(h_ref[...], g_ref[...])
        yb = y.astype(BF)
        n_ref[...] = yb
        kn = _dot(yb, wk_ref[...])
        v = _dot(yb, wv_ref[...])
        kr = _rope128(kr_ref[...], c_ref[...], s_ref[...])
        krb = kr.astype(BF)
        krt = kr.T.astype(BF)
        for hd in range(HEADS):
            knh = kn[:, LANE * hd:LANE * (hd + 1)]
            k_ref[hd, :, 0:LANE] = knh.astype(BF)
            k_ref[hd, :, LANE:2 * LANE] = krb
            kt_ref[hd, 0:LANE, :] = knh.T.astype(BF)
            kt_ref[hd, LANE:2 * LANE, :] = krt
            v_ref[hd, :, 0:LANE] = v[:, LANE * hd:LANE * (hd + 1)].astype(BF)
            v_ref[hd, :, LANE:2 * LANE] = jnp.ones((tm, LANE), BF)

    tab = pl.BlockSpec((tm, LANE), lambda i: (i, 0))
    wsp = pl.BlockSpec((R_KV, MLA_W), lambda i: (0, 0))
    return _call(
        body, name="kv_proj", grid=(s // tm,),
        in_specs=[pl.BlockSpec((tm, R_KV), lambda i: (i, 1)), pl.BlockSpec((tm, LANE), lambda i: (i, kr_blk)),
                  pl.BlockSpec((1, R_KV), lambda i: (0, 0)), wsp, wsp, tab, tab],
        out_specs=[pl.BlockSpec((HEADS, tm, 2 * LANE), lambda i: (0, i, 0)), pl.BlockSpec((HEADS, 2 * LANE, tm), lambda i: (0, 0, i)),
                   pl.BlockSpec((HEADS, tm, 2 * LANE), lambda i: (0, i, 0)), pl.BlockSpec((tm, R_KV), lambda i: (i, 0))],
        out_shape=[_sds((HEADS, s, 2 * LANE), BF), _sds((HEADS, 2 * LANE, s), BF), _sds((HEADS, s, 2 * LANE), BF), _sds((s, R_KV), BF)],
        compiler_params=_cp(("arbitrary",)),
    )(h, h, g_ckv, wuk, wuv, cos, sin)


def attn_fwd(qc, kc, v):
    _, s, _ = qc.shape
    tq = _tile(s, 512)
    tk = _tile(s, 512)
    scale = D_QK ** -0.5
    c2 = scale * LOG2E
    nk = s // tk
    nb = tk // LANE
    un = 8

    def body(q_ref, k_ref, v_ref, o_ref, ob_ref, l_ref, s_scr, m_scr):
        q = q_ref[...]

        def scores(j, mpart):
            off = pl.multiple_of(j * tk, tk)
            sc = _dot_nt(q, k_ref[pl.ds(off, tk), :]) * c2
            s_scr[:, pl.ds(off, tk)] = sc
            for b in range(nb):
                mpart = jnp.maximum(mpart, sc[:, LANE * b:LANE * (b + 1)])
            return mpart

        mpart = _unrolled_loop(nk, un, scores, jnp.full((tq, LANE), -jnp.inf, F32))
        m = jnp.max(mpart, axis=-1, keepdims=True)
        m_scr[...] = jnp.broadcast_to(m, (tq, LANE))

        def weigh(j, acc):
            off = pl.multiple_of(j * tk, tk)
            ps = [jnp.exp2(s_scr[:, pl.ds(off + LANE * b, LANE)] - m_scr[...]).astype(BF) for b in range(nb)]
            return acc + _dot(jnp.concatenate(ps, axis=1), v_ref[pl.ds(off, tk), :])

        acc = _unrolled_loop(nk, un, weigh, jnp.zeros((tq, 2 * LANE), F32))
        l = acc[:, D_V:D_V + 1]
        o = acc[:, 0:D_V] / l
        o_ref[...] = o
        ob_ref[...] = o.astype(BF)
        l_ref[...] = _as_row(m + jnp.log(l) * LOG2E)

    return _call(
        body, name="attn_fwd", grid=(HEADS, s // tq),
        in_specs=[pl.BlockSpec((None, tq, 2 * LANE), lambda h, i: (h, i, 0)),
                  pl.BlockSpec((None, s, 2 * LANE), lambda h, i: (h, 0, 0)),
                  pl.BlockSpec((None, s, 2 * LANE), lambda h, i: (h, 0, 0))],
        out_specs=[pl.BlockSpec((tq, LANE), lambda h, i: (i, h)), pl.BlockSpec((tq, LANE), lambda h, i: (i, h)),
                   pl.BlockSpec((None, 1, tq), lambda h, i: (h, 0, i))],
        out_shape=[_sds((s, MLA_W), F32), _sds((s, MLA_W), BF), _sds((HEADS, 1, s), F32)],
        scratch_shapes=[pltpu.VMEM((tq, s + LANE), F32), pltpu.VMEM((tq, LANE), F32)],
        compiler_params=_cp(("arbitrary", "arbitrary")),
    )(qc, kc, v)


def _halo_specs(tm, s, width, col):
    r = tm // HALO
    nb = s // HALO
    cur = pl.BlockSpec((tm, width), lambda i: (i, col))
    prev = pl.BlockSpec((HALO, width), lambda i: (jnp.maximum(i * r - 1, 0), col))
    nxt = pl.BlockSpec((HALO, width), lambda i: (jnp.minimum((i + 1) * r, nb - 1), col))
    return cur, prev, nxt


def _slab_shapes(tm, c):
    return (tm + 2 * HALO, c + LANE), (SUB - 1, tm + 2 * HALO - SUB, c + LANE)


def _fill_slab(slab, tm, prev, cur, nxt):
    i = pl.program_id(0)
    last = pl.num_programs(0) - 1
    c = cur.shape[1]
    slab[0:HALO, 0:c] = jnp.where(i > 0, prev, 0.0)
    slab[HALO:HALO + tm, 0:c] = cur
    slab[HALO + tm:2 * HALO + tm, 0:c] = jnp.where(i < last, nxt, 0.0)


def _rotate_slab(slab, rot, tm):
    rows = tm + 2 * HALO - SUB
    c = slab.shape[1] - LANE
    for b in range(1, SUB):
        rot[b - 1, :, 0:c] = slab[pl.ds(b, rows), 0:c]


def _shifted(slab, rot, start, rc, cs):
    b = start % SUB
    if b == 0:
        return slab[pl.ds(start, rc), cs]
    return rot[b - 1, pl.ds(start - b, rc), cs]


def conv_fwd(h, conv_w, conv_b, g_ln, b_ln):
    s = h.shape[0]
    c = conv_w.shape[1]
    tm = _tile(s, 256)
    rc = _tile(tm, 64)

    def body(a_ref, ap_ref, an_ref, g_ref, gp_ref, gn_ref, w_ref, cb_ref, lg_ref, lb_ref, co_ref, uc_ref, slab, rot):
        _fill_slab(slab, tm, ap_ref[...] * _sigmoid(gp_ref[...]), a_ref[...] * _sigmoid(g_ref[...]),
                   an_ref[...] * _sigmoid(gn_ref[...]))
        _rotate_slab(slab, rot, tm)

        def lane_block(cb, carry):
            cs = pl.ds(pl.multiple_of(cb * LANE, LANE), LANE)
            for r0 in range(0, tm, rc):
                acc = jnp.zeros((rc, LANE), F32)
                for k in range(CONV_K):
                    acc = acc + w_ref[k:k + 1, cs] * _shifted(slab, rot, r0 + HALO - CONV_PAD + k, rc, cs)
                uc_ref[r0:r0 + rc, cs] = acc + cb_ref[:, cs]
            return carry

        lax.fori_loop(0, c // LANE, lane_block, 0)
        xhat, _ = _ln_stats(uc_ref[...])
        cl = xhat * lg_ref[...] + lb_ref[...]
        co_ref[...] = (cl * _sigmoid(cl)).astype(BF)

    a_specs = _halo_specs(tm, s, c, 1)
    g_specs = _halo_specs(tm, s, c, 2)
    row = pl.BlockSpec((1, c), lambda i: (0, 0))
    tok = pl.BlockSpec((tm, c), lambda i: (i, 0))
    return _call(
        body, name="conv_fwd", grid=(s // tm,),
        in_specs=[*a_specs, *g_specs, pl.BlockSpec(conv_w.shape, lambda i: (0, 0)), row, row, row],
        out_specs=[tok, tok], out_shape=[_sds((s, c), BF), _sds((s, c), F32)],
        scratch_shapes=[pltpu.VMEM(shp, F32) for shp in _slab_shapes(tm, c)],
        compiler_params=_cp(("arbitrary",)),
    )(h, h, h, h, h, h, conv_w, conv_b, g_ln, b_ln)


def out_proj_ln1(ob, co, wout, x0, g1, b1):
    s, d = x0.shape
    kh = ob.shape[1]
    tm = _tile(s, 256)

    def body(o_ref, c_ref, w_ref, x_ref, g_ref, b_ref, r_ref, x1_ref, x1b_ref, acc):
        acc[...] = _dot(o_ref[...], w_ref[0:kh, :]) + _dot(c_ref[...], w_ref[kh:2 * kh, :])
        g = g_ref[...]
        b = b_ref[...]

        def chunk(rows):
            r = ALPHA * x_ref[rows, :] + acc[rows, :]
            r_ref[rows, :] = r
            xhat, _ = _ln_stats(r)
            y = xhat * g + b
            x1_ref[rows, :] = y
            x1b_ref[rows, :] = y.astype(BF)

        _row_chunks(tm, chunk)

    half = pl.BlockSpec((tm, kh), lambda i: (i, 0))
    tok = pl.BlockSpec((tm, d), lambda i: (i, 0))
    row = pl.BlockSpec((1, d), lambda i: (0, 0))
    return _call(
        body, name="out_proj_ln1", grid=(s // tm,),
        in_specs=[half, half, pl.BlockSpec((2 * kh, d), lambda i: (0, 0)), tok, row, row],
        out_specs=[tok, tok, tok], out_shape=[_sds((s, d), F32), _sds((s, d), F32), _sds((s, d), BF)],
        scratch_shapes=[pltpu.VMEM((tm, d), F32)], compiler_params=_cp(("arbitrary",)),
    )(ob, co, wout, x0, g1, b1)


def ff1_fwd(x1b, wff1_g):
    s, d = x1b.shape
    nsh, _, fs = wff1_g.shape
    tm = _tile(s, 1024)
    tn = _tile(fs, 2048)
    per = fs // tn

    def body(a_ref, w_ref, r_ref, a1_ref):
        r = jnp.maximum(_dot(a_ref[...], w_ref[...]), 0.0)
        r_ref[...] = r.astype(BF)
        a1_ref[...] = (r * r).astype(BF)

    out = pl.BlockSpec((tm, tn), lambda i, j: (i, j))
    return _call(
        body, name="ff1_fwd", grid=(s // tm, nsh * per),
        in_specs=[pl.BlockSpec((tm, d), lambda i, j: (i, 0)),
                  pl.BlockSpec((None, d, tn), lambda i, j: (j // per, 0, j % per))],
        out_specs=[out, out], out_shape=[_sds((s, nsh * fs), BF)] * 2,
        compiler_params=_cp(("arbitrary", "arbitrary")),
    )(x1b, wff1_g)


def ff2_ln2_loss(a1b, wff2, x1, target, g2, b2):
    s, f = a1b.shape
    d = x1.shape[1]
    tm = _tile(s, 512)
    tk = _tile(f, 2048)
    nk = f // tk

    def body(a_ref, w_ref, x_ref, t_ref, g_ref, b_ref, dr_ref, drb_ref, loss_ref, dg_ref, db_ref, acc):
        i = pl.program_id(0)
        k = pl.program_id(1)

        @pl.when(k == 0)
        def _():
            acc[...] = _dot(a_ref[...], w_ref[...])

        @pl.when(k > 0)
        def _():
            acc[...] += _dot(a_ref[...], w_ref[...])

        @pl.when(jnp.logical_and(i == 0, k == 0))
        def _():
            loss_ref[...] = jnp.zeros_like(loss_ref)
            dg_ref[...] = jnp.zeros_like(dg_ref)
            db_ref[...] = jnp.zeros_like(db_ref)

        @pl.when(k == nk - 1)
        def _():
            g = g_ref[...]

            def chunk(rows):
                r = ALPHA * x_ref[rows, :] + acc[rows, :]
                xhat, rstd = _ln_stats(r)
                e = xhat * g + b_ref[...] - t_ref[rows, :]
                e2 = _rows8(e * e)
                part = e2[:, 0:LANE]
                for c in range(1, d // LANE):
                    part = part + e2[:, LANE * c:LANE * (c + 1)]
                loss_ref[...] += part * (0.5 / d)
                dy = e * (1.0 / d)
                dg_ref[...] += _rows8(dy * xhat)
                db_ref[...] += _rows8(dy)
                dr = _ln_bwd(dy, xhat, rstd, g)
                dr_ref[rows, :] = dr
                drb_ref[rows, :] = dr.astype(BF)

            _row_chunks(tm, chunk)

    tok = pl.BlockSpec((tm, d), lambda i, k: (i, 0))
    row = pl.BlockSpec((1, d), lambda i, k: (0, 0))
    accs = pl.BlockSpec((SUB, d), lambda i, k: (0, 0))
    return _call(
        body, name="ff2_ln2_loss", grid=(s // tm, nk),
        in_specs=[pl.BlockSpec((tm, tk), lambda i, k: (i, k)), pl.BlockSpec((tk, d), lambda i, k: (k, 0)),
                  tok, tok, row, row],
        out_specs=[tok, tok, pl.BlockSpec((SUB, LANE), lambda i, k: (0, 0)), accs, accs],
        out_shape=[_sds((s, d), F32), _sds((s, d), BF), _sds((SUB, LANE), F32), _sds((SUB, d), F32), _sds((SUB, d), F32)],
        scratch_shapes=[pltpu.VMEM((tm, d), F32)], compiler_params=_cp(("arbitrary", "arbitrary"), 60),
    )(a1b, wff2, x1, target, g2, b2)


def ff2_bwd_act(dr2b, wff2, rb):
    s, d = dr2b.shape
    f = wff2.shape[0]
    tm = _tile(s, 1024)
    tn = _tile(f, 2048)

    def body(a_ref, w_ref, r_ref, o_ref):
        o_ref[...] = (_dot_nt(a_ref[...], w_ref[...]) * (2.0 * r_ref[...].astype(F32))).astype(BF)

    return _call(
        body, name="ff2_bwd_act", grid=(s // tm, f // tn),
        in_specs=[pl.BlockSpec((tm, d), lambda i, j: (i, 0)), pl.BlockSpec((tn, d), lambda i, j: (j, 0)),
                  pl.BlockSpec((tm, tn), lambda i, j: (i, j))],
        out_specs=pl.BlockSpec((tm, tn), lambda i, j: (i, j)), out_shape=_sds((s, f), BF),
        compiler_params=_cp(("arbitrary", "arbitrary")),
    )(dr2b, wff2, rb)


def wgrad(name, a, b, tm, tn, tk=2048, shards=1):
    s, m = a.shape
    n = b.shape[1]
    tm = _tile(m, tm)
    ns = n // shards
    tn = _tile(ns, tn)
    tk = _tile(s, tk)
    per = ns // tn

    def body(a_ref, b_ref, o_ref):
        k = pl.program_id(2)

        @pl.when(k == 0)
        def _():
            o_ref[...] = _dot_tn(a_ref[...], b_ref[...])

        @pl.when(k > 0)
        def _():
            o_ref[...] += _dot_tn(a_ref[...], b_ref[...])

    return _call(
        body, name=name, grid=(m // tm, n // tn, s // tk),
        in_specs=[pl.BlockSpec((tk, tm), lambda i, j, k: (k, i)), pl.BlockSpec((tk, tn), lambda i, j, k: (k, j))],
        out_specs=pl.BlockSpec((None, tm, tn), lambda i, j, k: (j // per, i, j % per)),
        out_shape=_sds((shards, m, ns), F32), compiler_params=_cp(("arbitrary", "arbitrary", "arbitrary")),
    )(a, b)


def ff1_bwd_ln1(df1b, wff1_g, dr2, r1, g1):
    s, f = df1b.shape
    d = dr2.shape[1]
    tm = _tile(s, 512)
    tk = _tile(wff1_g.shape[2], 2048)
    per = wff1_g.shape[2] // tk
    nk = f // tk

    def body(a_ref, w_ref, d2_ref, r_ref, g_ref, dr_ref, drb_ref, dg_ref, db_ref, acc):
        i = pl.program_id(0)
        k = pl.program_id(1)

        @pl.when(k == 0)
        def _():
            acc[...] = _dot_nt(a_ref[...], w_ref[...])

        @pl.when(k > 0)
        def _():
            acc[...] += _dot_nt(a_ref[...], w_ref[...])

        @pl.when(jnp.logical_and(i == 0, k == 0))
        def _():
            dg_ref[...] = jnp.zeros_like(dg_ref)
            db_ref[...] = jnp.zeros_like(db_ref)

        @pl.when(k == nk - 1)
        def _():
            g = g_ref[...]

            def chunk(rows):
                dy = ALPHA * d2_ref[rows, :] + acc[rows, :]
                xhat, rstd = _ln_stats(r_ref[rows, :])
                dg_ref[...] += _rows8(dy * xhat)
                db_ref[...] += _rows8(dy)
                dr = _ln_bwd(dy, xhat, rstd, g)
                dr_ref[rows, :] = dr
                drb_ref[rows, :] = dr.astype(BF)

            _row_chunks(tm, chunk)

    tok = pl.BlockSpec((tm, d), lambda i, k: (i, 0))
    accs = pl.BlockSpec((SUB, d), lambda i, k: (0, 0))
    return _call(
        body, name="ff1_bwd_ln1", grid=(s // tm, nk),
        in_specs=[pl.BlockSpec((tm, tk), lambda i, k: (i, k)), pl.BlockSpec((None, d, tk), lambda i, k: (k // per, 0, k % per)),
                  tok, tok, pl.BlockSpec((1, d), lambda i, k: (0, 0))],
        out_specs=[tok, tok, accs, accs],
        out_shape=[_sds((s, d), F32), _sds((s, d), BF), _sds((SUB, d), F32), _sds((SUB, d), F32)],
        scratch_shapes=[pltpu.VMEM((tm, d), F32)], compiler_params=_cp(("arbitrary", "arbitrary"), 60),
    )(df1b, wff1_g, dr2, r1, g1)


def out_proj_bwd(dr1b, woutt, o):
    s, d = dr1b.shape
    tm = _tile(s, 256)

    def body(a_ref, w_ref, o_ref, do_ref, dot_ref, dc_ref, dl_ref):
        dcat = _dot(a_ref[...], w_ref[...])
        do = dcat[:, 0:MLA_W]
        do_ref[...] = do.astype(BF)
        dc_ref[...] = dcat[:, MLA_W:]
        prod = do * o_ref[...]
        for hd in range(HEADS):
            hs = slice(LANE * hd, LANE * (hd + 1))
            dl_ref[hd] = _as_row(jnp.sum(prod[:, hs], axis=-1, keepdims=True))
            dot_ref[hd] = do[:, hs].T.astype(BF)

    half = pl.BlockSpec((tm, MLA_W), lambda i: (i, 0))
    return _call(
        body, name="out_proj_bwd", grid=(s // tm,),
        in_specs=[pl.BlockSpec((tm, d), lambda i: (i, 0)), pl.BlockSpec((d, d), lambda i: (0, 0)), half],
        out_specs=[half, pl.BlockSpec((HEADS, LANE, tm), lambda i: (0, 0, i)),
                   pl.BlockSpec((tm, d - MLA_W), lambda i: (i, 0)), pl.BlockSpec((HEADS, 1, tm), lambda i: (0, 0, i))],
        out_shape=[_sds((s, MLA_W), BF), _sds((HEADS, LANE, s), BF), _sds((s, d - MLA_W), F32), _sds((HEADS, 1, s), F32)],
        compiler_params=_cp(("arbitrary",)),
    )(dr1b, woutt, o)


def conv_bwd_ln(uc, dco, g_ln, b_ln):
    s, c = uc.shape
    tm = _tile(s, 512)

    def body(u_ref, d_ref, g_ref, b_ref, du_ref, dg_ref, db_ref, dcb_ref):
        @pl.when(pl.program_id(0) == 0)
        def _():
            dg_ref[...] = jnp.zeros_like(dg_ref)
            db_ref[...] = jnp.zeros_like(db_ref)
            dcb_ref[...] = jnp.zeros_like(dcb_ref)

        xhat, rstd = _ln_stats(u_ref[...])
        g = g_ref[...]
        cl = xhat * g + b_ref[...]
        sg = _sigmoid(cl)
        dcl = d_ref[...] * (sg * (1.0 + cl * (1.0 - sg)))
        dg_ref[...] += _rows8(dcl * xhat)
        db_ref[...] += _rows8(dcl)
        du = _ln_bwd(dcl, xhat, rstd, g)
        du_ref[...] = du
        dcb_ref[...] += _rows8(du)

    tok = pl.BlockSpec((tm, c), lambda i: (i, 0))
    row = pl.BlockSpec((1, c), lambda i: (0, 0))
    accs = pl.BlockSpec((SUB, c), lambda i: (0, 0))
    return _call(
        body, name="conv_bwd_ln", grid=(s // tm,), in_specs=[tok, tok, row, row], out_specs=[tok, accs, accs, accs],
        out_shape=[_sds((s, c), F32)] + [_sds((SUB, c), F32)] * 3, compiler_params=_cp(("arbitrary",)),
    )(uc, dco, g_ln, b_ln)


def conv_bwd_taps(h, duc, conv_w):
    s, c = duc.shape
    tm = _tile(s, 256)
    rc = _tile(tm, 64)

    def body(a_ref, ap_ref, an_ref, g_ref, gp_ref, gn_ref, d_ref, dp_ref, dn_ref, w_ref, o_ref, dw_ref,
             uslab, dslab, du_s, urot, drot, dw8):
        @pl.when(pl.program_id(0) == 0)
        def _():
            dw8[...] = jnp.zeros_like(dw8)

        sg = _sigmoid(g_ref[...])
        a = a_ref[...]
        _fill_slab(uslab, tm, ap_ref[...] * _sigmoid(gp_ref[...]), a * sg, an_ref[...] * _sigmoid(gn_ref[...]))
        _fill_slab(dslab, tm, dp_ref[...], d_ref[...], dn_ref[...])
        _rotate_slab(uslab, urot, tm)
        _rotate_slab(dslab, drot, tm)

        def lane_block(cb, carry):
            cs = pl.ds(pl.multiple_of(cb * LANE, LANE), LANE)
            for r0 in range(0, tm, rc):
                acc = jnp.zeros((rc, LANE), F32)
                for k in range(CONV_K):
                    acc = acc + w_ref[k:k + 1, cs] * _shifted(dslab, drot, r0 + HALO + CONV_PAD - k, rc, cs)
                du_s[r0:r0 + rc, cs] = acc
            return carry

        def lane_block_taps(cb, carry):
            cs = pl.ds(pl.multiple_of(cb * LANE, LANE), LANE)
            parts = []
            for k in range(CONV_K):
                prod = None
                for r0 in range(0, tm, rc):
                    t = dslab[pl.ds(r0 + HALO, rc), cs] * _shifted(uslab, urot, r0 + HALO - CONV_PAD + k, rc, cs)
                    prod = t if prod is None else prod + t
                parts.append(_rows8(prod))
            rows = SUB * CONV_K
            dw8[0:rows, cs] = dw8[0:rows, cs] + jnp.concatenate(parts, axis=0)
            return carry

        lax.fori_loop(0, c // LANE, lane_block, 0)
        lax.fori_loop(0, c // LANE, lane_block_taps, 0)

        @pl.when(pl.program_id(0) == pl.num_programs(0) - 1)
        def _():
            dw_ref[...] = jnp.zeros_like(dw_ref)
            for k in range(CONV_K):
                dw_ref[k:k + 1, :] = jnp.sum(dw8[SUB * k:SUB * (k + 1), :], axis=0, keepdims=True)

        du = du_s[...]
        o_ref[:, 0:c] = (du * sg).astype(BF)
        o_ref[:, c:2 * c] = (du * a * sg * (1.0 - sg)).astype(BF)

    a_specs = _halo_specs(tm, s, c, 1)
    g_specs = _halo_specs(tm, s, c, 2)
    d_specs = _halo_specs(tm, s, c, 0)
    wsp = pl.BlockSpec(conv_w.shape, lambda i: (0, 0))
    return _call(
        body, name="conv_bwd_taps", grid=(s // tm,), in_specs=[*a_specs, *g_specs, *d_specs, wsp],
        out_specs=[pl.BlockSpec((tm, 2 * c), lambda i: (i, 0)), wsp],
        out_shape=[_sds((s, 2 * c), BF), _sds(conv_w.shape, F32)],
        scratch_shapes=[pltpu.VMEM(_slab_shapes(tm, c)[0], F32), pltpu.VMEM(_slab_shapes(tm, c)[0], F32), pltpu.VMEM((tm, c), F32),
                        pltpu.VMEM(_slab_shapes(tm, c)[1], F32), pltpu.VMEM(_slab_shapes(tm, c)[1], F32),
                        pltpu.VMEM((SUB * conv_w.shape[0], c), F32)],
        compiler_params=_cp(("arbitrary",)),
    )(h, h, h, h, h, h, duc, duc, duc, conv_w)


def attn_bwd(qc, kc, kct, v, dob, dot, lse_r, delta_r):
    _, s, _ = qc.shape
    tk = _tile(s, 1024)
    tq = _tile(s, 512)
    scale = D_QK ** -0.5
    c2 = scale * LOG2E

    def body(k_ref, kt_ref, v_ref, q_ref, do_ref, dot_ref, l_ref, dl_ref, dqt_ref, dk_ref, dvt_ref):
        @pl.when(pl.program_id(1) == 0)
        def _():
            dqt_ref[...] = jnp.zeros_like(dqt_ref)

        k = k_ref[...]
        kt = kt_ref[...]
        vv = v_ref[...]

        def step(i, carry):
            dk, dvt = carry
            off = pl.multiple_of(i * tq, tq)
            q = q_ref[pl.ds(off, tq), :]
            do = do_ref[pl.ds(off, tq), :]
            pt = jnp.exp2(_dot_nt(k, q) * c2 - l_ref[:, pl.ds(off, tq)])
            dvt = dvt + _dot_nt(dot_ref[:, pl.ds(off, tq)], pt.astype(BF))
            dpt = _dot_nt(vv, do)
            dsb = (pt * (dpt - dl_ref[:, pl.ds(off, tq)]) * scale).astype(BF)
            dk = dk + _dot(dsb, q)
            dqt_ref[:, pl.ds(off, tq)] += _dot(kt, dsb)
            return dk, dvt

        dk, dvt = _unrolled_loop(s // tq, 16, step, (jnp.zeros((tk, 2 * LANE), F32), jnp.zeros((LANE, tk), F32)))
        dk_ref[...] = dk
        dvt_ref[...] = dvt

    rowv = pl.BlockSpec((None, 1, s), lambda h, j: (h, 0, 0))
    return _call(
        body, name="attn_bwd", grid=(HEADS, s // tk),
        in_specs=[pl.BlockSpec((None, tk, 2 * LANE), lambda h, j: (h, j, 0)),
                  pl.BlockSpec((None, 2 * LANE, tk), lambda h, j: (h, 0, j)),
                  pl.BlockSpec((None, tk, LANE), lambda h, j: (h, j, 0)),
                  pl.BlockSpec((None, s, 2 * LANE), lambda h, j: (h, 0, 0)),
                  pl.BlockSpec((s, LANE), lambda h, j: (0, h)),
                  pl.BlockSpec((None, LANE, s), lambda h, j: (h, 0, 0)), rowv, rowv],
        out_specs=[pl.BlockSpec((None, 2 * LANE, s), lambda h, j: (h, 0, 0)),
                   pl.BlockSpec((None, tk, 2 * LANE), lambda h, j: (h, j, 0)),
                   pl.BlockSpec((None, LANE, tk), lambda h, j: (h, 0, j))],
        out_shape=[_sds((HEADS, 2 * LANE, s), F32), _sds((HEADS, s, 2 * LANE), F32), _sds((HEADS, LANE, s), F32)],
        compiler_params=_cp(("arbitrary", "arbitrary"), 56),
    )(kc, kct, v, qc, dob, dot, lse_r, delta_r)


def q_bwd(dqt, h, g_cq, wuqt, cos, sin):
    s = h.shape[0]
    tm = _tile(s, 256)

    def body(d_ref, h_ref, g_ref, w_ref, c_ref, s_ref, dq_ref, dc_ref, dg_ref):
        @pl.when(pl.program_id(0) == 0)
        def _():
            dg_ref[...] = jnp.zeros_like(dg_ref)

        c = c_ref[...]
        sn = s_ref[...]
        for hd in range(HEADS):
            t = d_ref[hd].T
            dq_ref[:, LANE * hd:LANE * (hd + 1)] = t[:, 0:LANE].astype(BF)
            dq_ref[:, MLA_W + LANE * hd:MLA_W + LANE * (hd + 1)] = _unrope128(t[:, LANE:2 * LANE], c, sn).astype(BF)
        dy = _dot(dq_ref[...], w_ref[...])
        g = g_ref[...]
        _, xh, rr = _rms_fwd(h_ref[...], g)
        dg_ref[...] += _rows8(dy * xh)
        dc_ref[...] = _rms_bwd(dy, xh, rr, g).astype(BF)

    tab = pl.BlockSpec((tm, LANE), lambda i: (i, 0))
    return _call(
        body, name="q_bwd", grid=(s // tm,),
        in_specs=[pl.BlockSpec((HEADS, 2 * LANE, tm), lambda i: (0, 0, i)), pl.BlockSpec((tm, R_Q), lambda i: (i, 0)),
                  pl.BlockSpec((1, R_Q), lambda i: (0, 0)), pl.BlockSpec((2 * MLA_W, R_Q), lambda i: (0, 0)), tab, tab],
        out_specs=[pl.BlockSpec((tm, 2 * MLA_W), lambda i: (i, 0)), pl.BlockSpec((tm, R_Q), lambda i: (i, 0)),
                   pl.BlockSpec((SUB, R_Q), lambda i: (0, 0))],
        out_shape=[_sds((s, 2 * MLA_W), BF), _sds((s, R_Q), BF), _sds((SUB, R_Q), F32)],
        compiler_params=_cp(("arbitrary",)),
    )(dqt, h, g_cq, wuqt, cos, sin)


def kv_bwd(dk, dv, h, g_ckv, wukt, wuvt, cos, sin):
    s = h.shape[0]
    tm = _tile(s, 256)

    def body(dk_ref, dv_ref, h_ref, g_ref, wk_ref, wv_ref, c_ref, s_ref, dkn_ref, dvb_ref, dc_ref, dkr_ref, dg_ref):
        @pl.when(pl.program_id(0) == 0)
        def _():
            dg_ref[...] = jnp.zeros_like(dg_ref)

        dkr = dk_ref[0, :, LANE:2 * LANE]
        for hd in range(HEADS):
            dkn_ref[:, LANE * hd:LANE * (hd + 1)] = dk_ref[hd, :, 0:LANE].astype(BF)
            dvb_ref[:, LANE * hd:LANE * (hd + 1)] = dv_ref[hd].T.astype(BF)
            if hd > 0:
                dkr = dkr + dk_ref[hd, :, LANE:2 * LANE]
        dkr_ref[...] = _unrope128(dkr, c_ref[...], s_ref[...]).astype(BF)
        dy = _dot(dkn_ref[...], wk_ref[...]) + _dot(dvb_ref[...], wv_ref[...])
        g = g_ref[...]
        _, xh, rr = _rms_fwd(h_ref[...], g)
        dg_ref[...] += _rows8(dy * xh)
        dc_ref[...] = _rms_bwd(dy, xh, rr, g).astype(BF)

    tab = pl.BlockSpec((tm, LANE), lambda i: (i, 0))
    wsp = pl.BlockSpec((MLA_W, R_KV), lambda i: (0, 0))
    wide = pl.BlockSpec((tm, MLA_W), lambda i: (i, 0))
    return _call(
        body, name="kv_bwd", grid=(s // tm,),
        in_specs=[pl.BlockSpec((HEADS, tm, 2 * LANE), lambda i: (0, i, 0)), pl.BlockSpec((HEADS, LANE, tm), lambda i: (0, 0, i)),
                  pl.BlockSpec((tm, R_KV), lambda i: (i, 1)), pl.BlockSpec((1, R_KV), lambda i: (0, 0)), wsp, wsp, tab, tab],
        out_specs=[wide, wide, pl.BlockSpec((tm, R_KV), lambda i: (i, 0)), tab, pl.BlockSpec((SUB, R_KV), lambda i: (0, 0))],
        out_shape=[_sds((s, MLA_W), BF), _sds((s, MLA_W), BF), _sds((s, R_KV), BF), _sds((s, LANE), BF), _sds((SUB, R_KV), F32)],
        compiler_params=_cp(("arbitrary",)),
    )(dk, dv, h, g_ckv, wukt, wuvt, cos, sin)


def in_proj_bwd_ln(dh, wint, dr1, x, g_in):
    s, hc = dh.shape
    d = x.shape[1]
    tm = _tile(s, 256)

    def body(a_ref, w_ref, d1_ref, x_ref, g_ref, gx_ref, dg_ref, db_ref, acc):
        @pl.when(pl.program_id(0) == 0)
        def _():
            dg_ref[...] = jnp.zeros_like(dg_ref)
            db_ref[...] = jnp.zeros_like(db_ref)

        acc[...] = _dot(a_ref[...], w_ref[...])
        g = g_ref[...]

        def chunk(rows):
            dy = ALPHA * d1_ref[rows, :] + acc[rows, :]
            xhat, rstd = _ln_stats(x_ref[rows, :])
            dg_ref[...] += _rows8(dy * xhat)
            db_ref[...] += _rows8(dy)
            gx_ref[rows, :] = _ln_bwd(dy, xhat, rstd, g)

        _row_chunks(tm, chunk)

    tok = pl.BlockSpec((tm, d), lambda i: (i, 0))
    accs = pl.BlockSpec((SUB, d), lambda i: (0, 0))
    return _call(
        body, name="in_proj_bwd_ln", grid=(s // tm,),
        in_specs=[pl.BlockSpec((tm, hc), lambda i: (i, 0)), pl.BlockSpec((hc, d), lambda i: (0, 0)),
                  tok, tok, pl.BlockSpec((1, d), lambda i: (0, 0))],
        out_specs=[tok, accs, accs], out_shape=[_sds((s, d), F32), _sds((SUB, d), F32), _sds((SUB, d), F32)],
        scratch_shapes=[pltpu.VMEM((tm, d), F32)], compiler_params=_cp(("arbitrary",), 56),
    )(dh, wint, dr1, x, g_in)


def _adamw_math(w, g, m, v):
    m = ADAM_B1 * m + (1.0 - ADAM_B1) * g
    v = ADAM_B2 * v + (1.0 - ADAM_B2) * (g * g)
    m_hat = m / (1.0 - ADAM_B1 ** ADAM_STEP)
    v_hat = v / (1.0 - ADAM_B2 ** ADAM_STEP)
    delta = -ADAM_LR * (m_hat / (jnp.sqrt(v_hat) + ADAM_EPS) + ADAM_WD * w)
    return delta, m, v


def adamw(name, w, g, m, v):
    r, c = w.shape
    tr = _row_tile(r, c)

    def body(w_ref, g_ref, m_ref, v_ref, d_ref, mo_ref, vo_ref):
        d_ref[...], mo_ref[...], vo_ref[...] = _adamw_math(w_ref[...], g_ref[...], m_ref[...], v_ref[...])

    blk = pl.BlockSpec((tr, c), lambda i: (i, 0))
    return _call(
        body, name=name, grid=(r // tr,), in_specs=[blk] * 4, out_specs=[blk] * 3,
        out_shape=[_sds((r, c), F32)] * 3, compiler_params=_cp(("arbitrary",)),
    )(w, g, m, v)


def _coords():
    return lax.axis_index("x"), lax.axis_index("y"), lax.axis_index("c")


def _other_chips(x, y):
    return [(1 - x, y, 2 * (1 - x) + y), (x, 1 - y, 2 * x + 1 - y), (1 - x, 1 - y, 2 * (1 - x) + 1 - y)]


ANY = pl.BlockSpec(memory_space=pl.ANY)
HBM = pl.BlockSpec(memory_space=pltpu.HBM)
SEM = pl.BlockSpec(memory_space=pltpu.SEMAPHORE)
EFFECT = pltpu.SideEffectType.DATAFLOW_SIDE_EFFECTING


def _in_hbm(a):
    return pltpu.with_memory_space_constraint(a, pltpu.HBM)


def _split_plan(mode, src, land, x, y, c):
    if mode == "pair":
        rh = src.shape[1] // 2
        return [((x, y, 1 - c), src.at[:, pl.ds((1 - c) * rh, rh)], land, land)]
    me = 2 * x + y
    plan = []
    for j, (px, py, pk) in enumerate(_other_chips(x, y)):
        if mode == "gather":
            plan.append(((px, py, c), src, land.at[me], land.at[pk]))
        elif mode == "gather_half":
            mine = pl.ds(c * (src.shape[0] // 2), src.shape[0] // 2)
            plan.append(((px, py, c), src.at[mine], land.at[me, mine], land.at[pk, mine]))
        else:
            plan.append(((px, py, c), src.at[pk], land.at[j], land.at[j]))
    return plan


def _plan_len(mode):
    return 1 if mode == "pair" else N_CHIP - 1


def split_send_start(name, mode, srcs, land_shapes, order_after):
    n = len(srcs)
    np_ = _plan_len(mode)

    def body(*refs):
        ins, lands = refs[:n], refs[n:2 * n]
        ss, rs = refs[2 * n + 1], refs[2 * n + 2]
        token = refs[-1]
        x, y, c = _coords()
        for a in range(n):
            for j, (peer, src, dst, _) in enumerate(_split_plan(mode, ins[a], lands[a], x, y, c)):
                pltpu.make_async_remote_copy(src_ref=src, dst_ref=dst, send_sem=ss.at[np_ * a + j], recv_sem=rs.at[np_ * a + j],
                                             device_id=peer, device_id_type=MESH).start()
        token[...] = jnp.zeros_like(token)

    lands = [lax.empty(shp, s.dtype) for shp, s in zip(land_shapes, srcs)]
    outs = _call(
        body, name=name,
        out_shape=(pltpu.SemaphoreType.DMA((np_ * n,)), pltpu.SemaphoreType.DMA((np_ * n,)),
                   *[pltpu.HBM(s.shape, s.dtype) for s in srcs], *[pltpu.HBM(l.shape, l.dtype) for l in lands],
                   _sds((SUB, LANE), F32)),
        in_specs=[HBM] * (2 * n) + [ANY], out_specs=(SEM, SEM, *[HBM] * (2 * n), pl.BlockSpec(memory_space=pltpu.VMEM)),
        input_output_aliases={a: 2 + a for a in range(2 * n)},
        compiler_params=pltpu.CompilerParams(has_side_effects=EFFECT),
    )(*[_in_hbm(s) for s in srcs], *[_in_hbm(l) for l in lands], order_after)
    return outs[0], outs[1], list(outs[2:2 + n]), list(outs[2 + n:2 + 2 * n]), outs[-1]


def split_send_wait(name, mode, ss, rs, srcs, lands, order_after):
    n = len(srcs)
    np_ = _plan_len(mode)

    def body(*refs):
        ins, lnd = refs[:n], refs[n:2 * n]
        s_ref, r_ref = refs[2 * n], refs[2 * n + 1]
        x, y, c = _coords()
        for a in range(n):
            for j, (peer, src, _, got) in enumerate(_split_plan(mode, ins[a], lnd[a], x, y, c)):
                cp = pltpu.make_async_remote_copy(src_ref=src, dst_ref=got, send_sem=s_ref.at[np_ * a + j], recv_sem=r_ref.at[np_ * a + j],
                                                  device_id=peer, device_id_type=MESH)
                cp.wait_send()
                cp.wait_recv()

    outs = _call(
        body, name=name, out_shape=tuple(pltpu.HBM(t.shape, t.dtype) for t in (*srcs, *lands)),
        in_specs=[HBM] * (2 * n) + [SEM, SEM, ANY], out_specs=tuple([HBM] * (2 * n)),
        input_output_aliases={a: a for a in range(2 * n)},
        compiler_params=pltpu.CompilerParams(has_side_effects=EFFECT),
    )(*srcs, *lands, ss, rs, order_after)
    return list(outs[:n]), list(outs[n:])


def swap_gathered_halves(lands):
    n = len(lands)

    def body(*refs):
        outs = refs[n:2 * n]
        ss, rs = refs[2 * n:]
        x, y, c = _coords()
        cps = []
        for a in range(n):
            rh = outs[a].shape[1] // 2
            for j, (px, py, pk) in enumerate(_other_chips(x, y)):
                held = outs[a].at[pk, pl.ds(c * rh, rh)]
                cp = pltpu.make_async_remote_copy(src_ref=held, dst_ref=held, send_sem=ss.at[a, j], recv_sem=rs.at[a, j],
                                                  device_id=(x, y, 1 - c), device_id_type=MESH)
                cp.start()
                cps.append(cp)
        for a in range(n):
            rh = outs[a].shape[1] // 2
            for j, (px, py, pk) in enumerate(_other_chips(x, y)):
                theirs = outs[a].at[pk, pl.ds((1 - c) * rh, rh)]
                pltpu.make_async_remote_copy(src_ref=theirs, dst_ref=theirs, send_sem=ss.at[a, j], recv_sem=rs.at[a, j],
                                             device_id=(x, y, 1 - c), device_id_type=MESH).wait_recv()
        for cp in cps:
            cp.wait_send()

    return _call(
        body, name="swap_gathered_halves", in_specs=[ANY] * n, out_specs=[ANY] * n,
        out_shape=[_sds(l.shape, l.dtype) for l in lands], input_output_aliases={a: a for a in range(n)},
        scratch_shapes=[pltpu.SemaphoreType.DMA((n, 3))] * 2,
    )(*lands)


def pair_exchange(grads, tag):
    n = len(grads)

    def body(*refs):
        ins, outs = refs[:n], refs[n:2 * n]
        ss, rs = refs[2 * n:]
        x, y, c = _coords()
        cps = []
        for a in range(n):
            rh = ins[a].shape[1] // 2
            cp = pltpu.make_async_remote_copy(
                src_ref=ins[a].at[:, pl.ds((1 - c) * rh, rh)], dst_ref=outs[a], send_sem=ss.at[a], recv_sem=rs.at[a],
                device_id=(x, y, 1 - c), device_id_type=MESH)
            cp.start()
            cps.append(cp)
        for cp in cps:
            cp.wait()

    return _call(
        body, name="pair_exchange_" + tag, in_specs=[ANY] * n, out_specs=[ANY] * n,
        out_shape=[_sds((N_CHIP, g.shape[1] // 2, g.shape[2]), F32) for g in grads],
        scratch_shapes=[pltpu.SemaphoreType.DMA((n,))] * 2,
    )(*grads)


def _row_tile(rows, cols, itemsize=4, budget=2 * VMEM_MB):
    fits = [t for t in range(SUB, rows + 1, SUB) if rows % t == 0 and t * cols * itemsize <= budget]
    return max(fits) if fits and rows * cols * itemsize > budget else rows


def pair_add(g, r, cidx):
    _, rows, cols = g.shape
    rh = rows // 2
    tr = _row_tile(rh, cols)
    per = rh // tr

    def body(c_ref, g_ref, r_ref, o_ref):
        o_ref[...] = g_ref[...] + r_ref[...]

    return _call(
        body, name="pair_add",
        grid_spec=pltpu.PrefetchScalarGridSpec(
            num_scalar_prefetch=1, grid=(N_CHIP, per),
            in_specs=[pl.BlockSpec((None, tr, cols), lambda k, i, c: (k, c[0] * per + i, 0)),
                      pl.BlockSpec((None, tr, cols), lambda k, i, c: (k, i, 0))],
            out_specs=pl.BlockSpec((None, tr, cols), lambda k, i, c: (k, i, 0))),
        out_shape=_sds((N_CHIP, rh, cols), F32), compiler_params=_cp(("arbitrary", "arbitrary")),
    )(cidx, g, r)


def chip_add(p, r, kc):
    _, rh, cols = p.shape
    tr = _row_tile(rh, cols)
    per = rh // tr

    def body(k_ref, p_ref, r_ref, o_ref):
        o_ref[...] = ((p_ref[...] + r_ref[0]) + r_ref[1]) + r_ref[2]

    return _call(
        body, name="chip_add",
        grid_spec=pltpu.PrefetchScalarGridSpec(
            num_scalar_prefetch=1, grid=(per,),
            in_specs=[pl.BlockSpec((None, tr, cols), lambda i, k: (k[0], i, 0)),
                      pl.BlockSpec((N_CHIP - 1, tr, cols), lambda i, k: (0, i, 0))],
            out_specs=pl.BlockSpec((tr, cols), lambda i, k: (k[1] * per + i, 0))),
        out_shape=_sds((2 * rh, cols), F32), compiler_params=_cp(("arbitrary",)),
    )(kc, p, r)


def pair_share(fulls, tag):
    n = len(fulls)

    def body(*refs):
        outs = refs[n:2 * n]
        ss, rs = refs[2 * n:]
        x, y, c = _coords()
        cps = []
        for a in range(n):
            rh = outs[a].shape[0] // 2
            mine = outs[a].at[pl.ds(c * rh, rh)]
            cp = pltpu.make_async_remote_copy(
                src_ref=mine, dst_ref=mine, send_sem=ss.at[a], recv_sem=rs.at[a],
                device_id=(x, y, 1 - c), device_id_type=MESH)
            cp.start()
            cps.append(cp)
        for a, cp in enumerate(cps):
            rh = outs[a].shape[0] // 2
            theirs = outs[a].at[pl.ds((1 - c) * rh, rh)]
            cp.wait_send()
            pltpu.make_async_remote_copy(
                src_ref=theirs, dst_ref=theirs, send_sem=ss.at[a], recv_sem=rs.at[a],
                device_id=(x, y, 1 - c), device_id_type=MESH).wait_recv()

    return _call(
        body, name="pair_share_" + tag, in_specs=[ANY] * n, out_specs=[ANY] * n,
        out_shape=[_sds(f.shape, F32) for f in fulls], input_output_aliases={a: a for a in range(n)},
        scratch_shapes=[pltpu.SemaphoreType.DMA((n,))] * 2,
    )(*fulls)


def small_allreduce_adamw(part, w, m, v):
    n = part.shape[1]

    def body(p_ref, w_ref, m_ref, v_ref, g_ref, d_ref, mo_ref, vo_ref, mine, gath, ss, rs):
        x, y, c = _coords()
        me = 4 * x + 2 * y + c
        mine[...] = jnp.sum(p_ref[...], axis=0, keepdims=True)
        gath[me] = mine[...]
        cps = []
        for k in range(1, 8):
            px, py, pc = x ^ (k >> 2), y ^ ((k >> 1) & 1), c ^ (k & 1)
            cp = pltpu.make_async_remote_copy(
                src_ref=mine, dst_ref=gath.at[me], send_sem=ss.at[k - 1], recv_sem=rs.at[k - 1],
                device_id=(px, py, pc), device_id_type=MESH)
            cp.start()
            cps.append(cp)
        for k in range(1, 8):
            src = 4 * (x ^ (k >> 2)) + 2 * (y ^ ((k >> 1) & 1)) + (c ^ (k & 1))
            pltpu.make_async_remote_copy(
                src_ref=mine, dst_ref=gath.at[src], send_sem=ss.at[k - 1], recv_sem=rs.at[k - 1],
                device_id=(x, y, c), device_id_type=MESH).wait_recv()
        for cp in cps:
            cp.wait_send()
        g = gath[0]
        for dv in range(1, 8):
            g = g + gath[dv]
        g_ref[...] = g
        d_ref[...], mo_ref[...], vo_ref[...] = _adamw_math(w_ref[...], g, m_ref[...], v_ref[...])

    vm = pl.BlockSpec(memory_space=pltpu.VMEM)
    return _call(
        body, name="small_allreduce_adamw", in_specs=[vm] * 4, out_specs=[vm] * 4, out_shape=[_sds((1, n), F32)] * 4,
        scratch_shapes=[pltpu.VMEM((1, n), F32), pltpu.VMEM((8, 1, n), F32),
                        pltpu.SemaphoreType.DMA((7,)), pltpu.SemaphoreType.DMA((7,))],
    )(part, w, m, v)


def _unshard_cols(g):
    k, r, cs = g.shape
    return g.transpose(1, 0, 2).reshape(r, k * cs)


def _shard_cols(w):
    r, c = w.shape
    return w.reshape(r, N_CHIP, c // N_CHIP).transpose(1, 0, 2)


def local_step(x, positions, ln_in_g, ln_in_b, g_cq, g_ckv, conv_b, g_conv_ln, b_conv_ln, g_ln1, b_ln1, g_ln2, b_ln2,
               target, start_token, hooks):
    s, d = x.shape
    c = d - MLA_W
    row = lambda a: a.reshape(1, -1)
    ln_in_g = row(ln_in_g) + start_token[0:1, 0:1]

    half = D_ROPE // 2
    inv_freq = ROPE_BASE ** (-jnp.arange(half, dtype=F32) * (2.0 / D_ROPE))
    invf = jnp.concatenate([inv_freq, inv_freq, jnp.zeros((LANE - D_ROPE,), F32)]).reshape(1, LANE)
    cos, sin = rope_tables(positions.astype(F32).reshape(s, 1), invf)
    x0, x0b = ln_in_fwd(x, ln_in_g, row(ln_in_b))
    win_g, wuq_g, wuk_g, wuv_g, convw_g = hooks.early_weights(x0b)

    o_kr = R_Q + R_KV
    o_cv = o_kr + D_ROPE
    n_in = o_cv + 2 * c
    per = n_in // N_CHIP

    def in_cols(a, b):
        return [win_g[k, max(a, per * k) - per * k:min(b, per * (k + 1)) - per * k]
                for k in range(N_CHIP) if max(a, per * k) < min(b, per * (k + 1))]

    win_rt = jnp.concatenate(in_cols(0, o_kr) + in_cols(o_cv, n_in) + in_cols(o_kr, o_cv)
                             + [jnp.zeros((LANE - D_ROPE, d), BF)], axis=0)
    kr_blk = (o_kr + 2 * c) // LANE
    wuq = _unshard_cols(wuq_g).reshape(R_Q, HEADS, D_QK)
    wuq_r = jnp.concatenate([wuq[:, :, :D_NOPE].reshape(R_Q, MLA_W),
                             jnp.pad(wuq[:, :, D_NOPE:], ((0, 0), (0, 0), (0, LANE - D_ROPE))).reshape(R_Q, MLA_W)], axis=1)
    wuk = _unshard_cols(wuk_g)
    wuv = _unshard_cols(wuv_g)
    conv_w = jnp.pad(_unshard_cols(convw_g), ((0, 1), (0, 0)))

    h = matmul_nt("in_proj", x0b, win_rt, 256)
    qc, cqn = q_proj(h, g_cq, wuq_r, cos, sin)
    kc, kct, v, ckvn = kv_proj(h, g_ckv, wuk, wuv, cos, sin, kr_blk)
    o, ob, lse = attn_fwd(qc, kc, v)
    co, uc = conv_fwd(h, conv_w, conv_b, g_conv_ln, b_conv_ln)
    wout_g, wff1_g, wff2_g = hooks.late_weights(ob)
    wout = wout_g.reshape(d, d)
    wff2 = wff2_g.reshape(-1, d)
    r1, x1, x1b = out_proj_ln1(ob, co, wout, x0, g_ln1, b_ln1)
    rb, a1b = ff1_fwd(x1b, wff1_g)
    dr2, dr2b, loss8, dg2, db2 = ff2_ln2_loss(a1b, wff2, x1, target, g_ln2, b_ln2)

    df1b = ff2_bwd_act(dr2b, wff2, rb)
    gw_ff2 = wgrad("wgrad_ff2", a1b, dr2b, 1024, 2048).reshape(N_CHIP, -1, d)
    gw_ff1 = wgrad("wgrad_ff1", x1b, df1b, 1024, 2048, shards=N_CHIP)
    tok = hooks.ff_grads(gw_ff2, gw_ff1)
    dr1, dr1b, dg1, db1 = ff1_bwd_ln1(df1b, wff1_g, dr2, r1, g_ln1 + tok[0:1, 0:1])
    tok = hooks.ff_grads_mid(dr1b)
    gw_out = jnp.concatenate([wgrad("wgrad_out_attn", ob, dr1b, 1024, 1024)[0],
                              wgrad("wgrad_out_conv", co, dr1b, 1024, 1024)[0]], axis=0).reshape(N_CHIP, -1, d)
    dob, dot, dco, delta = out_proj_bwd(dr1b, wout.T, o)
    duc, dgc, dbc, dcb = conv_bwd_ln(uc, dco, g_conv_ln + tok[0:1, 0:1], b_conv_ln)
    dconv, gconvw = conv_bwd_taps(h, duc, conv_w)
    dqt, dk, dv = attn_bwd(qc, kc, kct, v, dob, dot, lse, delta)
    dqb, dcq, dgq = q_bwd(dqt, h, g_cq, wuq_r.T, cos, sin)
    dknb, dvb, dckv, dkr, dgkv = kv_bwd(dk, dv, h, g_ckv, wuk.T, wuv.T, cos, sin)
    gwuq_r = wgrad("wgrad_uq", cqn, dqb, 512, 1024)[0]
    gw_uk = wgrad("wgrad_uk", ckvn, dknb, 512, 1024, shards=N_CHIP)
    gw_uv = wgrad("wgrad_uv", ckvn, dvb, 512, 1024, shards=N_CHIP)
    dh = jnp.concatenate([dcq, dckv, dconv, dkr], axis=1)
    gwin_rt = wgrad("wgrad_in", dh, x0b, 640, 1024)[0]

    gwin_t = jnp.concatenate([gwin_rt[:o_kr], gwin_rt[o_kr + 2 * c:o_kr + 2 * c + D_ROPE], gwin_rt[o_kr:o_kr + 2 * c]], axis=0)
    gwin_t = jnp.pad(gwin_t.reshape(N_CHIP, per, d), ((0, 0), (0, win_g.shape[1] - per), (0, 0)))
    gwuq = jnp.concatenate([gwuq_r[:, :MLA_W].reshape(R_Q, HEADS, D_NOPE),
                            gwuq_r[:, MLA_W:].reshape(R_Q, HEADS, LANE)[:, :, :D_ROPE]], axis=2).reshape(R_Q, HEADS * D_QK)
    tok = hooks.rest_grads(dict(w_in=gwin_t, w_uq=_shard_cols(gwuq), w_uk=gw_uk, w_uv=gw_uv,
                                conv_w=_shard_cols(gconvw), w_out=gw_out))
    gx, dgin, dbin = in_proj_bwd_ln(dh, win_rt, dr1, x, ln_in_g + tok[0:1, 0:1])
    small = jnp.concatenate([dgin, dbin, dgq, dgkv, dcb, dgc, dbc, dg1, db1, dg2, db2, loss8], axis=1)
    return gx, small


BIG = ["w_in", "w_uq", "w_uk", "w_uv", "conv_w", "w_out", "w_ff1", "w_ff2"]
EARLY = ["w_in", "w_uq", "w_uk", "w_uv", "conv_w"]
LATE = ["w_out", "w_ff1", "w_ff2"]
SMALL = ["ln_in_g", "ln_in_b", "g_cq", "g_ckv", "conv_b", "g_conv_ln", "b_conv_ln", "g_ln1", "b_ln1", "g_ln2", "b_ln2"]
WEIGHTS = ["ln_in_g", "ln_in_b", "w_in", "g_cq", "w_uq", "g_ckv", "w_uk", "w_uv", "conv_w", "conv_b", "g_conv_ln",
           "b_conv_ln", "w_out", "g_ln1", "b_ln1", "w_ff1", "w_ff2", "g_ln2", "b_ln2"]


def _pad_rows(a, rows):
    return jnp.pad(a, ((0, rows - a.shape[0]), (0, 0)))


def kernel(x, positions, ln_in_g, ln_in_b, w_in, g_cq, w_uq, g_ckv, w_uk, w_uv, conv_w, conv_b, g_conv_ln, b_conv_ln, w_out, g_ln1, b_ln1, w_ff1, w_ff2, g_ln2, b_ln2, loss_target, m_ln_in_g, m_ln_in_b, m_w_in, m_g_cq, m_w_uq, m_g_ckv, m_w_uk, m_w_uv, m_conv_w, m_conv_b, m_g_conv_ln, m_b_conv_ln, m_w_out, m_g_ln1, m_b_ln1, m_w_ff1, m_w_ff2, m_g_ln2, m_b_ln2, v_ln_in_g, v_ln_in_b, v_w_in, v_g_cq, v_w_uq, v_g_ckv, v_w_uk, v_w_uv, v_conv_w, v_conv_b, v_g_conv_ln, v_b_conv_ln, v_w_out, v_g_ln1, v_b_ln1, v_w_ff1, v_w_ff2, v_g_ln2, v_b_ln2):
    w = dict(ln_in_g=ln_in_g, ln_in_b=ln_in_b, w_in=w_in, g_cq=g_cq, w_uq=w_uq, g_ckv=g_ckv, w_uk=w_uk, w_uv=w_uv,
             conv_w=conv_w, conv_b=conv_b, g_conv_ln=g_conv_ln, b_conv_ln=b_conv_ln, w_out=w_out, g_ln1=g_ln1,
             b_ln1=b_ln1, w_ff1=w_ff1, w_ff2=w_ff2, g_ln2=g_ln2, b_ln2=b_ln2)
    m = dict(ln_in_g=m_ln_in_g, ln_in_b=m_ln_in_b, w_in=m_w_in, g_cq=m_g_cq, w_uq=m_w_uq, g_ckv=m_g_ckv, w_uk=m_w_uk,
             w_uv=m_w_uv, conv_w=m_conv_w, conv_b=m_conv_b, g_conv_ln=m_g_conv_ln, b_conv_ln=m_b_conv_ln, w_out=m_w_out,
             g_ln1=m_g_ln1, b_ln1=m_b_ln1, w_ff1=m_w_ff1, w_ff2=m_w_ff2, g_ln2=m_g_ln2, b_ln2=m_b_ln2)
    v = dict(ln_in_g=v_ln_in_g, ln_in_b=v_ln_in_b, w_in=v_w_in, g_cq=v_g_cq, w_uq=v_w_uq, g_ckv=v_g_ckv, w_uk=v_w_uk,
             w_uv=v_w_uv, conv_w=v_conv_w, conv_b=v_conv_b, g_conv_ln=v_g_conv_ln, b_conv_ln=v_b_conv_ln, w_out=v_w_out,
             g_ln1=v_g_ln1, b_ln1=v_b_ln1, w_ff1=v_w_ff1, w_ff2=v_w_ff2, g_ln2=v_g_ln2, b_ln2=v_b_ln2)

    as2d = lambda t, n: t[n][0].T if n == "w_in" else t[n][0]
    sh2 = {n: as2d(w, n) for n in BIG}
    cidx = lax.axis_index("c").astype(jnp.int32).reshape(1)
    me = 2 * lax.axis_index("x") + lax.axis_index("y")
    kc = jnp.stack([me, lax.axis_index("c")]).astype(jnp.int32)

    pad_to = {"conv_w": CONV_K + 1, "w_in": -(-sh2["w_in"].shape[0] // (4 * SUB)) * (4 * SUB)}
    early = [_pad_rows(sh2[n] if n == "conv_w" else sh2[n].astype(BF), pad_to.get(n, sh2[n].shape[0])) for n in EARLY]
    eg = split_send_start("early_weights_start", "gather_half", early, [(N_CHIP,) + a.shape for a in early], ln_in_g)
    late = [sh2[n].astype(BF) for n in LATE]
    ag = split_send_start("late_weights_start", "gather", late, [(N_CHIP,) + a.shape for a in late], eg[4])
    rest = [n for n in BIG if n not in ("w_ff2", "w_ff1")]
    flight = {}

    class Hooks:
        @staticmethod
        def early_weights(after):
            mine, lands = split_send_wait("early_weights_wait", "gather_half", *eg[:4], after)
            full = [lax.dynamic_update_slice(g, a[None], (me, 0, 0)) for g, a in zip(swap_gathered_halves(lands), mine)]
            return [g[:, :CONV_K] if n == "conv_w" else g for n, g in zip(EARLY, full)]

        @staticmethod
        def late_weights(after):
            mine, lands = split_send_wait("late_weights_wait", "gather", *ag[:4], after)
            return [lax.dynamic_update_slice(g, a[None], (me, 0, 0)) for g, a in zip(lands, mine)]

        @staticmethod
        def ff_grads(gw_ff2, gw_ff1):
            full = [gw_ff2, gw_ff1]
            st = split_send_start("ff_pair_start", "pair", full, [(N_CHIP, g.shape[1] // 2, g.shape[2]) for g in full], ag[4])
            flight["ff_pair"] = st[:4]
            flight["token"] = st[4]
            return st[4]

        @staticmethod
        def ff_grads_mid(after):
            full, recv = split_send_wait("ff_pair_wait", "pair", *flight["ff_pair"], after)
            psum = [pair_add(g, r, cidx) for g, r in zip(full, recv)]
            st = split_send_start("ff_grads_start", "scatter", psum, [(N_CHIP - 1,) + p.shape[1:] for p in psum], flight["token"])
            flight["ff"] = st[:4]
            flight["token"] = st[4]
            return st[4]

        @staticmethod
        def rest_grads(big):
            full = [big[n] for n in rest]
            psum = [pair_add(g, r, cidx) for g, r in zip(full, pair_exchange(full, "rest"))]
            st = split_send_start("rest_grads_start", "scatter", psum, [(N_CHIP - 1,) + p.shape[1:] for p in psum], flight["token"])
            flight["rest"] = st[:4]
            return st[4]

    gx, small = local_step(x[0], positions[0], ln_in_g, ln_in_b, g_cq, g_ckv, conv_b, g_conv_ln, b_conv_ln, g_ln1, b_ln1,
                           g_ln2, b_ln2, loss_target[0], ag[4], Hooks)

    ff_psum, ff_got = split_send_wait("ff_grads_wait", "scatter", *flight["ff"], gx)
    rest_psum, rest_got = split_send_wait("rest_grads_wait", "scatter", *flight["rest"], gx)
    summed = [chip_add(p, r, kc) for p, r in zip(rest_psum + ff_psum, rest_got + ff_got)]
    gsh = dict(zip(rest + ["w_ff2", "w_ff1"], pair_share(summed, "all")))
    for n in pad_to:
        gsh[n] = gsh[n][:sh2[n].shape[0]]

    grad, delta, new_m, new_v = {}, {}, {}, {}
    for n in BIG:
        back = (lambda a: a.T[None]) if n == "w_in" else (lambda a: a[None])
        d_, m_, v_ = adamw("adamw_" + n, sh2[n], gsh[n], as2d(m, n), as2d(v, n))
        grad[n], delta[n], new_m[n], new_v[n] = back(gsh[n]), back(d_), back(m_), back(v_)

    flat = lambda t: jnp.concatenate([t[n].reshape(1, -1) for n in SMALL] + [jnp.zeros((1, LANE), F32)], axis=1)
    g_s, d_s, m_s, v_s = small_allreduce_adamw(small, flat(w), flat(m), flat(v))
    off = 0
    for n in SMALL:
        sz = w[n].size
        for dst, src in ((grad, g_s), (delta, d_s), (new_m, m_s), (new_v, v_s)):
            dst[n] = src[0, off:off + sz].reshape(w[n].shape)
        off += sz
    loss = jnp.sum(g_s[0, off:off + LANE])

    return (loss, gx[None], *[grad[n] for n in WEIGHTS], *[delta[n] for n in WEIGHTS],
            *[new_m[n] for n in WEIGHTS], *[new_v[n] for n in WEIGHTS])
```
